```python
import jax, jax.numpy as jnp
from jax import lax
import numpy as np

D_MODEL = 2048
BATCH = 8
SEQ = 8192
DEPTH = 2

HEAD_DIM = 64
A_HEADS = 8
A_WIDTH = A_HEADS * HEAD_DIM
DILATED_PAIRS = ((128, 1), (512, 4), (2048, 16))
B_WIDTH = 512
CONV_WIDTH = 31
C_Q_HEADS = 16
C_KV_HEADS = 2
C_WIDTH = C_Q_HEADS * HEAD_DIM
C_WINDOW = 128
MIX_WIDTH = A_WIDTH + B_WIDTH + C_WIDTH
IN_WIDTHS = (A_WIDTH, A_WIDTH, A_WIDTH,
             B_WIDTH, B_WIDTH,
             C_WIDTH, C_KV_HEADS * HEAD_DIM, C_KV_HEADS * HEAD_DIM)
IN_WIDTH = sum(IN_WIDTHS)
D_FF = -(-(8 * D_MODEL) // (3 * 256)) * 256
BLOCK = 128
EPS = 1e-6

kernel_name = "hybrid_dilated_conformer_swa_sink_block"


def _rms_norm(x, g):
    xf = x.astype(jnp.float32)
    y = xf * lax.rsqrt(jnp.mean(xf * xf, axis=-1, keepdims=True) + EPS)
    return (y * g.astype(jnp.float32)).astype(x.dtype)


def _layer_norm(x, g, b):
    xf = x.astype(jnp.float32)
    mu = jnp.mean(xf, axis=-1, keepdims=True)
    var = jnp.mean(jnp.square(xf - mu), axis=-1, keepdims=True)
    y = (xf - mu) * lax.rsqrt(var + EPS)
    return (y * g.astype(jnp.float32) + b.astype(jnp.float32)).astype(x.dtype)


def _band_attention(q, k, v, max_dist, sinks=None):
    N, L, Hq, hd = q.shape
    Hk = k.shape[2]
    G = Hq // Hk
    blk = min(BLOCK, L)
    nb = -(-L // blk)
    Lp = nb * blk
    C = blk + max_dist
    qb = jnp.pad(q, ((0, 0), (0, Lp - L), (0, 0), (0, 0))).reshape(N, nb, blk, Hk, G, hd)
    kp = jnp.pad(k, ((0, 0), (max_dist, Lp - L), (0, 0), (0, 0)))
    vp = jnp.pad(v, ((0, 0), (max_dist, Lp - L), (0, 0), (0, 0)))
    idx = jnp.arange(nb)[:, None] * blk + jnp.arange(C)[None, :]
    kb = kp[:, idx]
    vb = vp[:, idx]
    s = jnp.einsum('nbqhgd,nbkhd->nbhgqk', qb, kb,
                   preferred_element_type=jnp.float32) * (hd ** -0.5)
    qpos = jnp.arange(nb)[:, None] * blk + jnp.arange(blk)[None, :]
    kpos = idx - max_dist
    dist = qpos[:, :, None] - kpos[:, None, :]
    valid = (dist >= 0) & (dist <= max_dist) & (kpos[:, None, :] >= 0)
    s = jnp.where(valid[None, :, None, None], s, -jnp.inf)
    m = jnp.max(s, axis=-1)
    if sinks is not None:
        sk = sinks.astype(jnp.float32).reshape(Hk, G)[None, None, :, :, None]
        m = jnp.maximum(m, sk)
    p = jnp.exp(s - m[..., None])
    denom = jnp.sum(p, axis=-1)
    if sinks is not None:
        denom = denom + jnp.exp(sk - m)
    o = jnp.einsum('nbhgqk,nbkhd->nbqhgd', p, vb.astype(jnp.float32))
    denom_t = jnp.transpose(denom, (0, 1, 4, 2, 3))
    o = o / denom_t[..., None]
    lse = jnp.transpose(m, (0, 1, 4, 2, 3)) + jnp.log(denom_t)
    o = o.reshape(N, Lp, Hq, hd)[:, :L]
    lse = lse.reshape(N, Lp, Hq)[:, :L]
    return o, lse


def _dilated_mixture(q, k, v):
    B, S, H, hd = q.shape
    outs, lses = [], []
    for (w, d) in DILATED_PAIRS:
        def to_res(t):
            return t.reshape(B, S // d, d, H, hd).transpose(0, 2, 1, 3, 4).reshape(B * d, S // d, H, hd)
        o, lse = _band_attention(to_res(q), to_res(k), to_res(v), w // d)
        outs.append(o.reshape(B, d, S // d, H, hd).transpose(0, 2, 1, 3, 4).reshape(B, S, H, hd))
        lses.append(lse.reshape(B, d, S // d, H).transpose(0, 2, 1, 3).reshape(B, S, H))
    wts = jax.nn.softmax(jnp.stack(lses, axis=0), axis=0)
    return jnp.einsum('cbsh,cbshd->bshd', wts, jnp.stack(outs, axis=0))


def _conformer_conv(u, gate, conv_w, conv_b, ln_g, ln_b):
    h = u * jax.nn.sigmoid(gate)
    C = h.shape[-1]
    y = lax.conv_general_dilated(h, conv_w[:, None, :].astype(h.dtype), window_strides=(1,),
                                 padding=[(CONV_WIDTH - 1, 0)],
                                 dimension_numbers=('NWC', 'WIO', 'NWC'),
                                 feature_group_count=C)
    y = y + conv_b
    return jax.nn.silu(_layer_norm(y, ln_g, ln_b))


def _fwd_setup_inputs(seed: int = 0) -> dict:
    key = jax.random.key(seed)
    ks = jax.random.split(key, 17)
    f32 = jnp.float32
    nrm = lambda k, shp: jax.random.normal(k, shp, dtype=f32)
    return {
        "x": nrm(ks[0], (BATCH, SEQ, D_MODEL)),
        "norm1_g": 1.0 + 0.02 * nrm(ks[1], (DEPTH, D_MODEL)),
        "w_in": nrm(ks[2], (DEPTH, D_MODEL, IN_WIDTH)) * D_MODEL ** -0.5,
        "a_q_g": 1.0 + 0.02 * nrm(ks[3], (DEPTH, HEAD_DIM)),
        "a_k_g": 1.0 + 0.02 * nrm(ks[4], (DEPTH, HEAD_DIM)),
        "conv_w": nrm(ks[5], (DEPTH, CONV_WIDTH, B_WIDTH)) * CONV_WIDTH ** -0.5,
        "conv_b": 0.02 * nrm(ks[6], (DEPTH, B_WIDTH)),
        "conv_ln_g": 1.0 + 0.02 * nrm(ks[7], (DEPTH, B_WIDTH)),
        "conv_ln_b": 0.02 * nrm(ks[8], (DEPTH, B_WIDTH)),
        "c_q_g": 1.0 + 0.02 * nrm(ks[9], (DEPTH, HEAD_DIM)),
        "c_k_g": 1.0 + 0.02 * nrm(ks[10], (DEPTH, HEAD_DIM)),
        "c_sinks": 0.5 * nrm(ks[11], (DEPTH, C_Q_HEADS)),
        "w_out": nrm(ks[12], (DEPTH, MIX_WIDTH, D_MODEL)) * MIX_WIDTH ** -0.5,
        "norm2_g": 1.0 + 0.02 * nrm(ks[13], (DEPTH, D_MODEL)),
        "w_gate": nrm(ks[14], (DEPTH, D_MODEL, D_FF)) * D_MODEL ** -0.5,
        "w_up": nrm(ks[15], (DEPTH, D_MODEL, D_FF)) * D_MODEL ** -0.5,
        "w_down": nrm(ks[16], (DEPTH, D_FF, D_MODEL)) * D_FF ** -0.5,
    }


def _fwd_reference(x, norm1_g, w_in, a_q_g, a_k_g, conv_w, conv_b, conv_ln_g, conv_ln_b,
              c_q_g, c_k_g, c_sinks, w_out, norm2_g, w_gate, w_up, w_down):
    B, S, _ = x.shape
    splits = [int(s) for s in np.cumsum(IN_WIDTHS)[:-1]]
    for l in range(DEPTH):
        h = _rms_norm(x, norm1_g[l])
        proj = h @ w_in[l]
        aq, ak, av, bu, bg, cq, ck, cv = jnp.split(proj, splits, axis=-1)
        aq = _rms_norm(aq.reshape(B, S, A_HEADS, HEAD_DIM), a_q_g[l])
        ak = _rms_norm(ak.reshape(B, S, A_HEADS, HEAD_DIM), a_k_g[l])
        av = av.reshape(B, S, A_HEADS, HEAD_DIM)
        out_a = _dilated_mixture(aq, ak, av).reshape(B, S, A_WIDTH)
        out_b = _conformer_conv(bu, bg, conv_w[l], conv_b[l], conv_ln_g[l], conv_ln_b[l])
        cq = _rms_norm(cq.reshape(B, S, C_Q_HEADS, HEAD_DIM), c_q_g[l])
        ck = _rms_norm(ck.reshape(B, S, C_KV_HEADS, HEAD_DIM), c_k_g[l])
        cv = cv.reshape(B, S, C_KV_HEADS, HEAD_DIM)
        out_c, _ = _band_attention(cq, ck, cv, C_WINDOW - 1, sinks=c_sinks[l])
        out_c = out_c.reshape(B, S, C_WIDTH)
        mix = jnp.concatenate([out_a.astype(x.dtype), out_b.astype(x.dtype),
                               out_c.astype(x.dtype)], axis=-1)
        x = x + mix @ w_out[l]
        h2 = _rms_norm(x, norm2_g[l])
        x = x + (jax.nn.silu(h2 @ w_gate[l]) * (h2 @ w_up[l])) @ w_down[l]
    return x


import jax as _jax
import jax.numpy as _jnp

TWIN_FORMAT = 'train_step'
FWD_PARAMS = ['x', 'norm1_g', 'w_in', 'a_q_g', 'a_k_g', 'conv_w', 'conv_b', 'conv_ln_g', 'conv_ln_b', 'c_q_g', 'c_k_g', 'c_sinks', 'w_out', 'norm2_g', 'w_gate', 'w_up', 'w_down']
TWIN_WEIGHTS = ['norm1_g', 'w_in', 'a_q_g', 'a_k_g', 'conv_w', 'conv_b', 'conv_ln_g', 'conv_ln_b', 'c_q_g', 'c_k_g', 'c_sinks', 'w_out', 'norm2_g', 'w_gate', 'w_up', 'w_down']
TWIN_DIFF_INPUT = 'x'
TWIN_INPUTS = ['x', 'norm1_g', 'w_in', 'a_q_g', 'a_k_g', 'conv_w', 'conv_b', 'conv_ln_g', 'conv_ln_b', 'c_q_g', 'c_k_g', 'c_sinks', 'w_out', 'norm2_g', 'w_gate', 'w_up', 'w_down', 'loss_target', 'm_norm1_g', 'm_w_in', 'm_a_q_g', 'm_a_k_g', 'm_conv_w', 'm_conv_b', 'm_conv_ln_g', 'm_conv_ln_b', 'm_c_q_g', 'm_c_k_g', 'm_c_sinks', 'm_w_out', 'm_norm2_g', 'm_w_gate', 'm_w_up', 'm_w_down', 'v_norm1_g', 'v_w_in', 'v_a_q_g', 'v_a_k_g', 'v_conv_w', 'v_conv_b', 'v_conv_ln_g', 'v_conv_ln_b', 'v_c_q_g', 'v_c_k_g', 'v_c_sinks', 'v_w_out', 'v_norm2_g', 'v_w_gate', 'v_w_up', 'v_w_down']
TWIN_OUTPUTS = ['loss', 'grad_x', 'grad_norm1_g', 'grad_w_in', 'grad_a_q_g', 'grad_a_k_g', 'grad_conv_w', 'grad_conv_b', 'grad_conv_ln_g', 'grad_conv_ln_b', 'grad_c_q_g', 'grad_c_k_g', 'grad_c_sinks', 'grad_w_out', 'grad_norm2_g', 'grad_w_gate', 'grad_w_up', 'grad_w_down', 'delta_norm1_g', 'delta_w_in', 'delta_a_q_g', 'delta_a_k_g', 'delta_conv_w', 'delta_conv_b', 'delta_conv_ln_g', 'delta_conv_ln_b', 'delta_c_q_g', 'delta_c_k_g', 'delta_c_sinks', 'delta_w_out', 'delta_norm2_g', 'delta_w_gate', 'delta_w_up', 'delta_w_down', 'new_m_norm1_g', 'new_m_w_in', 'new_m_a_q_g', 'new_m_a_k_g', 'new_m_conv_w', 'new_m_conv_b', 'new_m_conv_ln_g', 'new_m_conv_ln_b', 'new_m_c_q_g', 'new_m_c_k_g', 'new_m_c_sinks', 'new_m_w_out', 'new_m_norm2_g', 'new_m_w_gate', 'new_m_w_up', 'new_m_w_down', 'new_v_norm1_g', 'new_v_w_in', 'new_v_a_q_g', 'new_v_a_k_g', 'new_v_conv_w', 'new_v_conv_b', 'new_v_conv_ln_g', 'new_v_conv_ln_b', 'new_v_c_q_g', 'new_v_c_k_g', 'new_v_c_sinks', 'new_v_w_out', 'new_v_norm2_g', 'new_v_w_gate', 'new_v_w_up', 'new_v_w_down']
TWIN_LEAF_KINDS = {'loss': 'loss', 'grad_x': 'grad_x', 'grad_norm1_g': 'grad_w', 'grad_w_in': 'grad_w', 'grad_a_q_g': 'grad_w', 'grad_a_k_g': 'grad_w', 'grad_conv_w': 'grad_w', 'grad_conv_b': 'grad_w', 'grad_conv_ln_g': 'grad_w', 'grad_conv_ln_b': 'grad_w', 'grad_c_q_g': 'grad_w', 'grad_c_k_g': 'grad_w', 'grad_c_sinks': 'grad_w', 'grad_w_out': 'grad_w', 'grad_norm2_g': 'grad_w', 'grad_w_gate': 'grad_w', 'grad_w_up': 'grad_w', 'grad_w_down': 'grad_w', 'delta_norm1_g': 'delta_w', 'delta_w_in': 'delta_w', 'delta_a_q_g': 'delta_w', 'delta_a_k_g': 'delta_w', 'delta_conv_w': 'delta_w', 'delta_conv_b': 'delta_w', 'delta_conv_ln_g': 'delta_w', 'delta_conv_ln_b': 'delta_w', 'delta_c_q_g': 'delta_w', 'delta_c_k_g': 'delta_w', 'delta_c_sinks': 'delta_w', 'delta_w_out': 'delta_w', 'delta_norm2_g': 'delta_w', 'delta_w_gate': 'delta_w', 'delta_w_up': 'delta_w', 'delta_w_down': 'delta_w', 'new_m_norm1_g': 'new_m', 'new_m_w_in': 'new_m', 'new_m_a_q_g': 'new_m', 'new_m_a_k_g': 'new_m', 'new_m_conv_w': 'new_m', 'new_m_conv_b': 'new_m', 'new_m_conv_ln_g': 'new_m', 'new_m_conv_ln_b': 'new_m', 'new_m_c_q_g': 'new_m', 'new_m_c_k_g': 'new_m', 'new_m_c_sinks': 'new_m', 'new_m_w_out': 'new_m', 'new_m_norm2_g': 'new_m', 'new_m_w_gate': 'new_m', 'new_m_w_up': 'new_m', 'new_m_w_down': 'new_m', 'new_v_norm1_g': 'new_v', 'new_v_w_in': 'new_v', 'new_v_a_q_g': 'new_v', 'new_v_a_k_g': 'new_v', 'new_v_conv_w': 'new_v', 'new_v_conv_b': 'new_v', 'new_v_conv_ln_g': 'new_v', 'new_v_conv_ln_b': 'new_v', 'new_v_c_q_g': 'new_v', 'new_v_c_k_g': 'new_v', 'new_v_c_sinks': 'new_v', 'new_v_w_out': 'new_v', 'new_v_norm2_g': 'new_v', 'new_v_w_gate': 'new_v', 'new_v_w_up': 'new_v', 'new_v_w_down': 'new_v'}


def _forward(args):
    return _fwd_reference(*[args[k] for k in FWD_PARAMS])


def _output_shape():
    def fwd():
        inp = _fwd_setup_inputs(0)
        return _fwd_reference(*[inp[k] for k in FWD_PARAMS])
    out = _jax.eval_shape(fwd)
    return out.shape, out.dtype

N_MICROBATCH = 1
ADAM_LR = 0.001
ADAM_B1 = 0.9
ADAM_B2 = 0.999
ADAM_EPS = 1e-08
ADAM_WD = 0.01
ADAM_STEP = 10
PER_EXAMPLE_BATCH_AXIS = {'x': 0, 'loss_target': 0}
SHARED_INPUTS = []
_WEIGHT_DTYPES = {'norm1_g': _jnp.float32, 'w_in': _jnp.float32, 'a_q_g': _jnp.float32, 'a_k_g': _jnp.float32, 'conv_w': _jnp.float32, 'conv_b': _jnp.float32, 'conv_ln_g': _jnp.float32, 'conv_ln_b': _jnp.float32, 'c_q_g': _jnp.float32, 'c_k_g': _jnp.float32, 'c_sinks': _jnp.float32, 'w_out': _jnp.float32, 'norm2_g': _jnp.float32, 'w_gate': _jnp.float32, 'w_up': _jnp.float32, 'w_down': _jnp.float32}
MOMENT_SCALE = {'norm1_g': 4.492938e-01, 'w_in': 2.401214e-01, 'a_q_g': 1.651629e+00, 'a_k_g': 1.655137e+00, 'conv_w': 4.975254e-01, 'conv_b': 8.436542e+00, 'conv_ln_g': 1.509084e+01, 'conv_ln_b': 1.109160e+01, 'c_q_g': 5.758760e+00, 'c_k_g': 5.772375e+00, 'c_sinks': 8.847331e-01, 'w_out': 9.282216e-01, 'norm2_g': 2.469612e+01, 'w_gate': 2.989557e-01, 'w_up': 1.584492e-01, 'w_down': 2.448062e-01}


def _to_microbatches(a, axis):
    t = _jnp.moveaxis(a, axis, 0)
    t = t.reshape((N_MICROBATCH, t.shape[0] // N_MICROBATCH) + t.shape[1:])
    return _jnp.moveaxis(t, 1, axis + 1)


def setup_inputs(seed: int = 0) -> dict:
    inp = _fwd_setup_inputs(seed)
    key = _jax.random.fold_in(_jax.random.key(seed), 7919)
    shape, _ = _output_shape()
    out = dict(inp)
    out["loss_target"] = _jax.random.normal(_jax.random.fold_in(key, 0), shape, _jnp.float32)
    for i, name in enumerate(TWIN_WEIGHTS):
        w = inp[name].astype(_jnp.float32)
        if MOMENT_SCALE is None:
            s = _jnp.sqrt(_jnp.mean(_jnp.square(w)) + 1e-30)
        else:
            s = MOMENT_SCALE[name]
        km, kv = _jax.random.split(_jax.random.fold_in(key, i + 1))
        out[name] = w
        out["m_" + name] = s * _jax.random.normal(km, w.shape, _jnp.float32)
        out["v_" + name] = (s * s) * _jax.random.uniform(kv, w.shape, _jnp.float32, 0.5, 1.5)
    if N_MICROBATCH > 1:
        for name, axis in PER_EXAMPLE_BATCH_AXIS.items():
            out[name] = _to_microbatches(out[name], axis)
    return {'x': out['x'], 'norm1_g': out['norm1_g'], 'w_in': out['w_in'], 'a_q_g': out['a_q_g'], 'a_k_g': out['a_k_g'], 'conv_w': out['conv_w'], 'conv_b': out['conv_b'], 'conv_ln_g': out['conv_ln_g'], 'conv_ln_b': out['conv_ln_b'], 'c_q_g': out['c_q_g'], 'c_k_g': out['c_k_g'], 'c_sinks': out['c_sinks'], 'w_out': out['w_out'], 'norm2_g': out['norm2_g'], 'w_gate': out['w_gate'], 'w_up': out['w_up'], 'w_down': out['w_down'], 'loss_target': out['loss_target'], 'm_norm1_g': out['m_norm1_g'], 'm_w_in': out['m_w_in'], 'm_a_q_g': out['m_a_q_g'], 'm_a_k_g': out['m_a_k_g'], 'm_conv_w': out['m_conv_w'], 'm_conv_b': out['m_conv_b'], 'm_conv_ln_g': out['m_conv_ln_g'], 'm_conv_ln_b': out['m_conv_ln_b'], 'm_c_q_g': out['m_c_q_g'], 'm_c_k_g': out['m_c_k_g'], 'm_c_sinks': out['m_c_sinks'], 'm_w_out': out['m_w_out'], 'm_norm2_g': out['m_norm2_g'], 'm_w_gate': out['m_w_gate'], 'm_w_up': out['m_w_up'], 'm_w_down': out['m_w_down'], 'v_norm1_g': out['v_norm1_g'], 'v_w_in': out['v_w_in'], 'v_a_q_g': out['v_a_q_g'], 'v_a_k_g': out['v_a_k_g'], 'v_conv_w': out['v_conv_w'], 'v_conv_b': out['v_conv_b'], 'v_conv_ln_g': out['v_conv_ln_g'], 'v_conv_ln_b': out['v_conv_ln_b'], 'v_c_q_g': out['v_c_q_g'], 'v_c_k_g': out['v_c_k_g'], 'v_c_sinks': out['v_c_sinks'], 'v_w_out': out['v_w_out'], 'v_norm2_g': out['v_norm2_g'], 'v_w_gate': out['v_w_gate'], 'v_w_up': out['v_w_up'], 'v_w_down': out['v_w_down']}


def _loss(weights, diff, rest, loss_target):
    with _jax.named_scope("forward"):
        args = {**rest, TWIN_DIFF_INPUT: diff, **{k: w.astype(_WEIGHT_DTYPES[k]) for k, w in weights.items()}}
        y = _forward(args)
    with _jax.named_scope("loss_head"):
        err = _jnp.square(y.astype(_jnp.float32) - loss_target)
        return 0.5 * _jnp.sum(_jnp.mean(err, axis=-1)) if err.ndim else 0.5 * err


def _adamw(w, g, m, v):
    m = ADAM_B1 * m + (1.0 - ADAM_B1) * g
    v = ADAM_B2 * v + (1.0 - ADAM_B2) * _jnp.square(g)
    m_hat = m / (1.0 - ADAM_B1 ** ADAM_STEP)
    v_hat = v / (1.0 - ADAM_B2 ** ADAM_STEP)
    delta = -ADAM_LR * (m_hat / (_jnp.sqrt(v_hat) + ADAM_EPS) + ADAM_WD * w)
    return delta, m, v


def reference(x, norm1_g, w_in, a_q_g, a_k_g, conv_w, conv_b, conv_ln_g, conv_ln_b, c_q_g, c_k_g, c_sinks, w_out, norm2_g, w_gate, w_up, w_down, loss_target, m_norm1_g, m_w_in, m_a_q_g, m_a_k_g, m_conv_w, m_conv_b, m_conv_ln_g, m_conv_ln_b, m_c_q_g, m_c_k_g, m_c_sinks, m_w_out, m_norm2_g, m_w_gate, m_w_up, m_w_down, v_norm1_g, v_w_in, v_a_q_g, v_a_k_g, v_conv_w, v_conv_b, v_conv_ln_g, v_conv_ln_b, v_c_q_g, v_c_k_g, v_c_sinks, v_w_out, v_norm2_g, v_w_gate, v_w_up, v_w_down):
    given = dict(x=x, norm1_g=norm1_g, w_in=w_in, a_q_g=a_q_g, a_k_g=a_k_g, conv_w=conv_w, conv_b=conv_b, conv_ln_g=conv_ln_g, conv_ln_b=conv_ln_b, c_q_g=c_q_g, c_k_g=c_k_g, c_sinks=c_sinks, w_out=w_out, norm2_g=norm2_g, w_gate=w_gate, w_up=w_up, w_down=w_down, loss_target=loss_target, m_norm1_g=m_norm1_g, m_w_in=m_w_in, m_a_q_g=m_a_q_g, m_a_k_g=m_a_k_g, m_conv_w=m_conv_w, m_conv_b=m_conv_b, m_conv_ln_g=m_conv_ln_g, m_conv_ln_b=m_conv_ln_b, m_c_q_g=m_c_q_g, m_c_k_g=m_c_k_g, m_c_sinks=m_c_sinks, m_w_out=m_w_out, m_norm2_g=m_norm2_g, m_w_gate=m_w_gate, m_w_up=m_w_up, m_w_down=m_w_down, v_norm1_g=v_norm1_g, v_w_in=v_w_in, v_a_q_g=v_a_q_g, v_a_k_g=v_a_k_g, v_conv_w=v_conv_w, v_conv_b=v_conv_b, v_conv_ln_g=v_conv_ln_g, v_conv_ln_b=v_conv_ln_b, v_c_q_g=v_c_q_g, v_c_k_g=v_c_k_g, v_c_sinks=v_c_sinks, v_w_out=v_w_out, v_norm2_g=v_norm2_g, v_w_gate=v_w_gate, v_w_up=v_w_up, v_w_down=v_w_down)
    weights = {n: given[n] for n in TWIN_WEIGHTS}
    shared = {n: given[n] for n in SHARED_INPUTS}
    per_example = {n: given[n] for n in ['x']}
    grad_fn = _jax.value_and_grad(_loss, argnums=(0, 1))

    def one_microbatch(ex, loss_target):
        ex = dict(ex)
        diff = ex.pop(TWIN_DIFF_INPUT)
        return grad_fn(weights, diff, {**shared, **ex}, loss_target)

    if N_MICROBATCH == 1:
        loss, (grad_w, grad_x) = one_microbatch(per_example, given["loss_target"])
    else:
        def body(carry, xs):
            loss_sum, grad_sum = carry
            l_k, (gw_k, gx_k) = one_microbatch(xs[0], xs[1])
            with _jax.named_scope("update"):
                return (loss_sum + l_k, _jax.tree.map(_jnp.add, grad_sum, gw_k)), gx_k

        init = (_jnp.zeros((), _jnp.float32), _jax.tree.map(_jnp.zeros_like, weights))
        (loss, grad_w), grad_x = _jax.lax.scan(body, init, (per_example, given["loss_target"]))
    with _jax.named_scope("update"):
        delta_w, new_m, new_v = {}, {}, {}
        for n in TWIN_WEIGHTS:
            delta_w[n], new_m[n], new_v[n] = _adamw(weights[n], grad_w[n], given["m_" + n], given["v_" + n])
    return (loss, grad_x, *[grad_w[n] for n in TWIN_WEIGHTS], *[delta_w[n] for n in TWIN_WEIGHTS],
            *[new_m[n] for n in TWIN_WEIGHTS], *[new_v[n] for n in TWIN_WEIGHTS])
```

```python
import functools

import jax
import jax.numpy as jnp
from jax import lax
from jax.experimental import pallas as pl
from jax.experimental.pallas import tpu as pltpu

f32 = jnp.float32
MXU = jnp.bfloat16
S = jax.ShapeDtypeStruct
MESH = pl.DeviceIdType.MESH

EPS = 1e-6
NEG = -1e30
HD = 64
BLK = 128
A_W, B_W, C_W = 512, 512, 1024
KV_W = 128
IN_W = 3 * A_W + 2 * B_W + C_W + 2 * KV_W
CONV_K = 31
HALO = 32
DILATIONS = (1, 4, 16)
A_DIST, C_DIST = 128, 127
SCALE = HD ** -0.5
VMEM_LIMIT = 56 * 1024 * 1024

ADAM_LR, ADAM_B1, ADAM_B2, ADAM_EPS, ADAM_WD, ADAM_STEP = 0.001, 0.9, 0.999, 1e-08, 0.01, 10


def _cp(*sem):
    return pltpu.CompilerParams(dimension_semantics=sem, vmem_limit_bytes=VMEM_LIMIT)


def _nt(a, b):
    return lax.dot_general(a, b, (((1,), (1,)), ((), ())), preferred_element_type=f32)


def _tn(a, b):
    return lax.dot_general(a, b, (((0,), (0,)), ((), ())), preferred_element_type=f32)


def _nn(a, b):
    return jnp.dot(a, b, preferred_element_type=f32)


def _sigmoid(x):
    return 1.0 / (1.0 + jnp.exp(-x))


def _seg_sum(v, e_ref):
    hi = v.astype(jnp.bfloat16)
    lo = (v - hi.astype(f32)).astype(jnp.bfloat16)
    e = e_ref[...]
    return _nn(hi, e) + _nn(lo, e)


def _seg_sum128(v, e_ref):
    e = e_ref[0:128, 0:128]
    hi = v.astype(jnp.bfloat16)
    lo = (v - hi.astype(f32)).astype(jnp.bfloat16)
    return _nn(hi, e) + _nn(lo, e)


def _head_eye():
    r = lax.broadcasted_iota(jnp.int32, (512, 512), 0) // HD
    c = lax.broadcasted_iota(jnp.int32, (512, 512), 1) // HD
    return (r == c).astype(jnp.bfloat16)


def rms_proj(x, g, w, *, tm, tn, name):
    T, D = x.shape
    N = w.shape[1]

    def body(x_ref, g_ref, w_ref, h_ref, o_ref):
        @pl.when(pl.program_id(1) == 0)
        def _():
            xf = x_ref[...]
            r = lax.rsqrt(jnp.mean(xf * xf, axis=-1, keepdims=True) + EPS)
            h_ref[...] = (xf * r * g_ref[...]).astype(MXU)
        o_ref[...] = _nn(h_ref[...], w_ref[...])

    return pl.pallas_call(
        body, name=name, grid=(T // tm, N // tn),
        in_specs=[pl.BlockSpec((tm, D), lambda i, j: (i, 0)), pl.BlockSpec((1, D), lambda i, j: (0, 0)),
                  pl.BlockSpec((D, tn), lambda i, j: (0, j))],
        out_specs=[pl.BlockSpec((tm, D), lambda i, j: (i, 0)), pl.BlockSpec((tm, tn), lambda i, j: (i, j))],
        out_shape=[S((T, D), MXU), S((T, N), f32)],
        compiler_params=_cp("arbitrary", "arbitrary"),
    )(x, g, w)


def rms_swiglu(x, g, wg, wu, *, tm, tn, name):
    T, D = x.shape
    N = wg.shape[1]

    def body(x_ref, g_ref, wg_ref, wu_ref, h_ref, gate_ref, up_ref, act_ref):
        @pl.when(pl.program_id(1) == 0)
        def _():
            xf = x_ref[...]
            r = lax.rsqrt(jnp.mean(xf * xf, axis=-1, keepdims=True) + EPS)
            h_ref[...] = (xf * r * g_ref[...]).astype(MXU)
        h = h_ref[...]
        gate = _nn(h, wg_ref[...])
        up = _nn(h, wu_ref[...])
        gate_ref[...] = gate
        up_ref[...] = up
        act_ref[...] = (gate * _sigmoid(gate) * up).astype(MXU)

    wspec = pl.BlockSpec((D, tn), lambda i, j: (0, j))
    ospec = pl.BlockSpec((tm, tn), lambda i, j: (i, j))
    return pl.pallas_call(
        body, name=name, grid=(T // tm, N // tn),
        in_specs=[pl.BlockSpec((tm, D), lambda i, j: (i, 0)), pl.BlockSpec((1, D), lambda i, j: (0, 0)), wspec, wspec],
        out_specs=[pl.BlockSpec((tm, D), lambda i, j: (i, 0)), ospec, ospec, ospec],
        out_shape=[S((T, D), MXU), S((T, N), f32), S((T, N), f32), S((T, N), MXU)],
        compiler_params=_cp("arbitrary", "arbitrary"),
    )(x, g, wg, wu)


def matmul_res(a, w, res, *, tm, tn, tk, name):
    T, K = a.shape
    N = w.shape[1]
    nk = K // tk

    def body(a_ref, w_ref, r_ref, o_ref, acc_ref):
        k = pl.program_id(2)
        part = _nn(a_ref[...], w_ref[...])

        @pl.when(k == 0)
        def _():
            acc_ref[...] = part

        @pl.when(k > 0)
        def _():
            acc_ref[...] += part

        @pl.when(k == nk - 1)
        def _():
            o_ref[...] = r_ref[...] + acc_ref[...]

    return pl.pallas_call(
        body, name=name, grid=(T // tm, N // tn, nk),
        in_specs=[pl.BlockSpec((tm, tk), lambda i, j, k: (i, k)), pl.BlockSpec((tk, tn), lambda i, j, k: (k, j)),
                  pl.BlockSpec((tm, tn), lambda i, j, k: (i, j))],
        out_specs=pl.BlockSpec((tm, tn), lambda i, j, k: (i, j)),
        out_shape=S((T, N), f32),
        scratch_shapes=[pltpu.VMEM((tm, tn), f32)],
        compiler_params=_cp("arbitrary", "arbitrary", "arbitrary"),
    )(a, w, res)


def nt_plain(a, w, *, tm, tn, name):
    T, K = a.shape
    N = w.shape[0]

    def body(a_ref, w_ref, o_ref):
        o_ref[...] = _nt(a_ref[...], w_ref[...])

    return pl.pallas_call(
        body, name=name, grid=(T // tm, N // tn),
        in_specs=[pl.BlockSpec((tm, K), lambda i, j: (i, 0)), pl.BlockSpec((tn, K), lambda i, j: (j, 0))],
        out_specs=pl.BlockSpec((tm, tn), lambda i, j: (i, j)),
        out_shape=S((T, N), f32),
        compiler_params=_cp("arbitrary", "arbitrary"),
    )(a, w)


def nt_swiglu_bwd(dy, wd, gate, up, *, tm, tn, name):
    T, D = dy.shape
    F = wd.shape[0]

    def body(dy_ref, w_ref, g_ref, u_ref, dg_ref, du_ref):
        d_act = _nt(dy_ref[...], w_ref[...])
        g = g_ref[...]
        sg = _sigmoid(g)
        du_ref[...] = (d_act * (g * sg)).astype(MXU)
        dg_ref[...] = (d_act * u_ref[...] * (sg * (1.0 + g * (1.0 - sg)))).astype(MXU)

    blk = pl.BlockSpec((tm, tn), lambda i, j: (i, j))
    return pl.pallas_call(
        body, name=name, grid=(T // tm, F // tn),
        in_specs=[pl.BlockSpec((tm, D), lambda i, j: (i, 0)), pl.BlockSpec((tn, D), lambda i, j: (j, 0)), blk, blk],
        out_specs=[blk, blk],
        out_shape=[S((T, F), MXU), S((T, F), MXU)],
        compiler_params=_cp("arbitrary", "arbitrary"),
    )(dy, wd, gate, up)


def nt_rms_bwd(terms, x, g, dres, *, tm, tk, name):
    T, D = x.shape
    K = terms[0][0].shape[1]
    nk = K // tk
    nt = len(terms)
    ni = T // tm

    def body(*refs):
        a_refs = refs[0:2 * nt:2]
        w_refs = refs[1:2 * nt:2]
        x_ref, g_ref, r_ref, dx_ref, dxb_ref, dg_ref, acc_ref = refs[2 * nt:]
        i, k = pl.program_id(0), pl.program_id(1)
        part = _nt(a_refs[0][...], w_refs[0][...])
        for t in range(1, nt):
            part += _nt(a_refs[t][...], w_refs[t][...])

        @pl.when(k == 0)
        def _():
            acc_ref[...] = part

        @pl.when(k > 0)
        def _():
            acc_ref[...] += part

        @pl.when(k == nk - 1)
        def _():
            dh = acc_ref[...]
            xf = x_ref[...]
            r = lax.rsqrt(jnp.mean(xf * xf, axis=-1, keepdims=True) + EPS)
            y = xf * r
            dgain = jnp.sum(dh * y, axis=0, keepdims=True)
            dy = dh * g_ref[...]
            dx = r_ref[...] + r * (dy - y * jnp.mean(dy * y, axis=-1, keepdims=True))
            dx_ref[...] = dx
            dxb_ref[...] = dx.astype(MXU)

            @pl.when(i == 0)
            def _():
                dg_ref[...] = dgain

            @pl.when(i > 0)
            def _():
                dg_ref[...] += dgain

    in_specs, args = [], []
    for a, w in terms:
        in_specs += [pl.BlockSpec((tm, tk), lambda i, k: (i, k)), pl.BlockSpec((D, tk), lambda i, k: (0, k))]
        args += [a, w]
    row = pl.BlockSpec((tm, D), lambda i, k: (i, 0))
    vec = pl.BlockSpec((1, D), lambda i, k: (0, 0))
    in_specs += [row, vec, row]
    return pl.pallas_call(
        body, name=name, grid=(ni, nk), in_specs=in_specs,
        out_specs=[row, row, vec],
        out_shape=[S((T, D), f32), S((T, D), MXU), S((1, D), f32)],
        scratch_shapes=[pltpu.VMEM((tm, D), f32)],
        compiler_params=_cp("arbitrary", "arbitrary"),
    )(*args, x, g, dres)


def tn_matmul(a, b, *, tm, tn, tk, name):
    T, M = a.shape
    N = b.shape[1]

    def body(a_ref, b_ref, o_ref):
        part = _tn(a_ref[...], b_ref[...])

        @pl.when(pl.program_id(2) == 0)
        def _():
            o_ref[...] = part

        @pl.when(pl.program_id(2) > 0)
        def _():
            o_ref[...] += part

    return pl.pallas_call(
        body, name=name, grid=(M // tm, N // tn, T // tk),
        in_specs=[pl.BlockSpec((tk, tm), lambda i, j, k: (k, i)), pl.BlockSpec((tk, tn), lambda i, j, k: (k, j))],
        out_specs=pl.BlockSpec((tm, tn), lambda i, j, k: (i, j)),
        out_shape=S((M, N), f32),
        compiler_params=_cp("arbitrary", "arbitrary", "arbitrary"),
    )(a, b)


def loss_head(y, target, *, tm, name):
    T, D = y.shape
    ni = T // tm

    def body(y_ref, t_ref, dy_ref, dyb_ref, l_ref, acc_ref):
        i = pl.program_id(0)
        e = y_ref[...] - t_ref[...]
        dy = e * (1.0 / D)
        dy_ref[...] = dy
        dyb_ref[...] = dy.astype(MXU)
        part = jnp.sum(e * e, axis=0, keepdims=True)

        @pl.when(i == 0)
        def _():
            acc_ref[...] = part

        @pl.when(i > 0)
        def _():
            acc_ref[...] += part

        @pl.when(i == ni - 1)
        def _():
            tot = jnp.sum(acc_ref[...], axis=1, keepdims=True) * (0.5 / D)
            l_ref[...] = jnp.broadcast_to(tot, (1, 128))

    row = pl.BlockSpec((tm, D), lambda i: (i, 0))
    return pl.pallas_call(
        body, name=name, grid=(ni,), in_specs=[row, row],
        out_specs=[row, row, pl.BlockSpec((1, 128), lambda i: (0, 0))],
        out_shape=[S((T, D), f32), S((T, D), MXU), S((1, 128), f32)],
        scratch_shapes=[pltpu.VMEM((1, D), f32)],
        compiler_params=_cp("arbitrary"),
    )(y, target)


def _qk_norm(v, gain, e_ref):
    r = lax.rsqrt(_seg_sum(v * v, e_ref) * (1.0 / HD) + EPS)
    return v * r * gain


def _dup_halves(pair):
    rolled = pltpu.roll(pair, HD, 1)
    lo = lax.broadcasted_iota(jnp.int32, pair.shape, 1) < HD
    return jnp.where(lo, pair, rolled), jnp.where(lo, rolled, pair)


def prep_fwd(proj, e, gains, *, tm, name):
    T = proj.shape[0]

    def body(p_ref, e_ref, gaq, gak, gcq, gck, aq, ak, av, cq, ckk, cvv):
        aq[...] = _qk_norm(p_ref[:, 0:512], gaq[...], e_ref).astype(MXU)
        ak[...] = _qk_norm(p_ref[:, 512:1024], gak[...], e_ref).astype(MXU)
        av[...] = p_ref[:, 1024:1536].astype(MXU)
        cq[:, 0:512] = _qk_norm(p_ref[:, 2560:3072], gcq[...], e_ref).astype(MXU)
        cq[:, 512:1024] = _qk_norm(p_ref[:, 3072:3584], gcq[...], e_ref).astype(MXU)
        kraw = p_ref[:, 3584:3712]
        kn = kraw * lax.rsqrt(_seg_sum128(kraw * kraw, e_ref) * (1.0 / HD) + EPS) * gck[...]
        k0, k1 = _dup_halves(kn)
        ckk[:, 0:128] = k0.astype(MXU)
        ckk[:, 128:256] = k1.astype(MXU)
        v0, v1 = _dup_halves(p_ref[:, 3712:3840])
        cvv[:, 0:128] = v0.astype(MXU)
        cvv[:, 128:256] = v1.astype(MXU)

    def vec(n):
        return pl.BlockSpec((1, n), lambda i: (0, 0))

    def rows(n):
        return pl.BlockSpec((tm, n), lambda i: (i, 0))

    return pl.pallas_call(
        body, name=name, grid=(T // tm,),
        in_specs=[rows(IN_W), pl.BlockSpec((512, 512), lambda i: (0, 0)), vec(512), vec(512), vec(512), vec(128)],
        out_specs=[rows(512), rows(512), rows(512), rows(1024), rows(256), rows(256)],
        out_shape=[S((T, 512), MXU)] * 3 + [S((T, 1024), MXU), S((T, 256), MXU), S((T, 256), MXU)],
        compiler_params=_cp("arbitrary"),
    )(proj, e, *gains)


def _band_masks(max_dist, shut):
    r = lax.broadcasted_iota(jnp.int32, (BLK, BLK), 0)
    c = lax.broadcasted_iota(jnp.int32, (BLK, BLK), 1)
    return c <= r, c >= r + (BLK - max_dist) + shut


def _head_masks():
    lo = (lax.broadcasted_iota(jnp.int32, (BLK, BLK), 1) < HD).astype(f32)
    return lo.astype(MXU), (1.0 - lo).astype(MXU)


def band_attn_fwd(q, k, v, sinks, *, max_dist, group, name):
    L = q.shape[0]
    P = q.shape[1] // BLK
    QB = min(512, L)
    n = QB // BLK
    nb = L // QB

    def body(*refs):
        if sinks is None:
            q_ref, kc_ref, kp_ref, vc_ref, vp_ref, o_ref, l_ref = refs
        else:
            q_ref, kc_ref, kp_ref, vc_ref, vp_ref, s_ref, o_ref, l_ref = refs
        b = pl.program_id(0)
        cur_ok, prev_ok = _band_masks(max_dist, 0)
        first_ok = _band_masks(max_dist, jnp.where(b > 0, 0, BLK + 1))[1]
        lane_lo = lax.broadcasted_iota(jnp.int32, (BLK, BLK), 1) < HD
        head_masks = _head_masks()
        row_lo = lax.broadcasted_iota(jnp.int32, (1, BLK), 1) < HD
        for sub in range(n):
            rows = slice(sub * BLK, (sub + 1) * BLK)
            qs = q_ref[rows, :]
            kc, vc = kc_ref[rows, :], vc_ref[rows, :]
            if sub == 0:
                kp, vp = kp_ref[...], vp_ref[...]
                pmask = first_ok
            else:
                prow = slice((sub - 1) * BLK, sub * BLK)
                kp, vp = kc_ref[prow, :], vc_ref[prow, :]
                pmask = prev_ok
            outs, lses = [], []
            for j in range(2):
                qj = qs * head_masks[j]
                sc = jnp.where(cur_ok, _nt(qj, kc) * SCALE, NEG)
                sp = jnp.where(pmask, _nt(qj, kp) * SCALE, NEG)
                m = jnp.maximum(jnp.max(sc, axis=1, keepdims=True), jnp.max(sp, axis=1, keepdims=True))
                if sinks is not None:
                    in_head = row_lo if j == 0 else jnp.logical_not(row_lo)
                    sk = jnp.max(jnp.where(in_head, s_ref[...], NEG), axis=1, keepdims=True)
                    m = jnp.maximum(m, sk)
                pc = jnp.exp(sc - m)
                pp = jnp.exp(sp - m)
                den = jnp.sum(pc, axis=1, keepdims=True) + jnp.sum(pp, axis=1, keepdims=True)
                if sinks is not None:
                    den = den + jnp.exp(sk - m)
                o = _nn(pc.astype(MXU), vc) + _nn(pp.astype(MXU), vp)
                outs.append(o / den)
                lses.append(m + jnp.log(den))
            o_ref[rows, :] = jnp.where(lane_lo, outs[0], outs[1])
            l_ref[rows, :] = jnp.where(lane_lo, lses[0], lses[1])

    qspec = pl.BlockSpec((QB, BLK), lambda b, p: (b, p))
    cur = pl.BlockSpec((QB, BLK), lambda b, p: (b, p // group))
    prev = pl.BlockSpec((BLK, BLK), lambda b, p: (jnp.maximum(b * n - 1, 0), p // group))
    in_specs = [qspec, cur, prev, cur, prev]
    args = [q, k, k, v, v]
    if sinks is not None:
        in_specs.append(pl.BlockSpec((1, BLK), lambda b, p: (0, p)))
        args.append(sinks)
    return pl.pallas_call(
        body, name=name, grid=(nb, P), in_specs=in_specs, out_specs=[qspec, qspec],
        out_shape=[S(q.shape, f32), S(q.shape, f32)],
        compiler_params=_cp("arbitrary", "arbitrary"),
    )(*args)


def band_attn_bwd(q, k, v, lse, do, dd, *, max_dist, group, name):
    L = q.shape[0]
    P = q.shape[1] // BLK
    QB = min(512, L)
    n = QB // BLK
    nb = L // QB

    def body(q_ref, qn_ref, do_ref, don_ref, l_ref, ln_ref, d_ref, dn_ref, kc_ref, kp_ref, vc_ref, vp_ref,
             dq_ref, dk_ref, dv_ref):
        b, p = pl.program_id(0), pl.program_id(1)
        cur_ok, prev_ok = _band_masks(max_dist, 0)
        head_ok = _band_masks(max_dist, jnp.where(b > 0, 0, BLK + 1))[1]
        tail_ok = _band_masks(max_dist, jnp.where(b < nb - 1, 0, BLK + 1))[1]
        lane_lo = lax.broadcasted_iota(jnp.int32, (BLK, BLK), 1) < HD
        hms = tuple(zip((lane_lo, jnp.logical_not(lane_lo)), _head_masks()))

        def q_side(sub):
            if sub == n:
                return qn_ref[...], don_ref[...], ln_ref[...], dn_ref[...]
            rows = slice(sub * BLK, (sub + 1) * BLK)
            return q_ref[rows, :], do_ref[rows, :], l_ref[rows, :], d_ref[rows, :]

        def k_side(sub):
            if sub == -1:
                return kp_ref[...], vp_ref[...]
            rows = slice(sub * BLK, (sub + 1) * BLK)
            return kc_ref[rows, :], vc_ref[rows, :]

        def pair(qsub, ksub, mask, want_dq, want_dkv):
            qs, dos, lb, db = q_side(qsub)
            kk, vv = k_side(ksub)
            dq = jnp.zeros((BLK, BLK), f32)
            dk = jnp.zeros((BLK, BLK), f32)
            dv = jnp.zeros((BLK, BLK), f32)
            for hm, hmb in hms:
                qj = qs * hmb
                doj = dos * hmb
                lj = jnp.max(jnp.where(hm, lb, NEG), axis=1, keepdims=True)
                dj = jnp.max(jnp.where(hm, db, NEG), axis=1, keepdims=True)
                pr = jnp.where(mask, jnp.exp(_nt(qj, kk) * SCALE - lj), 0.0)
                ds = (pr * (_nt(doj, vv) - dj) * SCALE).astype(MXU)
                if want_dq:
                    dq += jnp.where(hm, _nn(ds, kk), 0.0)
                if want_dkv:
                    dk += _tn(ds, qj)
                    dv += _tn(pr.astype(MXU), doj)
            return dq, dk, dv

        dqs = [None] * n
        dks = [None] * n
        dvs = [None] * n
        for i in range(n):
            dq, dk, dv = pair(i, i, cur_ok, True, True)
            dqs[i], dks[i], dvs[i] = dq, dk, dv
        for i in range(1, n):
            dq, dk, dv = pair(i, i - 1, prev_ok, True, True)
            dqs[i] += dq
            dks[i - 1] += dk
            dvs[i - 1] += dv
        dq, _, _ = pair(0, -1, head_ok, True, False)
        dqs[0] += dq
        _, dk, dv = pair(n, n - 1, tail_ok, False, True)
        dks[n - 1] += dk
        dvs[n - 1] += dv
        first = p % group == 0
        for i in range(n):
            rows = slice(i * BLK, (i + 1) * BLK)
            dq_ref[rows, :] = dqs[i]
            if group == 1:
                dk_ref[rows, :] = dks[i]
                dv_ref[rows, :] = dvs[i]
            else:
                @pl.when(first)
                def _():
                    dk_ref[rows, :] = dks[i]
                    dv_ref[rows, :] = dvs[i]

                @pl.when(jnp.logical_not(first))
                def _():
                    dk_ref[rows, :] += dks[i]
                    dv_ref[rows, :] += dvs[i]

    qspec = pl.BlockSpec((QB, BLK), lambda b, p: (b, p))
    qnext = pl.BlockSpec((BLK, BLK), lambda b, p: (jnp.minimum((b + 1) * n, L // BLK - 1), p))
    cur = pl.BlockSpec((QB, BLK), lambda b, p: (b, p // group))
    prev = pl.BlockSpec((BLK, BLK), lambda b, p: (jnp.maximum(b * n - 1, 0), p // group))
    return pl.pallas_call(
        body, name=name, grid=(nb, P),
        in_specs=[qspec, qnext, qspec, qnext, qspec, qnext, qspec, qnext, cur, prev, cur, prev],
        out_specs=[qspec, cur, cur],
        out_shape=[S(q.shape, f32), S(k.shape, f32), S(k.shape, f32)],
        compiler_params=_cp("arbitrary", "arbitrary"),
    )(q, q, do, do, lse, lse, dd, dd, k, k, v, v)


def dil_combine_fwd(ols, *, tm, name):
    T = ols[0].shape[0]

    def body(o1, l1, o2, l2, o3, l3, out_ref):
        a, b, c = l1[...], l2[...], l3[...]
        m = jnp.maximum(jnp.maximum(a, b), c)
        ea, eb, ec = jnp.exp(a - m), jnp.exp(b - m), jnp.exp(c - m)
        out = (ea * o1[...] + eb * o2[...] + ec * o3[...]) / (ea + eb + ec)
        out_ref[...] = out.astype(MXU)

    row = pl.BlockSpec((tm, 512), lambda i: (i, 0))
    return pl.pallas_call(body, name=name, grid=(T // tm,), in_specs=[row] * 6, out_specs=row,
                          out_shape=S((T, 512), MXU), compiler_params=_cp("arbitrary"))(*ols)


def dil_combine_bwd(ols, dmix, e, *, tm, name):
    T = ols[0].shape[0]

    def body(o1, l1, o2, l2, o3, l3, d_ref, e_ref, do1, do2, do3, dd1, dd2, dd3):
        a, b, c = l1[...], l2[...], l3[...]
        m = jnp.maximum(jnp.maximum(a, b), c)
        ea, eb, ec = jnp.exp(a - m), jnp.exp(b - m), jnp.exp(c - m)
        inv = 1.0 / (ea + eb + ec)
        wa, wb, wc = ea * inv, eb * inv, ec * inv
        dout = d_ref[...]
        gbar = _seg_sum(dout * (wa * o1[...] + wb * o2[...] + wc * o3[...]), e_ref)
        do1[...] = (wa * dout).astype(MXU)
        do2[...] = (wb * dout).astype(MXU)
        do3[...] = (wc * dout).astype(MXU)
        dd1[...] = wa * gbar
        dd2[...] = wb * gbar
        dd3[...] = wc * gbar

    row = pl.BlockSpec((tm, 512), lambda i: (i, 0))
    return pl.pallas_call(
        body, name=name, grid=(T // tm,),
        in_specs=[row] * 6 + [row, pl.BlockSpec((512, 512), lambda i: (0, 0))],
        out_specs=[row] * 6,
        out_shape=[S((T, 512), MXU)] * 3 + [S((T, 512), f32)] * 3,
        compiler_params=_cp("arbitrary"),
    )(*ols, dmix, e)


def swa_pre_bwd(o, lse, dmix, sinks, e, *, tm, name):
    T = o.shape[0]
    ni = T // tm

    def body(o_ref, l_ref, d_ref, s_ref, e_ref, do_ref, dd_ref, ds_ref):
        i = pl.program_id(0)
        dout = d_ref[...]
        do_ref[...] = dout.astype(MXU)
        prod = dout * o_ref[...]
        dd = jnp.concatenate([_seg_sum(prod[:, 0:512], e_ref), _seg_sum(prod[:, 512:1024], e_ref)], axis=1)
        dd_ref[...] = dd
        part = -jnp.sum(jnp.exp(s_ref[...] - l_ref[...]) * dd, axis=0, keepdims=True)

        @pl.when(i == 0)
        def _():
            ds_ref[...] = part

        @pl.when(i > 0)
        def _():
            ds_ref[...] += part

    row = pl.BlockSpec((tm, 1024), lambda i: (i, 0))
    vec = pl.BlockSpec((1, 1024), lambda i: (0, 0))
    return pl.pallas_call(
        body, name=name, grid=(ni,),
        in_specs=[row, row, pl.BlockSpec((tm, 1024), lambda i: (i, 1)), vec, pl.BlockSpec((512, 512), lambda i: (0, 0))],
        out_specs=[row, row, vec],
        out_shape=[S((T, 1024), MXU), S((T, 1024), f32), S((1, 1024), f32)],
        compiler_params=_cp("arbitrary"),
    )(o, lse, dmix, sinks, e)


def _conv_taps(buf_ref, w_ref, start, rows):
    acc = buf_ref[pl.ds(start, rows), :] * w_ref[pl.ds(0, 1), :]
    for j in range(1, CONV_K):
        acc += buf_ref[pl.ds(start + j, rows), :] * w_ref[pl.ds(j, 1), :]
    return acc


def conv_fwd(proj, w, b, ln_g, ln_b, *, tb, name):
    T = proj.shape[0]
    hb = tb // HALO

    def body(u_ref, g_ref, up_ref, gp_ref, w_ref, b_ref, lg_ref, lb_ref, o_ref, hbuf):
        i = pl.program_id(0)
        hprev = up_ref[...] * _sigmoid(gp_ref[...])
        hbuf[0:HALO, :] = hprev * jnp.where(i > 0, 1.0, 0.0)
        hbuf[HALO:HALO + tb, :] = u_ref[...] * _sigmoid(g_ref[...])
        y = _conv_taps(hbuf, w_ref, HALO - (CONV_K - 1), tb) + b_ref[...]
        mu = jnp.mean(y, axis=-1, keepdims=True)
        yc = y - mu
        var = jnp.mean(yc * yc, axis=-1, keepdims=True)
        z = yc * lax.rsqrt(var + EPS) * lg_ref[...] + lb_ref[...]
        o_ref[...] = (z * _sigmoid(z)).astype(MXU)

    vec = pl.BlockSpec((1, 512), lambda i: (0, 0))
    return pl.pallas_call(
        body, name=name, grid=(T // tb,),
        in_specs=[pl.BlockSpec((tb, 512), lambda i: (i, 3)), pl.BlockSpec((tb, 512), lambda i: (i, 4)),
                  pl.BlockSpec((HALO, 512), lambda i: (jnp.maximum(i * hb - 1, 0), 3)),
                  pl.BlockSpec((HALO, 512), lambda i: (jnp.maximum(i * hb - 1, 0), 4)),
                  pl.BlockSpec((HALO, 512), lambda i: (0, 0)), vec, vec, vec],
        out_specs=pl.BlockSpec((tb, 512), lambda i: (i, 0)),
        out_shape=S((T, 512), MXU),
        scratch_shapes=[pltpu.VMEM((tb + HALO, 512), f32)],
        compiler_params=_cp("arbitrary"),
    )(proj, proj, proj, proj, w, b, ln_g, ln_b)


def conv_bwd(proj, dmix, w, b, ln_g, ln_b, *, tb, name):
    T = proj.shape[0]
    hb = tb // HALO
    ni = T // tb
    last_h = T // HALO - 1
    ext = tb + HALO

    def body(u_ref, g_ref, up_ref, gp_ref, un_ref, gn_ref, d_ref, dn_ref, w_ref, b_ref, lg_ref, lb_ref,
             du_ref, dg_ref, dw_ref, db_ref, dlg_ref, dlb_ref, hbuf, dybuf):
        i = pl.program_id(0)
        hbuf[0:HALO, :] = up_ref[...] * _sigmoid(gp_ref[...]) * jnp.where(i > 0, 1.0, 0.0)
        u = u_ref[...]
        sg = _sigmoid(g_ref[...])
        hbuf[HALO:HALO + tb, :] = u * sg
        hbuf[HALO + tb:HALO + ext, :] = un_ref[...] * _sigmoid(gn_ref[...])
        y = _conv_taps(hbuf, w_ref, HALO - (CONV_K - 1), ext) + b_ref[...]
        mu = jnp.mean(y, axis=-1, keepdims=True)
        yc = y - mu
        rstd = lax.rsqrt(jnp.mean(yc * yc, axis=-1, keepdims=True) + EPS)
        yn = yc * rstd
        z = yn * lg_ref[...] + lb_ref[...]
        sz = _sigmoid(z)
        row = lax.broadcasted_iota(jnp.int32, (ext, 1), 0)
        own = row < tb
        keep = row < jnp.where(i < ni - 1, ext, tb)
        dout = jnp.concatenate([d_ref[...], dn_ref[...]], axis=0)
        dz = jnp.where(keep, dout * (sz * (1.0 + z * (1.0 - sz))), 0.0)
        dyn = dz * lg_ref[...]
        dy = rstd * (dyn - jnp.mean(dyn, axis=-1, keepdims=True) - yn * jnp.mean(dyn * yn, axis=-1, keepdims=True))
        dybuf[...] = dy
        dz_own = jnp.where(own, dz, 0.0)
        dlg = jnp.sum(dz_own * yn, axis=0, keepdims=True)
        dlb = jnp.sum(dz_own, axis=0, keepdims=True)
        dy_own = dybuf[0:tb, :]
        dbias = jnp.sum(dy_own, axis=0, keepdims=True)
        dh = dybuf[pl.ds(CONV_K - 1, tb), :] * w_ref[pl.ds(0, 1), :]
        for j in range(1, CONV_K):
            dh += dybuf[pl.ds(CONV_K - 1 - j, tb), :] * w_ref[pl.ds(j, 1), :]
        du_ref[...] = (dh * sg).astype(MXU)
        dg_ref[...] = (dh * u * sg * (1.0 - sg)).astype(MXU)
        taps = [jnp.sum(dy_own * hbuf[pl.ds(HALO - (CONV_K - 1) + j, tb), :], axis=0, keepdims=True)
                for j in range(CONV_K)]
        taps.append(jnp.zeros((1, 512), f32))
        dwt = jnp.concatenate(taps, axis=0)

        @pl.when(i == 0)
        def _():
            dw_ref[...] = dwt
            db_ref[...] = dbias
            dlg_ref[...] = dlg
            dlb_ref[...] = dlb

        @pl.when(i > 0)
        def _():
            dw_ref[...] += dwt
            db_ref[...] += dbias
            dlg_ref[...] += dlg
            dlb_ref[...] += dlb

    vec = pl.BlockSpec((1, 512), lambda i: (0, 0))
    wspec = pl.BlockSpec((HALO, 512), lambda i: (0, 0))

    def halo_prev(col):
        return pl.BlockSpec((HALO, 512), lambda i: (jnp.maximum(i * hb - 1, 0), col))

    def halo_next(col):
        return pl.BlockSpec((HALO, 512), lambda i: (jnp.minimum((i + 1) * hb, last_h), col))

    row = pl.BlockSpec((tb, 512), lambda i: (i, 0))
    return pl.pallas_call(
        body, name=name, grid=(ni,),
        in_specs=[pl.BlockSpec((tb, 512), lambda i: (i, 3)), pl.BlockSpec((tb, 512), lambda i: (i, 4)),
                  halo_prev(3), halo_prev(4), halo_next(3), halo_next(4),
                  pl.BlockSpec((tb, 512), lambda i: (i, 1)), halo_next(1), wspec, vec, vec, vec],
        out_specs=[row, row, wspec, vec, vec, vec],
        out_shape=[S((T, 512), MXU), S((T, 512), MXU), S((HALO, 512), f32)] + [S((1, 512), f32)] * 3,
        scratch_shapes=[pltpu.VMEM((tb + 2 * HALO, 512), f32), pltpu.VMEM((ext, 512), f32)],
        compiler_params=_cp("arbitrary"),
    )(proj, proj, proj, proj, proj, proj, dmix, dmix, w, b, ln_g, ln_b)


def _qk_norm_bwd(v, gain, dout, e_ref):
    r = lax.rsqrt(_seg_sum(v * v, e_ref) * (1.0 / HD) + EPS)
    y = v * r
    dgain = jnp.sum(dout * y, axis=0, keepdims=True)
    dy = dout * gain
    dv = r * (dy - y * (_seg_sum(dy * y, e_ref) * (1.0 / HD)))
    return dv, dgain


def prep_bwd(proj, e, gains, da, dc, dconv, *, tm, name):
    T = proj.shape[0]

    def body(*refs):
        p_ref, e_ref, gaq, gak, gcq, gck = refs[0:6]
        a_refs = refs[6:15]
        dcq, dckk, dcvv, du, dgt = refs[15:20]
        dp, gaq_o, gak_o, gcq_o, gck_o = refs[20:]
        i = pl.program_id(0)
        dq = a_refs[0][...] + a_refs[3][...] + a_refs[6][...]
        dk = a_refs[1][...] + a_refs[4][...] + a_refs[7][...]
        dv = a_refs[2][...] + a_refs[5][...] + a_refs[8][...]
        d, g_aq = _qk_norm_bwd(p_ref[:, 0:512], gaq[...], dq, e_ref)
        dp[:, 0:512] = d.astype(MXU)
        d, g_ak = _qk_norm_bwd(p_ref[:, 512:1024], gak[...], dk, e_ref)
        dp[:, 512:1024] = d.astype(MXU)
        dp[:, 1024:1536] = dv.astype(MXU)
        dp[:, 1536:2048] = du[...]
        dp[:, 2048:2560] = dgt[...]
        d, g_cq0 = _qk_norm_bwd(p_ref[:, 2560:3072], gcq[...], dcq[:, 0:512], e_ref)
        dp[:, 2560:3072] = d.astype(MXU)
        d, g_cq1 = _qk_norm_bwd(p_ref[:, 3072:3584], gcq[...], dcq[:, 512:1024], e_ref)
        dp[:, 3072:3584] = d.astype(MXU)
        lo = lax.broadcasted_iota(jnp.int32, (tm, 128), 1) < HD

        def fold(ref):
            g0, g1 = ref[:, 0:128], ref[:, 128:256]
            s0 = g0 + pltpu.roll(g0, HD, 1)
            s1 = g1 + pltpu.roll(g1, HD, 1)
            return jnp.where(lo, s0, s1)

        dkn = fold(dckk)
        kraw = p_ref[:, 3584:3712]
        r = lax.rsqrt(_seg_sum128(kraw * kraw, e_ref) * (1.0 / HD) + EPS)
        y = kraw * r
        g_ck = jnp.sum(dkn * y, axis=0, keepdims=True)
        dy = dkn * gck[...]
        dp[:, 3584:3712] = (r * (dy - y * (_seg_sum128(dy * y, e_ref) * (1.0 / HD)))).astype(MXU)
        dp[:, 3712:3840] = fold(dcvv).astype(MXU)
        g_cq = jnp.concatenate([g_cq0, g_cq1], axis=1)

        @pl.when(i == 0)
        def _():
            gaq_o[...] = g_aq
            gak_o[...] = g_ak
            gcq_o[...] = g_cq
            gck_o[...] = g_ck

        @pl.when(i > 0)
        def _():
            gaq_o[...] += g_aq
            gak_o[...] += g_ak
            gcq_o[...] += g_cq
            gck_o[...] += g_ck

    def vec(n):
        return pl.BlockSpec((1, n), lambda i: (0, 0))

    def rows(n):
        return pl.BlockSpec((tm, n), lambda i: (i, 0))

    return pl.pallas_call(
        body, name=name, grid=(T // tm,),
        in_specs=[rows(IN_W), pl.BlockSpec((512, 512), lambda i: (0, 0)), vec(512), vec(512), vec(512), vec(128)]
        + [rows(512)] * 9 + [rows(1024), rows(256), rows(256), rows(512), rows(512)],
        out_specs=[rows(IN_W), vec(512), vec(512), vec(1024), vec(128)],
        out_shape=[S((T, IN_W), MXU), S((1, 512), f32), S((1, 512), f32), S((1, 1024), f32), S((1, 128), f32)],
        compiler_params=_cp("arbitrary"),
    )(proj, e, *gains, *da, *dc, *dconv)


def adamw(w, m, v, pieces, *, tr, name):
    n, R, C = w.shape
    c1 = 1.0 - ADAM_B1 ** ADAM_STEP
    c2 = 1.0 - ADAM_B2 ** ADAM_STEP
    npc = len(pieces)

    def body(*refs):
        w_ref, m_ref, v_ref = refs[0:3]
        p_refs = refs[3:3 + npc]
        g_ref, d_ref, mo_ref, vo_ref = refs[3 + npc:]
        g = p_refs[0][...].astype(f32)
        for p in p_refs[1:]:
            g = g + p[...].astype(f32)
        mn = ADAM_B1 * m_ref[...] + (1.0 - ADAM_B1) * g
        vn = ADAM_B2 * v_ref[...] + (1.0 - ADAM_B2) * (g * g)
        g_ref[...] = g
        mo_ref[...] = mn
        vo_ref[...] = vn
        d_ref[...] = -ADAM_LR * ((mn / c1) / (jnp.sqrt(vn / c2) + ADAM_EPS) + ADAM_WD * w_ref[...])

    blk = pl.BlockSpec((None, tr, C), lambda l, i: (l, i, 0))
    return pl.pallas_call(
        body, name=name, grid=(n, R // tr), in_specs=[blk] * (3 + npc), out_specs=[blk] * 4,
        out_shape=[S(w.shape, f32)] * 4, compiler_params=_cp("arbitrary", "arbitrary"),
    )(w, m, v, *pieces)


def add_halves(mine, other, *, tr, name):
    n, R, C = mine.shape

    def body(a_ref, b_ref, o_ref):
        o_ref[...] = (a_ref[...] + b_ref[...]).astype(jnp.bfloat16)

    blk = pl.BlockSpec((None, tr, C), lambda l, i: (l, i, 0))
    return pl.pallas_call(body, name=name, grid=(n, R // tr), in_specs=[blk, blk], out_specs=blk,
                          out_shape=S(mine.shape, jnp.bfloat16), compiler_params=_cp("arbitrary", "arbitrary"))(mine, other)


def sum8(parts, *, name):
    _, R, C = parts.shape

    def body(p_ref, o_ref):
        acc = p_ref[0]
        for d in range(1, 8):
            acc = acc + p_ref[d]
        o_ref[...] = acc

    return pl.pallas_call(body, name=name, out_shape=S((R, C), f32))(parts)


def _pos():
    return lax.axis_index("x"), lax.axis_index("y"), lax.axis_index("c")


def _other_chips(x, y):
    return [(1 - x, y), (x, 1 - y), (1 - x, 1 - y)]


ANY = pl.BlockSpec(memory_space=pl.ANY)


def gather_shards(shards, *, name):
    nt = len(shards)

    def body(*refs):
        in_refs = refs[:nt]
        out_refs = refs[nt:2 * nt]
        send_sems, recv_sems, local_sems = refs[2 * nt:]
        x, y, c = _pos()
        sibling = (x, y, 1 - c)
        chips = _other_chips(x, y)
        me = 2 * x + y

        def copy(t, k, chip, half, to, src=None):
            dst = out_refs[t].at[2 * chip[0] + chip[1], half]
            return pltpu.make_async_remote_copy(
                src_ref=dst if src is None else src, dst_ref=dst,
                send_sem=send_sems.at[t, k], recv_sem=recv_sems.at[t, k], device_id=to, device_id_type=MESH)

        local = [pltpu.make_async_copy(in_refs[t], out_refs[t].at[me], local_sems.at[t]) for t in range(nt)]
        for cp in local:
            cp.start()
        first = []
        for t in range(nt):
            for j, chip in enumerate(chips):
                cp = copy(t, j, (x, y), c, (*chip, c), src=in_refs[t].at[c])
                cp.start()
                first.append(cp)
        passed = []
        for t in range(nt):
            for j, chip in enumerate(chips):
                copy(t, j, chip, c, (x, y, c)).wait_recv()
                cp = copy(t, 3 + j, chip, c, sibling)
                cp.start()
                passed.append(cp)
        for t in range(nt):
            for j, chip in enumerate(chips):
                copy(t, 3 + j, chip, 1 - c, (x, y, c)).wait_recv()
        for cp in first + passed:
            cp.wait_send()
        for cp in local:
            cp.wait()

    return pl.pallas_call(
        body, name=name, in_specs=[ANY] * nt, out_specs=[ANY] * nt,
        out_shape=[S((4,) + s.shape, s.dtype) for s in shards],
        scratch_shapes=[pltpu.SemaphoreType.DMA((nt, 6)), pltpu.SemaphoreType.DMA((nt, 6)),
                        pltpu.SemaphoreType.DMA((nt,))],
    )(*shards)


def sibling_swap(arrs, *, name):
    nt = len(arrs)

    def body(*refs):
        in_refs = refs[:nt]
        out_refs = refs[nt:2 * nt]
        send_sems, recv_sems = refs[2 * nt:]
        x, y, c = _pos()
        cps = [pltpu.make_async_remote_copy(src_ref=in_refs[t], dst_ref=out_refs[t], send_sem=send_sems.at[t],
                                            recv_sem=recv_sems.at[t], device_id=(x, y, 1 - c), device_id_type=MESH)
               for t in range(nt)]
        for cp in cps:
            cp.start()
        for cp in cps:
            cp.wait()

    return pl.pallas_call(
        body, name=name, in_specs=[ANY] * nt, out_specs=[ANY] * nt,
        out_shape=[S(a.shape, a.dtype) for a in arrs],
        scratch_shapes=[pltpu.SemaphoreType.DMA((nt,)), pltpu.SemaphoreType.DMA((nt,))],
    )(*arrs)


def exchange_pieces(arrs, *, name):
    nt = len(arrs)

    def body(*refs):
        in_refs = refs[:nt]
        out_refs = refs[nt:2 * nt]
        send_sems, recv_sems, local_sems = refs[2 * nt:]
        x, y, c = _pos()
        sibling = (x, y, 1 - c)
        chips = _other_chips(x, y)
        me = 2 * x + y

        def copy(t, k, half, src_chip, to, src=None):
            dst = out_refs[t].at[half, src_chip]
            return pltpu.make_async_remote_copy(
                src_ref=dst if src is None else src, dst_ref=dst,
                send_sem=send_sems.at[t, k], recv_sem=recv_sems.at[t, k], device_id=to, device_id_type=MESH)

        local = [pltpu.make_async_copy(in_refs[t].at[me], out_refs[t].at[c, me], local_sems.at[t]) for t in range(nt)]
        for cp in local:
            cp.start()
        first = []
        for t in range(nt):
            for j, chip in enumerate(chips):
                cp = copy(t, j, c, me, (*chip, c), src=in_refs[t].at[2 * chip[0] + chip[1]])
                cp.start()
                first.append(cp)
            cp = copy(t, 6, c, me, sibling, src=in_refs[t].at[me])
            cp.start()
            first.append(cp)
        passed = []
        for t in range(nt):
            for j, chip in enumerate(chips):
                cid = 2 * chip[0] + chip[1]
                copy(t, j, c, cid, (x, y, c)).wait_recv()
                cp = copy(t, 3 + j, c, cid, sibling)
                cp.start()
                passed.append(cp)
        for t in range(nt):
            for j, chip in enumerate(chips):
                copy(t, 3 + j, 1 - c, 2 * chip[0] + chip[1], (x, y, c)).wait_recv()
            copy(t, 6, 1 - c, me, (x, y, c)).wait_recv()
        for cp in first + passed:
            cp.wait_send()
        for cp in local:
            cp.wait()

    return pl.pallas_call(
        body, name=name, in_specs=[ANY] * nt, out_specs=[ANY] * nt,
        out_shape=[S((2,) + a.shape, a.dtype) for a in arrs],
        scratch_shapes=[pltpu.SemaphoreType.DMA((nt, 7)), pltpu.SemaphoreType.DMA((nt, 7)),
                        pltpu.SemaphoreType.DMA((nt,))],
    )(*arrs)


def gather_small(vec, *, name):
    R, C = vec.shape

    def body(v_ref, out_ref, send_sems, recv_sems):
        x, y, c = _pos()
        me = 4 * x + 2 * y + c
        out_ref[me] = v_ref[...]
        cps = []
        def peer(k):
            fx, fy, fc = (k >> 2) & 1, (k >> 1) & 1, k & 1
            return (1 - x if fx else x), (1 - y if fy else y), (1 - c if fc else c)

        for k in range(1, 8):
            cp = pltpu.make_async_remote_copy(src_ref=v_ref, dst_ref=out_ref.at[me], send_sem=send_sems.at[k - 1],
                                              recv_sem=recv_sems.at[k - 1], device_id=peer(k), device_id_type=MESH)
            cp.start()
            cps.append(cp)
        for k in range(1, 8):
            px, py, pc = peer(k)
            pltpu.make_async_remote_copy(src_ref=v_ref, dst_ref=out_ref.at[4 * px + 2 * py + pc],
                                         send_sem=send_sems.at[k - 1], recv_sem=recv_sems.at[k - 1],
                                         device_id=(px, py, pc), device_id_type=MESH).wait_recv()
        for cp in cps:
            cp.wait_send()

    return pl.pallas_call(
        body, name=name,
        in_specs=[pl.BlockSpec(memory_space=pltpu.VMEM)], out_specs=pl.BlockSpec(memory_space=pltpu.VMEM),
        out_shape=S((8, R, C), vec.dtype),
        scratch_shapes=[pltpu.SemaphoreType.DMA((7,)), pltpu.SemaphoreType.DMA((7,))],
    )(vec)


def _tile(n, prefs):
    for p in prefs:
        if n % p == 0:
            return p
    return n


def _lanes(g, reps):
    return jnp.tile(g.reshape(1, -1), (1, reps))


def _stream(a, d):
    return a.reshape(a.shape[0] // d, d * a.shape[1])


def _layer_fwd(x, p, e):
    T, D = x.shape
    F = p["w_gate"].shape[1]
    tm = _tile(T, (512, 256, 128))
    h, proj = rms_proj(x, p["norm1_g"], p["w_in"], tm=tm, tn=_tile(IN_W, (768,)), name="rms_proj")
    gains = (_lanes(p["a_q_g"], 8), _lanes(p["a_k_g"], 8), _lanes(p["c_q_g"], 8), _lanes(p["c_k_g"], 2))
    aq, ak, av, cq, ckk, cvv = prep_fwd(proj, e, gains, tm=_tile(T, (256, 128)), name="prep_fwd")
    ols = []
    for d in DILATIONS:
        o, l = band_attn_fwd(_stream(aq, d), _stream(ak, d), _stream(av, d), None, max_dist=A_DIST, group=1,
                             name=f"dil_attn_fwd_{d}")
        ols += [o.reshape(T, A_W), l.reshape(T, A_W)]
    out_a = dil_combine_fwd(ols, tm=tm, name="dil_combine_fwd")
    out_b = conv_fwd(proj, p["conv_w"], p["conv_b"], p["conv_ln_g"], p["conv_ln_b"], tb=tm, name="conv_fwd")
    sinks = jnp.repeat(p["c_sinks"].reshape(-1), HD).reshape(1, C_W)
    o_c, l_c = band_attn_fwd(cq, ckk, cvv, sinks, max_dist=C_DIST, group=4, name="swa_attn_fwd")
    mix = jnp.concatenate([out_a, out_b, o_c.astype(MXU)], axis=1)
    x1 = matmul_res(mix, p["w_out"], x, tm=tm, tn=_tile(D, (1024, 512, 256)), tk=D, name="out_proj")
    h2, gate, up, act = rms_swiglu(x1, p["norm2_g"], p["w_gate"], p["w_up"], tm=tm, tn=_tile(F, (512, 256, 128)),
                                   name="rms_swiglu")
    x2 = matmul_res(act, p["w_down"], x1, tm=tm, tn=_tile(D, (1024, 512, 256)), tk=_tile(F, (512, 256, 128)),
                    name="ffn_down")
    saved = dict(x=x, h=h, proj=proj, gains=gains, aq=aq, ak=ak, av=av, cq=cq, ckk=ckk, cvv=cvv, ols=ols, o_c=o_c,
                 l_c=l_c, sinks=sinks, mix=mix, x1=x1, h2=h2, gate=gate, up=up, act=act)
    return x2, saved


def _layer_bwd(dx2, dx2b, p, s, e):
    T, D = dx2.shape
    F = p["w_gate"].shape[1]
    tm = _tile(T, (512, 256, 128))
    tkT = _tile(T, (512, 256, 128))
    tF = _tile(F, (512, 256, 128))
    g = {}
    d_gate, d_up = nt_swiglu_bwd(dx2b, p["w_down"], s["gate"], s["up"], tm=tm, tn=tF, name="ffn_down_bwd")
    g["w_down"] = tn_matmul(s["act"], dx2b, tm=tF, tn=D, tk=tkT, name="grad_w_down")
    g["w_gate"] = tn_matmul(s["h2"], d_gate, tm=_tile(D, (1024, 512, 256)), tn=tF, tk=tkT, name="grad_w_gate")
    g["w_up"] = tn_matmul(s["h2"], d_up, tm=_tile(D, (1024, 512, 256)), tn=tF, tk=tkT, name="grad_w_up")
    dx1, dx1b, g["norm2_g"] = nt_rms_bwd([(d_gate, p["w_gate"]), (d_up, p["w_up"])], s["x1"], p["norm2_g"], dx2,
                                         tm=_tile(T, (256, 128)), tk=tF, name="ffn_in_bwd")
    dmix = nt_plain(dx1b, p["w_out"], tm=tm, tn=_tile(D, (1024, 512, 256)), name="out_proj_bwd")
    g["w_out"] = tn_matmul(s["mix"], dx1b, tm=_tile(D, (1024, 512, 256)), tn=_tile(D, (1024, 512, 256)), tk=tkT,
                           name="grad_w_out")
    dos = dil_combine_bwd(s["ols"], dmix, e, tm=tm, name="dil_combine_bwd")
    da = []
    for n, d in enumerate(DILATIONS):
        dq, dk, dv = band_attn_bwd(_stream(s["aq"], d), _stream(s["ak"], d), _stream(s["av"], d),
                                   _stream(s["ols"][2 * n + 1], d), _stream(dos[n], d), _stream(dos[3 + n], d),
                                   max_dist=A_DIST, group=1, name=f"dil_attn_bwd_{d}")
        da += [dq.reshape(T, A_W), dk.reshape(T, A_W), dv.reshape(T, A_W)]
    du, dgt, gw, gb, glg, glb = conv_bwd(s["proj"], dmix, p["conv_w"], p["conv_b"], p["conv_ln_g"], p["conv_ln_b"],
                                         tb=tm, name="conv_bwd")
    g["conv_w"], g["conv_b"], g["conv_ln_g"], g["conv_ln_b"] = gw[:CONV_K], gb, glg, glb
    do_c, dd_c, dsink = swa_pre_bwd(s["o_c"], s["l_c"], dmix, s["sinks"], e, tm=_tile(T, (256, 128)), name="swa_pre_bwd")
    g["c_sinks"] = dsink.reshape(-1, HD)[:, 0]
    dcq, dckk, dcvv = band_attn_bwd(s["cq"], s["ckk"], s["cvv"], s["l_c"], do_c, dd_c, max_dist=C_DIST, group=4,
                                    name="swa_attn_bwd")
    dproj, gaq, gak, gcq, gck = prep_bwd(s["proj"], e, s["gains"], da, (dcq, dckk, dcvv), (du, dgt),
                                         tm=_tile(T, (256, 128)), name="prep_bwd")
    g["a_q_g"] = gaq.reshape(-1, HD).sum(0)
    g["a_k_g"] = gak.reshape(-1, HD).sum(0)
    g["c_q_g"] = gcq.reshape(-1, HD).sum(0)
    g["c_k_g"] = gck.reshape(-1, HD).sum(0)
    g["w_in"] = tn_matmul(s["h"], dproj, tm=_tile(D, (1024, 512, 256)), tn=_tile(IN_W, (1280,)), tk=tkT, name="grad_w_in")
    dx, dxb, g["norm1_g"] = nt_rms_bwd([(dproj, p["w_in"])], s["x"], p["norm1_g"], dx1, tm=_tile(T, (256, 128)),
                                       tk=_tile(IN_W, (768,)), name="in_proj_bwd")
    return dx, dxb, g


BIG = ("w_in", "w_out", "w_gate", "w_up", "w_down")
COL_SHARDED = ("w_in", "w_gate", "w_up")
SMALL = ("norm1_g", "a_q_g", "a_k_g", "conv_w", "conv_b", "conv_ln_g", "conv_ln_b", "c_q_g", "c_k_g", "c_sinks", "norm2_g")


def _to_pieces(g, name):
    R, C = g.shape
    if name in COL_SHARDED:
        return g.reshape(2, R // 2, 4, C // 4).transpose(2, 0, 1, 3)
    return g.reshape(4, 2, R // 8, C)


def _from_gathered(w, name):
    _, _, r, c = w.shape
    if name in COL_SHARDED:
        return w.transpose(1, 2, 0, 3).reshape(2 * r, 4 * c)
    return w.reshape(8 * r, c)


def _pack(items, rows):
    flat = jnp.concatenate([a.reshape(-1).astype(f32) for a in items])
    return jnp.pad(flat, (0, rows * 128 - flat.shape[0])).reshape(rows, 128)


def _unpack(packed, shapes):
    flat = packed.reshape(-1)
    out, off = [], 0
    for shp in shapes:
        n = 1
        for d in shp:
            n *= d
        out.append(flat[off:off + n].reshape(shp))
        off += n
    return out


def kernel(x, norm1_g, w_in, a_q_g, a_k_g, conv_w, conv_b, conv_ln_g, conv_ln_b, c_q_g, c_k_g, c_sinks, w_out, norm2_g, w_gate, w_up, w_down, loss_target, m_norm1_g, m_w_in, m_a_q_g, m_a_k_g, m_conv_w, m_conv_b, m_conv_ln_g, m_conv_ln_b, m_c_q_g, m_c_k_g, m_c_sinks, m_w_out, m_norm2_g, m_w_gate, m_w_up, m_w_down, v_norm1_g, v_w_in, v_a_q_g, v_a_k_g, v_conv_w, v_conv_b, v_conv_ln_g, v_conv_ln_b, v_c_q_g, v_c_k_g, v_c_sinks, v_w_out, v_norm2_g, v_w_gate, v_w_up, v_w_down):
    W = dict(norm1_g=norm1_g, w_in=w_in, a_q_g=a_q_g, a_k_g=a_k_g, conv_w=conv_w, conv_b=conv_b, conv_ln_g=conv_ln_g,
             conv_ln_b=conv_ln_b, c_q_g=c_q_g, c_k_g=c_k_g, c_sinks=c_sinks, w_out=w_out, norm2_g=norm2_g, w_gate=w_gate,
             w_up=w_up, w_down=w_down)
    M = dict(norm1_g=m_norm1_g, w_in=m_w_in, a_q_g=m_a_q_g, a_k_g=m_a_k_g, conv_w=m_conv_w, conv_b=m_conv_b,
             conv_ln_g=m_conv_ln_g, conv_ln_b=m_conv_ln_b, c_q_g=m_c_q_g, c_k_g=m_c_k_g, c_sinks=m_c_sinks, w_out=m_w_out,
             norm2_g=m_norm2_g, w_gate=m_w_gate, w_up=m_w_up, w_down=m_w_down)
    V = dict(norm1_g=v_norm1_g, w_in=v_w_in, a_q_g=v_a_q_g, a_k_g=v_a_k_g, conv_w=v_conv_w, conv_b=v_conv_b,
             conv_ln_g=v_conv_ln_g, conv_ln_b=v_conv_ln_b, c_q_g=v_c_q_g, c_k_g=v_c_k_g, c_sinks=v_c_sinks, w_out=v_w_out,
             norm2_g=v_norm2_g, w_gate=v_w_gate, w_up=v_w_up, w_down=v_w_down)
    depth = norm1_g.shape[0]
    T, D = x.shape[1], x.shape[2]
    xs = x.reshape(T, D)
    chip = 2 * lax.axis_index("x") + lax.axis_index("y")
    e = _head_eye()

    shards = []
    for l in range(depth):
        for n in BIG:
            w = W[n][l]
            shards.append(w.astype(MXU).reshape(2, w.shape[0] // 2, w.shape[1]))
    shards.append(conv_w)
    gathered = gather_shards(shards, name="gather_weights")
    conv_full = gathered[-1].transpose(1, 2, 0, 3).reshape(depth, CONV_K, B_W)
    params = []
    for l in range(depth):
        p = {n: _from_gathered(gathered[l * len(BIG) + i], n) for i, n in enumerate(BIG)}
        p["conv_w"] = jnp.pad(conv_full[l], ((0, HALO - CONV_K), (0, 0)))
        for n in SMALL:
            if n != "conv_w":
                p[n] = W[n][l].reshape(1, -1)
        params.append(p)

    saved = []
    act = xs
    for l in range(depth):
        act, s = _layer_fwd(act, params[l], e)
        saved.append(s)
    dy, dyb, loss_part = loss_head(act, loss_target.reshape(T, D), tm=_tile(T, (512, 256, 128)), name="loss_head")
    grads = [None] * depth
    for l in reversed(range(depth)):
        dy, dyb, grads[l] = _layer_bwd(dy, dyb, params[l], saved[l], e)
    grad_x = dy.reshape(x.shape)

    c = lax.axis_index("c")
    pieces = [_to_pieces(grads[l][n], n) for l in range(depth) for n in BIG]
    mine = [lax.dynamic_index_in_dim(pc, c, axis=1, keepdims=False) for pc in pieces]
    theirs = [lax.dynamic_index_in_dim(pc, 1 - c, axis=1, keepdims=False) for pc in pieces]
    from_sibling = sibling_swap(theirs, name="grad_sibling_swap")
    chip_sums = [add_halves(a, b, tr=_tile(a.shape[1], (256, 176, 128, 64, 32, 16)), name="grad_chip_sum")
                 for a, b in zip(mine, from_sibling)]
    landed = exchange_pieces(chip_sums, name="grad_exchange")
    out = {}
    for i, n in enumerate(BIG):
        per_layer = [landed[l * len(BIG) + i] for l in range(depth)]
        r, cc = per_layer[0].shape[2], per_layer[0].shape[3]
        srcs = [jnp.stack([pl_[:, s].reshape(2 * r, cc) for pl_ in per_layer]) for s in range(4)]
        out[n] = adamw(W[n], M[n], V[n], srcs, tr=_tile(2 * r, (256, 176, 128, 64, 32, 16)), name="adamw_" + n)

    small_shapes = []
    items = []
    for l in range(depth):
        for n in SMALL:
            a = grads[l][n]
            if n == "conv_w":
                a = a.reshape(CONV_K, 4, B_W // 4).transpose(1, 0, 2)
            items.append(a)
            small_shapes.append(a.shape)
    items.append(loss_part[0, 0:1])
    small_shapes.append((1,))
    total = sum(int(jnp.size(a)) for a in items)
    rows = -(-total // 1024) * 8
    summed = sum8(gather_small(_pack(items, rows), name="gather_small"), name="sum_small")
    parts = _unpack(summed, small_shapes)
    loss = parts[-1][0]
    small_g = {n: [] for n in SMALL}
    for l in range(depth):
        for i, n in enumerate(SMALL):
            a = parts[l * len(SMALL) + i]
            if n == "conv_w":
                a = lax.dynamic_index_in_dim(a, chip, axis=0, keepdims=False)
            small_g[n].append(a.reshape(W[n].shape[1:]))
    sw = [W[n] for n in SMALL]
    sm = [M[n] for n in SMALL]
    sv = [V[n] for n in SMALL]
    sg = [jnp.stack(small_g[n]) for n in SMALL]
    tot2 = sum(int(jnp.size(a)) for a in sw)
    rows2 = -(-tot2 // 1024) * 8
    res = adamw(_pack(sw, rows2)[None], _pack(sm, rows2)[None], _pack(sv, rows2)[None], [_pack(sg, rows2)[None]],
                tr=rows2, name="adamw_small")
    shapes2 = [a.shape for a in sw]
    small_out = [_unpack(r[0], shapes2) for r in res]
    for i, n in enumerate(SMALL):
        out[n] = [small_out[k][i] for k in range(4)]

    order = ("norm1_g", "w_in", "a_q_g", "a_k_g", "conv_w", "conv_b", "conv_ln_g", "conv_ln_b", "c_q_g", "c_k_g",
             "c_sinks", "w_out", "norm2_g", "w_gate", "w_up", "w_down")
    return (loss, grad_x, *[out[n][0] for n in order], *[out[n][1] for n in order], *[out[n][2] for n in order],
            *[out[n][3] for n in order])
```

```python
import functools

import jax
import jax.numpy as jnp
from jax import lax
from jax.experimental import pallas as pl
from jax.experimental.pallas import tpu as pltpu

f32 = jnp.float32
MXU = jnp.bfloat16
S = jax.ShapeDtypeStruct
MESH = pl.DeviceIdType.MESH

EPS = 1e-6
NEG = -1e30
HD = 64
BLK = 128
A_W, B_W, C_W = 512, 512, 1024
KV_W = 128
IN_W = 3 * A_W + 2 * B_W + C_W + 2 * KV_W
CONV_K = 31
HALO = 32
DILATIONS = (1, 4, 16)
A_DIST, C_DIST = 128, 127
SCALE = HD ** -0.5
VMEM_LIMIT = 56 * 1024 * 1024

ADAM_LR, ADAM_B1, ADAM_B2, ADAM_EPS, ADAM_WD, ADAM_STEP = 0.001, 0.9, 0.999, 1e-08, 0.01, 10


def _cp(*sem):
    return pltpu.CompilerParams(dimension_semantics=sem, vmem_limit_bytes=VMEM_LIMIT)


def _nt(a, b):
    return lax.dot_general(a, b, (((1,), (1,)), ((), ())), preferred_element_type=f32)


def _tn(a, b):
    return lax.dot_general(a, b, (((0,), (0,)), ((), ())), preferred_element_type=f32)


def _nn(a, b):
    return jnp.dot(a, b, preferred_element_type=f32)


def _sigmoid(x):
    return 1.0 / (1.0 + jnp.exp(-x))


def _seg_sum(v, e_ref):
    hi = v.astype(jnp.bfloat16)
    lo = (v - hi.astype(f32)).astype(jnp.bfloat16)
    e = e_ref[...]
    return _nn(hi, e) + _nn(lo, e)


def _seg_sum128(v, e_ref):
    e = e_ref[0:128, 0:128]
    hi = v.astype(jnp.bfloat16)
    lo = (v - hi.astype(f32)).astype(jnp.bfloat16)
    return _nn(hi, e) + _nn(lo, e)


def _head_eye():
    r = lax.broadcasted_iota(jnp.int32, (512, 512), 0) // HD
    c = lax.broadcasted_iota(jnp.int32, (512, 512), 1) // HD
    return (r == c).astype(jnp.bfloat16)


def _rms_norm_rows(x_ref, g_ref, h_ref, tm):
    def chunk(c, carry):
        rows = pl.ds(c * BLK, BLK)
        xf = x_ref[rows, :]
        r = lax.rsqrt(jnp.mean(xf * xf, axis=-1, keepdims=True) + EPS)
        h_ref[rows, :] = (xf * r * g_ref[...]).astype(MXU)
        return carry
    lax.fori_loop(0, tm // BLK, chunk, 0)


def rms_proj(x, g, w, *, tm, tn, name):
    T, D = x.shape
    N = w.shape[1]

    def body(x_ref, g_ref, w_ref, h_ref, o_ref):
        @pl.when(pl.program_id(1) == 0)
        def _():
            _rms_norm_rows(x_ref, g_ref, h_ref, tm)
        o_ref[...] = _nn(h_ref[...], w_ref[...])

    return pl.pallas_call(
        body, name=name, grid=(T // tm, N // tn),
        in_specs=[pl.BlockSpec((tm, D), lambda i, j: (i, 0)), pl.BlockSpec((1, D), lambda i, j: (0, 0)),
                  pl.BlockSpec((D, tn), lambda i, j: (0, j))],
        out_specs=[pl.BlockSpec((tm, D), lambda i, j: (i, 0)), pl.BlockSpec((tm, tn), lambda i, j: (i, j))],
        out_shape=[S((T, D), MXU), S((T, N), f32)],
        compiler_params=_cp("arbitrary", "arbitrary"),
    )(x, g, w)


def rms_swiglu(x, g, wg, wu, *, tm, tn, name):
    T, D = x.shape
    N = wg.shape[1]

    def body(x_ref, g_ref, wg_ref, wu_ref, h_ref, gate_ref, up_ref, act_ref):
        @pl.when(pl.program_id(1) == 0)
        def _():
            _rms_norm_rows(x_ref, g_ref, h_ref, tm)
        h = h_ref[...]
        gate = _nn(h, wg_ref[...])
        up = _nn(h, wu_ref[...])
        gate_ref[...] = gate
        up_ref[...] = up
        act_ref[...] = (gate * _sigmoid(gate) * up).astype(MXU)

    wspec = pl.BlockSpec((D, tn), lambda i, j: (0, j))
    ospec = pl.BlockSpec((tm, tn), lambda i, j: (i, j))
    return pl.pallas_call(
        body, name=name, grid=(T // tm, N // tn),
        in_specs=[pl.BlockSpec((tm, D), lambda i, j: (i, 0)), pl.BlockSpec((1, D), lambda i, j: (0, 0)), wspec, wspec],
        out_specs=[pl.BlockSpec((tm, D), lambda i, j: (i, 0)), ospec, ospec, ospec],
        out_shape=[S((T, D), MXU), S((T, N), f32), S((T, N), f32), S((T, N), MXU)],
        compiler_params=_cp("arbitrary", "arbitrary"),
    )(x, g, wg, wu)


def matmul_res(a, w, res, *, tm, tn, name):
    T, K = a.shape
    N = w.shape[1]

    def body(a_ref, w_ref, r_ref, o_ref):
        o_ref[...] = r_ref[...] + _nn(a_ref[...], w_ref[...])

    return pl.pallas_call(
        body, name=name, grid=(T // tm, N // tn),
        in_specs=[pl.BlockSpec((tm, K), lambda i, j: (i, 0)), pl.BlockSpec((K, tn), lambda i, j: (0, j)),
                  pl.BlockSpec((tm, tn), lambda i, j: (i, j))],
        out_specs=pl.BlockSpec((tm, tn), lambda i, j: (i, j)),
        out_shape=S((T, N), f32),
        compiler_params=_cp("arbitrary", "arbitrary"),
    )(a, w, res)


def nt_plain(a, w, *, tm, tn, name):
    T, K = a.shape
    N = w.shape[0]

    def body(a_ref, w_ref, o_ref):
        o_ref[...] = _nt(a_ref[...], w_ref[...])

    return pl.pallas_call(
        body, name=name, grid=(T // tm, N // tn),
        in_specs=[pl.BlockSpec((tm, K), lambda i, j: (i, 0)), pl.BlockSpec((tn, K), lambda i, j: (j, 0))],
        out_specs=pl.BlockSpec((tm, tn), lambda i, j: (i, j)),
        out_shape=S((T, N), f32),
        compiler_params=_cp("arbitrary", "arbitrary"),
    )(a, w)


def nt_swiglu_bwd(dy, wd, gate, up, *, tm, tn, name):
    T, D = dy.shape
    F = wd.shape[0]

    def body(dy_ref, w_ref, g_ref, u_ref, dg_ref, du_ref):
        d_act = _nt(dy_ref[...], w_ref[...])
        g = g_ref[...]
        sg = _sigmoid(g)
        du_ref[...] = (d_act * (g * sg)).astype(MXU)
        dg_ref[...] = (d_act * u_ref[...] * (sg * (1.0 + g * (1.0 - sg)))).astype(MXU)

    blk = pl.BlockSpec((tm, tn), lambda i, j: (i, j))
    return pl.pallas_call(
        body, name=name, grid=(T // tm, F // tn),
        in_specs=[pl.BlockSpec((tm, D), lambda i, j: (i, 0)), pl.BlockSpec((tn, D), lambda i, j: (j, 0)), blk, blk],
        out_specs=[blk, blk],
        out_shape=[S((T, F), MXU), S((T, F), MXU)],
        compiler_params=_cp("arbitrary", "arbitrary"),
    )(dy, wd, gate, up)


def nt_rms_bwd(terms, x, g, dres, *, tm, tk, name):
    T, D = x.shape
    K = terms[0][0].shape[1]
    nk = K // tk
    nt = len(terms)
    ni = T // tm

    def body(*refs):
        a_refs = refs[0:2 * nt:2]
        w_refs = refs[1:2 * nt:2]
        x_ref, g_ref, r_ref, dx_ref, dxb_ref, dg_ref, acc_ref = refs[2 * nt:]
        i, k = pl.program_id(0), pl.program_id(1)
        part = _nt(a_refs[0][...], w_refs[0][...])
        for t in range(1, nt):
            part += _nt(a_refs[t][...], w_refs[t][...])

        @pl.when(k == 0)
        def _():
            acc_ref[...] = part

        @pl.when(k > 0)
        def _():
            acc_ref[...] += part

        @pl.when(k == nk - 1)
        def _():
            def chunk(c, dgain):
                rows = pl.ds(c * BLK, BLK)
                dh = acc_ref[rows, :]
                xf = x_ref[rows, :]
                r = lax.rsqrt(jnp.mean(xf * xf, axis=-1, keepdims=True) + EPS)
                y = xf * r
                dy = dh * g_ref[...]
                dx = r_ref[rows, :] + r * (dy - y * jnp.mean(dy * y, axis=-1, keepdims=True))
                dx_ref[rows, :] = dx
                dxb_ref[rows, :] = dx.astype(MXU)
                return dgain + jnp.sum(dh * y, axis=0, keepdims=True)

            dgain = lax.fori_loop(0, tm // BLK, chunk, jnp.zeros((1, D), f32))

            @pl.when(i == 0)
            def _():
                dg_ref[...] = dgain

            @pl.when(i > 0)
            def _():
                dg_ref[...] += dgain

    in_specs, args = [], []
    for a, w in terms:
        in_specs += [pl.BlockSpec((tm, tk), lambda i, k: (i, k)), pl.BlockSpec((D, tk), lambda i, k: (0, k))]
        args += [a, w]
    row = pl.BlockSpec((tm, D), lambda i, k: (i, 0))
    vec = pl.BlockSpec((1, D), lambda i, k: (0, 0))
    in_specs += [row, vec, row]
    return pl.pallas_call(
        body, name=name, grid=(ni, nk), in_specs=in_specs,
        out_specs=[row, row, vec],
        out_shape=[S((T, D), f32), S((T, D), MXU), S((1, D), f32)],
        scratch_shapes=[pltpu.VMEM((tm, D), f32)],
        compiler_params=_cp("arbitrary", "arbitrary"),
    )(*args, x, g, dres)


def tn_matmul(a, b, *, tm, tn, tk, name):
    T, M = a.shape
    N = b.shape[1]

    def body(a_ref, b_ref, o_ref):
        part = _tn(a_ref[...], b_ref[...])

        @pl.when(pl.program_id(2) == 0)
        def _():
            o_ref[...] = part

        @pl.when(pl.program_id(2) > 0)
        def _():
            o_ref[...] += part

    return pl.pallas_call(
        body, name=name, grid=(M // tm, N // tn, T // tk),
        in_specs=[pl.BlockSpec((tk, tm), lambda i, j, k: (k, i)), pl.BlockSpec((tk, tn), lambda i, j, k: (k, j))],
        out_specs=pl.BlockSpec((tm, tn), lambda i, j, k: (i, j)),
        out_shape=S((M, N), f32),
        compiler_params=_cp("arbitrary", "arbitrary", "arbitrary"),
    )(a, b)


def loss_head(y, target, *, tm, name):
    T, D = y.shape
    ni = T // tm

    def body(y_ref, t_ref, dy_ref, dyb_ref, l_ref, acc_ref):
        i = pl.program_id(0)
        e = y_ref[...] - t_ref[...]
        dy = e * (1.0 / D)
        dy_ref[...] = dy
        dyb_ref[...] = dy.astype(MXU)
        part = jnp.sum(e * e, axis=0, keepdims=True)

        @pl.when(i == 0)
        def _():
            acc_ref[...] = part

        @pl.when(i > 0)
        def _():
            acc_ref[...] += part

        @pl.when(i == ni - 1)
        def _():
            tot = jnp.sum(acc_ref[...], axis=1, keepdims=True) * (0.5 / D)
            l_ref[...] = jnp.broadcast_to(tot, (1, 128))

    row = pl.BlockSpec((tm, D), lambda i: (i, 0))
    return pl.pallas_call(
        body, name=name, grid=(ni,), in_specs=[row, row],
        out_specs=[row, row, pl.BlockSpec((1, 128), lambda i: (0, 0))],
        out_shape=[S((T, D), f32), S((T, D), MXU), S((1, 128), f32)],
        scratch_shapes=[pltpu.VMEM((1, D), f32)],
        compiler_params=_cp("arbitrary"),
    )(y, target)


def _qk_norm(v, gain, e_ref):
    r = lax.rsqrt(_seg_sum(v * v, e_ref) * (1.0 / HD) + EPS)
    return v * r * gain


def _dup_halves(pair):
    rolled = pltpu.roll(pair, HD, 1)
    lo = lax.broadcasted_iota(jnp.int32, pair.shape, 1) < HD
    return jnp.where(lo, pair, rolled), jnp.where(lo, rolled, pair)


def prep_fwd(proj, e, gains, *, tm, name):
    T = proj.shape[0]

    def body(p_ref, e_ref, gaq, gak, gcq, gck, aq, ak, av, cq, ckk, cvv):
        aq[...] = _qk_norm(p_ref[:, 0:512], gaq[...], e_ref)
        ak[...] = _qk_norm(p_ref[:, 512:1024], gak[...], e_ref)
        av[...] = p_ref[:, 1024:1536]
        cq[:, 0:512] = _qk_norm(p_ref[:, 2560:3072], gcq[...], e_ref).astype(MXU)
        cq[:, 512:1024] = _qk_norm(p_ref[:, 3072:3584], gcq[...], e_ref).astype(MXU)
        kraw = p_ref[:, 3584:3712]
        kn = kraw * lax.rsqrt(_seg_sum128(kraw * kraw, e_ref) * (1.0 / HD) + EPS) * gck[...]
        k0, k1 = _dup_halves(kn)
        ckk[:, 0:128] = k0.astype(MXU)
        ckk[:, 128:256] = k1.astype(MXU)
        v0, v1 = _dup_halves(p_ref[:, 3712:3840])
        cvv[:, 0:128] = v0.astype(MXU)
        cvv[:, 128:256] = v1.astype(MXU)

    def vec(n):
        return pl.BlockSpec((1, n), lambda i: (0, 0))

    def rows(n):
        return pl.BlockSpec((tm, n), lambda i: (i, 0))

    return pl.pallas_call(
        body, name=name, grid=(T // tm,),
        in_specs=[rows(IN_W), pl.BlockSpec((512, 512), lambda i: (0, 0)), vec(512), vec(512), vec(512), vec(128)],
        out_specs=[rows(512), rows(512), rows(512), rows(1024), rows(256), rows(256)],
        out_shape=[S((T, 512), f32)] * 3 + [S((T, 1024), MXU), S((T, 256), MXU), S((T, 256), MXU)],
        compiler_params=_cp("arbitrary"),
    )(proj, e, *gains)


def _band_mask(max_dist, shut):
    r = lax.broadcasted_iota(jnp.int32, (2 * BLK, 2 * BLK), 0) & (BLK - 1)
    c = lax.broadcasted_iota(jnp.int32, (2 * BLK, 2 * BLK), 1)
    prev = jnp.logical_and(c < BLK, c >= r + (BLK - max_dist) + shut)
    return jnp.logical_or(prev, jnp.logical_and(c >= BLK, c - BLK <= r))


def _prev_mask(max_dist, shut):
    r = lax.broadcasted_iota(jnp.int32, (2 * BLK, BLK), 0) & (BLK - 1)
    c = lax.broadcasted_iota(jnp.int32, (2 * BLK, BLK), 1)
    return c >= r + (BLK - max_dist) + shut


def _head_masks():
    lo = (lax.broadcasted_iota(jnp.int32, (BLK, BLK), 1) < HD).astype(f32)
    return lo.astype(MXU), (1.0 - lo).astype(MXU)


def _stack_heads(x, hm):
    return jnp.concatenate([x * hm[0], x * hm[1]], axis=0)


def _unstack_heads(y, lane_lo):
    return jnp.where(lane_lo, y[0:BLK], y[BLK:2 * BLK])


def _rows(ref, start, dil):
    if dil == 1:
        return ref[pl.ds(start, BLK), :]
    return ref[pl.ds(start, BLK, stride=dil), :]


def _set_rows(ref, start, dil, val):
    if dil == 1:
        ref[pl.ds(start, BLK), :] = val
    else:
        ref[pl.ds(start, BLK, stride=dil), :] = val


def _attn_geometry(T, dil):
    span = BLK * dil
    n = max(1, 512 // span)
    return span, n, T // (span * n)


def band_attn_fwd(q, k, v, sinks, *, dil, max_dist, group, name):
    T = q.shape[0]
    P = q.shape[1] // BLK
    span, n, nb = _attn_geometry(T, dil)

    def body(*refs):
        if sinks is None:
            q_ref, kc_ref, kp_ref, vc_ref, vp_ref, o_ref, l_ref = refs
        else:
            q_ref, kc_ref, kp_ref, vc_ref, vp_ref, s_ref, o_ref, l_ref = refs
        b = pl.program_id(0)
        mask = _band_mask(max_dist, 0)
        mask0 = _band_mask(max_dist, jnp.where(b > 0, 0, BLK + 1))
        lane_lo = lax.broadcasted_iota(jnp.int32, (BLK, BLK), 1) < HD
        hm = _head_masks()
        if sinks is not None:
            row_lo = lax.broadcasted_iota(jnp.int32, (1, BLK), 1) < HD
            sk0 = jnp.max(jnp.where(row_lo, s_ref[...], NEG), axis=1, keepdims=True)
            sk1 = jnp.max(jnp.where(row_lo, NEG, s_ref[...]), axis=1, keepdims=True)
            sk = jnp.where(lax.broadcasted_iota(jnp.int32, (2 * BLK, 1), 0) < BLK, sk0, sk1)

        def load(r, sub):
            at = r + sub * span
            kc, vc = _rows(kc_ref, at, dil).astype(MXU), _rows(vc_ref, at, dil).astype(MXU)
            if sub == 0:
                kp, vp = _rows(kp_ref, r, dil).astype(MXU), _rows(vp_ref, r, dil).astype(MXU)
            else:
                kp, vp = _rows(kc_ref, at - span, dil).astype(MXU), _rows(vc_ref, at - span, dil).astype(MXU)
            qst = _stack_heads(_rows(q_ref, at, dil).astype(MXU), hm)
            return (qst, jnp.concatenate([kp, kc], axis=0), jnp.concatenate([vp, vc], axis=0),
                    mask0 if sub == 0 else mask, at)

        def attend(items):
            ss = [jnp.where(m_, _nt(qst, kcat) * SCALE, NEG) for qst, kcat, _, m_, _ in items]
            ms = [jnp.max(s, axis=1, keepdims=True) for s in ss]
            if sinks is not None:
                ms = [jnp.maximum(m, sk) for m in ms]
            ps = [jnp.exp(s - m) for s, m in zip(ss, ms)]
            dens = [jnp.sum(p_, axis=1, keepdims=True) for p_ in ps]
            if sinks is not None:
                dens = [d + jnp.exp(sk - m) for d, m in zip(dens, ms)]
            outs = [_nn(p_.astype(MXU), it[2]) / d for p_, it, d in zip(ps, items, dens)]
            for it, o, m, d in zip(items, outs, ms, dens):
                lse = m + jnp.log(d)
                _set_rows(o_ref, it[4], dil, _unstack_heads(o, lane_lo))
                _set_rows(l_ref, it[4], dil, jnp.where(lane_lo, lse[0:BLK], lse[BLK:2 * BLK]))

        if dil * n <= 4:
            work = [(r, sub) for r in range(dil) for sub in range(n)]
            for g in range(0, len(work), 2):
                attend([load(*w) for w in work[g:g + 2]])
        else:
            def two_streams(i, carry):
                attend([load(2 * i, 0), load(2 * i + 1, 0)])
                return carry
            lax.fori_loop(0, dil // 2, two_streams, 0)

    rows_per_step = span * n
    qspec = pl.BlockSpec((rows_per_step, BLK), lambda b, p: (b, p))
    cur = pl.BlockSpec((rows_per_step, BLK), lambda b, p: (b, p // group))
    prev = pl.BlockSpec((span, BLK), lambda b, p: (jnp.maximum(b * n - 1, 0), p // group))
    in_specs = [qspec, cur, prev, cur, prev]
    args = [q, k, k, v, v]
    if sinks is not None:
        in_specs.append(pl.BlockSpec((1, BLK), lambda b, p: (0, p)))
        args.append(sinks)
    return pl.pallas_call(
        body, name=name, grid=(nb, P), in_specs=in_specs, out_specs=[qspec, qspec],
        out_shape=[S(q.shape, f32), S(q.shape, f32)],
        compiler_params=_cp("arbitrary", "arbitrary"),
    )(*args)


def band_attn_bwd(q, k, v, lse, do, dd, *, dil, max_dist, group, name):
    T = q.shape[0]
    P = q.shape[1] // BLK
    span, n, nb = _attn_geometry(T, dil)
    assert group == 1 or dil == 1

    def body(q_ref, qn_ref, do_ref, don_ref, l_ref, ln_ref, d_ref, dn_ref, kc_ref, kp_ref, vc_ref, vp_ref,
             dq_ref, dk_ref, dv_ref):
        b, p = pl.program_id(0), pl.program_id(1)
        mask = _band_mask(max_dist, 0)
        mask0 = _band_mask(max_dist, jnp.where(b > 0, 0, BLK + 1))
        tail = _prev_mask(max_dist, jnp.where(b < nb - 1, 0, BLK + 1))
        lane_lo = lax.broadcasted_iota(jnp.int32, (BLK, BLK), 1) < HD
        hm = _head_masks()
        own_lanes = (lax.broadcasted_iota(jnp.int32, (2 * BLK, BLK), 1) < HD) == (
            lax.broadcasted_iota(jnp.int32, (2 * BLK, BLK), 0) < BLK)

        def per_row(x):
            return jnp.max(jnp.where(own_lanes, jnp.concatenate([x, x], axis=0), NEG), axis=1, keepdims=True)

        def q_side(refs, at):
            q_r, do_r, l_r, d_r = refs
            return (_stack_heads(_rows(q_r, at, dil).astype(MXU), hm), _stack_heads(_rows(do_r, at, dil).astype(MXU), hm),
                    per_row(_rows(l_r, at, dil)), per_row(_rows(d_r, at, dil)))

        def kv(ref, at):
            return _rows(ref, at, dil).astype(MXU)

        first = p % group == 0

        def put_kv(ref, at, val):
            if group == 1:
                _set_rows(ref, at, dil, val)
            else:
                @pl.when(first)
                def _():
                    ref[pl.ds(at, BLK), :] = val

                @pl.when(jnp.logical_not(first))
                def _():
                    ref[pl.ds(at, BLK), :] += val

        def stream(r):
            dks, dvs = [None] * n, [None] * n
            for sub in range(n):
                at = r + sub * span
                qst, dost, lrow, drow = q_side((q_ref, do_ref, l_ref, d_ref), at)
                if sub == 0:
                    kp, vp, m_ = kv(kp_ref, r), kv(vp_ref, r), mask0
                else:
                    kp, vp, m_ = kv(kc_ref, at - span), kv(vc_ref, at - span), mask
                kcat = jnp.concatenate([kp, kv(kc_ref, at)], axis=0)
                vcat = jnp.concatenate([vp, kv(vc_ref, at)], axis=0)
                pr = jnp.where(m_, jnp.exp(_nt(qst, kcat) * SCALE - lrow), 0.0)
                ds = (pr * (_nt(dost, vcat) - drow) * SCALE).astype(MXU)
                prb = pr.astype(MXU)
                _set_rows(dq_ref, at, dil, _unstack_heads(_nn(ds, kcat), lane_lo))
                if sub == 0:
                    dks[0] = _tn(ds[:, BLK:], qst)
                    dvs[0] = _tn(prb[:, BLK:], dost)
                else:
                    dkk, dvv = _tn(ds, qst), _tn(prb, dost)
                    dks[sub - 1] += dkk[0:BLK]
                    dvs[sub - 1] += dvv[0:BLK]
                    dks[sub], dvs[sub] = dkk[BLK:], dvv[BLK:]
            at = r + (n - 1) * span
            qst, dost, lrow, drow = q_side((qn_ref, don_ref, ln_ref, dn_ref), r)
            pr = jnp.where(tail, jnp.exp(_nt(qst, kv(kc_ref, at)) * SCALE - lrow), 0.0)
            ds = (pr * (_nt(dost, kv(vc_ref, at)) - drow) * SCALE).astype(MXU)
            dks[n - 1] += _tn(ds, qst)
            dvs[n - 1] += _tn(pr.astype(MXU), dost)
            for sub in range(n):
                put_kv(dk_ref, r + sub * span, dks[sub])
                put_kv(dv_ref, r + sub * span, dvs[sub])

        if dil <= 4:
            for r in range(dil):
                stream(r)
        else:
            def one_stream(r, carry):
                stream(r)
                return carry
            lax.fori_loop(0, dil, one_stream, 0)

    rows_per_step = span * n
    qspec = pl.BlockSpec((rows_per_step, BLK), lambda b, p: (b, p))
    qnext = pl.BlockSpec((span, BLK), lambda b, p: (jnp.minimum((b + 1) * n, T // span - 1), p))
    cur = pl.BlockSpec((rows_per_step, BLK), lambda b, p: (b, p // group))
    prev = pl.BlockSpec((span, BLK), lambda b, p: (jnp.maximum(b * n - 1, 0), p // group))
    return pl.pallas_call(
        body, name=name, grid=(nb, P),
        in_specs=[qspec, qnext, qspec, qnext, qspec, qnext, qspec, qnext, cur, prev, cur, prev],
        out_specs=[qspec, cur, cur],
        out_shape=[S(q.shape, f32), S(k.shape, f32), S(k.shape, f32)],
        compiler_params=_cp("arbitrary", "arbitrary"),
    )(q, q, do, do, lse, lse, dd, dd, k, k, v, v)


def dil_combine_fwd(ols, *, tm, name):
    T = ols[0].shape[0]

    def body(o1, l1, o2, l2, o3, l3, out_ref):
        a, b, c = l1[...], l2[...], l3[...]
        m = jnp.maximum(jnp.maximum(a, b), c)
        ea, eb, ec = jnp.exp(a - m), jnp.exp(b - m), jnp.exp(c - m)
        out = (ea * o1[...] + eb * o2[...] + ec * o3[...]) / (ea + eb + ec)
        out_ref[...] = out.astype(MXU)

    row = pl.BlockSpec((tm, 512), lambda i: (i, 0))
    return pl.pallas_call(body, name=name, grid=(T // tm,), in_specs=[row] * 6, out_specs=row,
                          out_shape=S((T, 512), MXU), compiler_params=_cp("arbitrary"))(*ols)


def dil_combine_bwd(ols, dmix, e, *, tm, name):
    T = ols[0].shape[0]

    def body(o1, l1, o2, l2, o3, l3, d_ref, e_ref, do1, do2, do3, dd1, dd2, dd3):
        a, b, c = l1[...], l2[...], l3[...]
        m = jnp.maximum(jnp.maximum(a, b), c)
        ea, eb, ec = jnp.exp(a - m), jnp.exp(b - m), jnp.exp(c - m)
        inv = 1.0 / (ea + eb + ec)
        wa, wb, wc = ea * inv, eb * inv, ec * inv
        dout = d_ref[...]
        gbar = _seg_sum(dout * (wa * o1[...] + wb * o2[...] + wc * o3[...]), e_ref)
        do1[...] = wa * dout
        do2[...] = wb * dout
        do3[...] = wc * dout
        dd1[...] = wa * gbar
        dd2[...] = wb * gbar
        dd3[...] = wc * gbar

    row = pl.BlockSpec((tm, 512), lambda i: (i, 0))
    return pl.pallas_call(
        body, name=name, grid=(T // tm,),
        in_specs=[row] * 6 + [row, pl.BlockSpec((512, 512), lambda i: (0, 0))],
        out_specs=[row] * 6,
        out_shape=[S((T, 512), f32)] * 6,
        compiler_params=_cp("arbitrary"),
    )(*ols, dmix, e)


def swa_pre_bwd(o, lse, dmix, sinks, e, *, tm, name):
    T = o.shape[0]
    ni = T // tm

    def body(o_ref, l_ref, d_ref, s_ref, e_ref, do_ref, dd_ref, ds_ref):
        i = pl.program_id(0)
        dout = d_ref[...]
        do_ref[...] = dout.astype(MXU)
        prod = dout * o_ref[...]
        dd = jnp.concatenate([_seg_sum(prod[:, 0:512], e_ref), _seg_sum(prod[:, 512:1024], e_ref)], axis=1)
        dd_ref[...] = dd
        part = -jnp.sum(jnp.exp(s_ref[...] - l_ref[...]) * dd, axis=0, keepdims=True)

        @pl.when(i == 0)
        def _():
            ds_ref[...] = part

        @pl.when(i > 0)
        def _():
            ds_ref[...] += part

    row = pl.BlockSpec((tm, 1024), lambda i: (i, 0))
    vec = pl.BlockSpec((1, 1024), lambda i: (0, 0))
    return pl.pallas_call(
        body, name=name, grid=(ni,),
        in_specs=[row, row, pl.BlockSpec((tm, 1024), lambda i: (i, 1)), vec, pl.BlockSpec((512, 512), lambda i: (0, 0))],
        out_specs=[row, row, vec],
        out_shape=[S((T, 1024), MXU), S((T, 1024), f32), S((1, 1024), f32)],
        compiler_params=_cp("arbitrary"),
    )(o, lse, dmix, sinks, e)


def _conv_taps(buf_ref, w_ref, start, rows):
    acc = buf_ref[pl.ds(start, rows), :] * w_ref[pl.ds(0, 1), :]
    for j in range(1, CONV_K):
        acc += buf_ref[pl.ds(start + j, rows), :] * w_ref[pl.ds(j, 1), :]
    return acc


def conv_fwd(proj, w, b, ln_g, ln_b, *, tb, name):
    T = proj.shape[0]
    hb = tb // HALO

    def body(u_ref, g_ref, up_ref, gp_ref, w_ref, b_ref, lg_ref, lb_ref, o_ref, hbuf):
        i = pl.program_id(0)
        hprev = up_ref[...] * _sigmoid(gp_ref[...])
        hbuf[0:HALO, :] = hprev * jnp.where(i > 0, 1.0, 0.0)
        hbuf[HALO:HALO + tb, :] = u_ref[...] * _sigmoid(g_ref[...])
        y = _conv_taps(hbuf, w_ref, HALO - (CONV_K - 1), tb) + b_ref[...]
        mu = jnp.mean(y, axis=-1, keepdims=True)
        yc = y - mu
        var = jnp.mean(yc * yc, axis=-1, keepdims=True)
        z = yc * lax.rsqrt(var + EPS) * lg_ref[...] + lb_ref[...]
        o_ref[...] = (z * _sigmoid(z)).astype(MXU)

    vec = pl.BlockSpec((1, 512), lambda i: (0, 0))
    return pl.pallas_call(
        body, name=name, grid=(T // tb,),
        in_specs=[pl.BlockSpec((tb, 512), lambda i: (i, 3)), pl.BlockSpec((tb, 512), lambda i: (i, 4)),
                  pl.BlockSpec((HALO, 512), lambda i: (jnp.maximum(i * hb - 1, 0), 3)),
                  pl.BlockSpec((HALO, 512), lambda i: (jnp.maximum(i * hb - 1, 0), 4)),
                  pl.BlockSpec((HALO, 512), lambda i: (0, 0)), vec, vec, vec],
        out_specs=pl.BlockSpec((tb, 512), lambda i: (i, 0)),
        out_shape=S((T, 512), MXU),
        scratch_shapes=[pltpu.VMEM((tb + HALO, 512), f32)],
        compiler_params=_cp("arbitrary"),
    )(proj, proj, proj, proj, w, b, ln_g, ln_b)


def conv_bwd(proj, dmix, w, b, ln_g, ln_b, *, tb, name):
    T = proj.shape[0]
    hb = tb // HALO
    ni = T // tb
    last_h = T // HALO - 1
    ext = tb + HALO

    def body(u_ref, g_ref, up_ref, gp_ref, un_ref, gn_ref, d_ref, dn_ref, w_ref, b_ref, lg_ref, lb_ref,
             du_ref, dg_ref, dw_ref, db_ref, dlg_ref, dlb_ref, hbuf, dybuf):
        i = pl.program_id(0)
        hbuf[0:HALO, :] = up_ref[...] * _sigmoid(gp_ref[...]) * jnp.where(i > 0, 1.0, 0.0)
        u = u_ref[...]
        sg = _sigmoid(g_ref[...])
        hbuf[HALO:HALO + tb, :] = u * sg
        hbuf[HALO + tb:HALO + ext, :] = un_ref[...] * _sigmoid(gn_ref[...])
        y = _conv_taps(hbuf, w_ref, HALO - (CONV_K - 1), ext) + b_ref[...]
        mu = jnp.mean(y, axis=-1, keepdims=True)
        yc = y - mu
        rstd = lax.rsqrt(jnp.mean(yc * yc, axis=-1, keepdims=True) + EPS)
        yn = yc * rstd
        z = yn * lg_ref[...] + lb_ref[...]
        sz = _sigmoid(z)
        row = lax.broadcasted_iota(jnp.int32, (ext, 1), 0)
        own = row < tb
        keep = row < jnp.where(i < ni - 1, ext, tb)
        dout = jnp.concatenate([d_ref[...], dn_ref[...]], axis=0)
        dz = jnp.where(keep, dout * (sz * (1.0 + z * (1.0 - sz))), 0.0)
        dyn = dz * lg_ref[...]
        dy = rstd * (dyn - jnp.mean(dyn, axis=-1, keepdims=True) - yn * jnp.mean(dyn * yn, axis=-1, keepdims=True))
        dybuf[...] = dy
        dz_own = jnp.where(own, dz, 0.0)
        dlg = jnp.sum(dz_own * yn, axis=0, keepdims=True)
        dlb = jnp.sum(dz_own, axis=0, keepdims=True)
        dy_own = dybuf[0:tb, :]
        dbias = jnp.sum(dy_own, axis=0, keepdims=True)
        dh = dybuf[pl.ds(CONV_K - 1, tb), :] * w_ref[pl.ds(0, 1), :]
        for j in range(1, CONV_K):
            dh += dybuf[pl.ds(CONV_K - 1 - j, tb), :] * w_ref[pl.ds(j, 1), :]
        du_ref[...] = (dh * sg).astype(MXU)
        dg_ref[...] = (dh * u * sg * (1.0 - sg)).astype(MXU)
        taps = [jnp.sum(dy_own * hbuf[pl.ds(HALO - (CONV_K - 1) + j, tb), :], axis=0, keepdims=True)
                for j in range(CONV_K)]
        taps.append(jnp.zeros((1, 512), f32))
        dwt = jnp.concatenate(taps, axis=0)

        @pl.when(i == 0)
        def _():
            dw_ref[...] = dwt
            db_ref[...] = dbias
            dlg_ref[...] = dlg
            dlb_ref[...] = dlb

        @pl.when(i > 0)
        def _():
            dw_ref[...] += dwt
            db_ref[...] += dbias
            dlg_ref[...] += dlg
            dlb_ref[...] += dlb

    vec = pl.BlockSpec((1, 512), lambda i: (0, 0))
    wspec = pl.BlockSpec((HALO, 512), lambda i: (0, 0))

    def halo_prev(col):
        return pl.BlockSpec((HALO, 512), lambda i: (jnp.maximum(i * hb - 1, 0), col))

    def halo_next(col):
        return pl.BlockSpec((HALO, 512), lambda i: (jnp.minimum((i + 1) * hb, last_h), col))

    row = pl.BlockSpec((tb, 512), lambda i: (i, 0))
    return pl.pallas_call(
        body, name=name, grid=(ni,),
        in_specs=[pl.BlockSpec((tb, 512), lambda i: (i, 3)), pl.BlockSpec((tb, 512), lambda i: (i, 4)),
                  halo_prev(3), halo_prev(4), halo_next(3), halo_next(4),
                  pl.BlockSpec((tb, 512), lambda i: (i, 1)), halo_next(1), wspec, vec, vec, vec],
        out_specs=[row, row, wspec, vec, vec, vec],
        out_shape=[S((T, 512), MXU), S((T, 512), MXU), S((HALO, 512), f32)] + [S((1, 512), f32)] * 3,
        scratch_shapes=[pltpu.VMEM((tb + 2 * HALO, 512), f32), pltpu.VMEM((ext, 512), f32)],
        compiler_params=_cp("arbitrary"),
    )(proj, proj, proj, proj, proj, proj, dmix, dmix, w, b, ln_g, ln_b)


def _qk_norm_bwd(v, gain, dout, e_ref):
    r = lax.rsqrt(_seg_sum(v * v, e_ref) * (1.0 / HD) + EPS)
    y = v * r
    dgain = jnp.sum(dout * y, axis=0, keepdims=True)
    dy = dout * gain
    dv = r * (dy - y * (_seg_sum(dy * y, e_ref) * (1.0 / HD)))
    return dv, dgain


def prep_bwd(proj, e, gains, da, dc, dconv, *, tm, name):
    T = proj.shape[0]

    def body(*refs):
        p_ref, e_ref, gaq, gak, gcq, gck = refs[0:6]
        a_refs = refs[6:15]
        dcq, dckk, dcvv, du, dgt = refs[15:20]
        dp, gaq_o, gak_o, gcq_o, gck_o = refs[20:]
        i = pl.program_id(0)
        dq = a_refs[0][...] + a_refs[3][...] + a_refs[6][...]
        dk = a_refs[1][...] + a_refs[4][...] + a_refs[7][...]
        dv = a_refs[2][...] + a_refs[5][...] + a_refs[8][...]
        d, g_aq = _qk_norm_bwd(p_ref[:, 0:512], gaq[...], dq, e_ref)
        dp[:, 0:512] = d.astype(MXU)
        d, g_ak = _qk_norm_bwd(p_ref[:, 512:1024], gak[...], dk, e_ref)
        dp[:, 512:1024] = d.astype(MXU)
        dp[:, 1024:1536] = dv.astype(MXU)
        dp[:, 1536:2048] = du[...]
        dp[:, 2048:2560] = dgt[...]
        d, g_cq0 = _qk_norm_bwd(p_ref[:, 2560:3072], gcq[...], dcq[:, 0:512], e_ref)
        dp[:, 2560:3072] = d.astype(MXU)
        d, g_cq1 = _qk_norm_bwd(p_ref[:, 3072:3584], gcq[...], dcq[:, 512:1024], e_ref)
        dp[:, 3072:3584] = d.astype(MXU)
        lo = lax.broadcasted_iota(jnp.int32, (tm, 128), 1) < HD

        def fold(ref):
            g0, g1 = ref[:, 0:128], ref[:, 128:256]
            s0 = g0 + pltpu.roll(g0, HD, 1)
            s1 = g1 + pltpu.roll(g1, HD, 1)
            return jnp.where(lo, s0, s1)

        dkn = fold(dckk)
        kraw = p_ref[:, 3584:3712]
        r = lax.rsqrt(_seg_sum128(kraw * kraw, e_ref) * (1.0 / HD) + EPS)
        y = kraw * r
        g_ck = jnp.sum(dkn * y, axis=0, keepdims=True)
        dy = dkn * gck[...]
        dp[:, 3584:3712] = (r * (dy - y * (_seg_sum128(dy * y, e_ref) * (1.0 / HD)))).astype(MXU)
        dp[:, 3712:3840] = fold(dcvv).astype(MXU)
        g_cq = jnp.concatenate([g_cq0, g_cq1], axis=1)

        @pl.when(i == 0)
        def _():
            gaq_o[...] = g_aq
            gak_o[...] = g_ak
            gcq_o[...] = g_cq
            gck_o[...] = g_ck

        @pl.when(i > 0)
        def _():
            gaq_o[...] += g_aq
            gak_o[...] += g_ak
            gcq_o[...] += g_cq
            gck_o[...] += g_ck

    def vec(n):
        return pl.BlockSpec((1, n), lambda i: (0, 0))

    def rows(n):
        return pl.BlockSpec((tm, n), lambda i: (i, 0))

    return pl.pallas_call(
        body, name=name, grid=(T // tm,),
        in_specs=[rows(IN_W), pl.BlockSpec((512, 512), lambda i: (0, 0)), vec(512), vec(512), vec(512), vec(128)]
        + [rows(512)] * 9 + [rows(1024), rows(256), rows(256), rows(512), rows(512)],
        out_specs=[rows(IN_W), vec(512), vec(512), vec(1024), vec(128)],
        out_shape=[S((T, IN_W), MXU), S((1, 512), f32), S((1, 512), f32), S((1, 1024), f32), S((1, 128), f32)],
        compiler_params=_cp("arbitrary"),
    )(proj, e, *gains, *da, *dc, *dconv)


def adamw(w, m, v, pieces, *, tr, name):
    n, R, C = w.shape
    c1 = 1.0 - ADAM_B1 ** ADAM_STEP
    c2 = 1.0 - ADAM_B2 ** ADAM_STEP
    npc = len(pieces)

    def body(*refs):
        w_ref, m_ref, v_ref = refs[0:3]
        p_refs = refs[3:3 + npc]
        g_ref, d_ref, mo_ref, vo_ref = refs[3 + npc:]
        g = p_refs[0][...].astype(f32)
        for p in p_refs[1:]:
            g = g + p[...].astype(f32)
        mn = ADAM_B1 * m_ref[...] + (1.0 - ADAM_B1) * g
        vn = ADAM_B2 * v_ref[...] + (1.0 - ADAM_B2) * (g * g)
        g_ref[...] = g
        mo_ref[...] = mn
        vo_ref[...] = vn
        d_ref[...] = -ADAM_LR * ((mn / c1) / (jnp.sqrt(vn / c2) + ADAM_EPS) + ADAM_WD * w_ref[...])

    blk = pl.BlockSpec((None, tr, C), lambda l, i: (l, i, 0))
    return pl.pallas_call(
        body, name=name, grid=(n, R // tr), in_specs=[blk] * (3 + npc), out_specs=[blk] * 4,
        out_shape=[S(w.shape, f32)] * 4, compiler_params=_cp("arbitrary", "arbitrary"),
    )(w, m, v, *pieces)


def add_halves(mine, other, *, tr, name):
    n, R, C = mine.shape

    def body(a_ref, b_ref, o_ref):
        o_ref[...] = (a_ref[...] + b_ref[...]).astype(jnp.bfloat16)

    blk = pl.BlockSpec((None, tr, C), lambda l, i: (l, i, 0))
    return pl.pallas_call(body, name=name, grid=(n, R // tr), in_specs=[blk, blk], out_specs=blk,
                          out_shape=S(mine.shape, jnp.bfloat16), compiler_params=_cp("arbitrary", "arbitrary"))(mine, other)


def sum8(parts, *, name):
    _, R, C = parts.shape

    def body(p_ref, o_ref):
        acc = p_ref[0]
        for d in range(1, 8):
            acc = acc + p_ref[d]
        o_ref[...] = acc

    return pl.pallas_call(body, name=name, out_shape=S((R, C), f32))(parts)


def _pos():
    return lax.axis_index("x"), lax.axis_index("y"), lax.axis_index("c")


def _other_chips(x, y):
    return [(1 - x, y), (x, 1 - y), (1 - x, 1 - y)]


ANY = pl.BlockSpec(memory_space=pl.ANY)


def gather_shards(shards, *, name):
    nt = len(shards)

    def body(*refs):
        in_refs = refs[:nt]
        out_refs = refs[nt:2 * nt]
        send_sems, recv_sems, local_sems = refs[2 * nt:]
        x, y, c = _pos()
        sibling = (x, y, 1 - c)
        chips = _other_chips(x, y)
        me = 2 * x + y

        def copy(t, k, chip, half, to, src=None):
            dst = out_refs[t].at[2 * chip[0] + chip[1], half]
            return pltpu.make_async_remote_copy(
                src_ref=dst if src is None else src, dst_ref=dst,
                send_sem=send_sems.at[t, k], recv_sem=recv_sems.at[t, k], device_id=to, device_id_type=MESH)

        local = [pltpu.make_async_copy(in_refs[t], out_refs[t].at[me], local_sems.at[t]) for t in range(nt)]
        for cp in local:
            cp.start()
        first = []
        for t in range(nt):
            for j, chip in enumerate(chips):
                cp = copy(t, j, (x, y), c, (*chip, c), src=in_refs[t].at[c])
                cp.start()
                first.append(cp)
        passed = []
        for t in range(nt):
            for j, chip in enumerate(chips):
                copy(t, j, chip, c, (x, y, c)).wait_recv()
                cp = copy(t, 3 + j, chip, c, sibling)
                cp.start()
                passed.append(cp)
        for t in range(nt):
            for j, chip in enumerate(chips):
                copy(t, 3 + j, chip, 1 - c, (x, y, c)).wait_recv()
        for cp in first + passed:
            cp.wait_send()
        for cp in local:
            cp.wait()

    return pl.pallas_call(
        body, name=name, in_specs=[ANY] * nt, out_specs=[ANY] * nt,
        out_shape=[S((4,) + s.shape, s.dtype) for s in shards],
        scratch_shapes=[pltpu.SemaphoreType.DMA((nt, 6)), pltpu.SemaphoreType.DMA((nt, 6)),
                        pltpu.SemaphoreType.DMA((nt,))],
    )(*shards)


def sibling_swap(arrs, *, name):
    nt = len(arrs)

    def body(*refs):
        in_refs = refs[:nt]
        out_refs = refs[nt:2 * nt]
        send_sems, recv_sems = refs[2 * nt:]
        x, y, c = _pos()
        cps = [pltpu.make_async_remote_copy(src_ref=in_refs[t], dst_ref=out_refs[t], send_sem=send_sems.at[t],
                                            recv_sem=recv_sems.at[t], device_id=(x, y, 1 - c), device_id_type=MESH)
               for t in range(nt)]
        for cp in cps:
            cp.start()
        for cp in cps:
            cp.wait()

    return pl.pallas_call(
        body, name=name, in_specs=[ANY] * nt, out_specs=[ANY] * nt,
        out_shape=[S(a.shape, a.dtype) for a in arrs],
        scratch_shapes=[pltpu.SemaphoreType.DMA((nt,)), pltpu.SemaphoreType.DMA((nt,))],
    )(*arrs)


def exchange_pieces(arrs, *, name):
    nt = len(arrs)

    def body(*refs):
        in_refs = refs[:nt]
        out_refs = refs[nt:2 * nt]
        send_sems, recv_sems, local_sems = refs[2 * nt:]
        x, y, c = _pos()
        sibling = (x, y, 1 - c)
        chips = _other_chips(x, y)
        me = 2 * x + y

        def copy(t, k, half, src_chip, to, src=None):
            dst = out_refs[t].at[half, src_chip]
            return pltpu.make_async_remote_copy(
                src_ref=dst if src is None else src, dst_ref=dst,
                send_sem=send_sems.at[t, k], recv_sem=recv_sems.at[t, k], device_id=to, device_id_type=MESH)

        local = [pltpu.make_async_copy(in_refs[t].at[me], out_refs[t].at[c, me], local_sems.at[t]) for t in range(nt)]
        for cp in local:
            cp.start()
        first = []
        for t in range(nt):
            for j, chip in enumerate(chips):
                cp = copy(t, j, c, me, (*chip, c), src=in_refs[t].at[2 * chip[0] + chip[1]])
                cp.start()
                first.append(cp)
            cp = copy(t, 6, c, me, sibling, src=in_refs[t].at[me])
            cp.start()
            first.append(cp)
        passed = []
        for t in range(nt):
            for j, chip in enumerate(chips):
                cid = 2 * chip[0] + chip[1]
                copy(t, j, c, cid, (x, y, c)).wait_recv()
                cp = copy(t, 3 + j, c, cid, sibling)
                cp.start()
                passed.append(cp)
        for t in range(nt):
            for j, chip in enumerate(chips):
                copy(t, 3 + j, 1 - c, 2 * chip[0] + chip[1], (x, y, c)).wait_recv()
            copy(t, 6, 1 - c, me, (x, y, c)).wait_recv()
        for cp in first + passed:
            cp.wait_send()
        for cp in local:
            cp.wait()

    return pl.pallas_call(
        body, name=name, in_specs=[ANY] * nt, out_specs=[ANY] * nt,
        out_shape=[S((2,) + a.shape, a.dtype) for a in arrs],
        scratch_shapes=[pltpu.SemaphoreType.DMA((nt, 7)), pltpu.SemaphoreType.DMA((nt, 7)),
                        pltpu.SemaphoreType.DMA((nt,))],
    )(*arrs)


def gather_small(vec, *, name):
    R, C = vec.shape

    def body(v_ref, out_ref, send_sems, recv_sems):
        x, y, c = _pos()
        me = 4 * x + 2 * y + c
        out_ref[me] = v_ref[...]
        cps = []
        def peer(k):
            fx, fy, fc = (k >> 2) & 1, (k >> 1) & 1, k & 1
            return (1 - x if fx else x), (1 - y if fy else y), (1 - c if fc else c)

        for k in range(1, 8):
            cp = pltpu.make_async_remote_copy(src_ref=v_ref, dst_ref=out_ref.at[me], send_sem=send_sems.at[k - 1],
                                              recv_sem=recv_sems.at[k - 1], device_id=peer(k), device_id_type=MESH)
            cp.start()
            cps.append(cp)
        for k in range(1, 8):
            px, py, pc = peer(k)
            pltpu.make_async_remote_copy(src_ref=v_ref, dst_ref=out_ref.at[4 * px + 2 * py + pc],
                                         send_sem=send_sems.at[k - 1], recv_sem=recv_sems.at[k - 1],
                                         device_id=(px, py, pc), device_id_type=MESH).wait_recv()
        for cp in cps:
            cp.wait_send()

    return pl.pallas_call(
        body, name=name,
        in_specs=[pl.BlockSpec(memory_space=pltpu.VMEM)], out_specs=pl.BlockSpec(memory_space=pltpu.VMEM),
        out_shape=S((8, R, C), vec.dtype),
        scratch_shapes=[pltpu.SemaphoreType.DMA((7,)), pltpu.SemaphoreType.DMA((7,))],
    )(vec)


def _tile(n, prefs):
    for p in prefs:
        if n % p == 0:
            return p
    return n


def _lanes(g, reps):
    return jnp.tile(g.reshape(1, -1), (1, reps))


def _layer_fwd(x, p, e):
    T, D = x.shape
    F = p["w_gate"].shape[1]
    tm = _tile(T, (512, 256, 128))
    tmm = _tile(T, (1024, 512, 256, 128))
    h, proj = rms_proj(x, p["norm1_g"], p["w_in"], tm=tmm, tn=_tile(IN_W, (768,)), name="rms_proj")
    gains = (_lanes(p["a_q_g"], 8), _lanes(p["a_k_g"], 8), _lanes(p["c_q_g"], 8), _lanes(p["c_k_g"], 2))
    aq, ak, av, cq, ckk, cvv = prep_fwd(proj, e, gains, tm=_tile(T, (256, 128)), name="prep_fwd")
    ols = []
    for d in DILATIONS:
        ols += band_attn_fwd(aq, ak, av, None, dil=d, max_dist=A_DIST, group=1, name=f"dil_attn_fwd_{d}")
    out_a = dil_combine_fwd(ols, tm=tm, name="dil_combine_fwd")
    out_b = conv_fwd(proj, p["conv_w"], p["conv_b"], p["conv_ln_g"], p["conv_ln_b"], tb=tm, name="conv_fwd")
    sinks = jnp.repeat(p["c_sinks"].reshape(-1), HD).reshape(1, C_W)
    o_c, l_c = band_attn_fwd(cq, ckk, cvv, sinks, dil=1, max_dist=C_DIST, group=4, name="swa_attn_fwd")
    mix = jnp.concatenate([out_a, out_b, o_c.astype(MXU)], axis=1)
    x1 = matmul_res(mix, p["w_out"], x, tm=tmm, tn=_tile(D, (1024, 512, 256)), name="out_proj")
    h2, gate, up, act = rms_swiglu(x1, p["norm2_g"], p["w_gate"], p["w_up"], tm=tmm, tn=_tile(F, (512, 256, 128)),
                                   name="rms_swiglu")
    x2 = matmul_res(act, p["w_down"], x1, tm=tmm, tn=_tile(D, (512, 256)), name="ffn_down")
    saved = dict(x=x, h=h, proj=proj, gains=gains, aq=aq, ak=ak, av=av, cq=cq, ckk=ckk, cvv=cvv, ols=ols, o_c=o_c,
                 l_c=l_c, sinks=sinks, mix=mix, x1=x1, h2=h2, gate=gate, up=up, act=act)
    return x2, saved


def _layer_bwd(dx2, dx2b, p, s, e):
    T, D = dx2.shape
    F = p["w_gate"].shape[1]
    tm = _tile(T, (512, 256, 128))
    tmm = _tile(T, (1024, 512, 256, 128))
    tkT = _tile(T, (2048, 1024, 512))
    tF = _tile(F, (512, 256, 128))
    tD = _tile(D, (1024, 512, 256))
    g = {}
    d_gate, d_up = nt_swiglu_bwd(dx2b, p["w_down"], s["gate"], s["up"], tm=tmm, tn=tF, name="ffn_down_bwd")
    g["w_down"] = tn_matmul(s["act"], dx2b, tm=tF, tn=tD, tk=tkT, name="grad_w_down")
    g["w_gate"] = tn_matmul(s["h2"], d_gate, tm=tD, tn=tF, tk=tkT, name="grad_w_gate")
    g["w_up"] = tn_matmul(s["h2"], d_up, tm=tD, tn=tF, tk=tkT, name="grad_w_up")
    dx1, dx1b, g["norm2_g"] = nt_rms_bwd([(d_gate, p["w_gate"]), (d_up, p["w_up"])], s["x1"], p["norm2_g"], dx2,
                                         tm=tm, tk=tF, name="ffn_in_bwd")
    dmix = nt_plain(dx1b, p["w_out"], tm=tmm, tn=tD, name="out_proj_bwd")
    g["w_out"] = tn_matmul(s["mix"], dx1b, tm=1024, tn=tD, tk=tkT, name="grad_w_out")
    dos = dil_combine_bwd(s["ols"], dmix, e, tm=tm, name="dil_combine_bwd")
    da = []
    for n, d in enumerate(DILATIONS):
        da += band_attn_bwd(s["aq"], s["ak"], s["av"], s["ols"][2 * n + 1], dos[n], dos[3 + n], dil=d,
                            max_dist=A_DIST, group=1, name=f"dil_attn_bwd_{d}")
    du, dgt, gw, gb, glg, glb = conv_bwd(s["proj"], dmix, p["conv_w"], p["conv_b"], p["conv_ln_g"], p["conv_ln_b"],
                                         tb=tm, name="conv_bwd")
    g["conv_w"], g["conv_b"], g["conv_ln_g"], g["conv_ln_b"] = gw[:CONV_K], gb, glg, glb
    do_c, dd_c, dsink = swa_pre_bwd(s["o_c"], s["l_c"], dmix, s["sinks"], e, tm=_tile(T, (256, 128)), name="swa_pre_bwd")
    g["c_sinks"] = dsink.reshape(-1, HD)[:, 0]
    dcq, dckk, dcvv = band_attn_bwd(s["cq"], s["ckk"], s["cvv"], s["l_c"], do_c, dd_c, dil=1, max_dist=C_DIST, group=4,
                                    name="swa_attn_bwd")
    dproj, gaq, gak, gcq, gck = prep_bwd(s["proj"], e, s["gains"], da, (dcq, dckk, dcvv), (du, dgt),
                                         tm=_tile(T, (256, 128)), name="prep_bwd")
    g["a_q_g"] = gaq.reshape(-1, HD).sum(0)
    g["a_k_g"] = gak.reshape(-1, HD).sum(0)
    g["c_q_g"] = gcq.reshape(-1, HD).sum(0)
    g["c_k_g"] = gck.reshape(-1, HD).sum(0)
    g["w_in"] = tn_matmul(s["h"], dproj, tm=tD, tn=_tile(IN_W, (1280,)), tk=tkT, name="grad_w_in")
    dx, dxb, g["norm1_g"] = nt_rms_bwd([(dproj, p["w_in"])], s["x"], p["norm1_g"], dx1, tm=tm,
                                       tk=_tile(IN_W, (768,)), name="in_proj_bwd")
    return dx, dxb, g


BIG = ("w_in", "w_out", "w_gate", "w_up", "w_down")
COL_SHARDED = ("w_in", "w_gate", "w_up")
SMALL = ("norm1_g", "a_q_g", "a_k_g", "conv_w", "conv_b", "conv_ln_g", "conv_ln_b", "c_q_g", "c_k_g", "c_sinks", "norm2_g")


def _to_pieces(g, name):
    R, C = g.shape
    if name in COL_SHARDED:
        return g.reshape(2, R // 2, 4, C // 4).transpose(2, 0, 1, 3)
    return g.reshape(4, 2, R // 8, C)


def _from_gathered(w, name):
    _, _, r, c = w.shape
    if name in COL_SHARDED:
        return w.transpose(1, 2, 0, 3).reshape(2 * r, 4 * c)
    return w.reshape(8 * r, c)


def _pack(items, rows):
    flat = jnp.concatenate([a.reshape(-1).astype(f32) for a in items])
    return jnp.pad(flat, (0, rows * 128 - flat.shape[0])).reshape(rows, 128)


def _unpack(packed, shapes):
    flat = packed.reshape(-1)
    out, off = [], 0
    for shp in shapes:
        n = 1
        for d in shp:
            n *= d
        out.append(flat[off:off + n].reshape(shp))
        off += n
    return out


def kernel(x, norm1_g, w_in, a_q_g, a_k_g, conv_w, conv_b, conv_ln_g, conv_ln_b, c_q_g, c_k_g, c_sinks, w_out, norm2_g, w_gate, w_up, w_down, loss_target, m_norm1_g, m_w_in, m_a_q_g, m_a_k_g, m_conv_w, m_conv_b, m_conv_ln_g, m_conv_ln_b, m_c_q_g, m_c_k_g, m_c_sinks, m_w_out, m_norm2_g, m_w_gate, m_w_up, m_w_down, v_norm1_g, v_w_in, v_a_q_g, v_a_k_g, v_conv_w, v_conv_b, v_conv_ln_g, v_conv_ln_b, v_c_q_g, v_c_k_g, v_c_sinks, v_w_out, v_norm2_g, v_w_gate, v_w_up, v_w_down):
    W = dict(norm1_g=norm1_g, w_in=w_in, a_q_g=a_q_g, a_k_g=a_k_g, conv_w=conv_w, conv_b=conv_b, conv_ln_g=conv_ln_g,
             conv_ln_b=conv_ln_b, c_q_g=c_q_g, c_k_g=c_k_g, c_sinks=c_sinks, w_out=w_out, norm2_g=norm2_g, w_gate=w_gate,
             w_up=w_up, w_down=w_down)
    M = dict(norm1_g=m_norm1_g, w_in=m_w_in, a_q_g=m_a_q_g, a_k_g=m_a_k_g, conv_w=m_conv_w, conv_b=m_conv_b,
             conv_ln_g=m_conv_ln_g, conv_ln_b=m_conv_ln_b, c_q_g=m_c_q_g, c_k_g=m_c_k_g, c_sinks=m_c_sinks, w_out=m_w_out,
             norm2_g=m_norm2_g, w_gate=m_w_gate, w_up=m_w_up, w_down=m_w_down)
    V = dict(norm1_g=v_norm1_g, w_in=v_w_in, a_q_g=v_a_q_g, a_k_g=v_a_k_g, conv_w=v_conv_w, conv_b=v_conv_b,
             conv_ln_g=v_conv_ln_g, conv_ln_b=v_conv_ln_b, c_q_g=v_c_q_g, c_k_g=v_c_k_g, c_sinks=v_c_sinks, w_out=v_w_out,
             norm2_g=v_norm2_g, w_gate=v_w_gate, w_up=v_w_up, w_down=v_w_down)
    depth = norm1_g.shape[0]
    T, D = x.shape[1], x.shape[2]
    xs = x.reshape(T, D)
    chip = 2 * lax.axis_index("x") + lax.axis_index("y")
    e = _head_eye()

    shards = []
    for l in range(depth):
        for n in BIG:
            w = W[n][l]
            shards.append(w.astype(MXU).reshape(2, w.shape[0] // 2, w.shape[1]))
    shards.append(conv_w)
    gathered = gather_shards(shards, name="gather_weights")
    conv_full = gathered[-1].transpose(1, 2, 0, 3).reshape(depth, CONV_K, B_W)
    params = []
    for l in range(depth):
        p = {n: _from_gathered(gathered[l * len(BIG) + i], n) for i, n in enumerate(BIG)}
        p["conv_w"] = jnp.pad(conv_full[l], ((0, HALO - CONV_K), (0, 0)))
        for n in SMALL:
            if n != "conv_w":
                p[n] = W[n][l].reshape(1, -1)
        params.append(p)

    saved = []
    act = xs
    for l in range(depth):
        act, s = _layer_fwd(act, params[l], e)
        saved.append(s)
    dy, dyb, loss_part = loss_head(act, loss_target.reshape(T, D), tm=_tile(T, (512, 256, 128)), name="loss_head")
    grads = [None] * depth
    for l in reversed(range(depth)):
        dy, dyb, grads[l] = _layer_bwd(dy, dyb, params[l], saved[l], e)
    grad_x = dy.reshape(x.shape)

    c = lax.axis_index("c")
    pieces = [_to_pieces(grads[l][n], n) for l in range(depth) for n in BIG]
    mine = [lax.dynamic_index_in_dim(pc, c, axis=1, keepdims=False) for pc in pieces]
    theirs = [lax.dynamic_index_in_dim(pc, 1 - c, axis=1, keepdims=False) for pc in pieces]
    from_sibling = sibling_swap(theirs, name="grad_sibling_swap")
    chip_sums = [add_halves(a, b, tr=_tile(a.shape[1], (256, 176, 128, 64, 32, 16)), name="grad_chip_sum")
                 for a, b in zip(mine, from_sibling)]
    landed = exchange_pieces(chip_sums, name="grad_exchange")
    out = {}
    for i, n in enumerate(BIG):
        per_layer = [landed[l * len(BIG) + i] for l in range(depth)]
        r, cc = per_layer[0].shape[2], per_layer[0].shape[3]
        srcs = [jnp.stack([pl_[:, s].reshape(2 * r, cc) for pl_ in per_layer]) for s in range(4)]
        out[n] = adamw(W[n], M[n], V[n], srcs, tr=_tile(2 * r, (256, 176, 128, 64, 32, 16)), name="adamw_" + n)

    small_shapes = []
    items = []
    for l in range(depth):
        for n in SMALL:
            a = grads[l][n]
            if n == "conv_w":
                a = a.reshape(CONV_K, 4, B_W // 4).transpose(1, 0, 2)
            items.append(a)
            small_shapes.append(a.shape)
    items.append(loss_part[0, 0:1])
    small_shapes.append((1,))
    total = sum(int(jnp.size(a)) for a in items)
    rows = -(-total // 1024) * 8
    summed = sum8(gather_small(_pack(items, rows), name="gather_small"), name="sum_small")
    parts = _unpack(summed, small_shapes)
    loss = parts[-1][0]
    small_g = {n: [] for n in SMALL}
    for l in range(depth):
        for i, n in enumerate(SMALL):
            a = parts[l * len(SMALL) + i]
            if n == "conv_w":
                a = lax.dynamic_index_in_dim(a, chip, axis=0, keepdims=False)
            small_g[n].append(a.reshape(W[n].shape[1:]))
    sw = [W[n] for n in SMALL]
    sm = [M[n] for n in SMALL]
    sv = [V[n] for n in SMALL]
    sg = [jnp.stack(small_g[n]) for n in SMALL]
    tot2 = sum(int(jnp.size(a)) for a in sw)
    rows2 = -(-tot2 // 1024) * 8
    res = adamw(_pack(sw, rows2)[None], _pack(sm, rows2)[None], _pack(sv, rows2)[None], [_pack(sg, rows2)[None]],
                tr=rows2, name="adamw_small")
    shapes2 = [a.shape for a in sw]
    small_out = [_unpack(r[0], shapes2) for r in res]
    for i, n in enumerate(SMALL):
        out[n] = [small_out[k][i] for k in range(4)]

    order = ("norm1_g", "w_in", "a_q_g", "a_k_g", "conv_w", "conv_b", "conv_ln_g", "conv_ln_b", "c_q_g", "c_k_g",
             "c_sinks", "w_out", "norm2_g", "w_gate", "w_up", "w_down")
    return (loss, grad_x, *[out[n][0] for n in order], *[out[n][1] for n in order], *[out[n][2] for n in order],
            *[out[n][3] for n in order])
```

```python
import functools

import jax
import jax.numpy as jnp
from jax import lax
from jax.experimental import pallas as pl
from jax.experimental.pallas import tpu as pltpu

f32 = jnp.float32
MXU = jnp.bfloat16
S = jax.ShapeDtypeStruct
MESH = pl.DeviceIdType.MESH

EPS = 1e-6
NEG = -1e30
HD = 64
BLK = 128
A_W, B_W, C_W = 512, 512, 1024
KV_W = 128
IN_W = 3 * A_W + 2 * B_W + C_W + 2 * KV_W
CONV_K = 31
HALO = 32
DILATIONS = (1, 4, 16)
A_DIST, C_DIST = 128, 127
SCALE = HD ** -0.5
VMEM_LIMIT = 56 * 1024 * 1024

ADAM_LR, ADAM_B1, ADAM_B2, ADAM_EPS, ADAM_WD, ADAM_STEP = 0.001, 0.9, 0.999, 1e-08, 0.01, 10


def _cp(*sem):
    return pltpu.CompilerParams(dimension_semantics=sem, vmem_limit_bytes=VMEM_LIMIT)


ANY = pl.BlockSpec(memory_space=pl.ANY)


def _pallas(body, *, comm=None, name, grid, in_specs, out_specs, out_shape, scratch_shapes=(), compiler_params):
    if comm is None:
        return pl.pallas_call(body, name=name, grid=grid, in_specs=in_specs, out_specs=out_specs, out_shape=out_shape,
                              scratch_shapes=list(scratch_shapes), compiler_params=compiler_params)
    single = not isinstance(out_shape, (list, tuple))
    o_shapes = [out_shape] if single else list(out_shape)
    o_specs = [out_specs] if single else list(out_specs)
    n_in, n_out, n_sc = len(in_specs), len(o_shapes), len(scratch_shapes)
    nci, nco = len(comm.ins), len(comm.out_shapes)
    total = 1
    for g in grid:
        total *= g

    def carried(*refs):
        ins, cins = refs[:n_in], refs[n_in:n_in + nci]
        o0 = n_in + nci
        outs, couts = refs[o0:o0 + n_out], refs[o0 + n_out:o0 + n_out + nco]
        s0 = o0 + n_out + nco
        scratch, sems = refs[s0:s0 + n_sc], refs[s0 + n_sc:]
        step = pl.program_id(0)
        for axis in range(1, len(grid)):
            step = step * grid[axis] + pl.program_id(axis)

        @pl.when(step == 0)
        def _():
            comm.start(cins, couts, sems)

        body(*ins, *outs, *scratch)

        @pl.when(step == total // 2)
        def _():
            comm.mid(cins, couts, sems)

        @pl.when(step == total - 1)
        def _():
            comm.finish(cins, couts, sems)

    call = pl.pallas_call(carried, name=name, grid=grid, in_specs=list(in_specs) + [ANY] * nci,
                          out_specs=o_specs + [ANY] * nco, out_shape=o_shapes + list(comm.out_shapes),
                          scratch_shapes=list(scratch_shapes) + list(comm.sem_shapes), compiler_params=compiler_params)

    def run(*args):
        res = call(*args, *comm.ins)
        comm.results = list(res[n_out:])
        return res[0] if single else list(res[:n_out])

    return run


def _run_comm(comm, *, name):
    nci, nco = len(comm.ins), len(comm.out_shapes)

    def body(*refs):
        cins, couts, sems = refs[:nci], refs[nci:nci + nco], refs[nci + nco:]
        comm.start(cins, couts, sems)
        comm.mid(cins, couts, sems)
        comm.finish(cins, couts, sems)

    comm.results = list(pl.pallas_call(body, name=name, in_specs=[ANY] * nci, out_specs=[ANY] * nco,
                                       out_shape=list(comm.out_shapes), scratch_shapes=list(comm.sem_shapes))(*comm.ins))


def _nt(a, b):
    return lax.dot_general(a, b, (((1,), (1,)), ((), ())), preferred_element_type=f32)


def _tn(a, b):
    return lax.dot_general(a, b, (((0,), (0,)), ((), ())), preferred_element_type=f32)


def _nn(a, b):
    return jnp.dot(a, b, preferred_element_type=f32)


def _sigmoid(x):
    return 1.0 / (1.0 + jnp.exp(-x))


def _seg_sum(v, e_ref):
    hi = v.astype(jnp.bfloat16)
    lo = (v - hi.astype(f32)).astype(jnp.bfloat16)
    e = e_ref[...]
    return _nn(hi, e) + _nn(lo, e)


def _seg_sum128(v, e_ref):
    e = e_ref[0:128, 0:128]
    hi = v.astype(jnp.bfloat16)
    lo = (v - hi.astype(f32)).astype(jnp.bfloat16)
    return _nn(hi, e) + _nn(lo, e)


def _head_eye():
    r = lax.broadcasted_iota(jnp.int32, (512, 512), 0) // HD
    c = lax.broadcasted_iota(jnp.int32, (512, 512), 1) // HD
    return (r == c).astype(jnp.bfloat16)


def _rms_norm_rows(x_ref, g_ref, h_ref, tm):
    def chunk(c, carry):
        rows = pl.ds(c * BLK, BLK)
        xf = x_ref[rows, :]
        r = lax.rsqrt(jnp.mean(xf * xf, axis=-1, keepdims=True) + EPS)
        h_ref[rows, :] = (xf * r * g_ref[...]).astype(MXU)
        return carry
    lax.fori_loop(0, tm // BLK, chunk, 0)


def rms_proj(x, g, w, *, tm, tn, name, comm=None):
    T, D = x.shape
    N = w.shape[1]

    def body(x_ref, g_ref, w_ref, h_ref, o_ref):
        @pl.when(pl.program_id(1) == 0)
        def _():
            _rms_norm_rows(x_ref, g_ref, h_ref, tm)
        o_ref[...] = _nn(h_ref[...], w_ref[...])

    return _pallas(
        body, comm=comm, name=name, grid=(T // tm, N // tn),
        in_specs=[pl.BlockSpec((tm, D), lambda i, j: (i, 0)), pl.BlockSpec((1, D), lambda i, j: (0, 0)),
                  pl.BlockSpec((D, tn), lambda i, j: (0, j))],
        out_specs=[pl.BlockSpec((tm, D), lambda i, j: (i, 0)), pl.BlockSpec((tm, tn), lambda i, j: (i, j))],
        out_shape=[S((T, D), MXU), S((T, N), f32)],
        compiler_params=_cp("arbitrary", "arbitrary"),
    )(x, g, w)


def rms_swiglu(x, g, wg, wu, *, tm, tn, name, comm=None):
    T, D = x.shape
    N = wg.shape[1]

    def body(x_ref, g_ref, wg_ref, wu_ref, h_ref, gate_ref, up_ref, act_ref):
        @pl.when(pl.program_id(1) == 0)
        def _():
            _rms_norm_rows(x_ref, g_ref, h_ref, tm)
        h = h_ref[...]
        gate = _nn(h, wg_ref[...])
        up = _nn(h, wu_ref[...])
        gate_ref[...] = gate
        up_ref[...] = up
        act_ref[...] = (gate * _sigmoid(gate) * up).astype(MXU)

    wspec = pl.BlockSpec((D, tn), lambda i, j: (0, j))
    ospec = pl.BlockSpec((tm, tn), lambda i, j: (i, j))
    return _pallas(
        body, comm=comm, name=name, grid=(T // tm, N // tn),
        in_specs=[pl.BlockSpec((tm, D), lambda i, j: (i, 0)), pl.BlockSpec((1, D), lambda i, j: (0, 0)), wspec, wspec],
        out_specs=[pl.BlockSpec((tm, D), lambda i, j: (i, 0)), ospec, ospec, ospec],
        out_shape=[S((T, D), MXU), S((T, N), f32), S((T, N), f32), S((T, N), MXU)],
        compiler_params=_cp("arbitrary", "arbitrary"),
    )(x, g, wg, wu)


def matmul_res(a, w, res, *, tm, tn, name, comm=None):
    T, K = a.shape
    N = w.shape[1]

    def body(a_ref, w_ref, r_ref, o_ref):
        o_ref[...] = r_ref[...] + _nn(a_ref[...], w_ref[...])

    return _pallas(
        body, comm=comm, name=name, grid=(T // tm, N // tn),
        in_specs=[pl.BlockSpec((tm, K), lambda i, j: (i, 0)), pl.BlockSpec((K, tn), lambda i, j: (0, j)),
                  pl.BlockSpec((tm, tn), lambda i, j: (i, j))],
        out_specs=pl.BlockSpec((tm, tn), lambda i, j: (i, j)),
        out_shape=S((T, N), f32),
        compiler_params=_cp("arbitrary", "arbitrary"),
    )(a, w, res)


def nt_plain(a, w, *, tm, tn, name, comm=None):
    T, K = a.shape
    N = w.shape[0]

    def body(a_ref, w_ref, o_ref):
        o_ref[...] = _nt(a_ref[...], w_ref[...])

    return _pallas(
        body, comm=comm, name=name, grid=(T // tm, N // tn),
        in_specs=[pl.BlockSpec((tm, K), lambda i, j: (i, 0)), pl.BlockSpec((tn, K), lambda i, j: (j, 0))],
        out_specs=pl.BlockSpec((tm, tn), lambda i, j: (i, j)),
        out_shape=S((T, N), f32),
        compiler_params=_cp("arbitrary", "arbitrary"),
    )(a, w)


def nt_swiglu_bwd(dy, wd, gate, up, *, tm, tn, name, comm=None):
    T, D = dy.shape
    F = wd.shape[0]

    def body(dy_ref, w_ref, g_ref, u_ref, dg_ref, du_ref):
        d_act = _nt(dy_ref[...], w_ref[...])
        g = g_ref[...]
        sg = _sigmoid(g)
        du_ref[...] = (d_act * (g * sg)).astype(MXU)
        dg_ref[...] = (d_act * u_ref[...] * (sg * (1.0 + g * (1.0 - sg)))).astype(MXU)

    blk = pl.BlockSpec((tm, tn), lambda i, j: (i, j))
    return _pallas(
        body, comm=comm, name=name, grid=(T // tm, F // tn),
        in_specs=[pl.BlockSpec((tm, D), lambda i, j: (i, 0)), pl.BlockSpec((tn, D), lambda i, j: (j, 0)), blk, blk],
        out_specs=[blk, blk],
        out_shape=[S((T, F), MXU), S((T, F), MXU)],
        compiler_params=_cp("arbitrary", "arbitrary"),
    )(dy, wd, gate, up)


def nt_rms_bwd(terms, x, g, dres, *, tm, tk, name, comm=None):
    T, D = x.shape
    K = terms[0][0].shape[1]
    nk = K // tk
    nt = len(terms)
    ni = T // tm

    def body(*refs):
        a_refs = refs[0:2 * nt:2]
        w_refs = refs[1:2 * nt:2]
        x_ref, g_ref, r_ref, dx_ref, dxb_ref, dg_ref, acc_ref = refs[2 * nt:]
        i, k = pl.program_id(0), pl.program_id(1)
        part = _nt(a_refs[0][...], w_refs[0][...])
        for t in range(1, nt):
            part += _nt(a_refs[t][...], w_refs[t][...])

        @pl.when(k == 0)
        def _():
            acc_ref[...] = part

        @pl.when(k > 0)
        def _():
            acc_ref[...] += part

        @pl.when(k == nk - 1)
        def _():
            def chunk(c, dgain):
                rows = pl.ds(c * BLK, BLK)
                dh = acc_ref[rows, :]
                xf = x_ref[rows, :]
                r = lax.rsqrt(jnp.mean(xf * xf, axis=-1, keepdims=True) + EPS)
                y = xf * r
                dy = dh * g_ref[...]
                dx = r_ref[rows, :] + r * (dy - y * jnp.mean(dy * y, axis=-1, keepdims=True))
                dx_ref[rows, :] = dx
                dxb_ref[rows, :] = dx.astype(MXU)
                return dgain + jnp.sum(dh * y, axis=0, keepdims=True)

            dgain = lax.fori_loop(0, tm // BLK, chunk, jnp.zeros((1, D), f32))

            @pl.when(i == 0)
            def _():
                dg_ref[...] = dgain

            @pl.when(i > 0)
            def _():
                dg_ref[...] += dgain

    in_specs, args = [], []
    for a, w in terms:
        in_specs += [pl.BlockSpec((tm, tk), lambda i, k: (i, k)), pl.BlockSpec((D, tk), lambda i, k: (0, k))]
        args += [a, w]
    row = pl.BlockSpec((tm, D), lambda i, k: (i, 0))
    vec = pl.BlockSpec((1, D), lambda i, k: (0, 0))
    in_specs += [row, vec, row]
    return _pallas(
        body, comm=comm, name=name, grid=(ni, nk), in_specs=in_specs,
        out_specs=[row, row, vec],
        out_shape=[S((T, D), f32), S((T, D), MXU), S((1, D), f32)],
        scratch_shapes=[pltpu.VMEM((tm, D), f32)],
        compiler_params=_cp("arbitrary", "arbitrary"),
    )(*args, x, g, dres)


def tn_matmul(a, b, *, tm, tn, tk, name):
    T, M = a.shape
    N = b.shape[1]

    def body(a_ref, b_ref, o_ref):
        part = _tn(a_ref[...], b_ref[...])

        @pl.when(pl.program_id(2) == 0)
        def _():
            o_ref[...] = part

        @pl.when(pl.program_id(2) > 0)
        def _():
            o_ref[...] += part

    return pl.pallas_call(
        body, name=name, grid=(M // tm, N // tn, T // tk),
        in_specs=[pl.BlockSpec((tk, tm), lambda i, j, k: (k, i)), pl.BlockSpec((tk, tn), lambda i, j, k: (k, j))],
        out_specs=pl.BlockSpec((tm, tn), lambda i, j, k: (i, j)),
        out_shape=S((M, N), f32),
        compiler_params=_cp("arbitrary", "arbitrary", "arbitrary"),
    )(a, b)


def loss_head(y, target, *, tm, name):
    T, D = y.shape
    ni = T // tm

    def body(y_ref, t_ref, dy_ref, dyb_ref, l_ref, acc_ref):
        i = pl.program_id(0)
        e = y_ref[...] - t_ref[...]
        dy = e * (1.0 / D)
        dy_ref[...] = dy
        dyb_ref[...] = dy.astype(MXU)
        part = jnp.sum(e * e, axis=0, keepdims=True)

        @pl.when(i == 0)
        def _():
            acc_ref[...] = part

        @pl.when(i > 0)
        def _():
            acc_ref[...] += part

        @pl.when(i == ni - 1)
        def _():
            tot = jnp.sum(acc_ref[...], axis=1, keepdims=True) * (0.5 / D)
            l_ref[...] = jnp.broadcast_to(tot, (1, 128))

    row = pl.BlockSpec((tm, D), lambda i: (i, 0))
    return pl.pallas_call(
        body, name=name, grid=(ni,), in_specs=[row, row],
        out_specs=[row, row, pl.BlockSpec((1, 128), lambda i: (0, 0))],
        out_shape=[S((T, D), f32), S((T, D), MXU), S((1, 128), f32)],
        scratch_shapes=[pltpu.VMEM((1, D), f32)],
        compiler_params=_cp("arbitrary"),
    )(y, target)


def _qk_norm(v, gain, e_ref):
    r = lax.rsqrt(_seg_sum(v * v, e_ref) * (1.0 / HD) + EPS)
    return v * r * gain


def _dup_halves(pair):
    rolled = pltpu.roll(pair, HD, 1)
    lo = lax.broadcasted_iota(jnp.int32, pair.shape, 1) < HD
    return jnp.where(lo, pair, rolled), jnp.where(lo, rolled, pair)


def prep_fwd(proj, e, gains, *, tm, name):
    T = proj.shape[0]

    def body(p_ref, e_ref, gaq, gak, gcq, gck, aq, ak, av, cq, ckk, cvv):
        aq[...] = _qk_norm(p_ref[:, 0:512], gaq[...], e_ref)
        ak[...] = _qk_norm(p_ref[:, 512:1024], gak[...], e_ref)
        av[...] = p_ref[:, 1024:1536]
        cq[:, 0:512] = _qk_norm(p_ref[:, 2560:3072], gcq[...], e_ref).astype(MXU)
        cq[:, 512:1024] = _qk_norm(p_ref[:, 3072:3584], gcq[...], e_ref).astype(MXU)
        kraw = p_ref[:, 3584:3712]
        kn = kraw * lax.rsqrt(_seg_sum128(kraw * kraw, e_ref) * (1.0 / HD) + EPS) * gck[...]
        k0, k1 = _dup_halves(kn)
        ckk[:, 0:128] = k0.astype(MXU)
        ckk[:, 128:256] = k1.astype(MXU)
        v0, v1 = _dup_halves(p_ref[:, 3712:3840])
        cvv[:, 0:128] = v0.astype(MXU)
        cvv[:, 128:256] = v1.astype(MXU)

    def vec(n):
        return pl.BlockSpec((1, n), lambda i: (0, 0))

    def rows(n):
        return pl.BlockSpec((tm, n), lambda i: (i, 0))

    return pl.pallas_call(
        body, name=name, grid=(T // tm,),
        in_specs=[rows(IN_W), pl.BlockSpec((512, 512), lambda i: (0, 0)), vec(512), vec(512), vec(512), vec(128)],
        out_specs=[rows(512), rows(512), rows(512), rows(1024), rows(256), rows(256)],
        out_shape=[S((T, 512), f32)] * 3 + [S((T, 1024), MXU), S((T, 256), MXU), S((T, 256), MXU)],
        compiler_params=_cp("arbitrary"),
    )(proj, e, *gains)


def _band_mask(max_dist, shut):
    r = lax.broadcasted_iota(jnp.int32, (2 * BLK, 2 * BLK), 0) & (BLK - 1)
    c = lax.broadcasted_iota(jnp.int32, (2 * BLK, 2 * BLK), 1)
    prev = jnp.logical_and(c < BLK, c >= r + (BLK - max_dist) + shut)
    return jnp.logical_or(prev, jnp.logical_and(c >= BLK, c - BLK <= r))


def _prev_mask(max_dist, shut):
    r = lax.broadcasted_iota(jnp.int32, (2 * BLK, BLK), 0) & (BLK - 1)
    c = lax.broadcasted_iota(jnp.int32, (2 * BLK, BLK), 1)
    return c >= r + (BLK - max_dist) + shut


def _head_masks():
    lo = (lax.broadcasted_iota(jnp.int32, (BLK, BLK), 1) < HD).astype(f32)
    return lo.astype(MXU), (1.0 - lo).astype(MXU)


def _stack_heads(x, hm):
    return jnp.concatenate([x * hm[0], x * hm[1]], axis=0)


def _unstack_heads(y, lane_lo):
    return jnp.where(lane_lo, y[0:BLK], y[BLK:2 * BLK])


def _rows(ref, start, dil):
    if dil == 1:
        return ref[pl.ds(start, BLK), :]
    return ref[pl.ds(start, BLK, stride=dil), :]


def _set_rows(ref, start, dil, val):
    if dil == 1:
        ref[pl.ds(start, BLK), :] = val
    else:
        ref[pl.ds(start, BLK, stride=dil), :] = val


def _attn_geometry(T, dil):
    span = BLK * dil
    n = max(1, 512 // span)
    return span, n, T // (span * n)


def band_attn_fwd(q, k, v, sinks, *, dil, max_dist, group, name, comm=None):
    T = q.shape[0]
    P = q.shape[1] // BLK
    span, n, nb = _attn_geometry(T, dil)

    def body(*refs):
        if sinks is None:
            q_ref, kc_ref, kp_ref, vc_ref, vp_ref, o_ref, l_ref = refs
        else:
            q_ref, kc_ref, kp_ref, vc_ref, vp_ref, s_ref, o_ref, l_ref = refs
        b = pl.program_id(0)
        mask = _band_mask(max_dist, 0)
        mask0 = _band_mask(max_dist, jnp.where(b > 0, 0, BLK + 1))
        lane_lo = lax.broadcasted_iota(jnp.int32, (BLK, BLK), 1) < HD
        hm = _head_masks()
        if sinks is not None:
            row_lo = lax.broadcasted_iota(jnp.int32, (1, BLK), 1) < HD
            sk0 = jnp.max(jnp.where(row_lo, s_ref[...], NEG), axis=1, keepdims=True)
            sk1 = jnp.max(jnp.where(row_lo, NEG, s_ref[...]), axis=1, keepdims=True)
            sk = jnp.where(lax.broadcasted_iota(jnp.int32, (2 * BLK, 1), 0) < BLK, sk0, sk1)

        def load(r, sub):
            at = r + sub * span
            kc, vc = _rows(kc_ref, at, dil).astype(MXU), _rows(vc_ref, at, dil).astype(MXU)
            if sub == 0:
                kp, vp = _rows(kp_ref, r, dil).astype(MXU), _rows(vp_ref, r, dil).astype(MXU)
            else:
                kp, vp = _rows(kc_ref, at - span, dil).astype(MXU), _rows(vc_ref, at - span, dil).astype(MXU)
            qst = _stack_heads(_rows(q_ref, at, dil).astype(MXU), hm)
            return (qst, jnp.concatenate([kp, kc], axis=0), jnp.concatenate([vp, vc], axis=0),
                    mask0 if sub == 0 else mask, at)

        def attend(items):
            ss = [jnp.where(m_, _nt(qst, kcat) * SCALE, NEG) for qst, kcat, _, m_, _ in items]
            ms = [jnp.max(s, axis=1, keepdims=True) for s in ss]
            if sinks is not None:
                ms = [jnp.maximum(m, sk) for m in ms]
            ps = [jnp.exp(s - m) for s, m in zip(ss, ms)]
            dens = [jnp.sum(p_, axis=1, keepdims=True) for p_ in ps]
            if sinks is not None:
                dens = [d + jnp.exp(sk - m) for d, m in zip(dens, ms)]
            outs = [_nn(p_.astype(MXU), it[2]) / d for p_, it, d in zip(ps, items, dens)]
            for it, o, m, d in zip(items, outs, ms, dens):
                lse = m + jnp.log(d)
                _set_rows(o_ref, it[4], dil, _unstack_heads(o, lane_lo))
                _set_rows(l_ref, it[4], dil, jnp.where(lane_lo, lse[0:BLK], lse[BLK:2 * BLK]))

        if dil * n <= 4:
            work = [(r, sub) for r in range(dil) for sub in range(n)]
            for g in range(0, len(work), 2):
                attend([load(*w) for w in work[g:g + 2]])
        else:
            def two_streams(i, carry):
                attend([load(2 * i, 0), load(2 * i + 1, 0)])
                return carry
            lax.fori_loop(0, dil // 2, two_streams, 0)

    rows_per_step = span * n
    qspec = pl.BlockSpec((rows_per_step, BLK), lambda b, p: (b, p))
    cur = pl.BlockSpec((rows_per_step, BLK), lambda b, p: (b, p // group))
    prev = pl.BlockSpec((span, BLK), lambda b, p: (jnp.maximum(b * n - 1, 0), p // group))
    in_specs = [qspec, cur, prev, cur, prev]
    args = [q, k, k, v, v]
    if sinks is not None:
        in_specs.append(pl.BlockSpec((1, BLK), lambda b, p: (0, p)))
        args.append(sinks)
    return _pallas(
        body, comm=comm, name=name, grid=(nb, P), in_specs=in_specs, out_specs=[qspec, qspec],
        out_shape=[S(q.shape, f32), S(q.shape, f32)],
        compiler_params=_cp("arbitrary", "arbitrary"),
    )(*args)


def band_attn_bwd(q, k, v, lse, do, dd, *, dil, max_dist, group, name, comm=None):
    T = q.shape[0]
    P = q.shape[1] // BLK
    span, n, nb = _attn_geometry(T, dil)
    assert group == 1 or dil == 1

    def body(q_ref, qn_ref, do_ref, don_ref, l_ref, ln_ref, d_ref, dn_ref, kc_ref, kp_ref, vc_ref, vp_ref,
             dq_ref, dk_ref, dv_ref):
        b, p = pl.program_id(0), pl.program_id(1)
        mask = _band_mask(max_dist, 0)
        mask0 = _band_mask(max_dist, jnp.where(b > 0, 0, BLK + 1))
        tail = _prev_mask(max_dist, jnp.where(b < nb - 1, 0, BLK + 1))
        lane_lo = lax.broadcasted_iota(jnp.int32, (BLK, BLK), 1) < HD
        hm = _head_masks()
        own_lanes = (lax.broadcasted_iota(jnp.int32, (2 * BLK, BLK), 1) < HD) == (
            lax.broadcasted_iota(jnp.int32, (2 * BLK, BLK), 0) < BLK)

        def per_row(x):
            return jnp.max(jnp.where(own_lanes, jnp.concatenate([x, x], axis=0), NEG), axis=1, keepdims=True)

        def q_side(refs, at):
            q_r, do_r, l_r, d_r = refs
            return (_stack_heads(_rows(q_r, at, dil).astype(MXU), hm), _stack_heads(_rows(do_r, at, dil).astype(MXU), hm),
                    per_row(_rows(l_r, at, dil)), per_row(_rows(d_r, at, dil)))

        def kv(ref, at):
            return _rows(ref, at, dil).astype(MXU)

        first = p % group == 0

        def put_kv(ref, at, val):
            if group == 1:
                _set_rows(ref, at, dil, val)
            else:
                @pl.when(first)
                def _():
                    ref[pl.ds(at, BLK), :] = val

                @pl.when(jnp.logical_not(first))
                def _():
                    ref[pl.ds(at, BLK), :] += val

        def stream(r):
            dks, dvs = [None] * n, [None] * n
            for sub in range(n):
                at = r + sub * span
                qst, dost, lrow, drow = q_side((q_ref, do_ref, l_ref, d_ref), at)
                if sub == 0:
                    kp, vp, m_ = kv(kp_ref, r), kv(vp_ref, r), mask0
                else:
                    kp, vp, m_ = kv(kc_ref, at - span), kv(vc_ref, at - span), mask
                kcat = jnp.concatenate([kp, kv(kc_ref, at)], axis=0)
                vcat = jnp.concatenate([vp, kv(vc_ref, at)], axis=0)
                pr = jnp.where(m_, jnp.exp(_nt(qst, kcat) * SCALE - lrow), 0.0)
                ds = (pr * (_nt(dost, vcat) - drow) * SCALE).astype(MXU)
                prb = pr.astype(MXU)
                _set_rows(dq_ref, at, dil, _unstack_heads(_nn(ds, kcat), lane_lo))
                if sub == 0:
                    dks[0] = _tn(ds[:, BLK:], qst)
                    dvs[0] = _tn(prb[:, BLK:], dost)
                else:
                    dkk, dvv = _tn(ds, qst), _tn(prb, dost)
                    dks[sub - 1] += dkk[0:BLK]
                    dvs[sub - 1] += dvv[0:BLK]
                    dks[sub], dvs[sub] = dkk[BLK:], dvv[BLK:]
            at = r + (n - 1) * span
            qst, dost, lrow, drow = q_side((qn_ref, don_ref, ln_ref, dn_ref), r)
            pr = jnp.where(tail, jnp.exp(_nt(qst, kv(kc_ref, at)) * SCALE - lrow), 0.0)
            ds = (pr * (_nt(dost, kv(vc_ref, at)) - drow) * SCALE).astype(MXU)
            dks[n - 1] += _tn(ds, qst)
            dvs[n - 1] += _tn(pr.astype(MXU), dost)
            for sub in range(n):
                put_kv(dk_ref, r + sub * span, dks[sub])
                put_kv(dv_ref, r + sub * span, dvs[sub])

        if dil <= 4:
            for r in range(dil):
                stream(r)
        else:
            def one_stream(r, carry):
                stream(r)
                return carry
            lax.fori_loop(0, dil, one_stream, 0)

    rows_per_step = span * n
    qspec = pl.BlockSpec((rows_per_step, BLK), lambda b, p: (b, p))
    qnext = pl.BlockSpec((span, BLK), lambda b, p: (jnp.minimum((b + 1) * n, T // span - 1), p))
    cur = pl.BlockSpec((rows_per_step, BLK), lambda b, p: (b, p // group))
    prev = pl.BlockSpec((span, BLK), lambda b, p: (jnp.maximum(b * n - 1, 0), p // group))
    return _pallas(
        body, comm=comm, name=name, grid=(nb, P),
        in_specs=[qspec, qnext, qspec, qnext, qspec, qnext, qspec, qnext, cur, prev, cur, prev],
        out_specs=[qspec, cur, cur],
        out_shape=[S(q.shape, f32), S(k.shape, f32), S(k.shape, f32)],
        compiler_params=_cp("arbitrary", "arbitrary"),
    )(q, q, do, do, lse, lse, dd, dd, k, k, v, v)


def dil_combine_fwd(ols, *, tm, name):
    T = ols[0].shape[0]

    def body(o1, l1, o2, l2, o3, l3, out_ref):
        a, b, c = l1[...], l2[...], l3[...]
        m = jnp.maximum(jnp.maximum(a, b), c)
        ea, eb, ec = jnp.exp(a - m), jnp.exp(b - m), jnp.exp(c - m)
        out = (ea * o1[...] + eb * o2[...] + ec * o3[...]) / (ea + eb + ec)
        out_ref[...] = out.astype(MXU)

    row = pl.BlockSpec((tm, 512), lambda i: (i, 0))
    return pl.pallas_call(body, name=name, grid=(T // tm,), in_specs=[row] * 6, out_specs=row,
                          out_shape=S((T, 512), MXU), compiler_params=_cp("arbitrary"))(*ols)


def dil_combine_bwd(ols, dmix, e, *, tm, name):
    T = ols[0].shape[0]

    def body(o1, l1, o2, l2, o3, l3, d_ref, e_ref, do1, do2, do3, dd1, dd2, dd3):
        a, b, c = l1[...], l2[...], l3[...]
        m = jnp.maximum(jnp.maximum(a, b), c)
        ea, eb, ec = jnp.exp(a - m), jnp.exp(b - m), jnp.exp(c - m)
        inv = 1.0 / (ea + eb + ec)
        wa, wb, wc = ea * inv, eb * inv, ec * inv
        dout = d_ref[...]
        gbar = _seg_sum(dout * (wa * o1[...] + wb * o2[...] + wc * o3[...]), e_ref)
        do1[...] = wa * dout
        do2[...] = wb * dout
        do3[...] = wc * dout
        dd1[...] = wa * gbar
        dd2[...] = wb * gbar
        dd3[...] = wc * gbar

    row = pl.BlockSpec((tm, 512), lambda i: (i, 0))
    return pl.pallas_call(
        body, name=name, grid=(T // tm,),
        in_specs=[row] * 6 + [row, pl.BlockSpec((512, 512), lambda i: (0, 0))],
        out_specs=[row] * 6,
        out_shape=[S((T, 512), f32)] * 6,
        compiler_params=_cp("arbitrary"),
    )(*ols, dmix, e)


def swa_pre_bwd(o, lse, dmix, sinks, e, *, tm, name):
    T = o.shape[0]
    ni = T // tm

    def body(o_ref, l_ref, d_ref, s_ref, e_ref, do_ref, dd_ref, ds_ref):
        i = pl.program_id(0)
        dout = d_ref[...]
        do_ref[...] = dout.astype(MXU)
        prod = dout * o_ref[...]
        dd = jnp.concatenate([_seg_sum(prod[:, 0:512], e_ref), _seg_sum(prod[:, 512:1024], e_ref)], axis=1)
        dd_ref[...] = dd
        part = -jnp.sum(jnp.exp(s_ref[...] - l_ref[...]) * dd, axis=0, keepdims=True)

        @pl.when(i == 0)
        def _():
            ds_ref[...] = part

        @pl.when(i > 0)
        def _():
            ds_ref[...] += part

    row = pl.BlockSpec((tm, 1024), lambda i: (i, 0))
    vec = pl.BlockSpec((1, 1024), lambda i: (0, 0))
    return pl.pallas_call(
        body, name=name, grid=(ni,),
        in_specs=[row, row, pl.BlockSpec((tm, 1024), lambda i: (i, 1)), vec, pl.BlockSpec((512, 512), lambda i: (0, 0))],
        out_specs=[row, row, vec],
        out_shape=[S((T, 1024), MXU), S((T, 1024), f32), S((1, 1024), f32)],
        compiler_params=_cp("arbitrary"),
    )(o, lse, dmix, sinks, e)


def _conv_taps(buf_ref, w_ref, start, rows):
    acc = buf_ref[pl.ds(start, rows), :] * w_ref[pl.ds(0, 1), :]
    for j in range(1, CONV_K):
        acc += buf_ref[pl.ds(start + j, rows), :] * w_ref[pl.ds(j, 1), :]
    return acc


def conv_fwd(proj, w, b, ln_g, ln_b, *, tb, name):
    T = proj.shape[0]
    hb = tb // HALO

    def body(u_ref, g_ref, up_ref, gp_ref, w_ref, b_ref, lg_ref, lb_ref, o_ref, hbuf):
        i = pl.program_id(0)
        hprev = up_ref[...] * _sigmoid(gp_ref[...])
        hbuf[0:HALO, :] = hprev * jnp.where(i > 0, 1.0, 0.0)
        hbuf[HALO:HALO + tb, :] = u_ref[...] * _sigmoid(g_ref[...])
        y = _conv_taps(hbuf, w_ref, HALO - (CONV_K - 1), tb) + b_ref[...]
        mu = jnp.mean(y, axis=-1, keepdims=True)
        yc = y - mu
        var = jnp.mean(yc * yc, axis=-1, keepdims=True)
        z = yc * lax.rsqrt(var + EPS) * lg_ref[...] + lb_ref[...]
        o_ref[...] = (z * _sigmoid(z)).astype(MXU)

    vec = pl.BlockSpec((1, 512), lambda i: (0, 0))
    return pl.pallas_call(
        body, name=name, grid=(T // tb,),
        in_specs=[pl.BlockSpec((tb, 512), lambda i: (i, 3)), pl.BlockSpec((tb, 512), lambda i: (i, 4)),
                  pl.BlockSpec((HALO, 512), lambda i: (jnp.maximum(i * hb - 1, 0), 3)),
                  pl.BlockSpec((HALO, 512), lambda i: (jnp.maximum(i * hb - 1, 0), 4)),
                  pl.BlockSpec((HALO, 512), lambda i: (0, 0)), vec, vec, vec],
        out_specs=pl.BlockSpec((tb, 512), lambda i: (i, 0)),
        out_shape=S((T, 512), MXU),
        scratch_shapes=[pltpu.VMEM((tb + HALO, 512), f32)],
        compiler_params=_cp("arbitrary"),
    )(proj, proj, proj, proj, w, b, ln_g, ln_b)


def conv_bwd(proj, dmix, w, b, ln_g, ln_b, *, tb, name, comm=None):
    T = proj.shape[0]
    hb = tb // HALO
    ni = T // tb
    last_h = T // HALO - 1
    ext = tb + HALO

    def body(u_ref, g_ref, up_ref, gp_ref, un_ref, gn_ref, d_ref, dn_ref, w_ref, b_ref, lg_ref, lb_ref,
             du_ref, dg_ref, dw_ref, db_ref, dlg_ref, dlb_ref, hbuf, dybuf):
        i = pl.program_id(0)
        hbuf[0:HALO, :] = up_ref[...] * _sigmoid(gp_ref[...]) * jnp.where(i > 0, 1.0, 0.0)
        u = u_ref[...]
        sg = _sigmoid(g_ref[...])
        hbuf[HALO:HALO + tb, :] = u * sg
        hbuf[HALO + tb:HALO + ext, :] = un_ref[...] * _sigmoid(gn_ref[...])
        y = _conv_taps(hbuf, w_ref, HALO - (CONV_K - 1), ext) + b_ref[...]
        mu = jnp.mean(y, axis=-1, keepdims=True)
        yc = y - mu
        rstd = lax.rsqrt(jnp.mean(yc * yc, axis=-1, keepdims=True) + EPS)
        yn = yc * rstd
        z = yn * lg_ref[...] + lb_ref[...]
        sz = _sigmoid(z)
        row = lax.broadcasted_iota(jnp.int32, (ext, 1), 0)
        own = row < tb
        keep = row < jnp.where(i < ni - 1, ext, tb)
        dout = jnp.concatenate([d_ref[...], dn_ref[...]], axis=0)
        dz = jnp.where(keep, dout * (sz * (1.0 + z * (1.0 - sz))), 0.0)
        dyn = dz * lg_ref[...]
        dy = rstd * (dyn - jnp.mean(dyn, axis=-1, keepdims=True) - yn * jnp.mean(dyn * yn, axis=-1, keepdims=True))
        dybuf[...] = dy
        dz_own = jnp.where(own, dz, 0.0)
        dlg = jnp.sum(dz_own * yn, axis=0, keepdims=True)
        dlb = jnp.sum(dz_own, axis=0, keepdims=True)
        dy_own = dybuf[0:tb, :]
        dbias = jnp.sum(dy_own, axis=0, keepdims=True)
        dh = dybuf[pl.ds(CONV_K - 1, tb), :] * w_ref[pl.ds(0, 1), :]
        for j in range(1, CONV_K):
            dh += dybuf[pl.ds(CONV_K - 1 - j, tb), :] * w_ref[pl.ds(j, 1), :]
        du_ref[...] = (dh * sg).astype(MXU)
        dg_ref[...] = (dh * u * sg * (1.0 - sg)).astype(MXU)
        taps = [jnp.sum(dy_own * hbuf[pl.ds(HALO - (CONV_K - 1) + j, tb), :], axis=0, keepdims=True)
                for j in range(CONV_K)]
        taps.append(jnp.zeros((1, 512), f32))
        dwt = jnp.concatenate(taps, axis=0)

        @pl.when(i == 0)
        def _():
            dw_ref[...] = dwt
            db_ref[...] = dbias
            dlg_ref[...] = dlg
            dlb_ref[...] = dlb

        @pl.when(i > 0)
        def _():
            dw_ref[...] += dwt
            db_ref[...] += dbias
            dlg_ref[...] += dlg
            dlb_ref[...] += dlb

    vec = pl.BlockSpec((1, 512), lambda i: (0, 0))
    wspec = pl.BlockSpec((HALO, 512), lambda i: (0, 0))

    def halo_prev(col):
        return pl.BlockSpec((HALO, 512), lambda i: (jnp.maximum(i * hb - 1, 0), col))

    def halo_next(col):
        return pl.BlockSpec((HALO, 512), lambda i: (jnp.minimum((i + 1) * hb, last_h), col))

    row = pl.BlockSpec((tb, 512), lambda i: (i, 0))
    return _pallas(
        body, comm=comm, name=name, grid=(ni,),
        in_specs=[pl.BlockSpec((tb, 512), lambda i: (i, 3)), pl.BlockSpec((tb, 512), lambda i: (i, 4)),
                  halo_prev(3), halo_prev(4), halo_next(3), halo_next(4),
                  pl.BlockSpec((tb, 512), lambda i: (i, 1)), halo_next(1), wspec, vec, vec, vec],
        out_specs=[row, row, wspec, vec, vec, vec],
        out_shape=[S((T, 512), MXU), S((T, 512), MXU), S((HALO, 512), f32)] + [S((1, 512), f32)] * 3,
        scratch_shapes=[pltpu.VMEM((tb + 2 * HALO, 512), f32), pltpu.VMEM((ext, 512), f32)],
        compiler_params=_cp("arbitrary"),
    )(proj, proj, proj, proj, proj, proj, dmix, dmix, w, b, ln_g, ln_b)


def _qk_norm_bwd(v, gain, dout, e_ref):
    r = lax.rsqrt(_seg_sum(v * v, e_ref) * (1.0 / HD) + EPS)
    y = v * r
    dgain = jnp.sum(dout * y, axis=0, keepdims=True)
    dy = dout * gain
    dv = r * (dy - y * (_seg_sum(dy * y, e_ref) * (1.0 / HD)))
    return dv, dgain


def prep_bwd(proj, e, gains, da, dc, dconv, *, tm, name):
    T = proj.shape[0]

    def body(*refs):
        p_ref, e_ref, gaq, gak, gcq, gck = refs[0:6]
        a_refs = refs[6:15]
        dcq, dckk, dcvv, du, dgt = refs[15:20]
        dp, gaq_o, gak_o, gcq_o, gck_o = refs[20:]
        i = pl.program_id(0)
        dq = a_refs[0][...] + a_refs[3][...] + a_refs[6][...]
        dk = a_refs[1][...] + a_refs[4][...] + a_refs[7][...]
        dv = a_refs[2][...] + a_refs[5][...] + a_refs[8][...]
        d, g_aq = _qk_norm_bwd(p_ref[:, 0:512], gaq[...], dq, e_ref)
        dp[:, 0:512] = d.astype(MXU)
        d, g_ak = _qk_norm_bwd(p_ref[:, 512:1024], gak[...], dk, e_ref)
        dp[:, 512:1024] = d.astype(MXU)
        dp[:, 1024:1536] = dv.astype(MXU)
        dp[:, 1536:2048] = du[...]
        dp[:, 2048:2560] = dgt[...]
        d, g_cq0 = _qk_norm_bwd(p_ref[:, 2560:3072], gcq[...], dcq[:, 0:512], e_ref)
        dp[:, 2560:3072] = d.astype(MXU)
        d, g_cq1 = _qk_norm_bwd(p_ref[:, 3072:3584], gcq[...], dcq[:, 512:1024], e_ref)
        dp[:, 3072:3584] = d.astype(MXU)
        lo = lax.broadcasted_iota(jnp.int32, (tm, 128), 1) < HD

        def fold(ref):
            g0, g1 = ref[:, 0:128], ref[:, 128:256]
            s0 = g0 + pltpu.roll(g0, HD, 1)
            s1 = g1 + pltpu.roll(g1, HD, 1)
            return jnp.where(lo, s0, s1)

        dkn = fold(dckk)
        kraw = p_ref[:, 3584:3712]
        r = lax.rsqrt(_seg_sum128(kraw * kraw, e_ref) * (1.0 / HD) + EPS)
        y = kraw * r
        g_ck = jnp.sum(dkn * y, axis=0, keepdims=True)
        dy = dkn * gck[...]
        dp[:, 3584:3712] = (r * (dy - y * (_seg_sum128(dy * y, e_ref) * (1.0 / HD)))).astype(MXU)
        dp[:, 3712:3840] = fold(dcvv).astype(MXU)
        g_cq = jnp.concatenate([g_cq0, g_cq1], axis=1)

        @pl.when(i == 0)
        def _():
            gaq_o[...] = g_aq
            gak_o[...] = g_ak
            gcq_o[...] = g_cq
            gck_o[...] = g_ck

        @pl.when(i > 0)
        def _():
            gaq_o[...] += g_aq
            gak_o[...] += g_ak
            gcq_o[...] += g_cq
            gck_o[...] += g_ck

    def vec(n):
        return pl.BlockSpec((1, n), lambda i: (0, 0))

    def rows(n):
        return pl.BlockSpec((tm, n), lambda i: (i, 0))

    return pl.pallas_call(
        body, name=name, grid=(T // tm,),
        in_specs=[rows(IN_W), pl.BlockSpec((512, 512), lambda i: (0, 0)), vec(512), vec(512), vec(512), vec(128)]
        + [rows(512)] * 9 + [rows(1024), rows(256), rows(256), rows(512), rows(512)],
        out_specs=[rows(IN_W), vec(512), vec(512), vec(1024), vec(128)],
        out_shape=[S((T, IN_W), MXU), S((1, 512), f32), S((1, 512), f32), S((1, 1024), f32), S((1, 128), f32)],
        compiler_params=_cp("arbitrary"),
    )(proj, e, *gains, *da, *dc, *dconv)


def adamw(w, m, v, pieces, *, tr, name, comm=None):
    n, R, C = w.shape
    c1 = 1.0 - ADAM_B1 ** ADAM_STEP
    c2 = 1.0 - ADAM_B2 ** ADAM_STEP
    npc = len(pieces)

    def body(*refs):
        w_ref, m_ref, v_ref = refs[0:3]
        p_refs = refs[3:3 + npc]
        g_ref, d_ref, mo_ref, vo_ref = refs[3 + npc:]
        g = p_refs[0][...].astype(f32)
        for p in p_refs[1:]:
            g = g + p[...].astype(f32)
        mn = ADAM_B1 * m_ref[...] + (1.0 - ADAM_B1) * g
        vn = ADAM_B2 * v_ref[...] + (1.0 - ADAM_B2) * (g * g)
        g_ref[...] = g
        mo_ref[...] = mn
        vo_ref[...] = vn
        d_ref[...] = -ADAM_LR * ((mn / c1) / (jnp.sqrt(vn / c2) + ADAM_EPS) + ADAM_WD * w_ref[...])

    blk = pl.BlockSpec((None, tr, C), lambda l, i: (l, i, 0))
    return _pallas(
        body, comm=comm, name=name, grid=(n, R // tr), in_specs=[blk] * (3 + npc), out_specs=[blk] * 4,
        out_shape=[S(w.shape, f32)] * 4, compiler_params=_cp("arbitrary", "arbitrary"),
    )(w, m, v, *pieces)


def add_halves(pieces, other, *, tr, name):
    _, _, r, cc = pieces.shape

    def body(c_ref, a_ref, b_ref, o_ref):
        o_ref[...] = (a_ref[...] + b_ref[...]).astype(jnp.bfloat16)

    blk = pl.BlockSpec((None, tr, cc), lambda s, i, c_ref: (s, i, 0))
    grid_spec = pltpu.PrefetchScalarGridSpec(
        num_scalar_prefetch=1, grid=(4, r // tr),
        in_specs=[pl.BlockSpec((None, None, tr, cc), lambda s, i, c_ref: (s, c_ref[0], i, 0)), blk], out_specs=blk)
    core = lax.axis_index("c").astype(jnp.int32).reshape(1)
    return pl.pallas_call(body, name=name, grid_spec=grid_spec, out_shape=S((4, r, cc), jnp.bfloat16),
                          compiler_params=_cp("arbitrary", "arbitrary"))(core, pieces, other)


def sum8(parts, *, name):
    _, R, C = parts.shape

    def body(p_ref, o_ref):
        acc = p_ref[0]
        for d in range(1, 8):
            acc = acc + p_ref[d]
        o_ref[...] = acc

    return pl.pallas_call(body, name=name, out_shape=S((R, C), f32))(parts)


def _pos():
    return lax.axis_index("x"), lax.axis_index("y"), lax.axis_index("c")


def _other_chips(x, y):
    return [(1 - x, y), (x, 1 - y), (1 - x, 1 - y)]


class GatherComm:
    def __init__(self, shards):
        self.ins = list(shards)
        self.nt = nt = len(shards)
        self.out_shapes = [S((4,) + s.shape, s.dtype) for s in shards]
        self.sem_shapes = [pltpu.SemaphoreType.DMA((nt, 6)), pltpu.SemaphoreType.DMA((nt, 6)),
                           pltpu.SemaphoreType.DMA((nt,))]
        self.results = None

    def _copy(self, couts, sems, t, k, chip, half, to, src=None):
        dst = couts[t].at[2 * chip[0] + chip[1], half]
        return pltpu.make_async_remote_copy(
            src_ref=dst if src is None else src, dst_ref=dst,
            send_sem=sems[0].at[t, k], recv_sem=sems[1].at[t, k], device_id=to, device_id_type=MESH)

    def _local(self, cins, couts, sems, t):
        x, y, _ = _pos()
        return pltpu.make_async_copy(cins[t], couts[t].at[2 * x + y], sems[2].at[t])

    def start(self, cins, couts, sems):
        x, y, c = _pos()
        for t in range(self.nt):
            self._local(cins, couts, sems, t).start()
            for j, chip in enumerate(_other_chips(x, y)):
                self._copy(couts, sems, t, j, (x, y), c, (*chip, c), src=cins[t].at[c]).start()

    def mid(self, cins, couts, sems):
        x, y, c = _pos()
        for t in range(self.nt):
            for j, chip in enumerate(_other_chips(x, y)):
                self._copy(couts, sems, t, j, chip, c, (x, y, c)).wait_recv()
                self._copy(couts, sems, t, 3 + j, chip, c, (x, y, 1 - c)).start()

    def finish(self, cins, couts, sems):
        x, y, c = _pos()
        for t in range(self.nt):
            for j, chip in enumerate(_other_chips(x, y)):
                self._copy(couts, sems, t, 3 + j, chip, 1 - c, (x, y, c)).wait_recv()
        for t in range(self.nt):
            for j, chip in enumerate(_other_chips(x, y)):
                self._copy(couts, sems, t, j, (x, y), c, (*chip, c), src=cins[t].at[c]).wait_send()
                self._copy(couts, sems, t, 3 + j, chip, c, (x, y, 1 - c)).wait_send()
            self._local(cins, couts, sems, t).wait()


class SwapComm:
    def __init__(self, pieces):
        self.ins = list(pieces)
        self.nt = nt = len(pieces)
        self.out_shapes = [S((4,) + p.shape[2:], p.dtype) for p in pieces]
        self.sem_shapes = [pltpu.SemaphoreType.DMA((nt, 4)), pltpu.SemaphoreType.DMA((nt, 4))]
        self.results = None

    def _copies(self, cins, couts, sems):
        x, y, c = _pos()
        return [pltpu.make_async_remote_copy(src_ref=cins[t].at[s, 1 - c], dst_ref=couts[t].at[s],
                                             send_sem=sems[0].at[t, s], recv_sem=sems[1].at[t, s],
                                             device_id=(x, y, 1 - c), device_id_type=MESH)
                for t in range(self.nt) for s in range(4)]

    def start(self, cins, couts, sems):
        for cp in self._copies(cins, couts, sems):
            cp.start()

    def mid(self, cins, couts, sems):
        pass

    def finish(self, cins, couts, sems):
        for cp in self._copies(cins, couts, sems):
            cp.wait()


class ExchangeComm:
    def __init__(self, arrs):
        self.ins = list(arrs)
        self.nt = nt = len(arrs)
        self.out_shapes = [S((2,) + a.shape, a.dtype) for a in arrs]
        self.sem_shapes = [pltpu.SemaphoreType.DMA((nt, 7)), pltpu.SemaphoreType.DMA((nt, 7)),
                           pltpu.SemaphoreType.DMA((nt,))]
        self.results = None

    def _copy(self, couts, sems, t, k, half, src_chip, to, src=None):
        dst = couts[t].at[half, src_chip]
        return pltpu.make_async_remote_copy(
            src_ref=dst if src is None else src, dst_ref=dst,
            send_sem=sems[0].at[t, k], recv_sem=sems[1].at[t, k], device_id=to, device_id_type=MESH)

    def _local(self, cins, couts, sems, t):
        x, y, c = _pos()
        return pltpu.make_async_copy(cins[t].at[2 * x + y], couts[t].at[c, 2 * x + y], sems[2].at[t])

    def _firsts(self, cins, couts, sems, t):
        x, y, c = _pos()
        me = 2 * x + y
        cps = [self._copy(couts, sems, t, j, c, me, (*chip, c), src=cins[t].at[2 * chip[0] + chip[1]])
               for j, chip in enumerate(_other_chips(x, y))]
        return cps + [self._copy(couts, sems, t, 6, c, me, (x, y, 1 - c), src=cins[t].at[me])]

    def start(self, cins, couts, sems):
        for t in range(self.nt):
            self._local(cins, couts, sems, t).start()
            for cp in self._firsts(cins, couts, sems, t):
                cp.start()

    def mid(self, cins, couts, sems):
        x, y, c = _pos()
        for t in range(self.nt):
            for j, chip in enumerate(_other_chips(x, y)):
                cid = 2 * chip[0] + chip[1]
                self._copy(couts, sems, t, j, c, cid, (x, y, c)).wait_recv()
                self._copy(couts, sems, t, 3 + j, c, cid, (x, y, 1 - c)).start()

    def finish(self, cins, couts, sems):
        x, y, c = _pos()
        for t in range(self.nt):
            for j, chip in enumerate(_other_chips(x, y)):
                self._copy(couts, sems, t, 3 + j, 1 - c, 2 * chip[0] + chip[1], (x, y, c)).wait_recv()
            self._copy(couts, sems, t, 6, 1 - c, 2 * x + y, (x, y, c)).wait_recv()
        for t in range(self.nt):
            for cp in self._firsts(cins, couts, sems, t):
                cp.wait_send()
            for j, chip in enumerate(_other_chips(x, y)):
                self._copy(couts, sems, t, 3 + j, c, 2 * chip[0] + chip[1], (x, y, 1 - c)).wait_send()
            self._local(cins, couts, sems, t).wait()


def gather_small(vec, *, name):
    R, C = vec.shape

    def body(v_ref, out_ref, send_sems, recv_sems):
        x, y, c = _pos()
        me = 4 * x + 2 * y + c
        out_ref[me] = v_ref[...]
        cps = []
        def peer(k):
            fx, fy, fc = (k >> 2) & 1, (k >> 1) & 1, k & 1
            return (1 - x if fx else x), (1 - y if fy else y), (1 - c if fc else c)

        for k in range(1, 8):
            cp = pltpu.make_async_remote_copy(src_ref=v_ref, dst_ref=out_ref.at[me], send_sem=send_sems.at[k - 1],
                                              recv_sem=recv_sems.at[k - 1], device_id=peer(k), device_id_type=MESH)
            cp.start()
            cps.append(cp)
        for k in range(1, 8):
            px, py, pc = peer(k)
            pltpu.make_async_remote_copy(src_ref=v_ref, dst_ref=out_ref.at[4 * px + 2 * py + pc],
                                         send_sem=send_sems.at[k - 1], recv_sem=recv_sems.at[k - 1],
                                         device_id=(px, py, pc), device_id_type=MESH).wait_recv()
        for cp in cps:
            cp.wait_send()

    return pl.pallas_call(
        body, name=name,
        in_specs=[pl.BlockSpec(memory_space=pltpu.VMEM)], out_specs=pl.BlockSpec(memory_space=pltpu.VMEM),
        out_shape=S((8, R, C), vec.dtype),
        scratch_shapes=[pltpu.SemaphoreType.DMA((7,)), pltpu.SemaphoreType.DMA((7,))],
    )(vec)


def _tile(n, prefs):
    for p in prefs:
        if n % p == 0:
            return p
    return n


def _lanes(g, reps):
    return jnp.tile(g.reshape(1, -1), (1, reps))


class _NoRide:
    def rider(self, name):
        return None

    def landed(self, comm):
        pass

    def grad(self, name, val):
        pass


def _layer_fwd(x, p, e, ride=_NoRide()):
    T, D = x.shape
    tm = _tile(T, (512, 256, 128))
    tmm = _tile(T, (1024, 512, 256, 128))

    def carried(fn, *args, name, **kw):
        comm = ride.rider(name)
        out = fn(*args, name=name, comm=comm, **kw)
        ride.landed(comm)
        return out

    h, proj = carried(rms_proj, x, p["norm1_g"], p["w_in"], tm=tmm, tn=_tile(IN_W, (768,)), name="rms_proj")
    gains = (_lanes(p["a_q_g"], 8), _lanes(p["a_k_g"], 8), _lanes(p["c_q_g"], 8), _lanes(p["c_k_g"], 2))
    aq, ak, av, cq, ckk, cvv = prep_fwd(proj, e, gains, tm=_tile(T, (256, 128)), name="prep_fwd")
    ols = []
    for d in DILATIONS:
        ols += carried(band_attn_fwd, aq, ak, av, None, dil=d, max_dist=A_DIST, group=1, name=f"dil_attn_fwd_{d}")
    out_a = dil_combine_fwd(ols, tm=tm, name="dil_combine_fwd")
    out_b = conv_fwd(proj, p["conv_w"], p["conv_b"], p["conv_ln_g"], p["conv_ln_b"], tb=tm, name="conv_fwd")
    sinks = jnp.repeat(p["c_sinks"].reshape(-1), HD).reshape(1, C_W)
    o_c, l_c = carried(band_attn_fwd, cq, ckk, cvv, sinks, dil=1, max_dist=C_DIST, group=4, name="swa_attn_fwd")
    mix = jnp.concatenate([out_a, out_b, o_c.astype(MXU)], axis=1)
    x1 = carried(matmul_res, mix, p["w_out"], x, tm=tmm, tn=_tile(D, (1024, 512, 256)), name="out_proj")
    F = p["w_gate"].shape[1]
    h2, gate, up, act = carried(rms_swiglu, x1, p["norm2_g"], p["w_gate"], p["w_up"], tm=tmm,
                                tn=_tile(F, (512, 256, 128)), name="rms_swiglu")
    x2 = carried(matmul_res, act, p["w_down"], x1, tm=tmm, tn=_tile(D, (512, 256)), name="ffn_down")
    saved = dict(x=x, h=h, proj=proj, gains=gains, aq=aq, ak=ak, av=av, cq=cq, ckk=ckk, cvv=cvv, ols=ols, o_c=o_c,
                 l_c=l_c, sinks=sinks, mix=mix, x1=x1, h2=h2, gate=gate, up=up, act=act)
    return x2, saved


def _layer_bwd(dx2, dx2b, p, s, e, ride=_NoRide()):
    T, D = dx2.shape
    F = p["w_gate"].shape[1]
    tm = _tile(T, (512, 256, 128))
    tmm = _tile(T, (1024, 512, 256, 128))
    tkT = _tile(T, (2048, 1024, 512))
    tF = _tile(F, (512, 256, 128))
    tD = _tile(D, (1024, 512, 256))
    g = {}

    def carried(fn, *args, name, **kw):
        comm = ride.rider(name)
        out = fn(*args, name=name, comm=comm, **kw)
        ride.landed(comm)
        return out

    def big(n, val):
        g[n] = val
        ride.grad(n, val)

    d_gate, d_up = carried(nt_swiglu_bwd, dx2b, p["w_down"], s["gate"], s["up"], tm=tmm, tn=tF, name="ffn_down_bwd")
    big("w_down", tn_matmul(s["act"], dx2b, tm=tF, tn=tD, tk=tkT, name="grad_w_down"))
    big("w_gate", tn_matmul(s["h2"], d_gate, tm=tD, tn=tF, tk=tkT, name="grad_w_gate"))
    big("w_up", tn_matmul(s["h2"], d_up, tm=tD, tn=tF, tk=tkT, name="grad_w_up"))
    dx1, dx1b, g["norm2_g"] = carried(nt_rms_bwd, [(d_gate, p["w_gate"]), (d_up, p["w_up"])], s["x1"], p["norm2_g"],
                                      dx2, tm=tm, tk=tF, name="ffn_in_bwd")
    dmix = nt_plain(dx1b, p["w_out"], tm=tmm, tn=tD, name="out_proj_bwd")
    big("w_out", tn_matmul(s["mix"], dx1b, tm=1024, tn=tD, tk=tkT, name="grad_w_out"))
    dos = dil_combine_bwd(s["ols"], dmix, e, tm=tm, name="dil_combine_bwd")
    da = []
    for n, d in enumerate(DILATIONS):
        da += carried(band_attn_bwd, s["aq"], s["ak"], s["av"], s["ols"][2 * n + 1], dos[n], dos[3 + n], dil=d,
                      max_dist=A_DIST, group=1, name=f"dil_attn_bwd_{d}")
    du, dgt, gw, gb, glg, glb = carried(conv_bwd, s["proj"], dmix, p["conv_w"], p["conv_b"], p["conv_ln_g"],
                                        p["conv_ln_b"], tb=tm, name="conv_bwd")
    g["conv_w"], g["conv_b"], g["conv_ln_g"], g["conv_ln_b"] = gw[:CONV_K], gb, glg, glb
    do_c, dd_c, dsink = swa_pre_bwd(s["o_c"], s["l_c"], dmix, s["sinks"], e, tm=_tile(T, (256, 128)), name="swa_pre_bwd")
    g["c_sinks"] = dsink.reshape(-1, HD)[:, 0]
    dcq, dckk, dcvv = carried(band_attn_bwd, s["cq"], s["ckk"], s["cvv"], s["l_c"], do_c, dd_c, dil=1, max_dist=C_DIST,
                              group=4, name="swa_attn_bwd")
    dproj, gaq, gak, gcq, gck = prep_bwd(s["proj"], e, s["gains"], da, (dcq, dckk, dcvv), (du, dgt),
                                         tm=_tile(T, (256, 128)), name="prep_bwd")
    g["a_q_g"] = gaq.reshape(-1, HD).sum(0)
    g["a_k_g"] = gak.reshape(-1, HD).sum(0)
    g["c_q_g"] = gcq.reshape(-1, HD).sum(0)
    g["c_k_g"] = gck.reshape(-1, HD).sum(0)
    big("w_in", tn_matmul(s["h"], dproj, tm=tD, tn=_tile(IN_W, (1280,)), tk=tkT, name="grad_w_in"))
    dx, dxb, g["norm1_g"] = carried(nt_rms_bwd, [(dproj, p["w_in"])], s["x"], p["norm1_g"], dx1, tm=tm,
                                    tk=_tile(IN_W, (768,)), name="in_proj_bwd")
    return dx, dxb, g


BIG = ("w_in", "w_out", "w_gate", "w_up", "w_down")
COL_SHARDED = ("w_in", "w_gate", "w_up")
SMALL = ("norm1_g", "a_q_g", "a_k_g", "conv_w", "conv_b", "conv_ln_g", "conv_ln_b", "c_q_g", "c_k_g", "c_sinks", "norm2_g")


def _to_pieces(g, name):
    R, C = g.shape
    if name in COL_SHARDED:
        return g.reshape(2, R // 2, 4, C // 4).transpose(2, 0, 1, 3)
    return g.reshape(4, 2, R // 8, C)


def _from_gathered(w, name):
    _, _, r, c = w.shape
    if name in COL_SHARDED:
        return w.transpose(1, 2, 0, 3).reshape(2 * r, 4 * c)
    return w.reshape(8 * r, c)


def _shard(W, l, n):
    w = W[n][l]
    return w.astype(MXU).reshape(2, w.shape[0] // 2, w.shape[1])


class _LayerParams(dict):
    def __init__(self, layer, full, small):
        super().__init__(small)
        self.layer, self.full = layer, full

    def __missing__(self, n):
        return self.full[(self.layer, n)]


FWD_RIDES = {
    (0, "rms_proj"): ((0, "w_out"), (0, "w_gate")),
    (0, "dil_attn_fwd_16"): ((0, "w_up"),),
    (0, "swa_attn_fwd"): ((0, "w_down"),),
    (0, "rms_swiglu"): ((1, "w_in"), (1, "w_out")),
    (0, "ffn_down"): ((1, "w_gate"),),
    (1, "rms_proj"): ((1, "w_up"),),
    (1, "swa_attn_fwd"): ((1, "w_down"),),
}
BWD_SWAPS = {"ffn_in_bwd": ("w_down", "w_gate", "w_up"), "conv_bwd": ("w_out",), "in_proj_bwd": ("w_in",)}
BWD_EXCHANGES = {"dil_attn_bwd_1": ("w_down",), "dil_attn_bwd_4": ("w_gate",), "dil_attn_bwd_16": ("w_up",),
                 "swa_attn_bwd": ("w_out",)}


class _FwdRide:
    def __init__(self, layer, W, full):
        self.layer, self.W, self.full = layer, W, full

    def rider(self, name):
        keys = FWD_RIDES.get((self.layer, name))
        if not keys:
            return None
        comm = GatherComm([_shard(self.W, l, n) for l, n in keys])
        comm.keys = keys
        return comm

    def landed(self, comm):
        if comm is not None:
            for (l, n), g in zip(comm.keys, comm.results):
                self.full[(l, n)] = _from_gathered(g, n)


class _GradFlow:
    def __init__(self):
        self.pieces, self.sums, self.landed, self.pending = {}, {}, {}, []

    def swap(self, keys):
        comm = SwapComm([self.pieces[k] for k in keys])
        comm.keys, comm.kind = list(keys), "swap"
        return comm

    def exchange(self, keys):
        if not keys:
            return None
        comm = ExchangeComm([self.sums[k] for k in keys])
        comm.keys, comm.kind = list(keys), "exchange"
        return comm

    def take_pending(self):
        keys, self.pending = self.pending, []
        return keys

    def land(self, comm):
        if comm is None:
            return
        for k, res in zip(comm.keys, comm.results):
            if comm.kind == "swap":
                r = res.shape[1]
                self.sums[k] = add_halves(self.pieces[k], res, tr=_tile(r, (256, 176, 128, 64, 32, 16)),
                                          name="grad_chip_sum")
                if k[1] == "w_in":
                    self.pending.append(k)
            else:
                self.landed[k] = res


class _BwdRide:
    def __init__(self, layer, flow):
        self.layer, self.flow = layer, flow

    def grad(self, name, val):
        self.flow.pieces[(self.layer, name)] = _to_pieces(val, name)

    def rider(self, name):
        if name in BWD_SWAPS:
            return self.flow.swap([(self.layer, n) for n in BWD_SWAPS[name]])
        if name in BWD_EXCHANGES:
            return self.flow.exchange([(self.layer, n) for n in BWD_EXCHANGES[name]])
        if name == "ffn_down_bwd":
            return self.flow.exchange(self.flow.take_pending())
        return None

    def landed(self, comm):
        self.flow.land(comm)


def _pack(items, rows):
    flat = jnp.concatenate([a.reshape(-1).astype(f32) for a in items])
    return jnp.pad(flat, (0, rows * 128 - flat.shape[0])).reshape(rows, 128)


def _unpack(packed, shapes):
    flat = packed.reshape(-1)
    out, off = [], 0
    for shp in shapes:
        n = 1
        for d in shp:
            n *= d
        out.append(flat[off:off + n].reshape(shp))
        off += n
    return out


def kernel(x, norm1_g, w_in, a_q_g, a_k_g, conv_w, conv_b, conv_ln_g, conv_ln_b, c_q_g, c_k_g, c_sinks, w_out, norm2_g, w_gate, w_up, w_down, loss_target, m_norm1_g, m_w_in, m_a_q_g, m_a_k_g, m_conv_w, m_conv_b, m_conv_ln_g, m_conv_ln_b, m_c_q_g, m_c_k_g, m_c_sinks, m_w_out, m_norm2_g, m_w_gate, m_w_up, m_w_down, v_norm1_g, v_w_in, v_a_q_g, v_a_k_g, v_conv_w, v_conv_b, v_conv_ln_g, v_conv_ln_b, v_c_q_g, v_c_k_g, v_c_sinks, v_w_out, v_norm2_g, v_w_gate, v_w_up, v_w_down):
    W = dict(norm1_g=norm1_g, w_in=w_in, a_q_g=a_q_g, a_k_g=a_k_g, conv_w=conv_w, conv_b=conv_b, conv_ln_g=conv_ln_g,
             conv_ln_b=conv_ln_b, c_q_g=c_q_g, c_k_g=c_k_g, c_sinks=c_sinks, w_out=w_out, norm2_g=norm2_g, w_gate=w_gate,
             w_up=w_up, w_down=w_down)
    M = dict(norm1_g=m_norm1_g, w_in=m_w_in, a_q_g=m_a_q_g, a_k_g=m_a_k_g, conv_w=m_conv_w, conv_b=m_conv_b,
             conv_ln_g=m_conv_ln_g, conv_ln_b=m_conv_ln_b, c_q_g=m_c_q_g, c_k_g=m_c_k_g, c_sinks=m_c_sinks, w_out=m_w_out,
             norm2_g=m_norm2_g, w_gate=m_w_gate, w_up=m_w_up, w_down=m_w_down)
    V = dict(norm1_g=v_norm1_g, w_in=v_w_in, a_q_g=v_a_q_g, a_k_g=v_a_k_g, conv_w=v_conv_w, conv_b=v_conv_b,
             conv_ln_g=v_conv_ln_g, conv_ln_b=v_conv_ln_b, c_q_g=v_c_q_g, c_k_g=v_c_k_g, c_sinks=v_c_sinks, w_out=v_w_out,
             norm2_g=v_norm2_g, w_gate=v_w_gate, w_up=v_w_up, w_down=v_w_down)
    depth = norm1_g.shape[0]
    T, D = x.shape[1], x.shape[2]
    xs = x.reshape(T, D)
    chip = 2 * lax.axis_index("x") + lax.axis_index("y")
    e = _head_eye()

    full = {}
    first = GatherComm([_shard(W, 0, "w_in"), conv_w])
    _run_comm(first, name="gather_first")
    full[(0, "w_in")] = _from_gathered(first.results[0], "w_in")
    conv_full = first.results[1].transpose(1, 2, 0, 3).reshape(depth, CONV_K, B_W)
    params = []
    for l in range(depth):
        small = {n: W[n][l].reshape(1, -1) for n in SMALL if n != "conv_w"}
        small["conv_w"] = jnp.pad(conv_full[l], ((0, HALO - CONV_K), (0, 0)))
        params.append(_LayerParams(l, full, small))

    saved = []
    act = xs
    for l in range(depth):
        act, s = _layer_fwd(act, params[l], e, _FwdRide(l, W, full))
        saved.append(s)
    dy, dyb, loss_part = loss_head(act, loss_target.reshape(T, D), tm=_tile(T, (512, 256, 128)), name="loss_head")
    grads = [None] * depth
    flow = _GradFlow()
    for l in reversed(range(depth)):
        dy, dyb, grads[l] = _layer_bwd(dy, dyb, params[l], saved[l], e, _BwdRide(l, flow))
    grad_x = dy.reshape(x.shape)

    out = {}
    for n in ("w_down", "w_gate", "w_up", "w_out", "w_in"):
        last = flow.exchange(flow.take_pending())
        per_layer = [flow.landed[(l, n)] for l in range(depth)]
        r, cc = per_layer[0].shape[2], per_layer[0].shape[3]
        srcs = [jnp.stack([pl_[:, s].reshape(2 * r, cc) for pl_ in per_layer]) for s in range(4)]
        out[n] = adamw(W[n], M[n], V[n], srcs, tr=_tile(2 * r, (256, 176, 128, 64, 32, 16)), name="adamw_" + n,
                       comm=last)
        flow.land(last)

    small_shapes = []
    items = []
    for l in range(depth):
        for n in SMALL:
            a = grads[l][n]
            if n == "conv_w":
                a = a.reshape(CONV_K, 4, B_W // 4).transpose(1, 0, 2)
            items.append(a)
            small_shapes.append(a.shape)
    items.append(loss_part[0, 0:1])
    small_shapes.append((1,))
    total = sum(int(jnp.size(a)) for a in items)
    rows = -(-total // 1024) * 8
    summed = sum8(gather_small(_pack(items, rows), name="gather_small"), name="sum_small")
    parts = _unpack(summed, small_shapes)
    loss = parts[-1][0]
    small_g = {n: [] for n in SMALL}
    for l in range(depth):
        for i, n in enumerate(SMALL):
            a = parts[l * len(SMALL) + i]
            if n == "conv_w":
                a = lax.dynamic_index_in_dim(a, chip, axis=0, keepdims=False)
            small_g[n].append(a.reshape(W[n].shape[1:]))
    sw = [W[n] for n in SMALL]
    sm = [M[n] for n in SMALL]
    sv = [V[n] for n in SMALL]
    sg = [jnp.stack(small_g[n]) for n in SMALL]
    tot2 = sum(int(jnp.size(a)) for a in sw)
    rows2 = -(-tot2 // 1024) * 8
    res = adamw(_pack(sw, rows2)[None], _pack(sm, rows2)[None], _pack(sv, rows2)[None], [_pack(sg, rows2)[None]],
                tr=rows2, name="adamw_small")
    shapes2 = [a.shape for a in sw]
    small_out = [_unpack(r[0], shapes2) for r in res]
    for i, n in enumerate(SMALL):
        out[n] = [small_out[k][i] for k in range(4)]

    order = ("norm1_g", "w_in", "a_q_g", "a_k_g", "conv_w", "conv_b", "conv_ln_g", "conv_ln_b", "c_q_g", "c_k_g",
             "c_sinks", "w_out", "norm2_g", "w_gate", "w_up", "w_down")
    return (loss, grad_x, *[out[n][0] for n in order], *[out[n][1] for n in order], *[out[n][2] for n in order],
            *[out[n][3] for n in order])
```

```python
import functools

import jax
import jax.numpy as jnp
from jax import lax
from jax.experimental import pallas as pl
from jax.experimental.pallas import tpu as pltpu

f32 = jnp.float32
MXU = jnp.bfloat16
S = jax.ShapeDtypeStruct
MESH = pl.DeviceIdType.MESH

EPS = 1e-6
NEG = -1e30
HD = 64
BLK = 128
A_W, B_W, C_W = 512, 512, 1024
KV_W = 128
IN_W = 3 * A_W + 2 * B_W + C_W + 2 * KV_W
CONV_K = 31
HALO = 32
DILATIONS = (1, 4, 16)
A_DIST, C_DIST = 128, 127
SCALE = HD ** -0.5
VMEM_LIMIT = 56 * 1024 * 1024
VMEM_TALL = 62 * 1024 * 1024

ADAM_LR, ADAM_B1, ADAM_B2, ADAM_EPS, ADAM_WD, ADAM_STEP = 0.001, 0.9, 0.999, 1e-08, 0.01, 10


def _cp(*sem, vmem=VMEM_LIMIT):
    return pltpu.CompilerParams(dimension_semantics=sem, vmem_limit_bytes=vmem)


ANY = pl.BlockSpec(memory_space=pl.ANY)


def _pallas(body, *, comm=None, name, grid, in_specs, out_specs, out_shape, scratch_shapes=(), compiler_params,
            input_output_aliases=None):
    aliases = dict(input_output_aliases or {})
    if comm is None:
        return pl.pallas_call(body, name=name, grid=grid, in_specs=in_specs, out_specs=out_specs, out_shape=out_shape,
                              scratch_shapes=list(scratch_shapes), compiler_params=compiler_params,
                              input_output_aliases=aliases)
    single = not isinstance(out_shape, (list, tuple))
    o_shapes = [out_shape] if single else list(out_shape)
    o_specs = [out_specs] if single else list(out_specs)
    n_in, n_out, n_sc = len(in_specs), len(o_shapes), len(scratch_shapes)
    nci, nco = len(comm.ins), len(comm.out_shapes)
    total = 1
    for g in grid:
        total *= g

    def carried(*refs):
        ins, cins = refs[:n_in], refs[n_in:n_in + nci]
        o0 = n_in + nci
        outs, couts = refs[o0:o0 + n_out], refs[o0 + n_out:o0 + n_out + nco]
        s0 = o0 + n_out + nco
        scratch, sems = refs[s0:s0 + n_sc], refs[s0 + n_sc:]
        step = pl.program_id(0)
        for axis in range(1, len(grid)):
            step = step * grid[axis] + pl.program_id(axis)

        @pl.when(step == 0)
        def _():
            comm.start(cins, couts, sems)

        body(*ins, *outs, *scratch)

        @pl.when(step == (3 * total) // 4)
        def _():
            comm.mid(cins, couts, sems)

        @pl.when(step == total - 1)
        def _():
            comm.finish(cins, couts, sems)

    call = pl.pallas_call(carried, name=name, grid=grid, in_specs=list(in_specs) + [ANY] * nci,
                          out_specs=o_specs + [ANY] * nco, out_shape=o_shapes + list(comm.out_shapes),
                          scratch_shapes=list(scratch_shapes) + list(comm.sem_shapes), compiler_params=compiler_params,
                          input_output_aliases=aliases)

    def run(*args):
        res = call(*args, *comm.ins)
        comm.results = list(res[n_out:])
        return res[0] if single else list(res[:n_out])

    return run


class MultiComm:
    def __init__(self, comms):
        self.comms = list(comms)
        self.ins = [a for c in self.comms for a in c.ins]
        self.out_shapes = [s for c in self.comms for s in c.out_shapes]
        self.sem_shapes = [s for c in self.comms for s in c.sem_shapes]

    def _each(self, cins, couts, sems):
        i = o = s = 0
        for c in self.comms:
            ni, no, ns = len(c.ins), len(c.out_shapes), len(c.sem_shapes)
            yield c, cins[i:i + ni], couts[o:o + no], sems[s:s + ns]
            i, o, s = i + ni, o + no, s + ns

    def start(self, cins, couts, sems):
        for c, a, b, d in self._each(cins, couts, sems):
            c.start(a, b, d)

    def mid(self, cins, couts, sems):
        for c, a, b, d in self._each(cins, couts, sems):
            c.mid(a, b, d)

    def finish(self, cins, couts, sems):
        for c, a, b, d in self._each(cins, couts, sems):
            c.finish(a, b, d)

    @property
    def results(self):
        return [r for c in self.comms for r in c.results]

    @results.setter
    def results(self, vals):
        o = 0
        for c in self.comms:
            c.results = list(vals[o:o + len(c.out_shapes)])
            o += len(c.out_shapes)


def _run_comm(comm, *, name):
    nci, nco = len(comm.ins), len(comm.out_shapes)

    def body(*refs):
        cins, couts, sems = refs[:nci], refs[nci:nci + nco], refs[nci + nco:]
        comm.start(cins, couts, sems)
        comm.mid(cins, couts, sems)
        comm.finish(cins, couts, sems)

    comm.results = list(pl.pallas_call(body, name=name, in_specs=[ANY] * nci, out_specs=[ANY] * nco,
                                       out_shape=list(comm.out_shapes), scratch_shapes=list(comm.sem_shapes))(*comm.ins))


def _nt(a, b):
    return lax.dot_general(a, b, (((1,), (1,)), ((), ())), preferred_element_type=f32)


def _tn(a, b):
    return lax.dot_general(a, b, (((0,), (0,)), ((), ())), preferred_element_type=f32)


def _nn(a, b):
    return jnp.dot(a, b, preferred_element_type=f32)


def _sigmoid(x):
    return 1.0 / (1.0 + jnp.exp(-x))


def _seg_sum(v, e_ref):
    hi = v.astype(jnp.bfloat16)
    lo = (v - hi.astype(f32)).astype(jnp.bfloat16)
    e = e_ref[...]
    return _nn(hi, e) + _nn(lo, e)


def _seg_sum128(v, e_ref):
    e = e_ref[0:128, 0:128]
    hi = v.astype(jnp.bfloat16)
    lo = (v - hi.astype(f32)).astype(jnp.bfloat16)
    return _nn(hi, e) + _nn(lo, e)


def _head_eye():
    r = lax.broadcasted_iota(jnp.int32, (512, 512), 0) // HD
    c = lax.broadcasted_iota(jnp.int32, (512, 512), 1) // HD
    return (r == c).astype(jnp.bfloat16)


def _rms_norm_rows(x_ref, g_ref, h_ref, tm):
    def chunk(c, carry):
        rows = pl.ds(c * BLK, BLK)
        xf = x_ref[rows, :]
        r = lax.rsqrt(jnp.mean(xf * xf, axis=-1, keepdims=True) + EPS)
        h_ref[rows, :] = (xf * r * g_ref[...]).astype(MXU)
        return carry
    lax.fori_loop(0, tm // BLK, chunk, 0)


def rms_proj(x, g, w, *, tm, tn, name, comm=None):
    T, D = x.shape
    N = w.shape[1]

    def body(x_ref, g_ref, w_ref, h_ref, o_ref):
        @pl.when(pl.program_id(1) == 0)
        def _():
            _rms_norm_rows(x_ref, g_ref, h_ref, tm)
        o_ref[...] = _nn(h_ref[...], w_ref[...])

    return _pallas(
        body, comm=comm, name=name, grid=(T // tm, N // tn),
        in_specs=[pl.BlockSpec((tm, D), lambda i, j: (i, 0)), pl.BlockSpec((1, D), lambda i, j: (0, 0)),
                  pl.BlockSpec((D, tn), lambda i, j: (0, j))],
        out_specs=[pl.BlockSpec((tm, D), lambda i, j: (i, 0)), pl.BlockSpec((tm, tn), lambda i, j: (i, j))],
        out_shape=[S((T, D), MXU), S((T, N), f32)],
        compiler_params=_cp("arbitrary", "arbitrary"),
    )(x, g, w)


def rms_swiglu(x, g, wg, wu, *, tm, tn, name, comm=None):
    T, D = x.shape
    N = wg.shape[1]

    def body(x_ref, g_ref, wg_ref, wu_ref, h_ref, gate_ref, up_ref, act_ref):
        @pl.when(pl.program_id(1) == 0)
        def _():
            _rms_norm_rows(x_ref, g_ref, h_ref, tm)
        h = h_ref[...]
        gate = _nn(h, wg_ref[...])
        up = _nn(h, wu_ref[...])
        gate_ref[...] = gate
        up_ref[...] = up
        act_ref[...] = (gate * _sigmoid(gate) * up).astype(MXU)

    wspec = pl.BlockSpec((D, tn), lambda i, j: (0, j))
    ospec = pl.BlockSpec((tm, tn), lambda i, j: (i, j))
    return _pallas(
        body, comm=comm, name=name, grid=(T // tm, N // tn),
        in_specs=[pl.BlockSpec((tm, D), lambda i, j: (i, 0)), pl.BlockSpec((1, D), lambda i, j: (0, 0)), wspec, wspec],
        out_specs=[pl.BlockSpec((tm, D), lambda i, j: (i, 0)), ospec, ospec, ospec],
        out_shape=[S((T, D), MXU), S((T, N), f32), S((T, N), f32), S((T, N), MXU)],
        compiler_params=_cp("arbitrary", "arbitrary"),
    )(x, g, wg, wu)


def matmul_res(a, w, res, *, tm, tn, name, comm=None):
    T, K = a.shape
    N = w.shape[1]

    def body(a_ref, w_ref, r_ref, o_ref):
        o_ref[...] = r_ref[...] + _nn(a_ref[...], w_ref[...])

    return _pallas(
        body, comm=comm, name=name, grid=(T // tm, N // tn),
        in_specs=[pl.BlockSpec((tm, K), lambda i, j: (i, 0)), pl.BlockSpec((K, tn), lambda i, j: (0, j)),
                  pl.BlockSpec((tm, tn), lambda i, j: (i, j))],
        out_specs=pl.BlockSpec((tm, tn), lambda i, j: (i, j)),
        out_shape=S((T, N), f32),
        compiler_params=_cp("arbitrary", "arbitrary"),
    )(a, w, res)


def nt_plain(a, w, *, tm, tn, name, comm=None):
    T, K = a.shape
    N = w.shape[0]

    def body(a_ref, w_ref, o_ref):
        o_ref[...] = _nt(a_ref[...], w_ref[...])

    return _pallas(
        body, comm=comm, name=name, grid=(T // tm, N // tn),
        in_specs=[pl.BlockSpec((tm, K), lambda i, j: (i, 0)), pl.BlockSpec((tn, K), lambda i, j: (j, 0))],
        out_specs=pl.BlockSpec((tm, tn), lambda i, j: (i, j)),
        out_shape=S((T, N), f32),
        compiler_params=_cp("arbitrary", "arbitrary"),
    )(a, w)


def nt_swiglu_bwd(dy, wd, gate, up, *, tm, tn, name, comm=None):
    T, D = dy.shape
    F = wd.shape[0]

    def body(dy_ref, w_ref, g_ref, u_ref, dg_ref, du_ref):
        d_act = _nt(dy_ref[...], w_ref[...])
        g = g_ref[...]
        sg = _sigmoid(g)
        du_ref[...] = (d_act * (g * sg)).astype(MXU)
        dg_ref[...] = (d_act * u_ref[...] * (sg * (1.0 + g * (1.0 - sg)))).astype(MXU)

    blk = pl.BlockSpec((tm, tn), lambda i, j: (i, j))
    return _pallas(
        body, comm=comm, name=name, grid=(T // tm, F // tn),
        in_specs=[pl.BlockSpec((tm, D), lambda i, j: (i, 0)), pl.BlockSpec((tn, D), lambda i, j: (j, 0)), blk, blk],
        out_specs=[blk, blk],
        out_shape=[S((T, F), MXU), S((T, F), MXU)],
        compiler_params=_cp("arbitrary", "arbitrary"),
    )(dy, wd, gate, up)


def nt_rms_bwd(terms, x, g, dres, *, tm, tk, name, comm=None):
    T, D = x.shape
    K = terms[0][0].shape[1]
    nk = K // tk
    nt = len(terms)
    ni = T // tm

    def body(*refs):
        a_refs = refs[0:2 * nt:2]
        w_refs = refs[1:2 * nt:2]
        x_ref, g_ref, r_ref, dx_ref, dxb_ref, dg_ref, acc_ref = refs[2 * nt:]
        i, k = pl.program_id(0), pl.program_id(1)
        part = _nt(a_refs[0][...], w_refs[0][...])
        for t in range(1, nt):
            part += _nt(a_refs[t][...], w_refs[t][...])

        @pl.when(k == 0)
        def _():
            acc_ref[...] = part

        @pl.when(k > 0)
        def _():
            acc_ref[...] += part

        @pl.when(k == nk - 1)
        def _():
            def chunk(c, dgain):
                rows = pl.ds(c * BLK, BLK)
                dh = acc_ref[rows, :]
                xf = x_ref[rows, :]
                r = lax.rsqrt(jnp.mean(xf * xf, axis=-1, keepdims=True) + EPS)
                y = xf * r
                dy = dh * g_ref[...]
                dx = r_ref[rows, :] + r * (dy - y * jnp.mean(dy * y, axis=-1, keepdims=True))
                dx_ref[rows, :] = dx
                dxb_ref[rows, :] = dx.astype(MXU)
                return dgain + jnp.sum(dh * y, axis=0, keepdims=True)

            dgain = lax.fori_loop(0, tm // BLK, chunk, jnp.zeros((1, D), f32))

            @pl.when(i == 0)
            def _():
                dg_ref[...] = dgain

            @pl.when(i > 0)
            def _():
                dg_ref[...] += dgain

    in_specs, args = [], []
    for a, w in terms:
        in_specs += [pl.BlockSpec((tm, tk), lambda i, k: (i, k)), pl.BlockSpec((D, tk), lambda i, k: (0, k))]
        args += [a, w]
    row = pl.BlockSpec((tm, D), lambda i, k: (i, 0), pipeline_mode=pl.Buffered(1))
    vec = pl.BlockSpec((1, D), lambda i, k: (0, 0))
    in_specs += [row, vec, row]
    return _pallas(
        body, comm=comm, name=name, grid=(ni, nk), in_specs=in_specs,
        out_specs=[row, row, vec],
        out_shape=[S((T, D), f32), S((T, D), MXU), S((1, D), f32)],
        scratch_shapes=[pltpu.VMEM((tm, D), f32)],
        compiler_params=_cp("arbitrary", "arbitrary", vmem=VMEM_TALL),
    )(*args, x, g, dres)


def tn_matmul(a, b, *, tm, tn, tk, name, by_chip=False, comm=None):
    T, M = a.shape
    N = b.shape[1]
    if by_chip:
        tn = N // 4
        out_spec = pl.BlockSpec((None, tm, tn), lambda i, j, k: (j, i, 0))
        out_shape = S((4, M, tn), f32)
    else:
        out_spec = pl.BlockSpec((tm, tn), lambda i, j, k: (i, j))
        out_shape = S((M, N), f32)

    def body(a_ref, b_ref, o_ref):
        part = _tn(a_ref[...], b_ref[...])

        @pl.when(pl.program_id(2) == 0)
        def _():
            o_ref[...] = part

        @pl.when(pl.program_id(2) > 0)
        def _():
            o_ref[...] += part

    return _pallas(
        body, comm=comm, name=name, grid=(M // tm, N // tn, T // tk),
        in_specs=[pl.BlockSpec((tk, tm), lambda i, j, k: (k, i)), pl.BlockSpec((tk, tn), lambda i, j, k: (k, j))],
        out_specs=out_spec, out_shape=out_shape,
        compiler_params=_cp("arbitrary", "arbitrary", "arbitrary"),
    )(a, b)


def loss_head(y, target, *, tm, name):
    T, D = y.shape
    ni = T // tm

    def body(y_ref, t_ref, dy_ref, dyb_ref, l_ref, acc_ref):
        i = pl.program_id(0)
        e = y_ref[...] - t_ref[...]
        dy = e * (1.0 / D)
        dy_ref[...] = dy
        dyb_ref[...] = dy.astype(MXU)
        part = jnp.sum(e * e, axis=0, keepdims=True)

        @pl.when(i == 0)
        def _():
            acc_ref[...] = part

        @pl.when(i > 0)
        def _():
            acc_ref[...] += part

        @pl.when(i == ni - 1)
        def _():
            tot = jnp.sum(acc_ref[...], axis=1, keepdims=True) * (0.5 / D)
            l_ref[...] = jnp.broadcast_to(tot, (1, 128))

    row = pl.BlockSpec((tm, D), lambda i: (i, 0))
    return pl.pallas_call(
        body, name=name, grid=(ni,), in_specs=[row, row],
        out_specs=[row, row, pl.BlockSpec((1, 128), lambda i: (0, 0))],
        out_shape=[S((T, D), f32), S((T, D), MXU), S((1, 128), f32)],
        scratch_shapes=[pltpu.VMEM((1, D), f32)],
        compiler_params=_cp("arbitrary"),
    )(y, target)


def _qk_norm(v, gain, e_ref):
    r = lax.rsqrt(_seg_sum(v * v, e_ref) * (1.0 / HD) + EPS)
    return v * r * gain


def _dup_halves(pair):
    rolled = pltpu.roll(pair, HD, 1)
    lo = lax.broadcasted_iota(jnp.int32, pair.shape, 1) < HD
    return jnp.where(lo, pair, rolled), jnp.where(lo, rolled, pair)


def prep_fwd(proj, e, gains, *, tm, name):
    T = proj.shape[0]

    def body(p_ref, e_ref, gaq, gak, gcq, gck, aq, ak, av, cq, ckk, cvv):
        aq[...] = _qk_norm(p_ref[:, 0:512], gaq[...], e_ref)
        ak[...] = _qk_norm(p_ref[:, 512:1024], gak[...], e_ref)
        av[...] = p_ref[:, 1024:1536]
        cq[:, 0:512] = _qk_norm(p_ref[:, 2560:3072], gcq[...], e_ref).astype(MXU)
        cq[:, 512:1024] = _qk_norm(p_ref[:, 3072:3584], gcq[...], e_ref).astype(MXU)
        kraw = p_ref[:, 3584:3712]
        kn = kraw * lax.rsqrt(_seg_sum128(kraw * kraw, e_ref) * (1.0 / HD) + EPS) * gck[...]
        k0, k1 = _dup_halves(kn)
        ckk[:, 0:128] = k0.astype(MXU)
        ckk[:, 128:256] = k1.astype(MXU)
        v0, v1 = _dup_halves(p_ref[:, 3712:3840])
        cvv[:, 0:128] = v0.astype(MXU)
        cvv[:, 128:256] = v1.astype(MXU)

    def vec(n):
        return pl.BlockSpec((1, n), lambda i: (0, 0))

    def rows(n):
        return pl.BlockSpec((tm, n), lambda i: (i, 0))

    return pl.pallas_call(
        body, name=name, grid=(T // tm,),
        in_specs=[rows(IN_W), pl.BlockSpec((512, 512), lambda i: (0, 0)), vec(512), vec(512), vec(512), vec(128)],
        out_specs=[rows(512), rows(512), rows(512), rows(1024), rows(256), rows(256)],
        out_shape=[S((T, 512), f32)] * 3 + [S((T, 1024), MXU), S((T, 256), MXU), S((T, 256), MXU)],
        compiler_params=_cp("arbitrary"),
    )(proj, e, *gains)


def _band_mask(max_dist, shut):
    r = lax.broadcasted_iota(jnp.int32, (2 * BLK, 2 * BLK), 0) & (BLK - 1)
    c = lax.broadcasted_iota(jnp.int32, (2 * BLK, 2 * BLK), 1)
    prev = jnp.logical_and(c < BLK, c >= r + (BLK - max_dist) + shut)
    return jnp.logical_or(prev, jnp.logical_and(c >= BLK, c - BLK <= r))


def _prev_mask(max_dist, shut):
    r = lax.broadcasted_iota(jnp.int32, (2 * BLK, BLK), 0) & (BLK - 1)
    c = lax.broadcasted_iota(jnp.int32, (2 * BLK, BLK), 1)
    return c >= r + (BLK - max_dist) + shut


def _head_masks():
    lo = (lax.broadcasted_iota(jnp.int32, (BLK, BLK), 1) < HD).astype(f32)
    return lo.astype(MXU), (1.0 - lo).astype(MXU)


def _stack_heads(x, hm):
    return jnp.concatenate([x * hm[0], x * hm[1]], axis=0)


def _unstack_heads(y, lane_lo):
    return jnp.where(lane_lo, y[0:BLK], y[BLK:2 * BLK])


def _rows(ref, start, dil):
    if dil == 1:
        return ref[pl.ds(start, BLK), :]
    return ref[pl.ds(start, BLK, stride=dil), :]


def _set_rows(ref, start, dil, val):
    if dil == 1:
        ref[pl.ds(start, BLK), :] = val
    else:
        ref[pl.ds(start, BLK, stride=dil), :] = val


def _attn_geometry(T, dil):
    span = BLK * dil
    n = max(1, 512 // span)
    return span, n, T // (span * n)


def band_attn_fwd(q, k, v, sinks, *, dil, max_dist, group, name, mix=None, comm=None):
    T = q.shape[0]
    P = q.shape[1] // BLK
    span, n, nb = _attn_geometry(T, dil)

    def body(*refs):
        s_ref = m_ref = None
        q_ref, kc_ref, kp_ref, vc_ref, vp_ref = refs[:5]
        rest = list(refs[5:])
        if sinks is not None:
            s_ref = rest.pop(0)
        if mix is not None:
            rest.pop(0)
            o_ref, l_ref, m_ref = rest
        else:
            o_ref, l_ref = rest
        b = pl.program_id(0)
        mask = _band_mask(max_dist, 0)
        mask0 = _band_mask(max_dist, jnp.where(b > 0, 0, BLK + 1))
        lane_lo = lax.broadcasted_iota(jnp.int32, (BLK, BLK), 1) < HD
        hm = _head_masks()
        if sinks is not None:
            row_lo = lax.broadcasted_iota(jnp.int32, (1, BLK), 1) < HD
            sk0 = jnp.max(jnp.where(row_lo, s_ref[...], NEG), axis=1, keepdims=True)
            sk1 = jnp.max(jnp.where(row_lo, NEG, s_ref[...]), axis=1, keepdims=True)
            sk = jnp.where(lax.broadcasted_iota(jnp.int32, (2 * BLK, 1), 0) < BLK, sk0, sk1)

        def load(r, sub):
            at = r + sub * span
            kc, vc = _rows(kc_ref, at, dil).astype(MXU), _rows(vc_ref, at, dil).astype(MXU)
            if sub == 0:
                kp, vp = _rows(kp_ref, r, dil).astype(MXU), _rows(vp_ref, r, dil).astype(MXU)
            else:
                kp, vp = _rows(kc_ref, at - span, dil).astype(MXU), _rows(vc_ref, at - span, dil).astype(MXU)
            qst = _stack_heads(_rows(q_ref, at, dil).astype(MXU), hm)
            return (qst, jnp.concatenate([kp, kc], axis=0), jnp.concatenate([vp, vc], axis=0),
                    mask0 if sub == 0 else mask, at)

        def attend(items):
            ss = [jnp.where(m_, _nt(qst, kcat) * SCALE, NEG) for qst, kcat, _, m_, _ in items]
            ms = [jnp.max(s, axis=1, keepdims=True) for s in ss]
            if sinks is not None:
                ms = [jnp.maximum(m, sk) for m in ms]
            ps = [jnp.exp(s - m) for s, m in zip(ss, ms)]
            dens = [jnp.sum(p_, axis=1, keepdims=True) for p_ in ps]
            if sinks is not None:
                dens = [d + jnp.exp(sk - m) for d, m in zip(dens, ms)]
            outs = [_nn(p_.astype(MXU), it[2]) / d for p_, it, d in zip(ps, items, dens)]
            for it, o, m, d in zip(items, outs, ms, dens):
                lse = m + jnp.log(d)
                if m_ref is not None:
                    _set_rows(m_ref, it[4], dil, _unstack_heads(o, lane_lo).astype(MXU))
                _set_rows(o_ref, it[4], dil, _unstack_heads(o, lane_lo))
                _set_rows(l_ref, it[4], dil, jnp.where(lane_lo, lse[0:BLK], lse[BLK:2 * BLK]))

        if dil * n <= 4:
            work = [(r, sub) for r in range(dil) for sub in range(n)]
            for g in range(0, len(work), 2):
                attend([load(*w) for w in work[g:g + 2]])
        else:
            def two_streams(i, carry):
                attend([load(2 * i, 0), load(2 * i + 1, 0)])
                return carry
            lax.fori_loop(0, dil // 2, two_streams, 0)

    rows_per_step = span * n
    qspec = pl.BlockSpec((rows_per_step, BLK), lambda b, p: (b, p))
    cur = pl.BlockSpec((rows_per_step, BLK), lambda b, p: (b, p // group))
    prev = pl.BlockSpec((span, BLK), lambda b, p: (jnp.maximum(b * n - 1, 0), p // group))
    in_specs = [qspec, cur, prev, cur, prev]
    args = [q, k, k, v, v]
    if sinks is not None:
        in_specs.append(pl.BlockSpec((1, BLK), lambda b, p: (0, p)))
        args.append(sinks)
    out_specs, out_shape, aliases = [qspec, qspec], [S(q.shape, f32), S(q.shape, f32)], {}
    if mix is not None:
        first_block = mix.shape[1] // BLK - P
        aliases = {len(args): 2}
        in_specs.append(ANY)
        args.append(mix)
        out_specs.append(pl.BlockSpec((rows_per_step, BLK), lambda b, p: (b, first_block + p)))
        out_shape.append(S(mix.shape, mix.dtype))
    return _pallas(
        body, comm=comm, name=name, grid=(nb, P), in_specs=in_specs, out_specs=out_specs, out_shape=out_shape,
        compiler_params=_cp("arbitrary", "arbitrary"), input_output_aliases=aliases,
    )(*args)


def band_attn_bwd(q, k, v, lse, do, dd, *, dil, max_dist, group, name, comm=None):
    T = q.shape[0]
    P = q.shape[1] // BLK
    span, n, nb = _attn_geometry(T, dil)
    assert group == 1 or dil == 1

    def body(q_ref, qn_ref, do_ref, don_ref, l_ref, ln_ref, d_ref, dn_ref, kc_ref, kp_ref, vc_ref, vp_ref,
             dq_ref, dk_ref, dv_ref):
        b, p = pl.program_id(0), pl.program_id(1)
        mask = _band_mask(max_dist, 0)
        mask0 = _band_mask(max_dist, jnp.where(b > 0, 0, BLK + 1))
        tail = _prev_mask(max_dist, jnp.where(b < nb - 1, 0, BLK + 1))
        lane_lo = lax.broadcasted_iota(jnp.int32, (BLK, BLK), 1) < HD
        hm = _head_masks()
        own_lanes = (lax.broadcasted_iota(jnp.int32, (2 * BLK, BLK), 1) < HD) == (
            lax.broadcasted_iota(jnp.int32, (2 * BLK, BLK), 0) < BLK)

        def per_row(x):
            return jnp.max(jnp.where(own_lanes, jnp.concatenate([x, x], axis=0), NEG), axis=1, keepdims=True)

        def q_side(refs, at):
            q_r, do_r, l_r, d_r = refs
            return (_stack_heads(_rows(q_r, at, dil).astype(MXU), hm), _stack_heads(_rows(do_r, at, dil).astype(MXU), hm),
                    per_row(_rows(l_r, at, dil)), per_row(_rows(d_r, at, dil)))

        def kv(ref, at):
            return _rows(ref, at, dil).astype(MXU)

        first = p % group == 0

        def put_kv(ref, at, val):
            if group == 1:
                _set_rows(ref, at, dil, val)
            else:
                @pl.when(first)
                def _():
                    ref[pl.ds(at, BLK), :] = val

                @pl.when(jnp.logical_not(first))
                def _():
                    ref[pl.ds(at, BLK), :] += val

        def stream(r):
            dks, dvs = [None] * n, [None] * n
            for sub in range(n):
                at = r + sub * span
                qst, dost, lrow, drow = q_side((q_ref, do_ref, l_ref, d_ref), at)
                if sub == 0:
                    kp, vp, m_ = kv(kp_ref, r), kv(vp_ref, r), mask0
                else:
                    kp, vp, m_ = kv(kc_ref, at - span), kv(vc_ref, at - span), mask
                kcat = jnp.concatenate([kp, kv(kc_ref, at)], axis=0)
                vcat = jnp.concatenate([vp, kv(vc_ref, at)], axis=0)
                pr = jnp.where(m_, jnp.exp(_nt(qst, kcat) * SCALE - lrow), 0.0)
                ds = (pr * (_nt(dost, vcat) - drow) * SCALE).astype(MXU)
                prb = pr.astype(MXU)
                _set_rows(dq_ref, at, dil, _unstack_heads(_nn(ds, kcat), lane_lo))
                if sub == 0:
                    dks[0] = _tn(ds[:, BLK:], qst)
                    dvs[0] = _tn(prb[:, BLK:], dost)
                else:
                    dkk, dvv = _tn(ds, qst), _tn(prb, dost)
                    dks[sub - 1] += dkk[0:BLK]
                    dvs[sub - 1] += dvv[0:BLK]
                    dks[sub], dvs[sub] = dkk[BLK:], dvv[BLK:]
            at = r + (n - 1) * span
            qst, dost, lrow, drow = q_side((qn_ref, don_ref, ln_ref, dn_ref), r)
            pr = jnp.where(tail, jnp.exp(_nt(qst, kv(kc_ref, at)) * SCALE - lrow), 0.0)
            ds = (pr * (_nt(dost, kv(vc_ref, at)) - drow) * SCALE).astype(MXU)
            dks[n - 1] += _tn(ds, qst)
            dvs[n - 1] += _tn(pr.astype(MXU), dost)
            for sub in range(n):
                put_kv(dk_ref, r + sub * span, dks[sub])
                put_kv(dv_ref, r + sub * span, dvs[sub])

        if dil <= 4:
            for r in range(dil):
                stream(r)
        else:
            def one_stream(r, carry):
                stream(r)
                return carry
            lax.fori_loop(0, dil, one_stream, 0)

    rows_per_step = span * n
    qspec = pl.BlockSpec((rows_per_step, BLK), lambda b, p: (b, p))
    qnext = pl.BlockSpec((span, BLK), lambda b, p: (jnp.minimum((b + 1) * n, T // span - 1), p))
    cur = pl.BlockSpec((rows_per_step, BLK), lambda b, p: (b, p // group))
    prev = pl.BlockSpec((span, BLK), lambda b, p: (jnp.maximum(b * n - 1, 0), p // group))
    return _pallas(
        body, comm=comm, name=name, grid=(nb, P),
        in_specs=[qspec, qnext, qspec, qnext, qspec, qnext, qspec, qnext, cur, prev, cur, prev],
        out_specs=[qspec, cur, cur],
        out_shape=[S(q.shape, f32), S(k.shape, f32), S(k.shape, f32)],
        compiler_params=_cp("arbitrary", "arbitrary"),
    )(q, q, do, do, lse, lse, dd, dd, k, k, v, v)


def dil_combine_fwd(ols, *, tm, name):
    T = ols[0].shape[0]

    def body(o1, l1, o2, l2, o3, l3, out_ref):
        a, b, c = l1[...], l2[...], l3[...]
        m = jnp.maximum(jnp.maximum(a, b), c)
        ea, eb, ec = jnp.exp(a - m), jnp.exp(b - m), jnp.exp(c - m)
        out = (ea * o1[...] + eb * o2[...] + ec * o3[...]) / (ea + eb + ec)
        out_ref[...] = out.astype(MXU)

    row = pl.BlockSpec((tm, 512), lambda i: (i, 0))
    return pl.pallas_call(body, name=name, grid=(T // tm,), in_specs=[row] * 6, out_specs=row,
                          out_shape=S((T, A_W + B_W + C_W), MXU), compiler_params=_cp("arbitrary"))(*ols)


def dil_combine_bwd(ols, dmix, e, *, tm, name):
    T = ols[0].shape[0]

    def body(o1, l1, o2, l2, o3, l3, d_ref, e_ref, do1, do2, do3, dd1, dd2, dd3):
        a, b, c = l1[...], l2[...], l3[...]
        m = jnp.maximum(jnp.maximum(a, b), c)
        ea, eb, ec = jnp.exp(a - m), jnp.exp(b - m), jnp.exp(c - m)
        inv = 1.0 / (ea + eb + ec)
        wa, wb, wc = ea * inv, eb * inv, ec * inv
        dout = d_ref[...]
        gbar = _seg_sum(dout * (wa * o1[...] + wb * o2[...] + wc * o3[...]), e_ref)
        do1[...] = wa * dout
        do2[...] = wb * dout
        do3[...] = wc * dout
        dd1[...] = wa * gbar
        dd2[...] = wb * gbar
        dd3[...] = wc * gbar

    row = pl.BlockSpec((tm, 512), lambda i: (i, 0))
    return pl.pallas_call(
        body, name=name, grid=(T // tm,),
        in_specs=[row] * 6 + [row, pl.BlockSpec((512, 512), lambda i: (0, 0))],
        out_specs=[row] * 6,
        out_shape=[S((T, 512), f32)] * 6,
        compiler_params=_cp("arbitrary"),
    )(*ols, dmix, e)


def swa_pre_bwd(o, lse, dmix, sinks, e, *, tm, name):
    T = o.shape[0]
    ni = T // tm

    def body(o_ref, l_ref, d_ref, s_ref, e_ref, do_ref, dd_ref, ds_ref):
        i = pl.program_id(0)
        dout = d_ref[...]
        do_ref[...] = dout.astype(MXU)
        prod = dout * o_ref[...]
        dd = jnp.concatenate([_seg_sum(prod[:, 0:512], e_ref), _seg_sum(prod[:, 512:1024], e_ref)], axis=1)
        dd_ref[...] = dd
        part = -jnp.sum(jnp.exp(s_ref[...] - l_ref[...]) * dd, axis=0, keepdims=True)

        @pl.when(i == 0)
        def _():
            ds_ref[...] = part

        @pl.when(i > 0)
        def _():
            ds_ref[...] += part

    row = pl.BlockSpec((tm, 1024), lambda i: (i, 0))
    vec = pl.BlockSpec((1, 1024), lambda i: (0, 0))
    return pl.pallas_call(
        body, name=name, grid=(ni,),
        in_specs=[row, row, pl.BlockSpec((tm, 1024), lambda i: (i, 1)), vec, pl.BlockSpec((512, 512), lambda i: (0, 0))],
        out_specs=[row, row, vec],
        out_shape=[S((T, 1024), MXU), S((T, 1024), f32), S((1, 1024), f32)],
        compiler_params=_cp("arbitrary"),
    )(o, lse, dmix, sinks, e)


def _conv_taps(buf_ref, w_ref, start, rows):
    acc = buf_ref[pl.ds(start, rows), :] * w_ref[pl.ds(0, 1), :]
    for j in range(1, CONV_K):
        acc += buf_ref[pl.ds(start + j, rows), :] * w_ref[pl.ds(j, 1), :]
    return acc


def conv_fwd(proj, mix, w, b, ln_g, ln_b, *, tb, name):
    T = proj.shape[0]
    hb = tb // HALO

    def body(u_ref, g_ref, up_ref, gp_ref, w_ref, b_ref, lg_ref, lb_ref, mix_ref, o_ref, hbuf):
        i = pl.program_id(0)
        hprev = up_ref[...] * _sigmoid(gp_ref[...])
        hbuf[0:HALO, :] = hprev * jnp.where(i > 0, 1.0, 0.0)
        hbuf[HALO:HALO + tb, :] = u_ref[...] * _sigmoid(g_ref[...])
        y = _conv_taps(hbuf, w_ref, HALO - (CONV_K - 1), tb) + b_ref[...]
        mu = jnp.mean(y, axis=-1, keepdims=True)
        yc = y - mu
        var = jnp.mean(yc * yc, axis=-1, keepdims=True)
        z = yc * lax.rsqrt(var + EPS) * lg_ref[...] + lb_ref[...]
        o_ref[...] = (z * _sigmoid(z)).astype(MXU)

    vec = pl.BlockSpec((1, 512), lambda i: (0, 0))
    return pl.pallas_call(
        body, name=name, grid=(T // tb,),
        in_specs=[pl.BlockSpec((tb, 512), lambda i: (i, 3)), pl.BlockSpec((tb, 512), lambda i: (i, 4)),
                  pl.BlockSpec((HALO, 512), lambda i: (jnp.maximum(i * hb - 1, 0), 3)),
                  pl.BlockSpec((HALO, 512), lambda i: (jnp.maximum(i * hb - 1, 0), 4)),
                  pl.BlockSpec((HALO, 512), lambda i: (0, 0)), vec, vec, vec, ANY],
        out_specs=pl.BlockSpec((tb, 512), lambda i: (i, 1)),
        out_shape=S(mix.shape, mix.dtype),
        scratch_shapes=[pltpu.VMEM((tb + HALO, 512), f32)],
        compiler_params=_cp("arbitrary"), input_output_aliases={8: 0},
    )(proj, proj, proj, proj, w, b, ln_g, ln_b, mix)


def conv_bwd(proj, dmix, w, b, ln_g, ln_b, *, tb, name, comm=None):
    T = proj.shape[0]
    hb = tb // HALO
    ni = T // tb
    last_h = T // HALO - 1
    ext = tb + HALO

    def body(u_ref, g_ref, up_ref, gp_ref, un_ref, gn_ref, d_ref, dn_ref, w_ref, b_ref, lg_ref, lb_ref,
             du_ref, dg_ref, dw_ref, db_ref, dlg_ref, dlb_ref, hbuf, dybuf):
        i = pl.program_id(0)
        hbuf[0:HALO, :] = up_ref[...] * _sigmoid(gp_ref[...]) * jnp.where(i > 0, 1.0, 0.0)
        u = u_ref[...]
        sg = _sigmoid(g_ref[...])
        hbuf[HALO:HALO + tb, :] = u * sg
        hbuf[HALO + tb:HALO + ext, :] = un_ref[...] * _sigmoid(gn_ref[...])
        y = _conv_taps(hbuf, w_ref, HALO - (CONV_K - 1), ext) + b_ref[...]
        mu = jnp.mean(y, axis=-1, keepdims=True)
        yc = y - mu
        rstd = lax.rsqrt(jnp.mean(yc * yc, axis=-1, keepdims=True) + EPS)
        yn = yc * rstd
        z = yn * lg_ref[...] + lb_ref[...]
        sz = _sigmoid(z)
        row = lax.broadcasted_iota(jnp.int32, (ext, 1), 0)
        own = row < tb
        keep = row < jnp.where(i < ni - 1, ext, tb)
        dout = jnp.concatenate([d_ref[...], dn_ref[...]], axis=0)
        dz = jnp.where(keep, dout * (sz * (1.0 + z * (1.0 - sz))), 0.0)
        dyn = dz * lg_ref[...]
        dy = rstd * (dyn - jnp.mean(dyn, axis=-1, keepdims=True) - yn * jnp.mean(dyn * yn, axis=-1, keepdims=True))
        dybuf[...] = dy
        dz_own = jnp.where(own, dz, 0.0)
        dlg = jnp.sum(dz_own * yn, axis=0, keepdims=True)
        dlb = jnp.sum(dz_own, axis=0, keepdims=True)
        dy_own = dybuf[0:tb, :]
        dbias = jnp.sum(dy_own, axis=0, keepdims=True)
        dh = dybuf[pl.ds(CONV_K - 1, tb), :] * w_ref[pl.ds(0, 1), :]
        for j in range(1, CONV_K):
            dh += dybuf[pl.ds(CONV_K - 1 - j, tb), :] * w_ref[pl.ds(j, 1), :]
        du_ref[...] = (dh * sg).astype(MXU)
        dg_ref[...] = (dh * u * sg * (1.0 - sg)).astype(MXU)
        taps = [jnp.sum(dy_own * hbuf[pl.ds(HALO - (CONV_K - 1) + j, tb), :], axis=0, keepdims=True)
                for j in range(CONV_K)]
        taps.append(jnp.zeros((1, 512), f32))
        dwt = jnp.concatenate(taps, axis=0)

        @pl.when(i == 0)
        def _():
            dw_ref[...] = dwt
            db_ref[...] = dbias
            dlg_ref[...] = dlg
            dlb_ref[...] = dlb

        @pl.when(i > 0)
        def _():
            dw_ref[...] += dwt
            db_ref[...] += dbias
            dlg_ref[...] += dlg
            dlb_ref[...] += dlb

    vec = pl.BlockSpec((1, 512), lambda i: (0, 0))
    wspec = pl.BlockSpec((HALO, 512), lambda i: (0, 0))

    def halo_prev(col):
        return pl.BlockSpec((HALO, 512), lambda i: (jnp.maximum(i * hb - 1, 0), col))

    def halo_next(col):
        return pl.BlockSpec((HALO, 512), lambda i: (jnp.minimum((i + 1) * hb, last_h), col))

    row = pl.BlockSpec((tb, 512), lambda i: (i, 0))
    return _pallas(
        body, comm=comm, name=name, grid=(ni,),
        in_specs=[pl.BlockSpec((tb, 512), lambda i: (i, 3)), pl.BlockSpec((tb, 512), lambda i: (i, 4)),
                  halo_prev(3), halo_prev(4), halo_next(3), halo_next(4),
                  pl.BlockSpec((tb, 512), lambda i: (i, 1)), halo_next(1), wspec, vec, vec, vec],
        out_specs=[row, row, wspec, vec, vec, vec],
        out_shape=[S((T, 512), MXU), S((T, 512), MXU), S((HALO, 512), f32)] + [S((1, 512), f32)] * 3,
        scratch_shapes=[pltpu.VMEM((tb + 2 * HALO, 512), f32), pltpu.VMEM((ext, 512), f32)],
        compiler_params=_cp("arbitrary"),
    )(proj, proj, proj, proj, proj, proj, dmix, dmix, w, b, ln_g, ln_b)


def _qk_norm_bwd(v, gain, dout, e_ref):
    r = lax.rsqrt(_seg_sum(v * v, e_ref) * (1.0 / HD) + EPS)
    y = v * r
    dgain = jnp.sum(dout * y, axis=0, keepdims=True)
    dy = dout * gain
    dv = r * (dy - y * (_seg_sum(dy * y, e_ref) * (1.0 / HD)))
    return dv, dgain


def prep_bwd(proj, e, gains, da, dc, dconv, *, tm, name):
    T = proj.shape[0]

    def body(*refs):
        p_ref, e_ref, gaq, gak, gcq, gck = refs[0:6]
        a_refs = refs[6:15]
        dcq, dckk, dcvv, du, dgt = refs[15:20]
        dp, gaq_o, gak_o, gcq_o, gck_o = refs[20:]
        i = pl.program_id(0)
        dq = a_refs[0][...] + a_refs[3][...] + a_refs[6][...]
        dk = a_refs[1][...] + a_refs[4][...] + a_refs[7][...]
        dv = a_refs[2][...] + a_refs[5][...] + a_refs[8][...]
        d, g_aq = _qk_norm_bwd(p_ref[:, 0:512], gaq[...], dq, e_ref)
        dp[:, 0:512] = d.astype(MXU)
        d, g_ak = _qk_norm_bwd(p_ref[:, 512:1024], gak[...], dk, e_ref)
        dp[:, 512:1024] = d.astype(MXU)
        dp[:, 1024:1536] = dv.astype(MXU)
        dp[:, 1536:2048] = du[...]
        dp[:, 2048:2560] = dgt[...]
        d, g_cq0 = _qk_norm_bwd(p_ref[:, 2560:3072], gcq[...], dcq[:, 0:512], e_ref)
        dp[:, 2560:3072] = d.astype(MXU)
        d, g_cq1 = _qk_norm_bwd(p_ref[:, 3072:3584], gcq[...], dcq[:, 512:1024], e_ref)
        dp[:, 3072:3584] = d.astype(MXU)
        lo = lax.broadcasted_iota(jnp.int32, (tm, 128), 1) < HD

        def fold(ref):
            g0, g1 = ref[:, 0:128], ref[:, 128:256]
            s0 = g0 + pltpu.roll(g0, HD, 1)
            s1 = g1 + pltpu.roll(g1, HD, 1)
            return jnp.where(lo, s0, s1)

        dkn = fold(dckk)
        kraw = p_ref[:, 3584:3712]
        r = lax.rsqrt(_seg_sum128(kraw * kraw, e_ref) * (1.0 / HD) + EPS)
        y = kraw * r
        g_ck = jnp.sum(dkn * y, axis=0, keepdims=True)
        dy = dkn * gck[...]
        dp[:, 3584:3712] = (r * (dy - y * (_seg_sum128(dy * y, e_ref) * (1.0 / HD)))).astype(MXU)
        dp[:, 3712:3840] = fold(dcvv).astype(MXU)
        g_cq = jnp.concatenate([g_cq0, g_cq1], axis=1)

        @pl.when(i == 0)
        def _():
            gaq_o[...] = g_aq
            gak_o[...] = g_ak
            gcq_o[...] = g_cq
            gck_o[...] = g_ck

        @pl.when(i > 0)
        def _():
            gaq_o[...] += g_aq
            gak_o[...] += g_ak
            gcq_o[...] += g_cq
            gck_o[...] += g_ck

    def vec(n):
        return pl.BlockSpec((1, n), lambda i: (0, 0))

    def rows(n):
        return pl.BlockSpec((tm, n), lambda i: (i, 0))

    return pl.pallas_call(
        body, name=name, grid=(T // tm,),
        in_specs=[rows(IN_W), pl.BlockSpec((512, 512), lambda i: (0, 0)), vec(512), vec(512), vec(512), vec(128)]
        + [rows(512)] * 9 + [rows(1024), rows(256), rows(256), rows(512), rows(512)],
        out_specs=[rows(IN_W), vec(512), vec(512), vec(1024), vec(128)],
        out_shape=[S((T, IN_W), MXU), S((1, 512), f32), S((1, 512), f32), S((1, 1024), f32), S((1, 128), f32)],
        compiler_params=_cp("arbitrary"),
    )(proj, e, *gains, *da, *dc, *dconv)


def adamw(w, m, v, pieces, *, tr, name, comm=None):
    n, R, C = w.shape
    c1 = 1.0 - ADAM_B1 ** ADAM_STEP
    c2 = 1.0 - ADAM_B2 ** ADAM_STEP
    npc = len(pieces)

    def body(*refs):
        w_ref, m_ref, v_ref = refs[0:3]
        p_refs = refs[3:3 + npc]
        g_ref, d_ref, mo_ref, vo_ref = refs[3 + npc:]
        g = p_refs[0][...].astype(f32)
        for p in p_refs[1:]:
            g = g + p[...].astype(f32)
        mn = ADAM_B1 * m_ref[...] + (1.0 - ADAM_B1) * g
        vn = ADAM_B2 * v_ref[...] + (1.0 - ADAM_B2) * (g * g)
        g_ref[...] = g
        mo_ref[...] = mn
        vo_ref[...] = vn
        d_ref[...] = -ADAM_LR * ((mn / c1) / (jnp.sqrt(vn / c2) + ADAM_EPS) + ADAM_WD * w_ref[...])

    blk = pl.BlockSpec((None, tr, C), lambda l, i: (l, i, 0))
    return _pallas(
        body, comm=comm, name=name, grid=(n, R // tr), in_specs=[blk] * (3 + npc), out_specs=[blk] * 4,
        out_shape=[S(w.shape, f32)] * 4, compiler_params=_cp("arbitrary", "arbitrary"),
    )(w, m, v, *pieces)


def add_halves(pieces, other, *, tr, name):
    _, _, r, cc = pieces.shape

    def body(c_ref, a_ref, b_ref, o_ref):
        o_ref[...] = (a_ref[...] + b_ref[...]).astype(jnp.bfloat16)

    blk = pl.BlockSpec((None, tr, cc), lambda s, i, c_ref: (s, i, 0))
    grid_spec = pltpu.PrefetchScalarGridSpec(
        num_scalar_prefetch=1, grid=(4, r // tr),
        in_specs=[pl.BlockSpec((None, None, tr, cc), lambda s, i, c_ref: (s, c_ref[0], i, 0)), blk], out_specs=blk)
    core = lax.axis_index("c").astype(jnp.int32).reshape(1)
    return pl.pallas_call(body, name=name, grid_spec=grid_spec, out_shape=S((4, r, cc), jnp.bfloat16),
                          compiler_params=_cp("arbitrary", "arbitrary"))(core, pieces, other)


def sum8(parts, *, name):
    _, R, C = parts.shape

    def body(p_ref, o_ref):
        acc = p_ref[0]
        for d in range(1, 8):
            acc = acc + p_ref[d]
        o_ref[...] = acc

    return pl.pallas_call(body, name=name, out_shape=S((R, C), f32))(parts)


def _pos():
    return lax.axis_index("x"), lax.axis_index("y"), lax.axis_index("c")


def _other_chips(x, y):
    return [(1 - x, y), (x, 1 - y), (1 - x, 1 - y)]


class GatherComm:
    def __init__(self, shards, in_place=None):
        self.ins = list(shards)
        self.nt = nt = len(shards)
        self.in_place = list(in_place) if in_place is not None else [False] * nt
        self.out_shapes = [S((2 * s.shape[1], 4 * s.shape[2]), s.dtype) if ip else S((4,) + s.shape, s.dtype)
                           for s, ip in zip(shards, self.in_place)]
        self.sem_shapes = [pltpu.SemaphoreType.DMA((nt, 6)), pltpu.SemaphoreType.DMA((nt, 6)),
                           pltpu.SemaphoreType.DMA((nt, 2))]
        self.results = None

    def _place(self, couts, t, chip, half):
        cid = 2 * chip[0] + chip[1]
        if not self.in_place[t]:
            return couts[t].at[cid, half]
        _, r, c = self.ins[t].shape
        row0 = half * r if isinstance(half, int) else pl.multiple_of(half * r, 16)
        return couts[t].at[pl.ds(row0, r), pl.ds(pl.multiple_of(cid * c, 128), c)]

    def _copy(self, couts, sems, t, k, chip, half, to, src=None):
        dst = self._place(couts, t, chip, half)
        return pltpu.make_async_remote_copy(
            src_ref=dst if src is None else src, dst_ref=dst,
            send_sem=sems[0].at[t, k], recv_sem=sems[1].at[t, k], device_id=to, device_id_type=MESH)

    def _local(self, cins, couts, sems, t):
        x, y, _ = _pos()
        return [pltpu.make_async_copy(cins[t].at[half], self._place(couts, t, (x, y), half), sems[2].at[t, half])
                for half in range(2)]

    def start(self, cins, couts, sems):
        x, y, c = _pos()
        for t in range(self.nt):
            for cp in self._local(cins, couts, sems, t):
                cp.start()
            for j, chip in enumerate(_other_chips(x, y)):
                self._copy(couts, sems, t, j, (x, y), c, (*chip, c), src=cins[t].at[c]).start()

    def mid(self, cins, couts, sems):
        x, y, c = _pos()
        for t in range(self.nt):
            for j, chip in enumerate(_other_chips(x, y)):
                self._copy(couts, sems, t, j, chip, c, (x, y, c)).wait_recv()
                self._copy(couts, sems, t, 3 + j, chip, c, (x, y, 1 - c)).start()

    def finish(self, cins, couts, sems):
        x, y, c = _pos()
        for t in range(self.nt):
            for j, chip in enumerate(_other_chips(x, y)):
                self._copy(couts, sems, t, 3 + j, chip, 1 - c, (x, y, c)).wait_recv()
        for t in range(self.nt):
            for j, chip in enumerate(_other_chips(x, y)):
                self._copy(couts, sems, t, j, (x, y), c, (*chip, c), src=cins[t].at[c]).wait_send()
                self._copy(couts, sems, t, 3 + j, chip, c, (x, y, 1 - c)).wait_send()
            for cp in self._local(cins, couts, sems, t):
                cp.wait()


class SwapComm:
    def __init__(self, pieces):
        self.ins = list(pieces)
        self.nt = nt = len(pieces)
        self.out_shapes = [S((4,) + p.shape[2:], p.dtype) for p in pieces]
        self.sem_shapes = [pltpu.SemaphoreType.DMA((nt, 4)), pltpu.SemaphoreType.DMA((nt, 4))]
        self.results = None

    def _copies(self, cins, couts, sems):
        x, y, c = _pos()
        return [pltpu.make_async_remote_copy(src_ref=cins[t].at[s, 1 - c], dst_ref=couts[t].at[s],
                                             send_sem=sems[0].at[t, s], recv_sem=sems[1].at[t, s],
                                             device_id=(x, y, 1 - c), device_id_type=MESH)
                for t in range(self.nt) for s in range(4)]

    def start(self, cins, couts, sems):
        for cp in self._copies(cins, couts, sems):
            cp.start()

    def mid(self, cins, couts, sems):
        pass

    def finish(self, cins, couts, sems):
        for cp in self._copies(cins, couts, sems):
            cp.wait()


class ExchangeComm:
    def __init__(self, arrs):
        self.ins = list(arrs)
        self.nt = nt = len(arrs)
        self.out_shapes = [S((2,) + a.shape, a.dtype) for a in arrs]
        self.sem_shapes = [pltpu.SemaphoreType.DMA((nt, 7)), pltpu.SemaphoreType.DMA((nt, 7)),
                           pltpu.SemaphoreType.DMA((nt,))]
        self.results = None

    def _copy(self, couts, sems, t, k, half, src_chip, to, src=None):
        dst = couts[t].at[half, src_chip]
        return pltpu.make_async_remote_copy(
            src_ref=dst if src is None else src, dst_ref=dst,
            send_sem=sems[0].at[t, k], recv_sem=sems[1].at[t, k], device_id=to, device_id_type=MESH)

    def _local(self, cins, couts, sems, t):
        x, y, c = _pos()
        return pltpu.make_async_copy(cins[t].at[2 * x + y], couts[t].at[c, 2 * x + y], sems[2].at[t])

    def _firsts(self, cins, couts, sems, t):
        x, y, c = _pos()
        me = 2 * x + y
        cps = [self._copy(couts, sems, t, j, c, me, (*chip, c), src=cins[t].at[2 * chip[0] + chip[1]])
               for j, chip in enumerate(_other_chips(x, y))]
        return cps + [self._copy(couts, sems, t, 6, c, me, (x, y, 1 - c), src=cins[t].at[me])]

    def start(self, cins, couts, sems):
        for t in range(self.nt):
            self._local(cins, couts, sems, t).start()
            for cp in self._firsts(cins, couts, sems, t):
                cp.start()

    def mid(self, cins, couts, sems):
        x, y, c = _pos()
        for t in range(self.nt):
            for j, chip in enumerate(_other_chips(x, y)):
                cid = 2 * chip[0] + chip[1]
                self._copy(couts, sems, t, j, c, cid, (x, y, c)).wait_recv()
                self._copy(couts, sems, t, 3 + j, c, cid, (x, y, 1 - c)).start()

    def finish(self, cins, couts, sems):
        x, y, c = _pos()
        for t in range(self.nt):
            for j, chip in enumerate(_other_chips(x, y)):
                self._copy(couts, sems, t, 3 + j, 1 - c, 2 * chip[0] + chip[1], (x, y, c)).wait_recv()
            self._copy(couts, sems, t, 6, 1 - c, 2 * x + y, (x, y, c)).wait_recv()
        for t in range(self.nt):
            for cp in self._firsts(cins, couts, sems, t):
                cp.wait_send()
            for j, chip in enumerate(_other_chips(x, y)):
                self._copy(couts, sems, t, 3 + j, c, 2 * chip[0] + chip[1], (x, y, 1 - c)).wait_send()
            self._local(cins, couts, sems, t).wait()


def gather_small(vec, *, name):
    R, C = vec.shape

    def body(v_ref, out_ref, send_sems, recv_sems):
        x, y, c = _pos()
        me = 4 * x + 2 * y + c
        out_ref[me] = v_ref[...]
        cps = []
        def peer(k):
            fx, fy, fc = (k >> 2) & 1, (k >> 1) & 1, k & 1
            return (1 - x if fx else x), (1 - y if fy else y), (1 - c if fc else c)

        for k in range(1, 8):
            cp = pltpu.make_async_remote_copy(src_ref=v_ref, dst_ref=out_ref.at[me], send_sem=send_sems.at[k - 1],
                                              recv_sem=recv_sems.at[k - 1], device_id=peer(k), device_id_type=MESH)
            cp.start()
            cps.append(cp)
        for k in range(1, 8):
            px, py, pc = peer(k)
            pltpu.make_async_remote_copy(src_ref=v_ref, dst_ref=out_ref.at[4 * px + 2 * py + pc],
                                         send_sem=send_sems.at[k - 1], recv_sem=recv_sems.at[k - 1],
                                         device_id=(px, py, pc), device_id_type=MESH).wait_recv()
        for cp in cps:
            cp.wait_send()

    return pl.pallas_call(
        body, name=name,
        in_specs=[pl.BlockSpec(memory_space=pltpu.VMEM)], out_specs=pl.BlockSpec(memory_space=pltpu.VMEM),
        out_shape=S((8, R, C), vec.dtype),
        scratch_shapes=[pltpu.SemaphoreType.DMA((7,)), pltpu.SemaphoreType.DMA((7,))],
    )(vec)


def _tile(n, prefs):
    for p in prefs:
        if n % p == 0:
            return p
    return n


def _lanes(g, reps):
    return jnp.tile(g.reshape(1, -1), (1, reps))


class _NoRide:
    def rider(self, name):
        return None

    def landed(self, comm):
        pass

    def grad(self, name, val):
        pass


def _layer_fwd(x, p, e, ride=_NoRide()):
    T, D = x.shape
    tm = _tile(T, (512, 256, 128))
    tmm = _tile(T, (1024, 512, 256, 128))

    def carried(fn, *args, name, **kw):
        comm = ride.rider(name)
        out = fn(*args, name=name, comm=comm, **kw)
        ride.landed(comm)
        return out

    h, proj = carried(rms_proj, x, p["norm1_g"], p["w_in"], tm=tmm, tn=_tile(IN_W, (768,)), name="rms_proj")
    gains = (_lanes(p["a_q_g"], 8), _lanes(p["a_k_g"], 8), _lanes(p["c_q_g"], 8), _lanes(p["c_k_g"], 2))
    aq, ak, av, cq, ckk, cvv = prep_fwd(proj, e, gains, tm=_tile(T, (256, 128)), name="prep_fwd")
    ols = []
    for d in DILATIONS:
        ols += carried(band_attn_fwd, aq, ak, av, None, dil=d, max_dist=A_DIST, group=1, name=f"dil_attn_fwd_{d}")
    mix = dil_combine_fwd(ols, tm=tm, name="dil_combine_fwd")
    mix = conv_fwd(proj, mix, p["conv_w"], p["conv_b"], p["conv_ln_g"], p["conv_ln_b"], tb=tm, name="conv_fwd")
    sinks = jnp.repeat(p["c_sinks"].reshape(-1), HD).reshape(1, C_W)
    o_c, l_c, mix = carried(band_attn_fwd, cq, ckk, cvv, sinks, dil=1, max_dist=C_DIST, group=4, mix=mix,
                            name="swa_attn_fwd")
    x1 = carried(matmul_res, mix, p["w_out"], x, tm=tmm, tn=_tile(D, (1024, 512, 256)), name="out_proj")
    F = p["w_gate"].shape[1]
    h2, gate, up, act = carried(rms_swiglu, x1, p["norm2_g"], p["w_gate"], p["w_up"], tm=tmm,
                                tn=_tile(F, (512, 256, 128)), name="rms_swiglu")
    x2 = carried(matmul_res, act, p["w_down"], x1, tm=tmm, tn=_tile(D, (512, 256)), name="ffn_down")
    saved = dict(x=x, h=h, proj=proj, gains=gains, aq=aq, ak=ak, av=av, cq=cq, ckk=ckk, cvv=cvv, ols=ols, o_c=o_c,
                 l_c=l_c, sinks=sinks, mix=mix, x1=x1, h2=h2, gate=gate, up=up, act=act)
    return x2, saved


def _layer_bwd(dx2, dx2b, p, s, e, ride=_NoRide()):
    T, D = dx2.shape
    F = p["w_gate"].shape[1]
    tm = _tile(T, (512, 256, 128))
    tmm = _tile(T, (1024, 512, 256, 128))
    tkT = _tile(T, (2048, 1024, 512))
    tF = _tile(F, (512, 256, 128))
    tD = _tile(D, (1024, 512, 256))
    g = {}

    def carried(fn, *args, name, **kw):
        comm = ride.rider(name)
        out = fn(*args, name=name, comm=comm, **kw)
        ride.landed(comm)
        return out

    def big(n, val):
        g[n] = val
        ride.grad(n, val)

    d_gate, d_up = carried(nt_swiglu_bwd, dx2b, p["w_down"], s["gate"], s["up"], tm=tmm, tn=tF, name="ffn_down_bwd")
    big("w_down", tn_matmul(s["act"], dx2b, tm=tF, tn=tD, tk=tkT, name="grad_w_down"))
    big("w_gate", carried(tn_matmul, s["h2"], d_gate, tm=tD, tn=tF, tk=tkT, by_chip=True, name="grad_w_gate"))
    big("w_up", carried(tn_matmul, s["h2"], d_up, tm=tD, tn=tF, tk=tkT, by_chip=True, name="grad_w_up"))
    dx1, dx1b, g["norm2_g"] = carried(nt_rms_bwd, [(d_gate, p["w_gate"]), (d_up, p["w_up"])], s["x1"], p["norm2_g"],
                                      dx2, tm=tmm, tk=tF, name="ffn_in_bwd")
    dmix = nt_plain(dx1b, p["w_out"], tm=tmm, tn=tD, name="out_proj_bwd")
    big("w_out", tn_matmul(s["mix"], dx1b, tm=1024, tn=tD, tk=tkT, name="grad_w_out"))
    dos = dil_combine_bwd(s["ols"], dmix, e, tm=tm, name="dil_combine_bwd")
    da = []
    for n, d in enumerate(DILATIONS):
        da += carried(band_attn_bwd, s["aq"], s["ak"], s["av"], s["ols"][2 * n + 1], dos[n], dos[3 + n], dil=d,
                      max_dist=A_DIST, group=1, name=f"dil_attn_bwd_{d}")
    du, dgt, gw, gb, glg, glb = carried(conv_bwd, s["proj"], dmix, p["conv_w"], p["conv_b"], p["conv_ln_g"],
                                        p["conv_ln_b"], tb=tm, name="conv_bwd")
    g["conv_w"], g["conv_b"], g["conv_ln_g"], g["conv_ln_b"] = gw[:CONV_K], gb, glg, glb
    do_c, dd_c, dsink = swa_pre_bwd(s["o_c"], s["l_c"], dmix, s["sinks"], e, tm=_tile(T, (256, 128)), name="swa_pre_bwd")
    g["c_sinks"] = dsink.reshape(-1, HD)[:, 0]
    dcq, dckk, dcvv = carried(band_attn_bwd, s["cq"], s["ckk"], s["cvv"], s["l_c"], do_c, dd_c, dil=1, max_dist=C_DIST,
                              group=4, name="swa_attn_bwd")
    dproj, gaq, gak, gcq, gck = prep_bwd(s["proj"], e, s["gains"], da, (dcq, dckk, dcvv), (du, dgt),
                                         tm=_tile(T, (256, 128)), name="prep_bwd")
    g["a_q_g"] = gaq.reshape(-1, HD).sum(0)
    g["a_k_g"] = gak.reshape(-1, HD).sum(0)
    g["c_q_g"] = gcq.reshape(-1, HD).sum(0)
    g["c_k_g"] = gck.reshape(-1, HD).sum(0)
    big("w_in", tn_matmul(s["h"], dproj, tm=tD, tn=_tile(IN_W, (1280,)), tk=tkT, name="grad_w_in"))
    dx, dxb, g["norm1_g"] = carried(nt_rms_bwd, [(dproj, p["w_in"])], s["x"], p["norm1_g"], dx1, tm=tmm,
                                    tk=_tile(IN_W, (768,)), name="in_proj_bwd")
    return dx, dxb, g


BIG = ("w_in", "w_out", "w_gate", "w_up", "w_down")
COL_SHARDED = ("w_in", "w_gate", "w_up")
SMALL = ("norm1_g", "a_q_g", "a_k_g", "conv_w", "conv_b", "conv_ln_g", "conv_ln_b", "c_q_g", "c_k_g", "c_sinks", "norm2_g")


def _to_pieces(g, name):
    if g.ndim == 3:
        return g.reshape(4, 2, g.shape[1] // 2, g.shape[2])
    R, C = g.shape
    if name in COL_SHARDED:
        return g.reshape(2, R // 2, 4, C // 4).transpose(2, 0, 1, 3)
    return g.reshape(4, 2, R // 8, C)


def _from_gathered(w, name):
    _, _, r, c = w.shape
    if name in COL_SHARDED:
        return w.transpose(1, 2, 0, 3).reshape(2 * r, 4 * c)
    return w.reshape(8 * r, c)


def _shard(W, l, n):
    w = W[n][l]
    return w.astype(MXU).reshape(2, w.shape[0] // 2, w.shape[1])


class _LayerParams(dict):
    def __init__(self, layer, full, small):
        super().__init__(small)
        self.layer, self.full = layer, full

    def __missing__(self, n):
        return self.full[(self.layer, n)]


FWD_RIDES = {
    (0, "rms_proj"): ((0, "w_out"), (0, "w_gate")),
    (0, "swa_attn_fwd"): ((0, "w_up"),),
    (0, "rms_swiglu"): ((0, "w_down"), (1, "w_in"), (1, "w_out")),
    (0, "ffn_down"): ((1, "w_gate"),),
    (1, "rms_proj"): ((1, "w_up"),),
    (1, "rms_swiglu"): ((1, "w_down"),),
}
BWD_RIDES = {
    "grad_w_gate": (("w_down",), ()),
    "grad_w_up": (("w_gate",), ("w_down",)),
    "ffn_in_bwd": (("w_up",), ("w_gate",)),
    "dil_attn_bwd_1": (("w_out",), ()),
    "dil_attn_bwd_16": ((), ("w_up",)),
    "conv_bwd": ((), ("w_out",)),
    "in_proj_bwd": (("w_in",), ()),
}
IN_PLACE = ("w_gate", "w_up")


class _FwdRide:
    def __init__(self, layer, W, full):
        self.layer, self.W, self.full = layer, W, full

    def rider(self, name):
        keys = FWD_RIDES.get((self.layer, name))
        if not keys:
            return None
        comm = GatherComm([_shard(self.W, l, n) for l, n in keys], [n in IN_PLACE for _, n in keys])
        comm.keys = keys
        return comm

    def landed(self, comm):
        if comm is not None:
            for (l, n), g in zip(comm.keys, comm.results):
                self.full[(l, n)] = g if n in IN_PLACE else _from_gathered(g, n)


class _GradFlow:
    def __init__(self):
        self.pieces, self.sums, self.landed, self.pending = {}, {}, {}, []

    def swap(self, keys):
        comm = SwapComm([self.pieces[k] for k in keys])
        comm.keys, comm.kind = list(keys), "swap"
        return comm

    def exchange(self, keys):
        if not keys:
            return None
        comm = ExchangeComm([self.sums[k] for k in keys])
        comm.keys, comm.kind = list(keys), "exchange"
        return comm

    def take_pending(self):
        keys, self.pending = self.pending, []
        return keys

    def land(self, comm):
        if comm is None:
            return
        if isinstance(comm, MultiComm):
            for sub in comm.comms:
                self.land(sub)
            return
        for k, res in zip(comm.keys, comm.results):
            if comm.kind == "swap":
                r = res.shape[1]
                self.sums[k] = add_halves(self.pieces[k], res, tr=_tile(r, (256, 176, 128, 64, 32, 16)),
                                          name="grad_chip_sum")
                if k[1] == "w_in":
                    self.pending.append(k)
            else:
                self.landed[k] = res


class _BwdRide:
    def __init__(self, layer, flow):
        self.layer, self.flow = layer, flow

    def grad(self, name, val):
        self.flow.pieces[(self.layer, name)] = _to_pieces(val, name)

    def rider(self, name):
        if name == "ffn_down_bwd":
            return self.flow.exchange(self.flow.take_pending())
        swaps, exchanges = BWD_RIDES.get(name, ((), ()))
        comms = []
        if swaps:
            comms.append(self.flow.swap([(self.layer, n) for n in swaps]))
        if exchanges:
            comms.append(self.flow.exchange([(self.layer, n) for n in exchanges]))
        return MultiComm(comms) if comms else None

    def landed(self, comm):
        self.flow.land(comm)


def _pack(items, rows):
    flat = jnp.concatenate([a.reshape(-1).astype(f32) for a in items])
    return jnp.pad(flat, (0, rows * 128 - flat.shape[0])).reshape(rows, 128)


def _unpack(packed, shapes):
    flat = packed.reshape(-1)
    out, off = [], 0
    for shp in shapes:
        n = 1
        for d in shp:
            n *= d
        out.append(flat[off:off + n].reshape(shp))
        off += n
    return out


def kernel(x, norm1_g, w_in, a_q_g, a_k_g, conv_w, conv_b, conv_ln_g, conv_ln_b, c_q_g, c_k_g, c_sinks, w_out, norm2_g, w_gate, w_up, w_down, loss_target, m_norm1_g, m_w_in, m_a_q_g, m_a_k_g, m_conv_w, m_conv_b, m_conv_ln_g, m_conv_ln_b, m_c_q_g, m_c_k_g, m_c_sinks, m_w_out, m_norm2_g, m_w_gate, m_w_up, m_w_down, v_norm1_g, v_w_in, v_a_q_g, v_a_k_g, v_conv_w, v_conv_b, v_conv_ln_g, v_conv_ln_b, v_c_q_g, v_c_k_g, v_c_sinks, v_w_out, v_norm2_g, v_w_gate, v_w_up, v_w_down):
    W = dict(norm1_g=norm1_g, w_in=w_in, a_q_g=a_q_g, a_k_g=a_k_g, conv_w=conv_w, conv_b=conv_b, conv_ln_g=conv_ln_g,
             conv_ln_b=conv_ln_b, c_q_g=c_q_g, c_k_g=c_k_g, c_sinks=c_sinks, w_out=w_out, norm2_g=norm2_g, w_gate=w_gate,
             w_up=w_up, w_down=w_down)
    M = dict(norm1_g=m_norm1_g, w_in=m_w_in, a_q_g=m_a_q_g, a_k_g=m_a_k_g, conv_w=m_conv_w, conv_b=m_conv_b,
             conv_ln_g=m_conv_ln_g, conv_ln_b=m_conv_ln_b, c_q_g=m_c_q_g, c_k_g=m_c_k_g, c_sinks=m_c_sinks, w_out=m_w_out,
             norm2_g=m_norm2_g, w_gate=m_w_gate, w_up=m_w_up, w_down=m_w_down)
    V = dict(norm1_g=v_norm1_g, w_in=v_w_in, a_q_g=v_a_q_g, a_k_g=v_a_k_g, conv_w=v_conv_w, conv_b=v_conv_b,
             conv_ln_g=v_conv_ln_g, conv_ln_b=v_conv_ln_b, c_q_g=v_c_q_g, c_k_g=v_c_k_g, c_sinks=v_c_sinks, w_out=v_w_out,
             norm2_g=v_norm2_g, w_gate=v_w_gate, w_up=v_w_up, w_down=v_w_down)
    depth = norm1_g.shape[0]
    T, D = x.shape[1], x.shape[2]
    xs = x.reshape(T, D)
    chip = 2 * lax.axis_index("x") + lax.axis_index("y")
    e = _head_eye()

    full = {}
    first = GatherComm([_shard(W, 0, "w_in"), conv_w])
    _run_comm(first, name="gather_first")
    full[(0, "w_in")] = _from_gathered(first.results[0], "w_in")
    conv_full = first.results[1].transpose(1, 2, 0, 3).reshape(depth, CONV_K, B_W)
    params = []
    for l in range(depth):
        small = {n: W[n][l].reshape(1, -1) for n in SMALL if n != "conv_w"}
        small["conv_w"] = jnp.pad(conv_full[l], ((0, HALO - CONV_K), (0, 0)))
        params.append(_LayerParams(l, full, small))

    saved = []
    act = xs
    for l in range(depth):
        act, s = _layer_fwd(act, params[l], e, _FwdRide(l, W, full))
        saved.append(s)
    dy, dyb, loss_part = loss_head(act, loss_target.reshape(T, D), tm=_tile(T, (512, 256, 128)), name="loss_head")
    grads = [None] * depth
    flow = _GradFlow()
    for l in reversed(range(depth)):
        dy, dyb, grads[l] = _layer_bwd(dy, dyb, params[l], saved[l], e, _BwdRide(l, flow))
    grad_x = dy.reshape(x.shape)

    out = {}
    for n in ("w_down", "w_gate", "w_up", "w_out", "w_in"):
        last = flow.exchange(flow.take_pending())
        per_layer = [flow.landed[(l, n)] for l in range(depth)]
        r, cc = per_layer[0].shape[2], per_layer[0].shape[3]
        srcs = [jnp.stack([pl_[:, s].reshape(2 * r, cc) for pl_ in per_layer]) for s in range(4)]
        out[n] = adamw(W[n], M[n], V[n], srcs, tr=_tile(2 * r, (256, 176, 128, 64, 32, 16)), name="adamw_" + n,
                       comm=last)
        flow.land(last)

    small_shapes = []
    items = []
    for l in range(depth):
        for n in SMALL:
            a = grads[l][n]
            if n == "conv_w":
                a = a.reshape(CONV_K, 4, B_W // 4).transpose(1, 0, 2)
            items.append(a)
            small_shapes.append(a.shape)
    items.append(loss_part[0, 0:1])
    small_shapes.append((1,))
    total = sum(int(jnp.size(a)) for a in items)
    rows = -(-total // 1024) * 8
    summed = sum8(gather_small(_pack(items, rows), name="gather_small"), name="sum_small")
    parts = _unpack(summed, small_shapes)
    loss = parts[-1][0]
    small_g = {n: [] for n in SMALL}
    for l in range(depth):
        for i, n in enumerate(SMALL):
            a = parts[l * len(SMALL) + i]
            if n == "conv_w":
                a = lax.dynamic_index_in_dim(a, chip, axis=0, keepdims=False)
            small_g[n].append(a.reshape(W[n].shape[1:]))
    sw = [W[n] for n in SMALL]
    sm = [M[n] for n in SMALL]
    sv = [V[n] for n in SMALL]
    sg = [jnp.stack(small_g[n]) for n in SMALL]
    tot2 = sum(int(jnp.size(a)) for a in sw)
    rows2 = -(-tot2 // 1024) * 8
    res = adamw(_pack(sw, rows2)[None], _pack(sm, rows2)[None], _pack(sv, rows2)[None], [_pack(sg, rows2)[None]],
                tr=rows2, name="adamw_small")
    shapes2 = [a.shape for a in sw]
    small_out = [_unpack(r[0], shapes2) for r in res]
    for i, n in enumerate(SMALL):
        out[n] = [small_out[k][i] for k in range(4)]

    order = ("norm1_g", "w_in", "a_q_g", "a_k_g", "conv_w", "conv_b", "conv_ln_g", "conv_ln_b", "c_q_g", "c_k_g",
             "c_sinks", "w_out", "norm2_g", "w_gate", "w_up", "w_down")
    return (loss, grad_x, *[out[n][0] for n in order], *[out[n][1] for n in order], *[out[n][2] for n in order],
            *[out[n][3] for n in order])
```

```python
import functools

import jax
import jax.numpy as jnp
from jax import lax
from jax.experimental import pallas as pl
from jax.experimental.pallas import tpu as pltpu

f32 = jnp.float32
MXU = jnp.bfloat16
S = jax.ShapeDtypeStruct
MESH = pl.DeviceIdType.MESH

EPS = 1e-6
NEG = -1e30
HD = 64
BLK = 128
A_W, B_W, C_W = 512, 512, 1024
KV_W = 128
IN_W = 3 * A_W + 2 * B_W + C_W + 2 * KV_W
CONV_K = 31
HALO = 32
DILATIONS = (1, 4, 16)
A_DIST, C_DIST = 128, 127
SCALE = HD ** -0.5
VMEM_LIMIT = 56 * 1024 * 1024
VMEM_TALL = 62 * 1024 * 1024

ADAM_LR, ADAM_B1, ADAM_B2, ADAM_EPS, ADAM_WD, ADAM_STEP = 0.001, 0.9, 0.999, 1e-08, 0.01, 10


def _cp(*sem, vmem=VMEM_LIMIT):
    return pltpu.CompilerParams(dimension_semantics=sem, vmem_limit_bytes=vmem)


ANY = pl.BlockSpec(memory_space=pl.ANY)


def _pallas(body, *, comm=None, name, grid, in_specs, out_specs, out_shape, scratch_shapes=(), compiler_params,
            input_output_aliases=None):
    aliases = dict(input_output_aliases or {})
    if comm is None:
        return pl.pallas_call(body, name=name, grid=grid, in_specs=in_specs, out_specs=out_specs, out_shape=out_shape,
                              scratch_shapes=list(scratch_shapes), compiler_params=compiler_params,
                              input_output_aliases=aliases)
    single = not isinstance(out_shape, (list, tuple))
    o_shapes = [out_shape] if single else list(out_shape)
    o_specs = [out_specs] if single else list(out_specs)
    n_in, n_out, n_sc = len(in_specs), len(o_shapes), len(scratch_shapes)
    nci, nco = len(comm.ins), len(comm.out_shapes)
    total = 1
    for g in grid:
        total *= g

    def carried(*refs):
        ins, cins = refs[:n_in], refs[n_in:n_in + nci]
        o0 = n_in + nci
        outs, couts = refs[o0:o0 + n_out], refs[o0 + n_out:o0 + n_out + nco]
        s0 = o0 + n_out + nco
        scratch, sems = refs[s0:s0 + n_sc], refs[s0 + n_sc:]
        step = pl.program_id(0)
        for axis in range(1, len(grid)):
            step = step * grid[axis] + pl.program_id(axis)

        @pl.when(step == 0)
        def _():
            comm.start(cins, couts, sems)

        body(*ins, *outs, *scratch)

        @pl.when(step == (3 * total) // 4)
        def _():
            comm.mid(cins, couts, sems)

        @pl.when(step == total - 1)
        def _():
            comm.finish(cins, couts, sems)

    call = pl.pallas_call(carried, name=name, grid=grid, in_specs=list(in_specs) + [ANY] * nci,
                          out_specs=o_specs + [ANY] * nco, out_shape=o_shapes + list(comm.out_shapes),
                          scratch_shapes=list(scratch_shapes) + list(comm.sem_shapes), compiler_params=compiler_params,
                          input_output_aliases=aliases)

    def run(*args):
        res = call(*args, *comm.ins)
        comm.results = list(res[n_out:])
        return res[0] if single else list(res[:n_out])

    return run


class MultiComm:
    def __init__(self, comms):
        self.comms = list(comms)
        self.ins = [a for c in self.comms for a in c.ins]
        self.out_shapes = [s for c in self.comms for s in c.out_shapes]
        self.sem_shapes = [s for c in self.comms for s in c.sem_shapes]

    def _each(self, cins, couts, sems):
        i = o = s = 0
        for c in self.comms:
            ni, no, ns = len(c.ins), len(c.out_shapes), len(c.sem_shapes)
            yield c, cins[i:i + ni], couts[o:o + no], sems[s:s + ns]
            i, o, s = i + ni, o + no, s + ns

    def start(self, cins, couts, sems):
        for c, a, b, d in self._each(cins, couts, sems):
            c.start(a, b, d)

    def mid(self, cins, couts, sems):
        for c, a, b, d in self._each(cins, couts, sems):
            c.mid(a, b, d)

    def finish(self, cins, couts, sems):
        for c, a, b, d in self._each(cins, couts, sems):
            c.finish(a, b, d)

    @property
    def results(self):
        return [r for c in self.comms for r in c.results]

    @results.setter
    def results(self, vals):
        o = 0
        for c in self.comms:
            c.results = list(vals[o:o + len(c.out_shapes)])
            o += len(c.out_shapes)


def _run_comm(comm, *, name):
    nci, nco = len(comm.ins), len(comm.out_shapes)

    def body(*refs):
        cins, couts, sems = refs[:nci], refs[nci:nci + nco], refs[nci + nco:]
        comm.start(cins, couts, sems)
        comm.mid(cins, couts, sems)
        comm.finish(cins, couts, sems)

    comm.results = list(pl.pallas_call(body, name=name, in_specs=[ANY] * nci, out_specs=[ANY] * nco,
                                       out_shape=list(comm.out_shapes), scratch_shapes=list(comm.sem_shapes))(*comm.ins))


def _nt(a, b):
    return lax.dot_general(a, b, (((1,), (1,)), ((), ())), preferred_element_type=f32)


def _tn(a, b):
    return lax.dot_general(a, b, (((0,), (0,)), ((), ())), preferred_element_type=f32)


def _nn(a, b):
    return jnp.dot(a, b, preferred_element_type=f32)


def _sigmoid(x):
    return 1.0 / (1.0 + jnp.exp(-x))


def _seg_sum(v, e_ref):
    hi = v.astype(jnp.bfloat16)
    lo = (v - hi.astype(f32)).astype(jnp.bfloat16)
    e = e_ref[...]
    return _nn(hi, e) + _nn(lo, e)


def _seg_sum128(v, e_ref):
    e = e_ref[0:128, 0:128]
    hi = v.astype(jnp.bfloat16)
    lo = (v - hi.astype(f32)).astype(jnp.bfloat16)
    return _nn(hi, e) + _nn(lo, e)


def _head_eye():
    r = lax.broadcasted_iota(jnp.int32, (512, 512), 0) // HD
    c = lax.broadcasted_iota(jnp.int32, (512, 512), 1) // HD
    return (r == c).astype(jnp.bfloat16)


def _rms_norm_rows(x_ref, g_ref, h_ref, tm):
    def chunk(c, carry):
        rows = pl.ds(c * BLK, BLK)
        xf = x_ref[rows, :]
        r = lax.rsqrt(jnp.mean(xf * xf, axis=-1, keepdims=True) + EPS)
        h_ref[rows, :] = (xf * r * g_ref[...]).astype(MXU)
        return carry
    lax.fori_loop(0, tm // BLK, chunk, 0)


def rms_proj(x, g, w, *, tm, tn, name, comm=None):
    T, D = x.shape
    N = w.shape[1]

    def body(x_ref, g_ref, w_ref, h_ref, o_ref):
        @pl.when(pl.program_id(1) == 0)
        def _():
            _rms_norm_rows(x_ref, g_ref, h_ref, tm)
        o_ref[...] = _nn(h_ref[...], w_ref[...])

    return _pallas(
        body, comm=comm, name=name, grid=(T // tm, N // tn),
        in_specs=[pl.BlockSpec((tm, D), lambda i, j: (i, 0)), pl.BlockSpec((1, D), lambda i, j: (0, 0)),
                  pl.BlockSpec((D, tn), lambda i, j: (0, j))],
        out_specs=[pl.BlockSpec((tm, D), lambda i, j: (i, 0)), pl.BlockSpec((tm, tn), lambda i, j: (i, j))],
        out_shape=[S((T, D), MXU), S((T, N), f32)],
        compiler_params=_cp("arbitrary", "arbitrary"),
    )(x, g, w)


def rms_swiglu(x, g, wg, wu, *, tm, tn, name, comm=None):
    T, D = x.shape
    N = wg.shape[1]

    def body(x_ref, g_ref, wg_ref, wu_ref, h_ref, gate_ref, up_ref, act_ref):
        @pl.when(pl.program_id(1) == 0)
        def _():
            _rms_norm_rows(x_ref, g_ref, h_ref, tm)
        h = h_ref[...]
        gate = _nn(h, wg_ref[...])
        up = _nn(h, wu_ref[...])
        gate_ref[...] = gate
        up_ref[...] = up
        act_ref[...] = (gate * _sigmoid(gate) * up).astype(MXU)

    wspec = pl.BlockSpec((D, tn), lambda i, j: (0, j))
    ospec = pl.BlockSpec((tm, tn), lambda i, j: (i, j))
    return _pallas(
        body, comm=comm, name=name, grid=(T // tm, N // tn),
        in_specs=[pl.BlockSpec((tm, D), lambda i, j: (i, 0)), pl.BlockSpec((1, D), lambda i, j: (0, 0)), wspec, wspec],
        out_specs=[pl.BlockSpec((tm, D), lambda i, j: (i, 0)), ospec, ospec, ospec],
        out_shape=[S((T, D), MXU), S((T, N), f32), S((T, N), f32), S((T, N), MXU)],
        compiler_params=_cp("arbitrary", "arbitrary"),
    )(x, g, wg, wu)


def matmul_res(a, w, res, *, tm, tn, name, comm=None):
    T, K = a.shape
    N = w.shape[1]

    def body(a_ref, w_ref, r_ref, o_ref):
        o_ref[...] = r_ref[...] + _nn(a_ref[...], w_ref[...])

    return _pallas(
        body, comm=comm, name=name, grid=(T // tm, N // tn),
        in_specs=[pl.BlockSpec((tm, K), lambda i, j: (i, 0)), pl.BlockSpec((K, tn), lambda i, j: (0, j)),
                  pl.BlockSpec((tm, tn), lambda i, j: (i, j))],
        out_specs=pl.BlockSpec((tm, tn), lambda i, j: (i, j)),
        out_shape=S((T, N), f32),
        compiler_params=_cp("arbitrary", "arbitrary"),
    )(a, w, res)


def nt_plain(a, w, *, tm, tn, name, comm=None):
    T, K = a.shape
    N = w.shape[0]

    def body(a_ref, w_ref, o_ref):
        o_ref[...] = _nt(a_ref[...], w_ref[...])

    return _pallas(
        body, comm=comm, name=name, grid=(T // tm, N // tn),
        in_specs=[pl.BlockSpec((tm, K), lambda i, j: (i, 0)), pl.BlockSpec((tn, K), lambda i, j: (j, 0))],
        out_specs=pl.BlockSpec((tm, tn), lambda i, j: (i, j)),
        out_shape=S((T, N), f32),
        compiler_params=_cp("arbitrary", "arbitrary"),
    )(a, w)


def nt_swiglu_bwd(dy, wd, gate, up, *, tm, tn, name, comm=None):
    T, D = dy.shape
    F = wd.shape[0]

    def body(dy_ref, w_ref, g_ref, u_ref, dg_ref, du_ref):
        d_act = _nt(dy_ref[...], w_ref[...])
        g = g_ref[...]
        sg = _sigmoid(g)
        du_ref[...] = (d_act * (g * sg)).astype(MXU)
        dg_ref[...] = (d_act * u_ref[...] * (sg * (1.0 + g * (1.0 - sg)))).astype(MXU)

    blk = pl.BlockSpec((tm, tn), lambda i, j: (i, j))
    return _pallas(
        body, comm=comm, name=name, grid=(T // tm, F // tn),
        in_specs=[pl.BlockSpec((tm, D), lambda i, j: (i, 0)), pl.BlockSpec((tn, D), lambda i, j: (j, 0)), blk, blk],
        out_specs=[blk, blk],
        out_shape=[S((T, F), MXU), S((T, F), MXU)],
        compiler_params=_cp("arbitrary", "arbitrary"),
    )(dy, wd, gate, up)


def nt_rms_bwd(terms, x, g, dres, *, tm, tn, name, comm=None):
    T, D = x.shape
    K = terms[0][0].shape[1]
    nj = D // tn
    nt = len(terms)
    ni = T // tm

    def body(*refs):
        a_refs = refs[0:2 * nt:2]
        w_refs = refs[1:2 * nt:2]
        x_ref, g_ref, r_ref, dx_ref, dxb_ref, dg_ref, acc_ref = refs[2 * nt:]
        i, j = pl.program_id(0), pl.program_id(1)
        part = _nt(a_refs[0][...], w_refs[0][...])
        for t in range(1, nt):
            part += _nt(a_refs[t][...], w_refs[t][...])
        acc_ref[j] = part

        @pl.when(j == nj - 1)
        def _():
            def chunk(c, dgain):
                rows = pl.ds(c * BLK, BLK)
                dh = jnp.concatenate([acc_ref[jj, rows, :] for jj in range(nj)], axis=1)
                xf = x_ref[rows, :]
                r = lax.rsqrt(jnp.mean(xf * xf, axis=-1, keepdims=True) + EPS)
                y = xf * r
                dy = dh * g_ref[...]
                dx = r_ref[rows, :] + r * (dy - y * jnp.mean(dy * y, axis=-1, keepdims=True))
                dx_ref[rows, :] = dx
                dxb_ref[rows, :] = dx.astype(MXU)
                return dgain + jnp.sum(dh * y, axis=0, keepdims=True)

            dgain = lax.fori_loop(0, tm // BLK, chunk, jnp.zeros((1, D), f32))

            @pl.when(i == 0)
            def _():
                dg_ref[...] = dgain

            @pl.when(i > 0)
            def _():
                dg_ref[...] += dgain

    in_specs, args = [], []
    for a, w in terms:
        in_specs += [pl.BlockSpec((tm, K), lambda i, j: (i, 0), pipeline_mode=pl.Buffered(1)),
                     pl.BlockSpec((tn, K), lambda i, j: (j, 0))]
        args += [a, w]
    row = pl.BlockSpec((tm, D), lambda i, j: (i, 0), pipeline_mode=pl.Buffered(1))
    vec = pl.BlockSpec((1, D), lambda i, j: (0, 0))
    in_specs += [row, vec, row]
    return _pallas(
        body, comm=comm, name=name, grid=(ni, nj), in_specs=in_specs,
        out_specs=[row, row, vec],
        out_shape=[S((T, D), f32), S((T, D), MXU), S((1, D), f32)],
        scratch_shapes=[pltpu.VMEM((nj, tm, tn), f32)],
        compiler_params=_cp("arbitrary", "arbitrary", vmem=VMEM_TALL),
    )(*args, x, g, dres)


def tn_matmul(a, b, *, tm, tn, tk, name, by_chip=False, comm=None):
    T, M = a.shape
    N = b.shape[1]
    if by_chip:
        tn = N // 4
        out_spec = pl.BlockSpec((None, tm, tn), lambda i, j, k: (j, i, 0))
        out_shape = S((4, M, tn), f32)
    else:
        out_spec = pl.BlockSpec((tm, tn), lambda i, j, k: (i, j))
        out_shape = S((M, N), f32)

    def body(a_ref, b_ref, o_ref):
        part = _tn(a_ref[...], b_ref[...])

        @pl.when(pl.program_id(2) == 0)
        def _():
            o_ref[...] = part

        @pl.when(pl.program_id(2) > 0)
        def _():
            o_ref[...] += part

    return _pallas(
        body, comm=comm, name=name, grid=(M // tm, N // tn, T // tk),
        in_specs=[pl.BlockSpec((tk, tm), lambda i, j, k: (k, i)), pl.BlockSpec((tk, tn), lambda i, j, k: (k, j))],
        out_specs=out_spec, out_shape=out_shape,
        compiler_params=_cp("arbitrary", "arbitrary", "arbitrary"),
    )(a, b)


def loss_head(y, target, *, tm, name):
    T, D = y.shape
    ni = T // tm

    def body(y_ref, t_ref, dy_ref, dyb_ref, l_ref, acc_ref):
        i = pl.program_id(0)
        e = y_ref[...] - t_ref[...]
        dy = e * (1.0 / D)
        dy_ref[...] = dy
        dyb_ref[...] = dy.astype(MXU)
        part = jnp.sum(e * e, axis=0, keepdims=True)

        @pl.when(i == 0)
        def _():
            acc_ref[...] = part

        @pl.when(i > 0)
        def _():
            acc_ref[...] += part

        @pl.when(i == ni - 1)
        def _():
            tot = jnp.sum(acc_ref[...], axis=1, keepdims=True) * (0.5 / D)
            l_ref[...] = jnp.broadcast_to(tot, (1, 128))

    row = pl.BlockSpec((tm, D), lambda i: (i, 0))
    return pl.pallas_call(
        body, name=name, grid=(ni,), in_specs=[row, row],
        out_specs=[row, row, pl.BlockSpec((1, 128), lambda i: (0, 0))],
        out_shape=[S((T, D), f32), S((T, D), MXU), S((1, 128), f32)],
        scratch_shapes=[pltpu.VMEM((1, D), f32)],
        compiler_params=_cp("arbitrary"),
    )(y, target)


def _qk_norm(v, gain, e_ref):
    r = lax.rsqrt(_seg_sum(v * v, e_ref) * (1.0 / HD) + EPS)
    return v * r * gain


def _dup_halves(pair):
    rolled = pltpu.roll(pair, HD, 1)
    lo = lax.broadcasted_iota(jnp.int32, pair.shape, 1) < HD
    return jnp.where(lo, pair, rolled), jnp.where(lo, rolled, pair)


def prep_fwd(proj, e, gains, *, tm, name):
    T = proj.shape[0]

    def body(p_ref, e_ref, gaq, gak, gcq, gck, aq, ak, av, cq, ckk, cvv):
        aq[...] = _qk_norm(p_ref[:, 0:512], gaq[...], e_ref)
        ak[...] = _qk_norm(p_ref[:, 512:1024], gak[...], e_ref)
        av[...] = p_ref[:, 1024:1536]
        cq[:, 0:512] = _qk_norm(p_ref[:, 2560:3072], gcq[...], e_ref).astype(MXU)
        cq[:, 512:1024] = _qk_norm(p_ref[:, 3072:3584], gcq[...], e_ref).astype(MXU)
        kraw = p_ref[:, 3584:3712]
        kn = kraw * lax.rsqrt(_seg_sum128(kraw * kraw, e_ref) * (1.0 / HD) + EPS) * gck[...]
        k0, k1 = _dup_halves(kn)
        ckk[:, 0:128] = k0.astype(MXU)
        ckk[:, 128:256] = k1.astype(MXU)
        v0, v1 = _dup_halves(p_ref[:, 3712:3840])
        cvv[:, 0:128] = v0.astype(MXU)
        cvv[:, 128:256] = v1.astype(MXU)

    def vec(n):
        return pl.BlockSpec((1, n), lambda i: (0, 0))

    def rows(n):
        return pl.BlockSpec((tm, n), lambda i: (i, 0))

    return pl.pallas_call(
        body, name=name, grid=(T // tm,),
        in_specs=[rows(IN_W), pl.BlockSpec((512, 512), lambda i: (0, 0)), vec(512), vec(512), vec(512), vec(128)],
        out_specs=[rows(512), rows(512), rows(512), rows(1024), rows(256), rows(256)],
        out_shape=[S((T, 512), f32)] * 3 + [S((T, 1024), MXU), S((T, 256), MXU), S((T, 256), MXU)],
        compiler_params=_cp("arbitrary"),
    )(proj, e, *gains)


def _band_mask(max_dist, shut):
    r = lax.broadcasted_iota(jnp.int32, (2 * BLK, 2 * BLK), 0) & (BLK - 1)
    c = lax.broadcasted_iota(jnp.int32, (2 * BLK, 2 * BLK), 1)
    prev = jnp.logical_and(c < BLK, c >= r + (BLK - max_dist) + shut)
    return jnp.logical_or(prev, jnp.logical_and(c >= BLK, c - BLK <= r))


def _prev_mask(max_dist, shut):
    r = lax.broadcasted_iota(jnp.int32, (2 * BLK, BLK), 0) & (BLK - 1)
    c = lax.broadcasted_iota(jnp.int32, (2 * BLK, BLK), 1)
    return c >= r + (BLK - max_dist) + shut


def _head_masks():
    lo = (lax.broadcasted_iota(jnp.int32, (BLK, BLK), 1) < HD).astype(f32)
    return lo.astype(MXU), (1.0 - lo).astype(MXU)


def _stack_heads(x, hm):
    return jnp.concatenate([x * hm[0], x * hm[1]], axis=0)


def _unstack_heads(y, lane_lo):
    return jnp.where(lane_lo, y[0:BLK], y[BLK:2 * BLK])


def _rows(ref, start, dil):
    if dil == 1:
        return ref[pl.ds(start, BLK), :]
    return ref[pl.ds(start, BLK, stride=dil), :]


def _set_rows(ref, start, dil, val):
    if dil == 1:
        ref[pl.ds(start, BLK), :] = val
    else:
        ref[pl.ds(start, BLK, stride=dil), :] = val


def _attn_geometry(T, dil):
    span = BLK * dil
    n = max(1, 512 // span)
    return span, n, T // (span * n)


def band_attn_fwd(q, k, v, sinks, *, dil, max_dist, group, name, mix=None, comm=None):
    T = q.shape[0]
    P = q.shape[1] // BLK
    span, n, nb = _attn_geometry(T, dil)

    def body(*refs):
        s_ref = m_ref = None
        q_ref, kc_ref, kp_ref, vc_ref, vp_ref = refs[:5]
        rest = list(refs[5:])
        if sinks is not None:
            s_ref = rest.pop(0)
        if mix is not None:
            rest.pop(0)
            o_ref, l_ref, m_ref = rest
        else:
            o_ref, l_ref = rest
        b = pl.program_id(0)
        mask = _band_mask(max_dist, 0)
        mask0 = _band_mask(max_dist, jnp.where(b > 0, 0, BLK + 1))
        lane_lo = lax.broadcasted_iota(jnp.int32, (BLK, BLK), 1) < HD
        hm = _head_masks()
        if sinks is not None:
            row_lo = lax.broadcasted_iota(jnp.int32, (1, BLK), 1) < HD
            sk0 = jnp.max(jnp.where(row_lo, s_ref[...], NEG), axis=1, keepdims=True)
            sk1 = jnp.max(jnp.where(row_lo, NEG, s_ref[...]), axis=1, keepdims=True)
            sk = jnp.where(lax.broadcasted_iota(jnp.int32, (2 * BLK, 1), 0) < BLK, sk0, sk1)

        def load(r, sub):
            at = r + sub * span
            kc, vc = _rows(kc_ref, at, dil).astype(MXU), _rows(vc_ref, at, dil).astype(MXU)
            if sub == 0:
                kp, vp = _rows(kp_ref, r, dil).astype(MXU), _rows(vp_ref, r, dil).astype(MXU)
            else:
                kp, vp = _rows(kc_ref, at - span, dil).astype(MXU), _rows(vc_ref, at - span, dil).astype(MXU)
            qst = _stack_heads(_rows(q_ref, at, dil).astype(MXU), hm)
            return (qst, jnp.concatenate([kp, kc], axis=0), jnp.concatenate([vp, vc], axis=0),
                    mask0 if sub == 0 else mask, at)

        def attend(items):
            ss = [jnp.where(m_, _nt(qst, kcat) * SCALE, NEG) for qst, kcat, _, m_, _ in items]
            ms = [jnp.max(s, axis=1, keepdims=True) for s in ss]
            if sinks is not None:
                ms = [jnp.maximum(m, sk) for m in ms]
            ps = [jnp.exp(s - m) for s, m in zip(ss, ms)]
            dens = [jnp.sum(p_, axis=1, keepdims=True) for p_ in ps]
            if sinks is not None:
                dens = [d + jnp.exp(sk - m) for d, m in zip(dens, ms)]
            outs = [_nn(p_.astype(MXU), it[2]) / d for p_, it, d in zip(ps, items, dens)]
            for it, o, m, d in zip(items, outs, ms, dens):
                lse = m + jnp.log(d)
                if m_ref is not None:
                    _set_rows(m_ref, it[4], dil, _unstack_heads(o, lane_lo).astype(MXU))
                _set_rows(o_ref, it[4], dil, _unstack_heads(o, lane_lo))
                _set_rows(l_ref, it[4], dil, jnp.where(lane_lo, lse[0:BLK], lse[BLK:2 * BLK]))

        if dil * n <= 4:
            work = [(r, sub) for r in range(dil) for sub in range(n)]
            for g in range(0, len(work), 2):
                attend([load(*w) for w in work[g:g + 2]])
        else:
            def two_streams(i, carry):
                attend([load(2 * i, 0), load(2 * i + 1, 0)])
                return carry
            lax.fori_loop(0, dil // 2, two_streams, 0)

    rows_per_step = span * n
    qspec = pl.BlockSpec((rows_per_step, BLK), lambda b, p: (b, p))
    cur = pl.BlockSpec((rows_per_step, BLK), lambda b, p: (b, p // group))
    prev = pl.BlockSpec((span, BLK), lambda b, p: (jnp.maximum(b * n - 1, 0), p // group))
    in_specs = [qspec, cur, prev, cur, prev]
    args = [q, k, k, v, v]
    if sinks is not None:
        in_specs.append(pl.BlockSpec((1, BLK), lambda b, p: (0, p)))
        args.append(sinks)
    out_specs, out_shape, aliases = [qspec, qspec], [S(q.shape, f32), S(q.shape, f32)], {}
    if mix is not None:
        first_block = mix.shape[1] // BLK - P
        aliases = {len(args): 2}
        in_specs.append(ANY)
        args.append(mix)
        out_specs.append(pl.BlockSpec((rows_per_step, BLK), lambda b, p: (b, first_block + p)))
        out_shape.append(S(mix.shape, mix.dtype))
    return _pallas(
        body, comm=comm, name=name, grid=(nb, P), in_specs=in_specs, out_specs=out_specs, out_shape=out_shape,
        compiler_params=_cp("arbitrary", "arbitrary"), input_output_aliases=aliases,
    )(*args)


def band_attn_bwd(q, k, v, lse, do, dd, *, dil, max_dist, group, name, comm=None):
    T = q.shape[0]
    P = q.shape[1] // BLK
    span, n, nb = _attn_geometry(T, dil)
    assert group == 1 or dil == 1

    def body(q_ref, qn_ref, do_ref, don_ref, l_ref, ln_ref, d_ref, dn_ref, kc_ref, kp_ref, vc_ref, vp_ref,
             dq_ref, dk_ref, dv_ref):
        b, p = pl.program_id(0), pl.program_id(1)
        mask = _band_mask(max_dist, 0)
        mask0 = _band_mask(max_dist, jnp.where(b > 0, 0, BLK + 1))
        tail = _prev_mask(max_dist, jnp.where(b < nb - 1, 0, BLK + 1))
        lane_lo = lax.broadcasted_iota(jnp.int32, (BLK, BLK), 1) < HD
        hm = _head_masks()
        own_lanes = (lax.broadcasted_iota(jnp.int32, (2 * BLK, BLK), 1) < HD) == (
            lax.broadcasted_iota(jnp.int32, (2 * BLK, BLK), 0) < BLK)

        def per_row(x):
            return jnp.max(jnp.where(own_lanes, jnp.concatenate([x, x], axis=0), NEG), axis=1, keepdims=True)

        def q_side(refs, at):
            q_r, do_r, l_r, d_r = refs
            return (_stack_heads(_rows(q_r, at, dil).astype(MXU), hm), _stack_heads(_rows(do_r, at, dil).astype(MXU), hm),
                    per_row(_rows(l_r, at, dil)), per_row(_rows(d_r, at, dil)))

        def kv(ref, at):
            return _rows(ref, at, dil).astype(MXU)

        first = p % group == 0

        def put_kv(ref, at, val):
            if group == 1:
                _set_rows(ref, at, dil, val)
            else:
                @pl.when(first)
                def _():
                    ref[pl.ds(at, BLK), :] = val

                @pl.when(jnp.logical_not(first))
                def _():
                    ref[pl.ds(at, BLK), :] += val

        def stream(r):
            dks, dvs = [None] * n, [None] * n
            for sub in range(n):
                at = r + sub * span
                qst, dost, lrow, drow = q_side((q_ref, do_ref, l_ref, d_ref), at)
                if sub == 0:
                    kp, vp, m_ = kv(kp_ref, r), kv(vp_ref, r), mask0
                else:
                    kp, vp, m_ = kv(kc_ref, at - span), kv(vc_ref, at - span), mask
                kcat = jnp.concatenate([kp, kv(kc_ref, at)], axis=0)
                vcat = jnp.concatenate([vp, kv(vc_ref, at)], axis=0)
                pr = jnp.where(m_, jnp.exp(_nt(qst, kcat) * SCALE - lrow), 0.0)
                ds = (pr * (_nt(dost, vcat) - drow) * SCALE).astype(MXU)
                prb = pr.astype(MXU)
                _set_rows(dq_ref, at, dil, _unstack_heads(_nn(ds, kcat), lane_lo))
                if sub == 0:
                    dks[0] = _tn(ds[:, BLK:], qst)
                    dvs[0] = _tn(prb[:, BLK:], dost)
                else:
                    dkk, dvv = _tn(ds, qst), _tn(prb, dost)
                    dks[sub - 1] += dkk[0:BLK]
                    dvs[sub - 1] += dvv[0:BLK]
                    dks[sub], dvs[sub] = dkk[BLK:], dvv[BLK:]
            at = r + (n - 1) * span
            qst, dost, lrow, drow = q_side((qn_ref, don_ref, ln_ref, dn_ref), r)
            pr = jnp.where(tail, jnp.exp(_nt(qst, kv(kc_ref, at)) * SCALE - lrow), 0.0)
            ds = (pr * (_nt(dost, kv(vc_ref, at)) - drow) * SCALE).astype(MXU)
            dks[n - 1] += _tn(ds, qst)
            dvs[n - 1] += _tn(pr.astype(MXU), dost)
            for sub in range(n):
                put_kv(dk_ref, r + sub * span, dks[sub])
                put_kv(dv_ref, r + sub * span, dvs[sub])

        if dil <= 4:
            for r in range(dil):
                stream(r)
        else:
            def two_streams(i, carry):
                stream(2 * i)
                stream(2 * i + 1)
                return carry
            lax.fori_loop(0, dil // 2, two_streams, 0)

    rows_per_step = span * n
    qspec = pl.BlockSpec((rows_per_step, BLK), lambda b, p: (b, p))
    qnext = pl.BlockSpec((span, BLK), lambda b, p: (jnp.minimum((b + 1) * n, T // span - 1), p))
    cur = pl.BlockSpec((rows_per_step, BLK), lambda b, p: (b, p // group))
    prev = pl.BlockSpec((span, BLK), lambda b, p: (jnp.maximum(b * n - 1, 0), p // group))
    return _pallas(
        body, comm=comm, name=name, grid=(nb, P),
        in_specs=[qspec, qnext, qspec, qnext, qspec, qnext, qspec, qnext, cur, prev, cur, prev],
        out_specs=[qspec, cur, cur],
        out_shape=[S(q.shape, f32), S(k.shape, f32), S(k.shape, f32)],
        compiler_params=_cp("arbitrary", "arbitrary"),
    )(q, q, do, do, lse, lse, dd, dd, k, k, v, v)


def dil_combine_fwd(ols, *, tm, name):
    T = ols[0].shape[0]

    def body(o1, l1, o2, l2, o3, l3, out_ref):
        a, b, c = l1[...], l2[...], l3[...]
        m = jnp.maximum(jnp.maximum(a, b), c)
        ea, eb, ec = jnp.exp(a - m), jnp.exp(b - m), jnp.exp(c - m)
        out = (ea * o1[...] + eb * o2[...] + ec * o3[...]) / (ea + eb + ec)
        out_ref[...] = out.astype(MXU)

    row = pl.BlockSpec((tm, 512), lambda i: (i, 0))
    return pl.pallas_call(body, name=name, grid=(T // tm,), in_specs=[row] * 6, out_specs=row,
                          out_shape=S((T, A_W + B_W + C_W), MXU), compiler_params=_cp("arbitrary"))(*ols)


def dil_combine_bwd(ols, dmix, e, *, tm, name):
    T = ols[0].shape[0]

    def body(o1, l1, o2, l2, o3, l3, d_ref, e_ref, do1, do2, do3, dd1, dd2, dd3):
        a, b, c = l1[...], l2[...], l3[...]
        m = jnp.maximum(jnp.maximum(a, b), c)
        ea, eb, ec = jnp.exp(a - m), jnp.exp(b - m), jnp.exp(c - m)
        inv = 1.0 / (ea + eb + ec)
        wa, wb, wc = ea * inv, eb * inv, ec * inv
        dout = d_ref[...]
        gbar = _seg_sum(dout * (wa * o1[...] + wb * o2[...] + wc * o3[...]), e_ref)
        do1[...] = wa * dout
        do2[...] = wb * dout
        do3[...] = wc * dout
        dd1[...] = wa * gbar
        dd2[...] = wb * gbar
        dd3[...] = wc * gbar

    row = pl.BlockSpec((tm, 512), lambda i: (i, 0))
    return pl.pallas_call(
        body, name=name, grid=(T // tm,),
        in_specs=[row] * 6 + [row, pl.BlockSpec((512, 512), lambda i: (0, 0))],
        out_specs=[row] * 6,
        out_shape=[S((T, 512), f32)] * 6,
        compiler_params=_cp("arbitrary"),
    )(*ols, dmix, e)


def swa_pre_bwd(o, lse, dmix, sinks, e, *, tm, name):
    T = o.shape[0]
    ni = T // tm

    def body(o_ref, l_ref, d_ref, s_ref, e_ref, do_ref, dd_ref, ds_ref):
        i = pl.program_id(0)
        dout = d_ref[...]
        do_ref[...] = dout.astype(MXU)
        prod = dout * o_ref[...]
        dd = jnp.concatenate([_seg_sum(prod[:, 0:512], e_ref), _seg_sum(prod[:, 512:1024], e_ref)], axis=1)
        dd_ref[...] = dd
        part = -jnp.sum(jnp.exp(s_ref[...] - l_ref[...]) * dd, axis=0, keepdims=True)

        @pl.when(i == 0)
        def _():
            ds_ref[...] = part

        @pl.when(i > 0)
        def _():
            ds_ref[...] += part

    row = pl.BlockSpec((tm, 1024), lambda i: (i, 0))
    vec = pl.BlockSpec((1, 1024), lambda i: (0, 0))
    return pl.pallas_call(
        body, name=name, grid=(ni,),
        in_specs=[row, row, pl.BlockSpec((tm, 1024), lambda i: (i, 1)), vec, pl.BlockSpec((512, 512), lambda i: (0, 0))],
        out_specs=[row, row, vec],
        out_shape=[S((T, 1024), MXU), S((T, 1024), f32), S((1, 1024), f32)],
        compiler_params=_cp("arbitrary"),
    )(o, lse, dmix, sinks, e)


SHIFT_PAD = 24


def _shifted_copies(buf_ref, sh_ref, length):
    for r in range(1, 8):
        sh_ref[r - 1, 0:length, :] = buf_ref[pl.ds(r, length), :]


def _window(buf_ref, sh_ref, start, rows):
    q, r = divmod(start, 8)
    if r == 0:
        return buf_ref[pl.ds(8 * q, rows), :]
    return sh_ref[r - 1, pl.ds(8 * q, rows), :]


TAP_ROWS = 64


def _tap_sum(buf_ref, sh_ref, w_ref, starts, rows):
    outs = []
    for c0 in range(0, rows, TAP_ROWS):
        n = min(TAP_ROWS, rows - c0)
        acc = _window(buf_ref, sh_ref, starts[0] + c0, n) * w_ref[pl.ds(0, 1), :]
        for j in range(1, CONV_K):
            acc += _window(buf_ref, sh_ref, starts[j] + c0, n) * w_ref[pl.ds(j, 1), :]
        outs.append(acc)
    return jnp.concatenate(outs, axis=0)


def _conv_taps(buf_ref, sh_ref, w_ref, start, rows):
    return _tap_sum(buf_ref, sh_ref, w_ref, [start + j for j in range(CONV_K)], rows)


def conv_fwd(proj, mix, w, b, ln_g, ln_b, *, tb, name):
    T = proj.shape[0]
    hb = tb // HALO

    def body(u_ref, g_ref, up_ref, gp_ref, w_ref, b_ref, lg_ref, lb_ref, mix_ref, o_ref, hbuf, hsh):
        i = pl.program_id(0)
        hprev = up_ref[...] * _sigmoid(gp_ref[...])
        hbuf[0:HALO, :] = hprev * jnp.where(i > 0, 1.0, 0.0)
        hbuf[HALO:HALO + tb, :] = u_ref[...] * _sigmoid(g_ref[...])
        _shifted_copies(hbuf, hsh, tb + SHIFT_PAD)
        y = _conv_taps(hbuf, hsh, w_ref, HALO - (CONV_K - 1), tb) + b_ref[...]
        mu = jnp.mean(y, axis=-1, keepdims=True)
        yc = y - mu
        var = jnp.mean(yc * yc, axis=-1, keepdims=True)
        z = yc * lax.rsqrt(var + EPS) * lg_ref[...] + lb_ref[...]
        o_ref[...] = (z * _sigmoid(z)).astype(MXU)

    vec = pl.BlockSpec((1, 512), lambda i: (0, 0))
    return pl.pallas_call(
        body, name=name, grid=(T // tb,),
        in_specs=[pl.BlockSpec((tb, 512), lambda i: (i, 3)), pl.BlockSpec((tb, 512), lambda i: (i, 4)),
                  pl.BlockSpec((HALO, 512), lambda i: (jnp.maximum(i * hb - 1, 0), 3)),
                  pl.BlockSpec((HALO, 512), lambda i: (jnp.maximum(i * hb - 1, 0), 4)),
                  pl.BlockSpec((HALO, 512), lambda i: (0, 0)), vec, vec, vec, ANY],
        out_specs=pl.BlockSpec((tb, 512), lambda i: (i, 1)),
        out_shape=S(mix.shape, mix.dtype),
        scratch_shapes=[pltpu.VMEM((tb + HALO, 512), f32), pltpu.VMEM((7, tb + SHIFT_PAD, 512), f32)],
        compiler_params=_cp("arbitrary"), input_output_aliases={8: 0},
    )(proj, proj, proj, proj, w, b, ln_g, ln_b, mix)


def conv_bwd(proj, dmix, w, b, ln_g, ln_b, *, tb, name, comm=None):
    T = proj.shape[0]
    hb = tb // HALO
    ni = T // tb
    last_h = T // HALO - 1
    ext = tb + HALO

    def body(u_ref, g_ref, up_ref, gp_ref, un_ref, gn_ref, d_ref, dn_ref, w_ref, b_ref, lg_ref, lb_ref,
             du_ref, dg_ref, dw_ref, db_ref, dlg_ref, dlb_ref, hbuf, dybuf, hsh, dsh):
        i = pl.program_id(0)
        hbuf[0:HALO, :] = up_ref[...] * _sigmoid(gp_ref[...]) * jnp.where(i > 0, 1.0, 0.0)
        u = u_ref[...]
        sg = _sigmoid(g_ref[...])
        hbuf[HALO:HALO + tb, :] = u * sg
        hbuf[HALO + tb:HALO + ext, :] = un_ref[...] * _sigmoid(gn_ref[...])
        _shifted_copies(hbuf, hsh, ext + SHIFT_PAD)
        y = _conv_taps(hbuf, hsh, w_ref, HALO - (CONV_K - 1), ext) + b_ref[...]
        mu = jnp.mean(y, axis=-1, keepdims=True)
        yc = y - mu
        rstd = lax.rsqrt(jnp.mean(yc * yc, axis=-1, keepdims=True) + EPS)
        yn = yc * rstd
        z = yn * lg_ref[...] + lb_ref[...]
        sz = _sigmoid(z)
        row = lax.broadcasted_iota(jnp.int32, (ext, 1), 0)
        own = row < tb
        keep = row < jnp.where(i < ni - 1, ext, tb)
        dout = jnp.concatenate([d_ref[...], dn_ref[...]], axis=0)
        dz = jnp.where(keep, dout * (sz * (1.0 + z * (1.0 - sz))), 0.0)
        dyn = dz * lg_ref[...]
        dy = rstd * (dyn - jnp.mean(dyn, axis=-1, keepdims=True) - yn * jnp.mean(dyn * yn, axis=-1, keepdims=True))
        dybuf[...] = dy
        _shifted_copies(dybuf, dsh, tb + SHIFT_PAD)
        dz_own = jnp.where(own, dz, 0.0)
        dlg = jnp.sum(dz_own * yn, axis=0, keepdims=True)
        dlb = jnp.sum(dz_own, axis=0, keepdims=True)
        dy_own = dybuf[0:tb, :]
        dbias = jnp.sum(dy_own, axis=0, keepdims=True)
        dh = _tap_sum(dybuf, dsh, w_ref, [CONV_K - 1 - j for j in range(CONV_K)], tb)
        du_ref[...] = (dh * sg).astype(MXU)
        dg_ref[...] = (dh * u * sg * (1.0 - sg)).astype(MXU)
        taps = [jnp.sum(dy_own * _window(hbuf, hsh, HALO - (CONV_K - 1) + j, tb), axis=0, keepdims=True)
                for j in range(CONV_K)]
        taps.append(jnp.zeros((1, 512), f32))
        dwt = jnp.concatenate(taps, axis=0)

        @pl.when(i == 0)
        def _():
            dw_ref[...] = dwt
            db_ref[...] = dbias
            dlg_ref[...] = dlg
            dlb_ref[...] = dlb

        @pl.when(i > 0)
        def _():
            dw_ref[...] += dwt
            db_ref[...] += dbias
            dlg_ref[...] += dlg
            dlb_ref[...] += dlb

    vec = pl.BlockSpec((1, 512), lambda i: (0, 0))
    wspec = pl.BlockSpec((HALO, 512), lambda i: (0, 0))

    def halo_prev(col):
        return pl.BlockSpec((HALO, 512), lambda i: (jnp.maximum(i * hb - 1, 0), col))

    def halo_next(col):
        return pl.BlockSpec((HALO, 512), lambda i: (jnp.minimum((i + 1) * hb, last_h), col))

    row = pl.BlockSpec((tb, 512), lambda i: (i, 0))
    return _pallas(
        body, comm=comm, name=name, grid=(ni,),
        in_specs=[pl.BlockSpec((tb, 512), lambda i: (i, 3)), pl.BlockSpec((tb, 512), lambda i: (i, 4)),
                  halo_prev(3), halo_prev(4), halo_next(3), halo_next(4),
                  pl.BlockSpec((tb, 512), lambda i: (i, 1)), halo_next(1), wspec, vec, vec, vec],
        out_specs=[row, row, wspec, vec, vec, vec],
        out_shape=[S((T, 512), MXU), S((T, 512), MXU), S((HALO, 512), f32)] + [S((1, 512), f32)] * 3,
        scratch_shapes=[pltpu.VMEM((tb + 2 * HALO, 512), f32), pltpu.VMEM((ext, 512), f32),
                        pltpu.VMEM((7, ext + SHIFT_PAD, 512), f32), pltpu.VMEM((7, tb + SHIFT_PAD, 512), f32)],
        compiler_params=_cp("arbitrary"),
    )(proj, proj, proj, proj, proj, proj, dmix, dmix, w, b, ln_g, ln_b)


def _qk_norm_bwd(v, gain, dout, e_ref):
    r = lax.rsqrt(_seg_sum(v * v, e_ref) * (1.0 / HD) + EPS)
    y = v * r
    dgain = jnp.sum(dout * y, axis=0, keepdims=True)
    dy = dout * gain
    dv = r * (dy - y * (_seg_sum(dy * y, e_ref) * (1.0 / HD)))
    return dv, dgain


def prep_bwd(proj, e, gains, da, dc, dconv, *, tm, name):
    T = proj.shape[0]

    def body(*refs):
        p_ref, e_ref, gaq, gak, gcq, gck = refs[0:6]
        a_refs = refs[6:15]
        dcq, dckk, dcvv, du, dgt = refs[15:20]
        dp, gaq_o, gak_o, gcq_o, gck_o = refs[20:]
        i = pl.program_id(0)
        dq = a_refs[0][...] + a_refs[3][...] + a_refs[6][...]
        dk = a_refs[1][...] + a_refs[4][...] + a_refs[7][...]
        dv = a_refs[2][...] + a_refs[5][...] + a_refs[8][...]
        d, g_aq = _qk_norm_bwd(p_ref[:, 0:512], gaq[...], dq, e_ref)
        dp[:, 0:512] = d.astype(MXU)
        d, g_ak = _qk_norm_bwd(p_ref[:, 512:1024], gak[...], dk, e_ref)
        dp[:, 512:1024] = d.astype(MXU)
        dp[:, 1024:1536] = dv.astype(MXU)
        dp[:, 1536:2048] = du[...]
        dp[:, 2048:2560] = dgt[...]
        d, g_cq0 = _qk_norm_bwd(p_ref[:, 2560:3072], gcq[...], dcq[:, 0:512], e_ref)
        dp[:, 2560:3072] = d.astype(MXU)
        d, g_cq1 = _qk_norm_bwd(p_ref[:, 3072:3584], gcq[...], dcq[:, 512:1024], e_ref)
        dp[:, 3072:3584] = d.astype(MXU)
        lo = lax.broadcasted_iota(jnp.int32, (tm, 128), 1) < HD

        def fold(ref):
            g0, g1 = ref[:, 0:128], ref[:, 128:256]
            s0 = g0 + pltpu.roll(g0, HD, 1)
            s1 = g1 + pltpu.roll(g1, HD, 1)
            return jnp.where(lo, s0, s1)

        dkn = fold(dckk)
        kraw = p_ref[:, 3584:3712]
        r = lax.rsqrt(_seg_sum128(kraw * kraw, e_ref) * (1.0 / HD) + EPS)
        y = kraw * r
        g_ck = jnp.sum(dkn * y, axis=0, keepdims=True)
        dy = dkn * gck[...]
        dp[:, 3584:3712] = (r * (dy - y * (_seg_sum128(dy * y, e_ref) * (1.0 / HD)))).astype(MXU)
        dp[:, 3712:3840] = fold(dcvv).astype(MXU)
        g_cq = jnp.concatenate([g_cq0, g_cq1], axis=1)

        @pl.when(i == 0)
        def _():
            gaq_o[...] = g_aq
            gak_o[...] = g_ak
            gcq_o[...] = g_cq
            gck_o[...] = g_ck

        @pl.when(i > 0)
        def _():
            gaq_o[...] += g_aq
            gak_o[...] += g_ak
            gcq_o[...] += g_cq
            gck_o[...] += g_ck

    def vec(n):
        return pl.BlockSpec((1, n), lambda i: (0, 0))

    def rows(n):
        return pl.BlockSpec((tm, n), lambda i: (i, 0))

    return pl.pallas_call(
        body, name=name, grid=(T // tm,),
        in_specs=[rows(IN_W), pl.BlockSpec((512, 512), lambda i: (0, 0)), vec(512), vec(512), vec(512), vec(128)]
        + [rows(512)] * 9 + [rows(1024), rows(256), rows(256), rows(512), rows(512)],
        out_specs=[rows(IN_W), vec(512), vec(512), vec(1024), vec(128)],
        out_shape=[S((T, IN_W), MXU), S((1, 512), f32), S((1, 512), f32), S((1, 1024), f32), S((1, 128), f32)],
        compiler_params=_cp("arbitrary"),
    )(proj, e, *gains, *da, *dc, *dconv)


def adamw(w, m, v, pieces, *, tr, name, comm=None):
    n, R, C = w.shape
    c1 = 1.0 - ADAM_B1 ** ADAM_STEP
    c2 = 1.0 - ADAM_B2 ** ADAM_STEP
    npc = len(pieces)

    def body(*refs):
        w_ref, m_ref, v_ref = refs[0:3]
        p_refs = refs[3:3 + npc]
        g_ref, d_ref, mo_ref, vo_ref = refs[3 + npc:]
        g = p_refs[0][...].astype(f32)
        for p in p_refs[1:]:
            g = g + p[...].astype(f32)
        mn = ADAM_B1 * m_ref[...] + (1.0 - ADAM_B1) * g
        vn = ADAM_B2 * v_ref[...] + (1.0 - ADAM_B2) * (g * g)
        g_ref[...] = g
        mo_ref[...] = mn
        vo_ref[...] = vn
        d_ref[...] = -ADAM_LR * ((mn / c1) / (jnp.sqrt(vn / c2) + ADAM_EPS) + ADAM_WD * w_ref[...])

    blk = pl.BlockSpec((None, tr, C), lambda l, i: (l, i, 0))
    return _pallas(
        body, comm=comm, name=name, grid=(n, R // tr), in_specs=[blk] * (3 + npc), out_specs=[blk] * 4,
        out_shape=[S(w.shape, f32)] * 4, compiler_params=_cp("arbitrary", "arbitrary"),
    )(w, m, v, *pieces)


def add_halves(pieces, other, *, tr, name):
    _, _, r, cc = pieces.shape

    def body(c_ref, a_ref, b_ref, o_ref):
        o_ref[...] = (a_ref[...] + b_ref[...]).astype(jnp.bfloat16)

    blk = pl.BlockSpec((None, tr, cc), lambda s, i, c_ref: (s, i, 0))
    grid_spec = pltpu.PrefetchScalarGridSpec(
        num_scalar_prefetch=1, grid=(4, r // tr),
        in_specs=[pl.BlockSpec((None, None, tr, cc), lambda s, i, c_ref: (s, c_ref[0], i, 0)), blk], out_specs=blk)
    core = lax.axis_index("c").astype(jnp.int32).reshape(1)
    return pl.pallas_call(body, name=name, grid_spec=grid_spec, out_shape=S((4, r, cc), jnp.bfloat16),
                          compiler_params=_cp("arbitrary", "arbitrary"))(core, pieces, other)


def sum8(parts, *, name):
    _, R, C = parts.shape

    def body(p_ref, o_ref):
        acc = p_ref[0]
        for d in range(1, 8):
            acc = acc + p_ref[d]
        o_ref[...] = acc

    return pl.pallas_call(body, name=name, out_shape=S((R, C), f32))(parts)


def _pos():
    return lax.axis_index("x"), lax.axis_index("y"), lax.axis_index("c")


def _other_chips(x, y):
    return [(1 - x, y), (x, 1 - y), (1 - x, 1 - y)]


class GatherComm:
    def __init__(self, shards, in_place=None):
        self.ins = list(shards)
        self.nt = nt = len(shards)
        self.in_place = list(in_place) if in_place is not None else [False] * nt
        self.out_shapes = [S((2 * s.shape[1], 4 * s.shape[2]), s.dtype) if ip else S((4,) + s.shape, s.dtype)
                           for s, ip in zip(shards, self.in_place)]
        self.sem_shapes = [pltpu.SemaphoreType.DMA((nt, 6)), pltpu.SemaphoreType.DMA((nt, 6)),
                           pltpu.SemaphoreType.DMA((nt, 2))]
        self.results = None

    def _place(self, couts, t, chip, half):
        cid = 2 * chip[0] + chip[1]
        if not self.in_place[t]:
            return couts[t].at[cid, half]
        _, r, c = self.ins[t].shape
        row0 = half * r if isinstance(half, int) else pl.multiple_of(half * r, 16)
        return couts[t].at[pl.ds(row0, r), pl.ds(pl.multiple_of(cid * c, 128), c)]

    def _copy(self, couts, sems, t, k, chip, half, to, src=None):
        dst = self._place(couts, t, chip, half)
        return pltpu.make_async_remote_copy(
            src_ref=dst if src is None else src, dst_ref=dst,
            send_sem=sems[0].at[t, k], recv_sem=sems[1].at[t, k], device_id=to, device_id_type=MESH)

    def _local(self, cins, couts, sems, t):
        x, y, _ = _pos()
        return [pltpu.make_async_copy(cins[t].at[half], self._place(couts, t, (x, y), half), sems[2].at[t, half])
                for half in range(2)]

    def start(self, cins, couts, sems):
        x, y, c = _pos()
        for t in range(self.nt):
            for cp in self._local(cins, couts, sems, t):
                cp.start()
            for j, chip in enumerate(_other_chips(x, y)):
                self._copy(couts, sems, t, j, (x, y), c, (*chip, c), src=cins[t].at[c]).start()

    def mid(self, cins, couts, sems):
        x, y, c = _pos()
        for t in range(self.nt):
            for j, chip in enumerate(_other_chips(x, y)):
                self._copy(couts, sems, t, j, chip, c, (x, y, c)).wait_recv()
                self._copy(couts, sems, t, 3 + j, chip, c, (x, y, 1 - c)).start()

    def finish(self, cins, couts, sems):
        x, y, c = _pos()
        for t in range(self.nt):
            for j, chip in enumerate(_other_chips(x, y)):
                self._copy(couts, sems, t, 3 + j, chip, 1 - c, (x, y, c)).wait_recv()
        for t in range(self.nt):
            for j, chip in enumerate(_other_chips(x, y)):
                self._copy(couts, sems, t, j, (x, y), c, (*chip, c), src=cins[t].at[c]).wait_send()
                self._copy(couts, sems, t, 3 + j, chip, c, (x, y, 1 - c)).wait_send()
            for cp in self._local(cins, couts, sems, t):
                cp.wait()


class SwapComm:
    def __init__(self, pieces):
        self.ins = list(pieces)
        self.nt = nt = len(pieces)
        self.out_shapes = [S((4,) + p.shape[2:], p.dtype) for p in pieces]
        self.sem_shapes = [pltpu.SemaphoreType.DMA((nt, 4)), pltpu.SemaphoreType.DMA((nt, 4))]
        self.results = None

    def _copies(self, cins, couts, sems):
        x, y, c = _pos()
        return [pltpu.make_async_remote_copy(src_ref=cins[t].at[s, 1 - c], dst_ref=couts[t].at[s],
                                             send_sem=sems[0].at[t, s], recv_sem=sems[1].at[t, s],
                                             device_id=(x, y, 1 - c), device_id_type=MESH)
                for t in range(self.nt) for s in range(4)]

    def start(self, cins, couts, sems):
        for cp in self._copies(cins, couts, sems):
            cp.start()

    def mid(self, cins, couts, sems):
        pass

    def finish(self, cins, couts, sems):
        for cp in self._copies(cins, couts, sems):
            cp.wait()


class ExchangeComm:
    def __init__(self, arrs):
        self.ins = list(arrs)
        self.nt = nt = len(arrs)
        self.out_shapes = [S((2,) + a.shape, a.dtype) for a in arrs]
        self.sem_shapes = [pltpu.SemaphoreType.DMA((nt, 7)), pltpu.SemaphoreType.DMA((nt, 7)),
                           pltpu.SemaphoreType.DMA((nt,))]
        self.results = None

    def _copy(self, couts, sems, t, k, half, src_chip, to, src=None):
        dst = couts[t].at[half, src_chip]
        return pltpu.make_async_remote_copy(
            src_ref=dst if src is None else src, dst_ref=dst,
            send_sem=sems[0].at[t, k], recv_sem=sems[1].at[t, k], device_id=to, device_id_type=MESH)

    def _local(self, cins, couts, sems, t):
        x, y, c = _pos()
        return pltpu.make_async_copy(cins[t].at[2 * x + y], couts[t].at[c, 2 * x + y], sems[2].at[t])

    def _firsts(self, cins, couts, sems, t):
        x, y, c = _pos()
        me = 2 * x + y
        cps = [self._copy(couts, sems, t, j, c, me, (*chip, c), src=cins[t].at[2 * chip[0] + chip[1]])
               for j, chip in enumerate(_other_chips(x, y))]
        return cps + [self._copy(couts, sems, t, 6, c, me, (x, y, 1 - c), src=cins[t].at[me])]

    def start(self, cins, couts, sems):
        for t in range(self.nt):
            self._local(cins, couts, sems, t).start()
            for cp in self._firsts(cins, couts, sems, t):
                cp.start()

    def mid(self, cins, couts, sems):
        x, y, c = _pos()
        for t in range(self.nt):
            for j, chip in enumerate(_other_chips(x, y)):
                cid = 2 * chip[0] + chip[1]
                self._copy(couts, sems, t, j, c, cid, (x, y, c)).wait_recv()
                self._copy(couts, sems, t, 3 + j, c, cid, (x, y, 1 - c)).start()

    def finish(self, cins, couts, sems):
        x, y, c = _pos()
        for t in range(self.nt):
            for j, chip in enumerate(_other_chips(x, y)):
                self._copy(couts, sems, t, 3 + j, 1 - c, 2 * chip[0] + chip[1], (x, y, c)).wait_recv()
            self._copy(couts, sems, t, 6, 1 - c, 2 * x + y, (x, y, c)).wait_recv()
        for t in range(self.nt):
            for cp in self._firsts(cins, couts, sems, t):
                cp.wait_send()
            for j, chip in enumerate(_other_chips(x, y)):
                self._copy(couts, sems, t, 3 + j, c, 2 * chip[0] + chip[1], (x, y, 1 - c)).wait_send()
            self._local(cins, couts, sems, t).wait()


def gather_small(vec, *, name):
    R, C = vec.shape

    def body(v_ref, out_ref, send_sems, recv_sems):
        x, y, c = _pos()
        me = 4 * x + 2 * y + c
        out_ref[me] = v_ref[...]
        cps = []
        def peer(k):
            fx, fy, fc = (k >> 2) & 1, (k >> 1) & 1, k & 1
            return (1 - x if fx else x), (1 - y if fy else y), (1 - c if fc else c)

        for k in range(1, 8):
            cp = pltpu.make_async_remote_copy(src_ref=v_ref, dst_ref=out_ref.at[me], send_sem=send_sems.at[k - 1],
                                              recv_sem=recv_sems.at[k - 1], device_id=peer(k), device_id_type=MESH)
            cp.start()
            cps.append(cp)
        for k in range(1, 8):
            px, py, pc = peer(k)
            pltpu.make_async_remote_copy(src_ref=v_ref, dst_ref=out_ref.at[4 * px + 2 * py + pc],
                                         send_sem=send_sems.at[k - 1], recv_sem=recv_sems.at[k - 1],
                                         device_id=(px, py, pc), device_id_type=MESH).wait_recv()
        for cp in cps:
            cp.wait_send()

    return pl.pallas_call(
        body, name=name,
        in_specs=[pl.BlockSpec(memory_space=pltpu.VMEM)], out_specs=pl.BlockSpec(memory_space=pltpu.VMEM),
        out_shape=S((8, R, C), vec.dtype),
        scratch_shapes=[pltpu.SemaphoreType.DMA((7,)), pltpu.SemaphoreType.DMA((7,))],
    )(vec)


def _tile(n, prefs):
    for p in prefs:
        if n % p == 0:
            return p
    return n


def _lanes(g, reps):
    return jnp.tile(g.reshape(1, -1), (1, reps))


class _NoRide:
    def rider(self, name):
        return None

    def landed(self, comm):
        pass

    def grad(self, name, val):
        pass


def _layer_fwd(x, p, e, ride=_NoRide()):
    T, D = x.shape
    tm = _tile(T, (512, 256, 128))
    tmm = _tile(T, (1024, 512, 256, 128))

    def carried(fn, *args, name, **kw):
        comm = ride.rider(name)
        out = fn(*args, name=name, comm=comm, **kw)
        ride.landed(comm)
        return out

    h, proj = carried(rms_proj, x, p["norm1_g"], p["w_in"], tm=tmm, tn=_tile(IN_W, (768,)), name="rms_proj")
    gains = (_lanes(p["a_q_g"], 8), _lanes(p["a_k_g"], 8), _lanes(p["c_q_g"], 8), _lanes(p["c_k_g"], 2))
    aq, ak, av, cq, ckk, cvv = prep_fwd(proj, e, gains, tm=_tile(T, (256, 128)), name="prep_fwd")
    ols = []
    for d in DILATIONS:
        ols += carried(band_attn_fwd, aq, ak, av, None, dil=d, max_dist=A_DIST, group=1, name=f"dil_attn_fwd_{d}")
    mix = dil_combine_fwd(ols, tm=tm, name="dil_combine_fwd")
    mix = conv_fwd(proj, mix, p["conv_w"], p["conv_b"], p["conv_ln_g"], p["conv_ln_b"], tb=tm, name="conv_fwd")
    sinks = jnp.repeat(p["c_sinks"].reshape(-1), HD).reshape(1, C_W)
    o_c, l_c, mix = carried(band_attn_fwd, cq, ckk, cvv, sinks, dil=1, max_dist=C_DIST, group=4, mix=mix,
                            name="swa_attn_fwd")
    x1 = carried(matmul_res, mix, p["w_out"], x, tm=tmm, tn=_tile(D, (1024, 512, 256)), name="out_proj")
    F = p["w_gate"].shape[1]
    h2, gate, up, act = carried(rms_swiglu, x1, p["norm2_g"], p["w_gate"], p["w_up"], tm=tmm,
                                tn=_tile(F, (512, 256, 128)), name="rms_swiglu")
    x2 = carried(matmul_res, act, p["w_down"], x1, tm=tmm, tn=_tile(D, (512, 256)), name="ffn_down")
    saved = dict(x=x, h=h, proj=proj, gains=gains, aq=aq, ak=ak, av=av, cq=cq, ckk=ckk, cvv=cvv, ols=ols, o_c=o_c,
                 l_c=l_c, sinks=sinks, mix=mix, x1=x1, h2=h2, gate=gate, up=up, act=act)
    return x2, saved


def _layer_bwd(dx2, dx2b, p, s, e, ride=_NoRide()):
    T, D = dx2.shape
    F = p["w_gate"].shape[1]
    tm = _tile(T, (512, 256, 128))
    tmm = _tile(T, (1024, 512, 256, 128))
    tkT = _tile(T, (2048, 1024, 512))
    tF = _tile(F, (512, 256, 128))
    tD = _tile(D, (1024, 512, 256))
    g = {}

    def carried(fn, *args, name, **kw):
        comm = ride.rider(name)
        out = fn(*args, name=name, comm=comm, **kw)
        ride.landed(comm)
        return out

    def big(n, val):
        g[n] = val
        ride.grad(n, val)

    d_gate, d_up = carried(nt_swiglu_bwd, dx2b, p["w_down"], s["gate"], s["up"], tm=tmm, tn=tF, name="ffn_down_bwd")
    big("w_down", tn_matmul(s["act"], dx2b, tm=tF, tn=tD, tk=tkT, name="grad_w_down"))
    big("w_gate", carried(tn_matmul, s["h2"], d_gate, tm=tD, tn=tF, tk=tkT, by_chip=True, name="grad_w_gate"))
    big("w_up", carried(tn_matmul, s["h2"], d_up, tm=tD, tn=tF, tk=tkT, by_chip=True, name="grad_w_up"))
    dx1, dx1b, g["norm2_g"] = carried(nt_rms_bwd, [(d_gate, p["w_gate"]), (d_up, p["w_up"])], s["x1"], p["norm2_g"],
                                      dx2, tm=tm, tn=_tile(D, (512, 256)), name="ffn_in_bwd")
    dmix = nt_plain(dx1b, p["w_out"], tm=tmm, tn=tD, name="out_proj_bwd")
    big("w_out", tn_matmul(s["mix"], dx1b, tm=1024, tn=tD, tk=tkT, name="grad_w_out"))
    dos = dil_combine_bwd(s["ols"], dmix, e, tm=tm, name="dil_combine_bwd")
    da = []
    for n, d in enumerate(DILATIONS):
        da += carried(band_attn_bwd, s["aq"], s["ak"], s["av"], s["ols"][2 * n + 1], dos[n], dos[3 + n], dil=d,
                      max_dist=A_DIST, group=1, name=f"dil_attn_bwd_{d}")
    du, dgt, gw, gb, glg, glb = carried(conv_bwd, s["proj"], dmix, p["conv_w"], p["conv_b"], p["conv_ln_g"],
                                        p["conv_ln_b"], tb=tm, name="conv_bwd")
    g["conv_w"], g["conv_b"], g["conv_ln_g"], g["conv_ln_b"] = gw[:CONV_K], gb, glg, glb
    do_c, dd_c, dsink = swa_pre_bwd(s["o_c"], s["l_c"], dmix, s["sinks"], e, tm=_tile(T, (256, 128)), name="swa_pre_bwd")
    g["c_sinks"] = dsink.reshape(-1, HD)[:, 0]
    dcq, dckk, dcvv = carried(band_attn_bwd, s["cq"], s["ckk"], s["cvv"], s["l_c"], do_c, dd_c, dil=1, max_dist=C_DIST,
                              group=4, name="swa_attn_bwd")
    dproj, gaq, gak, gcq, gck = prep_bwd(s["proj"], e, s["gains"], da, (dcq, dckk, dcvv), (du, dgt),
                                         tm=_tile(T, (256, 128)), name="prep_bwd")
    g["a_q_g"] = gaq.reshape(-1, HD).sum(0)
    g["a_k_g"] = gak.reshape(-1, HD).sum(0)
    g["c_q_g"] = gcq.reshape(-1, HD).sum(0)
    g["c_k_g"] = gck.reshape(-1, HD).sum(0)
    big("w_in", tn_matmul(s["h"], dproj, tm=tD, tn=_tile(IN_W, (1280,)), tk=tkT, name="grad_w_in"))
    dx, dxb, g["norm1_g"] = carried(nt_rms_bwd, [(dproj, p["w_in"])], s["x"], p["norm1_g"], dx1, tm=tmm,
                                    tn=_tile(D, (512, 256)), name="in_proj_bwd")
    return dx, dxb, g


BIG = ("w_in", "w_out", "w_gate", "w_up", "w_down")
COL_SHARDED = ("w_in", "w_gate", "w_up")
SMALL = ("norm1_g", "a_q_g", "a_k_g", "conv_w", "conv_b", "conv_ln_g", "conv_ln_b", "c_q_g", "c_k_g", "c_sinks", "norm2_g")


def _to_pieces(g, name):
    if g.ndim == 3:
        return g.reshape(4, 2, g.shape[1] // 2, g.shape[2])
    R, C = g.shape
    if name in COL_SHARDED:
        return g.reshape(2, R // 2, 4, C // 4).transpose(2, 0, 1, 3)
    return g.reshape(4, 2, R // 8, C)


def _from_gathered(w, name):
    _, _, r, c = w.shape
    if name in COL_SHARDED:
        return w.transpose(1, 2, 0, 3).reshape(2 * r, 4 * c)
    return w.reshape(8 * r, c)


def _shard(W, l, n):
    w = W[n][l]
    return w.astype(MXU).reshape(2, w.shape[0] // 2, w.shape[1])


class _LayerParams(dict):
    def __init__(self, layer, full, small):
        super().__init__(small)
        self.layer, self.full = layer, full

    def __missing__(self, n):
        return self.full[(self.layer, n)]


FWD_RIDES = {
    (0, "rms_proj"): ((0, "w_out"), (0, "w_gate")),
    (0, "swa_attn_fwd"): ((0, "w_up"),),
    (0, "rms_swiglu"): ((0, "w_down"), (1, "w_in"), (1, "w_out")),
    (0, "ffn_down"): ((1, "w_gate"),),
    (1, "rms_proj"): ((1, "w_up"),),
    (1, "rms_swiglu"): ((1, "w_down"),),
}
BWD_RIDES = {
    "grad_w_gate": (("w_down",), ()),
    "grad_w_up": (("w_gate",), ("w_down",)),
    "ffn_in_bwd": (("w_up",), ("w_gate",)),
    "dil_attn_bwd_1": (("w_out",), ()),
    "dil_attn_bwd_16": ((), ("w_up",)),
    "conv_bwd": ((), ("w_out",)),
    "in_proj_bwd": (("w_in",), ()),
}
IN_PLACE = ("w_gate", "w_up")


class _FwdRide:
    def __init__(self, layer, W, full):
        self.layer, self.W, self.full = layer, W, full

    def rider(self, name):
        keys = FWD_RIDES.get((self.layer, name))
        if not keys:
            return None
        comm = GatherComm([_shard(self.W, l, n) for l, n in keys], [n in IN_PLACE for _, n in keys])
        comm.keys = keys
        return comm

    def landed(self, comm):
        if comm is not None:
            for (l, n), g in zip(comm.keys, comm.results):
                self.full[(l, n)] = g if n in IN_PLACE else _from_gathered(g, n)


class _GradFlow:
    def __init__(self):
        self.pieces, self.sums, self.landed, self.pending = {}, {}, {}, []

    def swap(self, keys):
        comm = SwapComm([self.pieces[k] for k in keys])
        comm.keys, comm.kind = list(keys), "swap"
        return comm

    def exchange(self, keys):
        if not keys:
            return None
        comm = ExchangeComm([self.sums[k] for k in keys])
        comm.keys, comm.kind = list(keys), "exchange"
        return comm

    def take_pending(self):
        keys, self.pending = self.pending, []
        return keys

    def land(self, comm):
        if comm is None:
            return
        if isinstance(comm, MultiComm):
            for sub in comm.comms:
                self.land(sub)
            return
        for k, res in zip(comm.keys, comm.results):
            if comm.kind == "swap":
                r = res.shape[1]
                self.sums[k] = add_halves(self.pieces[k], res, tr=_tile(r, (256, 176, 128, 64, 32, 16)),
                                          name="grad_chip_sum")
                if k[1] == "w_in":
                    self.pending.append(k)
            else:
                self.landed[k] = res


class _BwdRide:
    def __init__(self, layer, flow):
        self.layer, self.flow = layer, flow

    def grad(self, name, val):
        self.flow.pieces[(self.layer, name)] = _to_pieces(val, name)

    def rider(self, name):
        if name == "ffn_down_bwd":
            return self.flow.exchange(self.flow.take_pending())
        swaps, exchanges = BWD_RIDES.get(name, ((), ()))
        comms = []
        if swaps:
            comms.append(self.flow.swap([(self.layer, n) for n in swaps]))
        if exchanges:
            comms.append(self.flow.exchange([(self.layer, n) for n in exchanges]))
        return MultiComm(comms) if comms else None

    def landed(self, comm):
        self.flow.land(comm)


def _pack(items, rows):
    flat = jnp.concatenate([a.reshape(-1).astype(f32) for a in items])
    return jnp.pad(flat, (0, rows * 128 - flat.shape[0])).reshape(rows, 128)


def _unpack(packed, shapes):
    flat = packed.reshape(-1)
    out, off = [], 0
    for shp in shapes:
        n = 1
        for d in shp:
            n *= d
        out.append(flat[off:off + n].reshape(shp))
        off += n
    return out


def kernel(x, norm1_g, w_in, a_q_g, a_k_g, conv_w, conv_b, conv_ln_g, conv_ln_b, c_q_g, c_k_g, c_sinks, w_out, norm2_g, w_gate, w_up, w_down, loss_target, m_norm1_g, m_w_in, m_a_q_g, m_a_k_g, m_conv_w, m_conv_b, m_conv_ln_g, m_conv_ln_b, m_c_q_g, m_c_k_g, m_c_sinks, m_w_out, m_norm2_g, m_w_gate, m_w_up, m_w_down, v_norm1_g, v_w_in, v_a_q_g, v_a_k_g, v_conv_w, v_conv_b, v_conv_ln_g, v_conv_ln_b, v_c_q_g, v_c_k_g, v_c_sinks, v_w_out, v_norm2_g, v_w_gate, v_w_up, v_w_down):
    W = dict(norm1_g=norm1_g, w_in=w_in, a_q_g=a_q_g, a_k_g=a_k_g, conv_w=conv_w, conv_b=conv_b, conv_ln_g=conv_ln_g,
             conv_ln_b=conv_ln_b, c_q_g=c_q_g, c_k_g=c_k_g, c_sinks=c_sinks, w_out=w_out, norm2_g=norm2_g, w_gate=w_gate,
             w_up=w_up, w_down=w_down)
    M = dict(norm1_g=m_norm1_g, w_in=m_w_in, a_q_g=m_a_q_g, a_k_g=m_a_k_g, conv_w=m_conv_w, conv_b=m_conv_b,
             conv_ln_g=m_conv_ln_g, conv_ln_b=m_conv_ln_b, c_q_g=m_c_q_g, c_k_g=m_c_k_g, c_sinks=m_c_sinks, w_out=m_w_out,
             norm2_g=m_norm2_g, w_gate=m_w_gate, w_up=m_w_up, w_down=m_w_down)
    V = dict(norm1_g=v_norm1_g, w_in=v_w_in, a_q_g=v_a_q_g, a_k_g=v_a_k_g, conv_w=v_conv_w, conv_b=v_conv_b,
             conv_ln_g=v_conv_ln_g, conv_ln_b=v_conv_ln_b, c_q_g=v_c_q_g, c_k_g=v_c_k_g, c_sinks=v_c_sinks, w_out=v_w_out,
             norm2_g=v_norm2_g, w_gate=v_w_gate, w_up=v_w_up, w_down=v_w_down)
    depth = norm1_g.shape[0]
    T, D = x.shape[1], x.shape[2]
    xs = x.reshape(T, D)
    chip = 2 * lax.axis_index("x") + lax.axis_index("y")
    e = _head_eye()

    full = {}
    first = GatherComm([_shard(W, 0, "w_in"), conv_w])
    _run_comm(first, name="gather_first")
    full[(0, "w_in")] = _from_gathered(first.results[0], "w_in")
    conv_full = first.results[1].transpose(1, 2, 0, 3).reshape(depth, CONV_K, B_W)
    params = []
    for l in range(depth):
        small = {n: W[n][l].reshape(1, -1) for n in SMALL if n != "conv_w"}
        small["conv_w"] = jnp.pad(conv_full[l], ((0, HALO - CONV_K), (0, 0)))
        params.append(_LayerParams(l, full, small))

    saved = []
    act = xs
    for l in range(depth):
        act, s = _layer_fwd(act, params[l], e, _FwdRide(l, W, full))
        saved.append(s)
    dy, dyb, loss_part = loss_head(act, loss_target.reshape(T, D), tm=_tile(T, (512, 256, 128)), name="loss_head")
    grads = [None] * depth
    flow = _GradFlow()
    for l in reversed(range(depth)):
        dy, dyb, grads[l] = _layer_bwd(dy, dyb, params[l], saved[l], e, _BwdRide(l, flow))
    grad_x = dy.reshape(x.shape)

    out = {}
    for n in ("w_down", "w_gate", "w_up", "w_out", "w_in"):
        last = flow.exchange(flow.take_pending())
        per_layer = [flow.landed[(l, n)] for l in range(depth)]
        r, cc = per_layer[0].shape[2], per_layer[0].shape[3]
        srcs = [jnp.stack([pl_[:, s].reshape(2 * r, cc) for pl_ in per_layer]) for s in range(4)]
        out[n] = adamw(W[n], M[n], V[n], srcs, tr=_tile(2 * r, (256, 176, 128, 64, 32, 16)), name="adamw_" + n,
                       comm=last)
        flow.land(last)

    small_shapes = []
    items = []
    for l in range(depth):
        for n in SMALL:
            a = grads[l][n]
            if n == "conv_w":
                a = a.reshape(CONV_K, 4, B_W // 4).transpose(1, 0, 2)
            items.append(a)
            small_shapes.append(a.shape)
    items.append(loss_part[0, 0:1])
    small_shapes.append((1,))
    total = sum(int(jnp.size(a)) for a in items)
    rows = -(-total // 1024) * 8
    summed = sum8(gather_small(_pack(items, rows), name="gather_small"), name="sum_small")
    parts = _unpack(summed, small_shapes)
    loss = parts[-1][0]
    small_g = {n: [] for n in SMALL}
    for l in range(depth):
        for i, n in enumerate(SMALL):
            a = parts[l * len(SMALL) + i]
            if n == "conv_w":
                a = lax.dynamic_index_in_dim(a, chip, axis=0, keepdims=False)
            small_g[n].append(a.reshape(W[n].shape[1:]))
    sw = [W[n] for n in SMALL]
    sm = [M[n] for n in SMALL]
    sv = [V[n] for n in SMALL]
    sg = [jnp.stack(small_g[n]) for n in SMALL]
    tot2 = sum(int(jnp.size(a)) for a in sw)
    rows2 = -(-tot2 // 1024) * 8
    res = adamw(_pack(sw, rows2)[None], _pack(sm, rows2)[None], _pack(sv, rows2)[None], [_pack(sg, rows2)[None]],
                tr=rows2, name="adamw_small")
    shapes2 = [a.shape for a in sw]
    small_out = [_unpack(r[0], shapes2) for r in res]
    for i, n in enumerate(SMALL):
        out[n] = [small_out[k][i] for k in range(4)]

    order = ("norm1_g", "w_in", "a_q_g", "a_k_g", "conv_w", "conv_b", "conv_ln_g", "conv_ln_b", "c_q_g", "c_k_g",
             "c_sinks", "w_out", "norm2_g", "w_gate", "w_up", "w_down")
    return (loss, grad_x, *[out[n][0] for n in order], *[out[n][1] for n in order], *[out[n][2] for n in order],
            *[out[n][3] for n in order])
```

```python
import functools

import jax
import jax.numpy as jnp
from jax import lax
from jax.experimental import pallas as pl
from jax.experimental.pallas import tpu as pltpu

f32 = jnp.float32
MXU = jnp.bfloat16
S = jax.ShapeDtypeStruct
MESH = pl.DeviceIdType.MESH

EPS = 1e-6
NEG = -1e30
HD = 64
BLK = 128
A_W, B_W, C_W = 512, 512, 1024
KV_W = 128
IN_W = 3 * A_W + 2 * B_W + C_W + 2 * KV_W
CONV_K = 31
HALO = 32
DILATIONS = (1, 4, 16)
A_DIST, C_DIST = 128, 127
SCALE = HD ** -0.5
VMEM_LIMIT = 56 * 1024 * 1024
VMEM_TALL = 62 * 1024 * 1024

ADAM_LR, ADAM_B1, ADAM_B2, ADAM_EPS, ADAM_WD, ADAM_STEP = 0.001, 0.9, 0.999, 1e-08, 0.01, 10


def _cp(*sem, vmem=VMEM_LIMIT):
    return pltpu.CompilerParams(dimension_semantics=sem, vmem_limit_bytes=vmem)


ANY = pl.BlockSpec(memory_space=pl.ANY)


def _pallas(body, *, comm=None, name, grid, in_specs, out_specs, out_shape, scratch_shapes=(), compiler_params,
            input_output_aliases=None):
    aliases = dict(input_output_aliases or {})
    if comm is None:
        return pl.pallas_call(body, name=name, grid=grid, in_specs=in_specs, out_specs=out_specs, out_shape=out_shape,
                              scratch_shapes=list(scratch_shapes), compiler_params=compiler_params,
                              input_output_aliases=aliases)
    single = not isinstance(out_shape, (list, tuple))
    o_shapes = [out_shape] if single else list(out_shape)
    o_specs = [out_specs] if single else list(out_specs)
    n_in, n_out, n_sc = len(in_specs), len(o_shapes), len(scratch_shapes)
    nci, nco = len(comm.ins), len(comm.out_shapes)
    total = 1
    for g in grid:
        total *= g

    def carried(*refs):
        ins, cins = refs[:n_in], refs[n_in:n_in + nci]
        o0 = n_in + nci
        outs, couts = refs[o0:o0 + n_out], refs[o0 + n_out:o0 + n_out + nco]
        s0 = o0 + n_out + nco
        scratch, sems = refs[s0:s0 + n_sc], refs[s0 + n_sc:]
        step = pl.program_id(0)
        for axis in range(1, len(grid)):
            step = step * grid[axis] + pl.program_id(axis)

        @pl.when(step == 0)
        def _():
            comm.start(cins, couts, sems)

        body(*ins, *outs, *scratch)

        @pl.when(step == (3 * total) // 4)
        def _():
            comm.mid(cins, couts, sems)

        @pl.when(step == total - 1)
        def _():
            comm.finish(cins, couts, sems)

    call = pl.pallas_call(carried, name=name, grid=grid, in_specs=list(in_specs) + [ANY] * nci,
                          out_specs=o_specs + [ANY] * nco, out_shape=o_shapes + list(comm.out_shapes),
                          scratch_shapes=list(scratch_shapes) + list(comm.sem_shapes), compiler_params=compiler_params,
                          input_output_aliases=aliases)

    def run(*args):
        res = call(*args, *comm.ins)
        comm.results = list(res[n_out:])
        return res[0] if single else list(res[:n_out])

    return run


class MultiComm:
    def __init__(self, comms):
        self.comms = list(comms)
        self.ins = [a for c in self.comms for a in c.ins]
        self.out_shapes = [s for c in self.comms for s in c.out_shapes]
        self.sem_shapes = [s for c in self.comms for s in c.sem_shapes]

    def _each(self, cins, couts, sems):
        i = o = s = 0
        for c in self.comms:
            ni, no, ns = len(c.ins), len(c.out_shapes), len(c.sem_shapes)
            yield c, cins[i:i + ni], couts[o:o + no], sems[s:s + ns]
            i, o, s = i + ni, o + no, s + ns

    def start(self, cins, couts, sems):
        for c, a, b, d in self._each(cins, couts, sems):
            c.start(a, b, d)

    def mid(self, cins, couts, sems):
        for c, a, b, d in self._each(cins, couts, sems):
            c.mid(a, b, d)

    def finish(self, cins, couts, sems):
        for c, a, b, d in self._each(cins, couts, sems):
            c.finish(a, b, d)

    @property
    def results(self):
        return [r for c in self.comms for r in c.results]

    @results.setter
    def results(self, vals):
        o = 0
        for c in self.comms:
            c.results = list(vals[o:o + len(c.out_shapes)])
            o += len(c.out_shapes)


def _run_comm(comm, *, name):
    nci, nco = len(comm.ins), len(comm.out_shapes)

    def body(*refs):
        cins, couts, sems = refs[:nci], refs[nci:nci + nco], refs[nci + nco:]
        comm.start(cins, couts, sems)
        comm.mid(cins, couts, sems)
        comm.finish(cins, couts, sems)

    comm.results = list(pl.pallas_call(body, name=name, in_specs=[ANY] * nci, out_specs=[ANY] * nco,
                                       out_shape=list(comm.out_shapes), scratch_shapes=list(comm.sem_shapes))(*comm.ins))


def _nt(a, b):
    return lax.dot_general(a, b, (((1,), (1,)), ((), ())), preferred_element_type=f32)


def _tn(a, b):
    return lax.dot_general(a, b, (((0,), (0,)), ((), ())), preferred_element_type=f32)


def _nn(a, b):
    return jnp.dot(a, b, preferred_element_type=f32)


def _sigmoid(x):
    return 1.0 / (1.0 + jnp.exp(-x))


def _seg_sum(v, e_ref):
    hi = v.astype(jnp.bfloat16)
    lo = (v - hi.astype(f32)).astype(jnp.bfloat16)
    e = e_ref[...]
    return _nn(hi, e) + _nn(lo, e)


def _seg_sum128(v, e_ref):
    e = e_ref[0:128, 0:128]
    hi = v.astype(jnp.bfloat16)
    lo = (v - hi.astype(f32)).astype(jnp.bfloat16)
    return _nn(hi, e) + _nn(lo, e)


def _head_eye():
    r = lax.broadcasted_iota(jnp.int32, (512, 512), 0) // HD
    c = lax.broadcasted_iota(jnp.int32, (512, 512), 1) // HD
    return (r == c).astype(jnp.bfloat16)


def _once_per_row_block(tm, width):
    return pl.BlockSpec((tm, width), lambda i, j: (i, 0), pipeline_mode=pl.Buffered(1))


def _rms_norm_rows(x_ref, g_ref, h_ref, tm):
    def chunk(c, carry):
        rows = pl.ds(c * BLK, BLK)
        xf = x_ref[rows, :]
        r = lax.rsqrt(jnp.mean(xf * xf, axis=-1, keepdims=True) + EPS)
        h_ref[rows, :] = (xf * r * g_ref[...]).astype(MXU)
        return carry
    lax.fori_loop(0, tm // BLK, chunk, 0)


def rms_proj(x, g, w, *, tm, tn, name, comm=None):
    T, D = x.shape
    N = w.shape[1]

    def body(x_ref, g_ref, w_ref, h_ref, o_ref):
        @pl.when(pl.program_id(1) == 0)
        def _():
            _rms_norm_rows(x_ref, g_ref, h_ref, tm)
        o_ref[...] = _nn(h_ref[...], w_ref[...])

    return _pallas(
        body, comm=comm, name=name, grid=(T // tm, N // tn),
        in_specs=[_once_per_row_block(tm, D), pl.BlockSpec((1, D), lambda i, j: (0, 0)),
                  pl.BlockSpec((D, tn), lambda i, j: (0, j))],
        out_specs=[_once_per_row_block(tm, D), pl.BlockSpec((tm, tn), lambda i, j: (i, j))],
        out_shape=[S((T, D), MXU), S((T, N), f32)],
        compiler_params=_cp("arbitrary", "arbitrary"),
    )(x, g, w)


def rms_swiglu(x, g, wg, wu, *, tm, tn, name, comm=None):
    T, D = x.shape
    N = wg.shape[1]

    def body(x_ref, g_ref, wg_ref, wu_ref, h_ref, gate_ref, up_ref, act_ref):
        @pl.when(pl.program_id(1) == 0)
        def _():
            _rms_norm_rows(x_ref, g_ref, h_ref, tm)
        h = h_ref[...]
        gate = _nn(h, wg_ref[...])
        up = _nn(h, wu_ref[...])
        gate_ref[...] = gate
        up_ref[...] = up
        act_ref[...] = (gate * _sigmoid(gate) * up).astype(MXU)

    wspec = pl.BlockSpec((D, tn), lambda i, j: (0, j))
    ospec = pl.BlockSpec((tm, tn), lambda i, j: (i, j))
    return _pallas(
        body, comm=comm, name=name, grid=(T // tm, N // tn),
        in_specs=[_once_per_row_block(tm, D), pl.BlockSpec((1, D), lambda i, j: (0, 0)), wspec, wspec],
        out_specs=[_once_per_row_block(tm, D), ospec, ospec, ospec],
        out_shape=[S((T, D), MXU), S((T, N), f32), S((T, N), f32), S((T, N), MXU)],
        compiler_params=_cp("arbitrary", "arbitrary"),
    )(x, g, wg, wu)


def matmul_res(a, w, res, *, tm, tn, name, comm=None):
    T, K = a.shape
    N = w.shape[1]

    def body(a_ref, w_ref, r_ref, o_ref):
        o_ref[...] = r_ref[...] + _nn(a_ref[...], w_ref[...])

    return _pallas(
        body, comm=comm, name=name, grid=(T // tm, N // tn),
        in_specs=[_once_per_row_block(tm, K), pl.BlockSpec((K, tn), lambda i, j: (0, j)),
                  pl.BlockSpec((tm, tn), lambda i, j: (i, j))],
        out_specs=pl.BlockSpec((tm, tn), lambda i, j: (i, j)),
        out_shape=S((T, N), f32),
        compiler_params=_cp("arbitrary", "arbitrary"),
    )(a, w, res)


def nt_plain(a, w, *, tm, tn, name, comm=None):
    T, K = a.shape
    N = w.shape[0]

    def body(a_ref, w_ref, o_ref):
        o_ref[...] = _nt(a_ref[...], w_ref[...])

    return _pallas(
        body, comm=comm, name=name, grid=(T // tm, N // tn),
        in_specs=[pl.BlockSpec((tm, K), lambda i, j: (i, 0)), pl.BlockSpec((tn, K), lambda i, j: (j, 0))],
        out_specs=pl.BlockSpec((tm, tn), lambda i, j: (i, j)),
        out_shape=S((T, N), f32),
        compiler_params=_cp("arbitrary", "arbitrary"),
    )(a, w)


def nt_swiglu_bwd(dy, wd, gate, up, *, tm, tn, name, comm=None):
    T, D = dy.shape
    F = wd.shape[0]

    def body(dy_ref, w_ref, g_ref, u_ref, dg_ref, du_ref):
        d_act = _nt(dy_ref[...], w_ref[...])
        g = g_ref[...]
        sg = _sigmoid(g)
        du_ref[...] = (d_act * (g * sg)).astype(MXU)
        dg_ref[...] = (d_act * u_ref[...] * (sg * (1.0 + g * (1.0 - sg)))).astype(MXU)

    blk = pl.BlockSpec((tm, tn), lambda i, j: (i, j))
    return _pallas(
        body, comm=comm, name=name, grid=(T // tm, F // tn),
        in_specs=[_once_per_row_block(tm, D), pl.BlockSpec((tn, D), lambda i, j: (j, 0)), blk, blk],
        out_specs=[blk, blk],
        out_shape=[S((T, F), MXU), S((T, F), MXU)],
        compiler_params=_cp("arbitrary", "arbitrary"),
    )(dy, wd, gate, up)


def nt_rms_bwd(terms, x, g, dres, *, tm, tn, name, comm=None):
    T, D = x.shape
    K = terms[0][0].shape[1]
    nj = D // tn
    nt = len(terms)
    ni = T // tm

    def body(*refs):
        a_refs = refs[0:2 * nt:2]
        w_refs = refs[1:2 * nt:2]
        x_ref, g_ref, r_ref, dx_ref, dxb_ref, dg_ref, acc_ref = refs[2 * nt:]
        i, j = pl.program_id(0), pl.program_id(1)
        part = _nt(a_refs[0][...], w_refs[0][...])
        for t in range(1, nt):
            part += _nt(a_refs[t][...], w_refs[t][...])
        acc_ref[j] = part

        @pl.when(j == nj - 1)
        def _():
            def chunk(c, dgain):
                rows = pl.ds(c * BLK, BLK)
                dh = jnp.concatenate([acc_ref[jj, rows, :] for jj in range(nj)], axis=1)
                xf = x_ref[rows, :]
                r = lax.rsqrt(jnp.mean(xf * xf, axis=-1, keepdims=True) + EPS)
                y = xf * r
                dy = dh * g_ref[...]
                dx = r_ref[rows, :] + r * (dy - y * jnp.mean(dy * y, axis=-1, keepdims=True))
                dx_ref[rows, :] = dx
                dxb_ref[rows, :] = dx.astype(MXU)
                return dgain + jnp.sum(dh * y, axis=0, keepdims=True)

            dgain = lax.fori_loop(0, tm // BLK, chunk, jnp.zeros((1, D), f32))

            @pl.when(i == 0)
            def _():
                dg_ref[...] = dgain

            @pl.when(i > 0)
            def _():
                dg_ref[...] += dgain

    in_specs, args = [], []
    for a, w in terms:
        in_specs += [pl.BlockSpec((tm, K), lambda i, j: (i, 0), pipeline_mode=pl.Buffered(1)),
                     pl.BlockSpec((tn, K), lambda i, j: (j, 0))]
        args += [a, w]
    row = pl.BlockSpec((tm, D), lambda i, j: (i, 0), pipeline_mode=pl.Buffered(1))
    vec = pl.BlockSpec((1, D), lambda i, j: (0, 0))
    in_specs += [row, vec, row]
    return _pallas(
        body, comm=comm, name=name, grid=(ni, nj), in_specs=in_specs,
        out_specs=[row, row, vec],
        out_shape=[S((T, D), f32), S((T, D), MXU), S((1, D), f32)],
        scratch_shapes=[pltpu.VMEM((nj, tm, tn), f32)],
        compiler_params=_cp("arbitrary", "arbitrary", vmem=VMEM_TALL),
    )(*args, x, g, dres)


def tn_matmul(a, b, *, tm, tn, tk, name, by_chip=False, comm=None):
    T, M = a.shape
    N = b.shape[1]
    if by_chip:
        tn = N // 4
        out_spec = pl.BlockSpec((None, tm, tn), lambda i, j, k: (j, i, 0))
        out_shape = S((4, M, tn), f32)
    else:
        out_spec = pl.BlockSpec((tm, tn), lambda i, j, k: (i, j))
        out_shape = S((M, N), f32)

    def body(a_ref, b_ref, o_ref):
        part = _tn(a_ref[...], b_ref[...])

        @pl.when(pl.program_id(2) == 0)
        def _():
            o_ref[...] = part

        @pl.when(pl.program_id(2) > 0)
        def _():
            o_ref[...] += part

    return _pallas(
        body, comm=comm, name=name, grid=(M // tm, N // tn, T // tk),
        in_specs=[pl.BlockSpec((tk, tm), lambda i, j, k: (k, i)), pl.BlockSpec((tk, tn), lambda i, j, k: (k, j))],
        out_specs=out_spec, out_shape=out_shape,
        compiler_params=_cp("arbitrary", "arbitrary", "arbitrary"),
    )(a, b)


def loss_head(y, target, *, tm, name):
    T, D = y.shape
    ni = T // tm

    def body(y_ref, t_ref, dy_ref, dyb_ref, l_ref, acc_ref):
        i = pl.program_id(0)
        e = y_ref[...] - t_ref[...]
        dy = e * (1.0 / D)
        dy_ref[...] = dy
        dyb_ref[...] = dy.astype(MXU)
        part = jnp.sum(e * e, axis=0, keepdims=True)

        @pl.when(i == 0)
        def _():
            acc_ref[...] = part

        @pl.when(i > 0)
        def _():
            acc_ref[...] += part

        @pl.when(i == ni - 1)
        def _():
            tot = jnp.sum(acc_ref[...], axis=1, keepdims=True) * (0.5 / D)
            l_ref[...] = jnp.broadcast_to(tot, (1, 128))

    row = pl.BlockSpec((tm, D), lambda i: (i, 0))
    return pl.pallas_call(
        body, name=name, grid=(ni,), in_specs=[row, row],
        out_specs=[row, row, pl.BlockSpec((1, 128), lambda i: (0, 0))],
        out_shape=[S((T, D), f32), S((T, D), MXU), S((1, 128), f32)],
        scratch_shapes=[pltpu.VMEM((1, D), f32)],
        compiler_params=_cp("arbitrary"),
    )(y, target)


def _qk_norm(v, gain, e_ref):
    r = lax.rsqrt(_seg_sum(v * v, e_ref) * (1.0 / HD) + EPS)
    return v * r * gain


def _dup_halves(pair):
    rolled = pltpu.roll(pair, HD, 1)
    lo = lax.broadcasted_iota(jnp.int32, pair.shape, 1) < HD
    return jnp.where(lo, pair, rolled), jnp.where(lo, rolled, pair)


def prep_fwd(proj, e, gains, *, tm, name):
    T = proj.shape[0]

    def body(p_ref, e_ref, gaq, gak, gcq, gck, aq, ak, av, cq, ckk, cvv):
        aq[...] = _qk_norm(p_ref[:, 0:512], gaq[...], e_ref)
        ak[...] = _qk_norm(p_ref[:, 512:1024], gak[...], e_ref)
        av[...] = p_ref[:, 1024:1536]
        cq[:, 0:512] = _qk_norm(p_ref[:, 2560:3072], gcq[...], e_ref).astype(MXU)
        cq[:, 512:1024] = _qk_norm(p_ref[:, 3072:3584], gcq[...], e_ref).astype(MXU)
        kraw = p_ref[:, 3584:3712]
        kn = kraw * lax.rsqrt(_seg_sum128(kraw * kraw, e_ref) * (1.0 / HD) + EPS) * gck[...]
        k0, k1 = _dup_halves(kn)
        ckk[:, 0:128] = k0.astype(MXU)
        ckk[:, 128:256] = k1.astype(MXU)
        v0, v1 = _dup_halves(p_ref[:, 3712:3840])
        cvv[:, 0:128] = v0.astype(MXU)
        cvv[:, 128:256] = v1.astype(MXU)

    def vec(n):
        return pl.BlockSpec((1, n), lambda i: (0, 0))

    def rows(n):
        return pl.BlockSpec((tm, n), lambda i: (i, 0))

    return pl.pallas_call(
        body, name=name, grid=(T // tm,),
        in_specs=[rows(IN_W), pl.BlockSpec((512, 512), lambda i: (0, 0)), vec(512), vec(512), vec(512), vec(128)],
        out_specs=[rows(512), rows(512), rows(512), rows(1024), rows(256), rows(256)],
        out_shape=[S((T, 512), f32)] * 3 + [S((T, 1024), MXU), S((T, 256), MXU), S((T, 256), MXU)],
        compiler_params=_cp("arbitrary"),
    )(proj, e, *gains)


def _band_mask(max_dist, shut):
    r = lax.broadcasted_iota(jnp.int32, (2 * BLK, 2 * BLK), 0) & (BLK - 1)
    c = lax.broadcasted_iota(jnp.int32, (2 * BLK, 2 * BLK), 1)
    prev = jnp.logical_and(c < BLK, c >= r + (BLK - max_dist) + shut)
    return jnp.logical_or(prev, jnp.logical_and(c >= BLK, c - BLK <= r))


def _prev_mask(max_dist, shut):
    r = lax.broadcasted_iota(jnp.int32, (2 * BLK, BLK), 0) & (BLK - 1)
    c = lax.broadcasted_iota(jnp.int32, (2 * BLK, BLK), 1)
    return c >= r + (BLK - max_dist) + shut


def _head_masks():
    lo = (lax.broadcasted_iota(jnp.int32, (BLK, BLK), 1) < HD).astype(f32)
    return lo.astype(MXU), (1.0 - lo).astype(MXU)


def _stack_heads(x, hm):
    return jnp.concatenate([x * hm[0], x * hm[1]], axis=0)


def _unstack_heads(y, lane_lo):
    return jnp.where(lane_lo, y[0:BLK], y[BLK:2 * BLK])


def _rows(ref, start, dil):
    if dil == 1:
        return ref[pl.ds(start, BLK), :]
    return ref[pl.ds(start, BLK, stride=dil), :]


def _set_rows(ref, start, dil, val):
    if dil == 1:
        ref[pl.ds(start, BLK), :] = val
    else:
        ref[pl.ds(start, BLK, stride=dil), :] = val


def _attn_geometry(T, dil):
    span = BLK * dil
    n = max(1, 512 // span)
    return span, n, T // (span * n)


def band_attn_fwd(q, k, v, sinks, *, dil, max_dist, group, name, mix=None, comm=None):
    T = q.shape[0]
    P = q.shape[1] // BLK
    span, n, nb = _attn_geometry(T, dil)

    def body(*refs):
        s_ref = m_ref = None
        q_ref, kc_ref, kp_ref, vc_ref, vp_ref = refs[:5]
        rest = list(refs[5:])
        if sinks is not None:
            s_ref = rest.pop(0)
        if mix is not None:
            rest.pop(0)
            o_ref, l_ref, m_ref = rest
        else:
            o_ref, l_ref = rest
        b = pl.program_id(0)
        mask = _band_mask(max_dist, 0)
        mask0 = _band_mask(max_dist, jnp.where(b > 0, 0, BLK + 1))
        lane_lo = lax.broadcasted_iota(jnp.int32, (BLK, BLK), 1) < HD
        hm = _head_masks()
        if sinks is not None:
            row_lo = lax.broadcasted_iota(jnp.int32, (1, BLK), 1) < HD
            sk0 = jnp.max(jnp.where(row_lo, s_ref[...], NEG), axis=1, keepdims=True)
            sk1 = jnp.max(jnp.where(row_lo, NEG, s_ref[...]), axis=1, keepdims=True)
            sk = jnp.where(lax.broadcasted_iota(jnp.int32, (2 * BLK, 1), 0) < BLK, sk0, sk1)

        def load(r, sub):
            at = r + sub * span
            kc, vc = _rows(kc_ref, at, dil).astype(MXU), _rows(vc_ref, at, dil).astype(MXU)
            if sub == 0:
                kp, vp = _rows(kp_ref, r, dil).astype(MXU), _rows(vp_ref, r, dil).astype(MXU)
            else:
                kp, vp = _rows(kc_ref, at - span, dil).astype(MXU), _rows(vc_ref, at - span, dil).astype(MXU)
            qst = _stack_heads(_rows(q_ref, at, dil).astype(MXU), hm)
            return (qst, jnp.concatenate([kp, kc], axis=0), jnp.concatenate([vp, vc], axis=0),
                    mask0 if sub == 0 else mask, at)

        def attend(items):
            ss = [jnp.where(m_, _nt(qst, kcat) * SCALE, NEG) for qst, kcat, _, m_, _ in items]
            ms = [jnp.max(s, axis=1, keepdims=True) for s in ss]
            if sinks is not None:
                ms = [jnp.maximum(m, sk) for m in ms]
            ps = [jnp.exp(s - m) for s, m in zip(ss, ms)]
            dens = [jnp.sum(p_, axis=1, keepdims=True) for p_ in ps]
            if sinks is not None:
                dens = [d + jnp.exp(sk - m) for d, m in zip(dens, ms)]
            outs = [_nn(p_.astype(MXU), it[2]) / d for p_, it, d in zip(ps, items, dens)]
            for it, o, m, d in zip(items, outs, ms, dens):
                lse = m + jnp.log(d)
                if m_ref is not None:
                    _set_rows(m_ref, it[4], dil, _unstack_heads(o, lane_lo).astype(MXU))
                _set_rows(o_ref, it[4], dil, _unstack_heads(o, lane_lo))
                _set_rows(l_ref, it[4], dil, jnp.where(lane_lo, lse[0:BLK], lse[BLK:2 * BLK]))

        if dil * n <= 4:
            work = [(r, sub) for r in range(dil) for sub in range(n)]
            for g in range(0, len(work), 2):
                attend([load(*w) for w in work[g:g + 2]])
        else:
            def two_streams(i, carry):
                attend([load(2 * i, 0), load(2 * i + 1, 0)])
                return carry
            lax.fori_loop(0, dil // 2, two_streams, 0)

    rows_per_step = span * n
    qspec = pl.BlockSpec((rows_per_step, BLK), lambda b, p: (b, p))
    cur = pl.BlockSpec((rows_per_step, BLK), lambda b, p: (b, p // group))
    prev = pl.BlockSpec((span, BLK), lambda b, p: (jnp.maximum(b * n - 1, 0), p // group))
    in_specs = [qspec, cur, prev, cur, prev]
    args = [q, k, k, v, v]
    if sinks is not None:
        in_specs.append(pl.BlockSpec((1, BLK), lambda b, p: (0, p)))
        args.append(sinks)
    out_specs, out_shape, aliases = [qspec, qspec], [S(q.shape, f32), S(q.shape, f32)], {}
    if mix is not None:
        first_block = mix.shape[1] // BLK - P
        aliases = {len(args): 2}
        in_specs.append(ANY)
        args.append(mix)
        out_specs.append(pl.BlockSpec((rows_per_step, BLK), lambda b, p: (b, first_block + p)))
        out_shape.append(S(mix.shape, mix.dtype))
    return _pallas(
        body, comm=comm, name=name, grid=(nb, P), in_specs=in_specs, out_specs=out_specs, out_shape=out_shape,
        compiler_params=_cp("arbitrary", "arbitrary"), input_output_aliases=aliases,
    )(*args)


def band_attn_bwd(q, k, v, lse, do, dd, *, dil, max_dist, group, name, comm=None):
    T = q.shape[0]
    P = q.shape[1] // BLK
    span, n, nb = _attn_geometry(T, dil)
    assert group == 1 or dil == 1

    def body(q_ref, qn_ref, do_ref, don_ref, l_ref, ln_ref, d_ref, dn_ref, kc_ref, kp_ref, vc_ref, vp_ref,
             dq_ref, dk_ref, dv_ref):
        b, p = pl.program_id(0), pl.program_id(1)
        mask = _band_mask(max_dist, 0)
        mask0 = _band_mask(max_dist, jnp.where(b > 0, 0, BLK + 1))
        tail = _prev_mask(max_dist, jnp.where(b < nb - 1, 0, BLK + 1))
        lane_lo = lax.broadcasted_iota(jnp.int32, (BLK, BLK), 1) < HD
        hm = _head_masks()
        own_lanes = (lax.broadcasted_iota(jnp.int32, (2 * BLK, BLK), 1) < HD) == (
            lax.broadcasted_iota(jnp.int32, (2 * BLK, BLK), 0) < BLK)

        def per_row(x):
            return jnp.max(jnp.where(own_lanes, jnp.concatenate([x, x], axis=0), NEG), axis=1, keepdims=True)

        def q_side(refs, at):
            q_r, do_r, l_r, d_r = refs
            return (_stack_heads(_rows(q_r, at, dil).astype(MXU), hm), _stack_heads(_rows(do_r, at, dil).astype(MXU), hm),
                    per_row(_rows(l_r, at, dil)), per_row(_rows(d_r, at, dil)))

        def kv(ref, at):
            return _rows(ref, at, dil).astype(MXU)

        first = p % group == 0

        def put_kv(ref, at, val):
            if group == 1:
                _set_rows(ref, at, dil, val)
            else:
                @pl.when(first)
                def _():
                    ref[pl.ds(at, BLK), :] = val

                @pl.when(jnp.logical_not(first))
                def _():
                    ref[pl.ds(at, BLK), :] += val

        def stream(r):
            dks, dvs = [None] * n, [None] * n
            for sub in range(n):
                at = r + sub * span
                qst, dost, lrow, drow = q_side((q_ref, do_ref, l_ref, d_ref), at)
                if sub == 0:
                    kp, vp, m_ = kv(kp_ref, r), kv(vp_ref, r), mask0
                else:
                    kp, vp, m_ = kv(kc_ref, at - span), kv(vc_ref, at - span), mask
                kcat = jnp.concatenate([kp, kv(kc_ref, at)], axis=0)
                vcat = jnp.concatenate([vp, kv(vc_ref, at)], axis=0)
                pr = jnp.where(m_, jnp.exp(_nt(qst, kcat) * SCALE - lrow), 0.0)
                ds = (pr * (_nt(dost, vcat) - drow) * SCALE).astype(MXU)
                prb = pr.astype(MXU)
                _set_rows(dq_ref, at, dil, _unstack_heads(_nn(ds, kcat), lane_lo))
                if sub == 0:
                    dks[0] = _tn(ds[:, BLK:], qst)
                    dvs[0] = _tn(prb[:, BLK:], dost)
                else:
                    dkk, dvv = _tn(ds, qst), _tn(prb, dost)
                    dks[sub - 1] += dkk[0:BLK]
                    dvs[sub - 1] += dvv[0:BLK]
                    dks[sub], dvs[sub] = dkk[BLK:], dvv[BLK:]
            at = r + (n - 1) * span
            qst, dost, lrow, drow = q_side((qn_ref, don_ref, ln_ref, dn_ref), r)
            pr = jnp.where(tail, jnp.exp(_nt(qst, kv(kc_ref, at)) * SCALE - lrow), 0.0)
            ds = (pr * (_nt(dost, kv(vc_ref, at)) - drow) * SCALE).astype(MXU)
            dks[n - 1] += _tn(ds, qst)
            dvs[n - 1] += _tn(pr.astype(MXU), dost)
            for sub in range(n):
                put_kv(dk_ref, r + sub * span, dks[sub])
                put_kv(dv_ref, r + sub * span, dvs[sub])

        if dil <= 4:
            for r in range(dil):
                stream(r)
        else:
            def two_streams(i, carry):
                stream(2 * i)
                stream(2 * i + 1)
                return carry
            lax.fori_loop(0, dil // 2, two_streams, 0)

    rows_per_step = span * n
    qspec = pl.BlockSpec((rows_per_step, BLK), lambda b, p: (b, p))
    qnext = pl.BlockSpec((span, BLK), lambda b, p: (jnp.minimum((b + 1) * n, T // span - 1), p))
    cur = pl.BlockSpec((rows_per_step, BLK), lambda b, p: (b, p // group))
    prev = pl.BlockSpec((span, BLK), lambda b, p: (jnp.maximum(b * n - 1, 0), p // group))
    return _pallas(
        body, comm=comm, name=name, grid=(nb, P),
        in_specs=[qspec, qnext, qspec, qnext, qspec, qnext, qspec, qnext, cur, prev, cur, prev],
        out_specs=[qspec, cur, cur],
        out_shape=[S(q.shape, f32), S(k.shape, f32), S(k.shape, f32)],
        compiler_params=_cp("arbitrary", "arbitrary"),
    )(q, q, do, do, lse, lse, dd, dd, k, k, v, v)


def dil_combine_fwd(ols, *, tm, name):
    T = ols[0].shape[0]

    def body(o1, l1, o2, l2, o3, l3, out_ref):
        a, b, c = l1[...], l2[...], l3[...]
        m = jnp.maximum(jnp.maximum(a, b), c)
        ea, eb, ec = jnp.exp(a - m), jnp.exp(b - m), jnp.exp(c - m)
        out = (ea * o1[...] + eb * o2[...] + ec * o3[...]) / (ea + eb + ec)
        out_ref[...] = out.astype(MXU)

    row = pl.BlockSpec((tm, 512), lambda i: (i, 0))
    return pl.pallas_call(body, name=name, grid=(T // tm,), in_specs=[row] * 6, out_specs=row,
                          out_shape=S((T, A_W + B_W + C_W), MXU), compiler_params=_cp("arbitrary"))(*ols)


def dil_combine_bwd(ols, dmix, e, *, tm, name):
    T = ols[0].shape[0]

    def body(o1, l1, o2, l2, o3, l3, d_ref, e_ref, do1, do2, do3, dd1, dd2, dd3):
        a, b, c = l1[...], l2[...], l3[...]
        m = jnp.maximum(jnp.maximum(a, b), c)
        ea, eb, ec = jnp.exp(a - m), jnp.exp(b - m), jnp.exp(c - m)
        inv = 1.0 / (ea + eb + ec)
        wa, wb, wc = ea * inv, eb * inv, ec * inv
        dout = d_ref[...]
        gbar = _seg_sum(dout * (wa * o1[...] + wb * o2[...] + wc * o3[...]), e_ref)
        do1[...] = wa * dout
        do2[...] = wb * dout
        do3[...] = wc * dout
        dd1[...] = wa * gbar
        dd2[...] = wb * gbar
        dd3[...] = wc * gbar

    row = pl.BlockSpec((tm, 512), lambda i: (i, 0))
    return pl.pallas_call(
        body, name=name, grid=(T // tm,),
        in_specs=[row] * 6 + [row, pl.BlockSpec((512, 512), lambda i: (0, 0))],
        out_specs=[row] * 6,
        out_shape=[S((T, 512), f32)] * 6,
        compiler_params=_cp("arbitrary"),
    )(*ols, dmix, e)


def swa_pre_bwd(o, lse, dmix, sinks, e, *, tm, name):
    T = o.shape[0]
    ni = T // tm

    def body(o_ref, l_ref, d_ref, s_ref, e_ref, do_ref, dd_ref, ds_ref):
        i = pl.program_id(0)
        dout = d_ref[...]
        do_ref[...] = dout.astype(MXU)
        prod = dout * o_ref[...]
        dd = jnp.concatenate([_seg_sum(prod[:, 0:512], e_ref), _seg_sum(prod[:, 512:1024], e_ref)], axis=1)
        dd_ref[...] = dd
        part = -jnp.sum(jnp.exp(s_ref[...] - l_ref[...]) * dd, axis=0, keepdims=True)

        @pl.when(i == 0)
        def _():
            ds_ref[...] = part

        @pl.when(i > 0)
        def _():
            ds_ref[...] += part

    row = pl.BlockSpec((tm, 1024), lambda i: (i, 0))
    vec = pl.BlockSpec((1, 1024), lambda i: (0, 0))
    return pl.pallas_call(
        body, name=name, grid=(ni,),
        in_specs=[row, row, pl.BlockSpec((tm, 1024), lambda i: (i, 1)), vec, pl.BlockSpec((512, 512), lambda i: (0, 0))],
        out_specs=[row, row, vec],
        out_shape=[S((T, 1024), MXU), S((T, 1024), f32), S((1, 1024), f32)],
        compiler_params=_cp("arbitrary"),
    )(o, lse, dmix, sinks, e)


SHIFT_PAD = 24


def _shifted_copies(buf_ref, sh_ref, length):
    for r in range(1, 8):
        sh_ref[r - 1, 0:length, :] = buf_ref[pl.ds(r, length), :]


def _window(buf_ref, sh_ref, start, rows):
    q, r = divmod(start, 8)
    if r == 0:
        return buf_ref[pl.ds(8 * q, rows), :]
    return sh_ref[r - 1, pl.ds(8 * q, rows), :]


TAP_ROWS = 64


def _tap_sum(buf_ref, sh_ref, w_ref, starts, rows):
    outs = []
    for c0 in range(0, rows, TAP_ROWS):
        n = min(TAP_ROWS, rows - c0)
        acc = _window(buf_ref, sh_ref, starts[0] + c0, n) * w_ref[pl.ds(0, 1), :]
        for j in range(1, CONV_K):
            acc += _window(buf_ref, sh_ref, starts[j] + c0, n) * w_ref[pl.ds(j, 1), :]
        outs.append(acc)
    return jnp.concatenate(outs, axis=0)


def _conv_taps(buf_ref, sh_ref, w_ref, start, rows):
    return _tap_sum(buf_ref, sh_ref, w_ref, [start + j for j in range(CONV_K)], rows)


def conv_fwd(proj, mix, w, b, ln_g, ln_b, *, tb, name):
    T = proj.shape[0]
    hb = tb // HALO

    def body(u_ref, g_ref, up_ref, gp_ref, w_ref, b_ref, lg_ref, lb_ref, mix_ref, o_ref, hbuf, hsh):
        i = pl.program_id(0)
        hprev = up_ref[...] * _sigmoid(gp_ref[...])
        hbuf[0:HALO, :] = hprev * jnp.where(i > 0, 1.0, 0.0)
        hbuf[HALO:HALO + tb, :] = u_ref[...] * _sigmoid(g_ref[...])
        _shifted_copies(hbuf, hsh, tb + SHIFT_PAD)
        y = _conv_taps(hbuf, hsh, w_ref, HALO - (CONV_K - 1), tb) + b_ref[...]
        mu = jnp.mean(y, axis=-1, keepdims=True)
        yc = y - mu
        var = jnp.mean(yc * yc, axis=-1, keepdims=True)
        z = yc * lax.rsqrt(var + EPS) * lg_ref[...] + lb_ref[...]
        o_ref[...] = (z * _sigmoid(z)).astype(MXU)

    vec = pl.BlockSpec((1, 512), lambda i: (0, 0))
    return pl.pallas_call(
        body, name=name, grid=(T // tb,),
        in_specs=[pl.BlockSpec((tb, 512), lambda i: (i, 3)), pl.BlockSpec((tb, 512), lambda i: (i, 4)),
                  pl.BlockSpec((HALO, 512), lambda i: (jnp.maximum(i * hb - 1, 0), 3)),
                  pl.BlockSpec((HALO, 512), lambda i: (jnp.maximum(i * hb - 1, 0), 4)),
                  pl.BlockSpec((HALO, 512), lambda i: (0, 0)), vec, vec, vec, ANY],
        out_specs=pl.BlockSpec((tb, 512), lambda i: (i, 1)),
        out_shape=S(mix.shape, mix.dtype),
        scratch_shapes=[pltpu.VMEM((tb + HALO, 512), f32), pltpu.VMEM((7, tb + SHIFT_PAD, 512), f32)],
        compiler_params=_cp("arbitrary"), input_output_aliases={8: 0},
    )(proj, proj, proj, proj, w, b, ln_g, ln_b, mix)


def conv_bwd(proj, dmix, w, b, ln_g, ln_b, *, tb, name, comm=None):
    T = proj.shape[0]
    hb = tb // HALO
    ni = T // tb
    last_h = T // HALO - 1
    ext = tb + HALO

    def body(u_ref, g_ref, up_ref, gp_ref, un_ref, gn_ref, d_ref, dn_ref, w_ref, b_ref, lg_ref, lb_ref,
             du_ref, dg_ref, dw_ref, db_ref, dlg_ref, dlb_ref, hbuf, dybuf, hsh, dsh):
        i = pl.program_id(0)
        hbuf[0:HALO, :] = up_ref[...] * _sigmoid(gp_ref[...]) * jnp.where(i > 0, 1.0, 0.0)
        u = u_ref[...]
        sg = _sigmoid(g_ref[...])
        hbuf[HALO:HALO + tb, :] = u * sg
        hbuf[HALO + tb:HALO + ext, :] = un_ref[...] * _sigmoid(gn_ref[...])
        _shifted_copies(hbuf, hsh, ext + SHIFT_PAD)
        y = _conv_taps(hbuf, hsh, w_ref, HALO - (CONV_K - 1), ext) + b_ref[...]
        mu = jnp.mean(y, axis=-1, keepdims=True)
        yc = y - mu
        rstd = lax.rsqrt(jnp.mean(yc * yc, axis=-1, keepdims=True) + EPS)
        yn = yc * rstd
        z = yn * lg_ref[...] + lb_ref[...]
        sz = _sigmoid(z)
        row = lax.broadcasted_iota(jnp.int32, (ext, 1), 0)
        own = row < tb
        keep = row < jnp.where(i < ni - 1, ext, tb)
        dout = jnp.concatenate([d_ref[...], dn_ref[...]], axis=0)
        dz = jnp.where(keep, dout * (sz * (1.0 + z * (1.0 - sz))), 0.0)
        dyn = dz * lg_ref[...]
        dy = rstd * (dyn - jnp.mean(dyn, axis=-1, keepdims=True) - yn * jnp.mean(dyn * yn, axis=-1, keepdims=True))
        dybuf[...] = dy
        _shifted_copies(dybuf, dsh, tb + SHIFT_PAD)
        dz_own = jnp.where(own, dz, 0.0)
        dlg = jnp.sum(dz_own * yn, axis=0, keepdims=True)
        dlb = jnp.sum(dz_own, axis=0, keepdims=True)
        dy_own = dybuf[0:tb, :]
        dbias = jnp.sum(dy_own, axis=0, keepdims=True)
        dh = _tap_sum(dybuf, dsh, w_ref, [CONV_K - 1 - j for j in range(CONV_K)], tb)
        du_ref[...] = (dh * sg).astype(MXU)
        dg_ref[...] = (dh * u * sg * (1.0 - sg)).astype(MXU)
        taps = [jnp.sum(dy_own * _window(hbuf, hsh, HALO - (CONV_K - 1) + j, tb), axis=0, keepdims=True)
                for j in range(CONV_K)]
        taps.append(jnp.zeros((1, 512), f32))
        dwt = jnp.concatenate(taps, axis=0)

        @pl.when(i == 0)
        def _():
            dw_ref[...] = dwt
            db_ref[...] = dbias
            dlg_ref[...] = dlg
            dlb_ref[...] = dlb

        @pl.when(i > 0)
        def _():
            dw_ref[...] += dwt
            db_ref[...] += dbias
            dlg_ref[...] += dlg
            dlb_ref[...] += dlb

    vec = pl.BlockSpec((1, 512), lambda i: (0, 0))
    wspec = pl.BlockSpec((HALO, 512), lambda i: (0, 0))

    def halo_prev(col):
        return pl.BlockSpec((HALO, 512), lambda i: (jnp.maximum(i * hb - 1, 0), col))

    def halo_next(col):
        return pl.BlockSpec((HALO, 512), lambda i: (jnp.minimum((i + 1) * hb, last_h), col))

    row = pl.BlockSpec((tb, 512), lambda i: (i, 0))
    return _pallas(
        body, comm=comm, name=name, grid=(ni,),
        in_specs=[pl.BlockSpec((tb, 512), lambda i: (i, 3)), pl.BlockSpec((tb, 512), lambda i: (i, 4)),
                  halo_prev(3), halo_prev(4), halo_next(3), halo_next(4),
                  pl.BlockSpec((tb, 512), lambda i: (i, 1)), halo_next(1), wspec, vec, vec, vec],
        out_specs=[row, row, wspec, vec, vec, vec],
        out_shape=[S((T, 512), MXU), S((T, 512), MXU), S((HALO, 512), f32)] + [S((1, 512), f32)] * 3,
        scratch_shapes=[pltpu.VMEM((tb + 2 * HALO, 512), f32), pltpu.VMEM((ext, 512), f32),
                        pltpu.VMEM((7, ext + SHIFT_PAD, 512), f32), pltpu.VMEM((7, tb + SHIFT_PAD, 512), f32)],
        compiler_params=_cp("arbitrary"),
    )(proj, proj, proj, proj, proj, proj, dmix, dmix, w, b, ln_g, ln_b)


def _qk_norm_bwd(v, gain, dout, e_ref):
    r = lax.rsqrt(_seg_sum(v * v, e_ref) * (1.0 / HD) + EPS)
    y = v * r
    dgain = jnp.sum(dout * y, axis=0, keepdims=True)
    dy = dout * gain
    dv = r * (dy - y * (_seg_sum(dy * y, e_ref) * (1.0 / HD)))
    return dv, dgain


def prep_bwd(proj, e, gains, da, dc, dconv, *, tm, name):
    T = proj.shape[0]

    def body(*refs):
        p_ref, e_ref, gaq, gak, gcq, gck = refs[0:6]
        a_refs = refs[6:15]
        dcq, dckk, dcvv, du, dgt = refs[15:20]
        dp, gaq_o, gak_o, gcq_o, gck_o = refs[20:]
        i = pl.program_id(0)
        dq = a_refs[0][...] + a_refs[3][...] + a_refs[6][...]
        dk = a_refs[1][...] + a_refs[4][...] + a_refs[7][...]
        dv = a_refs[2][...] + a_refs[5][...] + a_refs[8][...]
        d, g_aq = _qk_norm_bwd(p_ref[:, 0:512], gaq[...], dq, e_ref)
        dp[:, 0:512] = d.astype(MXU)
        d, g_ak = _qk_norm_bwd(p_ref[:, 512:1024], gak[...], dk, e_ref)
        dp[:, 512:1024] = d.astype(MXU)
        dp[:, 1024:1536] = dv.astype(MXU)
        dp[:, 1536:2048] = du[...]
        dp[:, 2048:2560] = dgt[...]
        d, g_cq0 = _qk_norm_bwd(p_ref[:, 2560:3072], gcq[...], dcq[:, 0:512], e_ref)
        dp[:, 2560:3072] = d.astype(MXU)
        d, g_cq1 = _qk_norm_bwd(p_ref[:, 3072:3584], gcq[...], dcq[:, 512:1024], e_ref)
        dp[:, 3072:3584] = d.astype(MXU)
        lo = lax.broadcasted_iota(jnp.int32, (tm, 128), 1) < HD

        def fold(ref):
            g0, g1 = ref[:, 0:128], ref[:, 128:256]
            s0 = g0 + pltpu.roll(g0, HD, 1)
            s1 = g1 + pltpu.roll(g1, HD, 1)
            return jnp.where(lo, s0, s1)

        dkn = fold(dckk)
        kraw = p_ref[:, 3584:3712]
        r = lax.rsqrt(_seg_sum128(kraw * kraw, e_ref) * (1.0 / HD) + EPS)
        y = kraw * r
        g_ck = jnp.sum(dkn * y, axis=0, keepdims=True)
        dy = dkn * gck[...]
        dp[:, 3584:3712] = (r * (dy - y * (_seg_sum128(dy * y, e_ref) * (1.0 / HD)))).astype(MXU)
        dp[:, 3712:3840] = fold(dcvv).astype(MXU)
        g_cq = jnp.concatenate([g_cq0, g_cq1], axis=1)

        @pl.when(i == 0)
        def _():
            gaq_o[...] = g_aq
            gak_o[...] = g_ak
            gcq_o[...] = g_cq
            gck_o[...] = g_ck

        @pl.when(i > 0)
        def _():
            gaq_o[...] += g_aq
            gak_o[...] += g_ak
            gcq_o[...] += g_cq
            gck_o[...] += g_ck

    def vec(n):
        return pl.BlockSpec((1, n), lambda i: (0, 0))

    def rows(n):
        return pl.BlockSpec((tm, n), lambda i: (i, 0))

    return pl.pallas_call(
        body, name=name, grid=(T // tm,),
        in_specs=[rows(IN_W), pl.BlockSpec((512, 512), lambda i: (0, 0)), vec(512), vec(512), vec(512), vec(128)]
        + [rows(512)] * 9 + [rows(1024), rows(256), rows(256), rows(512), rows(512)],
        out_specs=[rows(IN_W), vec(512), vec(512), vec(1024), vec(128)],
        out_shape=[S((T, IN_W), MXU), S((1, 512), f32), S((1, 512), f32), S((1, 1024), f32), S((1, 128), f32)],
        compiler_params=_cp("arbitrary"),
    )(proj, e, *gains, *da, *dc, *dconv)


def adamw(w, m, v, pieces, *, tr, name, comm=None):
    n, R, C = w.shape
    c1 = 1.0 - ADAM_B1 ** ADAM_STEP
    c2 = 1.0 - ADAM_B2 ** ADAM_STEP
    npc = len(pieces)

    def body(*refs):
        w_ref, m_ref, v_ref = refs[0:3]
        p_refs = refs[3:3 + npc]
        g_ref, d_ref, mo_ref, vo_ref = refs[3 + npc:]
        g = p_refs[0][...].astype(f32)
        for p in p_refs[1:]:
            g = g + p[...].astype(f32)
        mn = ADAM_B1 * m_ref[...] + (1.0 - ADAM_B1) * g
        vn = ADAM_B2 * v_ref[...] + (1.0 - ADAM_B2) * (g * g)
        g_ref[...] = g
        mo_ref[...] = mn
        vo_ref[...] = vn
        d_ref[...] = -ADAM_LR * ((mn / c1) / (jnp.sqrt(vn / c2) + ADAM_EPS) + ADAM_WD * w_ref[...])

    blk = pl.BlockSpec((None, tr, C), lambda l, i: (l, i, 0))
    return _pallas(
        body, comm=comm, name=name, grid=(n, R // tr), in_specs=[blk] * (3 + npc), out_specs=[blk] * 4,
        out_shape=[S(w.shape, f32)] * 4, compiler_params=_cp("arbitrary", "arbitrary"),
    )(w, m, v, *pieces)


def add_halves(pieces, other, *, tr, name):
    _, _, r, cc = pieces.shape

    def body(c_ref, a_ref, b_ref, o_ref):
        o_ref[...] = (a_ref[...] + b_ref[...]).astype(jnp.bfloat16)

    blk = pl.BlockSpec((None, tr, cc), lambda s, i, c_ref: (s, i, 0))
    grid_spec = pltpu.PrefetchScalarGridSpec(
        num_scalar_prefetch=1, grid=(4, r // tr),
        in_specs=[pl.BlockSpec((None, None, tr, cc), lambda s, i, c_ref: (s, c_ref[0], i, 0)), blk], out_specs=blk)
    core = lax.axis_index("c").astype(jnp.int32).reshape(1)
    return pl.pallas_call(body, name=name, grid_spec=grid_spec, out_shape=S((4, r, cc), jnp.bfloat16),
                          compiler_params=_cp("arbitrary", "arbitrary"))(core, pieces, other)


def sum8(parts, *, name):
    _, R, C = parts.shape

    def body(p_ref, o_ref):
        acc = p_ref[0]
        for d in range(1, 8):
            acc = acc + p_ref[d]
        o_ref[...] = acc

    return pl.pallas_call(body, name=name, out_shape=S((R, C), f32))(parts)


def _pos():
    return lax.axis_index("x"), lax.axis_index("y"), lax.axis_index("c")


def _other_chips(x, y):
    return [(1 - x, y), (x, 1 - y), (1 - x, 1 - y)]


class GatherComm:
    def __init__(self, shards, in_place=None):
        self.ins = list(shards)
        self.nt = nt = len(shards)
        self.in_place = list(in_place) if in_place is not None else [False] * nt
        self.out_shapes = [S((2 * s.shape[1], 4 * s.shape[2]), s.dtype) if ip else S((4,) + s.shape, s.dtype)
                           for s, ip in zip(shards, self.in_place)]
        self.sem_shapes = [pltpu.SemaphoreType.DMA((nt, 6)), pltpu.SemaphoreType.DMA((nt, 6)),
                           pltpu.SemaphoreType.DMA((nt, 2))]
        self.results = None

    def _place(self, couts, t, chip, half):
        cid = 2 * chip[0] + chip[1]
        if not self.in_place[t]:
            return couts[t].at[cid, half]
        _, r, c = self.ins[t].shape
        row0 = half * r if isinstance(half, int) else pl.multiple_of(half * r, 16)
        return couts[t].at[pl.ds(row0, r), pl.ds(pl.multiple_of(cid * c, 128), c)]

    def _copy(self, couts, sems, t, k, chip, half, to, src=None):
        dst = self._place(couts, t, chip, half)
        return pltpu.make_async_remote_copy(
            src_ref=dst if src is None else src, dst_ref=dst,
            send_sem=sems[0].at[t, k], recv_sem=sems[1].at[t, k], device_id=to, device_id_type=MESH)

    def _local(self, cins, couts, sems, t):
        x, y, _ = _pos()
        return [pltpu.make_async_copy(cins[t].at[half], self._place(couts, t, (x, y), half), sems[2].at[t, half])
                for half in range(2)]

    def start(self, cins, couts, sems):
        x, y, c = _pos()
        for t in range(self.nt):
            for cp in self._local(cins, couts, sems, t):
                cp.start()
            for j, chip in enumerate(_other_chips(x, y)):
                self._copy(couts, sems, t, j, (x, y), c, (*chip, c), src=cins[t].at[c]).start()

    def mid(self, cins, couts, sems):
        x, y, c = _pos()
        for t in range(self.nt):
            for j, chip in enumerate(_other_chips(x, y)):
                self._copy(couts, sems, t, j, chip, c, (x, y, c)).wait_recv()
                self._copy(couts, sems, t, 3 + j, chip, c, (x, y, 1 - c)).start()

    def finish(self, cins, couts, sems):
        x, y, c = _pos()
        for t in range(self.nt):
            for j, chip in enumerate(_other_chips(x, y)):
                self._copy(couts, sems, t, 3 + j, chip, 1 - c, (x, y, c)).wait_recv()
        for t in range(self.nt):
            for j, chip in enumerate(_other_chips(x, y)):
                self._copy(couts, sems, t, j, (x, y), c, (*chip, c), src=cins[t].at[c]).wait_send()
                self._copy(couts, sems, t, 3 + j, chip, c, (x, y, 1 - c)).wait_send()
            for cp in self._local(cins, couts, sems, t):
                cp.wait()


class SwapComm:
    def __init__(self, pieces):
        self.ins = list(pieces)
        self.nt = nt = len(pieces)
        self.out_shapes = [S((4,) + p.shape[2:], p.dtype) for p in pieces]
        self.sem_shapes = [pltpu.SemaphoreType.DMA((nt, 4)), pltpu.SemaphoreType.DMA((nt, 4))]
        self.results = None

    def _copies(self, cins, couts, sems):
        x, y, c = _pos()
        return [pltpu.make_async_remote_copy(src_ref=cins[t].at[s, 1 - c], dst_ref=couts[t].at[s],
                                             send_sem=sems[0].at[t, s], recv_sem=sems[1].at[t, s],
                                             device_id=(x, y, 1 - c), device_id_type=MESH)
                for t in range(self.nt) for s in range(4)]

    def start(self, cins, couts, sems):
        for cp in self._copies(cins, couts, sems):
            cp.start()

    def mid(self, cins, couts, sems):
        pass

    def finish(self, cins, couts, sems):
        for cp in self._copies(cins, couts, sems):
            cp.wait()


class ExchangeComm:
    def __init__(self, arrs):
        self.ins = list(arrs)
        self.nt = nt = len(arrs)
        self.out_shapes = [S((2,) + a.shape, a.dtype) for a in arrs]
        self.sem_shapes = [pltpu.SemaphoreType.DMA((nt, 7)), pltpu.SemaphoreType.DMA((nt, 7)),
                           pltpu.SemaphoreType.DMA((nt,))]
        self.results = None

    def _copy(self, couts, sems, t, k, half, src_chip, to, src=None):
        dst = couts[t].at[half, src_chip]
        return pltpu.make_async_remote_copy(
            src_ref=dst if src is None else src, dst_ref=dst,
            send_sem=sems[0].at[t, k], recv_sem=sems[1].at[t, k], device_id=to, device_id_type=MESH)

    def _local(self, cins, couts, sems, t):
        x, y, c = _pos()
        return pltpu.make_async_copy(cins[t].at[2 * x + y], couts[t].at[c, 2 * x + y], sems[2].at[t])

    def _firsts(self, cins, couts, sems, t):
        x, y, c = _pos()
        me = 2 * x + y
        cps = [self._copy(couts, sems, t, j, c, me, (*chip, c), src=cins[t].at[2 * chip[0] + chip[1]])
               for j, chip in enumerate(_other_chips(x, y))]
        return cps + [self._copy(couts, sems, t, 6, c, me, (x, y, 1 - c), src=cins[t].at[me])]

    def start(self, cins, couts, sems):
        for t in range(self.nt):
            self._local(cins, couts, sems, t).start()
            for cp in self._firsts(cins, couts, sems, t):
                cp.start()

    def mid(self, cins, couts, sems):
        x, y, c = _pos()
        for t in range(self.nt):
            for j, chip in enumerate(_other_chips(x, y)):
                cid = 2 * chip[0] + chip[1]
                self._copy(couts, sems, t, j, c, cid, (x, y, c)).wait_recv()
                self._copy(couts, sems, t, 3 + j, c, cid, (x, y, 1 - c)).start()

    def finish(self, cins, couts, sems):
        x, y, c = _pos()
        for t in range(self.nt):
            for j, chip in enumerate(_other_chips(x, y)):
                self._copy(couts, sems, t, 3 + j, 1 - c, 2 * chip[0] + chip[1], (x, y, c)).wait_recv()
            self._copy(couts, sems, t, 6, 1 - c, 2 * x + y, (x, y, c)).wait_recv()
        for t in range(self.nt):
            for cp in self._firsts(cins, couts, sems, t):
                cp.wait_send()
            for j, chip in enumerate(_other_chips(x, y)):
                self._copy(couts, sems, t, 3 + j, c, 2 * chip[0] + chip[1], (x, y, 1 - c)).wait_send()
            self._local(cins, couts, sems, t).wait()


def gather_small(vec, *, name):
    R, C = vec.shape

    def body(v_ref, out_ref, send_sems, recv_sems):
        x, y, c = _pos()
        me = 4 * x + 2 * y + c
        out_ref[me] = v_ref[...]
        cps = []
        def peer(k):
            fx, fy, fc = (k >> 2) & 1, (k >> 1) & 1, k & 1
            return (1 - x if fx else x), (1 - y if fy else y), (1 - c if fc else c)

        for k in range(1, 8):
            cp = pltpu.make_async_remote_copy(src_ref=v_ref, dst_ref=out_ref.at[me], send_sem=send_sems.at[k - 1],
                                              recv_sem=recv_sems.at[k - 1], device_id=peer(k), device_id_type=MESH)
            cp.start()
            cps.append(cp)
        for k in range(1, 8):
            px, py, pc = peer(k)
            pltpu.make_async_remote_copy(src_ref=v_ref, dst_ref=out_ref.at[4 * px + 2 * py + pc],
                                         send_sem=send_sems.at[k - 1], recv_sem=recv_sems.at[k - 1],
                                         device_id=(px, py, pc), device_id_type=MESH).wait_recv()
        for cp in cps:
            cp.wait_send()

    return pl.pallas_call(
        body, name=name,
        in_specs=[pl.BlockSpec(memory_space=pltpu.VMEM)], out_specs=pl.BlockSpec(memory_space=pltpu.VMEM),
        out_shape=S((8, R, C), vec.dtype),
        scratch_shapes=[pltpu.SemaphoreType.DMA((7,)), pltpu.SemaphoreType.DMA((7,))],
    )(vec)


def _tile(n, prefs):
    for p in prefs:
        if n % p == 0:
            return p
    return n


def _lanes(g, reps):
    return jnp.tile(g.reshape(1, -1), (1, reps))


class _NoRide:
    def rider(self, name):
        return None

    def landed(self, comm):
        pass

    def grad(self, name, val):
        pass


def _layer_fwd(x, p, e, ride=_NoRide()):
    T, D = x.shape
    tm = _tile(T, (512, 256, 128))
    tall = _tile(T, (2048, 1024, 512, 256, 128))

    def carried(fn, *args, name, **kw):
        comm = ride.rider(name)
        out = fn(*args, name=name, comm=comm, **kw)
        ride.landed(comm)
        return out

    h, proj = carried(rms_proj, x, p["norm1_g"], p["w_in"], tm=tall, tn=_tile(IN_W, (768,)), name="rms_proj")
    gains = (_lanes(p["a_q_g"], 8), _lanes(p["a_k_g"], 8), _lanes(p["c_q_g"], 8), _lanes(p["c_k_g"], 2))
    aq, ak, av, cq, ckk, cvv = prep_fwd(proj, e, gains, tm=_tile(T, (256, 128)), name="prep_fwd")
    ols = []
    for d in DILATIONS:
        ols += carried(band_attn_fwd, aq, ak, av, None, dil=d, max_dist=A_DIST, group=1, name=f"dil_attn_fwd_{d}")
    mix = dil_combine_fwd(ols, tm=tm, name="dil_combine_fwd")
    mix = conv_fwd(proj, mix, p["conv_w"], p["conv_b"], p["conv_ln_g"], p["conv_ln_b"], tb=tm, name="conv_fwd")
    sinks = jnp.repeat(p["c_sinks"].reshape(-1), HD).reshape(1, C_W)
    o_c, l_c, mix = carried(band_attn_fwd, cq, ckk, cvv, sinks, dil=1, max_dist=C_DIST, group=4, mix=mix,
                            name="swa_attn_fwd")
    x1 = carried(matmul_res, mix, p["w_out"], x, tm=tall, tn=_tile(D, (512, 256)), name="out_proj")
    F = p["w_gate"].shape[1]
    h2, gate, up, act = carried(rms_swiglu, x1, p["norm2_g"], p["w_gate"], p["w_up"], tm=tall,
                                tn=_tile(F, (256, 128)), name="rms_swiglu")
    x2 = carried(matmul_res, act, p["w_down"], x1, tm=tall, tn=_tile(D, (256,)), name="ffn_down")
    saved = dict(x=x, h=h, proj=proj, gains=gains, aq=aq, ak=ak, av=av, cq=cq, ckk=ckk, cvv=cvv, ols=ols, o_c=o_c,
                 l_c=l_c, sinks=sinks, mix=mix, x1=x1, h2=h2, gate=gate, up=up, act=act)
    return x2, saved


def _layer_bwd(dx2, dx2b, p, s, e, ride=_NoRide()):
    T, D = dx2.shape
    F = p["w_gate"].shape[1]
    tm = _tile(T, (512, 256, 128))
    tmm = _tile(T, (1024, 512, 256, 128))
    tkT = _tile(T, (2048, 1024, 512))
    tF = _tile(F, (512, 256, 128))
    tD = _tile(D, (1024, 512, 256))
    g = {}

    def carried(fn, *args, name, **kw):
        comm = ride.rider(name)
        out = fn(*args, name=name, comm=comm, **kw)
        ride.landed(comm)
        return out

    def big(n, val):
        g[n] = val
        ride.grad(n, val)

    d_gate, d_up = carried(nt_swiglu_bwd, dx2b, p["w_down"], s["gate"], s["up"], tm=_tile(T, (2048, 1024, 512)),
                           tn=_tile(F, (256, 128)), name="ffn_down_bwd")
    big("w_down", tn_matmul(s["act"], dx2b, tm=tF, tn=tD, tk=tkT, name="grad_w_down"))
    big("w_gate", carried(tn_matmul, s["h2"], d_gate, tm=tD, tn=tF, tk=tkT, by_chip=True, name="grad_w_gate"))
    big("w_up", carried(tn_matmul, s["h2"], d_up, tm=tD, tn=tF, tk=tkT, by_chip=True, name="grad_w_up"))
    dx1, dx1b, g["norm2_g"] = carried(nt_rms_bwd, [(d_gate, p["w_gate"]), (d_up, p["w_up"])], s["x1"], p["norm2_g"],
                                      dx2, tm=tm, tn=_tile(D, (512, 256)), name="ffn_in_bwd")
    dmix = nt_plain(dx1b, p["w_out"], tm=tmm, tn=tD, name="out_proj_bwd")
    big("w_out", tn_matmul(s["mix"], dx1b, tm=1024, tn=tD, tk=tkT, name="grad_w_out"))
    dos = dil_combine_bwd(s["ols"], dmix, e, tm=tm, name="dil_combine_bwd")
    da = []
    for n, d in enumerate(DILATIONS):
        da += carried(band_attn_bwd, s["aq"], s["ak"], s["av"], s["ols"][2 * n + 1], dos[n], dos[3 + n], dil=d,
                      max_dist=A_DIST, group=1, name=f"dil_attn_bwd_{d}")
    du, dgt, gw, gb, glg, glb = carried(conv_bwd, s["proj"], dmix, p["conv_w"], p["conv_b"], p["conv_ln_g"],
                                        p["conv_ln_b"], tb=tm, name="conv_bwd")
    g["conv_w"], g["conv_b"], g["conv_ln_g"], g["conv_ln_b"] = gw[:CONV_K], gb, glg, glb
    do_c, dd_c, dsink = swa_pre_bwd(s["o_c"], s["l_c"], dmix, s["sinks"], e, tm=_tile(T, (256, 128)), name="swa_pre_bwd")
    g["c_sinks"] = dsink.reshape(-1, HD)[:, 0]
    dcq, dckk, dcvv = carried(band_attn_bwd, s["cq"], s["ckk"], s["cvv"], s["l_c"], do_c, dd_c, dil=1, max_dist=C_DIST,
                              group=4, name="swa_attn_bwd")
    dproj, gaq, gak, gcq, gck = prep_bwd(s["proj"], e, s["gains"], da, (dcq, dckk, dcvv), (du, dgt),
                                         tm=_tile(T, (256, 128)), name="prep_bwd")
    g["a_q_g"] = gaq.reshape(-1, HD).sum(0)
    g["a_k_g"] = gak.reshape(-1, HD).sum(0)
    g["c_q_g"] = gcq.reshape(-1, HD).sum(0)
    g["c_k_g"] = gck.reshape(-1, HD).sum(0)
    big("w_in", tn_matmul(s["h"], dproj, tm=tD, tn=_tile(IN_W, (1280,)), tk=tkT, name="grad_w_in"))
    dx, dxb, g["norm1_g"] = carried(nt_rms_bwd, [(dproj, p["w_in"])], s["x"], p["norm1_g"], dx1, tm=tmm,
                                    tn=_tile(D, (512, 256)), name="in_proj_bwd")
    return dx, dxb, g


BIG = ("w_in", "w_out", "w_gate", "w_up", "w_down")
COL_SHARDED = ("w_in", "w_gate", "w_up")
SMALL = ("norm1_g", "a_q_g", "a_k_g", "conv_w", "conv_b", "conv_ln_g", "conv_ln_b", "c_q_g", "c_k_g", "c_sinks", "norm2_g")


def _to_pieces(g, name):
    if g.ndim == 3:
        return g.reshape(4, 2, g.shape[1] // 2, g.shape[2])
    R, C = g.shape
    if name in COL_SHARDED:
        return g.reshape(2, R // 2, 4, C // 4).transpose(2, 0, 1, 3)
    return g.reshape(4, 2, R // 8, C)


def _from_gathered(w, name):
    _, _, r, c = w.shape
    if name in COL_SHARDED:
        return w.transpose(1, 2, 0, 3).reshape(2 * r, 4 * c)
    return w.reshape(8 * r, c)


def _shard(W, l, n):
    w = W[n][l]
    return w.astype(MXU).reshape(2, w.shape[0] // 2, w.shape[1])


class _LayerParams(dict):
    def __init__(self, layer, full, small):
        super().__init__(small)
        self.layer, self.full = layer, full

    def __missing__(self, n):
        return self.full[(self.layer, n)]


FWD_RIDES = {
    (0, "rms_proj"): ((0, "w_out"), (0, "w_gate")),
    (0, "swa_attn_fwd"): ((0, "w_up"),),
    (0, "rms_swiglu"): ((0, "w_down"), (1, "w_in"), (1, "w_out")),
    (0, "ffn_down"): ((1, "w_gate"),),
    (1, "rms_proj"): ((1, "w_up"),),
    (1, "rms_swiglu"): ((1, "w_down"),),
}
BWD_RIDES = {
    "grad_w_gate": (("w_down",), ()),
    "grad_w_up": (("w_gate",), ("w_down",)),
    "ffn_in_bwd": (("w_up",), ("w_gate",)),
    "dil_attn_bwd_1": (("w_out",), ()),
    "dil_attn_bwd_16": ((), ("w_up",)),
    "conv_bwd": ((), ("w_out",)),
    "in_proj_bwd": (("w_in",), ()),
}
IN_PLACE = ("w_gate", "w_up")


class _FwdRide:
    def __init__(self, layer, W, full):
        self.layer, self.W, self.full = layer, W, full

    def rider(self, name):
        keys = FWD_RIDES.get((self.layer, name))
        if not keys:
            return None
        comm = GatherComm([_shard(self.W, l, n) for l, n in keys], [n in IN_PLACE for _, n in keys])
        comm.keys = keys
        return comm

    def landed(self, comm):
        if comm is not None:
            for (l, n), g in zip(comm.keys, comm.results):
                self.full[(l, n)] = g if n in IN_PLACE else _from_gathered(g, n)


class _GradFlow:
    def __init__(self):
        self.pieces, self.sums, self.landed, self.pending = {}, {}, {}, []

    def swap(self, keys):
        comm = SwapComm([self.pieces[k] for k in keys])
        comm.keys, comm.kind = list(keys), "swap"
        return comm

    def exchange(self, keys):
        if not keys:
            return None
        comm = ExchangeComm([self.sums[k] for k in keys])
        comm.keys, comm.kind = list(keys), "exchange"
        return comm

    def take_pending(self):
        keys, self.pending = self.pending, []
        return keys

    def land(self, comm):
        if comm is None:
            return
        if isinstance(comm, MultiComm):
            for sub in comm.comms:
                self.land(sub)
            return
        for k, res in zip(comm.keys, comm.results):
            if comm.kind == "swap":
                r = res.shape[1]
                self.sums[k] = add_halves(self.pieces[k], res, tr=_tile(r, (256, 176, 128, 64, 32, 16)),
                                          name="grad_chip_sum")
                if k[1] == "w_in":
                    self.pending.append(k)
            else:
                self.landed[k] = res


class _BwdRide:
    def __init__(self, layer, flow):
        self.layer, self.flow = layer, flow
        self.final = layer == 0

    def grad(self, name, val):
        self.flow.pieces[(self.layer, name)] = _to_pieces(val, name)
        if self.final and name == "w_in":
            swap = self.flow.swap([(self.layer, name)])
            _run_comm(swap, name="swap_last")
            self.flow.land(swap)

    def rider(self, name):
        if name == "ffn_down_bwd" or (self.final and name == "in_proj_bwd"):
            return self.flow.exchange(self.flow.take_pending())
        swaps, exchanges = BWD_RIDES.get(name, ((), ()))
        comms = []
        if swaps:
            comms.append(self.flow.swap([(self.layer, n) for n in swaps]))
        if exchanges:
            comms.append(self.flow.exchange([(self.layer, n) for n in exchanges]))
        return MultiComm(comms) if comms else None

    def landed(self, comm):
        self.flow.land(comm)


def _pack(items, rows):
    flat = jnp.concatenate([a.reshape(-1).astype(f32) for a in items])
    return jnp.pad(flat, (0, rows * 128 - flat.shape[0])).reshape(rows, 128)


def _unpack(packed, shapes):
    flat = packed.reshape(-1)
    out, off = [], 0
    for shp in shapes:
        n = 1
        for d in shp:
            n *= d
        out.append(flat[off:off + n].reshape(shp))
        off += n
    return out


def kernel(x, norm1_g, w_in, a_q_g, a_k_g, conv_w, conv_b, conv_ln_g, conv_ln_b, c_q_g, c_k_g, c_sinks, w_out, norm2_g, w_gate, w_up, w_down, loss_target, m_norm1_g, m_w_in, m_a_q_g, m_a_k_g, m_conv_w, m_conv_b, m_conv_ln_g, m_conv_ln_b, m_c_q_g, m_c_k_g, m_c_sinks, m_w_out, m_norm2_g, m_w_gate, m_w_up, m_w_down, v_norm1_g, v_w_in, v_a_q_g, v_a_k_g, v_conv_w, v_conv_b, v_conv_ln_g, v_conv_ln_b, v_c_q_g, v_c_k_g, v_c_sinks, v_w_out, v_norm2_g, v_w_gate, v_w_up, v_w_down):
    W = dict(norm1_g=norm1_g, w_in=w_in, a_q_g=a_q_g, a_k_g=a_k_g, conv_w=conv_w, conv_b=conv_b, conv_ln_g=conv_ln_g,
             conv_ln_b=conv_ln_b, c_q_g=c_q_g, c_k_g=c_k_g, c_sinks=c_sinks, w_out=w_out, norm2_g=norm2_g, w_gate=w_gate,
             w_up=w_up, w_down=w_down)
    M = dict(norm1_g=m_norm1_g, w_in=m_w_in, a_q_g=m_a_q_g, a_k_g=m_a_k_g, conv_w=m_conv_w, conv_b=m_conv_b,
             conv_ln_g=m_conv_ln_g, conv_ln_b=m_conv_ln_b, c_q_g=m_c_q_g, c_k_g=m_c_k_g, c_sinks=m_c_sinks, w_out=m_w_out,
             norm2_g=m_norm2_g, w_gate=m_w_gate, w_up=m_w_up, w_down=m_w_down)
    V = dict(norm1_g=v_norm1_g, w_in=v_w_in, a_q_g=v_a_q_g, a_k_g=v_a_k_g, conv_w=v_conv_w, conv_b=v_conv_b,
             conv_ln_g=v_conv_ln_g, conv_ln_b=v_conv_ln_b, c_q_g=v_c_q_g, c_k_g=v_c_k_g, c_sinks=v_c_sinks, w_out=v_w_out,
             norm2_g=v_norm2_g, w_gate=v_w_gate, w_up=v_w_up, w_down=v_w_down)
    depth = norm1_g.shape[0]
    T, D = x.shape[1], x.shape[2]
    xs = x.reshape(T, D)
    chip = 2 * lax.axis_index("x") + lax.axis_index("y")
    e = _head_eye()

    full = {}
    first = GatherComm([_shard(W, 0, "w_in"), conv_w])
    _run_comm(first, name="gather_first")
    full[(0, "w_in")] = _from_gathered(first.results[0], "w_in")
    conv_full = first.results[1].transpose(1, 2, 0, 3).reshape(depth, CONV_K, B_W)
    params = []
    for l in range(depth):
        small = {n: W[n][l].reshape(1, -1) for n in SMALL if n != "conv_w"}
        small["conv_w"] = jnp.pad(conv_full[l], ((0, HALO - CONV_K), (0, 0)))
        params.append(_LayerParams(l, full, small))

    saved = []
    act = xs
    for l in range(depth):
        act, s = _layer_fwd(act, params[l], e, _FwdRide(l, W, full))
        saved.append(s)
    dy, dyb, loss_part = loss_head(act, loss_target.reshape(T, D), tm=_tile(T, (512, 256, 128)), name="loss_head")
    grads = [None] * depth
    flow = _GradFlow()
    for l in reversed(range(depth)):
        dy, dyb, grads[l] = _layer_bwd(dy, dyb, params[l], saved[l], e, _BwdRide(l, flow))
    grad_x = dy.reshape(x.shape)

    out = {}
    for n in ("w_down", "w_gate", "w_up", "w_out", "w_in"):
        last = flow.exchange(flow.take_pending())
        per_layer = [flow.landed[(l, n)] for l in range(depth)]
        r, cc = per_layer[0].shape[2], per_layer[0].shape[3]
        srcs = [jnp.stack([pl_[:, s].reshape(2 * r, cc) for pl_ in per_layer]) for s in range(4)]
        out[n] = adamw(W[n], M[n], V[n], srcs, tr=_tile(2 * r, (256, 176, 128, 64, 32, 16)), name="adamw_" + n,
                       comm=last)
        flow.land(last)

    small_shapes = []
    items = []
    for l in range(depth):
        for n in SMALL:
            a = grads[l][n]
            if n == "conv_w":
                a = a.reshape(CONV_K, 4, B_W // 4).transpose(1, 0, 2)
            items.append(a)
            small_shapes.append(a.shape)
    items.append(loss_part[0, 0:1])
    small_shapes.append((1,))
    total = sum(int(jnp.size(a)) for a in items)
    rows = -(-total // 1024) * 8
    summed = sum8(gather_small(_pack(items, rows), name="gather_small"), name="sum_small")
    parts = _unpack(summed, small_shapes)
    loss = parts[-1][0]
    small_g = {n: [] for n in SMALL}
    for l in range(depth):
        for i, n in enumerate(SMALL):
            a = parts[l * len(SMALL) + i]
            if n == "conv_w":
                a = lax.dynamic_index_in_dim(a, chip, axis=0, keepdims=False)
            small_g[n].append(a.reshape(W[n].shape[1:]))
    sw = [W[n] for n in SMALL]
    sm = [M[n] for n in SMALL]
    sv = [V[n] for n in SMALL]
    sg = [jnp.stack(small_g[n]) for n in SMALL]
    tot2 = sum(int(jnp.size(a)) for a in sw)
    rows2 = -(-tot2 // 1024) * 8
    res = adamw(_pack(sw, rows2)[None], _pack(sm, rows2)[None], _pack(sv, rows2)[None], [_pack(sg, rows2)[None]],
                tr=rows2, name="adamw_small")
    shapes2 = [a.shape for a in sw]
    small_out = [_unpack(r[0], shapes2) for r in res]
    for i, n in enumerate(SMALL):
        out[n] = [small_out[k][i] for k in range(4)]

    order = ("norm1_g", "w_in", "a_q_g", "a_k_g", "conv_w", "conv_b", "conv_ln_g", "conv_ln_b", "c_q_g", "c_k_g",
             "c_sinks", "w_out", "norm2_g", "w_gate", "w_up", "w_down")
    return (loss, grad_x, *[out[n][0] for n in order], *[out[n][1] for n in order], *[out[n][2] for n in order],
            *[out[n][3] for n in order])
```

```python
import functools

import jax
import jax.numpy as jnp
from jax import lax
from jax.experimental import pallas as pl
from jax.experimental.pallas import tpu as pltpu

f32 = jnp.float32
MXU = jnp.bfloat16
S = jax.ShapeDtypeStruct
MESH = pl.DeviceIdType.MESH

EPS = 1e-6
NEG = -1e30
HD = 64
BLK = 128
A_W, B_W, C_W = 512, 512, 1024
KV_W = 128
IN_W = 3 * A_W + 2 * B_W + C_W + 2 * KV_W
CONV_K = 31
HALO = 32
DILATIONS = (1, 4, 16)
A_DIST, C_DIST = 128, 127
SCALE = HD ** -0.5
VMEM_LIMIT = 56 * 1024 * 1024
VMEM_TALL = 62 * 1024 * 1024

ADAM_LR, ADAM_B1, ADAM_B2, ADAM_EPS, ADAM_WD, ADAM_STEP = 0.001, 0.9, 0.999, 1e-08, 0.01, 10


def _cp(*sem, vmem=VMEM_LIMIT):
    return pltpu.CompilerParams(dimension_semantics=sem, vmem_limit_bytes=vmem)


ANY = pl.BlockSpec(memory_space=pl.ANY)


def _pallas(body, *, comm=None, name, grid, in_specs, out_specs, out_shape, scratch_shapes=(), compiler_params,
            input_output_aliases=None):
    aliases = dict(input_output_aliases or {})
    if comm is None:
        return pl.pallas_call(body, name=name, grid=grid, in_specs=in_specs, out_specs=out_specs, out_shape=out_shape,
                              scratch_shapes=list(scratch_shapes), compiler_params=compiler_params,
                              input_output_aliases=aliases)
    single = not isinstance(out_shape, (list, tuple))
    o_shapes = [out_shape] if single else list(out_shape)
    o_specs = [out_specs] if single else list(out_specs)
    n_in, n_out, n_sc = len(in_specs), len(o_shapes), len(scratch_shapes)
    nci, nco = len(comm.ins), len(comm.out_shapes)
    total = 1
    for g in grid:
        total *= g

    def carried(*refs):
        ins, cins = refs[:n_in], refs[n_in:n_in + nci]
        o0 = n_in + nci
        outs, couts = refs[o0:o0 + n_out], refs[o0 + n_out:o0 + n_out + nco]
        s0 = o0 + n_out + nco
        scratch, sems = refs[s0:s0 + n_sc], refs[s0 + n_sc:]
        step = pl.program_id(0)
        for axis in range(1, len(grid)):
            step = step * grid[axis] + pl.program_id(axis)

        @pl.when(step == 0)
        def _():
            comm.start(cins, couts, sems)

        body(*ins, *outs, *scratch)

        @pl.when(step == (3 * total) // 4)
        def _():
            comm.mid(cins, couts, sems)

        @pl.when(step == total - 1)
        def _():
            comm.finish(cins, couts, sems)

    call = pl.pallas_call(carried, name=name, grid=grid, in_specs=list(in_specs) + [ANY] * nci,
                          out_specs=o_specs + [ANY] * nco, out_shape=o_shapes + list(comm.out_shapes),
                          scratch_shapes=list(scratch_shapes) + list(comm.sem_shapes), compiler_params=compiler_params,
                          input_output_aliases=aliases)

    def run(*args):
        res = call(*args, *comm.ins)
        comm.results = list(res[n_out:])
        return res[0] if single else list(res[:n_out])

    return run


class MultiComm:
    def __init__(self, comms):
        self.comms = list(comms)
        self.ins = [a for c in self.comms for a in c.ins]
        self.out_shapes = [s for c in self.comms for s in c.out_shapes]
        self.sem_shapes = [s for c in self.comms for s in c.sem_shapes]

    def _each(self, cins, couts, sems):
        i = o = s = 0
        for c in self.comms:
            ni, no, ns = len(c.ins), len(c.out_shapes), len(c.sem_shapes)
            yield c, cins[i:i + ni], couts[o:o + no], sems[s:s + ns]
            i, o, s = i + ni, o + no, s + ns

    def start(self, cins, couts, sems):
        for c, a, b, d in self._each(cins, couts, sems):
            c.start(a, b, d)

    def mid(self, cins, couts, sems):
        for c, a, b, d in self._each(cins, couts, sems):
            c.mid(a, b, d)

    def finish(self, cins, couts, sems):
        for c, a, b, d in self._each(cins, couts, sems):
            c.finish(a, b, d)

    @property
    def results(self):
        return [r for c in self.comms for r in c.results]

    @results.setter
    def results(self, vals):
        o = 0
        for c in self.comms:
            c.results = list(vals[o:o + len(c.out_shapes)])
            o += len(c.out_shapes)


def _run_comm(comm, *, name):
    nci, nco = len(comm.ins), len(comm.out_shapes)

    def body(*refs):
        cins, couts, sems = refs[:nci], refs[nci:nci + nco], refs[nci + nco:]
        comm.start(cins, couts, sems)
        comm.mid(cins, couts, sems)
        comm.finish(cins, couts, sems)

    comm.results = list(pl.pallas_call(body, name=name, in_specs=[ANY] * nci, out_specs=[ANY] * nco,
                                       out_shape=list(comm.out_shapes), scratch_shapes=list(comm.sem_shapes))(*comm.ins))


def _nt(a, b):
    return lax.dot_general(a, b, (((1,), (1,)), ((), ())), preferred_element_type=f32)


def _tn(a, b):
    return lax.dot_general(a, b, (((0,), (0,)), ((), ())), preferred_element_type=f32)


def _nn(a, b):
    return jnp.dot(a, b, preferred_element_type=f32)


def _sigmoid(x):
    return 1.0 / (1.0 + jnp.exp(-x))


def _seg_sum(v, e_ref):
    hi = v.astype(jnp.bfloat16)
    lo = (v - hi.astype(f32)).astype(jnp.bfloat16)
    e = e_ref[...]
    return _nn(hi, e) + _nn(lo, e)


def _seg_sum128(v, e_ref):
    e = e_ref[0:128, 0:128]
    hi = v.astype(jnp.bfloat16)
    lo = (v - hi.astype(f32)).astype(jnp.bfloat16)
    return _nn(hi, e) + _nn(lo, e)


def _head_eye():
    r = lax.broadcasted_iota(jnp.int32, (512, 512), 0) // HD
    c = lax.broadcasted_iota(jnp.int32, (512, 512), 1) // HD
    return (r == c).astype(jnp.bfloat16)


def _rms_norm_rows(x_ref, g_ref, h_ref, tm):
    def chunk(c, carry):
        rows = pl.ds(c * BLK, BLK)
        xf = x_ref[rows, :]
        r = lax.rsqrt(jnp.mean(xf * xf, axis=-1, keepdims=True) + EPS)
        h_ref[rows, :] = (xf * r * g_ref[...]).astype(MXU)
        return carry
    lax.fori_loop(0, tm // BLK, chunk, 0)


def rms_proj(x, g, w, *, tm, tn, name, comm=None):
    T, D = x.shape
    N = w.shape[1]

    def body(x_ref, g_ref, w_ref, h_ref, o_ref):
        @pl.when(pl.program_id(1) == 0)
        def _():
            _rms_norm_rows(x_ref, g_ref, h_ref, tm)
        o_ref[...] = _nn(h_ref[...], w_ref[...])

    return _pallas(
        body, comm=comm, name=name, grid=(T // tm, N // tn),
        in_specs=[pl.BlockSpec((tm, D), lambda i, j: (i, 0)), pl.BlockSpec((1, D), lambda i, j: (0, 0)),
                  pl.BlockSpec((D, tn), lambda i, j: (0, j))],
        out_specs=[pl.BlockSpec((tm, D), lambda i, j: (i, 0)), pl.BlockSpec((tm, tn), lambda i, j: (i, j))],
        out_shape=[S((T, D), MXU), S((T, N), f32)],
        compiler_params=_cp("arbitrary", "arbitrary"),
    )(x, g, w)


def rms_swiglu(x, g, wg, wu, *, tm, tn, name, comm=None):
    T, D = x.shape
    N = wg.shape[1]

    def body(x_ref, g_ref, wg_ref, wu_ref, h_ref, gate_ref, up_ref, act_ref):
        @pl.when(pl.program_id(1) == 0)
        def _():
            _rms_norm_rows(x_ref, g_ref, h_ref, tm)
        h = h_ref[...]
        gate = _nn(h, wg_ref[...])
        up = _nn(h, wu_ref[...])
        gate_ref[...] = gate
        up_ref[...] = up
        act_ref[...] = (gate * _sigmoid(gate) * up).astype(MXU)

    wspec = pl.BlockSpec((D, tn), lambda i, j: (0, j))
    ospec = pl.BlockSpec((tm, tn), lambda i, j: (i, j))
    return _pallas(
        body, comm=comm, name=name, grid=(T // tm, N // tn),
        in_specs=[pl.BlockSpec((tm, D), lambda i, j: (i, 0)), pl.BlockSpec((1, D), lambda i, j: (0, 0)), wspec, wspec],
        out_specs=[pl.BlockSpec((tm, D), lambda i, j: (i, 0)), ospec, ospec, ospec],
        out_shape=[S((T, D), MXU), S((T, N), f32), S((T, N), f32), S((T, N), MXU)],
        compiler_params=_cp("arbitrary", "arbitrary"),
    )(x, g, wg, wu)


def matmul_res(a, w, res, *, tm, tn, name, comm=None):
    T, K = a.shape
    N = w.shape[1]

    def body(a_ref, w_ref, r_ref, o_ref):
        o_ref[...] = r_ref[...] + _nn(a_ref[...], w_ref[...])

    return _pallas(
        body, comm=comm, name=name, grid=(T // tm, N // tn),
        in_specs=[pl.BlockSpec((tm, K), lambda i, j: (i, 0)), pl.BlockSpec((K, tn), lambda i, j: (0, j)),
                  pl.BlockSpec((tm, tn), lambda i, j: (i, j))],
        out_specs=pl.BlockSpec((tm, tn), lambda i, j: (i, j)),
        out_shape=S((T, N), f32),
        compiler_params=_cp("arbitrary", "arbitrary"),
    )(a, w, res)


def nt_plain(a, w, *, tm, tn, name, comm=None):
    T, K = a.shape
    N = w.shape[0]

    def body(a_ref, w_ref, o_ref):
        o_ref[...] = _nt(a_ref[...], w_ref[...])

    return _pallas(
        body, comm=comm, name=name, grid=(T // tm, N // tn),
        in_specs=[pl.BlockSpec((tm, K), lambda i, j: (i, 0)), pl.BlockSpec((tn, K), lambda i, j: (j, 0))],
        out_specs=pl.BlockSpec((tm, tn), lambda i, j: (i, j)),
        out_shape=S((T, N), f32),
        compiler_params=_cp("arbitrary", "arbitrary"),
    )(a, w)


def nt_swiglu_bwd(dy, wd, gate, up, *, tm, tn, name, comm=None):
    T, D = dy.shape
    F = wd.shape[0]

    def body(dy_ref, w_ref, g_ref, u_ref, dg_ref, du_ref):
        d_act = _nt(dy_ref[...], w_ref[...])
        g = g_ref[...]
        sg = _sigmoid(g)
        du_ref[...] = (d_act * (g * sg)).astype(MXU)
        dg_ref[...] = (d_act * u_ref[...] * (sg * (1.0 + g * (1.0 - sg)))).astype(MXU)

    blk = pl.BlockSpec((tm, tn), lambda i, j: (i, j))
    return _pallas(
        body, comm=comm, name=name, grid=(T // tm, F // tn),
        in_specs=[pl.BlockSpec((tm, D), lambda i, j: (i, 0)), pl.BlockSpec((tn, D), lambda i, j: (j, 0)), blk, blk],
        out_specs=[blk, blk],
        out_shape=[S((T, F), MXU), S((T, F), MXU)],
        compiler_params=_cp("arbitrary", "arbitrary"),
    )(dy, wd, gate, up)


def nt_rms_bwd(terms, x, g, dres, *, tm, tn, name, comm=None):
    T, D = x.shape
    K = terms[0][0].shape[1]
    nj = D // tn
    ne = tm // BLK
    nt = len(terms)
    ni = T // tm

    def body(*refs):
        a_refs = refs[0:2 * nt:2]
        w_refs = refs[1:2 * nt:2]
        x_ref, g_ref, r_ref, dx_ref, dxb_ref, dg_ref, acc_ref = refs[2 * nt:]
        i, j = pl.program_id(0), pl.program_id(1)

        @pl.when(j < nj)
        def _():
            part = _nt(a_refs[0][...], w_refs[0][...])
            for t in range(1, nt):
                part += _nt(a_refs[t][...], w_refs[t][...])
            acc_ref[j] = part

        @pl.when(j >= nj)
        def _():
            rows = pl.ds(pl.multiple_of((j - nj) * BLK, BLK), BLK)
            dh = jnp.concatenate([acc_ref[jj, rows, :] for jj in range(nj)], axis=1)
            xf = x_ref[...]
            r = lax.rsqrt(jnp.mean(xf * xf, axis=-1, keepdims=True) + EPS)
            y = xf * r
            dy = dh * g_ref[...]
            dx = r_ref[...] + r * (dy - y * jnp.mean(dy * y, axis=-1, keepdims=True))
            dx_ref[...] = dx
            dxb_ref[...] = dx.astype(MXU)
            dgain = jnp.sum(dh * y, axis=0, keepdims=True)
            first = jnp.logical_and(i == 0, j == nj)

            @pl.when(first)
            def _():
                dg_ref[...] = dgain

            @pl.when(jnp.logical_not(first))
            def _():
                dg_ref[...] += dgain

    in_specs, args = [], []
    for a, w in terms:
        in_specs += [pl.BlockSpec((tm, K), lambda i, j: (i, 0)),
                     pl.BlockSpec((tn, K), lambda i, j: (jnp.minimum(j, nj - 1), 0))]
        args += [a, w]
    row = pl.BlockSpec((BLK, D), lambda i, j: (i * ne + jnp.maximum(j - nj, 0), 0))
    vec = pl.BlockSpec((1, D), lambda i, j: (0, 0))
    in_specs += [row, vec, row]
    return _pallas(
        body, comm=comm, name=name, grid=(ni, nj + ne), in_specs=in_specs,
        out_specs=[row, row, vec],
        out_shape=[S((T, D), f32), S((T, D), MXU), S((1, D), f32)],
        scratch_shapes=[pltpu.VMEM((nj, tm, tn), f32)],
        compiler_params=_cp("arbitrary", "arbitrary", vmem=VMEM_TALL),
    )(*args, x, g, dres)


def tn_matmul(a, b, *, tm, tn, tk, name, by_chip=False, comm=None):
    T, M = a.shape
    N = b.shape[1]
    if by_chip:
        tn = N // 4
        out_spec = pl.BlockSpec((None, tm, tn), lambda i, j, k: (j, i, 0))
        out_shape = S((4, M, tn), f32)
    else:
        out_spec = pl.BlockSpec((tm, tn), lambda i, j, k: (i, j))
        out_shape = S((M, N), f32)

    def body(a_ref, b_ref, o_ref):
        part = _tn(a_ref[...], b_ref[...])

        @pl.when(pl.program_id(2) == 0)
        def _():
            o_ref[...] = part

        @pl.when(pl.program_id(2) > 0)
        def _():
            o_ref[...] += part

    return _pallas(
        body, comm=comm, name=name, grid=(M // tm, N // tn, T // tk),
        in_specs=[pl.BlockSpec((tk, tm), lambda i, j, k: (k, i)), pl.BlockSpec((tk, tn), lambda i, j, k: (k, j))],
        out_specs=out_spec, out_shape=out_shape,
        compiler_params=_cp("arbitrary", "arbitrary", "arbitrary"),
    )(a, b)


def loss_head(y, target, *, tm, name):
    T, D = y.shape
    ni = T // tm

    def body(y_ref, t_ref, dy_ref, dyb_ref, l_ref, acc_ref):
        i = pl.program_id(0)
        e = y_ref[...] - t_ref[...]
        dy = e * (1.0 / D)
        dy_ref[...] = dy
        dyb_ref[...] = dy.astype(MXU)
        part = jnp.sum(e * e, axis=0, keepdims=True)

        @pl.when(i == 0)
        def _():
            acc_ref[...] = part

        @pl.when(i > 0)
        def _():
            acc_ref[...] += part

        @pl.when(i == ni - 1)
        def _():
            tot = jnp.sum(acc_ref[...], axis=1, keepdims=True) * (0.5 / D)
            l_ref[...] = jnp.broadcast_to(tot, (1, 128))

    row = pl.BlockSpec((tm, D), lambda i: (i, 0))
    return pl.pallas_call(
        body, name=name, grid=(ni,), in_specs=[row, row],
        out_specs=[row, row, pl.BlockSpec((1, 128), lambda i: (0, 0))],
        out_shape=[S((T, D), f32), S((T, D), MXU), S((1, 128), f32)],
        scratch_shapes=[pltpu.VMEM((1, D), f32)],
        compiler_params=_cp("arbitrary"),
    )(y, target)


def _qk_norm(v, gain, e_ref):
    r = lax.rsqrt(_seg_sum(v * v, e_ref) * (1.0 / HD) + EPS)
    return v * r * gain


def _dup_halves(pair):
    rolled = pltpu.roll(pair, HD, 1)
    lo = lax.broadcasted_iota(jnp.int32, pair.shape, 1) < HD
    return jnp.where(lo, pair, rolled), jnp.where(lo, rolled, pair)


def prep_fwd(proj, e, gains, *, tm, name):
    T = proj.shape[0]

    def body(p_ref, e_ref, gaq, gak, gcq, gck, aq, ak, av, cq, ckk, cvv):
        aq[...] = _qk_norm(p_ref[:, 0:512], gaq[...], e_ref)
        ak[...] = _qk_norm(p_ref[:, 512:1024], gak[...], e_ref)
        av[...] = p_ref[:, 1024:1536]
        cq[:, 0:512] = _qk_norm(p_ref[:, 2560:3072], gcq[...], e_ref).astype(MXU)
        cq[:, 512:1024] = _qk_norm(p_ref[:, 3072:3584], gcq[...], e_ref).astype(MXU)
        kraw = p_ref[:, 3584:3712]
        kn = kraw * lax.rsqrt(_seg_sum128(kraw * kraw, e_ref) * (1.0 / HD) + EPS) * gck[...]
        k0, k1 = _dup_halves(kn)
        ckk[:, 0:128] = k0.astype(MXU)
        ckk[:, 128:256] = k1.astype(MXU)
        v0, v1 = _dup_halves(p_ref[:, 3712:3840])
        cvv[:, 0:128] = v0.astype(MXU)
        cvv[:, 128:256] = v1.astype(MXU)

    def vec(n):
        return pl.BlockSpec((1, n), lambda i: (0, 0))

    def rows(n):
        return pl.BlockSpec((tm, n), lambda i: (i, 0))

    return pl.pallas_call(
        body, name=name, grid=(T // tm,),
        in_specs=[rows(IN_W), pl.BlockSpec((512, 512), lambda i: (0, 0)), vec(512), vec(512), vec(512), vec(128)],
        out_specs=[rows(512), rows(512), rows(512), rows(1024), rows(256), rows(256)],
        out_shape=[S((T, 512), f32)] * 3 + [S((T, 1024), MXU), S((T, 256), MXU), S((T, 256), MXU)],
        compiler_params=_cp("arbitrary"),
    )(proj, e, *gains)


def _band_mask(max_dist, shut):
    r = lax.broadcasted_iota(jnp.int32, (2 * BLK, 2 * BLK), 0) & (BLK - 1)
    c = lax.broadcasted_iota(jnp.int32, (2 * BLK, 2 * BLK), 1)
    prev = jnp.logical_and(c < BLK, c >= r + (BLK - max_dist) + shut)
    return jnp.logical_or(prev, jnp.logical_and(c >= BLK, c - BLK <= r))


def _prev_mask(max_dist, shut):
    r = lax.broadcasted_iota(jnp.int32, (2 * BLK, BLK), 0) & (BLK - 1)
    c = lax.broadcasted_iota(jnp.int32, (2 * BLK, BLK), 1)
    return c >= r + (BLK - max_dist) + shut


def _head_masks():
    lo = (lax.broadcasted_iota(jnp.int32, (BLK, BLK), 1) < HD).astype(f32)
    return lo.astype(MXU), (1.0 - lo).astype(MXU)


def _stack_heads(x, hm):
    return jnp.concatenate([x * hm[0], x * hm[1]], axis=0)


def _unstack_heads(y, lane_lo):
    return jnp.where(lane_lo, y[0:BLK], y[BLK:2 * BLK])


def _rows(ref, start, dil):
    if dil == 1:
        return ref[pl.ds(start, BLK), :]
    return ref[pl.ds(start, BLK, stride=dil), :]


def _set_rows(ref, start, dil, val):
    if dil == 1:
        ref[pl.ds(start, BLK), :] = val
    else:
        ref[pl.ds(start, BLK, stride=dil), :] = val


def _attn_geometry(T, dil):
    span = BLK * dil
    n = max(1, 512 // span)
    return span, n, T // (span * n)


def band_attn_fwd(q, k, v, sinks, *, dil, max_dist, group, name, mix=None, comm=None):
    T = q.shape[0]
    P = q.shape[1] // BLK
    span, n, nb = _attn_geometry(T, dil)

    def body(*refs):
        s_ref = m_ref = None
        q_ref, kc_ref, kp_ref, vc_ref, vp_ref = refs[:5]
        rest = list(refs[5:])
        if sinks is not None:
            s_ref = rest.pop(0)
        if mix is not None:
            rest.pop(0)
            o_ref, l_ref, m_ref = rest
        else:
            o_ref, l_ref = rest
        b = pl.program_id(0)
        mask = _band_mask(max_dist, 0)
        mask0 = _band_mask(max_dist, jnp.where(b > 0, 0, BLK + 1))
        lane_lo = lax.broadcasted_iota(jnp.int32, (BLK, BLK), 1) < HD
        hm = _head_masks()
        if sinks is not None:
            row_lo = lax.broadcasted_iota(jnp.int32, (1, BLK), 1) < HD
            sk0 = jnp.max(jnp.where(row_lo, s_ref[...], NEG), axis=1, keepdims=True)
            sk1 = jnp.max(jnp.where(row_lo, NEG, s_ref[...]), axis=1, keepdims=True)
            sk = jnp.where(lax.broadcasted_iota(jnp.int32, (2 * BLK, 1), 0) < BLK, sk0, sk1)

        def load(r, sub):
            at = r + sub * span
            kc, vc = _rows(kc_ref, at, dil).astype(MXU), _rows(vc_ref, at, dil).astype(MXU)
            if sub == 0:
                kp, vp = _rows(kp_ref, r, dil).astype(MXU), _rows(vp_ref, r, dil).astype(MXU)
            else:
                kp, vp = _rows(kc_ref, at - span, dil).astype(MXU), _rows(vc_ref, at - span, dil).astype(MXU)
            qst = _stack_heads(_rows(q_ref, at, dil).astype(MXU), hm)
            return (qst, jnp.concatenate([kp, kc], axis=0), jnp.concatenate([vp, vc], axis=0),
                    mask0 if sub == 0 else mask, at)

        def attend(items):
            ss = [jnp.where(m_, _nt(qst, kcat) * SCALE, NEG) for qst, kcat, _, m_, _ in items]
            ms = [jnp.max(s, axis=1, keepdims=True) for s in ss]
            if sinks is not None:
                ms = [jnp.maximum(m, sk) for m in ms]
            ps = [jnp.exp(s - m) for s, m in zip(ss, ms)]
            dens = [jnp.sum(p_, axis=1, keepdims=True) for p_ in ps]
            if sinks is not None:
                dens = [d + jnp.exp(sk - m) for d, m in zip(dens, ms)]
            outs = [_nn(p_.astype(MXU), it[2]) / d for p_, it, d in zip(ps, items, dens)]
            for it, o, m, d in zip(items, outs, ms, dens):
                lse = m + jnp.log(d)
                if m_ref is not None:
                    _set_rows(m_ref, it[4], dil, _unstack_heads(o, lane_lo).astype(MXU))
                _set_rows(o_ref, it[4], dil, _unstack_heads(o, lane_lo))
                _set_rows(l_ref, it[4], dil, jnp.where(lane_lo, lse[0:BLK], lse[BLK:2 * BLK]))

        if dil * n <= 4:
            work = [(r, sub) for r in range(dil) for sub in range(n)]
            for g in range(0, len(work), 2):
                attend([load(*w) for w in work[g:g + 2]])
        else:
            def two_streams(i, carry):
                attend([load(2 * i, 0), load(2 * i + 1, 0)])
                return carry
            lax.fori_loop(0, dil // 2, two_streams, 0)

    rows_per_step = span * n
    qspec = pl.BlockSpec((rows_per_step, BLK), lambda b, p: (b, p))
    cur = pl.BlockSpec((rows_per_step, BLK), lambda b, p: (b, p // group))
    prev = pl.BlockSpec((span, BLK), lambda b, p: (jnp.maximum(b * n - 1, 0), p // group))
    in_specs = [qspec, cur, prev, cur, prev]
    args = [q, k, k, v, v]
    if sinks is not None:
        in_specs.append(pl.BlockSpec((1, BLK), lambda b, p: (0, p)))
        args.append(sinks)
    out_specs, out_shape, aliases = [qspec, qspec], [S(q.shape, f32), S(q.shape, f32)], {}
    if mix is not None:
        first_block = mix.shape[1] // BLK - P
        aliases = {len(args): 2}
        in_specs.append(ANY)
        args.append(mix)
        out_specs.append(pl.BlockSpec((rows_per_step, BLK), lambda b, p: (b, first_block + p)))
        out_shape.append(S(mix.shape, mix.dtype))
    return _pallas(
        body, comm=comm, name=name, grid=(nb, P), in_specs=in_specs, out_specs=out_specs, out_shape=out_shape,
        compiler_params=_cp("arbitrary", "arbitrary"), input_output_aliases=aliases,
    )(*args)


def band_attn_bwd(q, k, v, lse, do, dd, *, dil, max_dist, group, name, comm=None):
    T = q.shape[0]
    P = q.shape[1] // BLK
    span, n, nb = _attn_geometry(T, dil)
    assert group == 1 or dil == 1

    def body(q_ref, qn_ref, do_ref, don_ref, l_ref, ln_ref, d_ref, dn_ref, kc_ref, kp_ref, vc_ref, vp_ref,
             dq_ref, dk_ref, dv_ref):
        b, p = pl.program_id(0), pl.program_id(1)
        mask = _band_mask(max_dist, 0)
        mask0 = _band_mask(max_dist, jnp.where(b > 0, 0, BLK + 1))
        tail = _prev_mask(max_dist, jnp.where(b < nb - 1, 0, BLK + 1))
        lane_lo = lax.broadcasted_iota(jnp.int32, (BLK, BLK), 1) < HD
        hm = _head_masks()
        own_lanes = (lax.broadcasted_iota(jnp.int32, (2 * BLK, BLK), 1) < HD) == (
            lax.broadcasted_iota(jnp.int32, (2 * BLK, BLK), 0) < BLK)

        def per_row(x):
            return jnp.max(jnp.where(own_lanes, jnp.concatenate([x, x], axis=0), NEG), axis=1, keepdims=True)

        def q_side(refs, at):
            q_r, do_r, l_r, d_r = refs
            return (_stack_heads(_rows(q_r, at, dil).astype(MXU), hm), _stack_heads(_rows(do_r, at, dil).astype(MXU), hm),
                    per_row(_rows(l_r, at, dil)), per_row(_rows(d_r, at, dil)))

        def kv(ref, at):
            return _rows(ref, at, dil).astype(MXU)

        first = p % group == 0

        def put_kv(ref, at, val):
            if group == 1:
                _set_rows(ref, at, dil, val)
            else:
                @pl.when(first)
                def _():
                    ref[pl.ds(at, BLK), :] = val

                @pl.when(jnp.logical_not(first))
                def _():
                    ref[pl.ds(at, BLK), :] += val

        def stream(r):
            dks, dvs = [None] * n, [None] * n
            for sub in range(n):
                at = r + sub * span
                qst, dost, lrow, drow = q_side((q_ref, do_ref, l_ref, d_ref), at)
                if sub == 0:
                    kp, vp, m_ = kv(kp_ref, r), kv(vp_ref, r), mask0
                else:
                    kp, vp, m_ = kv(kc_ref, at - span), kv(vc_ref, at - span), mask
                kcat = jnp.concatenate([kp, kv(kc_ref, at)], axis=0)
                vcat = jnp.concatenate([vp, kv(vc_ref, at)], axis=0)
                pr = jnp.where(m_, jnp.exp(_nt(qst, kcat) * SCALE - lrow), 0.0)
                ds = (pr * (_nt(dost, vcat) - drow) * SCALE).astype(MXU)
                prb = pr.astype(MXU)
                _set_rows(dq_ref, at, dil, _unstack_heads(_nn(ds, kcat), lane_lo))
                if sub == 0:
                    dks[0] = _tn(ds[:, BLK:], qst)
                    dvs[0] = _tn(prb[:, BLK:], dost)
                else:
                    dkk, dvv = _tn(ds, qst), _tn(prb, dost)
                    dks[sub - 1] += dkk[0:BLK]
                    dvs[sub - 1] += dvv[0:BLK]
                    dks[sub], dvs[sub] = dkk[BLK:], dvv[BLK:]
            at = r + (n - 1) * span
            qst, dost, lrow, drow = q_side((qn_ref, don_ref, ln_ref, dn_ref), r)
            pr = jnp.where(tail, jnp.exp(_nt(qst, kv(kc_ref, at)) * SCALE - lrow), 0.0)
            ds = (pr * (_nt(dost, kv(vc_ref, at)) - drow) * SCALE).astype(MXU)
            dks[n - 1] += _tn(ds, qst)
            dvs[n - 1] += _tn(pr.astype(MXU), dost)
            for sub in range(n):
                put_kv(dk_ref, r + sub * span, dks[sub])
                put_kv(dv_ref, r + sub * span, dvs[sub])

        if dil <= 4:
            for r in range(dil):
                stream(r)
        else:
            def two_streams(i, carry):
                stream(2 * i)
                stream(2 * i + 1)
                return carry
            lax.fori_loop(0, dil // 2, two_streams, 0)

    rows_per_step = span * n
    qspec = pl.BlockSpec((rows_per_step, BLK), lambda b, p: (b, p))
    qnext = pl.BlockSpec((span, BLK), lambda b, p: (jnp.minimum((b + 1) * n, T // span - 1), p))
    cur = pl.BlockSpec((rows_per_step, BLK), lambda b, p: (b, p // group))
    prev = pl.BlockSpec((span, BLK), lambda b, p: (jnp.maximum(b * n - 1, 0), p // group))
    return _pallas(
        body, comm=comm, name=name, grid=(nb, P),
        in_specs=[qspec, qnext, qspec, qnext, qspec, qnext, qspec, qnext, cur, prev, cur, prev],
        out_specs=[qspec, cur, cur],
        out_shape=[S(q.shape, f32), S(k.shape, f32), S(k.shape, f32)],
        compiler_params=_cp("arbitrary", "arbitrary"),
    )(q, q, do, do, lse, lse, dd, dd, k, k, v, v)


def dil_combine_fwd(ols, *, tm, name):
    T = ols[0].shape[0]

    def body(o1, l1, o2, l2, o3, l3, out_ref):
        a, b, c = l1[...], l2[...], l3[...]
        m = jnp.maximum(jnp.maximum(a, b), c)
        ea, eb, ec = jnp.exp(a - m), jnp.exp(b - m), jnp.exp(c - m)
        out = (ea * o1[...] + eb * o2[...] + ec * o3[...]) / (ea + eb + ec)
        out_ref[...] = out.astype(MXU)

    row = pl.BlockSpec((tm, 512), lambda i: (i, 0))
    return pl.pallas_call(body, name=name, grid=(T // tm,), in_specs=[row] * 6, out_specs=row,
                          out_shape=S((T, A_W + B_W + C_W), MXU), compiler_params=_cp("arbitrary"))(*ols)


def dil_combine_bwd(ols, dmix, e, *, tm, name):
    T = ols[0].shape[0]

    def body(o1, l1, o2, l2, o3, l3, d_ref, e_ref, do1, do2, do3, dd1, dd2, dd3):
        a, b, c = l1[...], l2[...], l3[...]
        m = jnp.maximum(jnp.maximum(a, b), c)
        ea, eb, ec = jnp.exp(a - m), jnp.exp(b - m), jnp.exp(c - m)
        inv = 1.0 / (ea + eb + ec)
        wa, wb, wc = ea * inv, eb * inv, ec * inv
        dout = d_ref[...]
        gbar = _seg_sum(dout * (wa * o1[...] + wb * o2[...] + wc * o3[...]), e_ref)
        do1[...] = wa * dout
        do2[...] = wb * dout
        do3[...] = wc * dout
        dd1[...] = wa * gbar
        dd2[...] = wb * gbar
        dd3[...] = wc * gbar

    row = pl.BlockSpec((tm, 512), lambda i: (i, 0))
    return pl.pallas_call(
        body, name=name, grid=(T // tm,),
        in_specs=[row] * 6 + [row, pl.BlockSpec((512, 512), lambda i: (0, 0))],
        out_specs=[row] * 6,
        out_shape=[S((T, 512), f32)] * 6,
        compiler_params=_cp("arbitrary"),
    )(*ols, dmix, e)


def swa_pre_bwd(o, lse, dmix, sinks, e, *, tm, name):
    T = o.shape[0]
    ni = T // tm

    def body(o_ref, l_ref, d_ref, s_ref, e_ref, do_ref, dd_ref, ds_ref):
        i = pl.program_id(0)
        dout = d_ref[...]
        do_ref[...] = dout.astype(MXU)
        prod = dout * o_ref[...]
        dd = jnp.concatenate([_seg_sum(prod[:, 0:512], e_ref), _seg_sum(prod[:, 512:1024], e_ref)], axis=1)
        dd_ref[...] = dd
        part = -jnp.sum(jnp.exp(s_ref[...] - l_ref[...]) * dd, axis=0, keepdims=True)

        @pl.when(i == 0)
        def _():
            ds_ref[...] = part

        @pl.when(i > 0)
        def _():
            ds_ref[...] += part

    row = pl.BlockSpec((tm, 1024), lambda i: (i, 0))
    vec = pl.BlockSpec((1, 1024), lambda i: (0, 0))
    return pl.pallas_call(
        body, name=name, grid=(ni,),
        in_specs=[row, row, pl.BlockSpec((tm, 1024), lambda i: (i, 1)), vec, pl.BlockSpec((512, 512), lambda i: (0, 0))],
        out_specs=[row, row, vec],
        out_shape=[S((T, 1024), MXU), S((T, 1024), f32), S((1, 1024), f32)],
        compiler_params=_cp("arbitrary"),
    )(o, lse, dmix, sinks, e)


SHIFT_PAD = 24


def _shifted_copies(buf_ref, sh_ref, length):
    for r in range(1, 8):
        sh_ref[r - 1, 0:length, :] = buf_ref[pl.ds(r, length), :]


def _window(buf_ref, sh_ref, start, rows):
    q, r = divmod(start, 8)
    if r == 0:
        return buf_ref[pl.ds(8 * q, rows), :]
    return sh_ref[r - 1, pl.ds(8 * q, rows), :]


TAP_ROWS = 64


def _tap_sum(buf_ref, sh_ref, w_ref, starts, rows):
    outs = []
    for c0 in range(0, rows, TAP_ROWS):
        n = min(TAP_ROWS, rows - c0)
        acc = _window(buf_ref, sh_ref, starts[0] + c0, n) * w_ref[pl.ds(0, 1), :]
        for j in range(1, CONV_K):
            acc += _window(buf_ref, sh_ref, starts[j] + c0, n) * w_ref[pl.ds(j, 1), :]
        outs.append(acc)
    return jnp.concatenate(outs, axis=0)


def _conv_taps(buf_ref, sh_ref, w_ref, start, rows):
    return _tap_sum(buf_ref, sh_ref, w_ref, [start + j for j in range(CONV_K)], rows)


def conv_fwd(proj, mix, w, b, ln_g, ln_b, *, tb, name):
    T = proj.shape[0]
    hb = tb // HALO

    def body(u_ref, g_ref, up_ref, gp_ref, w_ref, b_ref, lg_ref, lb_ref, mix_ref, o_ref, hbuf, hsh):
        i = pl.program_id(0)
        hprev = up_ref[...] * _sigmoid(gp_ref[...])
        hbuf[0:HALO, :] = hprev * jnp.where(i > 0, 1.0, 0.0)
        hbuf[HALO:HALO + tb, :] = u_ref[...] * _sigmoid(g_ref[...])
        _shifted_copies(hbuf, hsh, tb + SHIFT_PAD)
        y = _conv_taps(hbuf, hsh, w_ref, HALO - (CONV_K - 1), tb) + b_ref[...]
        mu = jnp.mean(y, axis=-1, keepdims=True)
        yc = y - mu
        var = jnp.mean(yc * yc, axis=-1, keepdims=True)
        z = yc * lax.rsqrt(var + EPS) * lg_ref[...] + lb_ref[...]
        o_ref[...] = (z * _sigmoid(z)).astype(MXU)

    vec = pl.BlockSpec((1, 512), lambda i: (0, 0))
    return pl.pallas_call(
        body, name=name, grid=(T // tb,),
        in_specs=[pl.BlockSpec((tb, 512), lambda i: (i, 3)), pl.BlockSpec((tb, 512), lambda i: (i, 4)),
                  pl.BlockSpec((HALO, 512), lambda i: (jnp.maximum(i * hb - 1, 0), 3)),
                  pl.BlockSpec((HALO, 512), lambda i: (jnp.maximum(i * hb - 1, 0), 4)),
                  pl.BlockSpec((HALO, 512), lambda i: (0, 0)), vec, vec, vec, ANY],
        out_specs=pl.BlockSpec((tb, 512), lambda i: (i, 1)),
        out_shape=S(mix.shape, mix.dtype),
        scratch_shapes=[pltpu.VMEM((tb + HALO, 512), f32), pltpu.VMEM((7, tb + SHIFT_PAD, 512), f32)],
        compiler_params=_cp("arbitrary"), input_output_aliases={8: 0},
    )(proj, proj, proj, proj, w, b, ln_g, ln_b, mix)


def conv_bwd(proj, dmix, w, b, ln_g, ln_b, *, tb, name, comm=None):
    T = proj.shape[0]
    hb = tb // HALO
    ni = T // tb
    last_h = T // HALO - 1
    ext = tb + HALO

    def body(u_ref, g_ref, up_ref, gp_ref, un_ref, gn_ref, d_ref, dn_ref, w_ref, b_ref, lg_ref, lb_ref,
             du_ref, dg_ref, dw_ref, db_ref, dlg_ref, dlb_ref, hbuf, dybuf, hsh, dsh):
        i = pl.program_id(0)
        hbuf[0:HALO, :] = up_ref[...] * _sigmoid(gp_ref[...]) * jnp.where(i > 0, 1.0, 0.0)
        u = u_ref[...]
        sg = _sigmoid(g_ref[...])
        hbuf[HALO:HALO + tb, :] = u * sg
        hbuf[HALO + tb:HALO + ext, :] = un_ref[...] * _sigmoid(gn_ref[...])
        _shifted_copies(hbuf, hsh, ext + SHIFT_PAD)
        y = _conv_taps(hbuf, hsh, w_ref, HALO - (CONV_K - 1), ext) + b_ref[...]
        mu = jnp.mean(y, axis=-1, keepdims=True)
        yc = y - mu
        rstd = lax.rsqrt(jnp.mean(yc * yc, axis=-1, keepdims=True) + EPS)
        yn = yc * rstd
        z = yn * lg_ref[...] + lb_ref[...]
        sz = _sigmoid(z)
        row = lax.broadcasted_iota(jnp.int32, (ext, 1), 0)
        own = row < tb
        keep = row < jnp.where(i < ni - 1, ext, tb)
        dout = jnp.concatenate([d_ref[...], dn_ref[...]], axis=0)
        dz = jnp.where(keep, dout * (sz * (1.0 + z * (1.0 - sz))), 0.0)
        dyn = dz * lg_ref[...]
        dy = rstd * (dyn - jnp.mean(dyn, axis=-1, keepdims=True) - yn * jnp.mean(dyn * yn, axis=-1, keepdims=True))
        dybuf[...] = dy
        _shifted_copies(dybuf, dsh, tb + SHIFT_PAD)
        dz_own = jnp.where(own, dz, 0.0)
        dlg = jnp.sum(dz_own * yn, axis=0, keepdims=True)
        dlb = jnp.sum(dz_own, axis=0, keepdims=True)
        dy_own = dybuf[0:tb, :]
        dbias = jnp.sum(dy_own, axis=0, keepdims=True)
        dh = _tap_sum(dybuf, dsh, w_ref, [CONV_K - 1 - j for j in range(CONV_K)], tb)
        du_ref[...] = (dh * sg).astype(MXU)
        dg_ref[...] = (dh * u * sg * (1.0 - sg)).astype(MXU)
        taps = [jnp.sum(dy_own * _window(hbuf, hsh, HALO - (CONV_K - 1) + j, tb), axis=0, keepdims=True)
                for j in range(CONV_K)]
        taps.append(jnp.zeros((1, 512), f32))
        dwt = jnp.concatenate(taps, axis=0)

        @pl.when(i == 0)
        def _():
            dw_ref[...] = dwt
            db_ref[...] = dbias
            dlg_ref[...] = dlg
            dlb_ref[...] = dlb

        @pl.when(i > 0)
        def _():
            dw_ref[...] += dwt
            db_ref[...] += dbias
            dlg_ref[...] += dlg
            dlb_ref[...] += dlb

    vec = pl.BlockSpec((1, 512), lambda i: (0, 0))
    wspec = pl.BlockSpec((HALO, 512), lambda i: (0, 0))

    def halo_prev(col):
        return pl.BlockSpec((HALO, 512), lambda i: (jnp.maximum(i * hb - 1, 0), col))

    def halo_next(col):
        return pl.BlockSpec((HALO, 512), lambda i: (jnp.minimum((i + 1) * hb, last_h), col))

    row = pl.BlockSpec((tb, 512), lambda i: (i, 0))
    return _pallas(
        body, comm=comm, name=name, grid=(ni,),
        in_specs=[pl.BlockSpec((tb, 512), lambda i: (i, 3)), pl.BlockSpec((tb, 512), lambda i: (i, 4)),
                  halo_prev(3), halo_prev(4), halo_next(3), halo_next(4),
                  pl.BlockSpec((tb, 512), lambda i: (i, 1)), halo_next(1), wspec, vec, vec, vec],
        out_specs=[row, row, wspec, vec, vec, vec],
        out_shape=[S((T, 512), MXU), S((T, 512), MXU), S((HALO, 512), f32)] + [S((1, 512), f32)] * 3,
        scratch_shapes=[pltpu.VMEM((tb + 2 * HALO, 512), f32), pltpu.VMEM((ext, 512), f32),
                        pltpu.VMEM((7, ext + SHIFT_PAD, 512), f32), pltpu.VMEM((7, tb + SHIFT_PAD, 512), f32)],
        compiler_params=_cp("arbitrary"),
    )(proj, proj, proj, proj, proj, proj, dmix, dmix, w, b, ln_g, ln_b)


def _qk_norm_bwd(v, gain, dout, e_ref):
    r = lax.rsqrt(_seg_sum(v * v, e_ref) * (1.0 / HD) + EPS)
    y = v * r
    dgain = jnp.sum(dout * y, axis=0, keepdims=True)
    dy = dout * gain
    dv = r * (dy - y * (_seg_sum(dy * y, e_ref) * (1.0 / HD)))
    return dv, dgain


def prep_bwd(proj, e, gains, da, dc, dconv, *, tm, name):
    T = proj.shape[0]

    def body(*refs):
        p_ref, e_ref, gaq, gak, gcq, gck = refs[0:6]
        a_refs = refs[6:15]
        dcq, dckk, dcvv, du, dgt = refs[15:20]
        dp, gaq_o, gak_o, gcq_o, gck_o = refs[20:]
        i = pl.program_id(0)
        dq = a_refs[0][...] + a_refs[3][...] + a_refs[6][...]
        dk = a_refs[1][...] + a_refs[4][...] + a_refs[7][...]
        dv = a_refs[2][...] + a_refs[5][...] + a_refs[8][...]
        d, g_aq = _qk_norm_bwd(p_ref[:, 0:512], gaq[...], dq, e_ref)
        dp[:, 0:512] = d.astype(MXU)
        d, g_ak = _qk_norm_bwd(p_ref[:, 512:1024], gak[...], dk, e_ref)
        dp[:, 512:1024] = d.astype(MXU)
        dp[:, 1024:1536] = dv.astype(MXU)
        dp[:, 1536:2048] = du[...]
        dp[:, 2048:2560] = dgt[...]
        d, g_cq0 = _qk_norm_bwd(p_ref[:, 2560:3072], gcq[...], dcq[:, 0:512], e_ref)
        dp[:, 2560:3072] = d.astype(MXU)
        d, g_cq1 = _qk_norm_bwd(p_ref[:, 3072:3584], gcq[...], dcq[:, 512:1024], e_ref)
        dp[:, 3072:3584] = d.astype(MXU)
        lo = lax.broadcasted_iota(jnp.int32, (tm, 128), 1) < HD

        def fold(ref):
            g0, g1 = ref[:, 0:128], ref[:, 128:256]
            s0 = g0 + pltpu.roll(g0, HD, 1)
            s1 = g1 + pltpu.roll(g1, HD, 1)
            return jnp.where(lo, s0, s1)

        dkn = fold(dckk)
        kraw = p_ref[:, 3584:3712]
        r = lax.rsqrt(_seg_sum128(kraw * kraw, e_ref) * (1.0 / HD) + EPS)
        y = kraw * r
        g_ck = jnp.sum(dkn * y, axis=0, keepdims=True)
        dy = dkn * gck[...]
        dp[:, 3584:3712] = (r * (dy - y * (_seg_sum128(dy * y, e_ref) * (1.0 / HD)))).astype(MXU)
        dp[:, 3712:3840] = fold(dcvv).astype(MXU)
        g_cq = jnp.concatenate([g_cq0, g_cq1], axis=1)

        @pl.when(i == 0)
        def _():
            gaq_o[...] = g_aq
            gak_o[...] = g_ak
            gcq_o[...] = g_cq
            gck_o[...] = g_ck

        @pl.when(i > 0)
        def _():
            gaq_o[...] += g_aq
            gak_o[...] += g_ak
            gcq_o[...] += g_cq
            gck_o[...] += g_ck

    def vec(n):
        return pl.BlockSpec((1, n), lambda i: (0, 0))

    def rows(n):
        return pl.BlockSpec((tm, n), lambda i: (i, 0))

    return pl.pallas_call(
        body, name=name, grid=(T // tm,),
        in_specs=[rows(IN_W), pl.BlockSpec((512, 512), lambda i: (0, 0)), vec(512), vec(512), vec(512), vec(128)]
        + [rows(512)] * 9 + [rows(1024), rows(256), rows(256), rows(512), rows(512)],
        out_specs=[rows(IN_W), vec(512), vec(512), vec(1024), vec(128)],
        out_shape=[S((T, IN_W), MXU), S((1, 512), f32), S((1, 512), f32), S((1, 1024), f32), S((1, 128), f32)],
        compiler_params=_cp("arbitrary"),
    )(proj, e, *gains, *da, *dc, *dconv)


def adamw(w, m, v, pieces, *, tr, name, comm=None):
    n, R, C = w.shape
    c1 = 1.0 - ADAM_B1 ** ADAM_STEP
    c2 = 1.0 - ADAM_B2 ** ADAM_STEP
    npc = len(pieces)

    def body(*refs):
        w_ref, m_ref, v_ref = refs[0:3]
        p_refs = refs[3:3 + npc]
        g_ref, d_ref, mo_ref, vo_ref = refs[3 + npc:]
        g = p_refs[0][...].astype(f32)
        for p in p_refs[1:]:
            g = g + p[...].astype(f32)
        mn = ADAM_B1 * m_ref[...] + (1.0 - ADAM_B1) * g
        vn = ADAM_B2 * v_ref[...] + (1.0 - ADAM_B2) * (g * g)
        g_ref[...] = g
        mo_ref[...] = mn
        vo_ref[...] = vn
        d_ref[...] = -ADAM_LR * ((mn / c1) / (jnp.sqrt(vn / c2) + ADAM_EPS) + ADAM_WD * w_ref[...])

    blk = pl.BlockSpec((None, tr, C), lambda l, i: (l, i, 0))
    return _pallas(
        body, comm=comm, name=name, grid=(n, R // tr), in_specs=[blk] * (3 + npc), out_specs=[blk] * 4,
        out_shape=[S(w.shape, f32)] * 4, compiler_params=_cp("arbitrary", "arbitrary"),
    )(w, m, v, *pieces)


def add_halves(pieces, other, *, tr, name):
    _, _, r, cc = pieces.shape

    def body(c_ref, a_ref, b_ref, o_ref):
        o_ref[...] = (a_ref[...] + b_ref[...]).astype(jnp.bfloat16)

    blk = pl.BlockSpec((None, tr, cc), lambda s, i, c_ref: (s, i, 0))
    grid_spec = pltpu.PrefetchScalarGridSpec(
        num_scalar_prefetch=1, grid=(4, r // tr),
        in_specs=[pl.BlockSpec((None, None, tr, cc), lambda s, i, c_ref: (s, c_ref[0], i, 0)), blk], out_specs=blk)
    core = lax.axis_index("c").astype(jnp.int32).reshape(1)
    return pl.pallas_call(body, name=name, grid_spec=grid_spec, out_shape=S((4, r, cc), jnp.bfloat16),
                          compiler_params=_cp("arbitrary", "arbitrary"))(core, pieces, other)


def sum8(parts, *, name):
    _, R, C = parts.shape

    def body(p_ref, o_ref):
        acc = p_ref[0]
        for d in range(1, 8):
            acc = acc + p_ref[d]
        o_ref[...] = acc

    return pl.pallas_call(body, name=name, out_shape=S((R, C), f32))(parts)


def _pos():
    return lax.axis_index("x"), lax.axis_index("y"), lax.axis_index("c")


def _other_chips(x, y):
    return [(1 - x, y), (x, 1 - y), (1 - x, 1 - y)]


class GatherComm:
    def __init__(self, shards, in_place=None):
        self.ins = list(shards)
        self.nt = nt = len(shards)
        self.in_place = list(in_place) if in_place is not None else [False] * nt
        self.out_shapes = [S((2 * s.shape[1], 4 * s.shape[2]), s.dtype) if ip else S((4,) + s.shape, s.dtype)
                           for s, ip in zip(shards, self.in_place)]
        self.sem_shapes = [pltpu.SemaphoreType.DMA((nt, 6)), pltpu.SemaphoreType.DMA((nt, 6)),
                           pltpu.SemaphoreType.DMA((nt, 2))]
        self.results = None

    def _place(self, couts, t, chip, half):
        cid = 2 * chip[0] + chip[1]
        if not self.in_place[t]:
            return couts[t].at[cid, half]
        _, r, c = self.ins[t].shape
        row0 = half * r if isinstance(half, int) else pl.multiple_of(half * r, 16)
        return couts[t].at[pl.ds(row0, r), pl.ds(pl.multiple_of(cid * c, 128), c)]

    def _copy(self, couts, sems, t, k, chip, half, to, src=None):
        dst = self._place(couts, t, chip, half)
        return pltpu.make_async_remote_copy(
            src_ref=dst if src is None else src, dst_ref=dst,
            send_sem=sems[0].at[t, k], recv_sem=sems[1].at[t, k], device_id=to, device_id_type=MESH)

    def _local(self, cins, couts, sems, t):
        x, y, _ = _pos()
        return [pltpu.make_async_copy(cins[t].at[half], self._place(couts, t, (x, y), half), sems[2].at[t, half])
                for half in range(2)]

    def start(self, cins, couts, sems):
        x, y, c = _pos()
        for t in range(self.nt):
            for cp in self._local(cins, couts, sems, t):
                cp.start()
            for j, chip in enumerate(_other_chips(x, y)):
                self._copy(couts, sems, t, j, (x, y), c, (*chip, c), src=cins[t].at[c]).start()

    def mid(self, cins, couts, sems):
        x, y, c = _pos()
        for t in range(self.nt):
            for j, chip in enumerate(_other_chips(x, y)):
                self._copy(couts, sems, t, j, chip, c, (x, y, c)).wait_recv()
                self._copy(couts, sems, t, 3 + j, chip, c, (x, y, 1 - c)).start()

    def finish(self, cins, couts, sems):
        x, y, c = _pos()
        for t in range(self.nt):
            for j, chip in enumerate(_other_chips(x, y)):
                self._copy(couts, sems, t, 3 + j, chip, 1 - c, (x, y, c)).wait_recv()
        for t in range(self.nt):
            for j, chip in enumerate(_other_chips(x, y)):
                self._copy(couts, sems, t, j, (x, y), c, (*chip, c), src=cins[t].at[c]).wait_send()
                self._copy(couts, sems, t, 3 + j, chip, c, (x, y, 1 - c)).wait_send()
            for cp in self._local(cins, couts, sems, t):
                cp.wait()


class SwapComm:
    def __init__(self, pieces):
        self.ins = list(pieces)
        self.nt = nt = len(pieces)
        self.out_shapes = [S((4,) + p.shape[2:], p.dtype) for p in pieces]
        self.sem_shapes = [pltpu.SemaphoreType.DMA((nt, 4)), pltpu.SemaphoreType.DMA((nt, 4))]
        self.results = None

    def _copies(self, cins, couts, sems):
        x, y, c = _pos()
        return [pltpu.make_async_remote_copy(src_ref=cins[t].at[s, 1 - c], dst_ref=couts[t].at[s],
                                             send_sem=sems[0].at[t, s], recv_sem=sems[1].at[t, s],
                                             device_id=(x, y, 1 - c), device_id_type=MESH)
                for t in range(self.nt) for s in range(4)]

    def start(self, cins, couts, sems):
        for cp in self._copies(cins, couts, sems):
            cp.start()

    def mid(self, cins, couts, sems):
        pass

    def finish(self, cins, couts, sems):
        for cp in self._copies(cins, couts, sems):
            cp.wait()


class ExchangeComm:
    def __init__(self, arrs):
        self.ins = list(arrs)
        self.nt = nt = len(arrs)
        self.out_shapes = [S((2,) + a.shape, a.dtype) for a in arrs]
        self.sem_shapes = [pltpu.SemaphoreType.DMA((nt, 7)), pltpu.SemaphoreType.DMA((nt, 7)),
                           pltpu.SemaphoreType.DMA((nt,))]
        self.results = None

    def _copy(self, couts, sems, t, k, half, src_chip, to, src=None):
        dst = couts[t].at[half, src_chip]
        return pltpu.make_async_remote_copy(
            src_ref=dst if src is None else src, dst_ref=dst,
            send_sem=sems[0].at[t, k], recv_sem=sems[1].at[t, k], device_id=to, device_id_type=MESH)

    def _local(self, cins, couts, sems, t):
        x, y, c = _pos()
        return pltpu.make_async_copy(cins[t].at[2 * x + y], couts[t].at[c, 2 * x + y], sems[2].at[t])

    def _firsts(self, cins, couts, sems, t):
        x, y, c = _pos()
        me = 2 * x + y
        cps = [self._copy(couts, sems, t, j, c, me, (*chip, c), src=cins[t].at[2 * chip[0] + chip[1]])
               for j, chip in enumerate(_other_chips(x, y))]
        return cps + [self._copy(couts, sems, t, 6, c, me, (x, y, 1 - c), src=cins[t].at[me])]

    def start(self, cins, couts, sems):
        for t in range(self.nt):
            self._local(cins, couts, sems, t).start()
            for cp in self._firsts(cins, couts, sems, t):
                cp.start()

    def mid(self, cins, couts, sems):
        x, y, c = _pos()
        for t in range(self.nt):
            for j, chip in enumerate(_other_chips(x, y)):
                cid = 2 * chip[0] + chip[1]
                self._copy(couts, sems, t, j, c, cid, (x, y, c)).wait_recv()
                self._copy(couts, sems, t, 3 + j, c, cid, (x, y, 1 - c)).start()

    def finish(self, cins, couts, sems):
        x, y, c = _pos()
        for t in range(self.nt):
            for j, chip in enumerate(_other_chips(x, y)):
                self._copy(couts, sems, t, 3 + j, 1 - c, 2 * chip[0] + chip[1], (x, y, c)).wait_recv()
            self._copy(couts, sems, t, 6, 1 - c, 2 * x + y, (x, y, c)).wait_recv()
        for t in range(self.nt):
            for cp in self._firsts(cins, couts, sems, t):
                cp.wait_send()
            for j, chip in enumerate(_other_chips(x, y)):
                self._copy(couts, sems, t, 3 + j, c, 2 * chip[0] + chip[1], (x, y, 1 - c)).wait_send()
            self._local(cins, couts, sems, t).wait()


def gather_small(vec, *, name):
    R, C = vec.shape

    def body(v_ref, out_ref, send_sems, recv_sems):
        x, y, c = _pos()
        me = 4 * x + 2 * y + c
        out_ref[me] = v_ref[...]
        cps = []
        def peer(k):
            fx, fy, fc = (k >> 2) & 1, (k >> 1) & 1, k & 1
            return (1 - x if fx else x), (1 - y if fy else y), (1 - c if fc else c)

        for k in range(1, 8):
            cp = pltpu.make_async_remote_copy(src_ref=v_ref, dst_ref=out_ref.at[me], send_sem=send_sems.at[k - 1],
                                              recv_sem=recv_sems.at[k - 1], device_id=peer(k), device_id_type=MESH)
            cp.start()
            cps.append(cp)
        for k in range(1, 8):
            px, py, pc = peer(k)
            pltpu.make_async_remote_copy(src_ref=v_ref, dst_ref=out_ref.at[4 * px + 2 * py + pc],
                                         send_sem=send_sems.at[k - 1], recv_sem=recv_sems.at[k - 1],
                                         device_id=(px, py, pc), device_id_type=MESH).wait_recv()
        for cp in cps:
            cp.wait_send()

    return pl.pallas_call(
        body, name=name,
        in_specs=[pl.BlockSpec(memory_space=pltpu.VMEM)], out_specs=pl.BlockSpec(memory_space=pltpu.VMEM),
        out_shape=S((8, R, C), vec.dtype),
        scratch_shapes=[pltpu.SemaphoreType.DMA((7,)), pltpu.SemaphoreType.DMA((7,))],
    )(vec)


def _tile(n, prefs):
    for p in prefs:
        if n % p == 0:
            return p
    return n


def _lanes(g, reps):
    return jnp.tile(g.reshape(1, -1), (1, reps))


class _NoRide:
    def rider(self, name):
        return None

    def landed(self, comm):
        pass

    def grad(self, name, val):
        pass


def _layer_fwd(x, p, e, ride=_NoRide()):
    T, D = x.shape
    tm = _tile(T, (512, 256, 128))
    tall = _tile(T, (1024, 512, 256, 128))

    def carried(fn, *args, name, **kw):
        comm = ride.rider(name)
        out = fn(*args, name=name, comm=comm, **kw)
        ride.landed(comm)
        return out

    h, proj = carried(rms_proj, x, p["norm1_g"], p["w_in"], tm=tall, tn=_tile(IN_W, (768,)), name="rms_proj")
    gains = (_lanes(p["a_q_g"], 8), _lanes(p["a_k_g"], 8), _lanes(p["c_q_g"], 8), _lanes(p["c_k_g"], 2))
    aq, ak, av, cq, ckk, cvv = prep_fwd(proj, e, gains, tm=_tile(T, (256, 128)), name="prep_fwd")
    ols = []
    for d in DILATIONS:
        ols += carried(band_attn_fwd, aq, ak, av, None, dil=d, max_dist=A_DIST, group=1, name=f"dil_attn_fwd_{d}")
    mix = dil_combine_fwd(ols, tm=tm, name="dil_combine_fwd")
    mix = conv_fwd(proj, mix, p["conv_w"], p["conv_b"], p["conv_ln_g"], p["conv_ln_b"], tb=tm, name="conv_fwd")
    sinks = jnp.repeat(p["c_sinks"].reshape(-1), HD).reshape(1, C_W)
    o_c, l_c, mix = carried(band_attn_fwd, cq, ckk, cvv, sinks, dil=1, max_dist=C_DIST, group=4, mix=mix,
                            name="swa_attn_fwd")
    x1 = carried(matmul_res, mix, p["w_out"], x, tm=tall, tn=_tile(D, (1024, 512, 256)), name="out_proj")
    F = p["w_gate"].shape[1]
    h2, gate, up, act = carried(rms_swiglu, x1, p["norm2_g"], p["w_gate"], p["w_up"], tm=tall,
                                tn=_tile(F, (512, 256, 128)), name="rms_swiglu")
    x2 = carried(matmul_res, act, p["w_down"], x1, tm=tall, tn=_tile(D, (512, 256)), name="ffn_down")
    saved = dict(x=x, h=h, proj=proj, gains=gains, aq=aq, ak=ak, av=av, cq=cq, ckk=ckk, cvv=cvv, ols=ols, o_c=o_c,
                 l_c=l_c, sinks=sinks, mix=mix, x1=x1, h2=h2, gate=gate, up=up, act=act)
    return x2, saved


def _layer_bwd(dx2, dx2b, p, s, e, ride=_NoRide()):
    T, D = dx2.shape
    F = p["w_gate"].shape[1]
    tm = _tile(T, (512, 256, 128))
    tmm = _tile(T, (1024, 512, 256, 128))
    tkT = _tile(T, (2048, 1024, 512))
    tF = _tile(F, (512, 256, 128))
    tD = _tile(D, (1024, 512, 256))
    g = {}

    def carried(fn, *args, name, **kw):
        comm = ride.rider(name)
        out = fn(*args, name=name, comm=comm, **kw)
        ride.landed(comm)
        return out

    def big(n, val):
        g[n] = val
        ride.grad(n, val)

    d_gate, d_up = carried(nt_swiglu_bwd, dx2b, p["w_down"], s["gate"], s["up"], tm=tmm, tn=tF, name="ffn_down_bwd")
    big("w_down", tn_matmul(s["act"], dx2b, tm=tF, tn=tD, tk=tkT, name="grad_w_down"))
    big("w_gate", carried(tn_matmul, s["h2"], d_gate, tm=tD, tn=tF, tk=tkT, by_chip=True, name="grad_w_gate"))
    big("w_up", carried(tn_matmul, s["h2"], d_up, tm=tD, tn=tF, tk=tkT, by_chip=True, name="grad_w_up"))
    dx1, dx1b, g["norm2_g"] = carried(nt_rms_bwd, [(d_gate, p["w_gate"]), (d_up, p["w_up"])], s["x1"], p["norm2_g"],
                                      dx2, tm=tm, tn=_tile(D, (512, 256)), name="ffn_in_bwd")
    dmix = nt_plain(dx1b, p["w_out"], tm=tmm, tn=tD, name="out_proj_bwd")
    big("w_out", tn_matmul(s["mix"], dx1b, tm=1024, tn=tD, tk=tkT, name="grad_w_out"))
    dos = dil_combine_bwd(s["ols"], dmix, e, tm=tm, name="dil_combine_bwd")
    da = []
    for n, d in enumerate(DILATIONS):
        da += carried(band_attn_bwd, s["aq"], s["ak"], s["av"], s["ols"][2 * n + 1], dos[n], dos[3 + n], dil=d,
                      max_dist=A_DIST, group=1, name=f"dil_attn_bwd_{d}")
    du, dgt, gw, gb, glg, glb = carried(conv_bwd, s["proj"], dmix, p["conv_w"], p["conv_b"], p["conv_ln_g"],
                                        p["conv_ln_b"], tb=tm, name="conv_bwd")
    g["conv_w"], g["conv_b"], g["conv_ln_g"], g["conv_ln_b"] = gw[:CONV_K], gb, glg, glb
    do_c, dd_c, dsink = swa_pre_bwd(s["o_c"], s["l_c"], dmix, s["sinks"], e, tm=_tile(T, (256, 128)), name="swa_pre_bwd")
    g["c_sinks"] = dsink.reshape(-1, HD)[:, 0]
    dcq, dckk, dcvv = carried(band_attn_bwd, s["cq"], s["ckk"], s["cvv"], s["l_c"], do_c, dd_c, dil=1, max_dist=C_DIST,
                              group=4, name="swa_attn_bwd")
    dproj, gaq, gak, gcq, gck = prep_bwd(s["proj"], e, s["gains"], da, (dcq, dckk, dcvv), (du, dgt),
                                         tm=_tile(T, (256, 128)), name="prep_bwd")
    g["a_q_g"] = gaq.reshape(-1, HD).sum(0)
    g["a_k_g"] = gak.reshape(-1, HD).sum(0)
    g["c_q_g"] = gcq.reshape(-1, HD).sum(0)
    g["c_k_g"] = gck.reshape(-1, HD).sum(0)
    big("w_in", tn_matmul(s["h"], dproj, tm=tD, tn=_tile(IN_W, (1280,)), tk=tkT, name="grad_w_in"))
    dx, dxb, g["norm1_g"] = carried(nt_rms_bwd, [(dproj, p["w_in"])], s["x"], p["norm1_g"], dx1, tm=tmm,
                                    tn=_tile(D, (512, 256)), name="in_proj_bwd")
    return dx, dxb, g


BIG = ("w_in", "w_out", "w_gate", "w_up", "w_down")
COL_SHARDED = ("w_in", "w_gate", "w_up")
SMALL = ("norm1_g", "a_q_g", "a_k_g", "conv_w", "conv_b", "conv_ln_g", "conv_ln_b", "c_q_g", "c_k_g", "c_sinks", "norm2_g")


def _to_pieces(g, name):
    if g.ndim == 3:
        return g.reshape(4, 2, g.shape[1] // 2, g.shape[2])
    R, C = g.shape
    if name in COL_SHARDED:
        return g.reshape(2, R // 2, 4, C // 4).transpose(2, 0, 1, 3)
    return g.reshape(4, 2, R // 8, C)


def _from_gathered(w, name):
    _, _, r, c = w.shape
    if name in COL_SHARDED:
        return w.transpose(1, 2, 0, 3).reshape(2 * r, 4 * c)
    return w.reshape(8 * r, c)


def _shard(W, l, n):
    w = W[n][l]
    return w.astype(MXU).reshape(2, w.shape[0] // 2, w.shape[1])


class _LayerParams(dict):
    def __init__(self, layer, full, small):
        super().__init__(small)
        self.layer, self.full = layer, full

    def __missing__(self, n):
        return self.full[(self.layer, n)]


FWD_RIDES = {
    (0, "rms_proj"): ((0, "w_out"), (0, "w_gate")),
    (0, "swa_attn_fwd"): ((0, "w_up"),),
    (0, "rms_swiglu"): ((0, "w_down"), (1, "w_in"), (1, "w_out")),
    (0, "ffn_down"): ((1, "w_gate"),),
    (1, "rms_proj"): ((1, "w_up"),),
    (1, "rms_swiglu"): ((1, "w_down"),),
}
BWD_RIDES = {
    "grad_w_gate": (("w_down",), ()),
    "grad_w_up": (("w_gate",), ("w_down",)),
    "ffn_in_bwd": (("w_up",), ("w_gate",)),
    "dil_attn_bwd_1": (("w_out",), ()),
    "dil_attn_bwd_16": ((), ("w_up",)),
    "conv_bwd": ((), ("w_out",)),
    "in_proj_bwd": (("w_in",), ()),
}
IN_PLACE = ("w_gate", "w_up")


class _FwdRide:
    def __init__(self, layer, W, full):
        self.layer, self.W, self.full = layer, W, full

    def rider(self, name):
        keys = FWD_RIDES.get((self.layer, name))
        if not keys:
            return None
        comm = GatherComm([_shard(self.W, l, n) for l, n in keys], [n in IN_PLACE for _, n in keys])
        comm.keys = keys
        return comm

    def landed(self, comm):
        if comm is not None:
            for (l, n), g in zip(comm.keys, comm.results):
                self.full[(l, n)] = g if n in IN_PLACE else _from_gathered(g, n)


class _GradFlow:
    def __init__(self):
        self.pieces, self.sums, self.landed, self.pending = {}, {}, {}, []

    def swap(self, keys):
        comm = SwapComm([self.pieces[k] for k in keys])
        comm.keys, comm.kind = list(keys), "swap"
        return comm

    def exchange(self, keys):
        if not keys:
            return None
        comm = ExchangeComm([self.sums[k] for k in keys])
        comm.keys, comm.kind = list(keys), "exchange"
        return comm

    def take_pending(self):
        keys, self.pending = self.pending, []
        return keys

    def land(self, comm):
        if comm is None:
            return
        if isinstance(comm, MultiComm):
            for sub in comm.comms:
                self.land(sub)
            return
        for k, res in zip(comm.keys, comm.results):
            if comm.kind == "swap":
                r = res.shape[1]
                self.sums[k] = add_halves(self.pieces[k], res, tr=_tile(r, (256, 176, 128, 64, 32, 16)),
                                          name="grad_chip_sum")
                if k[1] == "w_in":
                    self.pending.append(k)
            else:
                self.landed[k] = res


class _BwdRide:
    def __init__(self, layer, flow):
        self.layer, self.flow = layer, flow
        self.final = layer == 0

    def grad(self, name, val):
        self.flow.pieces[(self.layer, name)] = _to_pieces(val, name)
        if self.final and name == "w_in":
            swap = self.flow.swap([(self.layer, name)])
            _run_comm(swap, name="swap_last")
            self.flow.land(swap)

    def rider(self, name):
        if name == "ffn_down_bwd" or (self.final and name == "in_proj_bwd"):
            return self.flow.exchange(self.flow.take_pending())
        swaps, exchanges = BWD_RIDES.get(name, ((), ()))
        comms = []
        if swaps:
            comms.append(self.flow.swap([(self.layer, n) for n in swaps]))
        if exchanges:
            comms.append(self.flow.exchange([(self.layer, n) for n in exchanges]))
        return MultiComm(comms) if comms else None

    def landed(self, comm):
        self.flow.land(comm)


def _pack(items, rows):
    flat = jnp.concatenate([a.reshape(-1).astype(f32) for a in items])
    return jnp.pad(flat, (0, rows * 128 - flat.shape[0])).reshape(rows, 128)


def _unpack(packed, shapes):
    flat = packed.reshape(-1)
    out, off = [], 0
    for shp in shapes:
        n = 1
        for d in shp:
            n *= d
        out.append(flat[off:off + n].reshape(shp))
        off += n
    return out


def kernel(x, norm1_g, w_in, a_q_g, a_k_g, conv_w, conv_b, conv_ln_g, conv_ln_b, c_q_g, c_k_g, c_sinks, w_out, norm2_g, w_gate, w_up, w_down, loss_target, m_norm1_g, m_w_in, m_a_q_g, m_a_k_g, m_conv_w, m_conv_b, m_conv_ln_g, m_conv_ln_b, m_c_q_g, m_c_k_g, m_c_sinks, m_w_out, m_norm2_g, m_w_gate, m_w_up, m_w_down, v_norm1_g, v_w_in, v_a_q_g, v_a_k_g, v_conv_w, v_conv_b, v_conv_ln_g, v_conv_ln_b, v_c_q_g, v_c_k_g, v_c_sinks, v_w_out, v_norm2_g, v_w_gate, v_w_up, v_w_down):
    W = dict(norm1_g=norm1_g, w_in=w_in, a_q_g=a_q_g, a_k_g=a_k_g, conv_w=conv_w, conv_b=conv_b, conv_ln_g=conv_ln_g,
             conv_ln_b=conv_ln_b, c_q_g=c_q_g, c_k_g=c_k_g, c_sinks=c_sinks, w_out=w_out, norm2_g=norm2_g, w_gate=w_gate,
             w_up=w_up, w_down=w_down)
    M = dict(norm1_g=m_norm1_g, w_in=m_w_in, a_q_g=m_a_q_g, a_k_g=m_a_k_g, conv_w=m_conv_w, conv_b=m_conv_b,
             conv_ln_g=m_conv_ln_g, conv_ln_b=m_conv_ln_b, c_q_g=m_c_q_g, c_k_g=m_c_k_g, c_sinks=m_c_sinks, w_out=m_w_out,
             norm2_g=m_norm2_g, w_gate=m_w_gate, w_up=m_w_up, w_down=m_w_down)
    V = dict(norm1_g=v_norm1_g, w_in=v_w_in, a_q_g=v_a_q_g, a_k_g=v_a_k_g, conv_w=v_conv_w, conv_b=v_conv_b,
             conv_ln_g=v_conv_ln_g, conv_ln_b=v_conv_ln_b, c_q_g=v_c_q_g, c_k_g=v_c_k_g, c_sinks=v_c_sinks, w_out=v_w_out,
             norm2_g=v_norm2_g, w_gate=v_w_gate, w_up=v_w_up, w_down=v_w_down)
    depth = norm1_g.shape[0]
    T, D = x.shape[1], x.shape[2]
    xs = x.reshape(T, D)
    chip = 2 * lax.axis_index("x") + lax.axis_index("y")
    e = _head_eye()

    full = {}
    first = GatherComm([_shard(W, 0, "w_in"), conv_w])
    _run_comm(first, name="gather_first")
    full[(0, "w_in")] = _from_gathered(first.results[0], "w_in")
    conv_full = first.results[1].transpose(1, 2, 0, 3).reshape(depth, CONV_K, B_W)
    params = []
    for l in range(depth):
        small = {n: W[n][l].reshape(1, -1) for n in SMALL if n != "conv_w"}
        small["conv_w"] = jnp.pad(conv_full[l], ((0, HALO - CONV_K), (0, 0)))
        params.append(_LayerParams(l, full, small))

    saved = []
    act = xs
    for l in range(depth):
        act, s = _layer_fwd(act, params[l], e, _FwdRide(l, W, full))
        saved.append(s)
    dy, dyb, loss_part = loss_head(act, loss_target.reshape(T, D), tm=_tile(T, (512, 256, 128)), name="loss_head")
    grads = [None] * depth
    flow = _GradFlow()
    for l in reversed(range(depth)):
        dy, dyb, grads[l] = _layer_bwd(dy, dyb, params[l], saved[l], e, _BwdRide(l, flow))
    grad_x = dy.reshape(x.shape)

    out = {}
    for n in ("w_down", "w_gate", "w_up", "w_out", "w_in"):
        last = flow.exchange(flow.take_pending())
        per_layer = [flow.landed[(l, n)] for l in range(depth)]
        r, cc = per_layer[0].shape[2], per_layer[0].shape[3]
        srcs = [jnp.stack([pl_[:, s].reshape(2 * r, cc) for pl_ in per_layer]) for s in range(4)]
        out[n] = adamw(W[n], M[n], V[n], srcs, tr=_tile(2 * r, (256, 176, 128, 64, 32, 16)), name="adamw_" + n,
                       comm=last)
        flow.land(last)

    small_shapes = []
    items = []
    for l in range(depth):
        for n in SMALL:
            a = grads[l][n]
            if n == "conv_w":
                a = a.reshape(CONV_K, 4, B_W // 4).transpose(1, 0, 2)
            items.append(a)
            small_shapes.append(a.shape)
    items.append(loss_part[0, 0:1])
    small_shapes.append((1,))
    total = sum(int(jnp.size(a)) for a in items)
    rows = -(-total // 1024) * 8
    summed = sum8(gather_small(_pack(items, rows), name="gather_small"), name="sum_small")
    parts = _unpack(summed, small_shapes)
    loss = parts[-1][0]
    small_g = {n: [] for n in SMALL}
    for l in range(depth):
        for i, n in enumerate(SMALL):
            a = parts[l * len(SMALL) + i]
            if n == "conv_w":
                a = lax.dynamic_index_in_dim(a, chip, axis=0, keepdims=False)
            small_g[n].append(a.reshape(W[n].shape[1:]))
    sw = [W[n] for n in SMALL]
    sm = [M[n] for n in SMALL]
    sv = [V[n] for n in SMALL]
    sg = [jnp.stack(small_g[n]) for n in SMALL]
    tot2 = sum(int(jnp.size(a)) for a in sw)
    rows2 = -(-tot2 // 1024) * 8
    res = adamw(_pack(sw, rows2)[None], _pack(sm, rows2)[None], _pack(sv, rows2)[None], [_pack(sg, rows2)[None]],
                tr=rows2, name="adamw_small")
    shapes2 = [a.shape for a in sw]
    small_out = [_unpack(r[0], shapes2) for r in res]
    for i, n in enumerate(SMALL):
        out[n] = [small_out[k][i] for k in range(4)]

    order = ("norm1_g", "w_in", "a_q_g", "a_k_g", "conv_w", "conv_b", "conv_ln_g", "conv_ln_b", "c_q_g", "c_k_g",
             "c_sinks", "w_out", "norm2_g", "w_gate", "w_up", "w_down")
    return (loss, grad_x, *[out[n][0] for n in order], *[out[n][1] for n in order], *[out[n][2] for n in order],
            *[out[n][3] for n in order])
```

```python
import functools

import jax
import jax.numpy as jnp
from jax import lax
from jax.experimental import pallas as pl
from jax.experimental.pallas import tpu as pltpu

f32 = jnp.float32
MXU = jnp.bfloat16
S = jax.ShapeDtypeStruct
MESH = pl.DeviceIdType.MESH

EPS = 1e-6
NEG = -1e30
HD = 64
BLK = 128
A_W, B_W, C_W = 512, 512, 1024
KV_W = 128
IN_W = 3 * A_W + 2 * B_W + C_W + 2 * KV_W
CONV_K = 31
HALO = 32
DILATIONS = (1, 4, 16)
A_DIST, C_DIST = 128, 127
SCALE = HD ** -0.5
VMEM_LIMIT = 56 * 1024 * 1024
VMEM_TALL = 62 * 1024 * 1024

ADAM_LR, ADAM_B1, ADAM_B2, ADAM_EPS, ADAM_WD, ADAM_STEP = 0.001, 0.9, 0.999, 1e-08, 0.01, 10


def _cp(*sem, vmem=VMEM_LIMIT):
    return pltpu.CompilerParams(dimension_semantics=sem, vmem_limit_bytes=vmem)


ANY = pl.BlockSpec(memory_space=pl.ANY)


def _pallas(body, *, comm=None, name, grid, in_specs, out_specs, out_shape, scratch_shapes=(), compiler_params,
            input_output_aliases=None):
    aliases = dict(input_output_aliases or {})
    if comm is None:
        return pl.pallas_call(body, name=name, grid=grid, in_specs=in_specs, out_specs=out_specs, out_shape=out_shape,
                              scratch_shapes=list(scratch_shapes), compiler_params=compiler_params,
                              input_output_aliases=aliases)
    single = not isinstance(out_shape, (list, tuple))
    o_shapes = [out_shape] if single else list(out_shape)
    o_specs = [out_specs] if single else list(out_specs)
    n_in, n_out, n_sc = len(in_specs), len(o_shapes), len(scratch_shapes)
    nci, nco = len(comm.ins), len(comm.out_shapes)
    total = 1
    for g in grid:
        total *= g

    def carried(*refs):
        ins, cins = refs[:n_in], refs[n_in:n_in + nci]
        o0 = n_in + nci
        outs, couts = refs[o0:o0 + n_out], refs[o0 + n_out:o0 + n_out + nco]
        s0 = o0 + n_out + nco
        scratch, sems = refs[s0:s0 + n_sc], refs[s0 + n_sc:]
        step = pl.program_id(0)
        for axis in range(1, len(grid)):
            step = step * grid[axis] + pl.program_id(axis)

        @pl.when(step == 0)
        def _():
            comm.start(cins, couts, sems)

        body(*ins, *outs, *scratch)

        @pl.when(step == (3 * total) // 4)
        def _():
            comm.mid(cins, couts, sems)

        @pl.when(step == total - 1)
        def _():
            comm.finish(cins, couts, sems)

    call = pl.pallas_call(carried, name=name, grid=grid, in_specs=list(in_specs) + [ANY] * nci,
                          out_specs=o_specs + [ANY] * nco, out_shape=o_shapes + list(comm.out_shapes),
                          scratch_shapes=list(scratch_shapes) + list(comm.sem_shapes), compiler_params=compiler_params,
                          input_output_aliases=aliases)

    def run(*args):
        res = call(*args, *comm.ins)
        comm.results = list(res[n_out:])
        return res[0] if single else list(res[:n_out])

    return run


class MultiComm:
    def __init__(self, comms):
        self.comms = list(comms)
        self.ins = [a for c in self.comms for a in c.ins]
        self.out_shapes = [s for c in self.comms for s in c.out_shapes]
        self.sem_shapes = [s for c in self.comms for s in c.sem_shapes]

    def _each(self, cins, couts, sems):
        i = o = s = 0
        for c in self.comms:
            ni, no, ns = len(c.ins), len(c.out_shapes), len(c.sem_shapes)
            yield c, cins[i:i + ni], couts[o:o + no], sems[s:s + ns]
            i, o, s = i + ni, o + no, s + ns

    def start(self, cins, couts, sems):
        for c, a, b, d in self._each(cins, couts, sems):
            c.start(a, b, d)

    def mid(self, cins, couts, sems):
        for c, a, b, d in self._each(cins, couts, sems):
            c.mid(a, b, d)

    def finish(self, cins, couts, sems):
        for c, a, b, d in self._each(cins, couts, sems):
            c.finish(a, b, d)

    @property
    def results(self):
        return [r for c in self.comms for r in c.results]

    @results.setter
    def results(self, vals):
        o = 0
        for c in self.comms:
            c.results = list(vals[o:o + len(c.out_shapes)])
            o += len(c.out_shapes)


def _run_comm(comm, *, name):
    nci, nco = len(comm.ins), len(comm.out_shapes)

    def body(*refs):
        cins, couts, sems = refs[:nci], refs[nci:nci + nco], refs[nci + nco:]
        comm.start(cins, couts, sems)
        comm.mid(cins, couts, sems)
        comm.finish(cins, couts, sems)

    comm.results = list(pl.pallas_call(body, name=name, in_specs=[ANY] * nci, out_specs=[ANY] * nco,
                                       out_shape=list(comm.out_shapes), scratch_shapes=list(comm.sem_shapes))(*comm.ins))


def _nt(a, b):
    return lax.dot_general(a, b, (((1,), (1,)), ((), ())), preferred_element_type=f32)


def _tn(a, b):
    return lax.dot_general(a, b, (((0,), (0,)), ((), ())), preferred_element_type=f32)


def _nn(a, b):
    return jnp.dot(a, b, preferred_element_type=f32)


def _sigmoid(x):
    return 1.0 / (1.0 + jnp.exp(-x))


def _seg_sum(v, e_ref):
    hi = v.astype(jnp.bfloat16)
    lo = (v - hi.astype(f32)).astype(jnp.bfloat16)
    e = e_ref[...]
    return _nn(hi, e) + _nn(lo, e)


def _seg_sum128(v, e_ref):
    e = e_ref[0:128, 0:128]
    hi = v.astype(jnp.bfloat16)
    lo = (v - hi.astype(f32)).astype(jnp.bfloat16)
    return _nn(hi, e) + _nn(lo, e)


def _head_eye():
    r = lax.broadcasted_iota(jnp.int32, (512, 512), 0) // HD
    c = lax.broadcasted_iota(jnp.int32, (512, 512), 1) // HD
    return (r == c).astype(jnp.bfloat16)


def _rms_norm_rows(x_ref, g_ref, h_ref, tm):
    def chunk(c, carry):
        rows = pl.ds(c * BLK, BLK)
        xf = x_ref[rows, :]
        r = lax.rsqrt(jnp.mean(xf * xf, axis=-1, keepdims=True) + EPS)
        h_ref[rows, :] = (xf * r * g_ref[...]).astype(MXU)
        return carry
    lax.fori_loop(0, tm // BLK, chunk, 0)


def rms_proj(x, g, w, *, tm, tn, name, comm=None):
    T, D = x.shape
    N = w.shape[1]

    def body(x_ref, g_ref, w_ref, h_ref, o_ref):
        @pl.when(pl.program_id(1) == 0)
        def _():
            _rms_norm_rows(x_ref, g_ref, h_ref, tm)
        o_ref[...] = _nn(h_ref[...], w_ref[...])

    return _pallas(
        body, comm=comm, name=name, grid=(T // tm, N // tn),
        in_specs=[pl.BlockSpec((tm, D), lambda i, j: (i, 0)), pl.BlockSpec((1, D), lambda i, j: (0, 0)),
                  pl.BlockSpec((D, tn), lambda i, j: (0, j))],
        out_specs=[pl.BlockSpec((tm, D), lambda i, j: (i, 0)), pl.BlockSpec((tm, tn), lambda i, j: (i, j))],
        out_shape=[S((T, D), MXU), S((T, N), f32)],
        compiler_params=_cp("arbitrary", "arbitrary"),
    )(x, g, w)


def rms_swiglu(x, g, wg, wu, *, tm, tn, name, comm=None):
    T, D = x.shape
    N = wg.shape[1]

    def body(x_ref, g_ref, wg_ref, wu_ref, h_ref, gate_ref, up_ref, act_ref):
        @pl.when(pl.program_id(1) == 0)
        def _():
            _rms_norm_rows(x_ref, g_ref, h_ref, tm)
        h = h_ref[...]
        gate = _nn(h, wg_ref[...])
        up = _nn(h, wu_ref[...])
        gate_ref[...] = gate
        up_ref[...] = up
        act_ref[...] = (gate * _sigmoid(gate) * up).astype(MXU)

    wspec = pl.BlockSpec((D, tn), lambda i, j: (0, j))
    ospec = pl.BlockSpec((tm, tn), lambda i, j: (i, j))
    return _pallas(
        body, comm=comm, name=name, grid=(T // tm, N // tn),
        in_specs=[pl.BlockSpec((tm, D), lambda i, j: (i, 0)), pl.BlockSpec((1, D), lambda i, j: (0, 0)), wspec, wspec],
        out_specs=[pl.BlockSpec((tm, D), lambda i, j: (i, 0)), ospec, ospec, ospec],
        out_shape=[S((T, D), MXU), S((T, N), f32), S((T, N), f32), S((T, N), MXU)],
        compiler_params=_cp("arbitrary", "arbitrary"),
    )(x, g, wg, wu)


def matmul_res(a, w, res, *, tm, tn, name, comm=None):
    T, K = a.shape
    N = w.shape[1]

    def body(a_ref, w_ref, r_ref, o_ref):
        o_ref[...] = r_ref[...] + _nn(a_ref[...], w_ref[...])

    return _pallas(
        body, comm=comm, name=name, grid=(T // tm, N // tn),
        in_specs=[pl.BlockSpec((tm, K), lambda i, j: (i, 0)), pl.BlockSpec((K, tn), lambda i, j: (0, j)),
                  pl.BlockSpec((tm, tn), lambda i, j: (i, j))],
        out_specs=pl.BlockSpec((tm, tn), lambda i, j: (i, j)),
        out_shape=S((T, N), f32),
        compiler_params=_cp("arbitrary", "arbitrary"),
    )(a, w, res)


def nt_plain(a, w, *, tm, tn, name, comm=None):
    T, K = a.shape
    N = w.shape[0]

    def body(a_ref, w_ref, o_ref):
        o_ref[...] = _nt(a_ref[...], w_ref[...])

    return _pallas(
        body, comm=comm, name=name, grid=(T // tm, N // tn),
        in_specs=[pl.BlockSpec((tm, K), lambda i, j: (i, 0)), pl.BlockSpec((tn, K), lambda i, j: (j, 0))],
        out_specs=pl.BlockSpec((tm, tn), lambda i, j: (i, j)),
        out_shape=S((T, N), f32),
        compiler_params=_cp("arbitrary", "arbitrary"),
    )(a, w)


def nt_swiglu_bwd(dy, wd, gate, up, *, tm, tn, name, comm=None):
    T, D = dy.shape
    F = wd.shape[0]

    def body(dy_ref, w_ref, g_ref, u_ref, dg_ref, du_ref):
        d_act = _nt(dy_ref[...], w_ref[...])
        g = g_ref[...]
        sg = _sigmoid(g)
        du_ref[...] = (d_act * (g * sg)).astype(MXU)
        dg_ref[...] = (d_act * u_ref[...] * (sg * (1.0 + g * (1.0 - sg)))).astype(MXU)

    blk = pl.BlockSpec((tm, tn), lambda i, j: (i, j))
    return _pallas(
        body, comm=comm, name=name, grid=(T // tm, F // tn),
        in_specs=[pl.BlockSpec((tm, D), lambda i, j: (i, 0)), pl.BlockSpec((tn, D), lambda i, j: (j, 0)), blk, blk],
        out_specs=[blk, blk],
        out_shape=[S((T, F), MXU), S((T, F), MXU)],
        compiler_params=_cp("arbitrary", "arbitrary"),
    )(dy, wd, gate, up)


def nt_rms_bwd(terms, x, g, dres, *, tm, tn, name, comm=None):
    T, D = x.shape
    K = terms[0][0].shape[1]
    nj = D // tn
    rc = tm // nj
    nt = len(terms)
    ni = T // tm

    def body(*refs):
        a_refs = refs[0:2 * nt:2]
        w_refs = refs[1:2 * nt:2]
        x_ref, g_ref, r_ref, dx_ref, dxb_ref, dg_ref, acc_ref = refs[2 * nt:]
        i, j = pl.program_id(0), pl.program_id(1)

        @pl.when(i < ni)
        def _():
            part = _nt(a_refs[0][...], w_refs[0][...])
            for t in range(1, nt):
                part += _nt(a_refs[t][...], w_refs[t][...])
            acc_ref[i % 2, j] = part

        @pl.when(i > 0)
        def _():
            rows = pl.ds(pl.multiple_of(j * rc, BLK), rc)
            dh = jnp.concatenate([acc_ref[(i - 1) % 2, jj, rows, :] for jj in range(nj)], axis=1)
            xf = x_ref[...]
            r = lax.rsqrt(jnp.mean(xf * xf, axis=-1, keepdims=True) + EPS)
            y = xf * r
            dy = dh * g_ref[...]
            dx = r_ref[...] + r * (dy - y * jnp.mean(dy * y, axis=-1, keepdims=True))
            dx_ref[...] = dx
            dxb_ref[...] = dx.astype(MXU)
            dgain = jnp.sum(dh * y, axis=0, keepdims=True)
            first = jnp.logical_and(i == 1, j == 0)

            @pl.when(first)
            def _():
                dg_ref[...] = dgain

            @pl.when(jnp.logical_not(first))
            def _():
                dg_ref[...] += dgain

    in_specs, args = [], []
    for a, w in terms:
        in_specs += [pl.BlockSpec((tm, K), lambda i, j: (jnp.minimum(i, ni - 1), 0)),
                     pl.BlockSpec((tn, K), lambda i, j: (jnp.where(i < ni, j, nj - 1), 0))]
        args += [a, w]
    row = pl.BlockSpec((rc, D), lambda i, j: (jnp.where(i > 0, (i - 1) * nj + j, 0), 0))
    vec = pl.BlockSpec((1, D), lambda i, j: (0, 0))
    in_specs += [row, vec, row]
    return _pallas(
        body, comm=comm, name=name, grid=(ni + 1, nj), in_specs=in_specs,
        out_specs=[row, row, vec],
        out_shape=[S((T, D), f32), S((T, D), MXU), S((1, D), f32)],
        scratch_shapes=[pltpu.VMEM((2, nj, tm, tn), f32)],
        compiler_params=_cp("arbitrary", "arbitrary", vmem=VMEM_TALL),
    )(*args, x, g, dres)


def tn_matmul(a, b, *, tm, tn, tk, name, by_chip=False, comm=None):
    T, M = a.shape
    N = b.shape[1]
    if by_chip:
        tn = N // 4
        out_spec = pl.BlockSpec((None, tm, tn), lambda i, j, k: (j, i, 0))
        out_shape = S((4, M, tn), f32)
    else:
        out_spec = pl.BlockSpec((tm, tn), lambda i, j, k: (i, j))
        out_shape = S((M, N), f32)

    def body(a_ref, b_ref, o_ref):
        part = _tn(a_ref[...], b_ref[...])

        @pl.when(pl.program_id(2) == 0)
        def _():
            o_ref[...] = part

        @pl.when(pl.program_id(2) > 0)
        def _():
            o_ref[...] += part

    return _pallas(
        body, comm=comm, name=name, grid=(M // tm, N // tn, T // tk),
        in_specs=[pl.BlockSpec((tk, tm), lambda i, j, k: (k, i)), pl.BlockSpec((tk, tn), lambda i, j, k: (k, j))],
        out_specs=out_spec, out_shape=out_shape,
        compiler_params=_cp("arbitrary", "arbitrary", "arbitrary"),
    )(a, b)


def loss_head(y, target, *, tm, name):
    T, D = y.shape
    ni = T // tm

    def body(y_ref, t_ref, dy_ref, dyb_ref, l_ref, acc_ref):
        i = pl.program_id(0)
        e = y_ref[...] - t_ref[...]
        dy = e * (1.0 / D)
        dy_ref[...] = dy
        dyb_ref[...] = dy.astype(MXU)
        part = jnp.sum(e * e, axis=0, keepdims=True)

        @pl.when(i == 0)
        def _():
            acc_ref[...] = part

        @pl.when(i > 0)
        def _():
            acc_ref[...] += part

        @pl.when(i == ni - 1)
        def _():
            tot = jnp.sum(acc_ref[...], axis=1, keepdims=True) * (0.5 / D)
            l_ref[...] = jnp.broadcast_to(tot, (1, 128))

    row = pl.BlockSpec((tm, D), lambda i: (i, 0))
    return pl.pallas_call(
        body, name=name, grid=(ni,), in_specs=[row, row],
        out_specs=[row, row, pl.BlockSpec((1, 128), lambda i: (0, 0))],
        out_shape=[S((T, D), f32), S((T, D), MXU), S((1, 128), f32)],
        scratch_shapes=[pltpu.VMEM((1, D), f32)],
        compiler_params=_cp("arbitrary"),
    )(y, target)


def _qk_norm(v, gain, e_ref):
    r = lax.rsqrt(_seg_sum(v * v, e_ref) * (1.0 / HD) + EPS)
    return v * r * gain


def _dup_halves(pair):
    rolled = pltpu.roll(pair, HD, 1)
    lo = lax.broadcasted_iota(jnp.int32, pair.shape, 1) < HD
    return jnp.where(lo, pair, rolled), jnp.where(lo, rolled, pair)


def prep_fwd(proj, e, gains, *, tm, name):
    T = proj.shape[0]

    def body(p_ref, e_ref, gaq, gak, gcq, gck, aq, ak, av, cq, ckk, cvv):
        aq[...] = _qk_norm(p_ref[:, 0:512], gaq[...], e_ref)
        ak[...] = _qk_norm(p_ref[:, 512:1024], gak[...], e_ref)
        av[...] = p_ref[:, 1024:1536]
        cq[:, 0:512] = _qk_norm(p_ref[:, 2560:3072], gcq[...], e_ref).astype(MXU)
        cq[:, 512:1024] = _qk_norm(p_ref[:, 3072:3584], gcq[...], e_ref).astype(MXU)
        kraw = p_ref[:, 3584:3712]
        kn = kraw * lax.rsqrt(_seg_sum128(kraw * kraw, e_ref) * (1.0 / HD) + EPS) * gck[...]
        k0, k1 = _dup_halves(kn)
        ckk[:, 0:128] = k0.astype(MXU)
        ckk[:, 128:256] = k1.astype(MXU)
        v0, v1 = _dup_halves(p_ref[:, 3712:3840])
        cvv[:, 0:128] = v0.astype(MXU)
        cvv[:, 128:256] = v1.astype(MXU)

    def vec(n):
        return pl.BlockSpec((1, n), lambda i: (0, 0))

    def rows(n):
        return pl.BlockSpec((tm, n), lambda i: (i, 0))

    return pl.pallas_call(
        body, name=name, grid=(T // tm,),
        in_specs=[rows(IN_W), pl.BlockSpec((512, 512), lambda i: (0, 0)), vec(512), vec(512), vec(512), vec(128)],
        out_specs=[rows(512), rows(512), rows(512), rows(1024), rows(256), rows(256)],
        out_shape=[S((T, 512), f32)] * 3 + [S((T, 1024), MXU), S((T, 256), MXU), S((T, 256), MXU)],
        compiler_params=_cp("arbitrary"),
    )(proj, e, *gains)


def _band_mask(max_dist, shut):
    r = lax.broadcasted_iota(jnp.int32, (2 * BLK, 2 * BLK), 0) & (BLK - 1)
    c = lax.broadcasted_iota(jnp.int32, (2 * BLK, 2 * BLK), 1)
    prev = jnp.logical_and(c < BLK, c >= r + (BLK - max_dist) + shut)
    return jnp.logical_or(prev, jnp.logical_and(c >= BLK, c - BLK <= r))


def _prev_mask(max_dist, shut):
    r = lax.broadcasted_iota(jnp.int32, (2 * BLK, BLK), 0) & (BLK - 1)
    c = lax.broadcasted_iota(jnp.int32, (2 * BLK, BLK), 1)
    return c >= r + (BLK - max_dist) + shut


def _head_masks():
    lo = (lax.broadcasted_iota(jnp.int32, (BLK, BLK), 1) < HD).astype(f32)
    return lo.astype(MXU), (1.0 - lo).astype(MXU)


def _stack_heads(x, hm):
    return jnp.concatenate([x * hm[0], x * hm[1]], axis=0)


def _unstack_heads(y, lane_lo):
    return jnp.where(lane_lo, y[0:BLK], y[BLK:2 * BLK])


def _rows(ref, start, dil):
    if dil == 1:
        return ref[pl.ds(start, BLK), :]
    return ref[pl.ds(start, BLK, stride=dil), :]


def _set_rows(ref, start, dil, val):
    if dil == 1:
        ref[pl.ds(start, BLK), :] = val
    else:
        ref[pl.ds(start, BLK, stride=dil), :] = val


def _attn_geometry(T, dil):
    span = BLK * dil
    n = max(1, 512 // span)
    return span, n, T // (span * n)


def band_attn_fwd(q, k, v, sinks, *, dil, max_dist, group, name, mix=None, comm=None):
    T = q.shape[0]
    P = q.shape[1] // BLK
    span, n, nb = _attn_geometry(T, dil)

    def body(*refs):
        s_ref = m_ref = None
        q_ref, kc_ref, kp_ref, vc_ref, vp_ref = refs[:5]
        rest = list(refs[5:])
        if sinks is not None:
            s_ref = rest.pop(0)
        if mix is not None:
            rest.pop(0)
            o_ref, l_ref, m_ref = rest
        else:
            o_ref, l_ref = rest
        b = pl.program_id(0)
        mask = _band_mask(max_dist, 0)
        mask0 = _band_mask(max_dist, jnp.where(b > 0, 0, BLK + 1))
        lane_lo = lax.broadcasted_iota(jnp.int32, (BLK, BLK), 1) < HD
        hm = _head_masks()
        if sinks is not None:
            row_lo = lax.broadcasted_iota(jnp.int32, (1, BLK), 1) < HD
            sk0 = jnp.max(jnp.where(row_lo, s_ref[...], NEG), axis=1, keepdims=True)
            sk1 = jnp.max(jnp.where(row_lo, NEG, s_ref[...]), axis=1, keepdims=True)
            sk = jnp.where(lax.broadcasted_iota(jnp.int32, (2 * BLK, 1), 0) < BLK, sk0, sk1)

        def load(r, sub):
            at = r + sub * span
            kc, vc = _rows(kc_ref, at, dil).astype(MXU), _rows(vc_ref, at, dil).astype(MXU)
            if sub == 0:
                kp, vp = _rows(kp_ref, r, dil).astype(MXU), _rows(vp_ref, r, dil).astype(MXU)
            else:
                kp, vp = _rows(kc_ref, at - span, dil).astype(MXU), _rows(vc_ref, at - span, dil).astype(MXU)
            qst = _stack_heads(_rows(q_ref, at, dil).astype(MXU), hm)
            return (qst, jnp.concatenate([kp, kc], axis=0), jnp.concatenate([vp, vc], axis=0),
                    mask0 if sub == 0 else mask, at)

        def attend(items):
            ss = [jnp.where(m_, _nt(qst, kcat) * SCALE, NEG) for qst, kcat, _, m_, _ in items]
            ms = [jnp.max(s, axis=1, keepdims=True) for s in ss]
            if sinks is not None:
                ms = [jnp.maximum(m, sk) for m in ms]
            ps = [jnp.exp(s - m) for s, m in zip(ss, ms)]
            dens = [jnp.sum(p_, axis=1, keepdims=True) for p_ in ps]
            if sinks is not None:
                dens = [d + jnp.exp(sk - m) for d, m in zip(dens, ms)]
            outs = [_nn(p_.astype(MXU), it[2]) / d for p_, it, d in zip(ps, items, dens)]
            for it, o, m, d in zip(items, outs, ms, dens):
                lse = m + jnp.log(d)
                if m_ref is not None:
                    _set_rows(m_ref, it[4], dil, _unstack_heads(o, lane_lo).astype(MXU))
                _set_rows(o_ref, it[4], dil, _unstack_heads(o, lane_lo))
                _set_rows(l_ref, it[4], dil, jnp.where(lane_lo, lse[0:BLK], lse[BLK:2 * BLK]))

        if dil * n <= 4:
            work = [(r, sub) for r in range(dil) for sub in range(n)]
            for g in range(0, len(work), 2):
                attend([load(*w) for w in work[g:g + 2]])
        else:
            def two_streams(i, carry):
                attend([load(2 * i, 0), load(2 * i + 1, 0)])
                return carry
            lax.fori_loop(0, dil // 2, two_streams, 0)

    rows_per_step = span * n
    qspec = pl.BlockSpec((rows_per_step, BLK), lambda b, p: (b, p))
    cur = pl.BlockSpec((rows_per_step, BLK), lambda b, p: (b, p // group))
    prev = pl.BlockSpec((span, BLK), lambda b, p: (jnp.maximum(b * n - 1, 0), p // group))
    in_specs = [qspec, cur, prev, cur, prev]
    args = [q, k, k, v, v]
    if sinks is not None:
        in_specs.append(pl.BlockSpec((1, BLK), lambda b, p: (0, p)))
        args.append(sinks)
    out_specs, out_shape, aliases = [qspec, qspec], [S(q.shape, f32), S(q.shape, f32)], {}
    if mix is not None:
        first_block = mix.shape[1] // BLK - P
        aliases = {len(args): 2}
        in_specs.append(ANY)
        args.append(mix)
        out_specs.append(pl.BlockSpec((rows_per_step, BLK), lambda b, p: (b, first_block + p)))
        out_shape.append(S(mix.shape, mix.dtype))
    return _pallas(
        body, comm=comm, name=name, grid=(nb, P), in_specs=in_specs, out_specs=out_specs, out_shape=out_shape,
        compiler_params=_cp("arbitrary", "arbitrary"), input_output_aliases=aliases,
    )(*args)


def band_attn_bwd(q, k, v, lse, do, dd, *, dil, max_dist, group, name, comm=None):
    T = q.shape[0]
    P = q.shape[1] // BLK
    span, n, nb = _attn_geometry(T, dil)
    assert group == 1 or dil == 1

    def body(q_ref, qn_ref, do_ref, don_ref, l_ref, ln_ref, d_ref, dn_ref, kc_ref, kp_ref, vc_ref, vp_ref,
             dq_ref, dk_ref, dv_ref):
        b, p = pl.program_id(0), pl.program_id(1)
        mask = _band_mask(max_dist, 0)
        mask0 = _band_mask(max_dist, jnp.where(b > 0, 0, BLK + 1))
        tail = _prev_mask(max_dist, jnp.where(b < nb - 1, 0, BLK + 1))
        lane_lo = lax.broadcasted_iota(jnp.int32, (BLK, BLK), 1) < HD
        hm = _head_masks()
        own_lanes = (lax.broadcasted_iota(jnp.int32, (2 * BLK, BLK), 1) < HD) == (
            lax.broadcasted_iota(jnp.int32, (2 * BLK, BLK), 0) < BLK)

        def per_row(x):
            return jnp.max(jnp.where(own_lanes, jnp.concatenate([x, x], axis=0), NEG), axis=1, keepdims=True)

        def q_side(refs, at):
            q_r, do_r, l_r, d_r = refs
            return (_stack_heads(_rows(q_r, at, dil).astype(MXU), hm), _stack_heads(_rows(do_r, at, dil).astype(MXU), hm),
                    per_row(_rows(l_r, at, dil)), per_row(_rows(d_r, at, dil)))

        def kv(ref, at):
            return _rows(ref, at, dil).astype(MXU)

        first = p % group == 0

        def put_kv(ref, at, val):
            if group == 1:
                _set_rows(ref, at, dil, val)
            else:
                @pl.when(first)
                def _():
                    ref[pl.ds(at, BLK), :] = val

                @pl.when(jnp.logical_not(first))
                def _():
                    ref[pl.ds(at, BLK), :] += val

        def stream(r):
            dks, dvs = [None] * n, [None] * n
            for sub in range(n):
                at = r + sub * span
                qst, dost, lrow, drow = q_side((q_ref, do_ref, l_ref, d_ref), at)
                if sub == 0:
                    kp, vp, m_ = kv(kp_ref, r), kv(vp_ref, r), mask0
                else:
                    kp, vp, m_ = kv(kc_ref, at - span), kv(vc_ref, at - span), mask
                kcat = jnp.concatenate([kp, kv(kc_ref, at)], axis=0)
                vcat = jnp.concatenate([vp, kv(vc_ref, at)], axis=0)
                pr = jnp.where(m_, jnp.exp(_nt(qst, kcat) * SCALE - lrow), 0.0)
                ds = (pr * (_nt(dost, vcat) - drow) * SCALE).astype(MXU)
                prb = pr.astype(MXU)
                _set_rows(dq_ref, at, dil, _unstack_heads(_nn(ds, kcat), lane_lo))
                if sub == 0:
                    dks[0] = _tn(ds[:, BLK:], qst)
                    dvs[0] = _tn(prb[:, BLK:], dost)
                else:
                    dkk, dvv = _tn(ds, qst), _tn(prb, dost)
                    dks[sub - 1] += dkk[0:BLK]
                    dvs[sub - 1] += dvv[0:BLK]
                    dks[sub], dvs[sub] = dkk[BLK:], dvv[BLK:]
            at = r + (n - 1) * span
            qst, dost, lrow, drow = q_side((qn_ref, don_ref, ln_ref, dn_ref), r)
            pr = jnp.where(tail, jnp.exp(_nt(qst, kv(kc_ref, at)) * SCALE - lrow), 0.0)
            ds = (pr * (_nt(dost, kv(vc_ref, at)) - drow) * SCALE).astype(MXU)
            dks[n - 1] += _tn(ds, qst)
            dvs[n - 1] += _tn(pr.astype(MXU), dost)
            for sub in range(n):
                put_kv(dk_ref, r + sub * span, dks[sub])
                put_kv(dv_ref, r + sub * span, dvs[sub])

        if dil <= 4:
            for r in range(dil):
                stream(r)
        else:
            def two_streams(i, carry):
                stream(2 * i)
                stream(2 * i + 1)
                return carry
            lax.fori_loop(0, dil // 2, two_streams, 0)

    rows_per_step = span * n
    qspec = pl.BlockSpec((rows_per_step, BLK), lambda b, p: (b, p))
    qnext = pl.BlockSpec((span, BLK), lambda b, p: (jnp.minimum((b + 1) * n, T // span - 1), p))
    cur = pl.BlockSpec((rows_per_step, BLK), lambda b, p: (b, p // group))
    prev = pl.BlockSpec((span, BLK), lambda b, p: (jnp.maximum(b * n - 1, 0), p // group))
    return _pallas(
        body, comm=comm, name=name, grid=(nb, P),
        in_specs=[qspec, qnext, qspec, qnext, qspec, qnext, qspec, qnext, cur, prev, cur, prev],
        out_specs=[qspec, cur, cur],
        out_shape=[S(q.shape, f32), S(k.shape, f32), S(k.shape, f32)],
        compiler_params=_cp("arbitrary", "arbitrary"),
    )(q, q, do, do, lse, lse, dd, dd, k, k, v, v)


def dil_combine_fwd(ols, *, tm, name):
    T = ols[0].shape[0]

    def body(o1, l1, o2, l2, o3, l3, out_ref):
        a, b, c = l1[...], l2[...], l3[...]
        m = jnp.maximum(jnp.maximum(a, b), c)
        ea, eb, ec = jnp.exp(a - m), jnp.exp(b - m), jnp.exp(c - m)
        out = (ea * o1[...] + eb * o2[...] + ec * o3[...]) / (ea + eb + ec)
        out_ref[...] = out.astype(MXU)

    row = pl.BlockSpec((tm, 512), lambda i: (i, 0))
    return pl.pallas_call(body, name=name, grid=(T // tm,), in_specs=[row] * 6, out_specs=row,
                          out_shape=S((T, A_W + B_W + C_W), MXU), compiler_params=_cp("arbitrary"))(*ols)


def dil_combine_bwd(ols, dmix, e, *, tm, name):
    T = ols[0].shape[0]

    def body(o1, l1, o2, l2, o3, l3, d_ref, e_ref, do1, do2, do3, dd1, dd2, dd3):
        a, b, c = l1[...], l2[...], l3[...]
        m = jnp.maximum(jnp.maximum(a, b), c)
        ea, eb, ec = jnp.exp(a - m), jnp.exp(b - m), jnp.exp(c - m)
        inv = 1.0 / (ea + eb + ec)
        wa, wb, wc = ea * inv, eb * inv, ec * inv
        dout = d_ref[...]
        gbar = _seg_sum(dout * (wa * o1[...] + wb * o2[...] + wc * o3[...]), e_ref)
        do1[...] = wa * dout
        do2[...] = wb * dout
        do3[...] = wc * dout
        dd1[...] = wa * gbar
        dd2[...] = wb * gbar
        dd3[...] = wc * gbar

    row = pl.BlockSpec((tm, 512), lambda i: (i, 0))
    return pl.pallas_call(
        body, name=name, grid=(T // tm,),
        in_specs=[row] * 6 + [row, pl.BlockSpec((512, 512), lambda i: (0, 0))],
        out_specs=[row] * 6,
        out_shape=[S((T, 512), f32)] * 6,
        compiler_params=_cp("arbitrary"),
    )(*ols, dmix, e)


def swa_pre_bwd(o, lse, dmix, sinks, e, *, tm, name):
    T = o.shape[0]
    ni = T // tm

    def body(o_ref, l_ref, d_ref, s_ref, e_ref, do_ref, dd_ref, ds_ref):
        i = pl.program_id(0)
        dout = d_ref[...]
        do_ref[...] = dout.astype(MXU)
        prod = dout * o_ref[...]
        dd = jnp.concatenate([_seg_sum(prod[:, 0:512], e_ref), _seg_sum(prod[:, 512:1024], e_ref)], axis=1)
        dd_ref[...] = dd
        part = -jnp.sum(jnp.exp(s_ref[...] - l_ref[...]) * dd, axis=0, keepdims=True)

        @pl.when(i == 0)
        def _():
            ds_ref[...] = part

        @pl.when(i > 0)
        def _():
            ds_ref[...] += part

    row = pl.BlockSpec((tm, 1024), lambda i: (i, 0))
    vec = pl.BlockSpec((1, 1024), lambda i: (0, 0))
    return pl.pallas_call(
        body, name=name, grid=(ni,),
        in_specs=[row, row, pl.BlockSpec((tm, 1024), lambda i: (i, 1)), vec, pl.BlockSpec((512, 512), lambda i: (0, 0))],
        out_specs=[row, row, vec],
        out_shape=[S((T, 1024), MXU), S((T, 1024), f32), S((1, 1024), f32)],
        compiler_params=_cp("arbitrary"),
    )(o, lse, dmix, sinks, e)


SHIFT_PAD = 24


def _shifted_copies(buf_ref, sh_ref, length):
    for r in range(1, 8):
        sh_ref[r - 1, 0:length, :] = buf_ref[pl.ds(r, length), :]


def _window(buf_ref, sh_ref, start, rows):
    q, r = divmod(start, 8)
    if r == 0:
        return buf_ref[pl.ds(8 * q, rows), :]
    return sh_ref[r - 1, pl.ds(8 * q, rows), :]


TAP_ROWS = 64


def _tap_sum(buf_ref, sh_ref, w_ref, starts, rows):
    outs = []
    for c0 in range(0, rows, TAP_ROWS):
        n = min(TAP_ROWS, rows - c0)
        acc = _window(buf_ref, sh_ref, starts[0] + c0, n) * w_ref[pl.ds(0, 1), :]
        for j in range(1, CONV_K):
            acc += _window(buf_ref, sh_ref, starts[j] + c0, n) * w_ref[pl.ds(j, 1), :]
        outs.append(acc)
    return jnp.concatenate(outs, axis=0)


def _conv_taps(buf_ref, sh_ref, w_ref, start, rows):
    return _tap_sum(buf_ref, sh_ref, w_ref, [start + j for j in range(CONV_K)], rows)


def conv_fwd(proj, mix, w, b, ln_g, ln_b, *, tb, name):
    T = proj.shape[0]
    hb = tb // HALO

    def body(u_ref, g_ref, up_ref, gp_ref, w_ref, b_ref, lg_ref, lb_ref, mix_ref, o_ref, hbuf, hsh):
        i = pl.program_id(0)
        hprev = up_ref[...] * _sigmoid(gp_ref[...])
        hbuf[0:HALO, :] = hprev * jnp.where(i > 0, 1.0, 0.0)
        hbuf[HALO:HALO + tb, :] = u_ref[...] * _sigmoid(g_ref[...])
        _shifted_copies(hbuf, hsh, tb + SHIFT_PAD)
        y = _conv_taps(hbuf, hsh, w_ref, HALO - (CONV_K - 1), tb) + b_ref[...]
        mu = jnp.mean(y, axis=-1, keepdims=True)
        yc = y - mu
        var = jnp.mean(yc * yc, axis=-1, keepdims=True)
        z = yc * lax.rsqrt(var + EPS) * lg_ref[...] + lb_ref[...]
        o_ref[...] = (z * _sigmoid(z)).astype(MXU)

    vec = pl.BlockSpec((1, 512), lambda i: (0, 0))
    return pl.pallas_call(
        body, name=name, grid=(T // tb,),
        in_specs=[pl.BlockSpec((tb, 512), lambda i: (i, 3)), pl.BlockSpec((tb, 512), lambda i: (i, 4)),
                  pl.BlockSpec((HALO, 512), lambda i: (jnp.maximum(i * hb - 1, 0), 3)),
                  pl.BlockSpec((HALO, 512), lambda i: (jnp.maximum(i * hb - 1, 0), 4)),
                  pl.BlockSpec((HALO, 512), lambda i: (0, 0)), vec, vec, vec, ANY],
        out_specs=pl.BlockSpec((tb, 512), lambda i: (i, 1)),
        out_shape=S(mix.shape, mix.dtype),
        scratch_shapes=[pltpu.VMEM((tb + HALO, 512), f32), pltpu.VMEM((7, tb + SHIFT_PAD, 512), f32)],
        compiler_params=_cp("arbitrary"), input_output_aliases={8: 0},
    )(proj, proj, proj, proj, w, b, ln_g, ln_b, mix)


def conv_bwd(proj, dmix, w, b, ln_g, ln_b, *, tb, name, comm=None):
    T = proj.shape[0]
    hb = tb // HALO
    ni = T // tb
    last_h = T // HALO - 1
    ext = tb + HALO

    def body(u_ref, g_ref, up_ref, gp_ref, un_ref, gn_ref, d_ref, dn_ref, w_ref, b_ref, lg_ref, lb_ref,
             du_ref, dg_ref, dw_ref, db_ref, dlg_ref, dlb_ref, hbuf, dybuf, hsh, dsh):
        i = pl.program_id(0)
        hbuf[0:HALO, :] = up_ref[...] * _sigmoid(gp_ref[...]) * jnp.where(i > 0, 1.0, 0.0)
        u = u_ref[...]
        sg = _sigmoid(g_ref[...])
        hbuf[HALO:HALO + tb, :] = u * sg
        hbuf[HALO + tb:HALO + ext, :] = un_ref[...] * _sigmoid(gn_ref[...])
        _shifted_copies(hbuf, hsh, ext + SHIFT_PAD)
        y = _conv_taps(hbuf, hsh, w_ref, HALO - (CONV_K - 1), ext) + b_ref[...]
        mu = jnp.mean(y, axis=-1, keepdims=True)
        yc = y - mu
        rstd = lax.rsqrt(jnp.mean(yc * yc, axis=-1, keepdims=True) + EPS)
        yn = yc * rstd
        z = yn * lg_ref[...] + lb_ref[...]
        sz = _sigmoid(z)
        row = lax.broadcasted_iota(jnp.int32, (ext, 1), 0)
        own = row < tb
        keep = row < jnp.where(i < ni - 1, ext, tb)
        dout = jnp.concatenate([d_ref[...], dn_ref[...]], axis=0)
        dz = jnp.where(keep, dout * (sz * (1.0 + z * (1.0 - sz))), 0.0)
        dyn = dz * lg_ref[...]
        dy = rstd * (dyn - jnp.mean(dyn, axis=-1, keepdims=True) - yn * jnp.mean(dyn * yn, axis=-1, keepdims=True))
        dybuf[...] = dy
        _shifted_copies(dybuf, dsh, tb + SHIFT_PAD)
        dz_own = jnp.where(own, dz, 0.0)
        dlg = jnp.sum(dz_own * yn, axis=0, keepdims=True)
        dlb = jnp.sum(dz_own, axis=0, keepdims=True)
        dy_own = dybuf[0:tb, :]
        dbias = jnp.sum(dy_own, axis=0, keepdims=True)
        dh = _tap_sum(dybuf, dsh, w_ref, [CONV_K - 1 - j for j in range(CONV_K)], tb)
        du_ref[...] = (dh * sg).astype(MXU)
        dg_ref[...] = (dh * u * sg * (1.0 - sg)).astype(MXU)
        taps = [jnp.sum(dy_own * _window(hbuf, hsh, HALO - (CONV_K - 1) + j, tb), axis=0, keepdims=True)
                for j in range(CONV_K)]
        taps.append(jnp.zeros((1, 512), f32))
        dwt = jnp.concatenate(taps, axis=0)

        @pl.when(i == 0)
        def _():
            dw_ref[...] = dwt
            db_ref[...] = dbias
            dlg_ref[...] = dlg
            dlb_ref[...] = dlb

        @pl.when(i > 0)
        def _():
            dw_ref[...] += dwt
            db_ref[...] += dbias
            dlg_ref[...] += dlg
            dlb_ref[...] += dlb

    vec = pl.BlockSpec((1, 512), lambda i: (0, 0))
    wspec = pl.BlockSpec((HALO, 512), lambda i: (0, 0))

    def halo_prev(col):
        return pl.BlockSpec((HALO, 512), lambda i: (jnp.maximum(i * hb - 1, 0), col))

    def halo_next(col):
        return pl.BlockSpec((HALO, 512), lambda i: (jnp.minimum((i + 1) * hb, last_h), col))

    row = pl.BlockSpec((tb, 512), lambda i: (i, 0))
    return _pallas(
        body, comm=comm, name=name, grid=(ni,),
        in_specs=[pl.BlockSpec((tb, 512), lambda i: (i, 3)), pl.BlockSpec((tb, 512), lambda i: (i, 4)),
                  halo_prev(3), halo_prev(4), halo_next(3), halo_next(4),
                  pl.BlockSpec((tb, 512), lambda i: (i, 1)), halo_next(1), wspec, vec, vec, vec],
        out_specs=[row, row, wspec, vec, vec, vec],
        out_shape=[S((T, 512), MXU), S((T, 512), MXU), S((HALO, 512), f32)] + [S((1, 512), f32)] * 3,
        scratch_shapes=[pltpu.VMEM((tb + 2 * HALO, 512), f32), pltpu.VMEM((ext, 512), f32),
                        pltpu.VMEM((7, ext + SHIFT_PAD, 512), f32), pltpu.VMEM((7, tb + SHIFT_PAD, 512), f32)],
        compiler_params=_cp("arbitrary"),
    )(proj, proj, proj, proj, proj, proj, dmix, dmix, w, b, ln_g, ln_b)


def _qk_norm_bwd(v, gain, dout, e_ref):
    r = lax.rsqrt(_seg_sum(v * v, e_ref) * (1.0 / HD) + EPS)
    y = v * r
    dgain = jnp.sum(dout * y, axis=0, keepdims=True)
    dy = dout * gain
    dv = r * (dy - y * (_seg_sum(dy * y, e_ref) * (1.0 / HD)))
    return dv, dgain


def prep_bwd(proj, e, gains, da, dc, dconv, *, tm, name):
    T = proj.shape[0]

    def body(*refs):
        p_ref, e_ref, gaq, gak, gcq, gck = refs[0:6]
        a_refs = refs[6:15]
        dcq, dckk, dcvv, du, dgt = refs[15:20]
        dp, gaq_o, gak_o, gcq_o, gck_o = refs[20:]
        i = pl.program_id(0)
        dq = a_refs[0][...] + a_refs[3][...] + a_refs[6][...]
        dk = a_refs[1][...] + a_refs[4][...] + a_refs[7][...]
        dv = a_refs[2][...] + a_refs[5][...] + a_refs[8][...]
        d, g_aq = _qk_norm_bwd(p_ref[:, 0:512], gaq[...], dq, e_ref)
        dp[:, 0:512] = d.astype(MXU)
        d, g_ak = _qk_norm_bwd(p_ref[:, 512:1024], gak[...], dk, e_ref)
        dp[:, 512:1024] = d.astype(MXU)
        dp[:, 1024:1536] = dv.astype(MXU)
        dp[:, 1536:2048] = du[...]
        dp[:, 2048:2560] = dgt[...]
        d, g_cq0 = _qk_norm_bwd(p_ref[:, 2560:3072], gcq[...], dcq[:, 0:512], e_ref)
        dp[:, 2560:3072] = d.astype(MXU)
        d, g_cq1 = _qk_norm_bwd(p_ref[:, 3072:3584], gcq[...], dcq[:, 512:1024], e_ref)
        dp[:, 3072:3584] = d.astype(MXU)
        lo = lax.broadcasted_iota(jnp.int32, (tm, 128), 1) < HD

        def fold(ref):
            g0, g1 = ref[:, 0:128], ref[:, 128:256]
            s0 = g0 + pltpu.roll(g0, HD, 1)
            s1 = g1 + pltpu.roll(g1, HD, 1)
            return jnp.where(lo, s0, s1)

        dkn = fold(dckk)
        kraw = p_ref[:, 3584:3712]
        r = lax.rsqrt(_seg_sum128(kraw * kraw, e_ref) * (1.0 / HD) + EPS)
        y = kraw * r
        g_ck = jnp.sum(dkn * y, axis=0, keepdims=True)
        dy = dkn * gck[...]
        dp[:, 3584:3712] = (r * (dy - y * (_seg_sum128(dy * y, e_ref) * (1.0 / HD)))).astype(MXU)
        dp[:, 3712:3840] = fold(dcvv).astype(MXU)
        g_cq = jnp.concatenate([g_cq0, g_cq1], axis=1)

        @pl.when(i == 0)
        def _():
            gaq_o[...] = g_aq
            gak_o[...] = g_ak
            gcq_o[...] = g_cq
            gck_o[...] = g_ck

        @pl.when(i > 0)
        def _():
            gaq_o[...] += g_aq
            gak_o[...] += g_ak
            gcq_o[...] += g_cq
            gck_o[...] += g_ck

    def vec(n):
        return pl.BlockSpec((1, n), lambda i: (0, 0))

    def rows(n):
        return pl.BlockSpec((tm, n), lambda i: (i, 0))

    return pl.pallas_call(
        body, name=name, grid=(T // tm,),
        in_specs=[rows(IN_W), pl.BlockSpec((512, 512), lambda i: (0, 0)), vec(512), vec(512), vec(512), vec(128)]
        + [rows(512)] * 9 + [rows(1024), rows(256), rows(256), rows(512), rows(512)],
        out_specs=[rows(IN_W), vec(512), vec(512), vec(1024), vec(128)],
        out_shape=[S((T, IN_W), MXU), S((1, 512), f32), S((1, 512), f32), S((1, 1024), f32), S((1, 128), f32)],
        compiler_params=_cp("arbitrary"),
    )(proj, e, *gains, *da, *dc, *dconv)


def adamw(w, m, v, pieces, *, tr, name, comm=None):
    n, R, C = w.shape
    c1 = 1.0 - ADAM_B1 ** ADAM_STEP
    c2 = 1.0 - ADAM_B2 ** ADAM_STEP
    npc = len(pieces)

    def body(*refs):
        w_ref, m_ref, v_ref = refs[0:3]
        p_refs = refs[3:3 + npc]
        g_ref, d_ref, mo_ref, vo_ref = refs[3 + npc:]
        g = p_refs[0][...].astype(f32)
        for p in p_refs[1:]:
            g = g + p[...].astype(f32)
        mn = ADAM_B1 * m_ref[...] + (1.0 - ADAM_B1) * g
        vn = ADAM_B2 * v_ref[...] + (1.0 - ADAM_B2) * (g * g)
        g_ref[...] = g
        mo_ref[...] = mn
        vo_ref[...] = vn
        d_ref[...] = -ADAM_LR * ((mn / c1) / (jnp.sqrt(vn / c2) + ADAM_EPS) + ADAM_WD * w_ref[...])

    blk = pl.BlockSpec((None, tr, C), lambda l, i: (l, i, 0))
    return _pallas(
        body, comm=comm, name=name, grid=(n, R // tr), in_specs=[blk] * (3 + npc), out_specs=[blk] * 4,
        out_shape=[S(w.shape, f32)] * 4, compiler_params=_cp("arbitrary", "arbitrary"),
    )(w, m, v, *pieces)


def add_halves(pieces, other, *, tr, name):
    _, _, r, cc = pieces.shape

    def body(c_ref, a_ref, b_ref, o_ref):
        o_ref[...] = (a_ref[...] + b_ref[...]).astype(jnp.bfloat16)

    blk = pl.BlockSpec((None, tr, cc), lambda s, i, c_ref: (s, i, 0))
    grid_spec = pltpu.PrefetchScalarGridSpec(
        num_scalar_prefetch=1, grid=(4, r // tr),
        in_specs=[pl.BlockSpec((None, None, tr, cc), lambda s, i, c_ref: (s, c_ref[0], i, 0)), blk], out_specs=blk)
    core = lax.axis_index("c").astype(jnp.int32).reshape(1)
    return pl.pallas_call(body, name=name, grid_spec=grid_spec, out_shape=S((4, r, cc), jnp.bfloat16),
                          compiler_params=_cp("arbitrary", "arbitrary"))(core, pieces, other)


def sum8(parts, *, name):
    _, R, C = parts.shape

    def body(p_ref, o_ref):
        acc = p_ref[0]
        for d in range(1, 8):
            acc = acc + p_ref[d]
        o_ref[...] = acc

    return pl.pallas_call(body, name=name, out_shape=S((R, C), f32))(parts)


def _pos():
    return lax.axis_index("x"), lax.axis_index("y"), lax.axis_index("c")


def _other_chips(x, y):
    return [(1 - x, y), (x, 1 - y), (1 - x, 1 - y)]


class GatherComm:
    def __init__(self, shards, in_place=None):
        self.ins = list(shards)
        self.nt = nt = len(shards)
        self.in_place = list(in_place) if in_place is not None else [False] * nt
        self.out_shapes = [S((2 * s.shape[1], 4 * s.shape[2]), s.dtype) if ip else S((4,) + s.shape, s.dtype)
                           for s, ip in zip(shards, self.in_place)]
        self.sem_shapes = [pltpu.SemaphoreType.DMA((nt, 6)), pltpu.SemaphoreType.DMA((nt, 6)),
                           pltpu.SemaphoreType.DMA((nt, 2))]
        self.results = None

    def _place(self, couts, t, chip, half):
        cid = 2 * chip[0] + chip[1]
        if not self.in_place[t]:
            return couts[t].at[cid, half]
        _, r, c = self.ins[t].shape
        row0 = half * r if isinstance(half, int) else pl.multiple_of(half * r, 16)
        return couts[t].at[pl.ds(row0, r), pl.ds(pl.multiple_of(cid * c, 128), c)]

    def _copy(self, couts, sems, t, k, chip, half, to, src=None):
        dst = self._place(couts, t, chip, half)
        return pltpu.make_async_remote_copy(
            src_ref=dst if src is None else src, dst_ref=dst,
            send_sem=sems[0].at[t, k], recv_sem=sems[1].at[t, k], device_id=to, device_id_type=MESH)

    def _local(self, cins, couts, sems, t):
        x, y, _ = _pos()
        return [pltpu.make_async_copy(cins[t].at[half], self._place(couts, t, (x, y), half), sems[2].at[t, half])
                for half in range(2)]

    def start(self, cins, couts, sems):
        x, y, c = _pos()
        for t in range(self.nt):
            for cp in self._local(cins, couts, sems, t):
                cp.start()
            for j, chip in enumerate(_other_chips(x, y)):
                self._copy(couts, sems, t, j, (x, y), c, (*chip, c), src=cins[t].at[c]).start()

    def mid(self, cins, couts, sems):
        x, y, c = _pos()
        for t in range(self.nt):
            for j, chip in enumerate(_other_chips(x, y)):
                self._copy(couts, sems, t, j, chip, c, (x, y, c)).wait_recv()
                self._copy(couts, sems, t, 3 + j, chip, c, (x, y, 1 - c)).start()

    def finish(self, cins, couts, sems):
        x, y, c = _pos()
        for t in range(self.nt):
            for j, chip in enumerate(_other_chips(x, y)):
                self._copy(couts, sems, t, 3 + j, chip, 1 - c, (x, y, c)).wait_recv()
        for t in range(self.nt):
            for j, chip in enumerate(_other_chips(x, y)):
                self._copy(couts, sems, t, j, (x, y), c, (*chip, c), src=cins[t].at[c]).wait_send()
                self._copy(couts, sems, t, 3 + j, chip, c, (x, y, 1 - c)).wait_send()
            for cp in self._local(cins, couts, sems, t):
                cp.wait()


class SwapComm:
    def __init__(self, pieces):
        self.ins = list(pieces)
        self.nt = nt = len(pieces)
        self.out_shapes = [S((4,) + p.shape[2:], p.dtype) for p in pieces]
        self.sem_shapes = [pltpu.SemaphoreType.DMA((nt, 4)), pltpu.SemaphoreType.DMA((nt, 4))]
        self.results = None

    def _copies(self, cins, couts, sems):
        x, y, c = _pos()
        return [pltpu.make_async_remote_copy(src_ref=cins[t].at[s, 1 - c], dst_ref=couts[t].at[s],
                                             send_sem=sems[0].at[t, s], recv_sem=sems[1].at[t, s],
                                             device_id=(x, y, 1 - c), device_id_type=MESH)
                for t in range(self.nt) for s in range(4)]

    def start(self, cins, couts, sems):
        for cp in self._copies(cins, couts, sems):
            cp.start()

    def mid(self, cins, couts, sems):
        pass

    def finish(self, cins, couts, sems):
        for cp in self._copies(cins, couts, sems):
            cp.wait()


class ExchangeComm:
    def __init__(self, arrs):
        self.ins = list(arrs)
        self.nt = nt = len(arrs)
        self.out_shapes = [S((2,) + a.shape, a.dtype) for a in arrs]
        self.sem_shapes = [pltpu.SemaphoreType.DMA((nt, 7)), pltpu.SemaphoreType.DMA((nt, 7)),
                           pltpu.SemaphoreType.DMA((nt,))]
        self.results = None

    def _copy(self, couts, sems, t, k, half, src_chip, to, src=None):
        dst = couts[t].at[half, src_chip]
        return pltpu.make_async_remote_copy(
            src_ref=dst if src is None else src, dst_ref=dst,
            send_sem=sems[0].at[t, k], recv_sem=sems[1].at[t, k], device_id=to, device_id_type=MESH)

    def _local(self, cins, couts, sems, t):
        x, y, c = _pos()
        return pltpu.make_async_copy(cins[t].at[2 * x + y], couts[t].at[c, 2 * x + y], sems[2].at[t])

    def _firsts(self, cins, couts, sems, t):
        x, y, c = _pos()
        me = 2 * x + y
        cps = [self._copy(couts, sems, t, j, c, me, (*chip, c), src=cins[t].at[2 * chip[0] + chip[1]])
               for j, chip in enumerate(_other_chips(x, y))]
        return cps + [self._copy(couts, sems, t, 6, c, me, (x, y, 1 - c), src=cins[t].at[me])]

    def start(self, cins, couts, sems):
        for t in range(self.nt):
            self._local(cins, couts, sems, t).start()
            for cp in self._firsts(cins, couts, sems, t):
                cp.start()

    def mid(self, cins, couts, sems):
        x, y, c = _pos()
        for t in range(self.nt):
            for j, chip in enumerate(_other_chips(x, y)):
                cid = 2 * chip[0] + chip[1]
                self._copy(couts, sems, t, j, c, cid, (x, y, c)).wait_recv()
                self._copy(couts, sems, t, 3 + j, c, cid, (x, y, 1 - c)).start()

    def finish(self, cins, couts, sems):
        x, y, c = _pos()
        for t in range(self.nt):
            for j, chip in enumerate(_other_chips(x, y)):
                self._copy(couts, sems, t, 3 + j, 1 - c, 2 * chip[0] + chip[1], (x, y, c)).wait_recv()
            self._copy(couts, sems, t, 6, 1 - c, 2 * x + y, (x, y, c)).wait_recv()
        for t in range(self.nt):
            for cp in self._firsts(cins, couts, sems, t):
                cp.wait_send()
            for j, chip in enumerate(_other_chips(x, y)):
                self._copy(couts, sems, t, 3 + j, c, 2 * chip[0] + chip[1], (x, y, 1 - c)).wait_send()
            self._local(cins, couts, sems, t).wait()


def gather_small(vec, *, name):
    R, C = vec.shape

    def body(v_ref, out_ref, send_sems, recv_sems):
        x, y, c = _pos()
        me = 4 * x + 2 * y + c
        out_ref[me] = v_ref[...]
        cps = []
        def peer(k):
            fx, fy, fc = (k >> 2) & 1, (k >> 1) & 1, k & 1
            return (1 - x if fx else x), (1 - y if fy else y), (1 - c if fc else c)

        for k in range(1, 8):
            cp = pltpu.make_async_remote_copy(src_ref=v_ref, dst_ref=out_ref.at[me], send_sem=send_sems.at[k - 1],
                                              recv_sem=recv_sems.at[k - 1], device_id=peer(k), device_id_type=MESH)
            cp.start()
            cps.append(cp)
        for k in range(1, 8):
            px, py, pc = peer(k)
            pltpu.make_async_remote_copy(src_ref=v_ref, dst_ref=out_ref.at[4 * px + 2 * py + pc],
                                         send_sem=send_sems.at[k - 1], recv_sem=recv_sems.at[k - 1],
                                         device_id=(px, py, pc), device_id_type=MESH).wait_recv()
        for cp in cps:
            cp.wait_send()

    return pl.pallas_call(
        body, name=name,
        in_specs=[pl.BlockSpec(memory_space=pltpu.VMEM)], out_specs=pl.BlockSpec(memory_space=pltpu.VMEM),
        out_shape=S((8, R, C), vec.dtype),
        scratch_shapes=[pltpu.SemaphoreType.DMA((7,)), pltpu.SemaphoreType.DMA((7,))],
    )(vec)


def _tile(n, prefs):
    for p in prefs:
        if n % p == 0:
            return p
    return n


def _lanes(g, reps):
    return jnp.tile(g.reshape(1, -1), (1, reps))


class _NoRide:
    def rider(self, name):
        return None

    def landed(self, comm):
        pass

    def grad(self, name, val):
        pass


def _layer_fwd(x, p, e, ride=_NoRide()):
    T, D = x.shape
    tm = _tile(T, (512, 256, 128))
    tall = _tile(T, (1024, 512, 256, 128))

    def carried(fn, *args, name, **kw):
        comm = ride.rider(name)
        out = fn(*args, name=name, comm=comm, **kw)
        ride.landed(comm)
        return out

    h, proj = carried(rms_proj, x, p["norm1_g"], p["w_in"], tm=tall, tn=_tile(IN_W, (768,)), name="rms_proj")
    gains = (_lanes(p["a_q_g"], 8), _lanes(p["a_k_g"], 8), _lanes(p["c_q_g"], 8), _lanes(p["c_k_g"], 2))
    aq, ak, av, cq, ckk, cvv = prep_fwd(proj, e, gains, tm=_tile(T, (256, 128)), name="prep_fwd")
    ols = []
    for d in DILATIONS:
        ols += carried(band_attn_fwd, aq, ak, av, None, dil=d, max_dist=A_DIST, group=1, name=f"dil_attn_fwd_{d}")
    mix = dil_combine_fwd(ols, tm=tm, name="dil_combine_fwd")
    mix = conv_fwd(proj, mix, p["conv_w"], p["conv_b"], p["conv_ln_g"], p["conv_ln_b"], tb=tm, name="conv_fwd")
    sinks = jnp.repeat(p["c_sinks"].reshape(-1), HD).reshape(1, C_W)
    o_c, l_c, mix = carried(band_attn_fwd, cq, ckk, cvv, sinks, dil=1, max_dist=C_DIST, group=4, mix=mix,
                            name="swa_attn_fwd")
    x1 = carried(matmul_res, mix, p["w_out"], x, tm=tall, tn=_tile(D, (1024, 512, 256)), name="out_proj")
    F = p["w_gate"].shape[1]
    h2, gate, up, act = carried(rms_swiglu, x1, p["norm2_g"], p["w_gate"], p["w_up"], tm=tall,
                                tn=_tile(F, (512, 256, 128)), name="rms_swiglu")
    x2 = carried(matmul_res, act, p["w_down"], x1, tm=tall, tn=_tile(D, (512, 256)), name="ffn_down")
    saved = dict(x=x, h=h, proj=proj, gains=gains, aq=aq, ak=ak, av=av, cq=cq, ckk=ckk, cvv=cvv, ols=ols, o_c=o_c,
                 l_c=l_c, sinks=sinks, mix=mix, x1=x1, h2=h2, gate=gate, up=up, act=act)
    return x2, saved


def _layer_bwd(dx2, dx2b, p, s, e, ride=_NoRide()):
    T, D = dx2.shape
    F = p["w_gate"].shape[1]
    tm = _tile(T, (512, 256, 128))
    tmm = _tile(T, (1024, 512, 256, 128))
    tkT = _tile(T, (2048, 1024, 512))
    tF = _tile(F, (512, 256, 128))
    tD = _tile(D, (1024, 512, 256))
    g = {}

    def carried(fn, *args, name, **kw):
        comm = ride.rider(name)
        out = fn(*args, name=name, comm=comm, **kw)
        ride.landed(comm)
        return out

    def big(n, val):
        g[n] = val
        ride.grad(n, val)

    d_gate, d_up = carried(nt_swiglu_bwd, dx2b, p["w_down"], s["gate"], s["up"], tm=tmm, tn=tF, name="ffn_down_bwd")
    big("w_down", tn_matmul(s["act"], dx2b, tm=tF, tn=tD, tk=tkT, name="grad_w_down"))
    big("w_gate", carried(tn_matmul, s["h2"], d_gate, tm=tD, tn=tF, tk=tkT, by_chip=True, name="grad_w_gate"))
    big("w_up", carried(tn_matmul, s["h2"], d_up, tm=tD, tn=tF, tk=tkT, by_chip=True, name="grad_w_up"))
    dx1, dx1b, g["norm2_g"] = carried(nt_rms_bwd, [(d_gate, p["w_gate"]), (d_up, p["w_up"])], s["x1"], p["norm2_g"],
                                      dx2, tm=tm, tn=_tile(D, (512, 256)), name="ffn_in_bwd")
    dmix = nt_plain(dx1b, p["w_out"], tm=tmm, tn=tD, name="out_proj_bwd")
    big("w_out", tn_matmul(s["mix"], dx1b, tm=1024, tn=tD, tk=tkT, name="grad_w_out"))
    dos = dil_combine_bwd(s["ols"], dmix, e, tm=tm, name="dil_combine_bwd")
    da = []
    for n, d in enumerate(DILATIONS):
        da += carried(band_attn_bwd, s["aq"], s["ak"], s["av"], s["ols"][2 * n + 1], dos[n], dos[3 + n], dil=d,
                      max_dist=A_DIST, group=1, name=f"dil_attn_bwd_{d}")
    du, dgt, gw, gb, glg, glb = carried(conv_bwd, s["proj"], dmix, p["conv_w"], p["conv_b"], p["conv_ln_g"],
                                        p["conv_ln_b"], tb=tm, name="conv_bwd")
    g["conv_w"], g["conv_b"], g["conv_ln_g"], g["conv_ln_b"] = gw[:CONV_K], gb, glg, glb
    do_c, dd_c, dsink = swa_pre_bwd(s["o_c"], s["l_c"], dmix, s["sinks"], e, tm=_tile(T, (256, 128)), name="swa_pre_bwd")
    g["c_sinks"] = dsink.reshape(-1, HD)[:, 0]
    dcq, dckk, dcvv = carried(band_attn_bwd, s["cq"], s["ckk"], s["cvv"], s["l_c"], do_c, dd_c, dil=1, max_dist=C_DIST,
                              group=4, name="swa_attn_bwd")
    dproj, gaq, gak, gcq, gck = prep_bwd(s["proj"], e, s["gains"], da, (dcq, dckk, dcvv), (du, dgt),
                                         tm=_tile(T, (256, 128)), name="prep_bwd")
    g["a_q_g"] = gaq.reshape(-1, HD).sum(0)
    g["a_k_g"] = gak.reshape(-1, HD).sum(0)
    g["c_q_g"] = gcq.reshape(-1, HD).sum(0)
    g["c_k_g"] = gck.reshape(-1, HD).sum(0)
    big("w_in", tn_matmul(s["h"], dproj, tm=tD, tn=_tile(IN_W, (1280,)), tk=tkT, name="grad_w_in"))
    dx, dxb, g["norm1_g"] = carried(nt_rms_bwd, [(dproj, p["w_in"])], s["x"], p["norm1_g"], dx1, tm=tmm,
                                    tn=_tile(D, (512, 256)), name="in_proj_bwd")
    return dx, dxb, g


BIG = ("w_in", "w_out", "w_gate", "w_up", "w_down")
COL_SHARDED = ("w_in", "w_gate", "w_up")
SMALL = ("norm1_g", "a_q_g", "a_k_g", "conv_w", "conv_b", "conv_ln_g", "conv_ln_b", "c_q_g", "c_k_g", "c_sinks", "norm2_g")


def _to_pieces(g, name):
    if g.ndim == 3:
        return g.reshape(4, 2, g.shape[1] // 2, g.shape[2])
    R, C = g.shape
    if name in COL_SHARDED:
        return g.reshape(2, R // 2, 4, C // 4).transpose(2, 0, 1, 3)
    return g.reshape(4, 2, R // 8, C)


def _from_gathered(w, name):
    _, _, r, c = w.shape
    if name in COL_SHARDED:
        return w.transpose(1, 2, 0, 3).reshape(2 * r, 4 * c)
    return w.reshape(8 * r, c)


def _shard(W, l, n):
    w = W[n][l]
    return w.astype(MXU).reshape(2, w.shape[0] // 2, w.shape[1])


class _LayerParams(dict):
    def __init__(self, layer, full, small):
        super().__init__(small)
        self.layer, self.full = layer, full

    def __missing__(self, n):
        return self.full[(self.layer, n)]


FWD_RIDES = {
    (0, "rms_proj"): ((0, "w_out"), (0, "w_gate")),
    (0, "swa_attn_fwd"): ((0, "w_up"),),
    (0, "rms_swiglu"): ((0, "w_down"), (1, "w_in"), (1, "w_out")),
    (0, "ffn_down"): ((1, "w_gate"),),
    (1, "rms_proj"): ((1, "w_up"),),
    (1, "rms_swiglu"): ((1, "w_down"),),
}
BWD_RIDES = {
    "grad_w_gate": (("w_down",), ()),
    "grad_w_up": (("w_gate",), ("w_down",)),
    "ffn_in_bwd": (("w_up",), ("w_gate",)),
    "dil_attn_bwd_1": (("w_out",), ()),
    "dil_attn_bwd_16": ((), ("w_up",)),
    "conv_bwd": ((), ("w_out",)),
    "in_proj_bwd": (("w_in",), ()),
}
IN_PLACE = ("w_gate", "w_up")


class _FwdRide:
    def __init__(self, layer, W, full):
        self.layer, self.W, self.full = layer, W, full

    def rider(self, name):
        keys = FWD_RIDES.get((self.layer, name))
        if not keys:
            return None
        comm = GatherComm([_shard(self.W, l, n) for l, n in keys], [n in IN_PLACE for _, n in keys])
        comm.keys = keys
        return comm

    def landed(self, comm):
        if comm is not None:
            for (l, n), g in zip(comm.keys, comm.results):
                self.full[(l, n)] = g if n in IN_PLACE else _from_gathered(g, n)


class _GradFlow:
    def __init__(self):
        self.pieces, self.sums, self.landed, self.pending = {}, {}, {}, []

    def swap(self, keys):
        comm = SwapComm([self.pieces[k] for k in keys])
        comm.keys, comm.kind = list(keys), "swap"
        return comm

    def exchange(self, keys):
        if not keys:
            return None
        comm = ExchangeComm([self.sums[k] for k in keys])
        comm.keys, comm.kind = list(keys), "exchange"
        return comm

    def take_pending(self):
        keys, self.pending = self.pending, []
        return keys

    def land(self, comm):
        if comm is None:
            return
        if isinstance(comm, MultiComm):
            for sub in comm.comms:
                self.land(sub)
            return
        for k, res in zip(comm.keys, comm.results):
            if comm.kind == "swap":
                r = res.shape[1]
                self.sums[k] = add_halves(self.pieces[k], res, tr=_tile(r, (256, 176, 128, 64, 32, 16)),
                                          name="grad_chip_sum")
                if k[1] == "w_in":
                    self.pending.append(k)
            else:
                self.landed[k] = res


class _BwdRide:
    def __init__(self, layer, flow):
        self.layer, self.flow = layer, flow
        self.final = layer == 0

    def grad(self, name, val):
        self.flow.pieces[(self.layer, name)] = _to_pieces(val, name)
        if self.final and name == "w_in":
            swap = self.flow.swap([(self.layer, name)])
            _run_comm(swap, name="swap_last")
            self.flow.land(swap)

    def rider(self, name):
        if name == "ffn_down_bwd" or (self.final and name == "in_proj_bwd"):
            return self.flow.exchange(self.flow.take_pending())
        swaps, exchanges = BWD_RIDES.get(name, ((), ()))
        comms = []
        if swaps:
            comms.append(self.flow.swap([(self.layer, n) for n in swaps]))
        if exchanges:
            comms.append(self.flow.exchange([(self.layer, n) for n in exchanges]))
        return MultiComm(comms) if comms else None

    def landed(self, comm):
        self.flow.land(comm)


def _pack(items, rows):
    flat = jnp.concatenate([a.reshape(-1).astype(f32) for a in items])
    return jnp.pad(flat, (0, rows * 128 - flat.shape[0])).reshape(rows, 128)


def _unpack(packed, shapes):
    flat = packed.reshape(-1)
    out, off = [], 0
    for shp in shapes:
        n = 1
        for d in shp:
            n *= d
        out.append(flat[off:off + n].reshape(shp))
        off += n
    return out


def kernel(x, norm1_g, w_in, a_q_g, a_k_g, conv_w, conv_b, conv_ln_g, conv_ln_b, c_q_g, c_k_g, c_sinks, w_out, norm2_g, w_gate, w_up, w_down, loss_target, m_norm1_g, m_w_in, m_a_q_g, m_a_k_g, m_conv_w, m_conv_b, m_conv_ln_g, m_conv_ln_b, m_c_q_g, m_c_k_g, m_c_sinks, m_w_out, m_norm2_g, m_w_gate, m_w_up, m_w_down, v_norm1_g, v_w_in, v_a_q_g, v_a_k_g, v_conv_w, v_conv_b, v_conv_ln_g, v_conv_ln_b, v_c_q_g, v_c_k_g, v_c_sinks, v_w_out, v_norm2_g, v_w_gate, v_w_up, v_w_down):
    W = dict(norm1_g=norm1_g, w_in=w_in, a_q_g=a_q_g, a_k_g=a_k_g, conv_w=conv_w, conv_b=conv_b, conv_ln_g=conv_ln_g,
             conv_ln_b=conv_ln_b, c_q_g=c_q_g, c_k_g=c_k_g, c_sinks=c_sinks, w_out=w_out, norm2_g=norm2_g, w_gate=w_gate,
             w_up=w_up, w_down=w_down)
    M = dict(norm1_g=m_norm1_g, w_in=m_w_in, a_q_g=m_a_q_g, a_k_g=m_a_k_g, conv_w=m_conv_w, conv_b=m_conv_b,
             conv_ln_g=m_conv_ln_g, conv_ln_b=m_conv_ln_b, c_q_g=m_c_q_g, c_k_g=m_c_k_g, c_sinks=m_c_sinks, w_out=m_w_out,
             norm2_g=m_norm2_g, w_gate=m_w_gate, w_up=m_w_up, w_down=m_w_down)
    V = dict(norm1_g=v_norm1_g, w_in=v_w_in, a_q_g=v_a_q_g, a_k_g=v_a_k_g, conv_w=v_conv_w, conv_b=v_conv_b,
             conv_ln_g=v_conv_ln_g, conv_ln_b=v_conv_ln_b, c_q_g=v_c_q_g, c_k_g=v_c_k_g, c_sinks=v_c_sinks, w_out=v_w_out,
             norm2_g=v_norm2_g, w_gate=v_w_gate, w_up=v_w_up, w_down=v_w_down)
    depth = norm1_g.shape[0]
    T, D = x.shape[1], x.shape[2]
    xs = x.reshape(T, D)
    chip = 2 * lax.axis_index("x") + lax.axis_index("y")
    e = _head_eye()

    full = {}
    first = GatherComm([_shard(W, 0, "w_in"), conv_w])
    _run_comm(first, name="gather_first")
    full[(0, "w_in")] = _from_gathered(first.results[0], "w_in")
    conv_full = first.results[1].transpose(1, 2, 0, 3).reshape(depth, CONV_K, B_W)
    params = []
    for l in range(depth):
        small = {n: W[n][l].reshape(1, -1) for n in SMALL if n != "conv_w"}
        small["conv_w"] = jnp.pad(conv_full[l], ((0, HALO - CONV_K), (0, 0)))
        params.append(_LayerParams(l, full, small))

    saved = []
    act = xs
    for l in range(depth):
        act, s = _layer_fwd(act, params[l], e, _FwdRide(l, W, full))
        saved.append(s)
    dy, dyb, loss_part = loss_head(act, loss_target.reshape(T, D), tm=_tile(T, (512, 256, 128)), name="loss_head")
    grads = [None] * depth
    flow = _GradFlow()
    for l in reversed(range(depth)):
        dy, dyb, grads[l] = _layer_bwd(dy, dyb, params[l], saved[l], e, _BwdRide(l, flow))
    grad_x = dy.reshape(x.shape)

    out = {}
    for n in ("w_down", "w_gate", "w_up", "w_out", "w_in"):
        last = flow.exchange(flow.take_pending())
        per_layer = [flow.landed[(l, n)] for l in range(depth)]
        r, cc = per_layer[0].shape[2], per_layer[0].shape[3]
        srcs = [jnp.stack([pl_[:, s].reshape(2 * r, cc) for pl_ in per_layer]) for s in range(4)]
        out[n] = adamw(W[n], M[n], V[n], srcs, tr=_tile(2 * r, (256, 176, 128, 64, 32, 16)), name="adamw_" + n,
                       comm=last)
        flow.land(last)

    small_shapes = []
    items = []
    for l in range(depth):
        for n in SMALL:
            a = grads[l][n]
            if n == "conv_w":
                a = a.reshape(CONV_K, 4, B_W // 4).transpose(1, 0, 2)
            items.append(a)
            small_shapes.append(a.shape)
    items.append(loss_part[0, 0:1])
    small_shapes.append((1,))
    total = sum(int(jnp.size(a)) for a in items)
    rows = -(-total // 1024) * 8
    summed = sum8(gather_small(_pack(items, rows), name="gather_small"), name="sum_small")
    parts = _unpack(summed, small_shapes)
    loss = parts[-1][0]
    small_g = {n: [] for n in SMALL}
    for l in range(depth):
        for i, n in enumerate(SMALL):
            a = parts[l * len(SMALL) + i]
            if n == "conv_w":
                a = lax.dynamic_index_in_dim(a, chip, axis=0, keepdims=False)
            small_g[n].append(a.reshape(W[n].shape[1:]))
    sw = [W[n] for n in SMALL]
    sm = [M[n] for n in SMALL]
    sv = [V[n] for n in SMALL]
    sg = [jnp.stack(small_g[n]) for n in SMALL]
    tot2 = sum(int(jnp.size(a)) for a in sw)
    rows2 = -(-tot2 // 1024) * 8
    res = adamw(_pack(sw, rows2)[None], _pack(sm, rows2)[None], _pack(sv, rows2)[None], [_pack(sg, rows2)[None]],
                tr=rows2, name="adamw_small")
    shapes2 = [a.shape for a in sw]
    small_out = [_unpack(r[0], shapes2) for r in res]
    for i, n in enumerate(SMALL):
        out[n] = [small_out[k][i] for k in range(4)]

    order = ("norm1_g", "w_in", "a_q_g", "a_k_g", "conv_w", "conv_b", "conv_ln_g", "conv_ln_b", "c_q_g", "c_k_g",
             "c_sinks", "w_out", "norm2_g", "w_gate", "w_up", "w_down")
    return (loss, grad_x, *[out[n][0] for n in order], *[out[n][1] for n in order], *[out[n][2] for n in order],
            *[out[n][3] for n in order])
```

```python
import functools

import jax
import jax.numpy as jnp
from jax import lax
from jax.experimental import pallas as pl
from jax.experimental.pallas import tpu as pltpu

f32 = jnp.float32
MXU = jnp.bfloat16
S = jax.ShapeDtypeStruct
MESH = pl.DeviceIdType.MESH

EPS = 1e-6
NEG = -1e30
HD = 64
BLK = 128
A_W, B_W, C_W = 512, 512, 1024
KV_W = 128
IN_W = 3 * A_W + 2 * B_W + C_W + 2 * KV_W
CONV_K = 31
HALO = 32
DILATIONS = (1, 4, 16)
A_DIST, C_DIST = 128, 127
SCALE = HD ** -0.5
VMEM_LIMIT = 56 * 1024 * 1024
VMEM_TALL = 62 * 1024 * 1024

ADAM_LR, ADAM_B1, ADAM_B2, ADAM_EPS, ADAM_WD, ADAM_STEP = 0.001, 0.9, 0.999, 1e-08, 0.01, 10


def _cp(*sem, vmem=VMEM_LIMIT):
    return pltpu.CompilerParams(dimension_semantics=sem, vmem_limit_bytes=vmem)


ANY = pl.BlockSpec(memory_space=pl.ANY)


def _pallas(body, *, comm=None, name, grid, in_specs, out_specs, out_shape, scratch_shapes=(), compiler_params,
            input_output_aliases=None):
    aliases = dict(input_output_aliases or {})
    if comm is None:
        return pl.pallas_call(body, name=name, grid=grid, in_specs=in_specs, out_specs=out_specs, out_shape=out_shape,
                              scratch_shapes=list(scratch_shapes), compiler_params=compiler_params,
                              input_output_aliases=aliases)
    single = not isinstance(out_shape, (list, tuple))
    o_shapes = [out_shape] if single else list(out_shape)
    o_specs = [out_specs] if single else list(out_specs)
    n_in, n_out, n_sc = len(in_specs), len(o_shapes), len(scratch_shapes)
    nci, nco = len(comm.ins), len(comm.out_shapes)
    total = 1
    for g in grid:
        total *= g

    def carried(*refs):
        ins, cins = refs[:n_in], refs[n_in:n_in + nci]
        o0 = n_in + nci
        outs, couts = refs[o0:o0 + n_out], refs[o0 + n_out:o0 + n_out + nco]
        s0 = o0 + n_out + nco
        scratch, sems = refs[s0:s0 + n_sc], refs[s0 + n_sc:]
        step = pl.program_id(0)
        for axis in range(1, len(grid)):
            step = step * grid[axis] + pl.program_id(axis)

        @pl.when(step == 0)
        def _():
            comm.start(cins, couts, sems)

        body(*ins, *outs, *scratch)

        @pl.when(step == (3 * total) // 4)
        def _():
            comm.mid(cins, couts, sems)

        @pl.when(step == total - 1)
        def _():
            comm.finish(cins, couts, sems)

    call = pl.pallas_call(carried, name=name, grid=grid, in_specs=list(in_specs) + [ANY] * nci,
                          out_specs=o_specs + [ANY] * nco, out_shape=o_shapes + list(comm.out_shapes),
                          scratch_shapes=list(scratch_shapes) + list(comm.sem_shapes), compiler_params=compiler_params,
                          input_output_aliases=aliases)

    def run(*args):
        res = call(*args, *comm.ins)
        comm.results = list(res[n_out:])
        return res[0] if single else list(res[:n_out])

    return run


class MultiComm:
    def __init__(self, comms):
        self.comms = list(comms)
        self.ins = [a for c in self.comms for a in c.ins]
        self.out_shapes = [s for c in self.comms for s in c.out_shapes]
        self.sem_shapes = [s for c in self.comms for s in c.sem_shapes]

    def _each(self, cins, couts, sems):
        i = o = s = 0
        for c in self.comms:
            ni, no, ns = len(c.ins), len(c.out_shapes), len(c.sem_shapes)
            yield c, cins[i:i + ni], couts[o:o + no], sems[s:s + ns]
            i, o, s = i + ni, o + no, s + ns

    def start(self, cins, couts, sems):
        for c, a, b, d in self._each(cins, couts, sems):
            c.start(a, b, d)

    def mid(self, cins, couts, sems):
        for c, a, b, d in self._each(cins, couts, sems):
            c.mid(a, b, d)

    def finish(self, cins, couts, sems):
        for c, a, b, d in self._each(cins, couts, sems):
            c.finish(a, b, d)

    @property
    def results(self):
        return [r for c in self.comms for r in c.results]

    @results.setter
    def results(self, vals):
        o = 0
        for c in self.comms:
            c.results = list(vals[o:o + len(c.out_shapes)])
            o += len(c.out_shapes)


def _run_comm(comm, *, name):
    nci, nco = len(comm.ins), len(comm.out_shapes)

    def body(*refs):
        cins, couts, sems = refs[:nci], refs[nci:nci + nco], refs[nci + nco:]
        comm.start(cins, couts, sems)
        comm.mid(cins, couts, sems)
        comm.finish(cins, couts, sems)

    comm.results = list(pl.pallas_call(body, name=name, in_specs=[ANY] * nci, out_specs=[ANY] * nco,
                                       out_shape=list(comm.out_shapes), scratch_shapes=list(comm.sem_shapes))(*comm.ins))


def _nt(a, b):
    return lax.dot_general(a, b, (((1,), (1,)), ((), ())), preferred_element_type=f32)


def _tn(a, b):
    return lax.dot_general(a, b, (((0,), (0,)), ((), ())), preferred_element_type=f32)


def _nn(a, b):
    return jnp.dot(a, b, preferred_element_type=f32)


def _sigmoid(x):
    return 1.0 / (1.0 + jnp.exp(-x))


def _seg_sum(v, e_ref):
    hi = v.astype(jnp.bfloat16)
    lo = (v - hi.astype(f32)).astype(jnp.bfloat16)
    e = e_ref[...]
    return _nn(hi, e) + _nn(lo, e)


def _seg_sum128(v, e_ref):
    e = e_ref[0:128, 0:128]
    hi = v.astype(jnp.bfloat16)
    lo = (v - hi.astype(f32)).astype(jnp.bfloat16)
    return _nn(hi, e) + _nn(lo, e)


def _head_eye():
    r = lax.broadcasted_iota(jnp.int32, (512, 512), 0) // HD
    c = lax.broadcasted_iota(jnp.int32, (512, 512), 1) // HD
    return (r == c).astype(jnp.bfloat16)


def _rms_norm_rows(x_ref, g_ref, h_ref, tm):
    def chunk(c, carry):
        rows = pl.ds(c * BLK, BLK)
        xf = x_ref[rows, :]
        r = lax.rsqrt(jnp.mean(xf * xf, axis=-1, keepdims=True) + EPS)
        h_ref[rows, :] = (xf * r * g_ref[...]).astype(MXU)
        return carry
    lax.fori_loop(0, tm // BLK, chunk, 0)


def rms_proj(x, g, w, *, tm, tn, name, comm=None):
    T, D = x.shape
    N = w.shape[1]

    def body(x_ref, g_ref, w_ref, h_ref, o_ref):
        @pl.when(pl.program_id(1) == 0)
        def _():
            _rms_norm_rows(x_ref, g_ref, h_ref, tm)
        o_ref[...] = _nn(h_ref[...], w_ref[...])

    return _pallas(
        body, comm=comm, name=name, grid=(T // tm, N // tn),
        in_specs=[pl.BlockSpec((tm, D), lambda i, j: (i, 0)), pl.BlockSpec((1, D), lambda i, j: (0, 0)),
                  pl.BlockSpec((D, tn), lambda i, j: (0, j))],
        out_specs=[pl.BlockSpec((tm, D), lambda i, j: (i, 0)), pl.BlockSpec((tm, tn), lambda i, j: (i, j))],
        out_shape=[S((T, D), MXU), S((T, N), f32)],
        compiler_params=_cp("arbitrary", "arbitrary"),
    )(x, g, w)


def rms_swiglu(x, g, wg, wu, *, tm, tn, name, comm=None):
    T, D = x.shape
    N = wg.shape[1]

    def body(x_ref, g_ref, wg_ref, wu_ref, h_ref, gate_ref, up_ref, act_ref):
        @pl.when(pl.program_id(1) == 0)
        def _():
            _rms_norm_rows(x_ref, g_ref, h_ref, tm)
        h = h_ref[...]
        gate = _nn(h, wg_ref[...])
        up = _nn(h, wu_ref[...])
        gate_ref[...] = gate
        up_ref[...] = up
        act_ref[...] = (gate * _sigmoid(gate) * up).astype(MXU)

    wspec = pl.BlockSpec((D, tn), lambda i, j: (0, j))
    ospec = pl.BlockSpec((tm, tn), lambda i, j: (i, j))
    return _pallas(
        body, comm=comm, name=name, grid=(T // tm, N // tn),
        in_specs=[pl.BlockSpec((tm, D), lambda i, j: (i, 0)), pl.BlockSpec((1, D), lambda i, j: (0, 0)), wspec, wspec],
        out_specs=[pl.BlockSpec((tm, D), lambda i, j: (i, 0)), ospec, ospec, ospec],
        out_shape=[S((T, D), MXU), S((T, N), f32), S((T, N), f32), S((T, N), MXU)],
        compiler_params=_cp("arbitrary", "arbitrary"),
    )(x, g, wg, wu)


def matmul_res(a, w, res, *, tm, tn, name, comm=None):
    T, K = a.shape
    N = w.shape[1]

    def body(a_ref, w_ref, r_ref, o_ref):
        o_ref[...] = r_ref[...] + _nn(a_ref[...], w_ref[...])

    return _pallas(
        body, comm=comm, name=name, grid=(T // tm, N // tn),
        in_specs=[pl.BlockSpec((tm, K), lambda i, j: (i, 0)), pl.BlockSpec((K, tn), lambda i, j: (0, j)),
                  pl.BlockSpec((tm, tn), lambda i, j: (i, j))],
        out_specs=pl.BlockSpec((tm, tn), lambda i, j: (i, j)),
        out_shape=S((T, N), f32),
        compiler_params=_cp("arbitrary", "arbitrary"),
    )(a, w, res)


def nt_plain(a, w, *, tm, tn, name, comm=None):
    T, K = a.shape
    N = w.shape[0]

    def body(a_ref, w_ref, o_ref):
        o_ref[...] = _nt(a_ref[...], w_ref[...])

    return _pallas(
        body, comm=comm, name=name, grid=(T // tm, N // tn),
        in_specs=[pl.BlockSpec((tm, K), lambda i, j: (i, 0)), pl.BlockSpec((tn, K), lambda i, j: (j, 0))],
        out_specs=pl.BlockSpec((tm, tn), lambda i, j: (i, j)),
        out_shape=S((T, N), f32),
        compiler_params=_cp("arbitrary", "arbitrary"),
    )(a, w)


def nt_swiglu_bwd(dy, wd, gate, up, *, tm, tn, name, comm=None):
    T, D = dy.shape
    F = wd.shape[0]
    nj = F // tn
    steps = (T // tm) * nj

    def body(dy_ref, w_ref, g_ref, u_ref, dg_ref, du_ref, acc_ref):
        s = pl.program_id(0)

        @pl.when(s < steps)
        def _():
            acc_ref[s % 2] = _nt(dy_ref[...], w_ref[...])

        @pl.when(s > 0)
        def _():
            d_act = acc_ref[(s - 1) % 2]
            g = g_ref[...]
            sg = _sigmoid(g)
            du_ref[...] = (d_act * (g * sg)).astype(MXU)
            dg_ref[...] = (d_act * u_ref[...] * (sg * (1.0 + g * (1.0 - sg)))).astype(MXU)

    def making(s):
        s = jnp.minimum(s, steps - 1)
        return s // nj, s % nj

    def finishing(s):
        s = jnp.maximum(s - 1, 0)
        return s // nj, s % nj

    blk = pl.BlockSpec((tm, tn), finishing)
    return _pallas(
        body, comm=comm, name=name, grid=(steps + 1,),
        in_specs=[pl.BlockSpec((tm, D), lambda s: (making(s)[0], 0)), pl.BlockSpec((tn, D), lambda s: (making(s)[1], 0)),
                  blk, blk],
        out_specs=[blk, blk],
        out_shape=[S((T, F), MXU), S((T, F), MXU)],
        scratch_shapes=[pltpu.VMEM((2, tm, tn), f32)],
        compiler_params=_cp("arbitrary"),
    )(dy, wd, gate, up)


def nt_rms_bwd(terms, x, g, dres, *, tm, tn, name, comm=None):
    T, D = x.shape
    K = terms[0][0].shape[1]
    nj = D // tn
    rc = tm // nj
    nt = len(terms)
    ni = T // tm

    def body(*refs):
        a_refs = refs[0:2 * nt:2]
        w_refs = refs[1:2 * nt:2]
        x_ref, g_ref, r_ref, dx_ref, dxb_ref, dg_ref, acc_ref = refs[2 * nt:]
        i, j = pl.program_id(0), pl.program_id(1)

        @pl.when(i < ni)
        def _():
            part = _nt(a_refs[0][...], w_refs[0][...])
            for t in range(1, nt):
                part += _nt(a_refs[t][...], w_refs[t][...])
            acc_ref[i % 2, j] = part

        @pl.when(i > 0)
        def _():
            rows = pl.ds(pl.multiple_of(j * rc, BLK), rc)
            dh = jnp.concatenate([acc_ref[(i - 1) % 2, jj, rows, :] for jj in range(nj)], axis=1)
            xf = x_ref[...]
            r = lax.rsqrt(jnp.mean(xf * xf, axis=-1, keepdims=True) + EPS)
            y = xf * r
            dy = dh * g_ref[...]
            dx = r_ref[...] + r * (dy - y * jnp.mean(dy * y, axis=-1, keepdims=True))
            dx_ref[...] = dx
            dxb_ref[...] = dx.astype(MXU)
            dgain = jnp.sum(dh * y, axis=0, keepdims=True)
            first = jnp.logical_and(i == 1, j == 0)

            @pl.when(first)
            def _():
                dg_ref[...] = dgain

            @pl.when(jnp.logical_not(first))
            def _():
                dg_ref[...] += dgain

    in_specs, args = [], []
    for a, w in terms:
        in_specs += [pl.BlockSpec((tm, K), lambda i, j: (jnp.minimum(i, ni - 1), 0)),
                     pl.BlockSpec((tn, K), lambda i, j: (jnp.where(i < ni, j, nj - 1), 0))]
        args += [a, w]
    row = pl.BlockSpec((rc, D), lambda i, j: (jnp.where(i > 0, (i - 1) * nj + j, 0), 0))
    vec = pl.BlockSpec((1, D), lambda i, j: (0, 0))
    in_specs += [row, vec, row]
    return _pallas(
        body, comm=comm, name=name, grid=(ni + 1, nj), in_specs=in_specs,
        out_specs=[row, row, vec],
        out_shape=[S((T, D), f32), S((T, D), MXU), S((1, D), f32)],
        scratch_shapes=[pltpu.VMEM((2, nj, tm, tn), f32)],
        compiler_params=_cp("arbitrary", "arbitrary", vmem=VMEM_TALL),
    )(*args, x, g, dres)


def tn_matmul(a, b, *, tm, tn, tk, name, by_chip=False, comm=None):
    T, M = a.shape
    N = b.shape[1]
    if by_chip:
        tn = N // 4
        out_spec = pl.BlockSpec((None, tm, tn), lambda i, j, k: (j, i, 0))
        out_shape = S((4, M, tn), f32)
    else:
        out_spec = pl.BlockSpec((tm, tn), lambda i, j, k: (i, j))
        out_shape = S((M, N), f32)

    def body(a_ref, b_ref, o_ref):
        part = _tn(a_ref[...], b_ref[...])

        @pl.when(pl.program_id(2) == 0)
        def _():
            o_ref[...] = part

        @pl.when(pl.program_id(2) > 0)
        def _():
            o_ref[...] += part

    return _pallas(
        body, comm=comm, name=name, grid=(M // tm, N // tn, T // tk),
        in_specs=[pl.BlockSpec((tk, tm), lambda i, j, k: (k, i)), pl.BlockSpec((tk, tn), lambda i, j, k: (k, j))],
        out_specs=out_spec, out_shape=out_shape,
        compiler_params=_cp("arbitrary", "arbitrary", "arbitrary"),
    )(a, b)


def loss_head(y, target, *, tm, name):
    T, D = y.shape
    ni = T // tm

    def body(y_ref, t_ref, dy_ref, dyb_ref, l_ref, acc_ref):
        i = pl.program_id(0)
        e = y_ref[...] - t_ref[...]
        dy = e * (1.0 / D)
        dy_ref[...] = dy
        dyb_ref[...] = dy.astype(MXU)
        part = jnp.sum(e * e, axis=0, keepdims=True)

        @pl.when(i == 0)
        def _():
            acc_ref[...] = part

        @pl.when(i > 0)
        def _():
            acc_ref[...] += part

        @pl.when(i == ni - 1)
        def _():
            tot = jnp.sum(acc_ref[...], axis=1, keepdims=True) * (0.5 / D)
            l_ref[...] = jnp.broadcast_to(tot, (1, 128))

    row = pl.BlockSpec((tm, D), lambda i: (i, 0))
    return pl.pallas_call(
        body, name=name, grid=(ni,), in_specs=[row, row],
        out_specs=[row, row, pl.BlockSpec((1, 128), lambda i: (0, 0))],
        out_shape=[S((T, D), f32), S((T, D), MXU), S((1, 128), f32)],
        scratch_shapes=[pltpu.VMEM((1, D), f32)],
        compiler_params=_cp("arbitrary"),
    )(y, target)


def _qk_norm(v, gain, e_ref):
    r = lax.rsqrt(_seg_sum(v * v, e_ref) * (1.0 / HD) + EPS)
    return v * r * gain


def _dup_halves(pair):
    rolled = pltpu.roll(pair, HD, 1)
    lo = lax.broadcasted_iota(jnp.int32, pair.shape, 1) < HD
    return jnp.where(lo, pair, rolled), jnp.where(lo, rolled, pair)


def prep_fwd(proj, e, gains, *, tm, name):
    T = proj.shape[0]

    def body(p_ref, e_ref, gaq, gak, gcq, gck, aq, ak, av, cq, ckk, cvv):
        aq[...] = _qk_norm(p_ref[:, 0:512], gaq[...], e_ref)
        ak[...] = _qk_norm(p_ref[:, 512:1024], gak[...], e_ref)
        av[...] = p_ref[:, 1024:1536]
        cq[:, 0:512] = _qk_norm(p_ref[:, 2560:3072], gcq[...], e_ref).astype(MXU)
        cq[:, 512:1024] = _qk_norm(p_ref[:, 3072:3584], gcq[...], e_ref).astype(MXU)
        kraw = p_ref[:, 3584:3712]
        kn = kraw * lax.rsqrt(_seg_sum128(kraw * kraw, e_ref) * (1.0 / HD) + EPS) * gck[...]
        k0, k1 = _dup_halves(kn)
        ckk[:, 0:128] = k0.astype(MXU)
        ckk[:, 128:256] = k1.astype(MXU)
        v0, v1 = _dup_halves(p_ref[:, 3712:3840])
        cvv[:, 0:128] = v0.astype(MXU)
        cvv[:, 128:256] = v1.astype(MXU)

    def vec(n):
        return pl.BlockSpec((1, n), lambda i: (0, 0))

    def rows(n):
        return pl.BlockSpec((tm, n), lambda i: (i, 0))

    return pl.pallas_call(
        body, name=name, grid=(T // tm,),
        in_specs=[rows(IN_W), pl.BlockSpec((512, 512), lambda i: (0, 0)), vec(512), vec(512), vec(512), vec(128)],
        out_specs=[rows(512), rows(512), rows(512), rows(1024), rows(256), rows(256)],
        out_shape=[S((T, 512), f32)] * 3 + [S((T, 1024), MXU), S((T, 256), MXU), S((T, 256), MXU)],
        compiler_params=_cp("arbitrary"),
    )(proj, e, *gains)


def _band_mask(max_dist, shut):
    r = lax.broadcasted_iota(jnp.int32, (2 * BLK, 2 * BLK), 0) & (BLK - 1)
    c = lax.broadcasted_iota(jnp.int32, (2 * BLK, 2 * BLK), 1)
    prev = jnp.logical_and(c < BLK, c >= r + (BLK - max_dist) + shut)
    return jnp.logical_or(prev, jnp.logical_and(c >= BLK, c - BLK <= r))


def _prev_mask(max_dist, shut):
    r = lax.broadcasted_iota(jnp.int32, (2 * BLK, BLK), 0) & (BLK - 1)
    c = lax.broadcasted_iota(jnp.int32, (2 * BLK, BLK), 1)
    return c >= r + (BLK - max_dist) + shut


def _head_masks():
    lo = (lax.broadcasted_iota(jnp.int32, (BLK, BLK), 1) < HD).astype(f32)
    return lo.astype(MXU), (1.0 - lo).astype(MXU)


def _stack_heads(x, hm):
    return jnp.concatenate([x * hm[0], x * hm[1]], axis=0)


def _unstack_heads(y, lane_lo):
    return jnp.where(lane_lo, y[0:BLK], y[BLK:2 * BLK])


def _rows(ref, start, dil):
    if dil == 1:
        return ref[pl.ds(start, BLK), :]
    return ref[pl.ds(start, BLK, stride=dil), :]


def _set_rows(ref, start, dil, val):
    if dil == 1:
        ref[pl.ds(start, BLK), :] = val
    else:
        ref[pl.ds(start, BLK, stride=dil), :] = val


def _attn_geometry(T, dil):
    span = BLK * dil
    n = max(1, 512 // span)
    return span, n, T // (span * n)


def band_attn_fwd(q, k, v, sinks, *, dil, max_dist, group, name, mix=None, comm=None):
    T = q.shape[0]
    P = q.shape[1] // BLK
    span, n, nb = _attn_geometry(T, dil)

    def body(*refs):
        s_ref = m_ref = None
        q_ref, kc_ref, kp_ref, vc_ref, vp_ref = refs[:5]
        rest = list(refs[5:])
        if sinks is not None:
            s_ref = rest.pop(0)
        if mix is not None:
            rest.pop(0)
            o_ref, l_ref, m_ref = rest
        else:
            o_ref, l_ref = rest
        b = pl.program_id(0)
        mask = _band_mask(max_dist, 0)
        mask0 = _band_mask(max_dist, jnp.where(b > 0, 0, BLK + 1))
        lane_lo = lax.broadcasted_iota(jnp.int32, (BLK, BLK), 1) < HD
        hm = _head_masks()
        if sinks is not None:
            row_lo = lax.broadcasted_iota(jnp.int32, (1, BLK), 1) < HD
            sk0 = jnp.max(jnp.where(row_lo, s_ref[...], NEG), axis=1, keepdims=True)
            sk1 = jnp.max(jnp.where(row_lo, NEG, s_ref[...]), axis=1, keepdims=True)
            sk = jnp.where(lax.broadcasted_iota(jnp.int32, (2 * BLK, 1), 0) < BLK, sk0, sk1)

        def load(r, sub):
            at = r + sub * span
            kc, vc = _rows(kc_ref, at, dil).astype(MXU), _rows(vc_ref, at, dil).astype(MXU)
            if sub == 0:
                kp, vp = _rows(kp_ref, r, dil).astype(MXU), _rows(vp_ref, r, dil).astype(MXU)
            else:
                kp, vp = _rows(kc_ref, at - span, dil).astype(MXU), _rows(vc_ref, at - span, dil).astype(MXU)
            qst = _stack_heads(_rows(q_ref, at, dil).astype(MXU), hm)
            return (qst, jnp.concatenate([kp, kc], axis=0), jnp.concatenate([vp, vc], axis=0),
                    mask0 if sub == 0 else mask, at)

        def attend(items):
            ss = [jnp.where(m_, _nt(qst, kcat) * SCALE, NEG) for qst, kcat, _, m_, _ in items]
            ms = [jnp.max(s, axis=1, keepdims=True) for s in ss]
            if sinks is not None:
                ms = [jnp.maximum(m, sk) for m in ms]
            ps = [jnp.exp(s - m) for s, m in zip(ss, ms)]
            dens = [jnp.sum(p_, axis=1, keepdims=True) for p_ in ps]
            if sinks is not None:
                dens = [d + jnp.exp(sk - m) for d, m in zip(dens, ms)]
            outs = [_nn(p_.astype(MXU), it[2]) / d for p_, it, d in zip(ps, items, dens)]
            for it, o, m, d in zip(items, outs, ms, dens):
                lse = m + jnp.log(d)
                if m_ref is not None:
                    _set_rows(m_ref, it[4], dil, _unstack_heads(o, lane_lo).astype(MXU))
                _set_rows(o_ref, it[4], dil, _unstack_heads(o, lane_lo))
                _set_rows(l_ref, it[4], dil, jnp.where(lane_lo, lse[0:BLK], lse[BLK:2 * BLK]))

        if dil * n <= 4:
            work = [(r, sub) for r in range(dil) for sub in range(n)]
            for g in range(0, len(work), 2):
                attend([load(*w) for w in work[g:g + 2]])
        else:
            def two_streams(i, carry):
                attend([load(2 * i, 0), load(2 * i + 1, 0)])
                return carry
            lax.fori_loop(0, dil // 2, two_streams, 0)

    rows_per_step = span * n
    qspec = pl.BlockSpec((rows_per_step, BLK), lambda b, p: (b, p))
    cur = pl.BlockSpec((rows_per_step, BLK), lambda b, p: (b, p // group))
    prev = pl.BlockSpec((span, BLK), lambda b, p: (jnp.maximum(b * n - 1, 0), p // group))
    in_specs = [qspec, cur, prev, cur, prev]
    args = [q, k, k, v, v]
    if sinks is not None:
        in_specs.append(pl.BlockSpec((1, BLK), lambda b, p: (0, p)))
        args.append(sinks)
    out_specs, out_shape, aliases = [qspec, qspec], [S(q.shape, f32), S(q.shape, f32)], {}
    if mix is not None:
        first_block = mix.shape[1] // BLK - P
        aliases = {len(args): 2}
        in_specs.append(ANY)
        args.append(mix)
        out_specs.append(pl.BlockSpec((rows_per_step, BLK), lambda b, p: (b, first_block + p)))
        out_shape.append(S(mix.shape, mix.dtype))
    return _pallas(
        body, comm=comm, name=name, grid=(nb, P), in_specs=in_specs, out_specs=out_specs, out_shape=out_shape,
        compiler_params=_cp("arbitrary", "arbitrary"), input_output_aliases=aliases,
    )(*args)


def band_attn_bwd(q, k, v, lse, do, dd, *, dil, max_dist, group, name, comm=None):
    T = q.shape[0]
    P = q.shape[1] // BLK
    span, n, nb = _attn_geometry(T, dil)
    assert group == 1 or dil == 1

    def body(q_ref, qn_ref, do_ref, don_ref, l_ref, ln_ref, d_ref, dn_ref, kc_ref, kp_ref, vc_ref, vp_ref,
             dq_ref, dk_ref, dv_ref):
        b, p = pl.program_id(0), pl.program_id(1)
        mask = _band_mask(max_dist, 0)
        mask0 = _band_mask(max_dist, jnp.where(b > 0, 0, BLK + 1))
        tail = _prev_mask(max_dist, jnp.where(b < nb - 1, 0, BLK + 1))
        lane_lo = lax.broadcasted_iota(jnp.int32, (BLK, BLK), 1) < HD
        hm = _head_masks()
        own_lanes = (lax.broadcasted_iota(jnp.int32, (2 * BLK, BLK), 1) < HD) == (
            lax.broadcasted_iota(jnp.int32, (2 * BLK, BLK), 0) < BLK)

        def per_row(x):
            return jnp.max(jnp.where(own_lanes, jnp.concatenate([x, x], axis=0), NEG), axis=1, keepdims=True)

        def q_side(refs, at):
            q_r, do_r, l_r, d_r = refs
            return (_stack_heads(_rows(q_r, at, dil).astype(MXU), hm), _stack_heads(_rows(do_r, at, dil).astype(MXU), hm),
                    per_row(_rows(l_r, at, dil)), per_row(_rows(d_r, at, dil)))

        def kv(ref, at):
            return _rows(ref, at, dil).astype(MXU)

        first = p % group == 0

        def put_kv(ref, at, val):
            if group == 1:
                _set_rows(ref, at, dil, val)
            else:
                @pl.when(first)
                def _():
                    ref[pl.ds(at, BLK), :] = val

                @pl.when(jnp.logical_not(first))
                def _():
                    ref[pl.ds(at, BLK), :] += val

        def stream(r):
            dks, dvs = [None] * n, [None] * n
            for sub in range(n):
                at = r + sub * span
                qst, dost, lrow, drow = q_side((q_ref, do_ref, l_ref, d_ref), at)
                if sub == 0:
                    kp, vp, m_ = kv(kp_ref, r), kv(vp_ref, r), mask0
                else:
                    kp, vp, m_ = kv(kc_ref, at - span), kv(vc_ref, at - span), mask
                kcat = jnp.concatenate([kp, kv(kc_ref, at)], axis=0)
                vcat = jnp.concatenate([vp, kv(vc_ref, at)], axis=0)
                pr = jnp.where(m_, jnp.exp(_nt(qst, kcat) * SCALE - lrow), 0.0)
                ds = (pr * (_nt(dost, vcat) - drow) * SCALE).astype(MXU)
                prb = pr.astype(MXU)
                _set_rows(dq_ref, at, dil, _unstack_heads(_nn(ds, kcat), lane_lo))
                if sub == 0:
                    dks[0] = _tn(ds[:, BLK:], qst)
                    dvs[0] = _tn(prb[:, BLK:], dost)
                else:
                    dkk, dvv = _tn(ds, qst), _tn(prb, dost)
                    dks[sub - 1] += dkk[0:BLK]
                    dvs[sub - 1] += dvv[0:BLK]
                    dks[sub], dvs[sub] = dkk[BLK:], dvv[BLK:]
            at = r + (n - 1) * span
            qst, dost, lrow, drow = q_side((qn_ref, don_ref, ln_ref, dn_ref), r)
            pr = jnp.where(tail, jnp.exp(_nt(qst, kv(kc_ref, at)) * SCALE - lrow), 0.0)
            ds = (pr * (_nt(dost, kv(vc_ref, at)) - drow) * SCALE).astype(MXU)
            dks[n - 1] += _tn(ds, qst)
            dvs[n - 1] += _tn(pr.astype(MXU), dost)
            for sub in range(n):
                put_kv(dk_ref, r + sub * span, dks[sub])
                put_kv(dv_ref, r + sub * span, dvs[sub])

        if dil <= 4:
            for r in range(dil):
                stream(r)
        else:
            def two_streams(i, carry):
                stream(2 * i)
                stream(2 * i + 1)
                return carry
            lax.fori_loop(0, dil // 2, two_streams, 0)

    rows_per_step = span * n
    qspec = pl.BlockSpec((rows_per_step, BLK), lambda b, p: (b, p))
    qnext = pl.BlockSpec((span, BLK), lambda b, p: (jnp.minimum((b + 1) * n, T // span - 1), p))
    cur = pl.BlockSpec((rows_per_step, BLK), lambda b, p: (b, p // group))
    prev = pl.BlockSpec((span, BLK), lambda b, p: (jnp.maximum(b * n - 1, 0), p // group))
    return _pallas(
        body, comm=comm, name=name, grid=(nb, P),
        in_specs=[qspec, qnext, qspec, qnext, qspec, qnext, qspec, qnext, cur, prev, cur, prev],
        out_specs=[qspec, cur, cur],
        out_shape=[S(q.shape, f32), S(k.shape, f32), S(k.shape, f32)],
        compiler_params=_cp("arbitrary", "arbitrary"),
    )(q, q, do, do, lse, lse, dd, dd, k, k, v, v)


def dil_combine_fwd(ols, *, tm, name):
    T = ols[0].shape[0]

    def body(o1, l1, o2, l2, o3, l3, out_ref):
        a, b, c = l1[...], l2[...], l3[...]
        m = jnp.maximum(jnp.maximum(a, b), c)
        ea, eb, ec = jnp.exp(a - m), jnp.exp(b - m), jnp.exp(c - m)
        out = (ea * o1[...] + eb * o2[...] + ec * o3[...]) / (ea + eb + ec)
        out_ref[...] = out.astype(MXU)

    row = pl.BlockSpec((tm, 512), lambda i: (i, 0))
    return pl.pallas_call(body, name=name, grid=(T // tm,), in_specs=[row] * 6, out_specs=row,
                          out_shape=S((T, A_W + B_W + C_W), MXU), compiler_params=_cp("arbitrary"))(*ols)


def dil_combine_bwd(ols, dmix, e, *, tm, name):
    T = ols[0].shape[0]

    def body(o1, l1, o2, l2, o3, l3, d_ref, e_ref, do1, do2, do3, dd1, dd2, dd3):
        a, b, c = l1[...], l2[...], l3[...]
        m = jnp.maximum(jnp.maximum(a, b), c)
        ea, eb, ec = jnp.exp(a - m), jnp.exp(b - m), jnp.exp(c - m)
        inv = 1.0 / (ea + eb + ec)
        wa, wb, wc = ea * inv, eb * inv, ec * inv
        dout = d_ref[...]
        gbar = _seg_sum(dout * (wa * o1[...] + wb * o2[...] + wc * o3[...]), e_ref)
        do1[...] = wa * dout
        do2[...] = wb * dout
        do3[...] = wc * dout
        dd1[...] = wa * gbar
        dd2[...] = wb * gbar
        dd3[...] = wc * gbar

    row = pl.BlockSpec((tm, 512), lambda i: (i, 0))
    return pl.pallas_call(
        body, name=name, grid=(T // tm,),
        in_specs=[row] * 6 + [row, pl.BlockSpec((512, 512), lambda i: (0, 0))],
        out_specs=[row] * 6,
        out_shape=[S((T, 512), f32)] * 6,
        compiler_params=_cp("arbitrary"),
    )(*ols, dmix, e)


def swa_pre_bwd(o, lse, dmix, sinks, e, *, tm, name):
    T = o.shape[0]
    ni = T // tm

    def body(o_ref, l_ref, d_ref, s_ref, e_ref, do_ref, dd_ref, ds_ref):
        i = pl.program_id(0)
        dout = d_ref[...]
        do_ref[...] = dout.astype(MXU)
        prod = dout * o_ref[...]
        dd = jnp.concatenate([_seg_sum(prod[:, 0:512], e_ref), _seg_sum(prod[:, 512:1024], e_ref)], axis=1)
        dd_ref[...] = dd
        part = -jnp.sum(jnp.exp(s_ref[...] - l_ref[...]) * dd, axis=0, keepdims=True)

        @pl.when(i == 0)
        def _():
            ds_ref[...] = part

        @pl.when(i > 0)
        def _():
            ds_ref[...] += part

    row = pl.BlockSpec((tm, 1024), lambda i: (i, 0))
    vec = pl.BlockSpec((1, 1024), lambda i: (0, 0))
    return pl.pallas_call(
        body, name=name, grid=(ni,),
        in_specs=[row, row, pl.BlockSpec((tm, 1024), lambda i: (i, 1)), vec, pl.BlockSpec((512, 512), lambda i: (0, 0))],
        out_specs=[row, row, vec],
        out_shape=[S((T, 1024), MXU), S((T, 1024), f32), S((1, 1024), f32)],
        compiler_params=_cp("arbitrary"),
    )(o, lse, dmix, sinks, e)


SHIFT_PAD = 24


def _shifted_copies(buf_ref, sh_ref, length):
    for r in range(1, 8):
        sh_ref[r - 1, 0:length, :] = buf_ref[pl.ds(r, length), :]


def _window(buf_ref, sh_ref, start, rows):
    q, r = divmod(start, 8)
    if r == 0:
        return buf_ref[pl.ds(8 * q, rows), :]
    return sh_ref[r - 1, pl.ds(8 * q, rows), :]


TAP_ROWS = 64


def _tap_sum(buf_ref, sh_ref, w_ref, starts, rows):
    outs = []
    for c0 in range(0, rows, TAP_ROWS):
        n = min(TAP_ROWS, rows - c0)
        acc = _window(buf_ref, sh_ref, starts[0] + c0, n) * w_ref[pl.ds(0, 1), :]
        for j in range(1, CONV_K):
            acc += _window(buf_ref, sh_ref, starts[j] + c0, n) * w_ref[pl.ds(j, 1), :]
        outs.append(acc)
    return jnp.concatenate(outs, axis=0)


def _conv_taps(buf_ref, sh_ref, w_ref, start, rows):
    return _tap_sum(buf_ref, sh_ref, w_ref, [start + j for j in range(CONV_K)], rows)


def conv_fwd(proj, mix, w, b, ln_g, ln_b, *, tb, name):
    T = proj.shape[0]
    hb = tb // HALO

    def body(u_ref, g_ref, up_ref, gp_ref, w_ref, b_ref, lg_ref, lb_ref, mix_ref, o_ref, hbuf, hsh):
        i = pl.program_id(0)
        hprev = up_ref[...] * _sigmoid(gp_ref[...])
        hbuf[0:HALO, :] = hprev * jnp.where(i > 0, 1.0, 0.0)
        hbuf[HALO:HALO + tb, :] = u_ref[...] * _sigmoid(g_ref[...])
        _shifted_copies(hbuf, hsh, tb + SHIFT_PAD)
        y = _conv_taps(hbuf, hsh, w_ref, HALO - (CONV_K - 1), tb) + b_ref[...]
        mu = jnp.mean(y, axis=-1, keepdims=True)
        yc = y - mu
        var = jnp.mean(yc * yc, axis=-1, keepdims=True)
        z = yc * lax.rsqrt(var + EPS) * lg_ref[...] + lb_ref[...]
        o_ref[...] = (z * _sigmoid(z)).astype(MXU)

    vec = pl.BlockSpec((1, 512), lambda i: (0, 0))
    return pl.pallas_call(
        body, name=name, grid=(T // tb,),
        in_specs=[pl.BlockSpec((tb, 512), lambda i: (i, 3)), pl.BlockSpec((tb, 512), lambda i: (i, 4)),
                  pl.BlockSpec((HALO, 512), lambda i: (jnp.maximum(i * hb - 1, 0), 3)),
                  pl.BlockSpec((HALO, 512), lambda i: (jnp.maximum(i * hb - 1, 0), 4)),
                  pl.BlockSpec((HALO, 512), lambda i: (0, 0)), vec, vec, vec, ANY],
        out_specs=pl.BlockSpec((tb, 512), lambda i: (i, 1)),
        out_shape=S(mix.shape, mix.dtype),
        scratch_shapes=[pltpu.VMEM((tb + HALO, 512), f32), pltpu.VMEM((7, tb + SHIFT_PAD, 512), f32)],
        compiler_params=_cp("arbitrary"), input_output_aliases={8: 0},
    )(proj, proj, proj, proj, w, b, ln_g, ln_b, mix)


def conv_bwd(proj, dmix, w, b, ln_g, ln_b, *, tb, name, comm=None):
    T = proj.shape[0]
    hb = tb // HALO
    ni = T // tb
    last_h = T // HALO - 1
    ext = tb + HALO

    def body(u_ref, g_ref, up_ref, gp_ref, un_ref, gn_ref, d_ref, dn_ref, w_ref, b_ref, lg_ref, lb_ref,
             du_ref, dg_ref, dw_ref, db_ref, dlg_ref, dlb_ref, hbuf, dybuf, hsh, dsh):
        i = pl.program_id(0)
        hbuf[0:HALO, :] = up_ref[...] * _sigmoid(gp_ref[...]) * jnp.where(i > 0, 1.0, 0.0)
        u = u_ref[...]
        sg = _sigmoid(g_ref[...])
        hbuf[HALO:HALO + tb, :] = u * sg
        hbuf[HALO + tb:HALO + ext, :] = un_ref[...] * _sigmoid(gn_ref[...])
        _shifted_copies(hbuf, hsh, ext + SHIFT_PAD)
        y = _conv_taps(hbuf, hsh, w_ref, HALO - (CONV_K - 1), ext) + b_ref[...]
        mu = jnp.mean(y, axis=-1, keepdims=True)
        yc = y - mu
        rstd = lax.rsqrt(jnp.mean(yc * yc, axis=-1, keepdims=True) + EPS)
        yn = yc * rstd
        z = yn * lg_ref[...] + lb_ref[...]
        sz = _sigmoid(z)
        row = lax.broadcasted_iota(jnp.int32, (ext, 1), 0)
        own = row < tb
        keep = row < jnp.where(i < ni - 1, ext, tb)
        dout = jnp.concatenate([d_ref[...], dn_ref[...]], axis=0)
        dz = jnp.where(keep, dout * (sz * (1.0 + z * (1.0 - sz))), 0.0)
        dyn = dz * lg_ref[...]
        dy = rstd * (dyn - jnp.mean(dyn, axis=-1, keepdims=True) - yn * jnp.mean(dyn * yn, axis=-1, keepdims=True))
        dybuf[...] = dy
        _shifted_copies(dybuf, dsh, tb + SHIFT_PAD)
        dz_own = jnp.where(own, dz, 0.0)
        dlg = jnp.sum(dz_own * yn, axis=0, keepdims=True)
        dlb = jnp.sum(dz_own, axis=0, keepdims=True)
        dy_own = dybuf[0:tb, :]
        dbias = jnp.sum(dy_own, axis=0, keepdims=True)
        dh = _tap_sum(dybuf, dsh, w_ref, [CONV_K - 1 - j for j in range(CONV_K)], tb)
        du_ref[...] = (dh * sg).astype(MXU)
        dg_ref[...] = (dh * u * sg * (1.0 - sg)).astype(MXU)
        taps = [jnp.sum(dy_own * _window(hbuf, hsh, HALO - (CONV_K - 1) + j, tb), axis=0, keepdims=True)
                for j in range(CONV_K)]
        taps.append(jnp.zeros((1, 512), f32))
        dwt = jnp.concatenate(taps, axis=0)

        @pl.when(i == 0)
        def _():
            dw_ref[...] = dwt
            db_ref[...] = dbias
            dlg_ref[...] = dlg
            dlb_ref[...] = dlb

        @pl.when(i > 0)
        def _():
            dw_ref[...] += dwt
            db_ref[...] += dbias
            dlg_ref[...] += dlg
            dlb_ref[...] += dlb

    vec = pl.BlockSpec((1, 512), lambda i: (0, 0))
    wspec = pl.BlockSpec((HALO, 512), lambda i: (0, 0))

    def halo_prev(col):
        return pl.BlockSpec((HALO, 512), lambda i: (jnp.maximum(i * hb - 1, 0), col))

    def halo_next(col):
        return pl.BlockSpec((HALO, 512), lambda i: (jnp.minimum((i + 1) * hb, last_h), col))

    row = pl.BlockSpec((tb, 512), lambda i: (i, 0))
    return _pallas(
        body, comm=comm, name=name, grid=(ni,),
        in_specs=[pl.BlockSpec((tb, 512), lambda i: (i, 3)), pl.BlockSpec((tb, 512), lambda i: (i, 4)),
                  halo_prev(3), halo_prev(4), halo_next(3), halo_next(4),
                  pl.BlockSpec((tb, 512), lambda i: (i, 1)), halo_next(1), wspec, vec, vec, vec],
        out_specs=[row, row, wspec, vec, vec, vec],
        out_shape=[S((T, 512), MXU), S((T, 512), MXU), S((HALO, 512), f32)] + [S((1, 512), f32)] * 3,
        scratch_shapes=[pltpu.VMEM((tb + 2 * HALO, 512), f32), pltpu.VMEM((ext, 512), f32),
                        pltpu.VMEM((7, ext + SHIFT_PAD, 512), f32), pltpu.VMEM((7, tb + SHIFT_PAD, 512), f32)],
        compiler_params=_cp("arbitrary"),
    )(proj, proj, proj, proj, proj, proj, dmix, dmix, w, b, ln_g, ln_b)


def _qk_norm_bwd(v, gain, dout, e_ref):
    r = lax.rsqrt(_seg_sum(v * v, e_ref) * (1.0 / HD) + EPS)
    y = v * r
    dgain = jnp.sum(dout * y, axis=0, keepdims=True)
    dy = dout * gain
    dv = r * (dy - y * (_seg_sum(dy * y, e_ref) * (1.0 / HD)))
    return dv, dgain


def prep_bwd(proj, e, gains, da, dc, dconv, *, tm, name):
    T = proj.shape[0]

    def body(*refs):
        p_ref, e_ref, gaq, gak, gcq, gck = refs[0:6]
        a_refs = refs[6:15]
        dcq, dckk, dcvv, du, dgt = refs[15:20]
        dp, gaq_o, gak_o, gcq_o, gck_o = refs[20:]
        i = pl.program_id(0)
        dq = a_refs[0][...] + a_refs[3][...] + a_refs[6][...]
        dk = a_refs[1][...] + a_refs[4][...] + a_refs[7][...]
        dv = a_refs[2][...] + a_refs[5][...] + a_refs[8][...]
        d, g_aq = _qk_norm_bwd(p_ref[:, 0:512], gaq[...], dq, e_ref)
        dp[:, 0:512] = d.astype(MXU)
        d, g_ak = _qk_norm_bwd(p_ref[:, 512:1024], gak[...], dk, e_ref)
        dp[:, 512:1024] = d.astype(MXU)
        dp[:, 1024:1536] = dv.astype(MXU)
        dp[:, 1536:2048] = du[...]
        dp[:, 2048:2560] = dgt[...]
        d, g_cq0 = _qk_norm_bwd(p_ref[:, 2560:3072], gcq[...], dcq[:, 0:512], e_ref)
        dp[:, 2560:3072] = d.astype(MXU)
        d, g_cq1 = _qk_norm_bwd(p_ref[:, 3072:3584], gcq[...], dcq[:, 512:1024], e_ref)
        dp[:, 3072:3584] = d.astype(MXU)
        lo = lax.broadcasted_iota(jnp.int32, (tm, 128), 1) < HD

        def fold(ref):
            g0, g1 = ref[:, 0:128], ref[:, 128:256]
            s0 = g0 + pltpu.roll(g0, HD, 1)
            s1 = g1 + pltpu.roll(g1, HD, 1)
            return jnp.where(lo, s0, s1)

        dkn = fold(dckk)
        kraw = p_ref[:, 3584:3712]
        r = lax.rsqrt(_seg_sum128(kraw * kraw, e_ref) * (1.0 / HD) + EPS)
        y = kraw * r
        g_ck = jnp.sum(dkn * y, axis=0, keepdims=True)
        dy = dkn * gck[...]
        dp[:, 3584:3712] = (r * (dy - y * (_seg_sum128(dy * y, e_ref) * (1.0 / HD)))).astype(MXU)
        dp[:, 3712:3840] = fold(dcvv).astype(MXU)
        g_cq = jnp.concatenate([g_cq0, g_cq1], axis=1)

        @pl.when(i == 0)
        def _():
            gaq_o[...] = g_aq
            gak_o[...] = g_ak
            gcq_o[...] = g_cq
            gck_o[...] = g_ck

        @pl.when(i > 0)
        def _():
            gaq_o[...] += g_aq
            gak_o[...] += g_ak
            gcq_o[...] += g_cq
            gck_o[...] += g_ck

    def vec(n):
        return pl.BlockSpec((1, n), lambda i: (0, 0))

    def rows(n):
        return pl.BlockSpec((tm, n), lambda i: (i, 0))

    return pl.pallas_call(
        body, name=name, grid=(T // tm,),
        in_specs=[rows(IN_W), pl.BlockSpec((512, 512), lambda i: (0, 0)), vec(512), vec(512), vec(512), vec(128)]
        + [rows(512)] * 9 + [rows(1024), rows(256), rows(256), rows(512), rows(512)],
        out_specs=[rows(IN_W), vec(512), vec(512), vec(1024), vec(128)],
        out_shape=[S((T, IN_W), MXU), S((1, 512), f32), S((1, 512), f32), S((1, 1024), f32), S((1, 128), f32)],
        compiler_params=_cp("arbitrary"),
    )(proj, e, *gains, *da, *dc, *dconv)


def adamw(w, m, v, pieces, *, tr, name, comm=None):
    n, R, C = w.shape
    c1 = 1.0 - ADAM_B1 ** ADAM_STEP
    c2 = 1.0 - ADAM_B2 ** ADAM_STEP
    npc = len(pieces)

    def body(*refs):
        w_ref, m_ref, v_ref = refs[0:3]
        p_refs = refs[3:3 + npc]
        g_ref, d_ref, mo_ref, vo_ref = refs[3 + npc:]
        g = p_refs[0][...].astype(f32)
        for p in p_refs[1:]:
            g = g + p[...].astype(f32)
        mn = ADAM_B1 * m_ref[...] + (1.0 - ADAM_B1) * g
        vn = ADAM_B2 * v_ref[...] + (1.0 - ADAM_B2) * (g * g)
        g_ref[...] = g
        mo_ref[...] = mn
        vo_ref[...] = vn
        d_ref[...] = -ADAM_LR * ((mn / c1) / (jnp.sqrt(vn / c2) + ADAM_EPS) + ADAM_WD * w_ref[...])

    blk = pl.BlockSpec((None, tr, C), lambda l, i: (l, i, 0))
    return _pallas(
        body, comm=comm, name=name, grid=(n, R // tr), in_specs=[blk] * (3 + npc), out_specs=[blk] * 4,
        out_shape=[S(w.shape, f32)] * 4, compiler_params=_cp("arbitrary", "arbitrary"),
    )(w, m, v, *pieces)


def add_halves(pieces, other, *, tr, name):
    _, _, r, cc = pieces.shape

    def body(c_ref, a_ref, b_ref, o_ref):
        o_ref[...] = (a_ref[...] + b_ref[...]).astype(jnp.bfloat16)

    blk = pl.BlockSpec((None, tr, cc), lambda s, i, c_ref: (s, i, 0))
    grid_spec = pltpu.PrefetchScalarGridSpec(
        num_scalar_prefetch=1, grid=(4, r // tr),
        in_specs=[pl.BlockSpec((None, None, tr, cc), lambda s, i, c_ref: (s, c_ref[0], i, 0)), blk], out_specs=blk)
    core = lax.axis_index("c").astype(jnp.int32).reshape(1)
    return pl.pallas_call(body, name=name, grid_spec=grid_spec, out_shape=S((4, r, cc), jnp.bfloat16),
                          compiler_params=_cp("arbitrary", "arbitrary"))(core, pieces, other)


def sum8(parts, *, name):
    _, R, C = parts.shape

    def body(p_ref, o_ref):
        acc = p_ref[0]
        for d in range(1, 8):
            acc = acc + p_ref[d]
        o_ref[...] = acc

    return pl.pallas_call(body, name=name, out_shape=S((R, C), f32))(parts)


def _pos():
    return lax.axis_index("x"), lax.axis_index("y"), lax.axis_index("c")


def _other_chips(x, y):
    return [(1 - x, y), (x, 1 - y), (1 - x, 1 - y)]


class GatherComm:
    def __init__(self, shards, in_place=None):
        self.ins = list(shards)
        self.nt = nt = len(shards)
        self.in_place = list(in_place) if in_place is not None else [False] * nt
        self.out_shapes = [S((2 * s.shape[1], 4 * s.shape[2]), s.dtype) if ip else S((4,) + s.shape, s.dtype)
                           for s, ip in zip(shards, self.in_place)]
        self.sem_shapes = [pltpu.SemaphoreType.DMA((nt, 6)), pltpu.SemaphoreType.DMA((nt, 6)),
                           pltpu.SemaphoreType.DMA((nt, 2))]
        self.results = None

    def _place(self, couts, t, chip, half):
        cid = 2 * chip[0] + chip[1]
        if not self.in_place[t]:
            return couts[t].at[cid, half]
        _, r, c = self.ins[t].shape
        row0 = half * r if isinstance(half, int) else pl.multiple_of(half * r, 16)
        return couts[t].at[pl.ds(row0, r), pl.ds(pl.multiple_of(cid * c, 128), c)]

    def _copy(self, couts, sems, t, k, chip, half, to, src=None):
        dst = self._place(couts, t, chip, half)
        return pltpu.make_async_remote_copy(
            src_ref=dst if src is None else src, dst_ref=dst,
            send_sem=sems[0].at[t, k], recv_sem=sems[1].at[t, k], device_id=to, device_id_type=MESH)

    def _local(self, cins, couts, sems, t):
        x, y, _ = _pos()
        return [pltpu.make_async_copy(cins[t].at[half], self._place(couts, t, (x, y), half), sems[2].at[t, half])
                for half in range(2)]

    def start(self, cins, couts, sems):
        x, y, c = _pos()
        for t in range(self.nt):
            for cp in self._local(cins, couts, sems, t):
                cp.start()
            for j, chip in enumerate(_other_chips(x, y)):
                self._copy(couts, sems, t, j, (x, y), c, (*chip, c), src=cins[t].at[c]).start()

    def mid(self, cins, couts, sems):
        x, y, c = _pos()
        for t in range(self.nt):
            for j, chip in enumerate(_other_chips(x, y)):
                self._copy(couts, sems, t, j, chip, c, (x, y, c)).wait_recv()
                self._copy(couts, sems, t, 3 + j, chip, c, (x, y, 1 - c)).start()

    def finish(self, cins, couts, sems):
        x, y, c = _pos()
        for t in range(self.nt):
            for j, chip in enumerate(_other_chips(x, y)):
                self._copy(couts, sems, t, 3 + j, chip, 1 - c, (x, y, c)).wait_recv()
        for t in range(self.nt):
            for j, chip in enumerate(_other_chips(x, y)):
                self._copy(couts, sems, t, j, (x, y), c, (*chip, c), src=cins[t].at[c]).wait_send()
                self._copy(couts, sems, t, 3 + j, chip, c, (x, y, 1 - c)).wait_send()
            for cp in self._local(cins, couts, sems, t):
                cp.wait()


class SwapComm:
    def __init__(self, pieces):
        self.ins = list(pieces)
        self.nt = nt = len(pieces)
        self.out_shapes = [S((4,) + p.shape[2:], p.dtype) for p in pieces]
        self.sem_shapes = [pltpu.SemaphoreType.DMA((nt, 4)), pltpu.SemaphoreType.DMA((nt, 4))]
        self.results = None

    def _copies(self, cins, couts, sems):
        x, y, c = _pos()
        return [pltpu.make_async_remote_copy(src_ref=cins[t].at[s, 1 - c], dst_ref=couts[t].at[s],
                                             send_sem=sems[0].at[t, s], recv_sem=sems[1].at[t, s],
                                             device_id=(x, y, 1 - c), device_id_type=MESH)
                for t in range(self.nt) for s in range(4)]

    def start(self, cins, couts, sems):
        for cp in self._copies(cins, couts, sems):
            cp.start()

    def mid(self, cins, couts, sems):
        pass

    def finish(self, cins, couts, sems):
        for cp in self._copies(cins, couts, sems):
            cp.wait()


class ExchangeComm:
    def __init__(self, arrs):
        self.ins = list(arrs)
        self.nt = nt = len(arrs)
        self.out_shapes = [S((2,) + a.shape, a.dtype) for a in arrs]
        self.sem_shapes = [pltpu.SemaphoreType.DMA((nt, 7)), pltpu.SemaphoreType.DMA((nt, 7)),
                           pltpu.SemaphoreType.DMA((nt,))]
        self.results = None

    def _copy(self, couts, sems, t, k, half, src_chip, to, src=None):
        dst = couts[t].at[half, src_chip]
        return pltpu.make_async_remote_copy(
            src_ref=dst if src is None else src, dst_ref=dst,
            send_sem=sems[0].at[t, k], recv_sem=sems[1].at[t, k], device_id=to, device_id_type=MESH)

    def _local(self, cins, couts, sems, t):
        x, y, c = _pos()
        return pltpu.make_async_copy(cins[t].at[2 * x + y], couts[t].at[c, 2 * x + y], sems[2].at[t])

    def _firsts(self, cins, couts, sems, t):
        x, y, c = _pos()
        me = 2 * x + y
        cps = [self._copy(couts, sems, t, j, c, me, (*chip, c), src=cins[t].at[2 * chip[0] + chip[1]])
               for j, chip in enumerate(_other_chips(x, y))]
        return cps + [self._copy(couts, sems, t, 6, c, me, (x, y, 1 - c), src=cins[t].at[me])]

    def start(self, cins, couts, sems):
        for t in range(self.nt):
            self._local(cins, couts, sems, t).start()
            for cp in self._firsts(cins, couts, sems, t):
                cp.start()

    def mid(self, cins, couts, sems):
        x, y, c = _pos()
        for t in range(self.nt):
            for j, chip in enumerate(_other_chips(x, y)):
                cid = 2 * chip[0] + chip[1]
                self._copy(couts, sems, t, j, c, cid, (x, y, c)).wait_recv()
                self._copy(couts, sems, t, 3 + j, c, cid, (x, y, 1 - c)).start()

    def finish(self, cins, couts, sems):
        x, y, c = _pos()
        for t in range(self.nt):
            for j, chip in enumerate(_other_chips(x, y)):
                self._copy(couts, sems, t, 3 + j, 1 - c, 2 * chip[0] + chip[1], (x, y, c)).wait_recv()
            self._copy(couts, sems, t, 6, 1 - c, 2 * x + y, (x, y, c)).wait_recv()
        for t in range(self.nt):
            for cp in self._firsts(cins, couts, sems, t):
                cp.wait_send()
            for j, chip in enumerate(_other_chips(x, y)):
                self._copy(couts, sems, t, 3 + j, c, 2 * chip[0] + chip[1], (x, y, 1 - c)).wait_send()
            self._local(cins, couts, sems, t).wait()


def gather_small(vec, *, name):
    R, C = vec.shape

    def body(v_ref, out_ref, send_sems, recv_sems):
        x, y, c = _pos()
        me = 4 * x + 2 * y + c
        out_ref[me] = v_ref[...]
        cps = []
        def peer(k):
            fx, fy, fc = (k >> 2) & 1, (k >> 1) & 1, k & 1
            return (1 - x if fx else x), (1 - y if fy else y), (1 - c if fc else c)

        for k in range(1, 8):
            cp = pltpu.make_async_remote_copy(src_ref=v_ref, dst_ref=out_ref.at[me], send_sem=send_sems.at[k - 1],
                                              recv_sem=recv_sems.at[k - 1], device_id=peer(k), device_id_type=MESH)
            cp.start()
            cps.append(cp)
        for k in range(1, 8):
            px, py, pc = peer(k)
            pltpu.make_async_remote_copy(src_ref=v_ref, dst_ref=out_ref.at[4 * px + 2 * py + pc],
                                         send_sem=send_sems.at[k - 1], recv_sem=recv_sems.at[k - 1],
                                         device_id=(px, py, pc), device_id_type=MESH).wait_recv()
        for cp in cps:
            cp.wait_send()

    return pl.pallas_call(
        body, name=name,
        in_specs=[pl.BlockSpec(memory_space=pltpu.VMEM)], out_specs=pl.BlockSpec(memory_space=pltpu.VMEM),
        out_shape=S((8, R, C), vec.dtype),
        scratch_shapes=[pltpu.SemaphoreType.DMA((7,)), pltpu.SemaphoreType.DMA((7,))],
    )(vec)


def _tile(n, prefs):
    for p in prefs:
        if n % p == 0:
            return p
    return n


def _lanes(g, reps):
    return jnp.tile(g.reshape(1, -1), (1, reps))


class _NoRide:
    def rider(self, name):
        return None

    def landed(self, comm):
        pass

    def grad(self, name, val):
        pass


def _layer_fwd(x, p, e, ride=_NoRide()):
    T, D = x.shape
    tm = _tile(T, (512, 256, 128))
    tall = _tile(T, (1024, 512, 256, 128))

    def carried(fn, *args, name, **kw):
        comm = ride.rider(name)
        out = fn(*args, name=name, comm=comm, **kw)
        ride.landed(comm)
        return out

    h, proj = carried(rms_proj, x, p["norm1_g"], p["w_in"], tm=tall, tn=_tile(IN_W, (768,)), name="rms_proj")
    gains = (_lanes(p["a_q_g"], 8), _lanes(p["a_k_g"], 8), _lanes(p["c_q_g"], 8), _lanes(p["c_k_g"], 2))
    aq, ak, av, cq, ckk, cvv = prep_fwd(proj, e, gains, tm=_tile(T, (256, 128)), name="prep_fwd")
    ols = []
    for d in DILATIONS:
        ols += carried(band_attn_fwd, aq, ak, av, None, dil=d, max_dist=A_DIST, group=1, name=f"dil_attn_fwd_{d}")
    mix = dil_combine_fwd(ols, tm=tm, name="dil_combine_fwd")
    mix = conv_fwd(proj, mix, p["conv_w"], p["conv_b"], p["conv_ln_g"], p["conv_ln_b"], tb=tm, name="conv_fwd")
    sinks = jnp.repeat(p["c_sinks"].reshape(-1), HD).reshape(1, C_W)
    o_c, l_c, mix = carried(band_attn_fwd, cq, ckk, cvv, sinks, dil=1, max_dist=C_DIST, group=4, mix=mix,
                            name="swa_attn_fwd")
    x1 = carried(matmul_res, mix, p["w_out"], x, tm=tall, tn=_tile(D, (1024, 512, 256)), name="out_proj")
    F = p["w_gate"].shape[1]
    h2, gate, up, act = carried(rms_swiglu, x1, p["norm2_g"], p["w_gate"], p["w_up"], tm=tall,
                                tn=_tile(F, (512, 256, 128)), name="rms_swiglu")
    x2 = carried(matmul_res, act, p["w_down"], x1, tm=tall, tn=_tile(D, (512, 256)), name="ffn_down")
    saved = dict(x=x, h=h, proj=proj, gains=gains, aq=aq, ak=ak, av=av, cq=cq, ckk=ckk, cvv=cvv, ols=ols, o_c=o_c,
                 l_c=l_c, sinks=sinks, mix=mix, x1=x1, h2=h2, gate=gate, up=up, act=act)
    return x2, saved


def _layer_bwd(dx2, dx2b, p, s, e, ride=_NoRide()):
    T, D = dx2.shape
    F = p["w_gate"].shape[1]
    tm = _tile(T, (512, 256, 128))
    tmm = _tile(T, (1024, 512, 256, 128))
    tkT = _tile(T, (2048, 1024, 512))
    tF = _tile(F, (512, 256, 128))
    tD = _tile(D, (1024, 512, 256))
    g = {}

    def carried(fn, *args, name, **kw):
        comm = ride.rider(name)
        out = fn(*args, name=name, comm=comm, **kw)
        ride.landed(comm)
        return out

    def big(n, val):
        g[n] = val
        ride.grad(n, val)

    d_gate, d_up = carried(nt_swiglu_bwd, dx2b, p["w_down"], s["gate"], s["up"], tm=tmm, tn=tF, name="ffn_down_bwd")
    big("w_down", tn_matmul(s["act"], dx2b, tm=tF, tn=tD, tk=tkT, name="grad_w_down"))
    big("w_gate", carried(tn_matmul, s["h2"], d_gate, tm=tD, tn=tF, tk=tkT, by_chip=True, name="grad_w_gate"))
    big("w_up", carried(tn_matmul, s["h2"], d_up, tm=tD, tn=tF, tk=tkT, by_chip=True, name="grad_w_up"))
    dx1, dx1b, g["norm2_g"] = carried(nt_rms_bwd, [(d_gate, p["w_gate"]), (d_up, p["w_up"])], s["x1"], p["norm2_g"],
                                      dx2, tm=tm, tn=_tile(D, (512, 256)), name="ffn_in_bwd")
    dmix = nt_plain(dx1b, p["w_out"], tm=tmm, tn=tD, name="out_proj_bwd")
    big("w_out", tn_matmul(s["mix"], dx1b, tm=1024, tn=tD, tk=tkT, name="grad_w_out"))
    dos = dil_combine_bwd(s["ols"], dmix, e, tm=tm, name="dil_combine_bwd")
    da = []
    for n, d in enumerate(DILATIONS):
        da += carried(band_attn_bwd, s["aq"], s["ak"], s["av"], s["ols"][2 * n + 1], dos[n], dos[3 + n], dil=d,
                      max_dist=A_DIST, group=1, name=f"dil_attn_bwd_{d}")
    du, dgt, gw, gb, glg, glb = carried(conv_bwd, s["proj"], dmix, p["conv_w"], p["conv_b"], p["conv_ln_g"],
                                        p["conv_ln_b"], tb=tm, name="conv_bwd")
    g["conv_w"], g["conv_b"], g["conv_ln_g"], g["conv_ln_b"] = gw[:CONV_K], gb, glg, glb
    do_c, dd_c, dsink = swa_pre_bwd(s["o_c"], s["l_c"], dmix, s["sinks"], e, tm=_tile(T, (256, 128)), name="swa_pre_bwd")
    g["c_sinks"] = dsink.reshape(-1, HD)[:, 0]
    dcq, dckk, dcvv = carried(band_attn_bwd, s["cq"], s["ckk"], s["cvv"], s["l_c"], do_c, dd_c, dil=1, max_dist=C_DIST,
                              group=4, name="swa_attn_bwd")
    dproj, gaq, gak, gcq, gck = prep_bwd(s["proj"], e, s["gains"], da, (dcq, dckk, dcvv), (du, dgt),
                                         tm=_tile(T, (256, 128)), name="prep_bwd")
    g["a_q_g"] = gaq.reshape(-1, HD).sum(0)
    g["a_k_g"] = gak.reshape(-1, HD).sum(0)
    g["c_q_g"] = gcq.reshape(-1, HD).sum(0)
    g["c_k_g"] = gck.reshape(-1, HD).sum(0)
    big("w_in", tn_matmul(s["h"], dproj, tm=tD, tn=_tile(IN_W, (1280,)), tk=tkT, name="grad_w_in"))
    dx, dxb, g["norm1_g"] = carried(nt_rms_bwd, [(dproj, p["w_in"])], s["x"], p["norm1_g"], dx1, tm=tmm,
                                    tn=_tile(D, (512, 256)), name="in_proj_bwd")
    return dx, dxb, g


BIG = ("w_in", "w_out", "w_gate", "w_up", "w_down")
COL_SHARDED = ("w_in", "w_gate", "w_up")
SMALL = ("norm1_g", "a_q_g", "a_k_g", "conv_w", "conv_b", "conv_ln_g", "conv_ln_b", "c_q_g", "c_k_g", "c_sinks", "norm2_g")


def _to_pieces(g, name):
    if g.ndim == 3:
        return g.reshape(4, 2, g.shape[1] // 2, g.shape[2])
    R, C = g.shape
    if name in COL_SHARDED:
        return g.reshape(2, R // 2, 4, C // 4).transpose(2, 0, 1, 3)
    return g.reshape(4, 2, R // 8, C)


def _from_gathered(w, name):
    _, _, r, c = w.shape
    if name in COL_SHARDED:
        return w.transpose(1, 2, 0, 3).reshape(2 * r, 4 * c)
    return w.reshape(8 * r, c)


def _shard(W, l, n):
    w = W[n][l]
    return w.astype(MXU).reshape(2, w.shape[0] // 2, w.shape[1])


class _LayerParams(dict):
    def __init__(self, layer, full, small):
        super().__init__(small)
        self.layer, self.full = layer, full

    def __missing__(self, n):
        return self.full[(self.layer, n)]


FWD_RIDES = {
    (0, "rms_proj"): ((0, "w_out"), (0, "w_gate")),
    (0, "swa_attn_fwd"): ((0, "w_up"),),
    (0, "rms_swiglu"): ((0, "w_down"), (1, "w_in"), (1, "w_out")),
    (0, "ffn_down"): ((1, "w_gate"),),
    (1, "rms_proj"): ((1, "w_up"),),
    (1, "rms_swiglu"): ((1, "w_down"),),
}
BWD_RIDES = {
    "grad_w_gate": (("w_down",), ()),
    "grad_w_up": (("w_gate",), ("w_down",)),
    "ffn_in_bwd": (("w_up",), ("w_gate",)),
    "dil_attn_bwd_1": (("w_out",), ()),
    "dil_attn_bwd_16": ((), ("w_up",)),
    "conv_bwd": ((), ("w_out",)),
    "in_proj_bwd": (("w_in",), ()),
}
IN_PLACE = ("w_gate", "w_up")


class _FwdRide:
    def __init__(self, layer, W, full):
        self.layer, self.W, self.full = layer, W, full

    def rider(self, name):
        keys = FWD_RIDES.get((self.layer, name))
        if not keys:
            return None
        comm = GatherComm([_shard(self.W, l, n) for l, n in keys], [n in IN_PLACE for _, n in keys])
        comm.keys = keys
        return comm

    def landed(self, comm):
        if comm is not None:
            for (l, n), g in zip(comm.keys, comm.results):
                self.full[(l, n)] = g if n in IN_PLACE else _from_gathered(g, n)


class _GradFlow:
    def __init__(self):
        self.pieces, self.sums, self.landed, self.pending = {}, {}, {}, []

    def swap(self, keys):
        comm = SwapComm([self.pieces[k] for k in keys])
        comm.keys, comm.kind = list(keys), "swap"
        return comm

    def exchange(self, keys):
        if not keys:
            return None
        comm = ExchangeComm([self.sums[k] for k in keys])
        comm.keys, comm.kind = list(keys), "exchange"
        return comm

    def take_pending(self):
        keys, self.pending = self.pending, []
        return keys

    def land(self, comm):
        if comm is None:
            return
        if isinstance(comm, MultiComm):
            for sub in comm.comms:
                self.land(sub)
            return
        for k, res in zip(comm.keys, comm.results):
            if comm.kind == "swap":
                r = res.shape[1]
                self.sums[k] = add_halves(self.pieces[k], res, tr=_tile(r, (256, 176, 128, 64, 32, 16)),
                                          name="grad_chip_sum")
                if k[1] == "w_in":
                    self.pending.append(k)
            else:
                self.landed[k] = res


class _BwdRide:
    def __init__(self, layer, flow):
        self.layer, self.flow = layer, flow
        self.final = layer == 0

    def grad(self, name, val):
        self.flow.pieces[(self.layer, name)] = _to_pieces(val, name)
        if self.final and name == "w_in":
            swap = self.flow.swap([(self.layer, name)])
            _run_comm(swap, name="swap_last")
            self.flow.land(swap)

    def rider(self, name):
        if name == "ffn_down_bwd" or (self.final and name == "in_proj_bwd"):
            return self.flow.exchange(self.flow.take_pending())
        swaps, exchanges = BWD_RIDES.get(name, ((), ()))
        comms = []
        if swaps:
            comms.append(self.flow.swap([(self.layer, n) for n in swaps]))
        if exchanges:
            comms.append(self.flow.exchange([(self.layer, n) for n in exchanges]))
        return MultiComm(comms) if comms else None

    def landed(self, comm):
        self.flow.land(comm)


def _pack(items, rows):
    flat = jnp.concatenate([a.reshape(-1).astype(f32) for a in items])
    return jnp.pad(flat, (0, rows * 128 - flat.shape[0])).reshape(rows, 128)


def _unpack(packed, shapes):
    flat = packed.reshape(-1)
    out, off = [], 0
    for shp in shapes:
        n = 1
        for d in shp:
            n *= d
        out.append(flat[off:off + n].reshape(shp))
        off += n
    return out


def kernel(x, norm1_g, w_in, a_q_g, a_k_g, conv_w, conv_b, conv_ln_g, conv_ln_b, c_q_g, c_k_g, c_sinks, w_out, norm2_g, w_gate, w_up, w_down, loss_target, m_norm1_g, m_w_in, m_a_q_g, m_a_k_g, m_conv_w, m_conv_b, m_conv_ln_g, m_conv_ln_b, m_c_q_g, m_c_k_g, m_c_sinks, m_w_out, m_norm2_g, m_w_gate, m_w_up, m_w_down, v_norm1_g, v_w_in, v_a_q_g, v_a_k_g, v_conv_w, v_conv_b, v_conv_ln_g, v_conv_ln_b, v_c_q_g, v_c_k_g, v_c_sinks, v_w_out, v_norm2_g, v_w_gate, v_w_up, v_w_down):
    W = dict(norm1_g=norm1_g, w_in=w_in, a_q_g=a_q_g, a_k_g=a_k_g, conv_w=conv_w, conv_b=conv_b, conv_ln_g=conv_ln_g,
             conv_ln_b=conv_ln_b, c_q_g=c_q_g, c_k_g=c_k_g, c_sinks=c_sinks, w_out=w_out, norm2_g=norm2_g, w_gate=w_gate,
             w_up=w_up, w_down=w_down)
    M = dict(norm1_g=m_norm1_g, w_in=m_w_in, a_q_g=m_a_q_g, a_k_g=m_a_k_g, conv_w=m_conv_w, conv_b=m_conv_b,
             conv_ln_g=m_conv_ln_g, conv_ln_b=m_conv_ln_b, c_q_g=m_c_q_g, c_k_g=m_c_k_g, c_sinks=m_c_sinks, w_out=m_w_out,
             norm2_g=m_norm2_g, w_gate=m_w_gate, w_up=m_w_up, w_down=m_w_down)
    V = dict(norm1_g=v_norm1_g, w_in=v_w_in, a_q_g=v_a_q_g, a_k_g=v_a_k_g, conv_w=v_conv_w, conv_b=v_conv_b,
             conv_ln_g=v_conv_ln_g, conv_ln_b=v_conv_ln_b, c_q_g=v_c_q_g, c_k_g=v_c_k_g, c_sinks=v_c_sinks, w_out=v_w_out,
             norm2_g=v_norm2_g, w_gate=v_w_gate, w_up=v_w_up, w_down=v_w_down)
    depth = norm1_g.shape[0]
    T, D = x.shape[1], x.shape[2]
    xs = x.reshape(T, D)
    chip = 2 * lax.axis_index("x") + lax.axis_index("y")
    e = _head_eye()

    full = {}
    first = GatherComm([_shard(W, 0, "w_in"), conv_w])
    _run_comm(first, name="gather_first")
    full[(0, "w_in")] = _from_gathered(first.results[0], "w_in")
    conv_full = first.results[1].transpose(1, 2, 0, 3).reshape(depth, CONV_K, B_W)
    params = []
    for l in range(depth):
        small = {n: W[n][l].reshape(1, -1) for n in SMALL if n != "conv_w"}
        small["conv_w"] = jnp.pad(conv_full[l], ((0, HALO - CONV_K), (0, 0)))
        params.append(_LayerParams(l, full, small))

    saved = []
    act = xs
    for l in range(depth):
        act, s = _layer_fwd(act, params[l], e, _FwdRide(l, W, full))
        saved.append(s)
    dy, dyb, loss_part = loss_head(act, loss_target.reshape(T, D), tm=_tile(T, (512, 256, 128)), name="loss_head")
    grads = [None] * depth
    flow = _GradFlow()
    for l in reversed(range(depth)):
        dy, dyb, grads[l] = _layer_bwd(dy, dyb, params[l], saved[l], e, _BwdRide(l, flow))
    grad_x = dy.reshape(x.shape)

    out = {}
    for n in ("w_down", "w_gate", "w_up", "w_out", "w_in"):
        last = flow.exchange(flow.take_pending())
        per_layer = [flow.landed[(l, n)] for l in range(depth)]
        r, cc = per_layer[0].shape[2], per_layer[0].shape[3]
        srcs = [jnp.stack([pl_[:, s].reshape(2 * r, cc) for pl_ in per_layer]) for s in range(4)]
        out[n] = adamw(W[n], M[n], V[n], srcs, tr=_tile(2 * r, (256, 176, 128, 64, 32, 16)), name="adamw_" + n,
                       comm=last)
        flow.land(last)

    small_shapes = []
    items = []
    for l in range(depth):
        for n in SMALL:
            a = grads[l][n]
            if n == "conv_w":
                a = a.reshape(CONV_K, 4, B_W // 4).transpose(1, 0, 2)
            items.append(a)
            small_shapes.append(a.shape)
    items.append(loss_part[0, 0:1])
    small_shapes.append((1,))
    total = sum(int(jnp.size(a)) for a in items)
    rows = -(-total // 1024) * 8
    summed = sum8(gather_small(_pack(items, rows), name="gather_small"), name="sum_small")
    parts = _unpack(summed, small_shapes)
    loss = parts[-1][0]
    small_g = {n: [] for n in SMALL}
    for l in range(depth):
        for i, n in enumerate(SMALL):
            a = parts[l * len(SMALL) + i]
            if n == "conv_w":
                a = lax.dynamic_index_in_dim(a, chip, axis=0, keepdims=False)
            small_g[n].append(a.reshape(W[n].shape[1:]))
    sw = [W[n] for n in SMALL]
    sm = [M[n] for n in SMALL]
    sv = [V[n] for n in SMALL]
    sg = [jnp.stack(small_g[n]) for n in SMALL]
    tot2 = sum(int(jnp.size(a)) for a in sw)
    rows2 = -(-tot2 // 1024) * 8
    res = adamw(_pack(sw, rows2)[None], _pack(sm, rows2)[None], _pack(sv, rows2)[None], [_pack(sg, rows2)[None]],
                tr=rows2, name="adamw_small")
    shapes2 = [a.shape for a in sw]
    small_out = [_unpack(r[0], shapes2) for r in res]
    for i, n in enumerate(SMALL):
        out[n] = [small_out[k][i] for k in range(4)]

    order = ("norm1_g", "w_in", "a_q_g", "a_k_g", "conv_w", "conv_b", "conv_ln_g", "conv_ln_b", "c_q_g", "c_k_g",
             "c_sinks", "w_out", "norm2_g", "w_gate", "w_up", "w_down")
    return (loss, grad_x, *[out[n][0] for n in order], *[out[n][1] for n in order], *[out[n][2] for n in order],
            *[out[n][3] for n in order])
```

```python
import functools

import jax
import jax.numpy as jnp
from jax import lax
from jax.experimental import pallas as pl
from jax.experimental.pallas import tpu as pltpu

f32 = jnp.float32
MXU = jnp.bfloat16
S = jax.ShapeDtypeStruct
MESH = pl.DeviceIdType.MESH

EPS = 1e-6
NEG = -1e30
HD = 64
BLK = 128
A_W, B_W, C_W = 512, 512, 1024
KV_W = 128
IN_W = 3 * A_W + 2 * B_W + C_W + 2 * KV_W
CONV_K = 31
HALO = 32
DILATIONS = (1, 4, 16)
A_DIST, C_DIST = 128, 127
SCALE = HD ** -0.5
VMEM_LIMIT = 56 * 1024 * 1024
VMEM_TALL = 62 * 1024 * 1024

ADAM_LR, ADAM_B1, ADAM_B2, ADAM_EPS, ADAM_WD, ADAM_STEP = 0.001, 0.9, 0.999, 1e-08, 0.01, 10


def _cp(*sem, vmem=VMEM_LIMIT):
    return pltpu.CompilerParams(dimension_semantics=sem, vmem_limit_bytes=vmem)


ANY = pl.BlockSpec(memory_space=pl.ANY)


def _pallas(body, *, comm=None, name, grid, in_specs, out_specs, out_shape, scratch_shapes=(), compiler_params,
            input_output_aliases=None):
    aliases = dict(input_output_aliases or {})
    if comm is None:
        return pl.pallas_call(body, name=name, grid=grid, in_specs=in_specs, out_specs=out_specs, out_shape=out_shape,
                              scratch_shapes=list(scratch_shapes), compiler_params=compiler_params,
                              input_output_aliases=aliases)
    single = not isinstance(out_shape, (list, tuple))
    o_shapes = [out_shape] if single else list(out_shape)
    o_specs = [out_specs] if single else list(out_specs)
    n_in, n_out, n_sc = len(in_specs), len(o_shapes), len(scratch_shapes)
    nci, nco = len(comm.ins), len(comm.out_shapes)
    total = 1
    for g in grid:
        total *= g

    def carried(*refs):
        ins, cins = refs[:n_in], refs[n_in:n_in + nci]
        o0 = n_in + nci
        outs, couts = refs[o0:o0 + n_out], refs[o0 + n_out:o0 + n_out + nco]
        s0 = o0 + n_out + nco
        scratch, sems = refs[s0:s0 + n_sc], refs[s0 + n_sc:]
        step = pl.program_id(0)
        for axis in range(1, len(grid)):
            step = step * grid[axis] + pl.program_id(axis)

        @pl.when(step == 0)
        def _():
            comm.start(cins, couts, sems)

        body(*ins, *outs, *scratch)

        @pl.when(step == (3 * total) // 4)
        def _():
            comm.mid(cins, couts, sems)

        @pl.when(step == total - 1)
        def _():
            comm.finish(cins, couts, sems)

    call = pl.pallas_call(carried, name=name, grid=grid, in_specs=list(in_specs) + [ANY] * nci,
                          out_specs=o_specs + [ANY] * nco, out_shape=o_shapes + list(comm.out_shapes),
                          scratch_shapes=list(scratch_shapes) + list(comm.sem_shapes), compiler_params=compiler_params,
                          input_output_aliases=aliases)

    def run(*args):
        res = call(*args, *comm.ins)
        comm.results = list(res[n_out:])
        return res[0] if single else list(res[:n_out])

    return run


class MultiComm:
    def __init__(self, comms):
        self.comms = list(comms)
        self.ins = [a for c in self.comms for a in c.ins]
        self.out_shapes = [s for c in self.comms for s in c.out_shapes]
        self.sem_shapes = [s for c in self.comms for s in c.sem_shapes]

    def _each(self, cins, couts, sems):
        i = o = s = 0
        for c in self.comms:
            ni, no, ns = len(c.ins), len(c.out_shapes), len(c.sem_shapes)
            yield c, cins[i:i + ni], couts[o:o + no], sems[s:s + ns]
            i, o, s = i + ni, o + no, s + ns

    def start(self, cins, couts, sems):
        for c, a, b, d in self._each(cins, couts, sems):
            c.start(a, b, d)

    def mid(self, cins, couts, sems):
        for c, a, b, d in self._each(cins, couts, sems):
            c.mid(a, b, d)

    def finish(self, cins, couts, sems):
        for c, a, b, d in self._each(cins, couts, sems):
            c.finish(a, b, d)

    @property
    def results(self):
        return [r for c in self.comms for r in c.results]

    @results.setter
    def results(self, vals):
        o = 0
        for c in self.comms:
            c.results = list(vals[o:o + len(c.out_shapes)])
            o += len(c.out_shapes)


def _run_comm(comm, *, name):
    nci, nco = len(comm.ins), len(comm.out_shapes)

    def body(*refs):
        cins, couts, sems = refs[:nci], refs[nci:nci + nco], refs[nci + nco:]
        comm.start(cins, couts, sems)
        comm.mid(cins, couts, sems)
        comm.finish(cins, couts, sems)

    comm.results = list(pl.pallas_call(body, name=name, in_specs=[ANY] * nci, out_specs=[ANY] * nco,
                                       out_shape=list(comm.out_shapes), scratch_shapes=list(comm.sem_shapes))(*comm.ins))


def _nt(a, b):
    return lax.dot_general(a, b, (((1,), (1,)), ((), ())), preferred_element_type=f32)


def _tn(a, b):
    return lax.dot_general(a, b, (((0,), (0,)), ((), ())), preferred_element_type=f32)


def _nn(a, b):
    return jnp.dot(a, b, preferred_element_type=f32)


def _sigmoid(x):
    return 1.0 / (1.0 + jnp.exp(-x))


def _seg_sum(v, e_ref):
    hi = v.astype(jnp.bfloat16)
    lo = (v - hi.astype(f32)).astype(jnp.bfloat16)
    e = e_ref[...]
    return _nn(hi, e) + _nn(lo, e)


def _seg_sum128(v, e_ref):
    e = e_ref[0:128, 0:128]
    hi = v.astype(jnp.bfloat16)
    lo = (v - hi.astype(f32)).astype(jnp.bfloat16)
    return _nn(hi, e) + _nn(lo, e)


def _head_eye():
    r = lax.broadcasted_iota(jnp.int32, (512, 512), 0) // HD
    c = lax.broadcasted_iota(jnp.int32, (512, 512), 1) // HD
    return (r == c).astype(jnp.bfloat16)


def _rms_norm_rows(x_ref, g_ref, h_ref, tm):
    def chunk(c, carry):
        rows = pl.ds(c * BLK, BLK)
        xf = x_ref[rows, :]
        r = lax.rsqrt(jnp.mean(xf * xf, axis=-1, keepdims=True) + EPS)
        h_ref[rows, :] = (xf * r * g_ref[...]).astype(MXU)
        return carry
    lax.fori_loop(0, tm // BLK, chunk, 0)


def rms_proj(x, g, w, *, tm, tn, name, comm=None):
    T, D = x.shape
    N = w.shape[1]
    ni, nj = T // tm, N // tn
    rc = tm // nj

    def body(x_ref, g_ref, w_ref, h_ref, o_ref, hbuf):
        i, j = pl.program_id(0), pl.program_id(1)

        @pl.when(i < ni)
        def _():
            def chunk(c, carry):
                rows = pl.ds(c * BLK, BLK)
                xf = x_ref[rows, :]
                r = lax.rsqrt(jnp.mean(xf * xf, axis=-1, keepdims=True) + EPS)
                hv = (xf * r * g_ref[...]).astype(MXU)
                h_ref[rows, :] = hv
                hbuf[i % 2, pl.ds(pl.multiple_of(j * rc + c * BLK, BLK), BLK), :] = hv
                return carry
            lax.fori_loop(0, rc // BLK, chunk, 0)

        @pl.when(i > 0)
        def _():
            o_ref[...] = _nn(hbuf[(i - 1) % 2], w_ref[...])

    rows = pl.BlockSpec((rc, D), lambda i, j: (jnp.where(i < ni, i * nj + j, ni * nj - 1), 0))
    return _pallas(
        body, comm=comm, name=name, grid=(ni + 1, nj),
        in_specs=[rows, pl.BlockSpec((1, D), lambda i, j: (0, 0)),
                  pl.BlockSpec((D, tn), lambda i, j: (0, jnp.where(i > 0, j, 0)))],
        out_specs=[rows, pl.BlockSpec((tm, tn), lambda i, j: (jnp.maximum(i - 1, 0), jnp.where(i > 0, j, 0)))],
        out_shape=[S((T, D), MXU), S((T, N), f32)],
        scratch_shapes=[pltpu.VMEM((2, tm, D), MXU)],
        compiler_params=_cp("arbitrary", "arbitrary"),
    )(x, g, w)


def rms_swiglu(x, g, wg, wu, *, tm, tn, name, comm=None):
    T, D = x.shape
    N = wg.shape[1]

    def body(x_ref, g_ref, wg_ref, wu_ref, h_ref, gate_ref, up_ref, act_ref):
        @pl.when(pl.program_id(1) == 0)
        def _():
            _rms_norm_rows(x_ref, g_ref, h_ref, tm)
        h = h_ref[...]
        gate = _nn(h, wg_ref[...])
        up = _nn(h, wu_ref[...])
        gate_ref[...] = gate
        up_ref[...] = up
        act_ref[...] = (gate * _sigmoid(gate) * up).astype(MXU)

    wspec = pl.BlockSpec((D, tn), lambda i, j: (0, j))
    ospec = pl.BlockSpec((tm, tn), lambda i, j: (i, j))
    return _pallas(
        body, comm=comm, name=name, grid=(T // tm, N // tn),
        in_specs=[pl.BlockSpec((tm, D), lambda i, j: (i, 0)), pl.BlockSpec((1, D), lambda i, j: (0, 0)), wspec, wspec],
        out_specs=[pl.BlockSpec((tm, D), lambda i, j: (i, 0)), ospec, ospec, ospec],
        out_shape=[S((T, D), MXU), S((T, N), f32), S((T, N), f32), S((T, N), MXU)],
        compiler_params=_cp("arbitrary", "arbitrary"),
    )(x, g, wg, wu)


def matmul_res(a, w, res, *, tm, tn, name, comm=None):
    T, K = a.shape
    N = w.shape[1]

    def body(a_ref, w_ref, r_ref, o_ref):
        o_ref[...] = r_ref[...] + _nn(a_ref[...], w_ref[...])

    return _pallas(
        body, comm=comm, name=name, grid=(T // tm, N // tn),
        in_specs=[pl.BlockSpec((tm, K), lambda i, j: (i, 0)), pl.BlockSpec((K, tn), lambda i, j: (0, j)),
                  pl.BlockSpec((tm, tn), lambda i, j: (i, j))],
        out_specs=pl.BlockSpec((tm, tn), lambda i, j: (i, j)),
        out_shape=S((T, N), f32),
        compiler_params=_cp("arbitrary", "arbitrary"),
    )(a, w, res)


def nt_plain(a, w, *, tm, tn, name, comm=None):
    T, K = a.shape
    N = w.shape[0]

    def body(a_ref, w_ref, o_ref):
        o_ref[...] = _nt(a_ref[...], w_ref[...])

    return _pallas(
        body, comm=comm, name=name, grid=(T // tm, N // tn),
        in_specs=[pl.BlockSpec((tm, K), lambda i, j: (i, 0)), pl.BlockSpec((tn, K), lambda i, j: (j, 0))],
        out_specs=pl.BlockSpec((tm, tn), lambda i, j: (i, j)),
        out_shape=S((T, N), f32),
        compiler_params=_cp("arbitrary", "arbitrary"),
    )(a, w)


def nt_swiglu_bwd(dy, wd, gate, up, *, tm, tn, name, comm=None):
    T, D = dy.shape
    F = wd.shape[0]

    def body(dy_ref, w_ref, g_ref, u_ref, dg_ref, du_ref):
        d_act = _nt(dy_ref[...], w_ref[...])
        g = g_ref[...]
        sg = _sigmoid(g)
        du_ref[...] = (d_act * (g * sg)).astype(MXU)
        dg_ref[...] = (d_act * u_ref[...] * (sg * (1.0 + g * (1.0 - sg)))).astype(MXU)

    blk = pl.BlockSpec((tm, tn), lambda i, j: (i, j))
    return _pallas(
        body, comm=comm, name=name, grid=(T // tm, F // tn),
        in_specs=[pl.BlockSpec((tm, D), lambda i, j: (i, 0)), pl.BlockSpec((tn, D), lambda i, j: (j, 0)), blk, blk],
        out_specs=[blk, blk],
        out_shape=[S((T, F), MXU), S((T, F), MXU)],
        compiler_params=_cp("arbitrary", "arbitrary"),
    )(dy, wd, gate, up)


def nt_rms_bwd(terms, x, g, dres, *, tm, tn, name, comm=None):
    T, D = x.shape
    K = terms[0][0].shape[1]
    nj = D // tn
    rc = tm // nj
    nt = len(terms)
    ni = T // tm

    def body(*refs):
        a_refs = refs[0:2 * nt:2]
        w_refs = refs[1:2 * nt:2]
        x_ref, g_ref, r_ref, dx_ref, dxb_ref, dg_ref, acc_ref = refs[2 * nt:]
        i, j = pl.program_id(0), pl.program_id(1)

        @pl.when(i < ni)
        def _():
            part = _nt(a_refs[0][...], w_refs[0][...])
            for t in range(1, nt):
                part += _nt(a_refs[t][...], w_refs[t][...])
            acc_ref[i % 2, j] = part

        @pl.when(i > 0)
        def _():
            rows = pl.ds(pl.multiple_of(j * rc, BLK), rc)
            dh = jnp.concatenate([acc_ref[(i - 1) % 2, jj, rows, :] for jj in range(nj)], axis=1)
            xf = x_ref[...]
            r = lax.rsqrt(jnp.mean(xf * xf, axis=-1, keepdims=True) + EPS)
            y = xf * r
            dy = dh * g_ref[...]
            dx = r_ref[...] + r * (dy - y * jnp.mean(dy * y, axis=-1, keepdims=True))
            dx_ref[...] = dx
            dxb_ref[...] = dx.astype(MXU)
            dgain = jnp.sum(dh * y, axis=0, keepdims=True)
            first = jnp.logical_and(i == 1, j == 0)

            @pl.when(first)
            def _():
                dg_ref[...] = dgain

            @pl.when(jnp.logical_not(first))
            def _():
                dg_ref[...] += dgain

    in_specs, args = [], []
    for a, w in terms:
        in_specs += [pl.BlockSpec((tm, K), lambda i, j: (jnp.minimum(i, ni - 1), 0)),
                     pl.BlockSpec((tn, K), lambda i, j: (jnp.where(i < ni, j, nj - 1), 0))]
        args += [a, w]
    row = pl.BlockSpec((rc, D), lambda i, j: (jnp.where(i > 0, (i - 1) * nj + j, 0), 0))
    vec = pl.BlockSpec((1, D), lambda i, j: (0, 0))
    in_specs += [row, vec, row]
    return _pallas(
        body, comm=comm, name=name, grid=(ni + 1, nj), in_specs=in_specs,
        out_specs=[row, row, vec],
        out_shape=[S((T, D), f32), S((T, D), MXU), S((1, D), f32)],
        scratch_shapes=[pltpu.VMEM((2, nj, tm, tn), f32)],
        compiler_params=_cp("arbitrary", "arbitrary", vmem=VMEM_TALL),
    )(*args, x, g, dres)


def tn_matmul(a, b, *, tm, tn, tk, name, by_chip=False, comm=None):
    T, M = a.shape
    N = b.shape[1]
    if by_chip:
        tn = N // 4
        out_spec = pl.BlockSpec((None, tm, tn), lambda i, j, k: (j, i, 0))
        out_shape = S((4, M, tn), f32)
    else:
        out_spec = pl.BlockSpec((tm, tn), lambda i, j, k: (i, j))
        out_shape = S((M, N), f32)

    def body(a_ref, b_ref, o_ref):
        part = _tn(a_ref[...], b_ref[...])

        @pl.when(pl.program_id(2) == 0)
        def _():
            o_ref[...] = part

        @pl.when(pl.program_id(2) > 0)
        def _():
            o_ref[...] += part

    return _pallas(
        body, comm=comm, name=name, grid=(M // tm, N // tn, T // tk),
        in_specs=[pl.BlockSpec((tk, tm), lambda i, j, k: (k, i)), pl.BlockSpec((tk, tn), lambda i, j, k: (k, j))],
        out_specs=out_spec, out_shape=out_shape,
        compiler_params=_cp("arbitrary", "arbitrary", "arbitrary"),
    )(a, b)


def loss_head(y, target, *, tm, name):
    T, D = y.shape
    ni = T // tm

    def body(y_ref, t_ref, dy_ref, dyb_ref, l_ref, acc_ref):
        i = pl.program_id(0)
        e = y_ref[...] - t_ref[...]
        dy = e * (1.0 / D)
        dy_ref[...] = dy
        dyb_ref[...] = dy.astype(MXU)
        part = jnp.sum(e * e, axis=0, keepdims=True)

        @pl.when(i == 0)
        def _():
            acc_ref[...] = part

        @pl.when(i > 0)
        def _():
            acc_ref[...] += part

        @pl.when(i == ni - 1)
        def _():
            tot = jnp.sum(acc_ref[...], axis=1, keepdims=True) * (0.5 / D)
            l_ref[...] = jnp.broadcast_to(tot, (1, 128))

    row = pl.BlockSpec((tm, D), lambda i: (i, 0))
    return pl.pallas_call(
        body, name=name, grid=(ni,), in_specs=[row, row],
        out_specs=[row, row, pl.BlockSpec((1, 128), lambda i: (0, 0))],
        out_shape=[S((T, D), f32), S((T, D), MXU), S((1, 128), f32)],
        scratch_shapes=[pltpu.VMEM((1, D), f32)],
        compiler_params=_cp("arbitrary"),
    )(y, target)


def _qk_norm(v, gain, e_ref):
    r = lax.rsqrt(_seg_sum(v * v, e_ref) * (1.0 / HD) + EPS)
    return v * r * gain


def _dup_halves(pair):
    rolled = pltpu.roll(pair, HD, 1)
    lo = lax.broadcasted_iota(jnp.int32, pair.shape, 1) < HD
    return jnp.where(lo, pair, rolled), jnp.where(lo, rolled, pair)


def prep_fwd(proj, e, gains, *, tm, name):
    T = proj.shape[0]

    def body(p_ref, e_ref, gaq, gak, gcq, gck, aq, ak, av, cq, ckk, cvv):
        aq[...] = _qk_norm(p_ref[:, 0:512], gaq[...], e_ref)
        ak[...] = _qk_norm(p_ref[:, 512:1024], gak[...], e_ref)
        av[...] = p_ref[:, 1024:1536]
        cq[:, 0:512] = _qk_norm(p_ref[:, 2560:3072], gcq[...], e_ref).astype(MXU)
        cq[:, 512:1024] = _qk_norm(p_ref[:, 3072:3584], gcq[...], e_ref).astype(MXU)
        kraw = p_ref[:, 3584:3712]
        kn = kraw * lax.rsqrt(_seg_sum128(kraw * kraw, e_ref) * (1.0 / HD) + EPS) * gck[...]
        k0, k1 = _dup_halves(kn)
        ckk[:, 0:128] = k0.astype(MXU)
        ckk[:, 128:256] = k1.astype(MXU)
        v0, v1 = _dup_halves(p_ref[:, 3712:3840])
        cvv[:, 0:128] = v0.astype(MXU)
        cvv[:, 128:256] = v1.astype(MXU)

    def vec(n):
        return pl.BlockSpec((1, n), lambda i: (0, 0))

    def rows(n):
        return pl.BlockSpec((tm, n), lambda i: (i, 0))

    return pl.pallas_call(
        body, name=name, grid=(T // tm,),
        in_specs=[rows(IN_W), pl.BlockSpec((512, 512), lambda i: (0, 0)), vec(512), vec(512), vec(512), vec(128)],
        out_specs=[rows(512), rows(512), rows(512), rows(1024), rows(256), rows(256)],
        out_shape=[S((T, 512), f32)] * 3 + [S((T, 1024), MXU), S((T, 256), MXU), S((T, 256), MXU)],
        compiler_params=_cp("arbitrary"),
    )(proj, e, *gains)


def _band_mask(max_dist, shut):
    r = lax.broadcasted_iota(jnp.int32, (2 * BLK, 2 * BLK), 0) & (BLK - 1)
    c = lax.broadcasted_iota(jnp.int32, (2 * BLK, 2 * BLK), 1)
    prev = jnp.logical_and(c < BLK, c >= r + (BLK - max_dist) + shut)
    return jnp.logical_or(prev, jnp.logical_and(c >= BLK, c - BLK <= r))


def _prev_mask(max_dist, shut):
    r = lax.broadcasted_iota(jnp.int32, (2 * BLK, BLK), 0) & (BLK - 1)
    c = lax.broadcasted_iota(jnp.int32, (2 * BLK, BLK), 1)
    return c >= r + (BLK - max_dist) + shut


def _head_masks():
    lo = (lax.broadcasted_iota(jnp.int32, (BLK, BLK), 1) < HD).astype(f32)
    return lo.astype(MXU), (1.0 - lo).astype(MXU)


def _stack_heads(x, hm):
    return jnp.concatenate([x * hm[0], x * hm[1]], axis=0)


def _unstack_heads(y, lane_lo):
    return jnp.where(lane_lo, y[0:BLK], y[BLK:2 * BLK])


def _rows(ref, start, dil):
    if dil == 1:
        return ref[pl.ds(start, BLK), :]
    return ref[pl.ds(start, BLK, stride=dil), :]


def _set_rows(ref, start, dil, val):
    if dil == 1:
        ref[pl.ds(start, BLK), :] = val
    else:
        ref[pl.ds(start, BLK, stride=dil), :] = val


def _attn_geometry(T, dil):
    span = BLK * dil
    n = max(1, 512 // span)
    return span, n, T // (span * n)


def band_attn_fwd(q, k, v, sinks, *, dil, max_dist, group, name, mix=None, comm=None):
    T = q.shape[0]
    P = q.shape[1] // BLK
    span, n, nb = _attn_geometry(T, dil)

    def body(*refs):
        s_ref = m_ref = None
        q_ref, kc_ref, kp_ref, vc_ref, vp_ref = refs[:5]
        rest = list(refs[5:])
        if sinks is not None:
            s_ref = rest.pop(0)
        if mix is not None:
            rest.pop(0)
            o_ref, l_ref, m_ref = rest
        else:
            o_ref, l_ref = rest
        b = pl.program_id(0)
        mask = _band_mask(max_dist, 0)
        mask0 = _band_mask(max_dist, jnp.where(b > 0, 0, BLK + 1))
        lane_lo = lax.broadcasted_iota(jnp.int32, (BLK, BLK), 1) < HD
        hm = _head_masks()
        if sinks is not None:
            row_lo = lax.broadcasted_iota(jnp.int32, (1, BLK), 1) < HD
            sk0 = jnp.max(jnp.where(row_lo, s_ref[...], NEG), axis=1, keepdims=True)
            sk1 = jnp.max(jnp.where(row_lo, NEG, s_ref[...]), axis=1, keepdims=True)
            sk = jnp.where(lax.broadcasted_iota(jnp.int32, (2 * BLK, 1), 0) < BLK, sk0, sk1)

        def load(r, sub):
            at = r + sub * span
            kc, vc = _rows(kc_ref, at, dil).astype(MXU), _rows(vc_ref, at, dil).astype(MXU)
            if sub == 0:
                kp, vp = _rows(kp_ref, r, dil).astype(MXU), _rows(vp_ref, r, dil).astype(MXU)
            else:
                kp, vp = _rows(kc_ref, at - span, dil).astype(MXU), _rows(vc_ref, at - span, dil).astype(MXU)
            qst = _stack_heads(_rows(q_ref, at, dil).astype(MXU), hm)
            return (qst, jnp.concatenate([kp, kc], axis=0), jnp.concatenate([vp, vc], axis=0),
                    mask0 if sub == 0 else mask, at)

        def attend(items):
            ss = [jnp.where(m_, _nt(qst, kcat) * SCALE, NEG) for qst, kcat, _, m_, _ in items]
            ms = [jnp.max(s, axis=1, keepdims=True) for s in ss]
            if sinks is not None:
                ms = [jnp.maximum(m, sk) for m in ms]
            ps = [jnp.exp(s - m) for s, m in zip(ss, ms)]
            dens = [jnp.sum(p_, axis=1, keepdims=True) for p_ in ps]
            if sinks is not None:
                dens = [d + jnp.exp(sk - m) for d, m in zip(dens, ms)]
            outs = [_nn(p_.astype(MXU), it[2]) / d for p_, it, d in zip(ps, items, dens)]
            for it, o, m, d in zip(items, outs, ms, dens):
                lse = m + jnp.log(d)
                if m_ref is not None:
                    _set_rows(m_ref, it[4], dil, _unstack_heads(o, lane_lo).astype(MXU))
                _set_rows(o_ref, it[4], dil, _unstack_heads(o, lane_lo))
                _set_rows(l_ref, it[4], dil, jnp.where(lane_lo, lse[0:BLK], lse[BLK:2 * BLK]))

        if dil * n <= 4:
            work = [(r, sub) for r in range(dil) for sub in range(n)]
            for g in range(0, len(work), 2):
                attend([load(*w) for w in work[g:g + 2]])
        else:
            def two_streams(i, carry):
                attend([load(2 * i, 0), load(2 * i + 1, 0)])
                return carry
            lax.fori_loop(0, dil // 2, two_streams, 0)

    rows_per_step = span * n
    qspec = pl.BlockSpec((rows_per_step, BLK), lambda b, p: (b, p))
    cur = pl.BlockSpec((rows_per_step, BLK), lambda b, p: (b, p // group))
    prev = pl.BlockSpec((span, BLK), lambda b, p: (jnp.maximum(b * n - 1, 0), p // group))
    in_specs = [qspec, cur, prev, cur, prev]
    args = [q, k, k, v, v]
    if sinks is not None:
        in_specs.append(pl.BlockSpec((1, BLK), lambda b, p: (0, p)))
        args.append(sinks)
    out_specs, out_shape, aliases = [qspec, qspec], [S(q.shape, f32), S(q.shape, f32)], {}
    if mix is not None:
        first_block = mix.shape[1] // BLK - P
        aliases = {len(args): 2}
        in_specs.append(ANY)
        args.append(mix)
        out_specs.append(pl.BlockSpec((rows_per_step, BLK), lambda b, p: (b, first_block + p)))
        out_shape.append(S(mix.shape, mix.dtype))
    return _pallas(
        body, comm=comm, name=name, grid=(nb, P), in_specs=in_specs, out_specs=out_specs, out_shape=out_shape,
        compiler_params=_cp("arbitrary", "arbitrary"), input_output_aliases=aliases,
    )(*args)


def band_attn_bwd(q, k, v, lse, do, dd, *, dil, max_dist, group, name, comm=None):
    T = q.shape[0]
    P = q.shape[1] // BLK
    span, n, nb = _attn_geometry(T, dil)
    assert group == 1 or dil == 1

    def body(q_ref, qn_ref, do_ref, don_ref, l_ref, ln_ref, d_ref, dn_ref, kc_ref, kp_ref, vc_ref, vp_ref,
             dq_ref, dk_ref, dv_ref):
        b, p = pl.program_id(0), pl.program_id(1)
        mask = _band_mask(max_dist, 0)
        mask0 = _band_mask(max_dist, jnp.where(b > 0, 0, BLK + 1))
        tail = _prev_mask(max_dist, jnp.where(b < nb - 1, 0, BLK + 1))
        lane_lo = lax.broadcasted_iota(jnp.int32, (BLK, BLK), 1) < HD
        hm = _head_masks()
        own_lanes = (lax.broadcasted_iota(jnp.int32, (2 * BLK, BLK), 1) < HD) == (
            lax.broadcasted_iota(jnp.int32, (2 * BLK, BLK), 0) < BLK)

        def per_row(x):
            return jnp.max(jnp.where(own_lanes, jnp.concatenate([x, x], axis=0), NEG), axis=1, keepdims=True)

        def q_side(refs, at):
            q_r, do_r, l_r, d_r = refs
            return (_stack_heads(_rows(q_r, at, dil).astype(MXU), hm), _stack_heads(_rows(do_r, at, dil).astype(MXU), hm),
                    per_row(_rows(l_r, at, dil)), per_row(_rows(d_r, at, dil)))

        def kv(ref, at):
            return _rows(ref, at, dil).astype(MXU)

        first = p % group == 0

        def put_kv(ref, at, val):
            if group == 1:
                _set_rows(ref, at, dil, val)
            else:
                @pl.when(first)
                def _():
                    ref[pl.ds(at, BLK), :] = val

                @pl.when(jnp.logical_not(first))
                def _():
                    ref[pl.ds(at, BLK), :] += val

        def stream(r):
            dks, dvs = [None] * n, [None] * n
            for sub in range(n):
                at = r + sub * span
                qst, dost, lrow, drow = q_side((q_ref, do_ref, l_ref, d_ref), at)
                if sub == 0:
                    kp, vp, m_ = kv(kp_ref, r), kv(vp_ref, r), mask0
                else:
                    kp, vp, m_ = kv(kc_ref, at - span), kv(vc_ref, at - span), mask
                kcat = jnp.concatenate([kp, kv(kc_ref, at)], axis=0)
                vcat = jnp.concatenate([vp, kv(vc_ref, at)], axis=0)
                pr = jnp.where(m_, jnp.exp(_nt(qst, kcat) * SCALE - lrow), 0.0)
                ds = (pr * (_nt(dost, vcat) - drow) * SCALE).astype(MXU)
                prb = pr.astype(MXU)
                _set_rows(dq_ref, at, dil, _unstack_heads(_nn(ds, kcat), lane_lo))
                if sub == 0:
                    dks[0] = _tn(ds[:, BLK:], qst)
                    dvs[0] = _tn(prb[:, BLK:], dost)
                else:
                    dkk, dvv = _tn(ds, qst), _tn(prb, dost)
                    dks[sub - 1] += dkk[0:BLK]
                    dvs[sub - 1] += dvv[0:BLK]
                    dks[sub], dvs[sub] = dkk[BLK:], dvv[BLK:]
            at = r + (n - 1) * span
            qst, dost, lrow, drow = q_side((qn_ref, don_ref, ln_ref, dn_ref), r)
            pr = jnp.where(tail, jnp.exp(_nt(qst, kv(kc_ref, at)) * SCALE - lrow), 0.0)
            ds = (pr * (_nt(dost, kv(vc_ref, at)) - drow) * SCALE).astype(MXU)
            dks[n - 1] += _tn(ds, qst)
            dvs[n - 1] += _tn(pr.astype(MXU), dost)
            for sub in range(n):
                put_kv(dk_ref, r + sub * span, dks[sub])
                put_kv(dv_ref, r + sub * span, dvs[sub])

        if dil <= 4:
            for r in range(dil):
                stream(r)
        else:
            def two_streams(i, carry):
                stream(2 * i)
                stream(2 * i + 1)
                return carry
            lax.fori_loop(0, dil // 2, two_streams, 0)

    rows_per_step = span * n
    qspec = pl.BlockSpec((rows_per_step, BLK), lambda b, p: (b, p))
    qnext = pl.BlockSpec((span, BLK), lambda b, p: (jnp.minimum((b + 1) * n, T // span - 1), p))
    cur = pl.BlockSpec((rows_per_step, BLK), lambda b, p: (b, p // group))
    prev = pl.BlockSpec((span, BLK), lambda b, p: (jnp.maximum(b * n - 1, 0), p // group))
    return _pallas(
        body, comm=comm, name=name, grid=(nb, P),
        in_specs=[qspec, qnext, qspec, qnext, qspec, qnext, qspec, qnext, cur, prev, cur, prev],
        out_specs=[qspec, cur, cur],
        out_shape=[S(q.shape, f32), S(k.shape, f32), S(k.shape, f32)],
        compiler_params=_cp("arbitrary", "arbitrary"),
    )(q, q, do, do, lse, lse, dd, dd, k, k, v, v)


def dil_combine_fwd(ols, *, tm, name):
    T = ols[0].shape[0]

    def body(o1, l1, o2, l2, o3, l3, out_ref):
        a, b, c = l1[...], l2[...], l3[...]
        m = jnp.maximum(jnp.maximum(a, b), c)
        ea, eb, ec = jnp.exp(a - m), jnp.exp(b - m), jnp.exp(c - m)
        out = (ea * o1[...] + eb * o2[...] + ec * o3[...]) / (ea + eb + ec)
        out_ref[...] = out.astype(MXU)

    row = pl.BlockSpec((tm, 512), lambda i: (i, 0))
    return pl.pallas_call(body, name=name, grid=(T // tm,), in_specs=[row] * 6, out_specs=row,
                          out_shape=S((T, A_W + B_W + C_W), MXU), compiler_params=_cp("arbitrary"))(*ols)


def dil_combine_bwd(ols, dmix, e, *, tm, name):
    T = ols[0].shape[0]

    def body(o1, l1, o2, l2, o3, l3, d_ref, e_ref, do1, do2, do3, dd1, dd2, dd3):
        a, b, c = l1[...], l2[...], l3[...]
        m = jnp.maximum(jnp.maximum(a, b), c)
        ea, eb, ec = jnp.exp(a - m), jnp.exp(b - m), jnp.exp(c - m)
        inv = 1.0 / (ea + eb + ec)
        wa, wb, wc = ea * inv, eb * inv, ec * inv
        dout = d_ref[...]
        gbar = _seg_sum(dout * (wa * o1[...] + wb * o2[...] + wc * o3[...]), e_ref)
        do1[...] = wa * dout
        do2[...] = wb * dout
        do3[...] = wc * dout
        dd1[...] = wa * gbar
        dd2[...] = wb * gbar
        dd3[...] = wc * gbar

    row = pl.BlockSpec((tm, 512), lambda i: (i, 0))
    return pl.pallas_call(
        body, name=name, grid=(T // tm,),
        in_specs=[row] * 6 + [row, pl.BlockSpec((512, 512), lambda i: (0, 0))],
        out_specs=[row] * 6,
        out_shape=[S((T, 512), f32)] * 6,
        compiler_params=_cp("arbitrary"),
    )(*ols, dmix, e)


def swa_pre_bwd(o, lse, dmix, sinks, e, *, tm, name):
    T = o.shape[0]
    ni = T // tm

    def body(o_ref, l_ref, d_ref, s_ref, e_ref, do_ref, dd_ref, ds_ref):
        i = pl.program_id(0)
        dout = d_ref[...]
        do_ref[...] = dout.astype(MXU)
        prod = dout * o_ref[...]
        dd = jnp.concatenate([_seg_sum(prod[:, 0:512], e_ref), _seg_sum(prod[:, 512:1024], e_ref)], axis=1)
        dd_ref[...] = dd
        part = -jnp.sum(jnp.exp(s_ref[...] - l_ref[...]) * dd, axis=0, keepdims=True)

        @pl.when(i == 0)
        def _():
            ds_ref[...] = part

        @pl.when(i > 0)
        def _():
            ds_ref[...] += part

    row = pl.BlockSpec((tm, 1024), lambda i: (i, 0))
    vec = pl.BlockSpec((1, 1024), lambda i: (0, 0))
    return pl.pallas_call(
        body, name=name, grid=(ni,),
        in_specs=[row, row, pl.BlockSpec((tm, 1024), lambda i: (i, 1)), vec, pl.BlockSpec((512, 512), lambda i: (0, 0))],
        out_specs=[row, row, vec],
        out_shape=[S((T, 1024), MXU), S((T, 1024), f32), S((1, 1024), f32)],
        compiler_params=_cp("arbitrary"),
    )(o, lse, dmix, sinks, e)


SHIFT_PAD = 24


def _shifted_copies(buf_ref, sh_ref, length):
    for r in range(1, 8):
        sh_ref[r - 1, 0:length, :] = buf_ref[pl.ds(r, length), :]


def _window(buf_ref, sh_ref, start, rows):
    q, r = divmod(start, 8)
    if r == 0:
        return buf_ref[pl.ds(8 * q, rows), :]
    return sh_ref[r - 1, pl.ds(8 * q, rows), :]


TAP_ROWS = 64


def _tap_sum(buf_ref, sh_ref, w_ref, starts, rows):
    outs = []
    for c0 in range(0, rows, TAP_ROWS):
        n = min(TAP_ROWS, rows - c0)
        acc = _window(buf_ref, sh_ref, starts[0] + c0, n) * w_ref[pl.ds(0, 1), :]
        for j in range(1, CONV_K):
            acc += _window(buf_ref, sh_ref, starts[j] + c0, n) * w_ref[pl.ds(j, 1), :]
        outs.append(acc)
    return jnp.concatenate(outs, axis=0)


def _conv_taps(buf_ref, sh_ref, w_ref, start, rows):
    return _tap_sum(buf_ref, sh_ref, w_ref, [start + j for j in range(CONV_K)], rows)


def conv_fwd(proj, mix, w, b, ln_g, ln_b, *, tb, name):
    T = proj.shape[0]
    hb = tb // HALO

    def body(u_ref, g_ref, up_ref, gp_ref, w_ref, b_ref, lg_ref, lb_ref, mix_ref, o_ref, hbuf, hsh):
        i = pl.program_id(0)
        hprev = up_ref[...] * _sigmoid(gp_ref[...])
        hbuf[0:HALO, :] = hprev * jnp.where(i > 0, 1.0, 0.0)
        hbuf[HALO:HALO + tb, :] = u_ref[...] * _sigmoid(g_ref[...])
        _shifted_copies(hbuf, hsh, tb + SHIFT_PAD)
        y = _conv_taps(hbuf, hsh, w_ref, HALO - (CONV_K - 1), tb) + b_ref[...]
        mu = jnp.mean(y, axis=-1, keepdims=True)
        yc = y - mu
        var = jnp.mean(yc * yc, axis=-1, keepdims=True)
        z = yc * lax.rsqrt(var + EPS) * lg_ref[...] + lb_ref[...]
        o_ref[...] = (z * _sigmoid(z)).astype(MXU)

    vec = pl.BlockSpec((1, 512), lambda i: (0, 0))
    return pl.pallas_call(
        body, name=name, grid=(T // tb,),
        in_specs=[pl.BlockSpec((tb, 512), lambda i: (i, 3)), pl.BlockSpec((tb, 512), lambda i: (i, 4)),
                  pl.BlockSpec((HALO, 512), lambda i: (jnp.maximum(i * hb - 1, 0), 3)),
                  pl.BlockSpec((HALO, 512), lambda i: (jnp.maximum(i * hb - 1, 0), 4)),
                  pl.BlockSpec((HALO, 512), lambda i: (0, 0)), vec, vec, vec, ANY],
        out_specs=pl.BlockSpec((tb, 512), lambda i: (i, 1)),
        out_shape=S(mix.shape, mix.dtype),
        scratch_shapes=[pltpu.VMEM((tb + HALO, 512), f32), pltpu.VMEM((7, tb + SHIFT_PAD, 512), f32)],
        compiler_params=_cp("arbitrary"), input_output_aliases={8: 0},
    )(proj, proj, proj, proj, w, b, ln_g, ln_b, mix)


def conv_bwd(proj, dmix, w, b, ln_g, ln_b, *, tb, name, comm=None):
    T = proj.shape[0]
    hb = tb // HALO
    ni = T // tb
    last_h = T // HALO - 1
    ext = tb + HALO

    def body(u_ref, g_ref, up_ref, gp_ref, un_ref, gn_ref, d_ref, dn_ref, w_ref, b_ref, lg_ref, lb_ref,
             du_ref, dg_ref, dw_ref, db_ref, dlg_ref, dlb_ref, hbuf, dybuf, hsh, dsh):
        i = pl.program_id(0)
        hbuf[0:HALO, :] = up_ref[...] * _sigmoid(gp_ref[...]) * jnp.where(i > 0, 1.0, 0.0)
        u = u_ref[...]
        sg = _sigmoid(g_ref[...])
        hbuf[HALO:HALO + tb, :] = u * sg
        hbuf[HALO + tb:HALO + ext, :] = un_ref[...] * _sigmoid(gn_ref[...])
        _shifted_copies(hbuf, hsh, ext + SHIFT_PAD)
        y = _conv_taps(hbuf, hsh, w_ref, HALO - (CONV_K - 1), ext) + b_ref[...]
        mu = jnp.mean(y, axis=-1, keepdims=True)
        yc = y - mu
        rstd = lax.rsqrt(jnp.mean(yc * yc, axis=-1, keepdims=True) + EPS)
        yn = yc * rstd
        z = yn * lg_ref[...] + lb_ref[...]
        sz = _sigmoid(z)
        row = lax.broadcasted_iota(jnp.int32, (ext, 1), 0)
        own = row < tb
        keep = row < jnp.where(i < ni - 1, ext, tb)
        dout = jnp.concatenate([d_ref[...], dn_ref[...]], axis=0)
        dz = jnp.where(keep, dout * (sz * (1.0 + z * (1.0 - sz))), 0.0)
        dyn = dz * lg_ref[...]
        dy = rstd * (dyn - jnp.mean(dyn, axis=-1, keepdims=True) - yn * jnp.mean(dyn * yn, axis=-1, keepdims=True))
        dybuf[...] = dy
        _shifted_copies(dybuf, dsh, tb + SHIFT_PAD)
        dz_own = jnp.where(own, dz, 0.0)
        dlg = jnp.sum(dz_own * yn, axis=0, keepdims=True)
        dlb = jnp.sum(dz_own, axis=0, keepdims=True)
        dy_own = dybuf[0:tb, :]
        dbias = jnp.sum(dy_own, axis=0, keepdims=True)
        dh = _tap_sum(dybuf, dsh, w_ref, [CONV_K - 1 - j for j in range(CONV_K)], tb)
        du_ref[...] = (dh * sg).astype(MXU)
        dg_ref[...] = (dh * u * sg * (1.0 - sg)).astype(MXU)
        taps = [jnp.sum(dy_own * _window(hbuf, hsh, HALO - (CONV_K - 1) + j, tb), axis=0, keepdims=True)
                for j in range(CONV_K)]
        taps.append(jnp.zeros((1, 512), f32))
        dwt = jnp.concatenate(taps, axis=0)

        @pl.when(i == 0)
        def _():
            dw_ref[...] = dwt
            db_ref[...] = dbias
            dlg_ref[...] = dlg
            dlb_ref[...] = dlb

        @pl.when(i > 0)
        def _():
            dw_ref[...] += dwt
            db_ref[...] += dbias
            dlg_ref[...] += dlg
            dlb_ref[...] += dlb

    vec = pl.BlockSpec((1, 512), lambda i: (0, 0))
    wspec = pl.BlockSpec((HALO, 512), lambda i: (0, 0))

    def halo_prev(col):
        return pl.BlockSpec((HALO, 512), lambda i: (jnp.maximum(i * hb - 1, 0), col))

    def halo_next(col):
        return pl.BlockSpec((HALO, 512), lambda i: (jnp.minimum((i + 1) * hb, last_h), col))

    row = pl.BlockSpec((tb, 512), lambda i: (i, 0))
    return _pallas(
        body, comm=comm, name=name, grid=(ni,),
        in_specs=[pl.BlockSpec((tb, 512), lambda i: (i, 3)), pl.BlockSpec((tb, 512), lambda i: (i, 4)),
                  halo_prev(3), halo_prev(4), halo_next(3), halo_next(4),
                  pl.BlockSpec((tb, 512), lambda i: (i, 1)), halo_next(1), wspec, vec, vec, vec],
        out_specs=[row, row, wspec, vec, vec, vec],
        out_shape=[S((T, 512), MXU), S((T, 512), MXU), S((HALO, 512), f32)] + [S((1, 512), f32)] * 3,
        scratch_shapes=[pltpu.VMEM((tb + 2 * HALO, 512), f32), pltpu.VMEM((ext, 512), f32),
                        pltpu.VMEM((7, ext + SHIFT_PAD, 512), f32), pltpu.VMEM((7, tb + SHIFT_PAD, 512), f32)],
        compiler_params=_cp("arbitrary"),
    )(proj, proj, proj, proj, proj, proj, dmix, dmix, w, b, ln_g, ln_b)


def _qk_norm_bwd(v, gain, dout, e_ref):
    r = lax.rsqrt(_seg_sum(v * v, e_ref) * (1.0 / HD) + EPS)
    y = v * r
    dgain = jnp.sum(dout * y, axis=0, keepdims=True)
    dy = dout * gain
    dv = r * (dy - y * (_seg_sum(dy * y, e_ref) * (1.0 / HD)))
    return dv, dgain


def prep_bwd(proj, e, gains, da, dc, dconv, *, tm, name):
    T = proj.shape[0]

    def body(*refs):
        p_ref, e_ref, gaq, gak, gcq, gck = refs[0:6]
        a_refs = refs[6:15]
        dcq, dckk, dcvv, du, dgt = refs[15:20]
        dp, gaq_o, gak_o, gcq_o, gck_o = refs[20:]
        i = pl.program_id(0)
        dq = a_refs[0][...] + a_refs[3][...] + a_refs[6][...]
        dk = a_refs[1][...] + a_refs[4][...] + a_refs[7][...]
        dv = a_refs[2][...] + a_refs[5][...] + a_refs[8][...]
        d, g_aq = _qk_norm_bwd(p_ref[:, 0:512], gaq[...], dq, e_ref)
        dp[:, 0:512] = d.astype(MXU)
        d, g_ak = _qk_norm_bwd(p_ref[:, 512:1024], gak[...], dk, e_ref)
        dp[:, 512:1024] = d.astype(MXU)
        dp[:, 1024:1536] = dv.astype(MXU)
        dp[:, 1536:2048] = du[...]
        dp[:, 2048:2560] = dgt[...]
        d, g_cq0 = _qk_norm_bwd(p_ref[:, 2560:3072], gcq[...], dcq[:, 0:512], e_ref)
        dp[:, 2560:3072] = d.astype(MXU)
        d, g_cq1 = _qk_norm_bwd(p_ref[:, 3072:3584], gcq[...], dcq[:, 512:1024], e_ref)
        dp[:, 3072:3584] = d.astype(MXU)
        lo = lax.broadcasted_iota(jnp.int32, (tm, 128), 1) < HD

        def fold(ref):
            g0, g1 = ref[:, 0:128], ref[:, 128:256]
            s0 = g0 + pltpu.roll(g0, HD, 1)
            s1 = g1 + pltpu.roll(g1, HD, 1)
            return jnp.where(lo, s0, s1)

        dkn = fold(dckk)
        kraw = p_ref[:, 3584:3712]
        r = lax.rsqrt(_seg_sum128(kraw * kraw, e_ref) * (1.0 / HD) + EPS)
        y = kraw * r
        g_ck = jnp.sum(dkn * y, axis=0, keepdims=True)
        dy = dkn * gck[...]
        dp[:, 3584:3712] = (r * (dy - y * (_seg_sum128(dy * y, e_ref) * (1.0 / HD)))).astype(MXU)
        dp[:, 3712:3840] = fold(dcvv).astype(MXU)
        g_cq = jnp.concatenate([g_cq0, g_cq1], axis=1)

        @pl.when(i == 0)
        def _():
            gaq_o[...] = g_aq
            gak_o[...] = g_ak
            gcq_o[...] = g_cq
            gck_o[...] = g_ck

        @pl.when(i > 0)
        def _():
            gaq_o[...] += g_aq
            gak_o[...] += g_ak
            gcq_o[...] += g_cq
            gck_o[...] += g_ck

    def vec(n):
        return pl.BlockSpec((1, n), lambda i: (0, 0))

    def rows(n):
        return pl.BlockSpec((tm, n), lambda i: (i, 0))

    return pl.pallas_call(
        body, name=name, grid=(T // tm,),
        in_specs=[rows(IN_W), pl.BlockSpec((512, 512), lambda i: (0, 0)), vec(512), vec(512), vec(512), vec(128)]
        + [rows(512)] * 9 + [rows(1024), rows(256), rows(256), rows(512), rows(512)],
        out_specs=[rows(IN_W), vec(512), vec(512), vec(1024), vec(128)],
        out_shape=[S((T, IN_W), MXU), S((1, 512), f32), S((1, 512), f32), S((1, 1024), f32), S((1, 128), f32)],
        compiler_params=_cp("arbitrary"),
    )(proj, e, *gains, *da, *dc, *dconv)


def adamw(w, m, v, pieces, *, tr, name, comm=None):
    n, R, C = w.shape
    c1 = 1.0 - ADAM_B1 ** ADAM_STEP
    c2 = 1.0 - ADAM_B2 ** ADAM_STEP
    npc = len(pieces)

    def body(*refs):
        w_ref, m_ref, v_ref = refs[0:3]
        p_refs = refs[3:3 + npc]
        g_ref, d_ref, mo_ref, vo_ref = refs[3 + npc:]
        g = p_refs[0][...].astype(f32)
        for p in p_refs[1:]:
            g = g + p[...].astype(f32)
        mn = ADAM_B1 * m_ref[...] + (1.0 - ADAM_B1) * g
        vn = ADAM_B2 * v_ref[...] + (1.0 - ADAM_B2) * (g * g)
        g_ref[...] = g
        mo_ref[...] = mn
        vo_ref[...] = vn
        d_ref[...] = -ADAM_LR * ((mn / c1) / (jnp.sqrt(vn / c2) + ADAM_EPS) + ADAM_WD * w_ref[...])

    blk = pl.BlockSpec((None, tr, C), lambda l, i: (l, i, 0))
    return _pallas(
        body, comm=comm, name=name, grid=(n, R // tr), in_specs=[blk] * (3 + npc), out_specs=[blk] * 4,
        out_shape=[S(w.shape, f32)] * 4, compiler_params=_cp("arbitrary", "arbitrary"),
    )(w, m, v, *pieces)


def add_halves(pieces, other, *, tr, name):
    _, _, r, cc = pieces.shape

    def body(c_ref, a_ref, b_ref, o_ref):
        o_ref[...] = (a_ref[...] + b_ref[...]).astype(jnp.bfloat16)

    blk = pl.BlockSpec((None, tr, cc), lambda s, i, c_ref: (s, i, 0))
    grid_spec = pltpu.PrefetchScalarGridSpec(
        num_scalar_prefetch=1, grid=(4, r // tr),
        in_specs=[pl.BlockSpec((None, None, tr, cc), lambda s, i, c_ref: (s, c_ref[0], i, 0)), blk], out_specs=blk)
    core = lax.axis_index("c").astype(jnp.int32).reshape(1)
    return pl.pallas_call(body, name=name, grid_spec=grid_spec, out_shape=S((4, r, cc), jnp.bfloat16),
                          compiler_params=_cp("arbitrary", "arbitrary"))(core, pieces, other)


def sum8(parts, *, name):
    _, R, C = parts.shape

    def body(p_ref, o_ref):
        acc = p_ref[0]
        for d in range(1, 8):
            acc = acc + p_ref[d]
        o_ref[...] = acc

    return pl.pallas_call(body, name=name, out_shape=S((R, C), f32))(parts)


def _pos():
    return lax.axis_index("x"), lax.axis_index("y"), lax.axis_index("c")


def _other_chips(x, y):
    return [(1 - x, y), (x, 1 - y), (1 - x, 1 - y)]


class GatherComm:
    def __init__(self, shards, in_place=None):
        self.ins = list(shards)
        self.nt = nt = len(shards)
        self.in_place = list(in_place) if in_place is not None else [False] * nt
        self.out_shapes = [S((2 * s.shape[1], 4 * s.shape[2]), s.dtype) if ip else S((4,) + s.shape, s.dtype)
                           for s, ip in zip(shards, self.in_place)]
        self.sem_shapes = [pltpu.SemaphoreType.DMA((nt, 6)), pltpu.SemaphoreType.DMA((nt, 6)),
                           pltpu.SemaphoreType.DMA((nt, 2))]
        self.results = None

    def _place(self, couts, t, chip, half):
        cid = 2 * chip[0] + chip[1]
        if not self.in_place[t]:
            return couts[t].at[cid, half]
        _, r, c = self.ins[t].shape
        row0 = half * r if isinstance(half, int) else pl.multiple_of(half * r, 16)
        return couts[t].at[pl.ds(row0, r), pl.ds(pl.multiple_of(cid * c, 128), c)]

    def _copy(self, couts, sems, t, k, chip, half, to, src=None):
        dst = self._place(couts, t, chip, half)
        return pltpu.make_async_remote_copy(
            src_ref=dst if src is None else src, dst_ref=dst,
            send_sem=sems[0].at[t, k], recv_sem=sems[1].at[t, k], device_id=to, device_id_type=MESH)

    def _local(self, cins, couts, sems, t):
        x, y, _ = _pos()
        return [pltpu.make_async_copy(cins[t].at[half], self._place(couts, t, (x, y), half), sems[2].at[t, half])
                for half in range(2)]

    def start(self, cins, couts, sems):
        x, y, c = _pos()
        for t in range(self.nt):
            for cp in self._local(cins, couts, sems, t):
                cp.start()
            for j, chip in enumerate(_other_chips(x, y)):
                self._copy(couts, sems, t, j, (x, y), c, (*chip, c), src=cins[t].at[c]).start()

    def mid(self, cins, couts, sems):
        x, y, c = _pos()
        for t in range(self.nt):
            for j, chip in enumerate(_other_chips(x, y)):
                self._copy(couts, sems, t, j, chip, c, (x, y, c)).wait_recv()
                self._copy(couts, sems, t, 3 + j, chip, c, (x, y, 1 - c)).start()

    def finish(self, cins, couts, sems):
        x, y, c = _pos()
        for t in range(self.nt):
            for j, chip in enumerate(_other_chips(x, y)):
                self._copy(couts, sems, t, 3 + j, chip, 1 - c, (x, y, c)).wait_recv()
        for t in range(self.nt):
            for j, chip in enumerate(_other_chips(x, y)):
                self._copy(couts, sems, t, j, (x, y), c, (*chip, c), src=cins[t].at[c]).wait_send()
                self._copy(couts, sems, t, 3 + j, chip, c, (x, y, 1 - c)).wait_send()
            for cp in self._local(cins, couts, sems, t):
                cp.wait()


class SwapComm:
    def __init__(self, pieces):
        self.ins = list(pieces)
        self.nt = nt = len(pieces)
        self.out_shapes = [S((4,) + p.shape[2:], p.dtype) for p in pieces]
        self.sem_shapes = [pltpu.SemaphoreType.DMA((nt, 4)), pltpu.SemaphoreType.DMA((nt, 4))]
        self.results = None

    def _copies(self, cins, couts, sems):
        x, y, c = _pos()
        return [pltpu.make_async_remote_copy(src_ref=cins[t].at[s, 1 - c], dst_ref=couts[t].at[s],
                                             send_sem=sems[0].at[t, s], recv_sem=sems[1].at[t, s],
                                             device_id=(x, y, 1 - c), device_id_type=MESH)
                for t in range(self.nt) for s in range(4)]

    def start(self, cins, couts, sems):
        for cp in self._copies(cins, couts, sems):
            cp.start()

    def mid(self, cins, couts, sems):
        pass

    def finish(self, cins, couts, sems):
        for cp in self._copies(cins, couts, sems):
            cp.wait()


class ExchangeComm:
    def __init__(self, arrs):
        self.ins = list(arrs)
        self.nt = nt = len(arrs)
        self.out_shapes = [S((2,) + a.shape, a.dtype) for a in arrs]
        self.sem_shapes = [pltpu.SemaphoreType.DMA((nt, 7)), pltpu.SemaphoreType.DMA((nt, 7)),
                           pltpu.SemaphoreType.DMA((nt,))]
        self.results = None

    def _copy(self, couts, sems, t, k, half, src_chip, to, src=None):
        dst = couts[t].at[half, src_chip]
        return pltpu.make_async_remote_copy(
            src_ref=dst if src is None else src, dst_ref=dst,
            send_sem=sems[0].at[t, k], recv_sem=sems[1].at[t, k], device_id=to, device_id_type=MESH)

    def _local(self, cins, couts, sems, t):
        x, y, c = _pos()
        return pltpu.make_async_copy(cins[t].at[2 * x + y], couts[t].at[c, 2 * x + y], sems[2].at[t])

    def _firsts(self, cins, couts, sems, t):
        x, y, c = _pos()
        me = 2 * x + y
        cps = [self._copy(couts, sems, t, j, c, me, (*chip, c), src=cins[t].at[2 * chip[0] + chip[1]])
               for j, chip in enumerate(_other_chips(x, y))]
        return cps + [self._copy(couts, sems, t, 6, c, me, (x, y, 1 - c), src=cins[t].at[me])]

    def start(self, cins, couts, sems):
        for t in range(self.nt):
            self._local(cins, couts, sems, t).start()
            for cp in self._firsts(cins, couts, sems, t):
                cp.start()

    def mid(self, cins, couts, sems):
        x, y, c = _pos()
        for t in range(self.nt):
            for j, chip in enumerate(_other_chips(x, y)):
                cid = 2 * chip[0] + chip[1]
                self._copy(couts, sems, t, j, c, cid, (x, y, c)).wait_recv()
                self._copy(couts, sems, t, 3 + j, c, cid, (x, y, 1 - c)).start()

    def finish(self, cins, couts, sems):
        x, y, c = _pos()
        for t in range(self.nt):
            for j, chip in enumerate(_other_chips(x, y)):
                self._copy(couts, sems, t, 3 + j, 1 - c, 2 * chip[0] + chip[1], (x, y, c)).wait_recv()
            self._copy(couts, sems, t, 6, 1 - c, 2 * x + y, (x, y, c)).wait_recv()
        for t in range(self.nt):
            for cp in self._firsts(cins, couts, sems, t):
                cp.wait_send()
            for j, chip in enumerate(_other_chips(x, y)):
                self._copy(couts, sems, t, 3 + j, c, 2 * chip[0] + chip[1], (x, y, 1 - c)).wait_send()
            self._local(cins, couts, sems, t).wait()


def gather_small(vec, *, name):
    R, C = vec.shape

    def body(v_ref, out_ref, send_sems, recv_sems):
        x, y, c = _pos()
        me = 4 * x + 2 * y + c
        out_ref[me] = v_ref[...]
        cps = []
        def peer(k):
            fx, fy, fc = (k >> 2) & 1, (k >> 1) & 1, k & 1
            return (1 - x if fx else x), (1 - y if fy else y), (1 - c if fc else c)

        for k in range(1, 8):
            cp = pltpu.make_async_remote_copy(src_ref=v_ref, dst_ref=out_ref.at[me], send_sem=send_sems.at[k - 1],
                                              recv_sem=recv_sems.at[k - 1], device_id=peer(k), device_id_type=MESH)
            cp.start()
            cps.append(cp)
        for k in range(1, 8):
            px, py, pc = peer(k)
            pltpu.make_async_remote_copy(src_ref=v_ref, dst_ref=out_ref.at[4 * px + 2 * py + pc],
                                         send_sem=send_sems.at[k - 1], recv_sem=recv_sems.at[k - 1],
                                         device_id=(px, py, pc), device_id_type=MESH).wait_recv()
        for cp in cps:
            cp.wait_send()

    return pl.pallas_call(
        body, name=name,
        in_specs=[pl.BlockSpec(memory_space=pltpu.VMEM)], out_specs=pl.BlockSpec(memory_space=pltpu.VMEM),
        out_shape=S((8, R, C), vec.dtype),
        scratch_shapes=[pltpu.SemaphoreType.DMA((7,)), pltpu.SemaphoreType.DMA((7,))],
    )(vec)


def _tile(n, prefs):
    for p in prefs:
        if n % p == 0:
            return p
    return n


def _lanes(g, reps):
    return jnp.tile(g.reshape(1, -1), (1, reps))


class _NoRide:
    def rider(self, name):
        return None

    def landed(self, comm):
        pass

    def grad(self, name, val):
        pass


def _layer_fwd(x, p, e, ride=_NoRide()):
    T, D = x.shape
    tm = _tile(T, (512, 256, 128))
    tall = _tile(T, (1024, 512, 256, 128))

    def carried(fn, *args, name, **kw):
        comm = ride.rider(name)
        out = fn(*args, name=name, comm=comm, **kw)
        ride.landed(comm)
        return out

    h, proj = carried(rms_proj, x, p["norm1_g"], p["w_in"], tm=tall, tn=_tile(IN_W, (1920,)), name="rms_proj")
    gains = (_lanes(p["a_q_g"], 8), _lanes(p["a_k_g"], 8), _lanes(p["c_q_g"], 8), _lanes(p["c_k_g"], 2))
    aq, ak, av, cq, ckk, cvv = prep_fwd(proj, e, gains, tm=_tile(T, (256, 128)), name="prep_fwd")
    ols = []
    for d in DILATIONS:
        ols += carried(band_attn_fwd, aq, ak, av, None, dil=d, max_dist=A_DIST, group=1, name=f"dil_attn_fwd_{d}")
    mix = dil_combine_fwd(ols, tm=tm, name="dil_combine_fwd")
    mix = conv_fwd(proj, mix, p["conv_w"], p["conv_b"], p["conv_ln_g"], p["conv_ln_b"], tb=tm, name="conv_fwd")
    sinks = jnp.repeat(p["c_sinks"].reshape(-1), HD).reshape(1, C_W)
    o_c, l_c, mix = carried(band_attn_fwd, cq, ckk, cvv, sinks, dil=1, max_dist=C_DIST, group=4, mix=mix,
                            name="swa_attn_fwd")
    x1 = carried(matmul_res, mix, p["w_out"], x, tm=tall, tn=_tile(D, (1024, 512, 256)), name="out_proj")
    F = p["w_gate"].shape[1]
    h2, gate, up, act = carried(rms_swiglu, x1, p["norm2_g"], p["w_gate"], p["w_up"], tm=tall,
                                tn=_tile(F, (512, 256, 128)), name="rms_swiglu")
    x2 = carried(matmul_res, act, p["w_down"], x1, tm=tall, tn=_tile(D, (512, 256)), name="ffn_down")
    saved = dict(x=x, h=h, proj=proj, gains=gains, aq=aq, ak=ak, av=av, cq=cq, ckk=ckk, cvv=cvv, ols=ols, o_c=o_c,
                 l_c=l_c, sinks=sinks, mix=mix, x1=x1, h2=h2, gate=gate, up=up, act=act)
    return x2, saved


def _layer_bwd(dx2, dx2b, p, s, e, ride=_NoRide()):
    T, D = dx2.shape
    F = p["w_gate"].shape[1]
    tm = _tile(T, (512, 256, 128))
    tmm = _tile(T, (1024, 512, 256, 128))
    tkT = _tile(T, (2048, 1024, 512))
    tF = _tile(F, (512, 256, 128))
    tD = _tile(D, (1024, 512, 256))
    g = {}

    def carried(fn, *args, name, **kw):
        comm = ride.rider(name)
        out = fn(*args, name=name, comm=comm, **kw)
        ride.landed(comm)
        return out

    def big(n, val):
        g[n] = val
        ride.grad(n, val)

    d_gate, d_up = carried(nt_swiglu_bwd, dx2b, p["w_down"], s["gate"], s["up"], tm=tmm, tn=tF, name="ffn_down_bwd")
    big("w_down", tn_matmul(s["act"], dx2b, tm=_tile(F, (1408, 512, 256, 128)), tn=tD, tk=tkT, name="grad_w_down"))
    big("w_gate", carried(tn_matmul, s["h2"], d_gate, tm=tD, tn=tF, tk=tkT, by_chip=True, name="grad_w_gate"))
    big("w_up", carried(tn_matmul, s["h2"], d_up, tm=tD, tn=tF, tk=tkT, by_chip=True, name="grad_w_up"))
    dx1, dx1b, g["norm2_g"] = carried(nt_rms_bwd, [(d_gate, p["w_gate"]), (d_up, p["w_up"])], s["x1"], p["norm2_g"],
                                      dx2, tm=tm, tn=_tile(D, (512, 256)), name="ffn_in_bwd")
    dmix = nt_plain(dx1b, p["w_out"], tm=tmm, tn=tD, name="out_proj_bwd")
    big("w_out", tn_matmul(s["mix"], dx1b, tm=1024, tn=tD, tk=tkT, name="grad_w_out"))
    dos = dil_combine_bwd(s["ols"], dmix, e, tm=tm, name="dil_combine_bwd")
    da = []
    for n, d in enumerate(DILATIONS):
        da += carried(band_attn_bwd, s["aq"], s["ak"], s["av"], s["ols"][2 * n + 1], dos[n], dos[3 + n], dil=d,
                      max_dist=A_DIST, group=1, name=f"dil_attn_bwd_{d}")
    du, dgt, gw, gb, glg, glb = carried(conv_bwd, s["proj"], dmix, p["conv_w"], p["conv_b"], p["conv_ln_g"],
                                        p["conv_ln_b"], tb=tm, name="conv_bwd")
    g["conv_w"], g["conv_b"], g["conv_ln_g"], g["conv_ln_b"] = gw[:CONV_K], gb, glg, glb
    do_c, dd_c, dsink = swa_pre_bwd(s["o_c"], s["l_c"], dmix, s["sinks"], e, tm=_tile(T, (256, 128)), name="swa_pre_bwd")
    g["c_sinks"] = dsink.reshape(-1, HD)[:, 0]
    dcq, dckk, dcvv = carried(band_attn_bwd, s["cq"], s["ckk"], s["cvv"], s["l_c"], do_c, dd_c, dil=1, max_dist=C_DIST,
                              group=4, name="swa_attn_bwd")
    dproj, gaq, gak, gcq, gck = prep_bwd(s["proj"], e, s["gains"], da, (dcq, dckk, dcvv), (du, dgt),
                                         tm=_tile(T, (256, 128)), name="prep_bwd")
    g["a_q_g"] = gaq.reshape(-1, HD).sum(0)
    g["a_k_g"] = gak.reshape(-1, HD).sum(0)
    g["c_q_g"] = gcq.reshape(-1, HD).sum(0)
    g["c_k_g"] = gck.reshape(-1, HD).sum(0)
    big("w_in", tn_matmul(s["h"], dproj, tm=tD, tn=_tile(IN_W, (1280,)), tk=tkT, name="grad_w_in"))
    dx, dxb, g["norm1_g"] = carried(nt_rms_bwd, [(dproj, p["w_in"])], s["x"], p["norm1_g"], dx1, tm=tmm,
                                    tn=_tile(D, (512, 256)), name="in_proj_bwd")
    return dx, dxb, g


BIG = ("w_in", "w_out", "w_gate", "w_up", "w_down")
COL_SHARDED = ("w_in", "w_gate", "w_up")
SMALL = ("norm1_g", "a_q_g", "a_k_g", "conv_w", "conv_b", "conv_ln_g", "conv_ln_b", "c_q_g", "c_k_g", "c_sinks", "norm2_g")


def _to_pieces(g, name):
    if g.ndim == 3:
        return g.reshape(4, 2, g.shape[1] // 2, g.shape[2])
    R, C = g.shape
    if name in COL_SHARDED:
        return g.reshape(2, R // 2, 4, C // 4).transpose(2, 0, 1, 3)
    return g.reshape(4, 2, R // 8, C)


def _from_gathered(w, name):
    _, _, r, c = w.shape
    if name in COL_SHARDED:
        return w.transpose(1, 2, 0, 3).reshape(2 * r, 4 * c)
    return w.reshape(8 * r, c)


def _shard(W, l, n):
    w = W[n][l]
    return w.astype(MXU).reshape(2, w.shape[0] // 2, w.shape[1])


class _LayerParams(dict):
    def __init__(self, layer, full, small):
        super().__init__(small)
        self.layer, self.full = layer, full

    def __missing__(self, n):
        return self.full[(self.layer, n)]


FWD_RIDES = {
    (0, "rms_proj"): ((0, "w_out"), (0, "w_gate")),
    (0, "swa_attn_fwd"): ((0, "w_up"),),
    (0, "rms_swiglu"): ((0, "w_down"), (1, "w_in"), (1, "w_out")),
    (0, "ffn_down"): ((1, "w_gate"),),
    (1, "rms_proj"): ((1, "w_up"),),
    (1, "rms_swiglu"): ((1, "w_down"),),
}
BWD_RIDES = {
    "grad_w_gate": (("w_down",), ()),
    "grad_w_up": (("w_gate",), ("w_down",)),
    "ffn_in_bwd": (("w_up",), ("w_gate",)),
    "dil_attn_bwd_1": (("w_out",), ()),
    "dil_attn_bwd_16": ((), ("w_up",)),
    "conv_bwd": ((), ("w_out",)),
    "in_proj_bwd": (("w_in",), ()),
}
IN_PLACE = ("w_gate", "w_up")


class _FwdRide:
    def __init__(self, layer, W, full):
        self.layer, self.W, self.full = layer, W, full

    def rider(self, name):
        keys = FWD_RIDES.get((self.layer, name))
        if not keys:
            return None
        comm = GatherComm([_shard(self.W, l, n) for l, n in keys], [n in IN_PLACE for _, n in keys])
        comm.keys = keys
        return comm

    def landed(self, comm):
        if comm is not None:
            for (l, n), g in zip(comm.keys, comm.results):
                self.full[(l, n)] = g if n in IN_PLACE else _from_gathered(g, n)


class _GradFlow:
    def __init__(self):
        self.pieces, self.sums, self.landed, self.pending = {}, {}, {}, []

    def swap(self, keys):
        comm = SwapComm([self.pieces[k] for k in keys])
        comm.keys, comm.kind = list(keys), "swap"
        return comm

    def exchange(self, keys):
        if not keys:
            return None
        comm = ExchangeComm([self.sums[k] for k in keys])
        comm.keys, comm.kind = list(keys), "exchange"
        return comm

    def take_pending(self):
        keys, self.pending = self.pending, []
        return keys

    def land(self, comm):
        if comm is None:
            return
        if isinstance(comm, MultiComm):
            for sub in comm.comms:
                self.land(sub)
            return
        for k, res in zip(comm.keys, comm.results):
            if comm.kind == "swap":
                r = res.shape[1]
                self.sums[k] = add_halves(self.pieces[k], res, tr=_tile(r, (256, 176, 128, 64, 32, 16)),
                                          name="grad_chip_sum")
                if k[1] == "w_in":
                    self.pending.append(k)
            else:
                self.landed[k] = res


class _BwdRide:
    def __init__(self, layer, flow):
        self.layer, self.flow = layer, flow
        self.final = layer == 0

    def grad(self, name, val):
        self.flow.pieces[(self.layer, name)] = _to_pieces(val, name)
        if self.final and name == "w_in":
            swap = self.flow.swap([(self.layer, name)])
            _run_comm(swap, name="swap_last")
            self.flow.land(swap)

    def rider(self, name):
        if name == "ffn_down_bwd" or (self.final and name == "in_proj_bwd"):
            return self.flow.exchange(self.flow.take_pending())
        swaps, exchanges = BWD_RIDES.get(name, ((), ()))
        comms = []
        if swaps:
            comms.append(self.flow.swap([(self.layer, n) for n in swaps]))
        if exchanges:
            comms.append(self.flow.exchange([(self.layer, n) for n in exchanges]))
        return MultiComm(comms) if comms else None

    def landed(self, comm):
        self.flow.land(comm)


def _pack(items, rows):
    flat = jnp.concatenate([a.reshape(-1).astype(f32) for a in items])
    return jnp.pad(flat, (0, rows * 128 - flat.shape[0])).reshape(rows, 128)


def _unpack(packed, shapes):
    flat = packed.reshape(-1)
    out, off = [], 0
    for shp in shapes:
        n = 1
        for d in shp:
            n *= d
        out.append(flat[off:off + n].reshape(shp))
        off += n
    return out


def kernel(x, norm1_g, w_in, a_q_g, a_k_g, conv_w, conv_b, conv_ln_g, conv_ln_b, c_q_g, c_k_g, c_sinks, w_out, norm2_g, w_gate, w_up, w_down, loss_target, m_norm1_g, m_w_in, m_a_q_g, m_a_k_g, m_conv_w, m_conv_b, m_conv_ln_g, m_conv_ln_b, m_c_q_g, m_c_k_g, m_c_sinks, m_w_out, m_norm2_g, m_w_gate, m_w_up, m_w_down, v_norm1_g, v_w_in, v_a_q_g, v_a_k_g, v_conv_w, v_conv_b, v_conv_ln_g, v_conv_ln_b, v_c_q_g, v_c_k_g, v_c_sinks, v_w_out, v_norm2_g, v_w_gate, v_w_up, v_w_down):
    W = dict(norm1_g=norm1_g, w_in=w_in, a_q_g=a_q_g, a_k_g=a_k_g, conv_w=conv_w, conv_b=conv_b, conv_ln_g=conv_ln_g,
             conv_ln_b=conv_ln_b, c_q_g=c_q_g, c_k_g=c_k_g, c_sinks=c_sinks, w_out=w_out, norm2_g=norm2_g, w_gate=w_gate,
             w_up=w_up, w_down=w_down)
    M = dict(norm1_g=m_norm1_g, w_in=m_w_in, a_q_g=m_a_q_g, a_k_g=m_a_k_g, conv_w=m_conv_w, conv_b=m_conv_b,
             conv_ln_g=m_conv_ln_g, conv_ln_b=m_conv_ln_b, c_q_g=m_c_q_g, c_k_g=m_c_k_g, c_sinks=m_c_sinks, w_out=m_w_out,
             norm2_g=m_norm2_g, w_gate=m_w_gate, w_up=m_w_up, w_down=m_w_down)
    V = dict(norm1_g=v_norm1_g, w_in=v_w_in, a_q_g=v_a_q_g, a_k_g=v_a_k_g, conv_w=v_conv_w, conv_b=v_conv_b,
             conv_ln_g=v_conv_ln_g, conv_ln_b=v_conv_ln_b, c_q_g=v_c_q_g, c_k_g=v_c_k_g, c_sinks=v_c_sinks, w_out=v_w_out,
             norm2_g=v_norm2_g, w_gate=v_w_gate, w_up=v_w_up, w_down=v_w_down)
    depth = norm1_g.shape[0]
    T, D = x.shape[1], x.shape[2]
    xs = x.reshape(T, D)
    chip = 2 * lax.axis_index("x") + lax.axis_index("y")
    e = _head_eye()

    full = {}
    first = GatherComm([_shard(W, 0, "w_in"), conv_w])
    _run_comm(first, name="gather_first")
    full[(0, "w_in")] = _from_gathered(first.results[0], "w_in")
    conv_full = first.results[1].transpose(1, 2, 0, 3).reshape(depth, CONV_K, B_W)
    params = []
    for l in range(depth):
        small = {n: W[n][l].reshape(1, -1) for n in SMALL if n != "conv_w"}
        small["conv_w"] = jnp.pad(conv_full[l], ((0, HALO - CONV_K), (0, 0)))
        params.append(_LayerParams(l, full, small))

    saved = []
    act = xs
    for l in range(depth):
        act, s = _layer_fwd(act, params[l], e, _FwdRide(l, W, full))
        saved.append(s)
    dy, dyb, loss_part = loss_head(act, loss_target.reshape(T, D), tm=_tile(T, (512, 256, 128)), name="loss_head")
    grads = [None] * depth
    flow = _GradFlow()
    for l in reversed(range(depth)):
        dy, dyb, grads[l] = _layer_bwd(dy, dyb, params[l], saved[l], e, _BwdRide(l, flow))
    grad_x = dy.reshape(x.shape)

    out = {}
    for n in ("w_down", "w_gate", "w_up", "w_out", "w_in"):
        last = flow.exchange(flow.take_pending())
        per_layer = [flow.landed[(l, n)] for l in range(depth)]
        r, cc = per_layer[0].shape[2], per_layer[0].shape[3]
        srcs = [jnp.stack([pl_[:, s].reshape(2 * r, cc) for pl_ in per_layer]) for s in range(4)]
        out[n] = adamw(W[n], M[n], V[n], srcs, tr=_tile(2 * r, (256, 176, 128, 64, 32, 16)), name="adamw_" + n,
                       comm=last)
        flow.land(last)

    small_shapes = []
    items = []
    for l in range(depth):
        for n in SMALL:
            a = grads[l][n]
            if n == "conv_w":
                a = a.reshape(CONV_K, 4, B_W // 4).transpose(1, 0, 2)
            items.append(a)
            small_shapes.append(a.shape)
    items.append(loss_part[0, 0:1])
    small_shapes.append((1,))
    total = sum(int(jnp.size(a)) for a in items)
    rows = -(-total // 1024) * 8
    summed = sum8(gather_small(_pack(items, rows), name="gather_small"), name="sum_small")
    parts = _unpack(summed, small_shapes)
    loss = parts[-1][0]
    small_g = {n: [] for n in SMALL}
    for l in range(depth):
        for i, n in enumerate(SMALL):
            a = parts[l * len(SMALL) + i]
            if n == "conv_w":
                a = lax.dynamic_index_in_dim(a, chip, axis=0, keepdims=False)
            small_g[n].append(a.reshape(W[n].shape[1:]))
    sw = [W[n] for n in SMALL]
    sm = [M[n] for n in SMALL]
    sv = [V[n] for n in SMALL]
    sg = [jnp.stack(small_g[n]) for n in SMALL]
    tot2 = sum(int(jnp.size(a)) for a in sw)
    rows2 = -(-tot2 // 1024) * 8
    res = adamw(_pack(sw, rows2)[None], _pack(sm, rows2)[None], _pack(sv, rows2)[None], [_pack(sg, rows2)[None]],
                tr=rows2, name="adamw_small")
    shapes2 = [a.shape for a in sw]
    small_out = [_unpack(r[0], shapes2) for r in res]
    for i, n in enumerate(SMALL):
        out[n] = [small_out[k][i] for k in range(4)]

    order = ("norm1_g", "w_in", "a_q_g", "a_k_g", "conv_w", "conv_b", "conv_ln_g", "conv_ln_b", "c_q_g", "c_k_g",
             "c_sinks", "w_out", "norm2_g", "w_gate", "w_up", "w_down")
    return (loss, grad_x, *[out[n][0] for n in order], *[out[n][1] for n in order], *[out[n][2] for n in order],
            *[out[n][3] for n in order])
```

```python
import functools

import jax
import jax.numpy as jnp
from jax import lax
from jax.experimental import pallas as pl
from jax.experimental.pallas import tpu as pltpu

f32 = jnp.float32
MXU = jnp.bfloat16
S = jax.ShapeDtypeStruct
MESH = pl.DeviceIdType.MESH

EPS = 1e-6
NEG = -1e30
HD = 64
BLK = 128
A_W, B_W, C_W = 512, 512, 1024
KV_W = 128
IN_W = 3 * A_W + 2 * B_W + C_W + 2 * KV_W
CONV_K = 31
HALO = 32
DILATIONS = (1, 4, 16)
A_DIST, C_DIST = 128, 127
SCALE = HD ** -0.5
VMEM_LIMIT = 56 * 1024 * 1024
VMEM_TALL = 62 * 1024 * 1024

ADAM_LR, ADAM_B1, ADAM_B2, ADAM_EPS, ADAM_WD, ADAM_STEP = 0.001, 0.9, 0.999, 1e-08, 0.01, 10


def _cp(*sem, vmem=VMEM_LIMIT):
    return pltpu.CompilerParams(dimension_semantics=sem, vmem_limit_bytes=vmem)


ANY = pl.BlockSpec(memory_space=pl.ANY)


def _pallas(body, *, comm=None, name, grid, in_specs, out_specs, out_shape, scratch_shapes=(), compiler_params,
            input_output_aliases=None):
    aliases = dict(input_output_aliases or {})
    if comm is None:
        return pl.pallas_call(body, name=name, grid=grid, in_specs=in_specs, out_specs=out_specs, out_shape=out_shape,
                              scratch_shapes=list(scratch_shapes), compiler_params=compiler_params,
                              input_output_aliases=aliases)
    single = not isinstance(out_shape, (list, tuple))
    o_shapes = [out_shape] if single else list(out_shape)
    o_specs = [out_specs] if single else list(out_specs)
    n_in, n_out, n_sc = len(in_specs), len(o_shapes), len(scratch_shapes)
    nci, nco = len(comm.ins), len(comm.out_shapes)
    total = 1
    for g in grid:
        total *= g

    def carried(*refs):
        ins, cins = refs[:n_in], refs[n_in:n_in + nci]
        o0 = n_in + nci
        outs, couts = refs[o0:o0 + n_out], refs[o0 + n_out:o0 + n_out + nco]
        s0 = o0 + n_out + nco
        scratch, sems = refs[s0:s0 + n_sc], refs[s0 + n_sc:]
        step = pl.program_id(0)
        for axis in range(1, len(grid)):
            step = step * grid[axis] + pl.program_id(axis)

        @pl.when(step == 0)
        def _():
            comm.start(cins, couts, sems)

        body(*ins, *outs, *scratch)

        @pl.when(step == (3 * total) // 4)
        def _():
            comm.mid(cins, couts, sems)

        @pl.when(step == total - 1)
        def _():
            comm.finish(cins, couts, sems)

    call = pl.pallas_call(carried, name=name, grid=grid, in_specs=list(in_specs) + [ANY] * nci,
                          out_specs=o_specs + [ANY] * nco, out_shape=o_shapes + list(comm.out_shapes),
                          scratch_shapes=list(scratch_shapes) + list(comm.sem_shapes), compiler_params=compiler_params,
                          input_output_aliases=aliases)

    def run(*args):
        res = call(*args, *comm.ins)
        comm.results = list(res[n_out:])
        return res[0] if single else list(res[:n_out])

    return run


class MultiComm:
    def __init__(self, comms):
        self.comms = list(comms)
        self.ins = [a for c in self.comms for a in c.ins]
        self.out_shapes = [s for c in self.comms for s in c.out_shapes]
        self.sem_shapes = [s for c in self.comms for s in c.sem_shapes]

    def _each(self, cins, couts, sems):
        i = o = s = 0
        for c in self.comms:
            ni, no, ns = len(c.ins), len(c.out_shapes), len(c.sem_shapes)
            yield c, cins[i:i + ni], couts[o:o + no], sems[s:s + ns]
            i, o, s = i + ni, o + no, s + ns

    def start(self, cins, couts, sems):
        for c, a, b, d in self._each(cins, couts, sems):
            c.start(a, b, d)

    def mid(self, cins, couts, sems):
        for c, a, b, d in self._each(cins, couts, sems):
            c.mid(a, b, d)

    def finish(self, cins, couts, sems):
        for c, a, b, d in self._each(cins, couts, sems):
            c.finish(a, b, d)

    @property
    def results(self):
        return [r for c in self.comms for r in c.results]

    @results.setter
    def results(self, vals):
        o = 0
        for c in self.comms:
            c.results = list(vals[o:o + len(c.out_shapes)])
            o += len(c.out_shapes)


def _run_comm(comm, *, name):
    nci, nco = len(comm.ins), len(comm.out_shapes)

    def body(*refs):
        cins, couts, sems = refs[:nci], refs[nci:nci + nco], refs[nci + nco:]
        comm.start(cins, couts, sems)
        comm.mid(cins, couts, sems)
        comm.finish(cins, couts, sems)

    comm.results = list(pl.pallas_call(body, name=name, in_specs=[ANY] * nci, out_specs=[ANY] * nco,
                                       out_shape=list(comm.out_shapes), scratch_shapes=list(comm.sem_shapes))(*comm.ins))


def _nt(a, b):
    return lax.dot_general(a, b, (((1,), (1,)), ((), ())), preferred_element_type=f32)


def _tn(a, b):
    return lax.dot_general(a, b, (((0,), (0,)), ((), ())), preferred_element_type=f32)


def _nn(a, b):
    return jnp.dot(a, b, preferred_element_type=f32)


def _sigmoid(x):
    return 1.0 / (1.0 + jnp.exp(-x))


def _seg_sum(v, e_ref):
    hi = v.astype(jnp.bfloat16)
    lo = (v - hi.astype(f32)).astype(jnp.bfloat16)
    e = e_ref[...]
    return _nn(hi, e) + _nn(lo, e)


def _seg_sum128(v, e_ref):
    e = e_ref[0:128, 0:128]
    hi = v.astype(jnp.bfloat16)
    lo = (v - hi.astype(f32)).astype(jnp.bfloat16)
    return _nn(hi, e) + _nn(lo, e)


def _head_eye():
    r = lax.broadcasted_iota(jnp.int32, (512, 512), 0) // HD
    c = lax.broadcasted_iota(jnp.int32, (512, 512), 1) // HD
    return (r == c).astype(jnp.bfloat16)


def _rms_norm_rows(x_ref, g_ref, h_ref, tm):
    def chunk(c, carry):
        rows = pl.ds(c * BLK, BLK)
        xf = x_ref[rows, :]
        r = lax.rsqrt(jnp.mean(xf * xf, axis=-1, keepdims=True) + EPS)
        h_ref[rows, :] = (xf * r * g_ref[...]).astype(MXU)
        return carry
    lax.fori_loop(0, tm // BLK, chunk, 0)


def rms_proj(x, g, w, *, tm, tn, name, comm=None):
    T, D = x.shape
    N = w.shape[1]
    ni, nj = T // tm, N // tn
    rc = tm // nj

    def body(x_ref, g_ref, w_ref, h_ref, o_ref, hbuf):
        i, j = pl.program_id(0), pl.program_id(1)

        @pl.when(i < ni)
        def _():
            def chunk(c, carry):
                rows = pl.ds(c * BLK, BLK)
                xf = x_ref[rows, :]
                r = lax.rsqrt(jnp.mean(xf * xf, axis=-1, keepdims=True) + EPS)
                hv = (xf * r * g_ref[...]).astype(MXU)
                h_ref[rows, :] = hv
                hbuf[i % 2, pl.ds(pl.multiple_of(j * rc + c * BLK, BLK), BLK), :] = hv
                return carry
            lax.fori_loop(0, rc // BLK, chunk, 0)

        @pl.when(i > 0)
        def _():
            o_ref[...] = _nn(hbuf[(i - 1) % 2], w_ref[...])

    rows = pl.BlockSpec((rc, D), lambda i, j: (jnp.where(i < ni, i * nj + j, ni * nj - 1), 0))
    return _pallas(
        body, comm=comm, name=name, grid=(ni + 1, nj),
        in_specs=[rows, pl.BlockSpec((1, D), lambda i, j: (0, 0)),
                  pl.BlockSpec((D, tn), lambda i, j: (0, jnp.where(i > 0, j, 0)))],
        out_specs=[rows, pl.BlockSpec((tm, tn), lambda i, j: (jnp.maximum(i - 1, 0), jnp.where(i > 0, j, 0)))],
        out_shape=[S((T, D), MXU), S((T, N), f32)],
        scratch_shapes=[pltpu.VMEM((2, tm, D), MXU)],
        compiler_params=_cp("arbitrary", "arbitrary"),
    )(x, g, w)


def rms_swiglu(x, g, wg, wu, *, tm, tn, name, comm=None):
    T, D = x.shape
    N = wg.shape[1]

    def body(x_ref, g_ref, wg_ref, wu_ref, h_ref, gate_ref, up_ref, act_ref):
        @pl.when(pl.program_id(1) == 0)
        def _():
            _rms_norm_rows(x_ref, g_ref, h_ref, tm)
        h = h_ref[...]
        gate = _nn(h, wg_ref[...])
        up = _nn(h, wu_ref[...])
        gate_ref[...] = gate
        up_ref[...] = up
        act_ref[...] = (gate * _sigmoid(gate) * up).astype(MXU)

    wspec = pl.BlockSpec((D, tn), lambda i, j: (0, j))
    ospec = pl.BlockSpec((tm, tn), lambda i, j: (i, j))
    return _pallas(
        body, comm=comm, name=name, grid=(T // tm, N // tn),
        in_specs=[pl.BlockSpec((tm, D), lambda i, j: (i, 0)), pl.BlockSpec((1, D), lambda i, j: (0, 0)), wspec, wspec],
        out_specs=[pl.BlockSpec((tm, D), lambda i, j: (i, 0)), ospec, ospec, ospec],
        out_shape=[S((T, D), MXU), S((T, N), f32), S((T, N), f32), S((T, N), MXU)],
        compiler_params=_cp("arbitrary", "arbitrary"),
    )(x, g, wg, wu)


def matmul_res(a, w, res, *, tm, tn, name, comm=None):
    T, K = a.shape
    N = w.shape[1]

    def body(a_ref, w_ref, r_ref, o_ref):
        o_ref[...] = r_ref[...] + _nn(a_ref[...], w_ref[...])

    return _pallas(
        body, comm=comm, name=name, grid=(T // tm, N // tn),
        in_specs=[pl.BlockSpec((tm, K), lambda i, j: (i, 0)), pl.BlockSpec((K, tn), lambda i, j: (0, j)),
                  pl.BlockSpec((tm, tn), lambda i, j: (i, j))],
        out_specs=pl.BlockSpec((tm, tn), lambda i, j: (i, j)),
        out_shape=S((T, N), f32),
        compiler_params=_cp("arbitrary", "arbitrary"),
    )(a, w, res)


def nt_plain(a, w, *, tm, tn, name, comm=None):
    T, K = a.shape
    N = w.shape[0]

    def body(a_ref, w_ref, o_ref):
        o_ref[...] = _nt(a_ref[...], w_ref[...])

    return _pallas(
        body, comm=comm, name=name, grid=(T // tm, N // tn),
        in_specs=[pl.BlockSpec((tm, K), lambda i, j: (i, 0)), pl.BlockSpec((tn, K), lambda i, j: (j, 0))],
        out_specs=pl.BlockSpec((tm, tn), lambda i, j: (i, j)),
        out_shape=S((T, N), f32),
        compiler_params=_cp("arbitrary", "arbitrary"),
    )(a, w)


def nt_swiglu_bwd(dy, wd, gate, up, *, tm, tn, name, comm=None):
    T, D = dy.shape
    F = wd.shape[0]

    def body(dy_ref, w_ref, g_ref, u_ref, dg_ref, du_ref):
        d_act = _nt(dy_ref[...], w_ref[...])
        g = g_ref[...]
        sg = _sigmoid(g)
        du_ref[...] = (d_act * (g * sg)).astype(MXU)
        dg_ref[...] = (d_act * u_ref[...] * (sg * (1.0 + g * (1.0 - sg)))).astype(MXU)

    blk = pl.BlockSpec((tm, tn), lambda i, j: (i, j))
    return _pallas(
        body, comm=comm, name=name, grid=(T // tm, F // tn),
        in_specs=[pl.BlockSpec((tm, D), lambda i, j: (i, 0)), pl.BlockSpec((tn, D), lambda i, j: (j, 0)), blk, blk],
        out_specs=[blk, blk],
        out_shape=[S((T, F), MXU), S((T, F), MXU)],
        compiler_params=_cp("arbitrary", "arbitrary"),
    )(dy, wd, gate, up)


def nt_rms_bwd(terms, x, g, dres, *, tm, tn, name, comm=None):
    T, D = x.shape
    K = terms[0][0].shape[1]
    nj = D // tn
    rc = tm // nj
    nt = len(terms)
    ni = T // tm

    def body(*refs):
        a_refs = refs[0:2 * nt:2]
        w_refs = refs[1:2 * nt:2]
        x_ref, g_ref, r_ref, dx_ref, dxb_ref, dg_ref, acc_ref = refs[2 * nt:]
        i, j = pl.program_id(0), pl.program_id(1)

        @pl.when(i < ni)
        def _():
            part = _nt(a_refs[0][...], w_refs[0][...])
            for t in range(1, nt):
                part += _nt(a_refs[t][...], w_refs[t][...])
            acc_ref[i % 2, j] = part

        @pl.when(i > 0)
        def _():
            rows = pl.ds(pl.multiple_of(j * rc, BLK), rc)
            dh = jnp.concatenate([acc_ref[(i - 1) % 2, jj, rows, :] for jj in range(nj)], axis=1)
            xf = x_ref[...]
            r = lax.rsqrt(jnp.mean(xf * xf, axis=-1, keepdims=True) + EPS)
            y = xf * r
            dy = dh * g_ref[...]
            dx = r_ref[...] + r * (dy - y * jnp.mean(dy * y, axis=-1, keepdims=True))
            dx_ref[...] = dx
            dxb_ref[...] = dx.astype(MXU)
            dgain = jnp.sum(dh * y, axis=0, keepdims=True)
            first = jnp.logical_and(i == 1, j == 0)

            @pl.when(first)
            def _():
                dg_ref[...] = dgain

            @pl.when(jnp.logical_not(first))
            def _():
                dg_ref[...] += dgain

    in_specs, args = [], []
    for a, w in terms:
        in_specs += [pl.BlockSpec((tm, K), lambda i, j: (jnp.minimum(i, ni - 1), 0)),
                     pl.BlockSpec((tn, K), lambda i, j: (jnp.where(i < ni, j, nj - 1), 0))]
        args += [a, w]
    row = pl.BlockSpec((rc, D), lambda i, j: (jnp.where(i > 0, (i - 1) * nj + j, 0), 0))
    vec = pl.BlockSpec((1, D), lambda i, j: (0, 0))
    in_specs += [row, vec, row]
    return _pallas(
        body, comm=comm, name=name, grid=(ni + 1, nj), in_specs=in_specs,
        out_specs=[row, row, vec],
        out_shape=[S((T, D), f32), S((T, D), MXU), S((1, D), f32)],
        scratch_shapes=[pltpu.VMEM((2, nj, tm, tn), f32)],
        compiler_params=_cp("arbitrary", "arbitrary", vmem=VMEM_TALL),
    )(*args, x, g, dres)


def tn_matmul(a, b, *, tm, tn, tk, name, by_chip=False, comm=None):
    T, M = a.shape
    N = b.shape[1]
    if by_chip:
        tn = N // 4
        out_spec = pl.BlockSpec((None, tm, tn), lambda i, j, k: (j, i, 0))
        out_shape = S((4, M, tn), f32)
    else:
        out_spec = pl.BlockSpec((tm, tn), lambda i, j, k: (i, j))
        out_shape = S((M, N), f32)

    def body(a_ref, b_ref, o_ref):
        part = _tn(a_ref[...], b_ref[...])

        @pl.when(pl.program_id(2) == 0)
        def _():
            o_ref[...] = part

        @pl.when(pl.program_id(2) > 0)
        def _():
            o_ref[...] += part

    return _pallas(
        body, comm=comm, name=name, grid=(M // tm, N // tn, T // tk),
        in_specs=[pl.BlockSpec((tk, tm), lambda i, j, k: (k, i)), pl.BlockSpec((tk, tn), lambda i, j, k: (k, j))],
        out_specs=out_spec, out_shape=out_shape,
        compiler_params=_cp("arbitrary", "arbitrary", "arbitrary"),
    )(a, b)


def loss_head(y, target, *, tm, name):
    T, D = y.shape
    ni = T // tm

    def body(y_ref, t_ref, dy_ref, dyb_ref, l_ref, acc_ref):
        i = pl.program_id(0)
        e = y_ref[...] - t_ref[...]
        dy = e * (1.0 / D)
        dy_ref[...] = dy
        dyb_ref[...] = dy.astype(MXU)
        part = jnp.sum(e * e, axis=0, keepdims=True)

        @pl.when(i == 0)
        def _():
            acc_ref[...] = part

        @pl.when(i > 0)
        def _():
            acc_ref[...] += part

        @pl.when(i == ni - 1)
        def _():
            tot = jnp.sum(acc_ref[...], axis=1, keepdims=True) * (0.5 / D)
            l_ref[...] = jnp.broadcast_to(tot, (1, 128))

    row = pl.BlockSpec((tm, D), lambda i: (i, 0))
    return pl.pallas_call(
        body, name=name, grid=(ni,), in_specs=[row, row],
        out_specs=[row, row, pl.BlockSpec((1, 128), lambda i: (0, 0))],
        out_shape=[S((T, D), f32), S((T, D), MXU), S((1, 128), f32)],
        scratch_shapes=[pltpu.VMEM((1, D), f32)],
        compiler_params=_cp("arbitrary"),
    )(y, target)


def _qk_norm(v, gain, e_ref):
    r = lax.rsqrt(_seg_sum(v * v, e_ref) * (1.0 / HD) + EPS)
    return v * r * gain


def _dup_halves(pair):
    rolled = pltpu.roll(pair, HD, 1)
    lo = lax.broadcasted_iota(jnp.int32, pair.shape, 1) < HD
    return jnp.where(lo, pair, rolled), jnp.where(lo, rolled, pair)


def prep_fwd(proj, e, gains, *, tm, name):
    T = proj.shape[0]

    def body(p_ref, e_ref, gaq, gak, gcq, gck, aq, ak, av, cq, ckk, cvv):
        aq[...] = _qk_norm(p_ref[:, 0:512], gaq[...], e_ref)
        ak[...] = _qk_norm(p_ref[:, 512:1024], gak[...], e_ref)
        av[...] = p_ref[:, 1024:1536]
        cq[:, 0:512] = _qk_norm(p_ref[:, 2560:3072], gcq[...], e_ref).astype(MXU)
        cq[:, 512:1024] = _qk_norm(p_ref[:, 3072:3584], gcq[...], e_ref).astype(MXU)
        kraw = p_ref[:, 3584:3712]
        kn = kraw * lax.rsqrt(_seg_sum128(kraw * kraw, e_ref) * (1.0 / HD) + EPS) * gck[...]
        k0, k1 = _dup_halves(kn)
        ckk[:, 0:128] = k0.astype(MXU)
        ckk[:, 128:256] = k1.astype(MXU)
        v0, v1 = _dup_halves(p_ref[:, 3712:3840])
        cvv[:, 0:128] = v0.astype(MXU)
        cvv[:, 128:256] = v1.astype(MXU)

    def vec(n):
        return pl.BlockSpec((1, n), lambda i: (0, 0))

    def rows(n):
        return pl.BlockSpec((tm, n), lambda i: (i, 0))

    return pl.pallas_call(
        body, name=name, grid=(T // tm,),
        in_specs=[rows(IN_W), pl.BlockSpec((512, 512), lambda i: (0, 0)), vec(512), vec(512), vec(512), vec(128)],
        out_specs=[rows(512), rows(512), rows(512), rows(1024), rows(256), rows(256)],
        out_shape=[S((T, 512), f32)] * 3 + [S((T, 1024), MXU), S((T, 256), MXU), S((T, 256), MXU)],
        compiler_params=_cp("arbitrary"),
    )(proj, e, *gains)


def _band_mask(max_dist, shut):
    r = lax.broadcasted_iota(jnp.int32, (2 * BLK, 2 * BLK), 0) & (BLK - 1)
    c = lax.broadcasted_iota(jnp.int32, (2 * BLK, 2 * BLK), 1)
    prev = jnp.logical_and(c < BLK, c >= r + (BLK - max_dist) + shut)
    return jnp.logical_or(prev, jnp.logical_and(c >= BLK, c - BLK <= r))


def _prev_mask(max_dist, shut):
    r = lax.broadcasted_iota(jnp.int32, (2 * BLK, BLK), 0) & (BLK - 1)
    c = lax.broadcasted_iota(jnp.int32, (2 * BLK, BLK), 1)
    return c >= r + (BLK - max_dist) + shut


def _head_masks():
    lo = (lax.broadcasted_iota(jnp.int32, (BLK, BLK), 1) < HD).astype(f32)
    return lo.astype(MXU), (1.0 - lo).astype(MXU)


def _stack_heads(x, hm):
    return jnp.concatenate([x * hm[0], x * hm[1]], axis=0)


def _unstack_heads(y, lane_lo):
    return jnp.where(lane_lo, y[0:BLK], y[BLK:2 * BLK])


def _rows(ref, start, dil):
    if dil == 1:
        return ref[pl.ds(start, BLK), :]
    return ref[pl.ds(start, BLK, stride=dil), :]


def _set_rows(ref, start, dil, val):
    if dil == 1:
        ref[pl.ds(start, BLK), :] = val
    else:
        ref[pl.ds(start, BLK, stride=dil), :] = val


def _attn_geometry(T, dil):
    span = BLK * dil
    n = max(1, 512 // span)
    return span, n, T // (span * n)


def band_attn_fwd(q, k, v, sinks, *, dil, max_dist, group, name, mix=None, comm=None):
    T = q.shape[0]
    P = q.shape[1] // BLK
    span, n, nb = _attn_geometry(T, dil)

    def body(*refs):
        s_ref = m_ref = None
        q_ref, kc_ref, kp_ref, vc_ref, vp_ref = refs[:5]
        rest = list(refs[5:])
        if sinks is not None:
            s_ref = rest.pop(0)
        if mix is not None:
            rest.pop(0)
            o_ref, l_ref, m_ref = rest
        else:
            o_ref, l_ref = rest
        b = pl.program_id(0)
        mask = _band_mask(max_dist, 0)
        mask0 = _band_mask(max_dist, jnp.where(b > 0, 0, BLK + 1))
        lane_lo = lax.broadcasted_iota(jnp.int32, (BLK, BLK), 1) < HD
        hm = _head_masks()
        if sinks is not None:
            row_lo = lax.broadcasted_iota(jnp.int32, (1, BLK), 1) < HD
            sk0 = jnp.max(jnp.where(row_lo, s_ref[...], NEG), axis=1, keepdims=True)
            sk1 = jnp.max(jnp.where(row_lo, NEG, s_ref[...]), axis=1, keepdims=True)
            sk = jnp.where(lax.broadcasted_iota(jnp.int32, (2 * BLK, 1), 0) < BLK, sk0, sk1)

        def load(r, sub):
            at = r + sub * span
            kc, vc = _rows(kc_ref, at, dil).astype(MXU), _rows(vc_ref, at, dil).astype(MXU)
            if sub == 0:
                kp, vp = _rows(kp_ref, r, dil).astype(MXU), _rows(vp_ref, r, dil).astype(MXU)
            else:
                kp, vp = _rows(kc_ref, at - span, dil).astype(MXU), _rows(vc_ref, at - span, dil).astype(MXU)
            qst = _stack_heads(_rows(q_ref, at, dil).astype(MXU), hm)
            return (qst, jnp.concatenate([kp, kc], axis=0), jnp.concatenate([vp, vc], axis=0),
                    mask0 if sub == 0 else mask, at)

        def attend(items):
            ss = [jnp.where(m_, _nt(qst, kcat) * SCALE, NEG) for qst, kcat, _, m_, _ in items]
            ms = [jnp.max(s, axis=1, keepdims=True) for s in ss]
            if sinks is not None:
                ms = [jnp.maximum(m, sk) for m in ms]
            ps = [jnp.exp(s - m) for s, m in zip(ss, ms)]
            dens = [jnp.sum(p_, axis=1, keepdims=True) for p_ in ps]
            if sinks is not None:
                dens = [d + jnp.exp(sk - m) for d, m in zip(dens, ms)]
            outs = [_nn(p_.astype(MXU), it[2]) / d for p_, it, d in zip(ps, items, dens)]
            for it, o, m, d in zip(items, outs, ms, dens):
                lse = m + jnp.log(d)
                if m_ref is not None:
                    _set_rows(m_ref, it[4], dil, _unstack_heads(o, lane_lo).astype(MXU))
                _set_rows(o_ref, it[4], dil, _unstack_heads(o, lane_lo))
                _set_rows(l_ref, it[4], dil, jnp.where(lane_lo, lse[0:BLK], lse[BLK:2 * BLK]))

        if dil * n <= 4:
            work = [(r, sub) for r in range(dil) for sub in range(n)]
            for g in range(0, len(work), 2):
                attend([load(*w) for w in work[g:g + 2]])
        else:
            def two_streams(i, carry):
                attend([load(2 * i, 0), load(2 * i + 1, 0)])
                return carry
            lax.fori_loop(0, dil // 2, two_streams, 0)

    rows_per_step = span * n
    qspec = pl.BlockSpec((rows_per_step, BLK), lambda b, p: (b, p))
    cur = pl.BlockSpec((rows_per_step, BLK), lambda b, p: (b, p // group))
    prev = pl.BlockSpec((span, BLK), lambda b, p: (jnp.maximum(b * n - 1, 0), p // group))
    in_specs = [qspec, cur, prev, cur, prev]
    args = [q, k, k, v, v]
    if sinks is not None:
        in_specs.append(pl.BlockSpec((1, BLK), lambda b, p: (0, p)))
        args.append(sinks)
    out_specs, out_shape, aliases = [qspec, qspec], [S(q.shape, f32), S(q.shape, f32)], {}
    if mix is not None:
        first_block = mix.shape[1] // BLK - P
        aliases = {len(args): 2}
        in_specs.append(ANY)
        args.append(mix)
        out_specs.append(pl.BlockSpec((rows_per_step, BLK), lambda b, p: (b, first_block + p)))
        out_shape.append(S(mix.shape, mix.dtype))
    return _pallas(
        body, comm=comm, name=name, grid=(nb, P), in_specs=in_specs, out_specs=out_specs, out_shape=out_shape,
        compiler_params=_cp("arbitrary", "arbitrary"), input_output_aliases=aliases,
    )(*args)


def band_attn_bwd(q, k, v, lse, do, dd, *, dil, max_dist, group, name, comm=None):
    T = q.shape[0]
    P = q.shape[1] // BLK
    span, n, nb = _attn_geometry(T, dil)
    assert group == 1 or dil == 1

    def body(q_ref, qn_ref, do_ref, don_ref, l_ref, ln_ref, d_ref, dn_ref, kc_ref, kp_ref, vc_ref, vp_ref,
             dq_ref, dk_ref, dv_ref):
        b, p = pl.program_id(0), pl.program_id(1)
        mask = _band_mask(max_dist, 0)
        mask0 = _band_mask(max_dist, jnp.where(b > 0, 0, BLK + 1))
        tail = _prev_mask(max_dist, jnp.where(b < nb - 1, 0, BLK + 1))
        lane_lo = lax.broadcasted_iota(jnp.int32, (BLK, BLK), 1) < HD
        hm = _head_masks()
        own_lanes = (lax.broadcasted_iota(jnp.int32, (2 * BLK, BLK), 1) < HD) == (
            lax.broadcasted_iota(jnp.int32, (2 * BLK, BLK), 0) < BLK)

        def per_row(x):
            return jnp.max(jnp.where(own_lanes, jnp.concatenate([x, x], axis=0), NEG), axis=1, keepdims=True)

        def q_side(refs, at):
            q_r, do_r, l_r, d_r = refs
            return (_stack_heads(_rows(q_r, at, dil).astype(MXU), hm), _stack_heads(_rows(do_r, at, dil).astype(MXU), hm),
                    per_row(_rows(l_r, at, dil)), per_row(_rows(d_r, at, dil)))

        def kv(ref, at):
            return _rows(ref, at, dil).astype(MXU)

        first = p % group == 0

        def put_kv(ref, at, val):
            if group == 1:
                _set_rows(ref, at, dil, val)
            else:
                @pl.when(first)
                def _():
                    ref[pl.ds(at, BLK), :] = val

                @pl.when(jnp.logical_not(first))
                def _():
                    ref[pl.ds(at, BLK), :] += val

        def stream(r):
            dks, dvs = [None] * n, [None] * n
            for sub in range(n):
                at = r + sub * span
                qst, dost, lrow, drow = q_side((q_ref, do_ref, l_ref, d_ref), at)
                if sub == 0:
                    kp, vp, m_ = kv(kp_ref, r), kv(vp_ref, r), mask0
                else:
                    kp, vp, m_ = kv(kc_ref, at - span), kv(vc_ref, at - span), mask
                kcat = jnp.concatenate([kp, kv(kc_ref, at)], axis=0)
                vcat = jnp.concatenate([vp, kv(vc_ref, at)], axis=0)
                pr = jnp.where(m_, jnp.exp(_nt(qst, kcat) * SCALE - lrow), 0.0)
                ds = (pr * (_nt(dost, vcat) - drow) * SCALE).astype(MXU)
                prb = pr.astype(MXU)
                _set_rows(dq_ref, at, dil, _unstack_heads(_nn(ds, kcat), lane_lo))
                if sub == 0:
                    dks[0] = _tn(ds[:, BLK:], qst)
                    dvs[0] = _tn(prb[:, BLK:], dost)
                else:
                    dkk, dvv = _tn(ds, qst), _tn(prb, dost)
                    dks[sub - 1] += dkk[0:BLK]
                    dvs[sub - 1] += dvv[0:BLK]
                    dks[sub], dvs[sub] = dkk[BLK:], dvv[BLK:]
            at = r + (n - 1) * span
            qst, dost, lrow, drow = q_side((qn_ref, don_ref, ln_ref, dn_ref), r)
            pr = jnp.where(tail, jnp.exp(_nt(qst, kv(kc_ref, at)) * SCALE - lrow), 0.0)
            ds = (pr * (_nt(dost, kv(vc_ref, at)) - drow) * SCALE).astype(MXU)
            dks[n - 1] += _tn(ds, qst)
            dvs[n - 1] += _tn(pr.astype(MXU), dost)
            for sub in range(n):
                put_kv(dk_ref, r + sub * span, dks[sub])
                put_kv(dv_ref, r + sub * span, dvs[sub])

        if dil <= 4:
            for r in range(dil):
                stream(r)
        else:
            def two_streams(i, carry):
                stream(2 * i)
                stream(2 * i + 1)
                return carry
            lax.fori_loop(0, dil // 2, two_streams, 0)

    rows_per_step = span * n
    qspec = pl.BlockSpec((rows_per_step, BLK), lambda b, p: (b, p))
    qnext = pl.BlockSpec((span, BLK), lambda b, p: (jnp.minimum((b + 1) * n, T // span - 1), p))
    cur = pl.BlockSpec((rows_per_step, BLK), lambda b, p: (b, p // group))
    prev = pl.BlockSpec((span, BLK), lambda b, p: (jnp.maximum(b * n - 1, 0), p // group))
    return _pallas(
        body, comm=comm, name=name, grid=(nb, P),
        in_specs=[qspec, qnext, qspec, qnext, qspec, qnext, qspec, qnext, cur, prev, cur, prev],
        out_specs=[qspec, cur, cur],
        out_shape=[S(q.shape, f32), S(k.shape, f32), S(k.shape, f32)],
        compiler_params=_cp("arbitrary", "arbitrary"),
    )(q, q, do, do, lse, lse, dd, dd, k, k, v, v)


def dil_combine_fwd(ols, *, tm, name):
    T = ols[0].shape[0]

    def body(o1, l1, o2, l2, o3, l3, out_ref):
        a, b, c = l1[...], l2[...], l3[...]
        m = jnp.maximum(jnp.maximum(a, b), c)
        ea, eb, ec = jnp.exp(a - m), jnp.exp(b - m), jnp.exp(c - m)
        out = (ea * o1[...] + eb * o2[...] + ec * o3[...]) / (ea + eb + ec)
        out_ref[...] = out.astype(MXU)

    row = pl.BlockSpec((tm, 512), lambda i: (i, 0))
    return pl.pallas_call(body, name=name, grid=(T // tm,), in_specs=[row] * 6, out_specs=row,
                          out_shape=S((T, A_W + B_W + C_W), MXU), compiler_params=_cp("arbitrary"))(*ols)


def dil_combine_bwd(ols, dmix, e, *, tm, name):
    T = ols[0].shape[0]

    def body(o1, l1, o2, l2, o3, l3, d_ref, e_ref, do1, do2, do3, dd1, dd2, dd3):
        a, b, c = l1[...], l2[...], l3[...]
        m = jnp.maximum(jnp.maximum(a, b), c)
        ea, eb, ec = jnp.exp(a - m), jnp.exp(b - m), jnp.exp(c - m)
        inv = 1.0 / (ea + eb + ec)
        wa, wb, wc = ea * inv, eb * inv, ec * inv
        dout = d_ref[...]
        gbar = _seg_sum(dout * (wa * o1[...] + wb * o2[...] + wc * o3[...]), e_ref)
        do1[...] = wa * dout
        do2[...] = wb * dout
        do3[...] = wc * dout
        dd1[...] = wa * gbar
        dd2[...] = wb * gbar
        dd3[...] = wc * gbar

    row = pl.BlockSpec((tm, 512), lambda i: (i, 0))
    return pl.pallas_call(
        body, name=name, grid=(T // tm,),
        in_specs=[row] * 6 + [row, pl.BlockSpec((512, 512), lambda i: (0, 0))],
        out_specs=[row] * 6,
        out_shape=[S((T, 512), f32)] * 6,
        compiler_params=_cp("arbitrary"),
    )(*ols, dmix, e)


def swa_pre_bwd(o, lse, dmix, sinks, e, *, tm, name):
    T = o.shape[0]
    ni = T // tm

    def body(o_ref, l_ref, d_ref, s_ref, e_ref, do_ref, dd_ref, ds_ref):
        i = pl.program_id(0)
        dout = d_ref[...]
        do_ref[...] = dout.astype(MXU)
        prod = dout * o_ref[...]
        dd = jnp.concatenate([_seg_sum(prod[:, 0:512], e_ref), _seg_sum(prod[:, 512:1024], e_ref)], axis=1)
        dd_ref[...] = dd
        part = -jnp.sum(jnp.exp(s_ref[...] - l_ref[...]) * dd, axis=0, keepdims=True)

        @pl.when(i == 0)
        def _():
            ds_ref[...] = part

        @pl.when(i > 0)
        def _():
            ds_ref[...] += part

    row = pl.BlockSpec((tm, 1024), lambda i: (i, 0))
    vec = pl.BlockSpec((1, 1024), lambda i: (0, 0))
    return pl.pallas_call(
        body, name=name, grid=(ni,),
        in_specs=[row, row, pl.BlockSpec((tm, 1024), lambda i: (i, 1)), vec, pl.BlockSpec((512, 512), lambda i: (0, 0))],
        out_specs=[row, row, vec],
        out_shape=[S((T, 1024), MXU), S((T, 1024), f32), S((1, 1024), f32)],
        compiler_params=_cp("arbitrary"),
    )(o, lse, dmix, sinks, e)


SHIFT_PAD = 24


def _shifted_copies(buf_ref, sh_ref, length):
    for r in range(1, 8):
        sh_ref[r - 1, 0:length, :] = buf_ref[pl.ds(r, length), :]


def _window(buf_ref, sh_ref, start, rows):
    q, r = divmod(start, 8)
    if r == 0:
        return buf_ref[pl.ds(8 * q, rows), :]
    return sh_ref[r - 1, pl.ds(8 * q, rows), :]


TAP_ROWS = 64


def _tap_sum(buf_ref, sh_ref, w_ref, starts, rows):
    outs = []
    for c0 in range(0, rows, TAP_ROWS):
        n = min(TAP_ROWS, rows - c0)
        acc = _window(buf_ref, sh_ref, starts[0] + c0, n) * w_ref[pl.ds(0, 1), :]
        for j in range(1, CONV_K):
            acc += _window(buf_ref, sh_ref, starts[j] + c0, n) * w_ref[pl.ds(j, 1), :]
        outs.append(acc)
    return jnp.concatenate(outs, axis=0)


def _conv_taps(buf_ref, sh_ref, w_ref, start, rows):
    return _tap_sum(buf_ref, sh_ref, w_ref, [start + j for j in range(CONV_K)], rows)


def conv_fwd(proj, mix, w, b, ln_g, ln_b, *, tb, name):
    T = proj.shape[0]
    hb = tb // HALO

    def body(u_ref, g_ref, up_ref, gp_ref, w_ref, b_ref, lg_ref, lb_ref, mix_ref, o_ref, hbuf, hsh):
        i = pl.program_id(0)
        hprev = up_ref[...] * _sigmoid(gp_ref[...])
        hbuf[0:HALO, :] = hprev * jnp.where(i > 0, 1.0, 0.0)
        hbuf[HALO:HALO + tb, :] = u_ref[...] * _sigmoid(g_ref[...])
        _shifted_copies(hbuf, hsh, tb + SHIFT_PAD)
        y = _conv_taps(hbuf, hsh, w_ref, HALO - (CONV_K - 1), tb) + b_ref[...]
        mu = jnp.mean(y, axis=-1, keepdims=True)
        yc = y - mu
        var = jnp.mean(yc * yc, axis=-1, keepdims=True)
        z = yc * lax.rsqrt(var + EPS) * lg_ref[...] + lb_ref[...]
        o_ref[...] = (z * _sigmoid(z)).astype(MXU)

    vec = pl.BlockSpec((1, 512), lambda i: (0, 0))
    return pl.pallas_call(
        body, name=name, grid=(T // tb,),
        in_specs=[pl.BlockSpec((tb, 512), lambda i: (i, 3)), pl.BlockSpec((tb, 512), lambda i: (i, 4)),
                  pl.BlockSpec((HALO, 512), lambda i: (jnp.maximum(i * hb - 1, 0), 3)),
                  pl.BlockSpec((HALO, 512), lambda i: (jnp.maximum(i * hb - 1, 0), 4)),
                  pl.BlockSpec((HALO, 512), lambda i: (0, 0)), vec, vec, vec, ANY],
        out_specs=pl.BlockSpec((tb, 512), lambda i: (i, 1)),
        out_shape=S(mix.shape, mix.dtype),
        scratch_shapes=[pltpu.VMEM((tb + HALO, 512), f32), pltpu.VMEM((7, tb + SHIFT_PAD, 512), f32)],
        compiler_params=_cp("arbitrary"), input_output_aliases={8: 0},
    )(proj, proj, proj, proj, w, b, ln_g, ln_b, mix)


def conv_bwd(proj, dmix, w, b, ln_g, ln_b, *, tb, name, comm=None):
    T = proj.shape[0]
    hb = tb // HALO
    ni = T // tb
    last_h = T // HALO - 1
    ext = tb + HALO

    def body(u_ref, g_ref, up_ref, gp_ref, un_ref, gn_ref, d_ref, dn_ref, w_ref, b_ref, lg_ref, lb_ref,
             du_ref, dg_ref, dw_ref, db_ref, dlg_ref, dlb_ref, hbuf, dybuf, hsh, dsh):
        i = pl.program_id(0)
        hbuf[0:HALO, :] = up_ref[...] * _sigmoid(gp_ref[...]) * jnp.where(i > 0, 1.0, 0.0)
        u = u_ref[...]
        sg = _sigmoid(g_ref[...])
        hbuf[HALO:HALO + tb, :] = u * sg
        hbuf[HALO + tb:HALO + ext, :] = un_ref[...] * _sigmoid(gn_ref[...])
        _shifted_copies(hbuf, hsh, ext + SHIFT_PAD)
        y = _conv_taps(hbuf, hsh, w_ref, HALO - (CONV_K - 1), ext) + b_ref[...]
        mu = jnp.mean(y, axis=-1, keepdims=True)
        yc = y - mu
        rstd = lax.rsqrt(jnp.mean(yc * yc, axis=-1, keepdims=True) + EPS)
        yn = yc * rstd
        z = yn * lg_ref[...] + lb_ref[...]
        sz = _sigmoid(z)
        row = lax.broadcasted_iota(jnp.int32, (ext, 1), 0)
        own = row < tb
        keep = row < jnp.where(i < ni - 1, ext, tb)
        dout = jnp.concatenate([d_ref[...], dn_ref[...]], axis=0)
        dz = jnp.where(keep, dout * (sz * (1.0 + z * (1.0 - sz))), 0.0)
        dyn = dz * lg_ref[...]
        dy = rstd * (dyn - jnp.mean(dyn, axis=-1, keepdims=True) - yn * jnp.mean(dyn * yn, axis=-1, keepdims=True))
        dybuf[...] = dy
        _shifted_copies(dybuf, dsh, tb + SHIFT_PAD)
        dz_own = jnp.where(own, dz, 0.0)
        dlg = jnp.sum(dz_own * yn, axis=0, keepdims=True)
        dlb = jnp.sum(dz_own, axis=0, keepdims=True)
        dy_own = dybuf[0:tb, :]
        dbias = jnp.sum(dy_own, axis=0, keepdims=True)
        dh = _tap_sum(dybuf, dsh, w_ref, [CONV_K - 1 - j for j in range(CONV_K)], tb)
        du_ref[...] = (dh * sg).astype(MXU)
        dg_ref[...] = (dh * u * sg * (1.0 - sg)).astype(MXU)
        taps = [jnp.sum(dy_own * _window(hbuf, hsh, HALO - (CONV_K - 1) + j, tb), axis=0, keepdims=True)
                for j in range(CONV_K)]
        taps.append(jnp.zeros((1, 512), f32))
        dwt = jnp.concatenate(taps, axis=0)

        @pl.when(i == 0)
        def _():
            dw_ref[...] = dwt
            db_ref[...] = dbias
            dlg_ref[...] = dlg
            dlb_ref[...] = dlb

        @pl.when(i > 0)
        def _():
            dw_ref[...] += dwt
            db_ref[...] += dbias
            dlg_ref[...] += dlg
            dlb_ref[...] += dlb

    vec = pl.BlockSpec((1, 512), lambda i: (0, 0))
    wspec = pl.BlockSpec((HALO, 512), lambda i: (0, 0))

    def halo_prev(col):
        return pl.BlockSpec((HALO, 512), lambda i: (jnp.maximum(i * hb - 1, 0), col))

    def halo_next(col):
        return pl.BlockSpec((HALO, 512), lambda i: (jnp.minimum((i + 1) * hb, last_h), col))

    row = pl.BlockSpec((tb, 512), lambda i: (i, 0))
    return _pallas(
        body, comm=comm, name=name, grid=(ni,),
        in_specs=[pl.BlockSpec((tb, 512), lambda i: (i, 3)), pl.BlockSpec((tb, 512), lambda i: (i, 4)),
                  halo_prev(3), halo_prev(4), halo_next(3), halo_next(4),
                  pl.BlockSpec((tb, 512), lambda i: (i, 1)), halo_next(1), wspec, vec, vec, vec],
        out_specs=[row, row, wspec, vec, vec, vec],
        out_shape=[S((T, 512), MXU), S((T, 512), MXU), S((HALO, 512), f32)] + [S((1, 512), f32)] * 3,
        scratch_shapes=[pltpu.VMEM((tb + 2 * HALO, 512), f32), pltpu.VMEM((ext, 512), f32),
                        pltpu.VMEM((7, ext + SHIFT_PAD, 512), f32), pltpu.VMEM((7, tb + SHIFT_PAD, 512), f32)],
        compiler_params=_cp("arbitrary"),
    )(proj, proj, proj, proj, proj, proj, dmix, dmix, w, b, ln_g, ln_b)


def _qk_norm_bwd(v, gain, dout, e_ref):
    r = lax.rsqrt(_seg_sum(v * v, e_ref) * (1.0 / HD) + EPS)
    y = v * r
    dgain = jnp.sum(dout * y, axis=0, keepdims=True)
    dy = dout * gain
    dv = r * (dy - y * (_seg_sum(dy * y, e_ref) * (1.0 / HD)))
    return dv, dgain


def prep_bwd(proj, e, gains, da, dc, dconv, *, tm, name):
    T = proj.shape[0]

    def body(*refs):
        p_ref, e_ref, gaq, gak, gcq, gck = refs[0:6]
        a_refs = refs[6:15]
        dcq, dckk, dcvv, du, dgt = refs[15:20]
        dp, gaq_o, gak_o, gcq_o, gck_o = refs[20:]
        i = pl.program_id(0)
        dq = a_refs[0][...] + a_refs[3][...] + a_refs[6][...]
        dk = a_refs[1][...] + a_refs[4][...] + a_refs[7][...]
        dv = a_refs[2][...] + a_refs[5][...] + a_refs[8][...]
        d, g_aq = _qk_norm_bwd(p_ref[:, 0:512], gaq[...], dq, e_ref)
        dp[:, 0:512] = d.astype(MXU)
        d, g_ak = _qk_norm_bwd(p_ref[:, 512:1024], gak[...], dk, e_ref)
        dp[:, 512:1024] = d.astype(MXU)
        dp[:, 1024:1536] = dv.astype(MXU)
        dp[:, 1536:2048] = du[...]
        dp[:, 2048:2560] = dgt[...]
        d, g_cq0 = _qk_norm_bwd(p_ref[:, 2560:3072], gcq[...], dcq[:, 0:512], e_ref)
        dp[:, 2560:3072] = d.astype(MXU)
        d, g_cq1 = _qk_norm_bwd(p_ref[:, 3072:3584], gcq[...], dcq[:, 512:1024], e_ref)
        dp[:, 3072:3584] = d.astype(MXU)
        lo = lax.broadcasted_iota(jnp.int32, (tm, 128), 1) < HD

        def fold(ref):
            g0, g1 = ref[:, 0:128], ref[:, 128:256]
            s0 = g0 + pltpu.roll(g0, HD, 1)
            s1 = g1 + pltpu.roll(g1, HD, 1)
            return jnp.where(lo, s0, s1)

        dkn = fold(dckk)
        kraw = p_ref[:, 3584:3712]
        r = lax.rsqrt(_seg_sum128(kraw * kraw, e_ref) * (1.0 / HD) + EPS)
        y = kraw * r
        g_ck = jnp.sum(dkn * y, axis=0, keepdims=True)
        dy = dkn * gck[...]
        dp[:, 3584:3712] = (r * (dy - y * (_seg_sum128(dy * y, e_ref) * (1.0 / HD)))).astype(MXU)
        dp[:, 3712:3840] = fold(dcvv).astype(MXU)
        g_cq = jnp.concatenate([g_cq0, g_cq1], axis=1)

        @pl.when(i == 0)
        def _():
            gaq_o[...] = g_aq
            gak_o[...] = g_ak
            gcq_o[...] = g_cq
            gck_o[...] = g_ck

        @pl.when(i > 0)
        def _():
            gaq_o[...] += g_aq
            gak_o[...] += g_ak
            gcq_o[...] += g_cq
            gck_o[...] += g_ck

    def vec(n):
        return pl.BlockSpec((1, n), lambda i: (0, 0))

    def rows(n):
        return pl.BlockSpec((tm, n), lambda i: (i, 0))

    return pl.pallas_call(
        body, name=name, grid=(T // tm,),
        in_specs=[rows(IN_W), pl.BlockSpec((512, 512), lambda i: (0, 0)), vec(512), vec(512), vec(512), vec(128)]
        + [rows(512)] * 9 + [rows(1024), rows(256), rows(256), rows(512), rows(512)],
        out_specs=[rows(IN_W), vec(512), vec(512), vec(1024), vec(128)],
        out_shape=[S((T, IN_W), MXU), S((1, 512), f32), S((1, 512), f32), S((1, 1024), f32), S((1, 128), f32)],
        compiler_params=_cp("arbitrary"),
    )(proj, e, *gains, *da, *dc, *dconv)


def adamw(w, m, v, pieces, *, tr, name, comm=None):
    n, R, C = w.shape
    c1 = 1.0 - ADAM_B1 ** ADAM_STEP
    c2 = 1.0 - ADAM_B2 ** ADAM_STEP
    npc = len(pieces)

    def body(*refs):
        w_ref, m_ref, v_ref = refs[0:3]
        p_refs = refs[3:3 + npc]
        g_ref, d_ref, mo_ref, vo_ref = refs[3 + npc:]
        g = p_refs[0][...].astype(f32)
        for p in p_refs[1:]:
            g = g + p[...].astype(f32)
        mn = ADAM_B1 * m_ref[...] + (1.0 - ADAM_B1) * g
        vn = ADAM_B2 * v_ref[...] + (1.0 - ADAM_B2) * (g * g)
        g_ref[...] = g
        mo_ref[...] = mn
        vo_ref[...] = vn
        d_ref[...] = -ADAM_LR * ((mn / c1) / (jnp.sqrt(vn / c2) + ADAM_EPS) + ADAM_WD * w_ref[...])

    blk = pl.BlockSpec((None, tr, C), lambda l, i: (l, i, 0))
    return _pallas(
        body, comm=comm, name=name, grid=(n, R // tr), in_specs=[blk] * (3 + npc), out_specs=[blk] * 4,
        out_shape=[S(w.shape, f32)] * 4, compiler_params=_cp("arbitrary", "arbitrary"),
    )(w, m, v, *pieces)


def add_halves(pieces, other, *, tr, name):
    _, _, r, cc = pieces.shape

    def body(c_ref, a_ref, b_ref, o_ref):
        o_ref[...] = (a_ref[...] + b_ref[...]).astype(jnp.bfloat16)

    blk = pl.BlockSpec((None, tr, cc), lambda s, i, c_ref: (s, i, 0))
    grid_spec = pltpu.PrefetchScalarGridSpec(
        num_scalar_prefetch=1, grid=(4, r // tr),
        in_specs=[pl.BlockSpec((None, None, tr, cc), lambda s, i, c_ref: (s, c_ref[0], i, 0)), blk], out_specs=blk)
    core = lax.axis_index("c").astype(jnp.int32).reshape(1)
    return pl.pallas_call(body, name=name, grid_spec=grid_spec, out_shape=S((4, r, cc), jnp.bfloat16),
                          compiler_params=_cp("arbitrary", "arbitrary"))(core, pieces, other)


def sum8(parts, *, name):
    _, R, C = parts.shape

    def body(p_ref, o_ref):
        acc = p_ref[0]
        for d in range(1, 8):
            acc = acc + p_ref[d]
        o_ref[...] = acc

    return pl.pallas_call(body, name=name, out_shape=S((R, C), f32))(parts)


def _pos():
    return lax.axis_index("x"), lax.axis_index("y"), lax.axis_index("c")


def _other_chips(x, y):
    return [(1 - x, y), (x, 1 - y), (1 - x, 1 - y)]


class GatherComm:
    def __init__(self, shards, in_place=None):
        self.ins = list(shards)
        self.nt = nt = len(shards)
        self.in_place = list(in_place) if in_place is not None else [False] * nt
        self.out_shapes = [S((2 * s.shape[1], 4 * s.shape[2]), s.dtype) if ip else S((4,) + s.shape, s.dtype)
                           for s, ip in zip(shards, self.in_place)]
        self.sem_shapes = [pltpu.SemaphoreType.DMA((nt, 6)), pltpu.SemaphoreType.DMA((nt, 6)),
                           pltpu.SemaphoreType.DMA((nt, 2))]
        self.results = None

    def _place(self, couts, t, chip, half):
        cid = 2 * chip[0] + chip[1]
        if not self.in_place[t]:
            return couts[t].at[cid, half]
        _, r, c = self.ins[t].shape
        row0 = half * r if isinstance(half, int) else pl.multiple_of(half * r, 16)
        return couts[t].at[pl.ds(row0, r), pl.ds(pl.multiple_of(cid * c, 128), c)]

    def _copy(self, couts, sems, t, k, chip, half, to, src=None):
        dst = self._place(couts, t, chip, half)
        return pltpu.make_async_remote_copy(
            src_ref=dst if src is None else src, dst_ref=dst,
            send_sem=sems[0].at[t, k], recv_sem=sems[1].at[t, k], device_id=to, device_id_type=MESH)

    def _local(self, cins, couts, sems, t):
        x, y, _ = _pos()
        return [pltpu.make_async_copy(cins[t].at[half], self._place(couts, t, (x, y), half), sems[2].at[t, half])
                for half in range(2)]

    def start(self, cins, couts, sems):
        x, y, c = _pos()
        for t in range(self.nt):
            for cp in self._local(cins, couts, sems, t):
                cp.start()
            for j, chip in enumerate(_other_chips(x, y)):
                self._copy(couts, sems, t, j, (x, y), c, (*chip, c), src=cins[t].at[c]).start()

    def mid(self, cins, couts, sems):
        x, y, c = _pos()
        for t in range(self.nt):
            for j, chip in enumerate(_other_chips(x, y)):
                self._copy(couts, sems, t, j, chip, c, (x, y, c)).wait_recv()
                self._copy(couts, sems, t, 3 + j, chip, c, (x, y, 1 - c)).start()

    def finish(self, cins, couts, sems):
        x, y, c = _pos()
        for t in range(self.nt):
            for j, chip in enumerate(_other_chips(x, y)):
                self._copy(couts, sems, t, 3 + j, chip, 1 - c, (x, y, c)).wait_recv()
        for t in range(self.nt):
            for j, chip in enumerate(_other_chips(x, y)):
                self._copy(couts, sems, t, j, (x, y), c, (*chip, c), src=cins[t].at[c]).wait_send()
                self._copy(couts, sems, t, 3 + j, chip, c, (x, y, 1 - c)).wait_send()
            for cp in self._local(cins, couts, sems, t):
                cp.wait()


class SwapComm:
    def __init__(self, pieces):
        self.ins = list(pieces)
        self.nt = nt = len(pieces)
        self.out_shapes = [S((4,) + p.shape[2:], p.dtype) for p in pieces]
        self.sem_shapes = [pltpu.SemaphoreType.DMA((nt, 4)), pltpu.SemaphoreType.DMA((nt, 4))]
        self.results = None

    def _copies(self, cins, couts, sems):
        x, y, c = _pos()
        return [pltpu.make_async_remote_copy(src_ref=cins[t].at[s, 1 - c], dst_ref=couts[t].at[s],
                                             send_sem=sems[0].at[t, s], recv_sem=sems[1].at[t, s],
                                             device_id=(x, y, 1 - c), device_id_type=MESH)
                for t in range(self.nt) for s in range(4)]

    def start(self, cins, couts, sems):
        for cp in self._copies(cins, couts, sems):
            cp.start()

    def mid(self, cins, couts, sems):
        pass

    def finish(self, cins, couts, sems):
        for cp in self._copies(cins, couts, sems):
            cp.wait()


class ExchangeComm:
    def __init__(self, arrs):
        self.ins = list(arrs)
        self.nt = nt = len(arrs)
        self.out_shapes = [S((2,) + a.shape, a.dtype) for a in arrs]
        self.sem_shapes = [pltpu.SemaphoreType.DMA((nt, 7)), pltpu.SemaphoreType.DMA((nt, 7)),
                           pltpu.SemaphoreType.DMA((nt,))]
        self.results = None

    def _copy(self, couts, sems, t, k, half, src_chip, to, src=None):
        dst = couts[t].at[half, src_chip]
        return pltpu.make_async_remote_copy(
            src_ref=dst if src is None else src, dst_ref=dst,
            send_sem=sems[0].at[t, k], recv_sem=sems[1].at[t, k], device_id=to, device_id_type=MESH)

    def _local(self, cins, couts, sems, t):
        x, y, c = _pos()
        return pltpu.make_async_copy(cins[t].at[2 * x + y], couts[t].at[c, 2 * x + y], sems[2].at[t])

    def _firsts(self, cins, couts, sems, t):
        x, y, c = _pos()
        me = 2 * x + y
        cps = [self._copy(couts, sems, t, j, c, me, (*chip, c), src=cins[t].at[2 * chip[0] + chip[1]])
               for j, chip in enumerate(_other_chips(x, y))]
        return cps + [self._copy(couts, sems, t, 6, c, me, (x, y, 1 - c), src=cins[t].at[me])]

    def start(self, cins, couts, sems):
        for t in range(self.nt):
            self._local(cins, couts, sems, t).start()
            for cp in self._firsts(cins, couts, sems, t):
                cp.start()

    def mid(self, cins, couts, sems):
        x, y, c = _pos()
        for t in range(self.nt):
            for j, chip in enumerate(_other_chips(x, y)):
                cid = 2 * chip[0] + chip[1]
                self._copy(couts, sems, t, j, c, cid, (x, y, c)).wait_recv()
                self._copy(couts, sems, t, 3 + j, c, cid, (x, y, 1 - c)).start()

    def finish(self, cins, couts, sems):
        x, y, c = _pos()
        for t in range(self.nt):
            for j, chip in enumerate(_other_chips(x, y)):
                self._copy(couts, sems, t, 3 + j, 1 - c, 2 * chip[0] + chip[1], (x, y, c)).wait_recv()
            self._copy(couts, sems, t, 6, 1 - c, 2 * x + y, (x, y, c)).wait_recv()
        for t in range(self.nt):
            for cp in self._firsts(cins, couts, sems, t):
                cp.wait_send()
            for j, chip in enumerate(_other_chips(x, y)):
                self._copy(couts, sems, t, 3 + j, c, 2 * chip[0] + chip[1], (x, y, 1 - c)).wait_send()
            self._local(cins, couts, sems, t).wait()


def gather_small(vec, *, name):
    R, C = vec.shape

    def body(v_ref, out_ref, send_sems, recv_sems):
        x, y, c = _pos()
        me = 4 * x + 2 * y + c
        out_ref[me] = v_ref[...]
        cps = []
        def peer(k):
            fx, fy, fc = (k >> 2) & 1, (k >> 1) & 1, k & 1
            return (1 - x if fx else x), (1 - y if fy else y), (1 - c if fc else c)

        for k in range(1, 8):
            cp = pltpu.make_async_remote_copy(src_ref=v_ref, dst_ref=out_ref.at[me], send_sem=send_sems.at[k - 1],
                                              recv_sem=recv_sems.at[k - 1], device_id=peer(k), device_id_type=MESH)
            cp.start()
            cps.append(cp)
        for k in range(1, 8):
            px, py, pc = peer(k)
            pltpu.make_async_remote_copy(src_ref=v_ref, dst_ref=out_ref.at[4 * px + 2 * py + pc],
                                         send_sem=send_sems.at[k - 1], recv_sem=recv_sems.at[k - 1],
                                         device_id=(px, py, pc), device_id_type=MESH).wait_recv()
        for cp in cps:
            cp.wait_send()

    return pl.pallas_call(
        body, name=name,
        in_specs=[pl.BlockSpec(memory_space=pltpu.VMEM)], out_specs=pl.BlockSpec(memory_space=pltpu.VMEM),
        out_shape=S((8, R, C), vec.dtype),
        scratch_shapes=[pltpu.SemaphoreType.DMA((7,)), pltpu.SemaphoreType.DMA((7,))],
    )(vec)


def _tile(n, prefs):
    for p in prefs:
        if n % p == 0:
            return p
    return n


def _lanes(g, reps):
    return jnp.tile(g.reshape(1, -1), (1, reps))


class _NoRide:
    def rider(self, name):
        return None

    def landed(self, comm):
        pass

    def grad(self, name, val):
        pass


def _layer_fwd(x, p, e, ride=_NoRide()):
    T, D = x.shape
    tm = _tile(T, (512, 256, 128))
    tall = _tile(T, (1024, 512, 256, 128))

    def carried(fn, *args, name, **kw):
        comm = ride.rider(name)
        out = fn(*args, name=name, comm=comm, **kw)
        ride.landed(comm)
        return out

    h, proj = carried(rms_proj, x, p["norm1_g"], p["w_in"], tm=tall, tn=_tile(IN_W, (1920,)), name="rms_proj")
    gains = (_lanes(p["a_q_g"], 8), _lanes(p["a_k_g"], 8), _lanes(p["c_q_g"], 8), _lanes(p["c_k_g"], 2))
    aq, ak, av, cq, ckk, cvv = prep_fwd(proj, e, gains, tm=_tile(T, (256, 128)), name="prep_fwd")
    ols = []
    for d in DILATIONS:
        ols += carried(band_attn_fwd, aq, ak, av, None, dil=d, max_dist=A_DIST, group=1, name=f"dil_attn_fwd_{d}")
    mix = dil_combine_fwd(ols, tm=tm, name="dil_combine_fwd")
    mix = conv_fwd(proj, mix, p["conv_w"], p["conv_b"], p["conv_ln_g"], p["conv_ln_b"], tb=tm, name="conv_fwd")
    sinks = jnp.repeat(p["c_sinks"].reshape(-1), HD).reshape(1, C_W)
    o_c, l_c, mix = carried(band_attn_fwd, cq, ckk, cvv, sinks, dil=1, max_dist=C_DIST, group=4, mix=mix,
                            name="swa_attn_fwd")
    x1 = carried(matmul_res, mix, p["w_out"], x, tm=tall, tn=_tile(D, (1024, 512, 256)), name="out_proj")
    F = p["w_gate"].shape[1]
    h2, gate, up, act = carried(rms_swiglu, x1, p["norm2_g"], p["w_gate"], p["w_up"], tm=tall,
                                tn=_tile(F, (512, 256, 128)), name="rms_swiglu")
    x2 = carried(matmul_res, act, p["w_down"], x1, tm=tall, tn=_tile(D, (512, 256)), name="ffn_down")
    saved = dict(x=x, h=h, proj=proj, gains=gains, aq=aq, ak=ak, av=av, cq=cq, ckk=ckk, cvv=cvv, ols=ols, o_c=o_c,
                 l_c=l_c, sinks=sinks, mix=mix, x1=x1, h2=h2, gate=gate, up=up, act=act)
    return x2, saved


def _layer_bwd(dx2, dx2b, p, s, e, ride=_NoRide()):
    T, D = dx2.shape
    F = p["w_gate"].shape[1]
    tm = _tile(T, (512, 256, 128))
    tmm = _tile(T, (1024, 512, 256, 128))
    tkT = _tile(T, (2048, 1024, 512))
    tF = _tile(F, (512, 256, 128))
    tD = _tile(D, (1024, 512, 256))
    g = {}

    def carried(fn, *args, name, **kw):
        comm = ride.rider(name)
        out = fn(*args, name=name, comm=comm, **kw)
        ride.landed(comm)
        return out

    def big(n, val):
        g[n] = val
        ride.grad(n, val)

    d_gate, d_up = carried(nt_swiglu_bwd, dx2b, p["w_down"], s["gate"], s["up"], tm=tmm, tn=tF, name="ffn_down_bwd")
    big("w_down", tn_matmul(s["act"], dx2b, tm=_tile(F, (1408, 512, 256, 128)), tn=tD, tk=tkT, name="grad_w_down"))
    tM = _tile(D, (2048, 1024, 512, 256))
    big("w_gate", carried(tn_matmul, s["h2"], d_gate, tm=tD, tn=tF, tk=tkT, by_chip=True, name="grad_w_gate"))
    big("w_up", carried(tn_matmul, s["h2"], d_up, tm=tD, tn=tF, tk=tkT, by_chip=True, name="grad_w_up"))
    dx1, dx1b, g["norm2_g"] = carried(nt_rms_bwd, [(d_gate, p["w_gate"]), (d_up, p["w_up"])], s["x1"], p["norm2_g"],
                                      dx2, tm=tm, tn=_tile(D, (512, 256)), name="ffn_in_bwd")
    dmix = nt_plain(dx1b, p["w_out"], tm=tmm, tn=tD, name="out_proj_bwd")
    big("w_out", tn_matmul(s["mix"], dx1b, tm=2048, tn=tD, tk=tkT, name="grad_w_out"))
    dos = dil_combine_bwd(s["ols"], dmix, e, tm=tm, name="dil_combine_bwd")
    da = []
    for n, d in enumerate(DILATIONS):
        da += carried(band_attn_bwd, s["aq"], s["ak"], s["av"], s["ols"][2 * n + 1], dos[n], dos[3 + n], dil=d,
                      max_dist=A_DIST, group=1, name=f"dil_attn_bwd_{d}")
    du, dgt, gw, gb, glg, glb = carried(conv_bwd, s["proj"], dmix, p["conv_w"], p["conv_b"], p["conv_ln_g"],
                                        p["conv_ln_b"], tb=tm, name="conv_bwd")
    g["conv_w"], g["conv_b"], g["conv_ln_g"], g["conv_ln_b"] = gw[:CONV_K], gb, glg, glb
    do_c, dd_c, dsink = swa_pre_bwd(s["o_c"], s["l_c"], dmix, s["sinks"], e, tm=_tile(T, (256, 128)), name="swa_pre_bwd")
    g["c_sinks"] = dsink.reshape(-1, HD)[:, 0]
    dcq, dckk, dcvv = carried(band_attn_bwd, s["cq"], s["ckk"], s["cvv"], s["l_c"], do_c, dd_c, dil=1, max_dist=C_DIST,
                              group=4, name="swa_attn_bwd")
    dproj, gaq, gak, gcq, gck = prep_bwd(s["proj"], e, s["gains"], da, (dcq, dckk, dcvv), (du, dgt),
                                         tm=_tile(T, (256, 128)), name="prep_bwd")
    g["a_q_g"] = gaq.reshape(-1, HD).sum(0)
    g["a_k_g"] = gak.reshape(-1, HD).sum(0)
    g["c_q_g"] = gcq.reshape(-1, HD).sum(0)
    g["c_k_g"] = gck.reshape(-1, HD).sum(0)
    big("w_in", tn_matmul(s["h"], dproj, tm=tM, tn=_tile(IN_W, (768,)), tk=tkT, name="grad_w_in"))
    dx, dxb, g["norm1_g"] = carried(nt_rms_bwd, [(dproj, p["w_in"])], s["x"], p["norm1_g"], dx1, tm=tmm,
                                    tn=_tile(D, (512, 256)), name="in_proj_bwd")
    return dx, dxb, g


BIG = ("w_in", "w_out", "w_gate", "w_up", "w_down")
COL_SHARDED = ("w_in", "w_gate", "w_up")
SMALL = ("norm1_g", "a_q_g", "a_k_g", "conv_w", "conv_b", "conv_ln_g", "conv_ln_b", "c_q_g", "c_k_g", "c_sinks", "norm2_g")


def _to_pieces(g, name):
    if g.ndim == 3:
        return g.reshape(4, 2, g.shape[1] // 2, g.shape[2])
    R, C = g.shape
    if name in COL_SHARDED:
        return g.reshape(2, R // 2, 4, C // 4).transpose(2, 0, 1, 3)
    return g.reshape(4, 2, R // 8, C)


def _from_gathered(w, name):
    _, _, r, c = w.shape
    if name in COL_SHARDED:
        return w.transpose(1, 2, 0, 3).reshape(2 * r, 4 * c)
    return w.reshape(8 * r, c)


def _shard(W, l, n):
    w = W[n][l]
    return w.astype(MXU).reshape(2, w.shape[0] // 2, w.shape[1])


class _LayerParams(dict):
    def __init__(self, layer, full, small):
        super().__init__(small)
        self.layer, self.full = layer, full

    def __missing__(self, n):
        return self.full[(self.layer, n)]


FWD_RIDES = {
    (0, "rms_proj"): ((0, "w_out"), (0, "w_gate")),
    (0, "swa_attn_fwd"): ((0, "w_up"),),
    (0, "rms_swiglu"): ((0, "w_down"), (1, "w_in"), (1, "w_out")),
    (0, "ffn_down"): ((1, "w_gate"),),
    (1, "rms_proj"): ((1, "w_up"),),
    (1, "rms_swiglu"): ((1, "w_down"),),
}
BWD_RIDES = {
    "grad_w_gate": (("w_down",), ()),
    "grad_w_up": (("w_gate",), ("w_down",)),
    "ffn_in_bwd": (("w_up",), ("w_gate",)),
    "dil_attn_bwd_1": (("w_out",), ()),
    "dil_attn_bwd_16": ((), ("w_up",)),
    "conv_bwd": ((), ("w_out",)),
    "in_proj_bwd": (("w_in",), ()),
}
IN_PLACE = ("w_gate", "w_up")


class _FwdRide:
    def __init__(self, layer, W, full):
        self.layer, self.W, self.full = layer, W, full

    def rider(self, name):
        keys = FWD_RIDES.get((self.layer, name))
        if not keys:
            return None
        comm = GatherComm([_shard(self.W, l, n) for l, n in keys], [n in IN_PLACE for _, n in keys])
        comm.keys = keys
        return comm

    def landed(self, comm):
        if comm is not None:
            for (l, n), g in zip(comm.keys, comm.results):
                self.full[(l, n)] = g if n in IN_PLACE else _from_gathered(g, n)


class _GradFlow:
    def __init__(self):
        self.pieces, self.sums, self.landed, self.pending = {}, {}, {}, []

    def swap(self, keys):
        comm = SwapComm([self.pieces[k] for k in keys])
        comm.keys, comm.kind = list(keys), "swap"
        return comm

    def exchange(self, keys):
        if not keys:
            return None
        comm = ExchangeComm([self.sums[k] for k in keys])
        comm.keys, comm.kind = list(keys), "exchange"
        return comm

    def take_pending(self):
        keys, self.pending = self.pending, []
        return keys

    def land(self, comm):
        if comm is None:
            return
        if isinstance(comm, MultiComm):
            for sub in comm.comms:
                self.land(sub)
            return
        for k, res in zip(comm.keys, comm.results):
            if comm.kind == "swap":
                r = res.shape[1]
                self.sums[k] = add_halves(self.pieces[k], res, tr=_tile(r, (256, 176, 128, 64, 32, 16)),
                                          name="grad_chip_sum")
                if k[1] == "w_in":
                    self.pending.append(k)
            else:
                self.landed[k] = res


class _BwdRide:
    def __init__(self, layer, flow):
        self.layer, self.flow = layer, flow
        self.final = layer == 0

    def grad(self, name, val):
        self.flow.pieces[(self.layer, name)] = _to_pieces(val, name)
        if self.final and name == "w_in":
            swap = self.flow.swap([(self.layer, name)])
            _run_comm(swap, name="swap_last")
            self.flow.land(swap)

    def rider(self, name):
        if name == "ffn_down_bwd" or (self.final and name == "in_proj_bwd"):
            return self.flow.exchange(self.flow.take_pending())
        swaps, exchanges = BWD_RIDES.get(name, ((), ()))
        comms = []
        if swaps:
            comms.append(self.flow.swap([(self.layer, n) for n in swaps]))
        if exchanges:
            comms.append(self.flow.exchange([(self.layer, n) for n in exchanges]))
        return MultiComm(comms) if comms else None

    def landed(self, comm):
        self.flow.land(comm)


def _pack(items, rows):
    flat = jnp.concatenate([a.reshape(-1).astype(f32) for a in items])
    return jnp.pad(flat, (0, rows * 128 - flat.shape[0])).reshape(rows, 128)


def _unpack(packed, shapes):
    flat = packed.reshape(-1)
    out, off = [], 0
    for shp in shapes:
        n = 1
        for d in shp:
            n *= d
        out.append(flat[off:off + n].reshape(shp))
        off += n
    return out


def kernel(x, norm1_g, w_in, a_q_g, a_k_g, conv_w, conv_b, conv_ln_g, conv_ln_b, c_q_g, c_k_g, c_sinks, w_out, norm2_g, w_gate, w_up, w_down, loss_target, m_norm1_g, m_w_in, m_a_q_g, m_a_k_g, m_conv_w, m_conv_b, m_conv_ln_g, m_conv_ln_b, m_c_q_g, m_c_k_g, m_c_sinks, m_w_out, m_norm2_g, m_w_gate, m_w_up, m_w_down, v_norm1_g, v_w_in, v_a_q_g, v_a_k_g, v_conv_w, v_conv_b, v_conv_ln_g, v_conv_ln_b, v_c_q_g, v_c_k_g, v_c_sinks, v_w_out, v_norm2_g, v_w_gate, v_w_up, v_w_down):
    W = dict(norm1_g=norm1_g, w_in=w_in, a_q_g=a_q_g, a_k_g=a_k_g, conv_w=conv_w, conv_b=conv_b, conv_ln_g=conv_ln_g,
             conv_ln_b=conv_ln_b, c_q_g=c_q_g, c_k_g=c_k_g, c_sinks=c_sinks, w_out=w_out, norm2_g=norm2_g, w_gate=w_gate,
             w_up=w_up, w_down=w_down)
    M = dict(norm1_g=m_norm1_g, w_in=m_w_in, a_q_g=m_a_q_g, a_k_g=m_a_k_g, conv_w=m_conv_w, conv_b=m_conv_b,
             conv_ln_g=m_conv_ln_g, conv_ln_b=m_conv_ln_b, c_q_g=m_c_q_g, c_k_g=m_c_k_g, c_sinks=m_c_sinks, w_out=m_w_out,
             norm2_g=m_norm2_g, w_gate=m_w_gate, w_up=m_w_up, w_down=m_w_down)
    V = dict(norm1_g=v_norm1_g, w_in=v_w_in, a_q_g=v_a_q_g, a_k_g=v_a_k_g, conv_w=v_conv_w, conv_b=v_conv_b,
             conv_ln_g=v_conv_ln_g, conv_ln_b=v_conv_ln_b, c_q_g=v_c_q_g, c_k_g=v_c_k_g, c_sinks=v_c_sinks, w_out=v_w_out,
             norm2_g=v_norm2_g, w_gate=v_w_gate, w_up=v_w_up, w_down=v_w_down)
    depth = norm1_g.shape[0]
    T, D = x.shape[1], x.shape[2]
    xs = x.reshape(T, D)
    chip = 2 * lax.axis_index("x") + lax.axis_index("y")
    e = _head_eye()

    full = {}
    first = GatherComm([_shard(W, 0, "w_in"), conv_w])
    _run_comm(first, name="gather_first")
    full[(0, "w_in")] = _from_gathered(first.results[0], "w_in")
    conv_full = first.results[1].transpose(1, 2, 0, 3).reshape(depth, CONV_K, B_W)
    params = []
    for l in range(depth):
        small = {n: W[n][l].reshape(1, -1) for n in SMALL if n != "conv_w"}
        small["conv_w"] = jnp.pad(conv_full[l], ((0, HALO - CONV_K), (0, 0)))
        params.append(_LayerParams(l, full, small))

    saved = []
    act = xs
    for l in range(depth):
        act, s = _layer_fwd(act, params[l], e, _FwdRide(l, W, full))
        saved.append(s)
    dy, dyb, loss_part = loss_head(act, loss_target.reshape(T, D), tm=_tile(T, (512, 256, 128)), name="loss_head")
    grads = [None] * depth
    flow = _GradFlow()
    for l in reversed(range(depth)):
        dy, dyb, grads[l] = _layer_bwd(dy, dyb, params[l], saved[l], e, _BwdRide(l, flow))
    grad_x = dy.reshape(x.shape)

    out = {}
    for n in ("w_down", "w_gate", "w_up", "w_out", "w_in"):
        last = flow.exchange(flow.take_pending())
        per_layer = [flow.landed[(l, n)] for l in range(depth)]
        r, cc = per_layer[0].shape[2], per_layer[0].shape[3]
        srcs = [jnp.stack([pl_[:, s].reshape(2 * r, cc) for pl_ in per_layer]) for s in range(4)]
        out[n] = adamw(W[n], M[n], V[n], srcs, tr=_tile(2 * r, (256, 176, 128, 64, 32, 16)), name="adamw_" + n,
                       comm=last)
        flow.land(last)

    small_shapes = []
    items = []
    for l in range(depth):
        for n in SMALL:
            a = grads[l][n]
            if n == "conv_w":
                a = a.reshape(CONV_K, 4, B_W // 4).transpose(1, 0, 2)
            items.append(a)
            small_shapes.append(a.shape)
    items.append(loss_part[0, 0:1])
    small_shapes.append((1,))
    total = sum(int(jnp.size(a)) for a in items)
    rows = -(-total // 1024) * 8
    summed = sum8(gather_small(_pack(items, rows), name="gather_small"), name="sum_small")
    parts = _unpack(summed, small_shapes)
    loss = parts[-1][0]
    small_g = {n: [] for n in SMALL}
    for l in range(depth):
        for i, n in enumerate(SMALL):
            a = parts[l * len(SMALL) + i]
            if n == "conv_w":
                a = lax.dynamic_index_in_dim(a, chip, axis=0, keepdims=False)
            small_g[n].append(a.reshape(W[n].shape[1:]))
    sw = [W[n] for n in SMALL]
    sm = [M[n] for n in SMALL]
    sv = [V[n] for n in SMALL]
    sg = [jnp.stack(small_g[n]) for n in SMALL]
    tot2 = sum(int(jnp.size(a)) for a in sw)
    rows2 = -(-tot2 // 1024) * 8
    res = adamw(_pack(sw, rows2)[None], _pack(sm, rows2)[None], _pack(sv, rows2)[None], [_pack(sg, rows2)[None]],
                tr=rows2, name="adamw_small")
    shapes2 = [a.shape for a in sw]
    small_out = [_unpack(r[0], shapes2) for r in res]
    for i, n in enumerate(SMALL):
        out[n] = [small_out[k][i] for k in range(4)]

    order = ("norm1_g", "w_in", "a_q_g", "a_k_g", "conv_w", "conv_b", "conv_ln_g", "conv_ln_b", "c_q_g", "c_k_g",
             "c_sinks", "w_out", "norm2_g", "w_gate", "w_up", "w_down")
    return (loss, grad_x, *[out[n][0] for n in order], *[out[n][1] for n in order], *[out[n][2] for n in order],
            *[out[n][3] for n in order])
```

```python
import functools

import jax
import jax.numpy as jnp
from jax import lax
from jax.experimental import pallas as pl
from jax.experimental.pallas import tpu as pltpu

f32 = jnp.float32
MXU = jnp.bfloat16
S = jax.ShapeDtypeStruct
MESH = pl.DeviceIdType.MESH

EPS = 1e-6
NEG = -1e30
HD = 64
BLK = 128
A_W, B_W, C_W = 512, 512, 1024
KV_W = 128
IN_W = 3 * A_W + 2 * B_W + C_W + 2 * KV_W
CONV_K = 31
HALO = 32
DILATIONS = (1, 4, 16)
A_DIST, C_DIST = 128, 127
SCALE = HD ** -0.5
VMEM_LIMIT = 56 * 1024 * 1024
VMEM_TALL = 62 * 1024 * 1024

ADAM_LR, ADAM_B1, ADAM_B2, ADAM_EPS, ADAM_WD, ADAM_STEP = 0.001, 0.9, 0.999, 1e-08, 0.01, 10


def _cp(*sem, vmem=VMEM_LIMIT):
    return pltpu.CompilerParams(dimension_semantics=sem, vmem_limit_bytes=vmem)


ANY = pl.BlockSpec(memory_space=pl.ANY)


def _pallas(body, *, comm=None, name, grid, in_specs, out_specs, out_shape, scratch_shapes=(), compiler_params,
            input_output_aliases=None):
    aliases = dict(input_output_aliases or {})
    if comm is None:
        return pl.pallas_call(body, name=name, grid=grid, in_specs=in_specs, out_specs=out_specs, out_shape=out_shape,
                              scratch_shapes=list(scratch_shapes), compiler_params=compiler_params,
                              input_output_aliases=aliases)
    single = not isinstance(out_shape, (list, tuple))
    o_shapes = [out_shape] if single else list(out_shape)
    o_specs = [out_specs] if single else list(out_specs)
    n_in, n_out, n_sc = len(in_specs), len(o_shapes), len(scratch_shapes)
    nci, nco = len(comm.ins), len(comm.out_shapes)
    total = 1
    for g in grid:
        total *= g

    def carried(*refs):
        ins, cins = refs[:n_in], refs[n_in:n_in + nci]
        o0 = n_in + nci
        outs, couts = refs[o0:o0 + n_out], refs[o0 + n_out:o0 + n_out + nco]
        s0 = o0 + n_out + nco
        scratch, sems = refs[s0:s0 + n_sc], refs[s0 + n_sc:]
        step = pl.program_id(0)
        for axis in range(1, len(grid)):
            step = step * grid[axis] + pl.program_id(axis)

        @pl.when(step == 0)
        def _():
            comm.start(cins, couts, sems)

        body(*ins, *outs, *scratch)

        @pl.when(step == (3 * total) // 4)
        def _():
            comm.mid(cins, couts, sems)

        @pl.when(step == total - 1)
        def _():
            comm.finish(cins, couts, sems)

    call = pl.pallas_call(carried, name=name, grid=grid, in_specs=list(in_specs) + [ANY] * nci,
                          out_specs=o_specs + [ANY] * nco, out_shape=o_shapes + list(comm.out_shapes),
                          scratch_shapes=list(scratch_shapes) + list(comm.sem_shapes), compiler_params=compiler_params,
                          input_output_aliases=aliases)

    def run(*args):
        res = call(*args, *comm.ins)
        comm.results = list(res[n_out:])
        return res[0] if single else list(res[:n_out])

    return run


class MultiComm:
    def __init__(self, comms):
        self.comms = list(comms)
        self.ins = [a for c in self.comms for a in c.ins]
        self.out_shapes = [s for c in self.comms for s in c.out_shapes]
        self.sem_shapes = [s for c in self.comms for s in c.sem_shapes]

    def _each(self, cins, couts, sems):
        i = o = s = 0
        for c in self.comms:
            ni, no, ns = len(c.ins), len(c.out_shapes), len(c.sem_shapes)
            yield c, cins[i:i + ni], couts[o:o + no], sems[s:s + ns]
            i, o, s = i + ni, o + no, s + ns

    def start(self, cins, couts, sems):
        for c, a, b, d in self._each(cins, couts, sems):
            c.start(a, b, d)

    def mid(self, cins, couts, sems):
        for c, a, b, d in self._each(cins, couts, sems):
            c.mid(a, b, d)

    def finish(self, cins, couts, sems):
        for c, a, b, d in self._each(cins, couts, sems):
            c.finish(a, b, d)

    @property
    def results(self):
        return [r for c in self.comms for r in c.results]

    @results.setter
    def results(self, vals):
        o = 0
        for c in self.comms:
            c.results = list(vals[o:o + len(c.out_shapes)])
            o += len(c.out_shapes)


def _run_comm(comm, *, name):
    nci, nco = len(comm.ins), len(comm.out_shapes)

    def body(*refs):
        cins, couts, sems = refs[:nci], refs[nci:nci + nco], refs[nci + nco:]
        comm.start(cins, couts, sems)
        comm.mid(cins, couts, sems)
        comm.finish(cins, couts, sems)

    comm.results = list(pl.pallas_call(body, name=name, in_specs=[ANY] * nci, out_specs=[ANY] * nco,
                                       out_shape=list(comm.out_shapes), scratch_shapes=list(comm.sem_shapes))(*comm.ins))


def _nt(a, b):
    return lax.dot_general(a, b, (((1,), (1,)), ((), ())), preferred_element_type=f32)


def _tn(a, b):
    return lax.dot_general(a, b, (((0,), (0,)), ((), ())), preferred_element_type=f32)


def _nn(a, b):
    return jnp.dot(a, b, preferred_element_type=f32)


def _sigmoid(x):
    return 0.5 * jnp.tanh(0.5 * x) + 0.5


def _seg_sum(v, e_ref):
    hi = v.astype(jnp.bfloat16)
    lo = (v - hi.astype(f32)).astype(jnp.bfloat16)
    e = e_ref[...]
    return _nn(hi, e) + _nn(lo, e)


def _seg_sum128(v, e_ref):
    e = e_ref[0:128, 0:128]
    hi = v.astype(jnp.bfloat16)
    lo = (v - hi.astype(f32)).astype(jnp.bfloat16)
    return _nn(hi, e) + _nn(lo, e)


def _head_eye():
    r = lax.broadcasted_iota(jnp.int32, (512, 512), 0) // HD
    c = lax.broadcasted_iota(jnp.int32, (512, 512), 1) // HD
    return (r == c).astype(jnp.bfloat16)


def _rms_norm_rows(x_ref, g_ref, h_ref, tm):
    def chunk(c, carry):
        rows = pl.ds(c * BLK, BLK)
        xf = x_ref[rows, :]
        r = lax.rsqrt(jnp.mean(xf * xf, axis=-1, keepdims=True) + EPS)
        h_ref[rows, :] = (xf * r * g_ref[...]).astype(MXU)
        return carry
    lax.fori_loop(0, tm // BLK, chunk, 0)


def rms_proj(x, g, w, *, tm, tn, name, comm=None):
    T, D = x.shape
    N = w.shape[1]
    ni, nj = T // tm, N // tn
    rc = tm // nj

    def body(x_ref, g_ref, w_ref, h_ref, o_ref, hbuf):
        i, j = pl.program_id(0), pl.program_id(1)

        @pl.when(i < ni)
        def _():
            def chunk(c, carry):
                rows = pl.ds(c * BLK, BLK)
                xf = x_ref[rows, :]
                r = lax.rsqrt(jnp.mean(xf * xf, axis=-1, keepdims=True) + EPS)
                hv = (xf * r * g_ref[...]).astype(MXU)
                h_ref[rows, :] = hv
                hbuf[i % 2, pl.ds(pl.multiple_of(j * rc + c * BLK, BLK), BLK), :] = hv
                return carry
            lax.fori_loop(0, rc // BLK, chunk, 0)

        @pl.when(i > 0)
        def _():
            o_ref[...] = _nn(hbuf[(i - 1) % 2], w_ref[...])

    rows = pl.BlockSpec((rc, D), lambda i, j: (jnp.where(i < ni, i * nj + j, ni * nj - 1), 0))
    return _pallas(
        body, comm=comm, name=name, grid=(ni + 1, nj),
        in_specs=[rows, pl.BlockSpec((1, D), lambda i, j: (0, 0)),
                  pl.BlockSpec((D, tn), lambda i, j: (0, jnp.where(i > 0, j, 0)))],
        out_specs=[rows, pl.BlockSpec((tm, tn), lambda i, j: (jnp.maximum(i - 1, 0), jnp.where(i > 0, j, 0)))],
        out_shape=[S((T, D), MXU), S((T, N), f32)],
        scratch_shapes=[pltpu.VMEM((2, tm, D), MXU)],
        compiler_params=_cp("arbitrary", "arbitrary"),
    )(x, g, w)


def rms_swiglu(x, g, wg, wu, *, tm, tn, name, comm=None):
    T, D = x.shape
    N = wg.shape[1]

    def body(x_ref, g_ref, wg_ref, wu_ref, h_ref, gate_ref, up_ref, act_ref):
        @pl.when(pl.program_id(1) == 0)
        def _():
            _rms_norm_rows(x_ref, g_ref, h_ref, tm)
        h = h_ref[...]
        gate = _nn(h, wg_ref[...])
        up = _nn(h, wu_ref[...])
        gate_ref[...] = gate
        up_ref[...] = up
        act_ref[...] = (gate * _sigmoid(gate) * up).astype(MXU)

    wspec = pl.BlockSpec((D, tn), lambda i, j: (0, j))
    ospec = pl.BlockSpec((tm, tn), lambda i, j: (i, j))
    return _pallas(
        body, comm=comm, name=name, grid=(T // tm, N // tn),
        in_specs=[pl.BlockSpec((tm, D), lambda i, j: (i, 0)), pl.BlockSpec((1, D), lambda i, j: (0, 0)), wspec, wspec],
        out_specs=[pl.BlockSpec((tm, D), lambda i, j: (i, 0)), ospec, ospec, ospec],
        out_shape=[S((T, D), MXU), S((T, N), f32), S((T, N), f32), S((T, N), MXU)],
        compiler_params=_cp("arbitrary", "arbitrary"),
    )(x, g, wg, wu)


def matmul_res(a, w, res, *, tm, tn, name, comm=None):
    T, K = a.shape
    N = w.shape[1]

    def body(a_ref, w_ref, r_ref, o_ref):
        o_ref[...] = r_ref[...] + _nn(a_ref[...], w_ref[...])

    return _pallas(
        body, comm=comm, name=name, grid=(T // tm, N // tn),
        in_specs=[pl.BlockSpec((tm, K), lambda i, j: (i, 0)), pl.BlockSpec((K, tn), lambda i, j: (0, j)),
                  pl.BlockSpec((tm, tn), lambda i, j: (i, j))],
        out_specs=pl.BlockSpec((tm, tn), lambda i, j: (i, j)),
        out_shape=S((T, N), f32),
        compiler_params=_cp("arbitrary", "arbitrary"),
    )(a, w, res)


def nt_plain(a, w, *, tm, tn, name, comm=None):
    T, K = a.shape
    N = w.shape[0]

    def body(a_ref, w_ref, o_ref):
        o_ref[...] = _nt(a_ref[...], w_ref[...])

    return _pallas(
        body, comm=comm, name=name, grid=(T // tm, N // tn),
        in_specs=[pl.BlockSpec((tm, K), lambda i, j: (i, 0)), pl.BlockSpec((tn, K), lambda i, j: (j, 0))],
        out_specs=pl.BlockSpec((tm, tn), lambda i, j: (i, j)),
        out_shape=S((T, N), f32),
        compiler_params=_cp("arbitrary", "arbitrary"),
    )(a, w)


def nt_swiglu_bwd(dy, wd, gate, up, *, tm, tn, name, comm=None):
    T, D = dy.shape
    F = wd.shape[0]

    def body(dy_ref, w_ref, g_ref, u_ref, dg_ref, du_ref):
        d_act = _nt(dy_ref[...], w_ref[...])
        g = g_ref[...]
        sg = _sigmoid(g)
        du_ref[...] = (d_act * (g * sg)).astype(MXU)
        dg_ref[...] = (d_act * u_ref[...] * (sg * (1.0 + g * (1.0 - sg)))).astype(MXU)

    blk = pl.BlockSpec((tm, tn), lambda i, j: (i, j))
    return _pallas(
        body, comm=comm, name=name, grid=(T // tm, F // tn),
        in_specs=[pl.BlockSpec((tm, D), lambda i, j: (i, 0)), pl.BlockSpec((tn, D), lambda i, j: (j, 0)), blk, blk],
        out_specs=[blk, blk],
        out_shape=[S((T, F), MXU), S((T, F), MXU)],
        compiler_params=_cp("arbitrary", "arbitrary"),
    )(dy, wd, gate, up)


def nt_rms_bwd(terms, x, g, dres, *, tm, tn, name, comm=None):
    T, D = x.shape
    K = terms[0][0].shape[1]
    nj = D // tn
    rc = tm // nj
    nt = len(terms)
    ni = T // tm

    def body(*refs):
        a_refs = refs[0:2 * nt:2]
        w_refs = refs[1:2 * nt:2]
        x_ref, g_ref, r_ref, dx_ref, dxb_ref, dg_ref, acc_ref = refs[2 * nt:]
        i, j = pl.program_id(0), pl.program_id(1)

        @pl.when(i < ni)
        def _():
            part = _nt(a_refs[0][...], w_refs[0][...])
            for t in range(1, nt):
                part += _nt(a_refs[t][...], w_refs[t][...])
            acc_ref[i % 2, j] = part

        @pl.when(i > 0)
        def _():
            rows = pl.ds(pl.multiple_of(j * rc, BLK), rc)
            dh = jnp.concatenate([acc_ref[(i - 1) % 2, jj, rows, :] for jj in range(nj)], axis=1)
            xf = x_ref[...]
            r = lax.rsqrt(jnp.mean(xf * xf, axis=-1, keepdims=True) + EPS)
            y = xf * r
            dy = dh * g_ref[...]
            dx = r_ref[...] + r * (dy - y * jnp.mean(dy * y, axis=-1, keepdims=True))
            dx_ref[...] = dx
            dxb_ref[...] = dx.astype(MXU)
            dgain = jnp.sum(dh * y, axis=0, keepdims=True)
            first = jnp.logical_and(i == 1, j == 0)

            @pl.when(first)
            def _():
                dg_ref[...] = dgain

            @pl.when(jnp.logical_not(first))
            def _():
                dg_ref[...] += dgain

    in_specs, args = [], []
    for a, w in terms:
        in_specs += [pl.BlockSpec((tm, K), lambda i, j: (jnp.minimum(i, ni - 1), 0)),
                     pl.BlockSpec((tn, K), lambda i, j: (jnp.where(i < ni, j, nj - 1), 0))]
        args += [a, w]
    row = pl.BlockSpec((rc, D), lambda i, j: (jnp.where(i > 0, (i - 1) * nj + j, 0), 0))
    vec = pl.BlockSpec((1, D), lambda i, j: (0, 0))
    in_specs += [row, vec, row]
    return _pallas(
        body, comm=comm, name=name, grid=(ni + 1, nj), in_specs=in_specs,
        out_specs=[row, row, vec],
        out_shape=[S((T, D), f32), S((T, D), MXU), S((1, D), f32)],
        scratch_shapes=[pltpu.VMEM((2, nj, tm, tn), f32)],
        compiler_params=_cp("arbitrary", "arbitrary", vmem=VMEM_TALL),
    )(*args, x, g, dres)


def tn_matmul(a, b, *, tm, tn, tk, name, by_chip=False, comm=None):
    T, M = a.shape
    N = b.shape[1]
    if by_chip:
        tn = N // 4
        out_spec = pl.BlockSpec((None, tm, tn), lambda i, j, k: (j, i, 0))
        out_shape = S((4, M, tn), f32)
    else:
        out_spec = pl.BlockSpec((tm, tn), lambda i, j, k: (i, j))
        out_shape = S((M, N), f32)

    def body(a_ref, b_ref, o_ref):
        part = _tn(a_ref[...], b_ref[...])

        @pl.when(pl.program_id(2) == 0)
        def _():
            o_ref[...] = part

        @pl.when(pl.program_id(2) > 0)
        def _():
            o_ref[...] += part

    return _pallas(
        body, comm=comm, name=name, grid=(M // tm, N // tn, T // tk),
        in_specs=[pl.BlockSpec((tk, tm), lambda i, j, k: (k, i)), pl.BlockSpec((tk, tn), lambda i, j, k: (k, j))],
        out_specs=out_spec, out_shape=out_shape,
        compiler_params=_cp("arbitrary", "arbitrary", "arbitrary"),
    )(a, b)


def loss_head(y, target, *, tm, name):
    T, D = y.shape
    ni = T // tm

    def body(y_ref, t_ref, dy_ref, dyb_ref, l_ref, acc_ref):
        i = pl.program_id(0)
        e = y_ref[...] - t_ref[...]
        dy = e * (1.0 / D)
        dy_ref[...] = dy
        dyb_ref[...] = dy.astype(MXU)
        part = jnp.sum(e * e, axis=0, keepdims=True)

        @pl.when(i == 0)
        def _():
            acc_ref[...] = part

        @pl.when(i > 0)
        def _():
            acc_ref[...] += part

        @pl.when(i == ni - 1)
        def _():
            tot = jnp.sum(acc_ref[...], axis=1, keepdims=True) * (0.5 / D)
            l_ref[...] = jnp.broadcast_to(tot, (1, 128))

    row = pl.BlockSpec((tm, D), lambda i: (i, 0))
    return pl.pallas_call(
        body, name=name, grid=(ni,), in_specs=[row, row],
        out_specs=[row, row, pl.BlockSpec((1, 128), lambda i: (0, 0))],
        out_shape=[S((T, D), f32), S((T, D), MXU), S((1, 128), f32)],
        scratch_shapes=[pltpu.VMEM((1, D), f32)],
        compiler_params=_cp("arbitrary"),
    )(y, target)


def _qk_norm(v, gain, e_ref):
    r = lax.rsqrt(_seg_sum(v * v, e_ref) * (1.0 / HD) + EPS)
    return v * r * gain


def _dup_halves(pair):
    rolled = pltpu.roll(pair, HD, 1)
    lo = lax.broadcasted_iota(jnp.int32, pair.shape, 1) < HD
    return jnp.where(lo, pair, rolled), jnp.where(lo, rolled, pair)


def prep_fwd(proj, e, gains, *, tm, name):
    T = proj.shape[0]

    def body(p_ref, e_ref, gaq, gak, gcq, gck, aq, ak, av, cq, ckk, cvv):
        aq[...] = _qk_norm(p_ref[:, 0:512], gaq[...], e_ref)
        ak[...] = _qk_norm(p_ref[:, 512:1024], gak[...], e_ref)
        av[...] = p_ref[:, 1024:1536]
        cq[:, 0:512] = _qk_norm(p_ref[:, 2560:3072], gcq[...], e_ref).astype(MXU)
        cq[:, 512:1024] = _qk_norm(p_ref[:, 3072:3584], gcq[...], e_ref).astype(MXU)
        kraw = p_ref[:, 3584:3712]
        kn = kraw * lax.rsqrt(_seg_sum128(kraw * kraw, e_ref) * (1.0 / HD) + EPS) * gck[...]
        k0, k1 = _dup_halves(kn)
        ckk[:, 0:128] = k0.astype(MXU)
        ckk[:, 128:256] = k1.astype(MXU)
        v0, v1 = _dup_halves(p_ref[:, 3712:3840])
        cvv[:, 0:128] = v0.astype(MXU)
        cvv[:, 128:256] = v1.astype(MXU)

    def vec(n):
        return pl.BlockSpec((1, n), lambda i: (0, 0))

    def rows(n):
        return pl.BlockSpec((tm, n), lambda i: (i, 0))

    return pl.pallas_call(
        body, name=name, grid=(T // tm,),
        in_specs=[rows(IN_W), pl.BlockSpec((512, 512), lambda i: (0, 0)), vec(512), vec(512), vec(512), vec(128)],
        out_specs=[rows(512), rows(512), rows(512), rows(1024), rows(256), rows(256)],
        out_shape=[S((T, 512), f32)] * 3 + [S((T, 1024), MXU), S((T, 256), MXU), S((T, 256), MXU)],
        compiler_params=_cp("arbitrary"),
    )(proj, e, *gains)


def _band_mask(max_dist, shut):
    r = lax.broadcasted_iota(jnp.int32, (2 * BLK, 2 * BLK), 0) & (BLK - 1)
    c = lax.broadcasted_iota(jnp.int32, (2 * BLK, 2 * BLK), 1)
    prev = jnp.logical_and(c < BLK, c >= r + (BLK - max_dist) + shut)
    return jnp.logical_or(prev, jnp.logical_and(c >= BLK, c - BLK <= r))


def _prev_mask(max_dist, shut):
    r = lax.broadcasted_iota(jnp.int32, (2 * BLK, BLK), 0) & (BLK - 1)
    c = lax.broadcasted_iota(jnp.int32, (2 * BLK, BLK), 1)
    return c >= r + (BLK - max_dist) + shut


def _head_masks():
    lo = (lax.broadcasted_iota(jnp.int32, (BLK, BLK), 1) < HD).astype(f32)
    return lo.astype(MXU), (1.0 - lo).astype(MXU)


def _stack_heads(x, hm):
    return jnp.concatenate([x * hm[0], x * hm[1]], axis=0)


def _unstack_heads(y, lane_lo):
    return jnp.where(lane_lo, y[0:BLK], y[BLK:2 * BLK])


def _rows(ref, start, dil):
    if dil == 1:
        return ref[pl.ds(start, BLK), :]
    return ref[pl.ds(start, BLK, stride=dil), :]


def _set_rows(ref, start, dil, val):
    if dil == 1:
        ref[pl.ds(start, BLK), :] = val
    else:
        ref[pl.ds(start, BLK, stride=dil), :] = val


def _attn_geometry(T, dil):
    span = BLK * dil
    n = max(1, 512 // span)
    return span, n, T // (span * n)


def band_attn_fwd(q, k, v, sinks, *, dil, max_dist, group, name, mix=None, comm=None):
    T = q.shape[0]
    P = q.shape[1] // BLK
    span, n, nb = _attn_geometry(T, dil)

    def body(*refs):
        s_ref = m_ref = None
        q_ref, kc_ref, kp_ref, vc_ref, vp_ref = refs[:5]
        rest = list(refs[5:])
        if sinks is not None:
            s_ref = rest.pop(0)
        if mix is not None:
            rest.pop(0)
            o_ref, l_ref, m_ref = rest
        else:
            o_ref, l_ref = rest
        b = pl.program_id(0)
        mask = _band_mask(max_dist, 0)
        mask0 = _band_mask(max_dist, jnp.where(b > 0, 0, BLK + 1))
        lane_lo = lax.broadcasted_iota(jnp.int32, (BLK, BLK), 1) < HD
        hm = _head_masks()
        if sinks is not None:
            row_lo = lax.broadcasted_iota(jnp.int32, (1, BLK), 1) < HD
            sk0 = jnp.max(jnp.where(row_lo, s_ref[...], NEG), axis=1, keepdims=True)
            sk1 = jnp.max(jnp.where(row_lo, NEG, s_ref[...]), axis=1, keepdims=True)
            sk = jnp.where(lax.broadcasted_iota(jnp.int32, (2 * BLK, 1), 0) < BLK, sk0, sk1)

        def load(r, sub):
            at = r + sub * span
            kc, vc = _rows(kc_ref, at, dil).astype(MXU), _rows(vc_ref, at, dil).astype(MXU)
            if sub == 0:
                kp, vp = _rows(kp_ref, r, dil).astype(MXU), _rows(vp_ref, r, dil).astype(MXU)
            else:
                kp, vp = _rows(kc_ref, at - span, dil).astype(MXU), _rows(vc_ref, at - span, dil).astype(MXU)
            qst = _stack_heads(_rows(q_ref, at, dil).astype(MXU), hm)
            return (qst, jnp.concatenate([kp, kc], axis=0), jnp.concatenate([vp, vc], axis=0),
                    mask0 if sub == 0 else mask, at)

        def attend(items):
            ss = [jnp.where(m_, _nt(qst, kcat) * SCALE, NEG) for qst, kcat, _, m_, _ in items]
            ms = [jnp.max(s, axis=1, keepdims=True) for s in ss]
            if sinks is not None:
                ms = [jnp.maximum(m, sk) for m in ms]
            ps = [jnp.exp(s - m) for s, m in zip(ss, ms)]
            dens = [jnp.sum(p_, axis=1, keepdims=True) for p_ in ps]
            if sinks is not None:
                dens = [d + jnp.exp(sk - m) for d, m in zip(dens, ms)]
            outs = [_nn(p_.astype(MXU), it[2]) / d for p_, it, d in zip(ps, items, dens)]
            for it, o, m, d in zip(items, outs, ms, dens):
                lse = m + jnp.log(d)
                if m_ref is not None:
                    _set_rows(m_ref, it[4], dil, _unstack_heads(o, lane_lo).astype(MXU))
                _set_rows(o_ref, it[4], dil, _unstack_heads(o, lane_lo))
                _set_rows(l_ref, it[4], dil, jnp.where(lane_lo, lse[0:BLK], lse[BLK:2 * BLK]))

        if dil * n <= 4:
            work = [(r, sub) for r in range(dil) for sub in range(n)]
            for g in range(0, len(work), 2):
                attend([load(*w) for w in work[g:g + 2]])
        else:
            def two_streams(i, carry):
                attend([load(2 * i, 0), load(2 * i + 1, 0)])
                return carry
            lax.fori_loop(0, dil // 2, two_streams, 0)

    rows_per_step = span * n
    qspec = pl.BlockSpec((rows_per_step, BLK), lambda b, p: (b, p))
    cur = pl.BlockSpec((rows_per_step, BLK), lambda b, p: (b, p // group))
    prev = pl.BlockSpec((span, BLK), lambda b, p: (jnp.maximum(b * n - 1, 0), p // group))
    in_specs = [qspec, cur, prev, cur, prev]
    args = [q, k, k, v, v]
    if sinks is not None:
        in_specs.append(pl.BlockSpec((1, BLK), lambda b, p: (0, p)))
        args.append(sinks)
    out_specs, out_shape, aliases = [qspec, qspec], [S(q.shape, f32), S(q.shape, f32)], {}
    if mix is not None:
        first_block = mix.shape[1] // BLK - P
        aliases = {len(args): 2}
        in_specs.append(ANY)
        args.append(mix)
        out_specs.append(pl.BlockSpec((rows_per_step, BLK), lambda b, p: (b, first_block + p)))
        out_shape.append(S(mix.shape, mix.dtype))
    return _pallas(
        body, comm=comm, name=name, grid=(nb, P), in_specs=in_specs, out_specs=out_specs, out_shape=out_shape,
        compiler_params=_cp("arbitrary", "arbitrary"), input_output_aliases=aliases,
    )(*args)


def band_attn_bwd(q, k, v, lse, do, dd, *, dil, max_dist, group, name, comm=None):
    T = q.shape[0]
    P = q.shape[1] // BLK
    span, n, nb = _attn_geometry(T, dil)
    assert group == 1 or dil == 1

    def body(q_ref, qn_ref, do_ref, don_ref, l_ref, ln_ref, d_ref, dn_ref, kc_ref, kp_ref, vc_ref, vp_ref,
             dq_ref, dk_ref, dv_ref):
        b, p = pl.program_id(0), pl.program_id(1)
        mask = _band_mask(max_dist, 0)
        mask0 = _band_mask(max_dist, jnp.where(b > 0, 0, BLK + 1))
        tail = _prev_mask(max_dist, jnp.where(b < nb - 1, 0, BLK + 1))
        lane_lo = lax.broadcasted_iota(jnp.int32, (BLK, BLK), 1) < HD
        hm = _head_masks()
        own_lanes = (lax.broadcasted_iota(jnp.int32, (2 * BLK, BLK), 1) < HD) == (
            lax.broadcasted_iota(jnp.int32, (2 * BLK, BLK), 0) < BLK)

        def per_row(x):
            return jnp.max(jnp.where(own_lanes, jnp.concatenate([x, x], axis=0), NEG), axis=1, keepdims=True)

        def q_side(refs, at):
            q_r, do_r, l_r, d_r = refs
            return (_stack_heads(_rows(q_r, at, dil).astype(MXU), hm), _stack_heads(_rows(do_r, at, dil).astype(MXU), hm),
                    per_row(_rows(l_r, at, dil)), per_row(_rows(d_r, at, dil)))

        def kv(ref, at):
            return _rows(ref, at, dil).astype(MXU)

        first = p % group == 0

        def put_kv(ref, at, val):
            if group == 1:
                _set_rows(ref, at, dil, val)
            else:
                @pl.when(first)
                def _():
                    ref[pl.ds(at, BLK), :] = val

                @pl.when(jnp.logical_not(first))
                def _():
                    ref[pl.ds(at, BLK), :] += val

        def stream(r):
            dks, dvs = [None] * n, [None] * n
            for sub in range(n):
                at = r + sub * span
                qst, dost, lrow, drow = q_side((q_ref, do_ref, l_ref, d_ref), at)
                if sub == 0:
                    kp, vp, m_ = kv(kp_ref, r), kv(vp_ref, r), mask0
                else:
                    kp, vp, m_ = kv(kc_ref, at - span), kv(vc_ref, at - span), mask
                kcat = jnp.concatenate([kp, kv(kc_ref, at)], axis=0)
                vcat = jnp.concatenate([vp, kv(vc_ref, at)], axis=0)
                pr = jnp.where(m_, jnp.exp(_nt(qst, kcat) * SCALE - lrow), 0.0)
                ds = (pr * (_nt(dost, vcat) - drow) * SCALE).astype(MXU)
                prb = pr.astype(MXU)
                _set_rows(dq_ref, at, dil, _unstack_heads(_nn(ds, kcat), lane_lo))
                if sub == 0:
                    dks[0] = _tn(ds[:, BLK:], qst)
                    dvs[0] = _tn(prb[:, BLK:], dost)
                else:
                    dkk, dvv = _tn(ds, qst), _tn(prb, dost)
                    dks[sub - 1] += dkk[0:BLK]
                    dvs[sub - 1] += dvv[0:BLK]
                    dks[sub], dvs[sub] = dkk[BLK:], dvv[BLK:]
            at = r + (n - 1) * span
            qst, dost, lrow, drow = q_side((qn_ref, don_ref, ln_ref, dn_ref), r)
            pr = jnp.where(tail, jnp.exp(_nt(qst, kv(kc_ref, at)) * SCALE - lrow), 0.0)
            ds = (pr * (_nt(dost, kv(vc_ref, at)) - drow) * SCALE).astype(MXU)
            dks[n - 1] += _tn(ds, qst)
            dvs[n - 1] += _tn(pr.astype(MXU), dost)
            for sub in range(n):
                put_kv(dk_ref, r + sub * span, dks[sub])
                put_kv(dv_ref, r + sub * span, dvs[sub])

        if dil <= 4:
            for r in range(dil):
                stream(r)
        else:
            def two_streams(i, carry):
                stream(2 * i)
                stream(2 * i + 1)
                return carry
            lax.fori_loop(0, dil // 2, two_streams, 0)

    rows_per_step = span * n
    qspec = pl.BlockSpec((rows_per_step, BLK), lambda b, p: (b, p))
    qnext = pl.BlockSpec((span, BLK), lambda b, p: (jnp.minimum((b + 1) * n, T // span - 1), p))
    cur = pl.BlockSpec((rows_per_step, BLK), lambda b, p: (b, p // group))
    prev = pl.BlockSpec((span, BLK), lambda b, p: (jnp.maximum(b * n - 1, 0), p // group))
    return _pallas(
        body, comm=comm, name=name, grid=(nb, P),
        in_specs=[qspec, qnext, qspec, qnext, qspec, qnext, qspec, qnext, cur, prev, cur, prev],
        out_specs=[qspec, cur, cur],
        out_shape=[S(q.shape, f32), S(k.shape, f32), S(k.shape, f32)],
        compiler_params=_cp("arbitrary", "arbitrary"),
    )(q, q, do, do, lse, lse, dd, dd, k, k, v, v)


def dil_combine_fwd(ols, *, tm, name):
    T = ols[0].shape[0]

    def body(o1, l1, o2, l2, o3, l3, out_ref):
        a, b, c = l1[...], l2[...], l3[...]
        m = jnp.maximum(jnp.maximum(a, b), c)
        ea, eb, ec = jnp.exp(a - m), jnp.exp(b - m), jnp.exp(c - m)
        out = (ea * o1[...] + eb * o2[...] + ec * o3[...]) / (ea + eb + ec)
        out_ref[...] = out.astype(MXU)

    row = pl.BlockSpec((tm, 512), lambda i: (i, 0))
    return pl.pallas_call(body, name=name, grid=(T // tm,), in_specs=[row] * 6, out_specs=row,
                          out_shape=S((T, A_W + B_W + C_W), MXU), compiler_params=_cp("arbitrary"))(*ols)


def dil_combine_bwd(ols, dmix, e, *, tm, name):
    T = ols[0].shape[0]

    def body(o1, l1, o2, l2, o3, l3, d_ref, e_ref, do1, do2, do3, dd1, dd2, dd3):
        a, b, c = l1[...], l2[...], l3[...]
        m = jnp.maximum(jnp.maximum(a, b), c)
        ea, eb, ec = jnp.exp(a - m), jnp.exp(b - m), jnp.exp(c - m)
        inv = 1.0 / (ea + eb + ec)
        wa, wb, wc = ea * inv, eb * inv, ec * inv
        dout = d_ref[...]
        gbar = _seg_sum(dout * (wa * o1[...] + wb * o2[...] + wc * o3[...]), e_ref)
        do1[...] = wa * dout
        do2[...] = wb * dout
        do3[...] = wc * dout
        dd1[...] = wa * gbar
        dd2[...] = wb * gbar
        dd3[...] = wc * gbar

    row = pl.BlockSpec((tm, 512), lambda i: (i, 0))
    return pl.pallas_call(
        body, name=name, grid=(T // tm,),
        in_specs=[row] * 6 + [row, pl.BlockSpec((512, 512), lambda i: (0, 0))],
        out_specs=[row] * 6,
        out_shape=[S((T, 512), f32)] * 6,
        compiler_params=_cp("arbitrary"),
    )(*ols, dmix, e)


def swa_pre_bwd(o, lse, dmix, sinks, e, *, tm, name):
    T = o.shape[0]
    ni = T // tm

    def body(o_ref, l_ref, d_ref, s_ref, e_ref, do_ref, dd_ref, ds_ref):
        i = pl.program_id(0)
        dout = d_ref[...]
        do_ref[...] = dout.astype(MXU)
        prod = dout * o_ref[...]
        dd = jnp.concatenate([_seg_sum(prod[:, 0:512], e_ref), _seg_sum(prod[:, 512:1024], e_ref)], axis=1)
        dd_ref[...] = dd
        part = -jnp.sum(jnp.exp(s_ref[...] - l_ref[...]) * dd, axis=0, keepdims=True)

        @pl.when(i == 0)
        def _():
            ds_ref[...] = part

        @pl.when(i > 0)
        def _():
            ds_ref[...] += part

    row = pl.BlockSpec((tm, 1024), lambda i: (i, 0))
    vec = pl.BlockSpec((1, 1024), lambda i: (0, 0))
    return pl.pallas_call(
        body, name=name, grid=(ni,),
        in_specs=[row, row, pl.BlockSpec((tm, 1024), lambda i: (i, 1)), vec, pl.BlockSpec((512, 512), lambda i: (0, 0))],
        out_specs=[row, row, vec],
        out_shape=[S((T, 1024), MXU), S((T, 1024), f32), S((1, 1024), f32)],
        compiler_params=_cp("arbitrary"),
    )(o, lse, dmix, sinks, e)


SHIFT_PAD = 24


def _shifted_copies(buf_ref, sh_ref, length):
    for r in range(1, 8):
        sh_ref[r - 1, 0:length, :] = buf_ref[pl.ds(r, length), :]


def _window(buf_ref, sh_ref, start, rows):
    q, r = divmod(start, 8)
    if r == 0:
        return buf_ref[pl.ds(8 * q, rows), :]
    return sh_ref[r - 1, pl.ds(8 * q, rows), :]


TAP_ROWS = 64


def _tap_sum(buf_ref, sh_ref, w_ref, starts, rows):
    outs = []
    for c0 in range(0, rows, TAP_ROWS):
        n = min(TAP_ROWS, rows - c0)
        acc = _window(buf_ref, sh_ref, starts[0] + c0, n) * w_ref[pl.ds(0, 1), :]
        for j in range(1, CONV_K):
            acc += _window(buf_ref, sh_ref, starts[j] + c0, n) * w_ref[pl.ds(j, 1), :]
        outs.append(acc)
    return jnp.concatenate(outs, axis=0)


def _conv_taps(buf_ref, sh_ref, w_ref, start, rows):
    return _tap_sum(buf_ref, sh_ref, w_ref, [start + j for j in range(CONV_K)], rows)


def conv_fwd(proj, mix, w, b, ln_g, ln_b, *, tb, name):
    T = proj.shape[0]
    hb = tb // HALO

    def body(u_ref, g_ref, up_ref, gp_ref, w_ref, b_ref, lg_ref, lb_ref, mix_ref, o_ref, hbuf, hsh):
        i = pl.program_id(0)
        hprev = up_ref[...] * _sigmoid(gp_ref[...])
        hbuf[0:HALO, :] = hprev * jnp.where(i > 0, 1.0, 0.0)
        hbuf[HALO:HALO + tb, :] = u_ref[...] * _sigmoid(g_ref[...])
        _shifted_copies(hbuf, hsh, tb + SHIFT_PAD)
        y = _conv_taps(hbuf, hsh, w_ref, HALO - (CONV_K - 1), tb) + b_ref[...]
        mu = jnp.mean(y, axis=-1, keepdims=True)
        yc = y - mu
        var = jnp.mean(yc * yc, axis=-1, keepdims=True)
        z = yc * lax.rsqrt(var + EPS) * lg_ref[...] + lb_ref[...]
        o_ref[...] = (z * _sigmoid(z)).astype(MXU)

    vec = pl.BlockSpec((1, 512), lambda i: (0, 0))
    return pl.pallas_call(
        body, name=name, grid=(T // tb,),
        in_specs=[pl.BlockSpec((tb, 512), lambda i: (i, 3)), pl.BlockSpec((tb, 512), lambda i: (i, 4)),
                  pl.BlockSpec((HALO, 512), lambda i: (jnp.maximum(i * hb - 1, 0), 3)),
                  pl.BlockSpec((HALO, 512), lambda i: (jnp.maximum(i * hb - 1, 0), 4)),
                  pl.BlockSpec((HALO, 512), lambda i: (0, 0)), vec, vec, vec, ANY],
        out_specs=pl.BlockSpec((tb, 512), lambda i: (i, 1)),
        out_shape=S(mix.shape, mix.dtype),
        scratch_shapes=[pltpu.VMEM((tb + HALO, 512), f32), pltpu.VMEM((7, tb + SHIFT_PAD, 512), f32)],
        compiler_params=_cp("arbitrary"), input_output_aliases={8: 0},
    )(proj, proj, proj, proj, w, b, ln_g, ln_b, mix)


def conv_bwd(proj, dmix, w, b, ln_g, ln_b, *, tb, name, comm=None):
    T = proj.shape[0]
    hb = tb // HALO
    ni = T // tb
    last_h = T // HALO - 1
    ext = tb + HALO

    def body(u_ref, g_ref, up_ref, gp_ref, un_ref, gn_ref, d_ref, dn_ref, w_ref, b_ref, lg_ref, lb_ref,
             du_ref, dg_ref, dw_ref, db_ref, dlg_ref, dlb_ref, hbuf, dybuf, hsh, dsh):
        i = pl.program_id(0)
        hbuf[0:HALO, :] = up_ref[...] * _sigmoid(gp_ref[...]) * jnp.where(i > 0, 1.0, 0.0)
        u = u_ref[...]
        sg = _sigmoid(g_ref[...])
        hbuf[HALO:HALO + tb, :] = u * sg
        hbuf[HALO + tb:HALO + ext, :] = un_ref[...] * _sigmoid(gn_ref[...])
        _shifted_copies(hbuf, hsh, ext + SHIFT_PAD)
        y = _conv_taps(hbuf, hsh, w_ref, HALO - (CONV_K - 1), ext) + b_ref[...]
        mu = jnp.mean(y, axis=-1, keepdims=True)
        yc = y - mu
        rstd = lax.rsqrt(jnp.mean(yc * yc, axis=-1, keepdims=True) + EPS)
        yn = yc * rstd
        z = yn * lg_ref[...] + lb_ref[...]
        sz = _sigmoid(z)
        row = lax.broadcasted_iota(jnp.int32, (ext, 1), 0)
        own = row < tb
        keep = row < jnp.where(i < ni - 1, ext, tb)
        dout = jnp.concatenate([d_ref[...], dn_ref[...]], axis=0)
        dz = jnp.where(keep, dout * (sz * (1.0 + z * (1.0 - sz))), 0.0)
        dyn = dz * lg_ref[...]
        dy = rstd * (dyn - jnp.mean(dyn, axis=-1, keepdims=True) - yn * jnp.mean(dyn * yn, axis=-1, keepdims=True))
        dybuf[...] = dy
        _shifted_copies(dybuf, dsh, tb + SHIFT_PAD)
        dz_own = jnp.where(own, dz, 0.0)
        dlg = jnp.sum(dz_own * yn, axis=0, keepdims=True)
        dlb = jnp.sum(dz_own, axis=0, keepdims=True)
        dy_own = dybuf[0:tb, :]
        dbias = jnp.sum(dy_own, axis=0, keepdims=True)
        dh = _tap_sum(dybuf, dsh, w_ref, [CONV_K - 1 - j for j in range(CONV_K)], tb)
        du_ref[...] = (dh * sg).astype(MXU)
        dg_ref[...] = (dh * u * sg * (1.0 - sg)).astype(MXU)
        taps = [jnp.sum(dy_own * _window(hbuf, hsh, HALO - (CONV_K - 1) + j, tb), axis=0, keepdims=True)
                for j in range(CONV_K)]
        taps.append(jnp.zeros((1, 512), f32))
        dwt = jnp.concatenate(taps, axis=0)

        @pl.when(i == 0)
        def _():
            dw_ref[...] = dwt
            db_ref[...] = dbias
            dlg_ref[...] = dlg
            dlb_ref[...] = dlb

        @pl.when(i > 0)
        def _():
            dw_ref[...] += dwt
            db_ref[...] += dbias
            dlg_ref[...] += dlg
            dlb_ref[...] += dlb

    vec = pl.BlockSpec((1, 512), lambda i: (0, 0))
    wspec = pl.BlockSpec((HALO, 512), lambda i: (0, 0))

    def halo_prev(col):
        return pl.BlockSpec((HALO, 512), lambda i: (jnp.maximum(i * hb - 1, 0), col))

    def halo_next(col):
        return pl.BlockSpec((HALO, 512), lambda i: (jnp.minimum((i + 1) * hb, last_h), col))

    row = pl.BlockSpec((tb, 512), lambda i: (i, 0))
    return _pallas(
        body, comm=comm, name=name, grid=(ni,),
        in_specs=[pl.BlockSpec((tb, 512), lambda i: (i, 3)), pl.BlockSpec((tb, 512), lambda i: (i, 4)),
                  halo_prev(3), halo_prev(4), halo_next(3), halo_next(4),
                  pl.BlockSpec((tb, 512), lambda i: (i, 1)), halo_next(1), wspec, vec, vec, vec],
        out_specs=[row, row, wspec, vec, vec, vec],
        out_shape=[S((T, 512), MXU), S((T, 512), MXU), S((HALO, 512), f32)] + [S((1, 512), f32)] * 3,
        scratch_shapes=[pltpu.VMEM((tb + 2 * HALO, 512), f32), pltpu.VMEM((ext, 512), f32),
                        pltpu.VMEM((7, ext + SHIFT_PAD, 512), f32), pltpu.VMEM((7, tb + SHIFT_PAD, 512), f32)],
        compiler_params=_cp("arbitrary"),
    )(proj, proj, proj, proj, proj, proj, dmix, dmix, w, b, ln_g, ln_b)


def _qk_norm_bwd(v, gain, dout, e_ref):
    r = lax.rsqrt(_seg_sum(v * v, e_ref) * (1.0 / HD) + EPS)
    y = v * r
    dgain = jnp.sum(dout * y, axis=0, keepdims=True)
    dy = dout * gain
    dv = r * (dy - y * (_seg_sum(dy * y, e_ref) * (1.0 / HD)))
    return dv, dgain


def prep_bwd(proj, e, gains, da, dc, dconv, *, tm, name):
    T = proj.shape[0]

    def body(*refs):
        p_ref, e_ref, gaq, gak, gcq, gck = refs[0:6]
        a_refs = refs[6:15]
        dcq, dckk, dcvv, du, dgt = refs[15:20]
        dp, gaq_o, gak_o, gcq_o, gck_o = refs[20:]
        i = pl.program_id(0)
        dq = a_refs[0][...] + a_refs[3][...] + a_refs[6][...]
        dk = a_refs[1][...] + a_refs[4][...] + a_refs[7][...]
        dv = a_refs[2][...] + a_refs[5][...] + a_refs[8][...]
        d, g_aq = _qk_norm_bwd(p_ref[:, 0:512], gaq[...], dq, e_ref)
        dp[:, 0:512] = d.astype(MXU)
        d, g_ak = _qk_norm_bwd(p_ref[:, 512:1024], gak[...], dk, e_ref)
        dp[:, 512:1024] = d.astype(MXU)
        dp[:, 1024:1536] = dv.astype(MXU)
        dp[:, 1536:2048] = du[...]
        dp[:, 2048:2560] = dgt[...]
        d, g_cq0 = _qk_norm_bwd(p_ref[:, 2560:3072], gcq[...], dcq[:, 0:512], e_ref)
        dp[:, 2560:3072] = d.astype(MXU)
        d, g_cq1 = _qk_norm_bwd(p_ref[:, 3072:3584], gcq[...], dcq[:, 512:1024], e_ref)
        dp[:, 3072:3584] = d.astype(MXU)
        lo = lax.broadcasted_iota(jnp.int32, (tm, 128), 1) < HD

        def fold(ref):
            g0, g1 = ref[:, 0:128], ref[:, 128:256]
            s0 = g0 + pltpu.roll(g0, HD, 1)
            s1 = g1 + pltpu.roll(g1, HD, 1)
            return jnp.where(lo, s0, s1)

        dkn = fold(dckk)
        kraw = p_ref[:, 3584:3712]
        r = lax.rsqrt(_seg_sum128(kraw * kraw, e_ref) * (1.0 / HD) + EPS)
        y = kraw * r
        g_ck = jnp.sum(dkn * y, axis=0, keepdims=True)
        dy = dkn * gck[...]
        dp[:, 3584:3712] = (r * (dy - y * (_seg_sum128(dy * y, e_ref) * (1.0 / HD)))).astype(MXU)
        dp[:, 3712:3840] = fold(dcvv).astype(MXU)
        g_cq = jnp.concatenate([g_cq0, g_cq1], axis=1)

        @pl.when(i == 0)
        def _():
            gaq_o[...] = g_aq
            gak_o[...] = g_ak
            gcq_o[...] = g_cq
            gck_o[...] = g_ck

        @pl.when(i > 0)
        def _():
            gaq_o[...] += g_aq
            gak_o[...] += g_ak
            gcq_o[...] += g_cq
            gck_o[...] += g_ck

    def vec(n):
        return pl.BlockSpec((1, n), lambda i: (0, 0))

    def rows(n):
        return pl.BlockSpec((tm, n), lambda i: (i, 0))

    return pl.pallas_call(
        body, name=name, grid=(T // tm,),
        in_specs=[rows(IN_W), pl.BlockSpec((512, 512), lambda i: (0, 0)), vec(512), vec(512), vec(512), vec(128)]
        + [rows(512)] * 9 + [rows(1024), rows(256), rows(256), rows(512), rows(512)],
        out_specs=[rows(IN_W), vec(512), vec(512), vec(1024), vec(128)],
        out_shape=[S((T, IN_W), MXU), S((1, 512), f32), S((1, 512), f32), S((1, 1024), f32), S((1, 128), f32)],
        compiler_params=_cp("arbitrary"),
    )(proj, e, *gains, *da, *dc, *dconv)


def adamw(w, m, v, pieces, *, tr, name, comm=None):
    n, R, C = w.shape
    c1 = 1.0 - ADAM_B1 ** ADAM_STEP
    c2 = 1.0 - ADAM_B2 ** ADAM_STEP
    npc = len(pieces)

    def body(*refs):
        w_ref, m_ref, v_ref = refs[0:3]
        p_refs = refs[3:3 + npc]
        g_ref, d_ref, mo_ref, vo_ref = refs[3 + npc:]
        g = p_refs[0][...].astype(f32)
        for p in p_refs[1:]:
            g = g + p[...].astype(f32)
        mn = ADAM_B1 * m_ref[...] + (1.0 - ADAM_B1) * g
        vn = ADAM_B2 * v_ref[...] + (1.0 - ADAM_B2) * (g * g)
        g_ref[...] = g
        mo_ref[...] = mn
        vo_ref[...] = vn
        d_ref[...] = -ADAM_LR * ((mn / c1) / (jnp.sqrt(vn / c2) + ADAM_EPS) + ADAM_WD * w_ref[...])

    blk = pl.BlockSpec((None, tr, C), lambda l, i: (l, i, 0))
    return _pallas(
        body, comm=comm, name=name, grid=(n, R // tr), in_specs=[blk] * (3 + npc), out_specs=[blk] * 4,
        out_shape=[S(w.shape, f32)] * 4, compiler_params=_cp("arbitrary", "arbitrary"),
    )(w, m, v, *pieces)


def add_halves(pieces, other, *, tr, name):
    _, _, r, cc = pieces.shape

    def body(c_ref, a_ref, b_ref, o_ref):
        o_ref[...] = (a_ref[...] + b_ref[...]).astype(jnp.bfloat16)

    blk = pl.BlockSpec((None, tr, cc), lambda s, i, c_ref: (s, i, 0))
    grid_spec = pltpu.PrefetchScalarGridSpec(
        num_scalar_prefetch=1, grid=(4, r // tr),
        in_specs=[pl.BlockSpec((None, None, tr, cc), lambda s, i, c_ref: (s, c_ref[0], i, 0)), blk], out_specs=blk)
    core = lax.axis_index("c").astype(jnp.int32).reshape(1)
    return pl.pallas_call(body, name=name, grid_spec=grid_spec, out_shape=S((4, r, cc), jnp.bfloat16),
                          compiler_params=_cp("arbitrary", "arbitrary"))(core, pieces, other)


def sum8(parts, *, name):
    _, R, C = parts.shape

    def body(p_ref, o_ref):
        acc = p_ref[0]
        for d in range(1, 8):
            acc = acc + p_ref[d]
        o_ref[...] = acc

    return pl.pallas_call(body, name=name, out_shape=S((R, C), f32))(parts)


def _pos():
    return lax.axis_index("x"), lax.axis_index("y"), lax.axis_index("c")


def _other_chips(x, y):
    return [(1 - x, y), (x, 1 - y), (1 - x, 1 - y)]


class GatherComm:
    def __init__(self, shards, in_place=None):
        self.ins = list(shards)
        self.nt = nt = len(shards)
        self.in_place = list(in_place) if in_place is not None else [False] * nt
        self.out_shapes = [S((2 * s.shape[1], 4 * s.shape[2]), s.dtype) if ip else S((4,) + s.shape, s.dtype)
                           for s, ip in zip(shards, self.in_place)]
        self.sem_shapes = [pltpu.SemaphoreType.DMA((nt, 6)), pltpu.SemaphoreType.DMA((nt, 6)),
                           pltpu.SemaphoreType.DMA((nt, 2))]
        self.results = None

    def _place(self, couts, t, chip, half):
        cid = 2 * chip[0] + chip[1]
        if not self.in_place[t]:
            return couts[t].at[cid, half]
        _, r, c = self.ins[t].shape
        row0 = half * r if isinstance(half, int) else pl.multiple_of(half * r, 16)
        return couts[t].at[pl.ds(row0, r), pl.ds(pl.multiple_of(cid * c, 128), c)]

    def _copy(self, couts, sems, t, k, chip, half, to, src=None):
        dst = self._place(couts, t, chip, half)
        return pltpu.make_async_remote_copy(
            src_ref=dst if src is None else src, dst_ref=dst,
            send_sem=sems[0].at[t, k], recv_sem=sems[1].at[t, k], device_id=to, device_id_type=MESH)

    def _local(self, cins, couts, sems, t):
        x, y, _ = _pos()
        return [pltpu.make_async_copy(cins[t].at[half], self._place(couts, t, (x, y), half), sems[2].at[t, half])
                for half in range(2)]

    def start(self, cins, couts, sems):
        x, y, c = _pos()
        for t in range(self.nt):
            for cp in self._local(cins, couts, sems, t):
                cp.start()
            for j, chip in enumerate(_other_chips(x, y)):
                self._copy(couts, sems, t, j, (x, y), c, (*chip, c), src=cins[t].at[c]).start()

    def mid(self, cins, couts, sems):
        x, y, c = _pos()
        for t in range(self.nt):
            for j, chip in enumerate(_other_chips(x, y)):
                self._copy(couts, sems, t, j, chip, c, (x, y, c)).wait_recv()
                self._copy(couts, sems, t, 3 + j, chip, c, (x, y, 1 - c)).start()

    def finish(self, cins, couts, sems):
        x, y, c = _pos()
        for t in range(self.nt):
            for j, chip in enumerate(_other_chips(x, y)):
                self._copy(couts, sems, t, 3 + j, chip, 1 - c, (x, y, c)).wait_recv()
        for t in range(self.nt):
            for j, chip in enumerate(_other_chips(x, y)):
                self._copy(couts, sems, t, j, (x, y), c, (*chip, c), src=cins[t].at[c]).wait_send()
                self._copy(couts, sems, t, 3 + j, chip, c, (x, y, 1 - c)).wait_send()
            for cp in self._local(cins, couts, sems, t):
                cp.wait()


class SwapComm:
    def __init__(self, pieces):
        self.ins = list(pieces)
        self.nt = nt = len(pieces)
        self.out_shapes = [S((4,) + p.shape[2:], p.dtype) for p in pieces]
        self.sem_shapes = [pltpu.SemaphoreType.DMA((nt, 4)), pltpu.SemaphoreType.DMA((nt, 4))]
        self.results = None

    def _copies(self, cins, couts, sems):
        x, y, c = _pos()
        return [pltpu.make_async_remote_copy(src_ref=cins[t].at[s, 1 - c], dst_ref=couts[t].at[s],
                                             send_sem=sems[0].at[t, s], recv_sem=sems[1].at[t, s],
                                             device_id=(x, y, 1 - c), device_id_type=MESH)
                for t in range(self.nt) for s in range(4)]

    def start(self, cins, couts, sems):
        for cp in self._copies(cins, couts, sems):
            cp.start()

    def mid(self, cins, couts, sems):
        pass

    def finish(self, cins, couts, sems):
        for cp in self._copies(cins, couts, sems):
            cp.wait()


class ExchangeComm:
    def __init__(self, arrs):
        self.ins = list(arrs)
        self.nt = nt = len(arrs)
        self.out_shapes = [S((2,) + a.shape, a.dtype) for a in arrs]
        self.sem_shapes = [pltpu.SemaphoreType.DMA((nt, 7)), pltpu.SemaphoreType.DMA((nt, 7)),
                           pltpu.SemaphoreType.DMA((nt,))]
        self.results = None

    def _copy(self, couts, sems, t, k, half, src_chip, to, src=None):
        dst = couts[t].at[half, src_chip]
        return pltpu.make_async_remote_copy(
            src_ref=dst if src is None else src, dst_ref=dst,
            send_sem=sems[0].at[t, k], recv_sem=sems[1].at[t, k], device_id=to, device_id_type=MESH)

    def _local(self, cins, couts, sems, t):
        x, y, c = _pos()
        return pltpu.make_async_copy(cins[t].at[2 * x + y], couts[t].at[c, 2 * x + y], sems[2].at[t])

    def _firsts(self, cins, couts, sems, t):
        x, y, c = _pos()
        me = 2 * x + y
        cps = [self._copy(couts, sems, t, j, c, me, (*chip, c), src=cins[t].at[2 * chip[0] + chip[1]])
               for j, chip in enumerate(_other_chips(x, y))]
        return cps + [self._copy(couts, sems, t, 6, c, me, (x, y, 1 - c), src=cins[t].at[me])]

    def start(self, cins, couts, sems):
        for t in range(self.nt):
            self._local(cins, couts, sems, t).start()
            for cp in self._firsts(cins, couts, sems, t):
                cp.start()

    def mid(self, cins, couts, sems):
        x, y, c = _pos()
        for t in range(self.nt):
            for j, chip in enumerate(_other_chips(x, y)):
                cid = 2 * chip[0] + chip[1]
                self._copy(couts, sems, t, j, c, cid, (x, y, c)).wait_recv()
                self._copy(couts, sems, t, 3 + j, c, cid, (x, y, 1 - c)).start()

    def finish(self, cins, couts, sems):
        x, y, c = _pos()
        for t in range(self.nt):
            for j, chip in enumerate(_other_chips(x, y)):
                self._copy(couts, sems, t, 3 + j, 1 - c, 2 * chip[0] + chip[1], (x, y, c)).wait_recv()
            self._copy(couts, sems, t, 6, 1 - c, 2 * x + y, (x, y, c)).wait_recv()
        for t in range(self.nt):
            for cp in self._firsts(cins, couts, sems, t):
                cp.wait_send()
            for j, chip in enumerate(_other_chips(x, y)):
                self._copy(couts, sems, t, 3 + j, c, 2 * chip[0] + chip[1], (x, y, 1 - c)).wait_send()
            self._local(cins, couts, sems, t).wait()


def gather_small(vec, *, name):
    R, C = vec.shape

    def body(v_ref, out_ref, send_sems, recv_sems):
        x, y, c = _pos()
        me = 4 * x + 2 * y + c
        out_ref[me] = v_ref[...]
        cps = []
        def peer(k):
            fx, fy, fc = (k >> 2) & 1, (k >> 1) & 1, k & 1
            return (1 - x if fx else x), (1 - y if fy else y), (1 - c if fc else c)

        for k in range(1, 8):
            cp = pltpu.make_async_remote_copy(src_ref=v_ref, dst_ref=out_ref.at[me], send_sem=send_sems.at[k - 1],
                                              recv_sem=recv_sems.at[k - 1], device_id=peer(k), device_id_type=MESH)
            cp.start()
            cps.append(cp)
        for k in range(1, 8):
            px, py, pc = peer(k)
            pltpu.make_async_remote_copy(src_ref=v_ref, dst_ref=out_ref.at[4 * px + 2 * py + pc],
                                         send_sem=send_sems.at[k - 1], recv_sem=recv_sems.at[k - 1],
                                         device_id=(px, py, pc), device_id_type=MESH).wait_recv()
        for cp in cps:
            cp.wait_send()

    return pl.pallas_call(
        body, name=name,
        in_specs=[pl.BlockSpec(memory_space=pltpu.VMEM)], out_specs=pl.BlockSpec(memory_space=pltpu.VMEM),
        out_shape=S((8, R, C), vec.dtype),
        scratch_shapes=[pltpu.SemaphoreType.DMA((7,)), pltpu.SemaphoreType.DMA((7,))],
    )(vec)


def _tile(n, prefs):
    for p in prefs:
        if n % p == 0:
            return p
    return n


def _lanes(g, reps):
    return jnp.tile(g.reshape(1, -1), (1, reps))


class _NoRide:
    def rider(self, name):
        return None

    def landed(self, comm):
        pass

    def grad(self, name, val):
        pass


def _layer_fwd(x, p, e, ride=_NoRide()):
    T, D = x.shape
    tm = _tile(T, (512, 256, 128))
    tall = _tile(T, (1024, 512, 256, 128))

    def carried(fn, *args, name, **kw):
        comm = ride.rider(name)
        out = fn(*args, name=name, comm=comm, **kw)
        ride.landed(comm)
        return out

    h, proj = carried(rms_proj, x, p["norm1_g"], p["w_in"], tm=tall, tn=_tile(IN_W, (1920,)), name="rms_proj")
    gains = (_lanes(p["a_q_g"], 8), _lanes(p["a_k_g"], 8), _lanes(p["c_q_g"], 8), _lanes(p["c_k_g"], 2))
    aq, ak, av, cq, ckk, cvv = prep_fwd(proj, e, gains, tm=_tile(T, (256, 128)), name="prep_fwd")
    ols = []
    for d in DILATIONS:
        ols += carried(band_attn_fwd, aq, ak, av, None, dil=d, max_dist=A_DIST, group=1, name=f"dil_attn_fwd_{d}")
    mix = dil_combine_fwd(ols, tm=tm, name="dil_combine_fwd")
    mix = conv_fwd(proj, mix, p["conv_w"], p["conv_b"], p["conv_ln_g"], p["conv_ln_b"], tb=tm, name="conv_fwd")
    sinks = jnp.repeat(p["c_sinks"].reshape(-1), HD).reshape(1, C_W)
    o_c, l_c, mix = carried(band_attn_fwd, cq, ckk, cvv, sinks, dil=1, max_dist=C_DIST, group=4, mix=mix,
                            name="swa_attn_fwd")
    x1 = carried(matmul_res, mix, p["w_out"], x, tm=tall, tn=_tile(D, (1024, 512, 256)), name="out_proj")
    F = p["w_gate"].shape[1]
    h2, gate, up, act = carried(rms_swiglu, x1, p["norm2_g"], p["w_gate"], p["w_up"], tm=tall,
                                tn=_tile(F, (512, 256, 128)), name="rms_swiglu")
    x2 = carried(matmul_res, act, p["w_down"], x1, tm=tall, tn=_tile(D, (512, 256)), name="ffn_down")
    saved = dict(x=x, h=h, proj=proj, gains=gains, aq=aq, ak=ak, av=av, cq=cq, ckk=ckk, cvv=cvv, ols=ols, o_c=o_c,
                 l_c=l_c, sinks=sinks, mix=mix, x1=x1, h2=h2, gate=gate, up=up, act=act)
    return x2, saved


def _layer_bwd(dx2, dx2b, p, s, e, ride=_NoRide()):
    T, D = dx2.shape
    F = p["w_gate"].shape[1]
    tm = _tile(T, (512, 256, 128))
    tmm = _tile(T, (1024, 512, 256, 128))
    tkT = _tile(T, (2048, 1024, 512))
    tF = _tile(F, (512, 256, 128))
    tD = _tile(D, (1024, 512, 256))
    g = {}

    def carried(fn, *args, name, **kw):
        comm = ride.rider(name)
        out = fn(*args, name=name, comm=comm, **kw)
        ride.landed(comm)
        return out

    def big(n, val):
        g[n] = val
        ride.grad(n, val)

    d_gate, d_up = carried(nt_swiglu_bwd, dx2b, p["w_down"], s["gate"], s["up"], tm=tmm, tn=tF, name="ffn_down_bwd")
    big("w_down", tn_matmul(s["act"], dx2b, tm=_tile(F, (1408, 512, 256, 128)), tn=tD, tk=tkT, name="grad_w_down"))
    big("w_gate", carried(tn_matmul, s["h2"], d_gate, tm=tD, tn=tF, tk=tkT, by_chip=True, name="grad_w_gate"))
    big("w_up", carried(tn_matmul, s["h2"], d_up, tm=tD, tn=tF, tk=tkT, by_chip=True, name="grad_w_up"))
    dx1, dx1b, g["norm2_g"] = carried(nt_rms_bwd, [(d_gate, p["w_gate"]), (d_up, p["w_up"])], s["x1"], p["norm2_g"],
                                      dx2, tm=tm, tn=_tile(D, (512, 256)), name="ffn_in_bwd")
    dmix = nt_plain(dx1b, p["w_out"], tm=tmm, tn=tD, name="out_proj_bwd")
    big("w_out", tn_matmul(s["mix"], dx1b, tm=1024, tn=tD, tk=tkT, name="grad_w_out"))
    dos = dil_combine_bwd(s["ols"], dmix, e, tm=tm, name="dil_combine_bwd")
    da = []
    for n, d in enumerate(DILATIONS):
        da += carried(band_attn_bwd, s["aq"], s["ak"], s["av"], s["ols"][2 * n + 1], dos[n], dos[3 + n], dil=d,
                      max_dist=A_DIST, group=1, name=f"dil_attn_bwd_{d}")
    du, dgt, gw, gb, glg, glb = carried(conv_bwd, s["proj"], dmix, p["conv_w"], p["conv_b"], p["conv_ln_g"],
                                        p["conv_ln_b"], tb=tm, name="conv_bwd")
    g["conv_w"], g["conv_b"], g["conv_ln_g"], g["conv_ln_b"] = gw[:CONV_K], gb, glg, glb
    do_c, dd_c, dsink = swa_pre_bwd(s["o_c"], s["l_c"], dmix, s["sinks"], e, tm=_tile(T, (256, 128)), name="swa_pre_bwd")
    g["c_sinks"] = dsink.reshape(-1, HD)[:, 0]
    dcq, dckk, dcvv = carried(band_attn_bwd, s["cq"], s["ckk"], s["cvv"], s["l_c"], do_c, dd_c, dil=1, max_dist=C_DIST,
                              group=4, name="swa_attn_bwd")
    dproj, gaq, gak, gcq, gck = prep_bwd(s["proj"], e, s["gains"], da, (dcq, dckk, dcvv), (du, dgt),
                                         tm=_tile(T, (256, 128)), name="prep_bwd")
    g["a_q_g"] = gaq.reshape(-1, HD).sum(0)
    g["a_k_g"] = gak.reshape(-1, HD).sum(0)
    g["c_q_g"] = gcq.reshape(-1, HD).sum(0)
    g["c_k_g"] = gck.reshape(-1, HD).sum(0)
    big("w_in", tn_matmul(s["h"], dproj, tm=tD, tn=_tile(IN_W, (1280,)), tk=tkT, name="grad_w_in"))
    dx, dxb, g["norm1_g"] = carried(nt_rms_bwd, [(dproj, p["w_in"])], s["x"], p["norm1_g"], dx1, tm=tmm,
                                    tn=_tile(D, (512, 256)), name="in_proj_bwd")
    return dx, dxb, g


BIG = ("w_in", "w_out", "w_gate", "w_up", "w_down")
COL_SHARDED = ("w_in", "w_gate", "w_up")
SMALL = ("norm1_g", "a_q_g", "a_k_g", "conv_w", "conv_b", "conv_ln_g", "conv_ln_b", "c_q_g", "c_k_g", "c_sinks", "norm2_g")


def _to_pieces(g, name):
    if g.ndim == 3:
        return g.reshape(4, 2, g.shape[1] // 2, g.shape[2])
    R, C = g.shape
    if name in COL_SHARDED:
        return g.reshape(2, R // 2, 4, C // 4).transpose(2, 0, 1, 3)
    return g.reshape(4, 2, R // 8, C)


def _from_gathered(w, name):
    _, _, r, c = w.shape
    if name in COL_SHARDED:
        return w.transpose(1, 2, 0, 3).reshape(2 * r, 4 * c)
    return w.reshape(8 * r, c)


def _shard(W, l, n):
    w = W[n][l]
    return w.astype(MXU).reshape(2, w.shape[0] // 2, w.shape[1])


class _LayerParams(dict):
    def __init__(self, layer, full, small):
        super().__init__(small)
        self.layer, self.full = layer, full

    def __missing__(self, n):
        return self.full[(self.layer, n)]


FWD_RIDES = {
    (0, "rms_proj"): ((0, "w_out"), (0, "w_gate")),
    (0, "swa_attn_fwd"): ((0, "w_up"),),
    (0, "rms_swiglu"): ((0, "w_down"), (1, "w_in"), (1, "w_out")),
    (0, "ffn_down"): ((1, "w_gate"),),
    (1, "rms_proj"): ((1, "w_up"),),
    (1, "rms_swiglu"): ((1, "w_down"),),
}
BWD_RIDES = {
    "grad_w_gate": (("w_down",), ()),
    "grad_w_up": (("w_gate",), ("w_down",)),
    "ffn_in_bwd": (("w_up",), ("w_gate",)),
    "dil_attn_bwd_1": (("w_out",), ()),
    "dil_attn_bwd_16": ((), ("w_up",)),
    "conv_bwd": ((), ("w_out",)),
    "in_proj_bwd": (("w_in",), ()),
}
IN_PLACE = ("w_gate", "w_up")


class _FwdRide:
    def __init__(self, layer, W, full):
        self.layer, self.W, self.full = layer, W, full

    def rider(self, name):
        keys = FWD_RIDES.get((self.layer, name))
        if not keys:
            return None
        comm = GatherComm([_shard(self.W, l, n) for l, n in keys], [n in IN_PLACE for _, n in keys])
        comm.keys = keys
        return comm

    def landed(self, comm):
        if comm is not None:
            for (l, n), g in zip(comm.keys, comm.results):
                self.full[(l, n)] = g if n in IN_PLACE else _from_gathered(g, n)


class _GradFlow:
    def __init__(self):
        self.pieces, self.sums, self.landed, self.pending = {}, {}, {}, []

    def swap(self, keys):
        comm = SwapComm([self.pieces[k] for k in keys])
        comm.keys, comm.kind = list(keys), "swap"
        return comm

    def exchange(self, keys):
        if not keys:
            return None
        comm = ExchangeComm([self.sums[k] for k in keys])
        comm.keys, comm.kind = list(keys), "exchange"
        return comm

    def take_pending(self):
        keys, self.pending = self.pending, []
        return keys

    def land(self, comm):
        if comm is None:
            return
        if isinstance(comm, MultiComm):
            for sub in comm.comms:
                self.land(sub)
            return
        for k, res in zip(comm.keys, comm.results):
            if comm.kind == "swap":
                r = res.shape[1]
                self.sums[k] = add_halves(self.pieces[k], res, tr=_tile(r, (256, 176, 128, 64, 32, 16)),
                                          name="grad_chip_sum")
                if k[1] == "w_in":
                    self.pending.append(k)
            else:
                self.landed[k] = res


class _BwdRide:
    def __init__(self, layer, flow):
        self.layer, self.flow = layer, flow
        self.final = layer == 0

    def grad(self, name, val):
        self.flow.pieces[(self.layer, name)] = _to_pieces(val, name)
        if self.final and name == "w_in":
            swap = self.flow.swap([(self.layer, name)])
            _run_comm(swap, name="swap_last")
            self.flow.land(swap)

    def rider(self, name):
        if name == "ffn_down_bwd" or (self.final and name == "in_proj_bwd"):
            return self.flow.exchange(self.flow.take_pending())
        swaps, exchanges = BWD_RIDES.get(name, ((), ()))
        comms = []
        if swaps:
            comms.append(self.flow.swap([(self.layer, n) for n in swaps]))
        if exchanges:
            comms.append(self.flow.exchange([(self.layer, n) for n in exchanges]))
        return MultiComm(comms) if comms else None

    def landed(self, comm):
        self.flow.land(comm)


def _pack(items, rows):
    flat = jnp.concatenate([a.reshape(-1).astype(f32) for a in items])
    return jnp.pad(flat, (0, rows * 128 - flat.shape[0])).reshape(rows, 128)


def _unpack(packed, shapes):
    flat = packed.reshape(-1)
    out, off = [], 0
    for shp in shapes:
        n = 1
        for d in shp:
            n *= d
        out.append(flat[off:off + n].reshape(shp))
        off += n
    return out


def kernel(x, norm1_g, w_in, a_q_g, a_k_g, conv_w, conv_b, conv_ln_g, conv_ln_b, c_q_g, c_k_g, c_sinks, w_out, norm2_g, w_gate, w_up, w_down, loss_target, m_norm1_g, m_w_in, m_a_q_g, m_a_k_g, m_conv_w, m_conv_b, m_conv_ln_g, m_conv_ln_b, m_c_q_g, m_c_k_g, m_c_sinks, m_w_out, m_norm2_g, m_w_gate, m_w_up, m_w_down, v_norm1_g, v_w_in, v_a_q_g, v_a_k_g, v_conv_w, v_conv_b, v_conv_ln_g, v_conv_ln_b, v_c_q_g, v_c_k_g, v_c_sinks, v_w_out, v_norm2_g, v_w_gate, v_w_up, v_w_down):
    W = dict(norm1_g=norm1_g, w_in=w_in, a_q_g=a_q_g, a_k_g=a_k_g, conv_w=conv_w, conv_b=conv_b, conv_ln_g=conv_ln_g,
             conv_ln_b=conv_ln_b, c_q_g=c_q_g, c_k_g=c_k_g, c_sinks=c_sinks, w_out=w_out, norm2_g=norm2_g, w_gate=w_gate,
             w_up=w_up, w_down=w_down)
    M = dict(norm1_g=m_norm1_g, w_in=m_w_in, a_q_g=m_a_q_g, a_k_g=m_a_k_g, conv_w=m_conv_w, conv_b=m_conv_b,
             conv_ln_g=m_conv_ln_g, conv_ln_b=m_conv_ln_b, c_q_g=m_c_q_g, c_k_g=m_c_k_g, c_sinks=m_c_sinks, w_out=m_w_out,
             norm2_g=m_norm2_g, w_gate=m_w_gate, w_up=m_w_up, w_down=m_w_down)
    V = dict(norm1_g=v_norm1_g, w_in=v_w_in, a_q_g=v_a_q_g, a_k_g=v_a_k_g, conv_w=v_conv_w, conv_b=v_conv_b,
             conv_ln_g=v_conv_ln_g, conv_ln_b=v_conv_ln_b, c_q_g=v_c_q_g, c_k_g=v_c_k_g, c_sinks=v_c_sinks, w_out=v_w_out,
             norm2_g=v_norm2_g, w_gate=v_w_gate, w_up=v_w_up, w_down=v_w_down)
    depth = norm1_g.shape[0]
    T, D = x.shape[1], x.shape[2]
    xs = x.reshape(T, D)
    chip = 2 * lax.axis_index("x") + lax.axis_index("y")
    e = _head_eye()

    full = {}
    first = GatherComm([_shard(W, 0, "w_in"), conv_w])
    _run_comm(first, name="gather_first")
    full[(0, "w_in")] = _from_gathered(first.results[0], "w_in")
    conv_full = first.results[1].transpose(1, 2, 0, 3).reshape(depth, CONV_K, B_W)
    params = []
    for l in range(depth):
        small = {n: W[n][l].reshape(1, -1) for n in SMALL if n != "conv_w"}
        small["conv_w"] = jnp.pad(conv_full[l], ((0, HALO - CONV_K), (0, 0)))
        params.append(_LayerParams(l, full, small))

    saved = []
    act = xs
    for l in range(depth):
        act, s = _layer_fwd(act, params[l], e, _FwdRide(l, W, full))
        saved.append(s)
    dy, dyb, loss_part = loss_head(act, loss_target.reshape(T, D), tm=_tile(T, (512, 256, 128)), name="loss_head")
    grads = [None] * depth
    flow = _GradFlow()
    for l in reversed(range(depth)):
        dy, dyb, grads[l] = _layer_bwd(dy, dyb, params[l], saved[l], e, _BwdRide(l, flow))
    grad_x = dy.reshape(x.shape)

    out = {}
    for n in ("w_down", "w_gate", "w_up", "w_out", "w_in"):
        last = flow.exchange(flow.take_pending())
        per_layer = [flow.landed[(l, n)] for l in range(depth)]
        r, cc = per_layer[0].shape[2], per_layer[0].shape[3]
        srcs = [jnp.stack([pl_[:, s].reshape(2 * r, cc) for pl_ in per_layer]) for s in range(4)]
        out[n] = adamw(W[n], M[n], V[n], srcs, tr=_tile(2 * r, (256, 176, 128, 64, 32, 16)), name="adamw_" + n,
                       comm=last)
        flow.land(last)

    small_shapes = []
    items = []
    for l in range(depth):
        for n in SMALL:
            a = grads[l][n]
            if n == "conv_w":
                a = a.reshape(CONV_K, 4, B_W // 4).transpose(1, 0, 2)
            items.append(a)
            small_shapes.append(a.shape)
    items.append(loss_part[0, 0:1])
    small_shapes.append((1,))
    total = sum(int(jnp.size(a)) for a in items)
    rows = -(-total // 1024) * 8
    summed = sum8(gather_small(_pack(items, rows), name="gather_small"), name="sum_small")
    parts = _unpack(summed, small_shapes)
    loss = parts[-1][0]
    small_g = {n: [] for n in SMALL}
    for l in range(depth):
        for i, n in enumerate(SMALL):
            a = parts[l * len(SMALL) + i]
            if n == "conv_w":
                a = lax.dynamic_index_in_dim(a, chip, axis=0, keepdims=False)
            small_g[n].append(a.reshape(W[n].shape[1:]))
    sw = [W[n] for n in SMALL]
    sm = [M[n] for n in SMALL]
    sv = [V[n] for n in SMALL]
    sg = [jnp.stack(small_g[n]) for n in SMALL]
    tot2 = sum(int(jnp.size(a)) for a in sw)
    rows2 = -(-tot2 // 1024) * 8
    res = adamw(_pack(sw, rows2)[None], _pack(sm, rows2)[None], _pack(sv, rows2)[None], [_pack(sg, rows2)[None]],
                tr=rows2, name="adamw_small")
    shapes2 = [a.shape for a in sw]
    small_out = [_unpack(r[0], shapes2) for r in res]
    for i, n in enumerate(SMALL):
        out[n] = [small_out[k][i] for k in range(4)]

    order = ("norm1_g", "w_in", "a_q_g", "a_k_g", "conv_w", "conv_b", "conv_ln_g", "conv_ln_b", "c_q_g", "c_k_g",
             "c_sinks", "w_out", "norm2_g", "w_gate", "w_up", "w_down")
    return (loss, grad_x, *[out[n][0] for n in order], *[out[n][1] for n in order], *[out[n][2] for n in order],
            *[out[n][3] for n in order])
```
